```python
import math
import jax, jax.numpy as jnp
from jax import lax
import numpy as np

D_MODEL = 1024
BATCH = 16
SEQ = 256
DEPTH = 4
DEC_BATCH = 2
DEC_SEQ = 2048
PAST_LEN = 256

GRID_W = 64
N_MIXERS = 4
D_FF = 2816
CONV_W = 3
EPS = 1e-6
ROPE_BASE = 10000.0
Q_BLOCK = 128
NEG = -1e30

NA_HEADS = 16
NA_HD = 64
NA_WIN_R = 8
NA_WIN_C = 16
NA_QBLK = NA_WIN_C
NA_KCOLS = 2 * NA_WIN_C

GLA_HEADS = 4
GLA_DK = D_MODEL // 2 // GLA_HEADS
GLA_DV = D_MODEL // GLA_HEADS
GLA_HK = GLA_HEADS * GLA_DK
GLA_HV = GLA_HEADS * GLA_DV
GLA_GATE_RANK = 16
GLA_GATE_NORM = 16.0
GLA_CHUNK = 64

DIFF_HEADS = 8
DIFF_HD = D_MODEL // DIFF_HEADS // 2
DIFF_LAYER = 2
DIFF_LAMBDA_INIT = 0.8 - 0.6 * math.exp(-0.3 * DIFF_LAYER)

MLA_HEADS = 16
MLA_Q_RANK = 384
MLA_KV_RANK = 256
MLA_NOPE = 64
MLA_ROPE = 32
MLA_V = 64
MLA_QK = MLA_NOPE + MLA_ROPE

kernel_name = 'hybrid_diffusion_trunk_step'


def rmsnorm(x, g):
    xf = x.astype(jnp.float32)
    y = xf * lax.rsqrt(jnp.mean(xf * xf, axis=-1, keepdims=True) + EPS)
    return (y * g.astype(jnp.float32)).astype(x.dtype)


def ada_mod(cvec, w, b):
    m = (jax.nn.silu(cvec) @ w + b)[..., None, :]
    return jnp.split(m, 6, axis=-1)


def modulate(h, shift, scale):
    return h * (1 + scale) + shift


def axial_rope_tables(n_tok, rdim):
    nf = rdim // 4
    freqs = ROPE_BASE ** (-jnp.arange(nf, dtype=jnp.float32) / nf)
    t = jnp.arange(n_tok)
    row = (t // GRID_W).astype(jnp.float32)
    col = (t % GRID_W).astype(jnp.float32)
    ang = jnp.stack([row[:, None] * freqs, col[:, None] * freqs], axis=1)
    return jnp.cos(ang), jnp.sin(ang)


def apply_rope(x, cos, sin):
    sh = x.shape
    xr = x.reshape(sh[:-1] + (2, 2, sh[-1] // 4)).astype(jnp.float32)
    x1, x2 = xr[..., 0, :], xr[..., 1, :]
    out = jnp.stack([x1 * cos - x2 * sin, x1 * sin + x2 * cos], axis=-2)
    return out.reshape(sh).astype(x.dtype)


def map_query_blocks(fn, q):
    b, h, n = q.shape[:3]
    qb = jnp.moveaxis(q.reshape((b, h, n // Q_BLOCK, Q_BLOCK) + q.shape[3:]), 2, 0)
    out = jnp.moveaxis(lax.map(fn, qb), 0, 2)
    return out.reshape(out.shape[:2] + (n,) + out.shape[4:])


def dense_attention(q, k, v, scale):
    def blk(qb):
        s = jnp.einsum('bhqd,bhkd->bhqk', qb, k).astype(jnp.float32) * scale
        p = jax.nn.softmax(s, axis=-1).astype(v.dtype)
        return jnp.einsum('bhqk,bhkd->bhqd', p, v)
    return map_query_blocks(blk, q)


def merge_heads(o, w_o):
    b, h, n, d = o.shape
    return jnp.transpose(o, (0, 2, 1, 3)).reshape(b, n, h * d) @ w_o


def conv_ffn(h, w_up, conv_w, conv_b, w_down):
    u = h @ w_up
    up = jnp.pad(u, ((0, 0), (1, 1), (0, 0)))
    u = up[:, :-2] * conv_w[0] + up[:, 1:-1] * conv_w[1] + up[:, 2:] * conv_w[2] + conv_b
    gate, val = jnp.split(u, 2, axis=-1)
    return (jax.nn.silu(gate) * val) @ w_down


def na_qkv(h, w_qkv, gq, gk):
    b, n, _ = h.shape
    qkv = jnp.transpose((h @ w_qkv).reshape(b, n, 3, NA_HEADS, NA_HD), (2, 0, 3, 1, 4))
    return rmsnorm(qkv[0], gq), rmsnorm(qkv[1], gk), qkv[2]


def na_latent(q, k, v, kc, vc, bias_table):
    b, h, n, d = q.shape
    rows = n // GRID_W
    wr = min(NA_WIN_R, rows)
    nb = GRID_W // NA_QBLK
    r = jnp.arange(rows)
    key_rows = jnp.clip(r - wr // 2, 0, rows - wr)[:, None] + jnp.arange(wr)
    c0 = jnp.arange(nb) * NA_QBLK
    key_cols = jnp.clip(c0 - NA_WIN_C // 2, 0, GRID_W - NA_KCOLS)[:, None] + jnp.arange(NA_KCOLS)
    q_cols = c0[:, None] + jnp.arange(NA_QBLK)
    win0 = jnp.clip(q_cols - NA_WIN_C // 2, 0, GRID_W - NA_WIN_C)[..., None]
    kcols_b = key_cols[:, None, :]
    valid = (kcols_b >= win0) & (kcols_b < win0 + NA_WIN_C)
    roff = key_rows - r[:, None] + NA_WIN_R - 1
    coff = jnp.clip(kcols_b - q_cols[..., None] + NA_WIN_C - 1, 0, 2 * NA_WIN_C - 2)
    bias = bias_table[:, roff[:, None, None, :, None], coff[None, :, :, None, :]].astype(jnp.float32)
    kr = key_rows[:, None, :, None]
    kcl = key_cols[None, :, None, :]
    kg = k.reshape(b, h, rows, GRID_W, d)[:, :, kr, kcl]
    vg = v.reshape(b, h, rows, GRID_W, d)[:, :, kr, kcl]
    qg = q.reshape(b, h, rows, nb, NA_QBLK, d)
    scale = d ** -0.5
    n_loc = wr * NA_KCOLS
    s_loc = jnp.einsum('bhrnqd,bhrnikd->bhrnqik', qg, kg).astype(jnp.float32) * scale + bias
    s_loc = jnp.where(valid[:, :, None, :], s_loc, NEG).reshape(b, h, rows, nb, NA_QBLK, n_loc)
    s_ctx = jnp.einsum('bhrnqd,bhld->bhrnql', qg, kc).astype(jnp.float32) * scale
    p = jax.nn.softmax(jnp.concatenate([s_loc, s_ctx], axis=-1), axis=-1).astype(v.dtype)
    o = (jnp.einsum('bhrnqk,bhrnkd->bhrnqd', p[..., :n_loc], vg.reshape(b, h, rows, nb, n_loc, d))
         + jnp.einsum('bhrnql,bhld->bhrnqd', p[..., n_loc:], vc))
    return o.reshape(b, h, n, d)


def gla_project(h, w_qkvg, w_g1, w_g2, b_g):
    b, n, _ = h.shape
    q, k, v, g = jnp.split(h @ w_qkvg, [GLA_HK, 2 * GLA_HK, 2 * GLA_HK + GLA_HV], axis=-1)
    heads = lambda x, dh: jnp.transpose(x.reshape(b, n, GLA_HEADS, dh), (0, 2, 1, 3))
    q = heads(q, GLA_DK) * (GLA_DK ** -0.5)
    k = heads(k, GLA_DK)
    v = heads(v, GLA_DV)
    lg = jnp.einsum('zbnr,zre->zbne', jnp.einsum('bnd,zdr->zbnr', h, w_g1), w_g2) + b_g[:, None, None, :]
    lg = jax.nn.log_sigmoid(lg.astype(jnp.float32)) / GLA_GATE_NORM
    lg = jnp.transpose(lg.reshape(2, b, n, GLA_HEADS, GLA_DK), (0, 1, 3, 2, 4))
    return q, k, v, g, lg


def gla_scan(q, k, v, lg, s0):
    b, h, n, _ = q.shape
    nc = n // GLA_CHUNK
    chunks = lambda x: jnp.moveaxis(x.reshape(b, h, nc, GLA_CHUNK, x.shape[-1]), 2, 0)
    lower = jnp.tril(jnp.ones((GLA_CHUNK, GLA_CHUNK), dtype=bool))[..., None]

    def step(s, inp):
        qc, kc, vc, gc = inp
        qf, kf, vf = qc.astype(jnp.float32), kc.astype(jnp.float32), vc.astype(jnp.float32)
        bcum = jnp.cumsum(gc, axis=2)
        inter = jnp.einsum('bhtk,bhkv->bhtv', qf * jnp.exp(bcum), s)
        dlt = bcum[:, :, :, None, :] - bcum[:, :, None, :, :]
        decay = jnp.where(lower, jnp.exp(jnp.minimum(dlt, 0.0)), 0.0)
        a = jnp.einsum('bhtk,bhsk,bhtsk->bhts', qf, kf, decay)
        intra = jnp.einsum('bhts,bhsv->bhtv', a, vf)
        btot = bcum[:, :, -1:, :]
        s_new = jnp.exp(btot[:, :, 0, :, None]) * s + jnp.einsum('bhsk,bhsv->bhkv', kf * jnp.exp(btot - bcum), vf)
        return s_new, inter + intra

    s_fin, o = lax.scan(step, s0.astype(jnp.float32), (chunks(q), chunks(k), chunks(v), chunks(lg)))
    o = jnp.moveaxis(o, 0, 2).reshape(b, h, n, v.shape[-1]).astype(v.dtype)
    return o, s_fin


def gla_bidir(q, k, v, lg, s_fwd, s_bwd):
    o_f, sf = gla_scan(q, k, v, lg[0], s_fwd)
    flip = lambda x: jnp.flip(x, axis=2)
    o_b, sb = gla_scan(flip(q), flip(k), flip(v), flip(lg[1]), s_bwd)
    return o_f + flip(o_b), sf, sb


def gla_out(o, g, g_norm, w_o):
    b, h, n, d = o.shape
    o = jnp.transpose(rmsnorm(o, g_norm), (0, 2, 1, 3)).reshape(b, n, h * d)
    return (o * jax.nn.silu(g)) @ w_o


def diff_qkv(h, w_qkv, gq, gk):
    b, n, _ = h.shape
    q, k, v = jnp.split(h @ w_qkv, 3, axis=-1)
    two = lambda x: jnp.transpose(x.reshape(b, n, 2, DIFF_HEADS, DIFF_HD), (0, 2, 3, 1, 4)).reshape(b, 2 * DIFF_HEADS, n, DIFF_HD)
    v = jnp.transpose(v.reshape(b, n, DIFF_HEADS, 2 * DIFF_HD), (0, 2, 1, 3))
    return rmsnorm(two(q), gq), rmsnorm(two(k), gk), v


def diff_attention(q, k, v, lam):
    scale = DIFF_HD ** -0.5

    def blk(qb):
        s = jnp.einsum('bhqd,bhkd->bhqk', qb, k).astype(jnp.float32) * scale
        p = jax.nn.softmax(s, axis=-1)
        p = p.reshape(p.shape[0], 2, DIFF_HEADS, p.shape[2], p.shape[3])
        a = (p[:, 0] - lam * p[:, 1]).astype(v.dtype)
        return jnp.einsum('bhqk,bhkd->bhqd', a, v)
    return map_query_blocks(blk, q)


def mla_queries(h, w_dq, g_qa, w_uq, g_q):
    b, n, _ = h.shape
    q = (rmsnorm(h @ w_dq, g_qa) @ w_uq).reshape(b, n, MLA_HEADS, MLA_QK)
    return rmsnorm(jnp.transpose(q, (0, 2, 1, 3)), g_q)


def mla_compress(h, w_dkv, g_kva):
    kv = h @ w_dkv
    return rmsnorm(kv[..., :MLA_KV_RANK], g_kva), kv[..., MLA_KV_RANK:]


def mla_expand(ckv, krope, w_ukv, g_k):
    b, n, _ = ckv.shape
    kv = jnp.transpose((ckv @ w_ukv).reshape(b, n, MLA_HEADS, MLA_NOPE + MLA_V), (0, 2, 1, 3))
    k = jnp.concatenate([kv[..., :MLA_NOPE], jnp.broadcast_to(krope[:, None], (b, MLA_HEADS, n, MLA_ROPE))], axis=-1)
    return rmsnorm(k, g_k), kv[..., MLA_NOPE:]


def rope_tail(x, cos, sin):
    return jnp.concatenate([x[..., :MLA_NOPE], apply_rope(x[..., MLA_NOPE:], cos, sin)], axis=-1)


def setup_inputs(seed: int = 0) -> dict:
    key = jax.random.key(seed)
    ks = iter(jax.random.split(key, 64))
    f32 = jnp.float32
    nrm = lambda shape, s=1.0: jax.random.normal(next(ks), shape, f32) * s
    wt = lambda shape, fan_in: nrm(shape, fan_in ** -0.5)
    gain = lambda shape: 1.0 + nrm(shape, 0.05)
    F2 = 2 * D_FF
    return {
        'x_prompt': nrm((BATCH, SEQ, D_MODEL)),
        'x_sample': nrm((DEC_BATCH, DEC_SEQ, D_MODEL)),
        'cache_l0_k': nrm((DEC_BATCH, NA_HEADS, PAST_LEN, NA_HD)),
        'cache_l0_v': nrm((DEC_BATCH, NA_HEADS, PAST_LEN, NA_HD)),
        'state_l1_fwd': nrm((DEC_BATCH, GLA_HEADS, GLA_DK, GLA_DV)),
        'state_l1_bwd': nrm((DEC_BATCH, GLA_HEADS, GLA_DK, GLA_DV)),
        'cache_l2_k': nrm((DEC_BATCH, 2 * DIFF_HEADS, PAST_LEN, DIFF_HD)),
        'cache_l2_v': nrm((DEC_BATCH, DIFF_HEADS, PAST_LEN, 2 * DIFF_HD)),
        'cache_l3_ckv': nrm((DEC_BATCH, PAST_LEN, MLA_KV_RANK)),
        'cache_l3_krope': nrm((DEC_BATCH, PAST_LEN, MLA_ROPE)),
        'c': nrm((DEC_BATCH, D_MODEL)),
        'c_ctx': nrm((D_MODEL,)),
        'ada_w': wt((DEPTH, D_MODEL, 6 * D_MODEL), D_MODEL) * 0.5,
        'ada_b': nrm((DEPTH, 6 * D_MODEL), 0.02),
        'norm_mix': gain((DEPTH, D_MODEL)),
        'norm_ffn': gain((DEPTH, D_MODEL)),
        'ffn_w_up': wt((DEPTH, D_MODEL, F2), D_MODEL),
        'ffn_conv_w': wt((DEPTH, CONV_W, F2), CONV_W),
        'ffn_conv_b': nrm((DEPTH, F2), 0.02),
        'ffn_w_down': wt((DEPTH, D_FF, D_MODEL), D_FF),
        'na_w_qkv': wt((D_MODEL, 3 * NA_HEADS * NA_HD), D_MODEL),
        'na_q_norm': gain((NA_HD,)),
        'na_k_norm': gain((NA_HD,)),
        'na_bias': nrm((NA_HEADS, 2 * NA_WIN_R - 1, 2 * NA_WIN_C - 1), 0.2),
        'na_w_o': wt((NA_HEADS * NA_HD, D_MODEL), NA_HEADS * NA_HD),
        'gla_w_qkvg': wt((D_MODEL, 2 * GLA_HK + 2 * GLA_HV), D_MODEL),
        'gla_w_gate1': wt((2, D_MODEL, GLA_GATE_RANK), D_MODEL),
        'gla_w_gate2': wt((2, GLA_GATE_RANK, GLA_HK), GLA_GATE_RANK),
        'gla_b_gate': nrm((2, GLA_HK), 0.1),
        'gla_o_norm': gain((GLA_DV,)),
        'gla_w_o': wt((GLA_HV, D_MODEL), GLA_HV),
        'diff_w_qkv': wt((D_MODEL, 3 * D_MODEL), D_MODEL),
        'diff_q_norm': gain((DIFF_HD,)),
        'diff_k_norm': gain((DIFF_HD,)),
        'diff_lambda': nrm((4, DIFF_HD), 0.1),
        'diff_sub_norm': gain((2 * DIFF_HD,)),
        'diff_w_o': wt((D_MODEL, D_MODEL), D_MODEL),
        'mla_w_dq': wt((D_MODEL, MLA_Q_RANK), D_MODEL),
        'mla_q_a_norm': gain((MLA_Q_RANK,)),
        'mla_w_uq': wt((MLA_Q_RANK, MLA_HEADS * MLA_QK), MLA_Q_RANK),
        'mla_w_dkv': wt((D_MODEL, MLA_KV_RANK + MLA_ROPE), D_MODEL),
        'mla_kv_a_norm': gain((MLA_KV_RANK,)),
        'mla_w_ukv': wt((MLA_KV_RANK, MLA_HEADS * (MLA_NOPE + MLA_V)), MLA_KV_RANK),
        'mla_q_norm': gain((MLA_QK,)),
        'mla_k_norm': gain((MLA_QK,)),
        'mla_w_o': wt((MLA_HEADS * MLA_V, D_MODEL), MLA_HEADS * MLA_V),
    }


def reference(x_prompt, x_sample, cache_l0_k, cache_l0_v, state_l1_fwd, state_l1_bwd,
              cache_l2_k, cache_l2_v, cache_l3_ckv, cache_l3_krope, c, c_ctx,
              ada_w, ada_b, norm_mix, norm_ffn, ffn_w_up, ffn_conv_w, ffn_conv_b, ffn_w_down,
              na_w_qkv, na_q_norm, na_k_norm, na_bias, na_w_o,
              gla_w_qkvg, gla_w_gate1, gla_w_gate2, gla_b_gate, gla_o_norm, gla_w_o,
              diff_w_qkv, diff_q_norm, diff_k_norm, diff_lambda, diff_sub_norm, diff_w_o,
              mla_w_dq, mla_q_a_norm, mla_w_uq, mla_w_dkv, mla_kv_a_norm, mla_w_ukv,
              mla_q_norm, mla_k_norm, mla_w_o):
    xp, xs = x_prompt, x_sample
    n_lat = xs.shape[1]
    cos_d, sin_d = axial_rope_tables(n_lat, DIFF_HD)
    cos_m, sin_m = axial_rope_tables(n_lat, MLA_ROPE)
    lam = (jnp.exp(jnp.sum(diff_lambda[0] * diff_lambda[1]).astype(jnp.float32))
           - jnp.exp(jnp.sum(diff_lambda[2] * diff_lambda[3]).astype(jnp.float32)) + DIFF_LAMBDA_INIT)
    for i in range(DEPTH):
        kind = i % N_MIXERS
        mp = ada_mod(c_ctx, ada_w[i], ada_b[i])
        ms = ada_mod(c, ada_w[i], ada_b[i])
        hp = modulate(rmsnorm(xp, norm_mix[i]), mp[0], mp[1])
        hs = modulate(rmsnorm(xs, norm_mix[i]), ms[0], ms[1])
        if kind == 0:
            qp, kp, vp = na_qkv(hp, na_w_qkv, na_q_norm, na_k_norm)
            op = merge_heads(dense_attention(qp, kp, vp, NA_HD ** -0.5), na_w_o)
            new_l0_k, new_l0_v = kp, vp
            qs, ks_, vs = na_qkv(hs, na_w_qkv, na_q_norm, na_k_norm)
            os_ = merge_heads(na_latent(qs, ks_, vs, cache_l0_k, cache_l0_v, na_bias), na_w_o)
        elif kind == 1:
            qp, kp, vp, gp, lgp = gla_project(hp, gla_w_qkvg, gla_w_gate1, gla_w_gate2, gla_b_gate)
            s0 = jnp.zeros((hp.shape[0], GLA_HEADS, GLA_DK, GLA_DV), jnp.float32)
            o, new_l1_fwd, new_l1_bwd = gla_bidir(qp, kp, vp, lgp, s0, s0)
            op = gla_out(o, gp, gla_o_norm, gla_w_o)
            qs, ks_, vs, gs, lgs = gla_project(hs, gla_w_qkvg, gla_w_gate1, gla_w_gate2, gla_b_gate)
            o, _, _ = gla_bidir(qs, ks_, vs, lgs, state_l1_fwd, state_l1_bwd)
            os_ = gla_out(o, gs, gla_o_norm, gla_w_o)
        elif kind == 2:
            qp, kp, vp = diff_qkv(hp, diff_w_qkv, diff_q_norm, diff_k_norm)
            o = diff_attention(qp, kp, vp, lam)
            op = merge_heads(rmsnorm(o, diff_sub_norm) * (1 - DIFF_LAMBDA_INIT), diff_w_o)
            new_l2_k, new_l2_v = kp, vp
            qs, ks_, vs = diff_qkv(hs, diff_w_qkv, diff_q_norm, diff_k_norm)
            qs = apply_rope(qs, cos_d, sin_d)
            ks_ = apply_rope(ks_, cos_d, sin_d)
            o = diff_attention(qs, jnp.concatenate([cache_l2_k, ks_], axis=2),
                               jnp.concatenate([cache_l2_v, vs], axis=2), lam)
            os_ = merge_heads(rmsnorm(o, diff_sub_norm) * (1 - DIFF_LAMBDA_INIT), diff_w_o)
        else:
            qp = mla_queries(hp, mla_w_dq, mla_q_a_norm, mla_w_uq, mla_q_norm)
            cp, krp = mla_compress(hp, mla_w_dkv, mla_kv_a_norm)
            kp, vp = mla_expand(cp, krp, mla_w_ukv, mla_k_norm)
            op = merge_heads(dense_attention(qp, kp, vp, MLA_QK ** -0.5), mla_w_o)
            new_l3_ckv, new_l3_krope = cp, krp
            qs = rope_tail(mla_queries(hs, mla_w_dq, mla_q_a_norm, mla_w_uq, mla_q_norm), cos_m, sin_m)
            cs, krs = mla_compress(hs, mla_w_dkv, mla_kv_a_norm)
            ks_, vs = mla_expand(cs, krs, mla_w_ukv, mla_k_norm)
            ks_ = rope_tail(ks_, cos_m, sin_m)
            kc, vc = mla_expand(cache_l3_ckv, cache_l3_krope, mla_w_ukv, mla_k_norm)
            os_ = merge_heads(dense_attention(qs, jnp.concatenate([kc, ks_], axis=2),
                                              jnp.concatenate([vc, vs], axis=2), MLA_QK ** -0.5), mla_w_o)
        xp = xp + mp[2] * op
        xs = xs + ms[2] * os_
        hp = modulate(rmsnorm(xp, norm_ffn[i]), mp[3], mp[4])
        hs = modulate(rmsnorm(xs, norm_ffn[i]), ms[3], ms[4])
        xp = xp + mp[5] * conv_ffn(hp, ffn_w_up[i], ffn_conv_w[i], ffn_conv_b[i], ffn_w_down[i])
        xs = xs + ms[5] * conv_ffn(hs, ffn_w_up[i], ffn_conv_w[i], ffn_conv_b[i], ffn_w_down[i])
    return (xp, xs, new_l0_k, new_l0_v, new_l1_fwd, new_l1_bwd, new_l2_k, new_l2_v, new_l3_ckv, new_l3_krope)
```

```python
import functools
import math

import jax
import jax.numpy as jnp
from jax import lax
from jax.experimental import pallas as pl
from jax.experimental.pallas import tpu as pltpu

F32 = jnp.float32
BF16 = jnp.bfloat16

D_MODEL = 1024
BATCH = 16
SEQ = 256
DEPTH = 4
DEC_BATCH = 2
DEC_SEQ = 2048
PAST_LEN = 256
GRID_W = 64
D_FF = 2816
EPS = 1e-6
ROPE_BASE = 10000.0

NA_HEADS = 16
NA_HD = 64
NA_WIN_R = 8
NA_WIN_C = 16

GLA_HEADS = 4
GLA_DK = 128
GLA_DV = 256
GLA_HK = GLA_HEADS * GLA_DK
GLA_HV = GLA_HEADS * GLA_DV
GLA_GATE_RANK = 16
GLA_GATE_NORM = 16.0

DIFF_HEADS = 8
DIFF_HD = 64
DIFF_LAMBDA_INIT = 0.8 - 0.6 * math.exp(-0.3 * 2)

MLA_HEADS = 16
MLA_Q_RANK = 384
MLA_KV_RANK = 256
MLA_NOPE = 64
MLA_ROPE = 32
MLA_V = 64
MLA_QK = MLA_NOPE + MLA_ROPE

N_PROMPT = BATCH * SEQ
N_LATENT = DEC_BATCH * DEC_SEQ
N_TOK = N_PROMPT + N_LATENT
TOK_TILE = 2048
N_TOK_TILES = N_TOK // TOK_TILE
FF_CHUNK = 256
N_FF_CHUNKS = D_FF // FF_CHUNK
NEG = -1e30

VMEM_LIMIT = 56 * 1024 * 1024

_NT = (((1,), (1,)), ((), ()))
_TN = (((0,), (0,)), ((), ()))


def _params(sem, vmem=VMEM_LIMIT):
    return pltpu.CompilerParams(dimension_semantics=sem, vmem_limit_bytes=vmem)


def _log2(n):
    assert n & (n - 1) == 0
    return n.bit_length() - 1


def _silu(x):
    return x / (1.0 + jnp.exp(-x))


def _bdot(a, b):
    return jnp.dot(a, b, preferred_element_type=F32)


def _softmax_parts(parts):
    m = parts[0].max(axis=-1, keepdims=True)
    for s in parts[1:]:
        m = jnp.maximum(m, s.max(axis=-1, keepdims=True))
    ps = [jnp.exp(s - m) for s in parts]
    l = ps[0].sum(axis=-1, keepdims=True)
    for p in ps[1:]:
        l = l + p.sum(axis=-1, keepdims=True)
    return ps, 1.0 / l


def _group_rms(x, gain, group, n_real=None):
    lanes = x.shape[-1]
    n_real = n_real or group
    x2 = x * x
    if group == lanes:
        ms = jnp.sum(x2, axis=-1, keepdims=True)
    else:
        gid = lax.broadcasted_iota(jnp.int32, x.shape, 1) >> _log2(group)
        ms = jnp.zeros_like(x)
        for i in range(lanes // group):
            sel = gid == i
            si = jnp.sum(jnp.where(sel, x2, 0.0), axis=-1, keepdims=True)
            ms = jnp.where(sel, si, ms)
    return x * lax.rsqrt(ms * (1.0 / n_real) + EPS) * gain


def _rope(x, cos, sin, half):
    lanes = x.shape[-1]
    lane = lax.broadcasted_iota(jnp.int32, x.shape, 1)
    up = pltpu.roll(x, lanes - half, axis=1)
    dn = pltpu.roll(x, half, axis=1)
    swapped = jnp.where((lane & (2 * half - 1)) < half, up, dn)
    return x * cos + swapped * sin


def _stack_heads(q, n_heads, head_lanes):
    hid = lax.broadcasted_iota(jnp.int32, q.shape, 1) >> _log2(head_lanes)
    zero = jnp.zeros_like(q)
    return jnp.concatenate([jnp.where(hid == i, q, zero) for i in range(n_heads)], axis=0)


def _unstack_heads(o, n_heads, head_lanes):
    rows = o.shape[0] // n_heads
    hid = lax.broadcasted_iota(jnp.int32, (rows, o.shape[1]), 1) >> _log2(head_lanes)
    out = o[0:rows]
    for i in range(1, n_heads):
        out = jnp.where(hid == i, o[i * rows:(i + 1) * rows], out)
    return out


ADA_TN = 1536


def _ada_kernel(c_ref, w_ref, b_ref, o_ref):
    s = _silu(c_ref[...])
    o_ref[0] = jnp.dot(s, w_ref[0], preferred_element_type=F32,
                       precision=lax.Precision.HIGHEST) + b_ref[0]


def _ada_mods(cvecs, ada_w, ada_b):
    out = pl.pallas_call(
        _ada_kernel,
        grid=(DEPTH, 6 * D_MODEL // ADA_TN),
        in_specs=[pl.BlockSpec((8, D_MODEL), lambda l, j: (0, 0)),
                  pl.BlockSpec((1, D_MODEL, ADA_TN), lambda l, j: (l, 0, j)),
                  pl.BlockSpec((1, 1, ADA_TN), lambda l, j: (l, 0, j))],
        out_specs=pl.BlockSpec((1, 8, ADA_TN), lambda l, j: (l, 0, j)),
        out_shape=jax.ShapeDtypeStruct((DEPTH, 8, 6 * D_MODEL), F32),
        compiler_params=_params(("arbitrary", "arbitrary")),
        name="ada_mod",
    )(cvecs, ada_w, ada_b.reshape(DEPTH, 1, 6 * D_MODEL))
    return out.reshape(DEPTH, 8, 6, D_MODEL)[:, :3]


def _mod_group_of_tile(i):
    return jnp.maximum(i - (N_PROMPT // TOK_TILE - 1), 0)


def _norm_mod_rows(x_ref, g_ref, mod_ref, h_ref, shift_idx, scale_idx, rows=64):
    g = g_ref[...]
    sc = 1.0 + mod_ref[0, scale_idx:scale_idx + 1, :]
    sh = mod_ref[0, shift_idx:shift_idx + 1, :]

    def body(r, carry):
        sl = pl.ds(pl.multiple_of(r * rows, rows), rows)
        xf = x_ref[sl, :]
        ms = jnp.mean(xf * xf, axis=-1, keepdims=True)
        y = xf * lax.rsqrt(ms + EPS) * g
        h_ref[sl, :] = (y * sc + sh).astype(BF16)
        return carry

    lax.fori_loop(0, x_ref.shape[0] // rows, body, 0)


def _proj_kernel(x_ref, g_ref, mod_ref, w_ref, o_ref, h_ref, *, shift_idx, scale_idx):
    @pl.when(pl.program_id(1) == 0)
    def _():
        _norm_mod_rows(x_ref, g_ref, mod_ref, h_ref, shift_idx, scale_idx)

    o_ref[...] = _bdot(h_ref[...], w_ref[...].astype(BF16))


def _norm_mod_proj(x, g, mods, w, tn, name):
    n = w.shape[1]
    return pl.pallas_call(
        functools.partial(_proj_kernel, shift_idx=0, scale_idx=1),
        grid=(N_TOK_TILES, n // tn),
        in_specs=[pl.BlockSpec((TOK_TILE, D_MODEL), lambda i, j: (i, 0)),
                  pl.BlockSpec((1, D_MODEL), lambda i, j: (0, 0)),
                  pl.BlockSpec((1, 6, D_MODEL), lambda i, j: (_mod_group_of_tile(i), 0, 0)),
                  pl.BlockSpec((D_MODEL, tn), lambda i, j: (0, j))],
        out_specs=pl.BlockSpec((TOK_TILE, tn), lambda i, j: (i, j)),
        out_shape=jax.ShapeDtypeStruct((N_TOK, n), F32),
        scratch_shapes=[pltpu.VMEM((TOK_TILE, D_MODEL), BF16)],
        compiler_params=_params(("arbitrary", "arbitrary")),
        name=name,
    )(x, g.reshape(1, D_MODEL), mods, w)


def _rms_matmul_kernel(a_ref, g_ref, w_ref, o_ref, n_ref, *, normalise):
    a = a_ref[...]
    if normalise:
        a = a * lax.rsqrt(jnp.mean(a * a, axis=-1, keepdims=True) + EPS) * g_ref[...]
    n_ref[...] = a
    o_ref[...] = _bdot(a.astype(BF16), w_ref[...].astype(BF16))


def _rms_matmul(a, col_block, k, g, w, normalise, name, tm=512):
    rows, n = a.shape[0], w.shape[1]
    return pl.pallas_call(
        functools.partial(_rms_matmul_kernel, normalise=normalise),
        grid=(rows // tm,),
        in_specs=[pl.BlockSpec((tm, k), lambda i: (i, col_block)),
                  pl.BlockSpec((1, k), lambda i: (0, 0)),
                  pl.BlockSpec((k, n), lambda i: (0, 0))],
        out_specs=[pl.BlockSpec((tm, n), lambda i: (i, 0)),
                   pl.BlockSpec((tm, k), lambda i: (i, 0))],
        out_shape=[jax.ShapeDtypeStruct((rows, n), F32),
                   jax.ShapeDtypeStruct((rows, k), F32)],
        compiler_params=_params(("arbitrary",)),
        name=name,
    )(a, g.reshape(1, k), w)


OUT_TM = 512


def _oproj_kernel(*refs, gate_idx, gated):
    if gated:
        x_ref, a_ref, g_ref, mod_ref, w_ref, o_ref, wb_ref = refs
    else:
        x_ref, a_ref, mod_ref, w_ref, o_ref, wb_ref = refs

    @pl.when(pl.program_id(0) == 0)
    def _():
        wb_ref[...] = w_ref[...].astype(BF16)

    a = a_ref[...]
    if gated:
        a = a * _silu(g_ref[...])
    y = _bdot(a.astype(BF16), wb_ref[...])
    o_ref[...] = x_ref[...] + mod_ref[0, gate_idx:gate_idx + 1, :] * y


def _oproj_group(t):
    first_latent = N_PROMPT // OUT_TM
    return jnp.where(t < first_latent, 0, 1 + (t - first_latent) // (DEC_SEQ // OUT_TM))


def _out_proj_residual(x, a, mods, w, name, gate=None, gate_col_block=0):
    k = w.shape[0]
    in_specs = [pl.BlockSpec((OUT_TM, D_MODEL), lambda t: (t, 0)),
                pl.BlockSpec((OUT_TM, k), lambda t: (t, 0))]
    args = [x, a]
    if gate is not None:
        in_specs.append(pl.BlockSpec((OUT_TM, k), lambda t: (t, gate_col_block)))
        args.append(gate)
    in_specs += [pl.BlockSpec((1, 6, D_MODEL), lambda t: (_oproj_group(t), 0, 0)),
                 pl.BlockSpec((k, D_MODEL), lambda t: (0, 0))]
    args += [mods, w]
    return pl.pallas_call(
        functools.partial(_oproj_kernel, gate_idx=2, gated=gate is not None),
        grid=(N_TOK // OUT_TM,),
        in_specs=in_specs,
        out_specs=pl.BlockSpec((OUT_TM, D_MODEL), lambda t: (t, 0)),
        out_shape=jax.ShapeDtypeStruct((N_TOK, D_MODEL), F32),
        scratch_shapes=[pltpu.VMEM((k, D_MODEL), BF16)],
        compiler_params=_params(("arbitrary",)),
        name=name,
    )(*args)


FFN_ROWS = 256
FFN_PAD = 8


def _ffn_kernel(x_ref, g_ref, mod_ref, wg_ref, wv_ref, cwg_ref, cwv_ref, cbg_ref, cbv_ref,
                wd_ref, o_ref, h_ref, u_ref, act_ref):
    i = pl.program_id(0)
    c = pl.program_id(1)
    fc2 = 2 * FF_CHUNK

    @pl.when(c == 0)
    def _():
        _norm_mod_rows(x_ref, g_ref, mod_ref, h_ref, 3, 4)
        zeros = jnp.zeros((FFN_PAD, fc2), F32)
        u_ref[0:FFN_PAD, :] = zeros
        u_ref[FFN_PAD + TOK_TILE:, :] = zeros

    w_up = jnp.concatenate([wg_ref[...], wv_ref[...]], axis=1).astype(BF16)
    u_ref[FFN_PAD:FFN_PAD + TOK_TILE, :] = _bdot(h_ref[...], w_up)

    cw = jnp.concatenate([cwg_ref[...], cwv_ref[...]], axis=1)
    cb = jnp.concatenate([cbg_ref[...], cbv_ref[...]], axis=1)
    seq_len = jnp.where(i < N_PROMPT // TOK_TILE, SEQ, DEC_SEQ)

    def conv_rows(r, carry):
        r0 = pl.multiple_of(r * FFN_ROWS, FFN_ROWS)
        win = u_ref[pl.ds(r0, FFN_ROWS + 2 * FFN_PAD), :]
        prev = win[FFN_PAD - 1:FFN_PAD - 1 + FFN_ROWS]
        mid = win[FFN_PAD:FFN_PAD + FFN_ROWS]
        nxt = win[FFN_PAD + 1:FFN_PAD + 1 + FFN_ROWS]
        pos = (r0 + lax.broadcasted_iota(jnp.int32, (FFN_ROWS, 1), 0)) & (seq_len - 1)
        prev = jnp.where(pos == 0, 0.0, prev)
        nxt = jnp.where(pos == seq_len - 1, 0.0, nxt)
        u = prev * cw[0:1] + mid * cw[1:2] + nxt * cw[2:3] + cb
        act = _silu(u[:, :FF_CHUNK]) * u[:, FF_CHUNK:]
        act_ref[pl.ds(r0, FFN_ROWS), :] = act.astype(BF16)
        return carry

    lax.fori_loop(0, TOK_TILE // FFN_ROWS, conv_rows, 0)

    y = _bdot(act_ref[...], wd_ref[...].astype(BF16))

    @pl.when(c == 0)
    def _():
        o_ref[...] = y

    @pl.when(c > 0)
    def _():
        o_ref[...] += y

    @pl.when(c == N_FF_CHUNKS - 1)
    def _():
        o_ref[...] = x_ref[...] + mod_ref[0, 5:6, :] * o_ref[...]


def _ffn(x, g, mods, w_up, conv_w, conv_b, w_down):
    fc = FF_CHUNK
    ncb = N_FF_CHUNKS
    return pl.pallas_call(
        _ffn_kernel,
        grid=(N_TOK_TILES, ncb),
        in_specs=[pl.BlockSpec((TOK_TILE, D_MODEL), lambda i, c: (i, 0)),
                  pl.BlockSpec((1, D_MODEL), lambda i, c: (0, 0)),
                  pl.BlockSpec((1, 6, D_MODEL), lambda i, c: (_mod_group_of_tile(i), 0, 0)),
                  pl.BlockSpec((D_MODEL, fc), lambda i, c: (0, c)),
                  pl.BlockSpec((D_MODEL, fc), lambda i, c: (0, ncb + c)),
                  pl.BlockSpec((3, fc), lambda i, c: (0, c)),
                  pl.BlockSpec((3, fc), lambda i, c: (0, ncb + c)),
                  pl.BlockSpec((1, fc), lambda i, c: (0, c)),
                  pl.BlockSpec((1, fc), lambda i, c: (0, ncb + c)),
                  pl.BlockSpec((fc, D_MODEL), lambda i, c: (c, 0))],
        out_specs=pl.BlockSpec((TOK_TILE, D_MODEL), lambda i, c: (i, 0)),
        out_shape=jax.ShapeDtypeStruct((N_TOK, D_MODEL), F32),
        scratch_shapes=[pltpu.VMEM((TOK_TILE, D_MODEL), BF16),
                        pltpu.VMEM((TOK_TILE + 2 * FFN_PAD, 2 * fc), F32),
                        pltpu.VMEM((TOK_TILE, fc), BF16)],
        compiler_params=_params(("arbitrary", "arbitrary")),
        name="conv_ffn",
    )(x, g.reshape(1, D_MODEL), mods, w_up, w_up, conv_w, conv_w,
      conv_b.reshape(1, -1), conv_b.reshape(1, -1), w_down)


NA_HB = 4
NA_LANES = NA_HB * NA_HD
NA_ROWS = DEC_SEQ // GRID_W
NA_KEYS = NA_WIN_R * GRID_W


def _na_prompt_kernel(q_ref, k_ref, v_ref, gq_ref, gk_ref, o_ref, kn_ref):
    scale = NA_HD ** -0.5
    q = _group_rms(q_ref[...], gq_ref[...], NA_HD) * scale
    k = _group_rms(k_ref[...], gk_ref[...], NA_HD)
    kn_ref[...] = k
    q4 = _stack_heads(q, NA_HB, NA_HD).astype(BF16)
    s = lax.dot_general(q4, k.astype(BF16), _NT, preferred_element_type=F32)
    (p,), inv = _softmax_parts([s])
    o4 = _bdot(p.astype(BF16), v_ref[...].astype(BF16)) * inv
    o_ref[...] = _unstack_heads(o4, NA_HB, NA_HD)


def _na_prompt(qkv, gq, gk):
    nb = NA_HEADS // NA_HB
    blk = lambda off: pl.BlockSpec((SEQ, NA_LANES), lambda b, j: (b, off + j))
    vec = pl.BlockSpec((1, NA_LANES), lambda b, j: (0, 0))
    return pl.pallas_call(
        _na_prompt_kernel,
        grid=(BATCH, nb),
        in_specs=[blk(0), blk(nb), blk(2 * nb), vec, vec],
        out_specs=[blk(0), blk(0)],
        out_shape=[jax.ShapeDtypeStruct((N_PROMPT, D_MODEL), F32)] * 2,
        compiler_params=_params(("arbitrary", "arbitrary")),
        name="na_prompt",
    )(qkv, qkv, qkv, jnp.tile(gq, NA_HB).reshape(1, -1), jnp.tile(gk, NA_HB).reshape(1, -1))


def _na_latent_kernel(q_ref, k_ref, v_ref, kc_ref, vc_ref, bias_ref, gq_ref, gk_ref, o_ref,
                      qn_ref, kn_ref, vb_ref, kc4_ref, vc4_ref):
    scale = NA_HD ** -0.5
    rows = 256

    def prep(r, carry):
        sl = pl.ds(pl.multiple_of(r * rows, rows), rows)
        qn_ref[sl, :] = (_group_rms(q_ref[sl, :], gq_ref[...], NA_HD) * scale).astype(BF16)
        kn_ref[sl, :] = _group_rms(k_ref[sl, :], gk_ref[...], NA_HD).astype(BF16)
        vb_ref[sl, :] = v_ref[sl, :].astype(BF16)
        return carry

    lax.fori_loop(0, DEC_SEQ // rows, prep, 0)
    kc4_ref[...] = kc_ref[0].astype(BF16)
    vc4_ref[...] = vc_ref[0].astype(BF16)

    def row(r, carry):
        kr0 = jnp.clip(r - NA_WIN_R // 2, 0, NA_ROWS - NA_WIN_R)
        pat = kr0 - r + NA_WIN_R - 1
        qs = pl.ds(pl.multiple_of(r * GRID_W, GRID_W), GRID_W)
        ks = pl.ds(pl.multiple_of(kr0 * GRID_W, GRID_W), NA_KEYS)
        q4 = _stack_heads(qn_ref[qs, :], NA_HB, NA_HD)
        s_loc = lax.dot_general(q4, kn_ref[ks, :], _NT, preferred_element_type=F32)
        bias = jnp.concatenate([bias_ref[h, pat] for h in range(NA_HB)], axis=0)
        s_loc = s_loc + bias
        s_ctx = lax.dot_general(q4, kc4_ref[...], _NT, preferred_element_type=F32)
        (p_loc, p_ctx), inv = _softmax_parts([s_loc, s_ctx])
        o4 = _bdot(p_loc.astype(BF16), vb_ref[ks, :]) + _bdot(p_ctx.astype(BF16), vc4_ref[...])
        o_ref[qs, :] = _unstack_heads(o4 * inv, NA_HB, NA_HD)
        return carry

    lax.fori_loop(0, NA_ROWS, row, 0)


def _na_bias_blocks(bias_table):
    qc = jnp.arange(GRID_W)[:, None]
    kc = jnp.arange(GRID_W)[None, :]
    win0 = jnp.clip(qc - NA_WIN_C // 2, 0, GRID_W - NA_WIN_C)
    valid = (kc >= win0) & (kc < win0 + NA_WIN_C)
    coff = jnp.clip(kc - qc + NA_WIN_C - 1, 0, 2 * NA_WIN_C - 2)
    roff = jnp.arange(NA_WIN_R)[:, None] + jnp.arange(NA_WIN_R)[None, :]
    b = bias_table[:, roff[:, :, None, None], coff[None, None, :, :]]
    b = jnp.where(valid[None, None, None], b.astype(F32), NEG)
    b = jnp.transpose(b, (0, 1, 3, 2, 4))
    return b.reshape(NA_HEADS, NA_WIN_R, GRID_W, NA_KEYS)


def _na_latent(qkv, cache_k, cache_v, bias_blocks, gq, gk):
    nb = NA_HEADS // NA_HB
    lat0 = N_PROMPT // DEC_SEQ
    blk = lambda off: pl.BlockSpec((DEC_SEQ, NA_LANES), lambda b, j: (lat0 + b, off + j))
    vec = pl.BlockSpec((1, NA_LANES), lambda b, j: (0, 0))
    cache = pl.BlockSpec((1, PAST_LEN, NA_LANES), lambda b, j: (b, 0, j))
    return pl.pallas_call(
        _na_latent_kernel,
        grid=(DEC_BATCH, nb),
        in_specs=[blk(0), blk(nb), blk(2 * nb), cache, cache,
                  pl.BlockSpec((NA_HB, NA_WIN_R, GRID_W, NA_KEYS), lambda b, j: (j, 0, 0, 0)),
                  vec, vec],
        out_specs=pl.BlockSpec((DEC_SEQ, NA_LANES), lambda b, j: (b, j)),
        out_shape=jax.ShapeDtypeStruct((N_LATENT, D_MODEL), F32),
        scratch_shapes=[pltpu.VMEM((DEC_SEQ, NA_LANES), BF16),
                        pltpu.VMEM((DEC_SEQ, NA_LANES), BF16),
                        pltpu.VMEM((DEC_SEQ, NA_LANES), BF16),
                        pltpu.VMEM((PAST_LEN, NA_LANES), BF16),
                        pltpu.VMEM((PAST_LEN, NA_LANES), BF16)],
        compiler_params=_params(("arbitrary", "arbitrary")),
        name="na_latent",
    )(qkv, qkv, qkv, _tokens_first(cache_k), _tokens_first(cache_v), bias_blocks,
      jnp.tile(gq, NA_HB).reshape(1, -1), jnp.tile(gk, NA_HB).reshape(1, -1))


GLA_C = 128
GLA_SUB = 16
GLA_LEVELS = (64, 32, 16)


def _split_hi_lo(x):
    hi = x.astype(BF16)
    lo = (x - hi.astype(F32)).astype(BF16)
    return jnp.concatenate([hi, lo], axis=1)


def _gla_chunk(q, k, v, g, st_ref, rev):
    c = GLA_C
    row = lax.broadcasted_iota(jnp.int32, (c, c), 0)
    col = lax.broadcasted_iota(jnp.int32, (c, c), 1)
    tri = (col >= row) if rev else (col <= row)
    cs = _bdot(jnp.where(tri, 1.0, 0.0).astype(BF16), _split_hi_lo(g))
    b = cs[:, :GLA_DK] + cs[:, GLA_DK:]
    causal = (col >= row) if rev else (col <= row)

    a = jnp.zeros((c, c), F32)
    rid = lax.broadcasted_iota(jnp.int32, (c, GLA_DK), 0)
    for m in GLA_LEVELS:
        nblk = c // (2 * m)
        if rev:
            bnd = [b[j * 2 * m + m:j * 2 * m + m + 1] for j in range(nblk)]
        else:
            bnd = [b[j * 2 * m + m - 1:j * 2 * m + m] for j in range(nblk)]
        ref = jnp.concatenate([jnp.broadcast_to(x, (2 * m, GLA_DK)) for x in bnd], axis=0)
        later = ((rid & m) == 0) if rev else ((rid & m) != 0)
        dq = jnp.minimum(b - ref, 0.0)
        dk_ = jnp.minimum(ref - b, 0.0)
        qh = jnp.where(later, q * jnp.exp(dq), 0.0).astype(BF16)
        kh = jnp.where(later, 0.0, k * jnp.exp(dk_)).astype(BF16)
        blk = lax.dot_general(qh, kh, _NT, preferred_element_type=F32)
        same = (row >> _log2(2 * m)) == (col >> _log2(2 * m))
        a = a + jnp.where(same, blk, 0.0)

    nsub = c // GLA_SUB
    lane_c = lax.broadcasted_iota(jnp.int32, (GLA_SUB, c), 1)
    srow = lax.broadcasted_iota(jnp.int32, (GLA_SUB, 1), 0)
    diag_rows = []
    for blk_i in range(nsub):
        r0 = blk_i * GLA_SUB
        qb = q[r0:r0 + GLA_SUB]
        bb = b[r0:r0 + GLA_SUB]
        acc = jnp.zeros((GLA_SUB, c), F32)
        for s in range(GLA_SUB):
            ks = k[r0 + s:r0 + s + 1]
            bs = b[r0 + s:r0 + s + 1]
            w = jnp.sum(qb * ks * jnp.exp(jnp.minimum(bb - bs, 0.0)), axis=-1, keepdims=True)
            keep = (srow <= s) if rev else (srow >= s)
            w = jnp.where(keep, w, 0.0)
            acc = jnp.where(lane_c == r0 + s, w, acc)
        diag_rows.append(acc)
    a = a + jnp.concatenate(diag_rows, axis=0)
    a = jnp.where(causal, a, 0.0)

    st = st_ref[...]
    inter = lax.dot_general((q * jnp.exp(b)).astype(BF16), st.astype(BF16), _NT,
                            preferred_element_type=F32)
    o = inter + _bdot(a.astype(BF16), v.astype(BF16))

    btot = b[0:1] if rev else b[c - 1:c]
    kd = (k * jnp.exp(btot - b)).astype(BF16)
    st_ref[...] = st * jnp.exp(btot) + lax.dot_general(v.astype(BF16), kd, _TN,
                                                       preferred_element_type=F32)
    return o


def _gla_kernel(*refs, n_tok, has_state):
    if has_state:
        (q_ref, k_ref, v_ref, r_ref, w2_ref, bg_ref, gn_ref, s0f_ref, s0b_ref,
         o_ref, lg_ref, of_ref, ob_ref, stf_ref, stb_ref) = refs
    else:
        (q_ref, k_ref, v_ref, r_ref, w2_ref, bg_ref, gn_ref,
         o_ref, sf_ref, sb_ref, lg_ref, of_ref, ob_ref, stf_ref, stb_ref) = refs
    nc = n_tok // GLA_C
    scale = GLA_DK ** -0.5

    rb = r_ref[...].astype(BF16)
    for z in range(2):
        x = _bdot(rb, w2_ref[z].astype(BF16)) + bg_ref[z]
        lg_ref[z] = (jnp.minimum(x, 0.0) - jnp.log1p(jnp.exp(-jnp.abs(x)))) * (1.0 / GLA_GATE_NORM)

    if has_state:
        stf_ref[...] = s0f_ref[0, 0]
        stb_ref[...] = s0b_ref[0, 0]
    else:
        stf_ref[...] = jnp.zeros_like(stf_ref)
        stb_ref[...] = jnp.zeros_like(stb_ref)

    def step(ci, carry):
        for rev in (False, True):
            cc = (nc - 1 - ci) if rev else ci
            sl = pl.ds(pl.multiple_of(cc * GLA_C, GLA_C), GLA_C)
            q = q_ref[sl, :] * scale
            o = _gla_chunk(q, k_ref[sl, :], v_ref[sl, :], lg_ref[1 if rev else 0, sl, :],
                           stb_ref if rev else stf_ref, rev)
            (ob_ref if rev else of_ref)[sl, :] = o
        return carry

    lax.fori_loop(0, nc, step, 0)

    o = of_ref[...] + ob_ref[...]
    o_ref[...] = o * lax.rsqrt(jnp.mean(o * o, axis=-1, keepdims=True) + EPS) * gn_ref[...]
    if not has_state:
        sf_ref[0, 0] = stf_ref[...]
        sb_ref[0, 0] = stb_ref[...]


def _gla(proj, w2, bg, gnorm, n_seq, n_tok, row_block0, states=None):
    qb = GLA_HK // GLA_DK
    spec = lambda width, off: pl.BlockSpec((n_tok, width), lambda b, h: (row_block0 + b, off + h))
    in_specs = [spec(GLA_DK, 0), spec(GLA_DK, qb), spec(GLA_DV, 2 * GLA_HK // GLA_DV),
                pl.BlockSpec((n_tok, 128), lambda b, h: (row_block0 + b, (2 * GLA_HK + 2 * GLA_HV) // 128)),
                pl.BlockSpec((2, 128, GLA_DK), lambda b, h: (0, 0, h)),
                pl.BlockSpec((2, 1, GLA_DK), lambda b, h: (0, 0, h)),
                pl.BlockSpec((1, GLA_DV), lambda b, h: (0, 0))]
    args = [proj, proj, proj, proj, w2, bg, gnorm.reshape(1, GLA_DV)]
    st_spec = pl.BlockSpec((1, 1, GLA_DV, GLA_DK), lambda b, h: (b, h, 0, 0))
    o_spec = pl.BlockSpec((n_tok, GLA_DV), lambda b, h: (b, h))
    o_shape = jax.ShapeDtypeStruct((n_seq * n_tok, GLA_HV), F32)
    if states is not None:
        in_specs += [st_spec, st_spec]
        args += list(states)
        out_specs, out_shape = o_spec, o_shape
    else:
        st_shape = jax.ShapeDtypeStruct((n_seq, GLA_HEADS, GLA_DV, GLA_DK), F32)
        out_specs, out_shape = [o_spec, st_spec, st_spec], [o_shape, st_shape, st_shape]
    return pl.pallas_call(
        functools.partial(_gla_kernel, n_tok=n_tok, has_state=states is not None),
        grid=(n_seq, GLA_HEADS),
        in_specs=in_specs,
        out_specs=out_specs,
        out_shape=out_shape,
        scratch_shapes=[pltpu.VMEM((2, n_tok, GLA_DK), F32),
                        pltpu.VMEM((n_tok, GLA_DV), F32),
                        pltpu.VMEM((n_tok, GLA_DV), F32),
                        pltpu.VMEM((GLA_DV, GLA_DK), F32),
                        pltpu.VMEM((GLA_DV, GLA_DK), F32)],
        compiler_params=_params(("arbitrary", "arbitrary")),
        name="gla_latent" if states is not None else "gla_prompt",
    )(*args)


DIFF_HB = 2
DIFF_QL = DIFF_HB * DIFF_HD
DIFF_VL = DIFF_HB * 2 * DIFF_HD
DIFF_TQ = 128


def _diff_lambda(lam_ref):
    l = lam_ref[...]
    a = jnp.sum(l[0:1] * l[1:2], axis=-1, keepdims=True)
    b = jnp.sum(l[2:3] * l[3:4], axis=-1, keepdims=True)
    return jnp.exp(a) - jnp.exp(b) + DIFF_LAMBDA_INIT


def _diff_finish(a, v, sn_ref):
    o2 = _bdot(a.astype(BF16), v)
    o = _unstack_heads(o2, DIFF_HB, 2 * DIFF_HD)
    return _group_rms(o, sn_ref[...], 2 * DIFF_HD) * (1.0 - DIFF_LAMBDA_INIT)


def _diff_prompt_kernel(q0_ref, q1_ref, k0_ref, k1_ref, v_ref, gq_ref, gk_ref, lam_ref, sn_ref,
                        o_ref, kn0_ref, kn1_ref):
    scale = DIFF_HD ** -0.5
    lam = _diff_lambda(lam_ref)
    ps = []
    for q_ref, k_ref, kn_ref in ((q0_ref, k0_ref, kn0_ref), (q1_ref, k1_ref, kn1_ref)):
        q = _group_rms(q_ref[...], gq_ref[...], DIFF_HD) * scale
        k = _group_rms(k_ref[...], gk_ref[...], DIFF_HD)
        kn_ref[...] = k
        q2 = _stack_heads(q, DIFF_HB, DIFF_HD).astype(BF16)
        s = lax.dot_general(q2, k.astype(BF16), _NT, preferred_element_type=F32)
        (p,), inv = _softmax_parts([s])
        ps.append(p * inv)
    o_ref[...] = _diff_finish(ps[0] - lam * ps[1], v_ref[...].astype(BF16), sn_ref)


def _diff_prompt(qkv, gq, gk, lam, sub_norm):
    nb = DIFF_HEADS // DIFF_HB
    qk = lambda off: pl.BlockSpec((SEQ, DIFF_QL), lambda b, j: (b, off + j))
    vec = lambda n: pl.BlockSpec((1, n), lambda b, j: (0, 0))
    v_spec = pl.BlockSpec((SEQ, DIFF_VL), lambda b, j: (b, 2 * D_MODEL // DIFF_VL + j))
    kn_shape = jax.ShapeDtypeStruct((N_PROMPT, DIFF_HEADS * DIFF_HD), F32)
    return pl.pallas_call(
        _diff_prompt_kernel,
        grid=(BATCH, nb),
        in_specs=[qk(0), qk(nb), qk(2 * nb), qk(3 * nb), v_spec, vec(DIFF_QL), vec(DIFF_QL),
                  pl.BlockSpec((4, DIFF_HD), lambda b, j: (0, 0)), vec(DIFF_VL)],
        out_specs=[pl.BlockSpec((SEQ, DIFF_VL), lambda b, j: (b, j)), qk(0), qk(0)],
        out_shape=[jax.ShapeDtypeStruct((N_PROMPT, D_MODEL), F32), kn_shape, kn_shape],
        compiler_params=_params(("arbitrary", "arbitrary")),
        name="diff_prompt",
    )(qkv, qkv, qkv, qkv, qkv, jnp.tile(gq, DIFF_HB).reshape(1, -1),
      jnp.tile(gk, DIFF_HB).reshape(1, -1), lam, jnp.tile(sub_norm, DIFF_HB).reshape(1, -1))


def _diff_latent_kernel(q0_ref, q1_ref, k0_ref, k1_ref, v_ref, kc0_ref, kc1_ref, vc_ref,
                        cos_ref, sin_ref, cosq_ref, sinq_ref, gq_ref, gk_ref, lam_ref, sn_ref,
                        o_ref, kb0_ref, kb1_ref, vb_ref):
    scale = DIFF_HD ** -0.5
    half = DIFF_HD // 4
    rows = 256

    @pl.when(pl.program_id(2) == 0)
    def _():
        for k_ref, kc_ref, kb_ref in ((k0_ref, kc0_ref, kb0_ref), (k1_ref, kc1_ref, kb1_ref)):
            kb_ref[0:PAST_LEN, :] = kc_ref[0].astype(BF16)

            def prep(r, carry):
                sl = pl.ds(pl.multiple_of(r * rows, rows), rows)
                k = _group_rms(k_ref[sl, :], gk_ref[...], DIFF_HD)
                k = _rope(k, cos_ref[sl, :], sin_ref[sl, :], half)
                kb_ref[pl.ds(pl.multiple_of(PAST_LEN + r * rows, rows), rows), :] = k.astype(BF16)
                return carry

            lax.fori_loop(0, DEC_SEQ // rows, prep, 0)
        vb_ref[0:PAST_LEN, :] = vc_ref[0].astype(BF16)
        vb_ref[PAST_LEN:, :] = v_ref[...].astype(BF16)

    lam = _diff_lambda(lam_ref)
    ps = []
    for q_ref, kb_ref in ((q0_ref, kb0_ref), (q1_ref, kb1_ref)):
        q = _group_rms(q_ref[...], gq_ref[...], DIFF_HD)
        q = _rope(q, cosq_ref[...], sinq_ref[...], half) * scale
        q2 = _stack_heads(q, DIFF_HB, DIFF_HD).astype(BF16)
        s = lax.dot_general(q2, kb_ref[...], _NT, preferred_element_type=F32)
        (p,), inv = _softmax_parts([s])
        ps.append(p * inv)
    o_ref[...] = _diff_finish(ps[0] - lam * ps[1], vb_ref[...], sn_ref)


def _diff_latent(qkv, cache_k, cache_v, cos, sin, gq, gk, lam, sub_norm):
    nb = DIFF_HEADS // DIFF_HB
    nq = DEC_SEQ // DIFF_TQ
    q0 = N_PROMPT // DIFF_TQ
    lat0 = N_PROMPT // DEC_SEQ
    n_keys = PAST_LEN + DEC_SEQ
    q_spec = lambda off: pl.BlockSpec((DIFF_TQ, DIFF_QL), lambda b, j, t: (q0 + b * nq + t, off + j))
    k_spec = lambda off: pl.BlockSpec((DEC_SEQ, DIFF_QL), lambda b, j, t: (lat0 + b, off + j))
    v_spec = pl.BlockSpec((DEC_SEQ, DIFF_VL), lambda b, j, t: (lat0 + b, 2 * D_MODEL // DIFF_VL + j))
    kc_spec = lambda off: pl.BlockSpec((1, PAST_LEN, DIFF_QL), lambda b, j, t: (b, 0, off + j))
    vc_spec = pl.BlockSpec((1, PAST_LEN, DIFF_VL), lambda b, j, t: (b, 0, j))
    tab = pl.BlockSpec((DEC_SEQ, DIFF_QL), lambda b, j, t: (0, 0))
    tabq = pl.BlockSpec((DIFF_TQ, DIFF_QL), lambda b, j, t: (t, 0))
    vec = lambda n: pl.BlockSpec((1, n), lambda b, j, t: (0, 0))
    return pl.pallas_call(
        _diff_latent_kernel,
        grid=(DEC_BATCH, nb, nq),
        in_specs=[q_spec(0), q_spec(nb), k_spec(2 * nb), k_spec(3 * nb), v_spec,
                  kc_spec(0), kc_spec(nb), vc_spec, tab, tab, tabq, tabq,
                  vec(DIFF_QL), vec(DIFF_QL),
                  pl.BlockSpec((4, DIFF_HD), lambda b, j, t: (0, 0)), vec(DIFF_VL)],
        out_specs=pl.BlockSpec((DIFF_TQ, DIFF_VL), lambda b, j, t: (b * nq + t, j)),
        out_shape=jax.ShapeDtypeStruct((N_LATENT, D_MODEL), F32),
        scratch_shapes=[pltpu.VMEM((n_keys, DIFF_QL), BF16),
                        pltpu.VMEM((n_keys, DIFF_QL), BF16),
                        pltpu.VMEM((n_keys, DIFF_VL), BF16)],
        compiler_params=_params(("arbitrary", "arbitrary", "arbitrary")),
        name="diff_latent",
    )(qkv, qkv, qkv, qkv, qkv, _tokens_first(cache_k), _tokens_first(cache_k),
      _tokens_first(cache_v), cos, sin, cos, sin,
      jnp.tile(gq, DIFF_HB).reshape(1, -1), jnp.tile(gk, DIFF_HB).reshape(1, -1), lam,
      jnp.tile(sub_norm, DIFF_HB).reshape(1, -1))


MLA_HB = 2
MLA_HL = 128
MLA_LANES = MLA_HB * MLA_HL
MLA_TQ = 128


def _mla_keys(kv, kr, gk):
    lane = lax.broadcasted_iota(jnp.int32, kv.shape, 1)
    kr2 = jnp.concatenate([kr] * MLA_HB, axis=1)
    k = jnp.where((lane & (MLA_HL - 1)) < MLA_NOPE, kv, kr2)
    return _group_rms(k, gk, MLA_HL, n_real=MLA_QK)


def _mla_out(o2):
    tq = o2.shape[0] // MLA_HB
    oa = pltpu.roll(o2[0:tq, 0:MLA_HL], MLA_HL - MLA_V, axis=1)
    ob = o2[tq:, MLA_HL:]
    lane = lax.broadcasted_iota(jnp.int32, oa.shape, 1)
    return jnp.where(lane < MLA_V, oa, ob)


def _mla_prompt_kernel(q_ref, kv_ref, kr_ref, gq_ref, gk_ref, o_ref):
    scale = MLA_QK ** -0.5
    q = _group_rms(q_ref[...], gq_ref[...], MLA_HL, n_real=MLA_QK) * scale
    kv = kv_ref[...]
    k = _mla_keys(kv, kr_ref[...], gk_ref[...])
    q2 = _stack_heads(q, MLA_HB, MLA_HL).astype(BF16)
    s = lax.dot_general(q2, k.astype(BF16), _NT, preferred_element_type=F32)
    (p,), inv = _softmax_parts([s])
    o_ref[...] = _mla_out(_bdot(p.astype(BF16), kv.astype(BF16)) * inv)


def _mla_prompt(qp, kvp, low, gq, gk):
    nb = MLA_HEADS // MLA_HB
    blk = pl.BlockSpec((SEQ, MLA_LANES), lambda b, j: (b, j))
    vec = pl.BlockSpec((1, MLA_LANES), lambda b, j: (0, 0))
    return pl.pallas_call(
        _mla_prompt_kernel,
        grid=(BATCH, nb),
        in_specs=[blk, blk, pl.BlockSpec((SEQ, MLA_HL), lambda b, j: (b, MLA_LOW_KR // MLA_HL)),
                  vec, vec],
        out_specs=pl.BlockSpec((SEQ, MLA_HB * MLA_V), lambda b, j: (b, j)),
        out_shape=jax.ShapeDtypeStruct((N_PROMPT, MLA_HEADS * MLA_V), F32),
        compiler_params=_params(("arbitrary", "arbitrary")),
        name="mla_prompt",
    )(qp, kvp, low, gq, gk)


def _mla_latent_kernel(q_ref, kv_ref, kr_ref, kvc_ref, krc_ref, cos_ref, sin_ref, cosq_ref,
                       sinq_ref, gq_ref, gk_ref, o_ref, kb_ref, vb_ref):
    scale = MLA_QK ** -0.5
    half = MLA_ROPE // 4
    rows = 256

    @pl.when(pl.program_id(2) == 0)
    def _():
        kvc = kvc_ref[...]
        kb_ref[0:PAST_LEN, :] = _mla_keys(kvc, krc_ref[...], gk_ref[...]).astype(BF16)
        vb_ref[0:PAST_LEN, :] = kvc.astype(BF16)

        def prep(r, carry):
            sl = pl.ds(pl.multiple_of(r * rows, rows), rows)
            dst = pl.ds(pl.multiple_of(PAST_LEN + r * rows, rows), rows)
            kv = kv_ref[sl, :]
            k = _mla_keys(kv, kr_ref[sl, :], gk_ref[...])
            k = _rope(k, cos_ref[sl, :], sin_ref[sl, :], half)
            kb_ref[dst, :] = k.astype(BF16)
            vb_ref[dst, :] = kv.astype(BF16)
            return carry

        lax.fori_loop(0, DEC_SEQ // rows, prep, 0)

    q = _group_rms(q_ref[...], gq_ref[...], MLA_HL, n_real=MLA_QK)
    q = _rope(q, cosq_ref[...], sinq_ref[...], half) * scale
    q2 = _stack_heads(q, MLA_HB, MLA_HL).astype(BF16)
    s = lax.dot_general(q2, kb_ref[...], _NT, preferred_element_type=F32)
    (p,), inv = _softmax_parts([s])
    o_ref[...] = _mla_out(_bdot(p.astype(BF16), vb_ref[...]) * inv)


def _mla_latent(qp, kvp, low, kvc, krc, cos, sin, gq, gk):
    nb = MLA_HEADS // MLA_HB
    nq = DEC_SEQ // MLA_TQ
    q0 = N_PROMPT // MLA_TQ
    lat0 = N_PROMPT // DEC_SEQ
    n_keys = PAST_LEN + DEC_SEQ
    tab = pl.BlockSpec((DEC_SEQ, MLA_LANES), lambda b, j, t: (0, 0))
    tabq = pl.BlockSpec((MLA_TQ, MLA_LANES), lambda b, j, t: (t, 0))
    vec = pl.BlockSpec((1, MLA_LANES), lambda b, j, t: (0, 0))
    return pl.pallas_call(
        _mla_latent_kernel,
        grid=(DEC_BATCH, nb, nq),
        in_specs=[pl.BlockSpec((MLA_TQ, MLA_LANES), lambda b, j, t: (q0 + b * nq + t, j)),
                  pl.BlockSpec((DEC_SEQ, MLA_LANES), lambda b, j, t: (lat0 + b, j)),
                  pl.BlockSpec((DEC_SEQ, MLA_HL), lambda b, j, t: (lat0 + b, MLA_LOW_KR // MLA_HL)),
                  pl.BlockSpec((PAST_LEN, MLA_LANES), lambda b, j, t: (b, j)),
                  pl.BlockSpec((PAST_LEN, MLA_HL), lambda b, j, t: (b, 0)),
                  tab, tab, tabq, tabq, vec, vec],
        out_specs=pl.BlockSpec((MLA_TQ, MLA_HB * MLA_V), lambda b, j, t: (b * nq + t, j)),
        out_shape=jax.ShapeDtypeStruct((N_LATENT, MLA_HEADS * MLA_V), F32),
        scratch_shapes=[pltpu.VMEM((n_keys, MLA_LANES), BF16),
                        pltpu.VMEM((n_keys, MLA_LANES), BF16)],
        compiler_params=_params(("arbitrary", "arbitrary", "arbitrary")),
        name="mla_latent",
    )(qp, kvp, low, kvc, krc, cos, sin, cos, sin, gq, gk)


MLA_LOW_Q = 0
MLA_LOW_KV = 512
MLA_LOW_KR = 768
MLA_LOW_N = 896


def _axial_tables(n_tok, rdim):
    nf = rdim // 4
    freqs = ROPE_BASE ** (-jnp.arange(nf, dtype=F32) / nf)
    t = jnp.arange(n_tok)
    rowp = (t // GRID_W).astype(F32)
    colp = (t % GRID_W).astype(F32)
    ang = jnp.stack([rowp[:, None] * freqs, colp[:, None] * freqs], axis=1)
    cos, sin = jnp.cos(ang), jnp.sin(ang)
    cos_l = jnp.stack([cos, cos], axis=2).reshape(n_tok, rdim)
    sin_l = jnp.stack([-sin, sin], axis=2).reshape(n_tok, rdim)
    return cos_l, sin_l


def _diff_rope_tables():
    cos, sin = _axial_tables(DEC_SEQ, DIFF_HD)
    return jnp.tile(cos, (1, DIFF_HB)), jnp.tile(sin, (1, DIFF_HB))


def _mla_rope_tables():
    cos, sin = _axial_tables(DEC_SEQ, MLA_ROPE)
    ones = jnp.ones((DEC_SEQ, MLA_NOPE), F32)
    pad1 = jnp.ones((DEC_SEQ, MLA_HL - MLA_QK), F32)
    cos_h = jnp.concatenate([ones, cos, pad1], axis=1)
    sin_h = jnp.concatenate([0 * ones, sin, 0 * pad1], axis=1)
    return jnp.tile(cos_h, (1, MLA_HB)), jnp.tile(sin_h, (1, MLA_HB))


def _heads_first(x, n_seq, n_tok, heads, hd):
    return jnp.transpose(x.reshape(n_seq, n_tok, heads, hd), (0, 2, 1, 3))


def _tokens_first(cache):
    b, h, l, d = cache.shape
    return jnp.transpose(cache, (0, 2, 1, 3)).reshape(b, l, h * d)


def _pad_heads(w, heads, hd, hl):
    k = w.shape[0]
    return jnp.pad(w.reshape(k, heads, hd), ((0, 0), (0, 0), (0, hl - hd))).reshape(k, heads * hl)


def kernel(x_prompt, x_sample, cache_l0_k, cache_l0_v, state_l1_fwd, state_l1_bwd, cache_l2_k,
           cache_l2_v, cache_l3_ckv, cache_l3_krope, c, c_ctx, ada_w, ada_b, norm_mix, norm_ffn,
           ffn_w_up, ffn_conv_w, ffn_conv_b, ffn_w_down, na_w_qkv, na_q_norm, na_k_norm, na_bias,
           na_w_o, gla_w_qkvg, gla_w_gate1, gla_w_gate2, gla_b_gate, gla_o_norm, gla_w_o,
           diff_w_qkv, diff_q_norm, diff_k_norm, diff_lambda, diff_sub_norm, diff_w_o, mla_w_dq,
           mla_q_a_norm, mla_w_uq, mla_w_dkv, mla_kv_a_norm, mla_w_ukv, mla_q_norm, mla_k_norm,
           mla_w_o):
    x = jnp.concatenate([x_prompt.reshape(N_PROMPT, D_MODEL),
                         x_sample.reshape(N_LATENT, D_MODEL)], axis=0)
    cvecs = jnp.concatenate([c_ctx[None], c, jnp.zeros((5, D_MODEL), F32)], axis=0)
    mods_all = _ada_mods(cvecs, ada_w, ada_b)

    mods = mods_all[0]
    qkv = _norm_mod_proj(x, norm_mix[0], mods, na_w_qkv, 512, "na_qkv")
    o_p, kn_p = _na_prompt(qkv, na_q_norm, na_k_norm)
    o_s = _na_latent(qkv, cache_l0_k, cache_l0_v, _na_bias_blocks(na_bias), na_q_norm, na_k_norm)
    new_l0_k = _heads_first(kn_p, BATCH, SEQ, NA_HEADS, NA_HD)
    new_l0_v = _heads_first(qkv[:N_PROMPT, 2 * D_MODEL:], BATCH, SEQ, NA_HEADS, NA_HD)
    x = _out_proj_residual(x, jnp.concatenate([o_p, o_s], axis=0), mods, na_w_o, "na_out")
    x = _ffn(x, norm_ffn[0], mods, ffn_w_up[0], ffn_conv_w[0], ffn_conv_b[0], ffn_w_down[0])

    mods = mods_all[1]
    n_gla = 2 * GLA_HK + 2 * GLA_HV
    w_cat = jnp.concatenate(
        [gla_w_qkvg, gla_w_gate1[0], gla_w_gate1[1],
         jnp.zeros((D_MODEL, 128 - 2 * GLA_GATE_RANK), F32)], axis=1)
    proj = _norm_mod_proj(x, norm_mix[1], mods, w_cat, 640, "gla_proj")
    w2 = jnp.zeros((2, 128, GLA_HK), F32)
    w2 = w2.at[0, :GLA_GATE_RANK].set(gla_w_gate2[0])
    w2 = w2.at[1, GLA_GATE_RANK:2 * GLA_GATE_RANK].set(gla_w_gate2[1])
    bg = gla_b_gate.reshape(2, 1, GLA_HK)
    o_p, st_f, st_b = _gla(proj, w2, bg, gla_o_norm, BATCH, SEQ, 0)
    s0 = (jnp.swapaxes(state_l1_fwd, 2, 3), jnp.swapaxes(state_l1_bwd, 2, 3))
    o_s = _gla(proj, w2, bg, gla_o_norm, DEC_BATCH, DEC_SEQ, N_PROMPT // DEC_SEQ, states=s0)
    new_l1_fwd = jnp.swapaxes(st_f, 2, 3)
    new_l1_bwd = jnp.swapaxes(st_b, 2, 3)
    x = _out_proj_residual(x, jnp.concatenate([o_p, o_s], axis=0), mods, gla_w_o, "gla_out",
                           gate=proj, gate_col_block=(2 * GLA_HK + GLA_HV) // GLA_HV)
    x = _ffn(x, norm_ffn[1], mods, ffn_w_up[1], ffn_conv_w[1], ffn_conv_b[1], ffn_w_down[1])

    mods = mods_all[2]
    qkv = _norm_mod_proj(x, norm_mix[2], mods, diff_w_qkv, 512, "diff_qkv")
    o_p, kn0, kn1 = _diff_prompt(qkv, diff_q_norm, diff_k_norm, diff_lambda, diff_sub_norm)
    kn_p = jnp.concatenate([kn0, kn1], axis=1)
    cos_d, sin_d = _diff_rope_tables()
    o_s = _diff_latent(qkv, cache_l2_k, cache_l2_v, cos_d, sin_d, diff_q_norm, diff_k_norm,
                       diff_lambda, diff_sub_norm)
    new_l2_k = _heads_first(kn_p, BATCH, SEQ, 2 * DIFF_HEADS, DIFF_HD)
    new_l2_v = _heads_first(qkv[:N_PROMPT, 2 * D_MODEL:], BATCH, SEQ, DIFF_HEADS, 2 * DIFF_HD)
    x = _out_proj_residual(x, jnp.concatenate([o_p, o_s], axis=0), mods, diff_w_o, "diff_out")
    x = _ffn(x, norm_ffn[2], mods, ffn_w_up[2], ffn_conv_w[2], ffn_conv_b[2], ffn_w_down[2])

    mods = mods_all[3]
    zc = lambda n: jnp.zeros((D_MODEL, n), F32)
    w_low = jnp.concatenate(
        [mla_w_dq, zc(MLA_LOW_KV - MLA_Q_RANK), mla_w_dkv[:, :MLA_KV_RANK],
         zc(MLA_NOPE), mla_w_dkv[:, MLA_KV_RANK:], zc(MLA_HL - MLA_QK)], axis=1)
    low = _norm_mod_proj(x, norm_mix[3], mods, w_low, MLA_LOW_N, "mla_down")
    w_uq = _pad_heads(mla_w_uq, MLA_HEADS, MLA_QK, MLA_HL)
    qp, _ = _rms_matmul(low, 0, MLA_Q_RANK, mla_q_a_norm, w_uq, True, "mla_uq")
    kvp, ckv = _rms_matmul(low, MLA_LOW_KV // MLA_KV_RANK, MLA_KV_RANK, mla_kv_a_norm, mla_w_ukv,
                           True, "mla_ukv")
    kvc, _ = _rms_matmul(cache_l3_ckv.reshape(DEC_BATCH * PAST_LEN, MLA_KV_RANK), 0, MLA_KV_RANK,
                         mla_kv_a_norm, mla_w_ukv, False, "mla_ukv_cache")
    krc = jnp.pad(cache_l3_krope.reshape(DEC_BATCH * PAST_LEN, MLA_ROPE),
                  ((0, 0), (MLA_NOPE, MLA_HL - MLA_QK)))
    pad_gain = lambda g: jnp.tile(jnp.pad(g, (0, MLA_HL - MLA_QK)), MLA_HB).reshape(1, -1)
    gq, gk = pad_gain(mla_q_norm), pad_gain(mla_k_norm)
    o_p = _mla_prompt(qp, kvp, low, gq, gk)
    cos_m, sin_m = _mla_rope_tables()
    o_s = _mla_latent(qp, kvp, low, kvc, krc, cos_m, sin_m, gq, gk)
    new_l3_ckv = ckv[:N_PROMPT].reshape(BATCH, SEQ, MLA_KV_RANK)
    new_l3_krope = low[:N_PROMPT, MLA_LOW_KR + MLA_NOPE:MLA_LOW_KR + MLA_QK].reshape(
        BATCH, SEQ, MLA_ROPE)
    x = _out_proj_residual(x, jnp.concatenate([o_p, o_s], axis=0), mods, mla_w_o, "mla_out")
    x = _ffn(x, norm_ffn[3], mods, ffn_w_up[3], ffn_conv_w[3], ffn_conv_b[3], ffn_w_down[3])

    y_prompt = x[:N_PROMPT].reshape(BATCH, SEQ, D_MODEL)
    y_sample = x[N_PROMPT:].reshape(DEC_BATCH, DEC_SEQ, D_MODEL)
    return (y_prompt, y_sample, new_l0_k, new_l0_v, new_l1_fwd, new_l1_bwd, new_l2_k, new_l2_v,
            new_l3_ckv, new_l3_krope)
```

```python
import functools
import math

import jax
import jax.numpy as jnp
from jax import lax
from jax.experimental import pallas as pl
from jax.experimental.pallas import tpu as pltpu

F32 = jnp.float32
BF16 = jnp.bfloat16

D_MODEL = 1024
BATCH = 16
SEQ = 256
DEPTH = 4
DEC_BATCH = 2
DEC_SEQ = 2048
PAST_LEN = 256
GRID_W = 64
D_FF = 2816
EPS = 1e-6
ROPE_BASE = 10000.0

NA_HEADS = 16
NA_HD = 64
NA_WIN_R = 8
NA_WIN_C = 16

GLA_HEADS = 4
GLA_DK = 128
GLA_DV = 256
GLA_HK = GLA_HEADS * GLA_DK
GLA_HV = GLA_HEADS * GLA_DV
GLA_GATE_RANK = 16
GLA_GATE_NORM = 16.0

DIFF_HEADS = 8
DIFF_HD = 64
DIFF_LAMBDA_INIT = 0.8 - 0.6 * math.exp(-0.3 * 2)

MLA_HEADS = 16
MLA_Q_RANK = 384
MLA_KV_RANK = 256
MLA_NOPE = 64
MLA_ROPE = 32
MLA_V = 64
MLA_QK = MLA_NOPE + MLA_ROPE

N_PROMPT = BATCH * SEQ
N_LATENT = DEC_BATCH * DEC_SEQ
N_TOK = N_PROMPT + N_LATENT
TOK_TILE = 2048
N_TOK_TILES = N_TOK // TOK_TILE
FF_CHUNK = 256
N_FF_CHUNKS = D_FF // FF_CHUNK
NEG = -1e30

VMEM_LIMIT = 56 * 1024 * 1024

_NT = (((1,), (1,)), ((), ()))
_TN = (((0,), (0,)), ((), ()))


def _params(sem, vmem=VMEM_LIMIT):
    return pltpu.CompilerParams(dimension_semantics=sem, vmem_limit_bytes=vmem)


def _log2(n):
    assert n & (n - 1) == 0
    return n.bit_length() - 1


def _silu(x):
    return x / (1.0 + jnp.exp(-x))


def _bdot(a, b):
    return jnp.dot(a, b, preferred_element_type=F32)


def _softmax_parts(parts):
    m = parts[0].max(axis=-1, keepdims=True)
    for s in parts[1:]:
        m = jnp.maximum(m, s.max(axis=-1, keepdims=True))
    ps = [jnp.exp(s - m) for s in parts]
    l = ps[0].sum(axis=-1, keepdims=True)
    for p in ps[1:]:
        l = l + p.sum(axis=-1, keepdims=True)
    return ps, 1.0 / l


def _group_rms(x, gain, group, n_real=None):
    lanes = x.shape[-1]
    n_real = n_real or group
    x2 = x * x
    if group == lanes:
        ms = jnp.sum(x2, axis=-1, keepdims=True)
    else:
        gid = lax.broadcasted_iota(jnp.int32, x.shape, 1) >> _log2(group)
        ms = jnp.zeros_like(x)
        for i in range(lanes // group):
            sel = gid == i
            si = jnp.sum(jnp.where(sel, x2, 0.0), axis=-1, keepdims=True)
            ms = jnp.where(sel, si, ms)
    return x * lax.rsqrt(ms * (1.0 / n_real) + EPS) * gain


def _rope(x, cos, sin, half):
    lanes = x.shape[-1]
    lane = lax.broadcasted_iota(jnp.int32, x.shape, 1)
    up = pltpu.roll(x, lanes - half, axis=1)
    dn = pltpu.roll(x, half, axis=1)
    swapped = jnp.where((lane & (2 * half - 1)) < half, up, dn)
    return x * cos + swapped * sin


def _stack_heads(q, n_heads, head_lanes):
    hid = lax.broadcasted_iota(jnp.int32, q.shape, 1) >> _log2(head_lanes)
    zero = jnp.zeros_like(q)
    return jnp.concatenate([jnp.where(hid == i, q, zero) for i in range(n_heads)], axis=0)


def _unstack_heads(o, n_heads, head_lanes):
    rows = o.shape[0] // n_heads
    hid = lax.broadcasted_iota(jnp.int32, (rows, o.shape[1]), 1) >> _log2(head_lanes)
    out = o[0:rows]
    for i in range(1, n_heads):
        out = jnp.where(hid == i, o[i * rows:(i + 1) * rows], out)
    return out


ADA_TN = 1536


def _ada_kernel(c_ref, w_ref, b_ref, o_ref):
    s = _silu(c_ref[...])
    o_ref[0] = jnp.dot(s, w_ref[0], preferred_element_type=F32,
                       precision=lax.Precision.HIGHEST) + b_ref[0]


def _ada_mods(cvecs, ada_w, ada_b):
    out = pl.pallas_call(
        _ada_kernel,
        grid=(DEPTH, 6 * D_MODEL // ADA_TN),
        in_specs=[pl.BlockSpec((8, D_MODEL), lambda l, j: (0, 0)),
                  pl.BlockSpec((1, D_MODEL, ADA_TN), lambda l, j: (l, 0, j)),
                  pl.BlockSpec((1, 1, ADA_TN), lambda l, j: (l, 0, j))],
        out_specs=pl.BlockSpec((1, 8, ADA_TN), lambda l, j: (l, 0, j)),
        out_shape=jax.ShapeDtypeStruct((DEPTH, 8, 6 * D_MODEL), F32),
        compiler_params=_params(("arbitrary", "arbitrary")),
        name="ada_mod",
    )(cvecs, ada_w, ada_b.reshape(DEPTH, 1, 6 * D_MODEL))
    return out.reshape(DEPTH, 8, 6, D_MODEL)[:, :3]


def _mod_group_of_tile(i):
    return jnp.maximum(i - (N_PROMPT // TOK_TILE - 1), 0)


def _norm_mod_rows(x_ref, g_ref, mod_ref, h_ref, shift_idx, scale_idx, rows=64):
    g = g_ref[...]
    sc = 1.0 + mod_ref[0, scale_idx:scale_idx + 1, :]
    sh = mod_ref[0, shift_idx:shift_idx + 1, :]

    def body(r, carry):
        sl = pl.ds(pl.multiple_of(r * rows, rows), rows)
        xf = x_ref[sl, :]
        ms = jnp.mean(xf * xf, axis=-1, keepdims=True)
        y = xf * lax.rsqrt(ms + EPS) * g
        h_ref[sl, :] = (y * sc + sh).astype(BF16)
        return carry

    lax.fori_loop(0, x_ref.shape[0] // rows, body, 0)


def _proj_kernel(x_ref, g_ref, mod_ref, w_ref, o_ref, h_ref, *, shift_idx, scale_idx):
    @pl.when(pl.program_id(1) == 0)
    def _():
        _norm_mod_rows(x_ref, g_ref, mod_ref, h_ref, shift_idx, scale_idx)

    o_ref[...] = _bdot(h_ref[...], w_ref[...].astype(BF16))


def _norm_mod_proj(x, g, mods, w, tn, name):
    n = w.shape[1]
    return pl.pallas_call(
        functools.partial(_proj_kernel, shift_idx=0, scale_idx=1),
        grid=(N_TOK_TILES, n // tn),
        in_specs=[pl.BlockSpec((TOK_TILE, D_MODEL), lambda i, j: (i, 0)),
                  pl.BlockSpec((1, D_MODEL), lambda i, j: (0, 0)),
                  pl.BlockSpec((1, 6, D_MODEL), lambda i, j: (_mod_group_of_tile(i), 0, 0)),
                  pl.BlockSpec((D_MODEL, tn), lambda i, j: (0, j))],
        out_specs=pl.BlockSpec((TOK_TILE, tn), lambda i, j: (i, j)),
        out_shape=jax.ShapeDtypeStruct((N_TOK, n), F32),
        scratch_shapes=[pltpu.VMEM((TOK_TILE, D_MODEL), BF16)],
        compiler_params=_params(("arbitrary", "arbitrary")),
        name=name,
    )(x, g.reshape(1, D_MODEL), mods, w)


def _rms_matmul_kernel(a_ref, g_ref, w_ref, o_ref, n_ref, *, normalise):
    a = a_ref[...]
    if normalise:
        a = a * lax.rsqrt(jnp.mean(a * a, axis=-1, keepdims=True) + EPS) * g_ref[...]
    n_ref[...] = a
    o_ref[...] = _bdot(a.astype(BF16), w_ref[...].astype(BF16))


def _rms_matmul(a, col_block, k, g, w, normalise, name, tm=512):
    rows, n = a.shape[0], w.shape[1]
    return pl.pallas_call(
        functools.partial(_rms_matmul_kernel, normalise=normalise),
        grid=(rows // tm,),
        in_specs=[pl.BlockSpec((tm, k), lambda i: (i, col_block)),
                  pl.BlockSpec((1, k), lambda i: (0, 0)),
                  pl.BlockSpec((k, n), lambda i: (0, 0))],
        out_specs=[pl.BlockSpec((tm, n), lambda i: (i, 0)),
                   pl.BlockSpec((tm, k), lambda i: (i, 0))],
        out_shape=[jax.ShapeDtypeStruct((rows, n), F32),
                   jax.ShapeDtypeStruct((rows, k), F32)],
        compiler_params=_params(("arbitrary",)),
        name=name,
    )(a, g.reshape(1, k), w)


OUT_TM = 512


def _oproj_kernel(*refs, gate_idx, gated):
    if gated:
        x_ref, a_ref, g_ref, mod_ref, w_ref, o_ref, wb_ref = refs
    else:
        x_ref, a_ref, mod_ref, w_ref, o_ref, wb_ref = refs

    @pl.when(pl.program_id(0) == 0)
    def _():
        wb_ref[...] = w_ref[...].astype(BF16)

    a = a_ref[...]
    if gated:
        a = a * _silu(g_ref[...])
    y = _bdot(a.astype(BF16), wb_ref[...])
    o_ref[...] = x_ref[...] + mod_ref[0, gate_idx:gate_idx + 1, :] * y


def _oproj_group(t):
    first_latent = N_PROMPT // OUT_TM
    return jnp.where(t < first_latent, 0, 1 + (t - first_latent) // (DEC_SEQ // OUT_TM))


def _out_proj_residual(x, a, mods, w, name, gate=None, gate_col_block=0):
    k = w.shape[0]
    in_specs = [pl.BlockSpec((OUT_TM, D_MODEL), lambda t: (t, 0)),
                pl.BlockSpec((OUT_TM, k), lambda t: (t, 0))]
    args = [x, a]
    if gate is not None:
        in_specs.append(pl.BlockSpec((OUT_TM, k), lambda t: (t, gate_col_block)))
        args.append(gate)
    in_specs += [pl.BlockSpec((1, 6, D_MODEL), lambda t: (_oproj_group(t), 0, 0)),
                 pl.BlockSpec((k, D_MODEL), lambda t: (0, 0))]
    args += [mods, w]
    return pl.pallas_call(
        functools.partial(_oproj_kernel, gate_idx=2, gated=gate is not None),
        grid=(N_TOK // OUT_TM,),
        in_specs=in_specs,
        out_specs=pl.BlockSpec((OUT_TM, D_MODEL), lambda t: (t, 0)),
        out_shape=jax.ShapeDtypeStruct((N_TOK, D_MODEL), F32),
        scratch_shapes=[pltpu.VMEM((k, D_MODEL), BF16)],
        compiler_params=_params(("arbitrary",)),
        name=name,
    )(*args)


FFN_MM_ROWS = 512
FFN_ROWS = 64
FFN_PAD = 8


def _ffn_kernel(x_ref, g_ref, mod_ref, wg_ref, wv_ref, cwg_ref, cwv_ref, cbg_ref, cbv_ref,
                wd_ref, o_ref, h_ref, u_ref, act_ref, wup_ref, wdn_ref):
    i = pl.program_id(0)
    c = pl.program_id(1)
    fc = FF_CHUNK

    @pl.when(c == 0)
    def _():
        _norm_mod_rows(x_ref, g_ref, mod_ref, h_ref, 3, 4)
        zeros = jnp.zeros((FFN_PAD, 2 * fc), F32)
        u_ref[0:FFN_PAD, :] = zeros
        u_ref[FFN_PAD + TOK_TILE:, :] = zeros
        o_ref[...] = jnp.zeros_like(o_ref)

    wup_ref[:, :fc] = wg_ref[...].astype(BF16)
    wup_ref[:, fc:] = wv_ref[...].astype(BF16)
    wdn_ref[...] = wd_ref[...].astype(BF16)
    cw_g, cw_v = cwg_ref[...], cwv_ref[...]
    cb_g, cb_v = cbg_ref[...], cbv_ref[...]
    seq_len = jnp.where(i < N_PROMPT // TOK_TILE, SEQ, DEC_SEQ)
    row = lax.broadcasted_iota(jnp.int32, (FFN_ROWS, 1), 0)

    def up(t):
        r0 = t * FFN_MM_ROWS
        u_ref[FFN_PAD + r0:FFN_PAD + r0 + FFN_MM_ROWS, :] = _bdot(
            h_ref[r0:r0 + FFN_MM_ROWS, :], wup_ref[...])

    def conv_act(t):
        for r0 in range(t * FFN_MM_ROWS, (t + 1) * FFN_MM_ROWS, FFN_ROWS):
            halves = []
            for lo, cw, cb in ((0, cw_g, cb_g), (fc, cw_v, cb_v)):
                p0 = FFN_PAD + r0
                prev = u_ref[p0 - 1:p0 - 1 + FFN_ROWS, lo:lo + fc]
                mid = u_ref[p0:p0 + FFN_ROWS, lo:lo + fc]
                nxt = u_ref[p0 + 1:p0 + 1 + FFN_ROWS, lo:lo + fc]
                if r0 % SEQ == 0:
                    prev = jnp.where(((r0 + row) & (seq_len - 1)) == 0, 0.0, prev)
                if (r0 + FFN_ROWS) % SEQ == 0:
                    nxt = jnp.where(((r0 + row) & (seq_len - 1)) == seq_len - 1, 0.0, nxt)
                halves.append(prev * cw[0:1] + mid * cw[1:2] + nxt * cw[2:3] + cb)
            act_ref[r0:r0 + FFN_ROWS, :] = (_silu(halves[0]) * halves[1]).astype(BF16)

    def down(t):
        r0 = t * FFN_MM_ROWS
        o_ref[r0:r0 + FFN_MM_ROWS, :] += _bdot(act_ref[r0:r0 + FFN_MM_ROWS, :], wdn_ref[...])

    n = TOK_TILE // FFN_MM_ROWS
    for s in range(n + 2):
        if s < n:
            up(s)
        if 1 <= s <= n:
            conv_act(s - 1)
        if s >= 2:
            down(s - 2)

    @pl.when(c == N_FF_CHUNKS - 1)
    def _():
        o_ref[...] = x_ref[...] + mod_ref[0, 5:6, :] * o_ref[...]


def _ffn(x, g, mods, w_up, conv_w, conv_b, w_down):
    fc = FF_CHUNK
    ncb = N_FF_CHUNKS
    return pl.pallas_call(
        _ffn_kernel,
        grid=(N_TOK_TILES, ncb),
        in_specs=[pl.BlockSpec((TOK_TILE, D_MODEL), lambda i, c: (i, 0)),
                  pl.BlockSpec((1, D_MODEL), lambda i, c: (0, 0)),
                  pl.BlockSpec((1, 6, D_MODEL), lambda i, c: (_mod_group_of_tile(i), 0, 0)),
                  pl.BlockSpec((D_MODEL, fc), lambda i, c: (0, c)),
                  pl.BlockSpec((D_MODEL, fc), lambda i, c: (0, ncb + c)),
                  pl.BlockSpec((3, fc), lambda i, c: (0, c)),
                  pl.BlockSpec((3, fc), lambda i, c: (0, ncb + c)),
                  pl.BlockSpec((1, fc), lambda i, c: (0, c)),
                  pl.BlockSpec((1, fc), lambda i, c: (0, ncb + c)),
                  pl.BlockSpec((fc, D_MODEL), lambda i, c: (c, 0))],
        out_specs=pl.BlockSpec((TOK_TILE, D_MODEL), lambda i, c: (i, 0)),
        out_shape=jax.ShapeDtypeStruct((N_TOK, D_MODEL), F32),
        scratch_shapes=[pltpu.VMEM((TOK_TILE, D_MODEL), BF16),
                        pltpu.VMEM((TOK_TILE + 2 * FFN_PAD, 2 * fc), F32),
                        pltpu.VMEM((TOK_TILE, fc), BF16),
                        pltpu.VMEM((D_MODEL, 2 * fc), BF16),
                        pltpu.VMEM((fc, D_MODEL), BF16)],
        compiler_params=_params(("arbitrary", "arbitrary")),
        name="conv_ffn",
    )(x, g.reshape(1, D_MODEL), mods, w_up, w_up, conv_w, conv_w,
      conv_b.reshape(1, -1), conv_b.reshape(1, -1), w_down)


NA_HB = 4
NA_LANES = NA_HB * NA_HD
NA_ROWS = DEC_SEQ // GRID_W
NA_KEYS = NA_WIN_R * GRID_W


def _na_prompt_kernel(q_ref, k_ref, v_ref, gq_ref, gk_ref, o_ref, kn_ref):
    scale = NA_HD ** -0.5
    q = _group_rms(q_ref[...], gq_ref[...], NA_HD) * scale
    k = _group_rms(k_ref[...], gk_ref[...], NA_HD)
    kn_ref[...] = k
    q4 = _stack_heads(q, NA_HB, NA_HD).astype(BF16)
    s = lax.dot_general(q4, k.astype(BF16), _NT, preferred_element_type=F32)
    (p,), inv = _softmax_parts([s])
    o4 = _bdot(p.astype(BF16), v_ref[...].astype(BF16)) * inv
    o_ref[...] = _unstack_heads(o4, NA_HB, NA_HD)


def _na_prompt(qkv, gq, gk):
    nb = NA_HEADS // NA_HB
    blk = lambda off: pl.BlockSpec((SEQ, NA_LANES), lambda b, j: (b, off + j))
    vec = pl.BlockSpec((1, NA_LANES), lambda b, j: (0, 0))
    return pl.pallas_call(
        _na_prompt_kernel,
        grid=(BATCH, nb),
        in_specs=[blk(0), blk(nb), blk(2 * nb), vec, vec],
        out_specs=[blk(0), blk(0)],
        out_shape=[jax.ShapeDtypeStruct((N_PROMPT, D_MODEL), F32)] * 2,
        compiler_params=_params(("arbitrary", "arbitrary")),
        name="na_prompt",
    )(qkv, qkv, qkv, jnp.tile(gq, NA_HB).reshape(1, -1), jnp.tile(gk, NA_HB).reshape(1, -1))


def _na_latent_kernel(q_ref, k_ref, v_ref, kc_ref, vc_ref, bias_ref, gq_ref, gk_ref, o_ref,
                      qn_ref, kn_ref, vb_ref, kc4_ref, vc4_ref):
    scale = NA_HD ** -0.5
    rows = 256

    def prep(r, carry):
        sl = pl.ds(pl.multiple_of(r * rows, rows), rows)
        qn_ref[sl, :] = (_group_rms(q_ref[sl, :], gq_ref[...], NA_HD) * scale).astype(BF16)
        kn_ref[sl, :] = _group_rms(k_ref[sl, :], gk_ref[...], NA_HD).astype(BF16)
        vb_ref[sl, :] = v_ref[sl, :].astype(BF16)
        return carry

    lax.fori_loop(0, DEC_SEQ // rows, prep, 0)
    kc4_ref[...] = kc_ref[0].astype(BF16)
    vc4_ref[...] = vc_ref[0].astype(BF16)

    def row(r, carry):
        kr0 = jnp.clip(r - NA_WIN_R // 2, 0, NA_ROWS - NA_WIN_R)
        pat = kr0 - r + NA_WIN_R - 1
        qs = pl.ds(pl.multiple_of(r * GRID_W, GRID_W), GRID_W)
        ks = pl.ds(pl.multiple_of(kr0 * GRID_W, GRID_W), NA_KEYS)
        q4 = _stack_heads(qn_ref[qs, :], NA_HB, NA_HD)
        s_loc = lax.dot_general(q4, kn_ref[ks, :], _NT, preferred_element_type=F32)
        bias = jnp.concatenate([bias_ref[h, pat] for h in range(NA_HB)], axis=0)
        s_loc = s_loc + bias
        s_ctx = lax.dot_general(q4, kc4_ref[...], _NT, preferred_element_type=F32)
        (p_loc, p_ctx), inv = _softmax_parts([s_loc, s_ctx])
        o4 = _bdot(p_loc.astype(BF16), vb_ref[ks, :]) + _bdot(p_ctx.astype(BF16), vc4_ref[...])
        o_ref[qs, :] = _unstack_heads(o4 * inv, NA_HB, NA_HD)
        return carry

    lax.fori_loop(0, NA_ROWS, row, 0)


def _na_bias_blocks(bias_table):
    qc = jnp.arange(GRID_W)[:, None]
    kc = jnp.arange(GRID_W)[None, :]
    win0 = jnp.clip(qc - NA_WIN_C // 2, 0, GRID_W - NA_WIN_C)
    valid = (kc >= win0) & (kc < win0 + NA_WIN_C)
    n_ro, n_co = bias_table.shape[1:]
    c = NA_WIN_C - 1
    period = jnp.concatenate(
        [bias_table[..., c:], jnp.zeros((NA_HEADS, n_ro, 2 * GRID_W - n_co), F32),
         bias_table[..., :c]], axis=-1)
    flat = jnp.tile(period, (1, 1, GRID_W))[..., :GRID_W * (2 * GRID_W - 1)]
    t = flat.reshape(NA_HEADS, n_ro, GRID_W, 2 * GRID_W - 1)[..., :GRID_W]
    t = jnp.where(valid, t, NEG)
    b = jnp.stack([t[:, p:p + NA_WIN_R] for p in range(NA_WIN_R)], axis=1)
    b = jnp.transpose(b, (0, 1, 3, 2, 4))
    return b.reshape(NA_HEADS, NA_WIN_R, GRID_W, NA_KEYS)


def _na_latent(qkv, cache_k, cache_v, bias_blocks, gq, gk):
    nb = NA_HEADS // NA_HB
    lat0 = N_PROMPT // DEC_SEQ
    blk = lambda off: pl.BlockSpec((DEC_SEQ, NA_LANES), lambda b, j: (lat0 + b, off + j))
    vec = pl.BlockSpec((1, NA_LANES), lambda b, j: (0, 0))
    cache = pl.BlockSpec((1, PAST_LEN, NA_LANES), lambda b, j: (b, 0, j))
    return pl.pallas_call(
        _na_latent_kernel,
        grid=(DEC_BATCH, nb),
        in_specs=[blk(0), blk(nb), blk(2 * nb), cache, cache,
                  pl.BlockSpec((NA_HB, NA_WIN_R, GRID_W, NA_KEYS), lambda b, j: (j, 0, 0, 0)),
                  vec, vec],
        out_specs=pl.BlockSpec((DEC_SEQ, NA_LANES), lambda b, j: (b, j)),
        out_shape=jax.ShapeDtypeStruct((N_LATENT, D_MODEL), F32),
        scratch_shapes=[pltpu.VMEM((DEC_SEQ, NA_LANES), BF16),
                        pltpu.VMEM((DEC_SEQ, NA_LANES), BF16),
                        pltpu.VMEM((DEC_SEQ, NA_LANES), BF16),
                        pltpu.VMEM((PAST_LEN, NA_LANES), BF16),
                        pltpu.VMEM((PAST_LEN, NA_LANES), BF16)],
        compiler_params=_params(("arbitrary", "arbitrary")),
        name="na_latent",
    )(qkv, qkv, qkv, _tokens_first(cache_k), _tokens_first(cache_v), bias_blocks,
      jnp.tile(gq, NA_HB).reshape(1, -1), jnp.tile(gk, NA_HB).reshape(1, -1))


GLA_C = 128
GLA_SUB = 16
GLA_LEVELS = (64, 32, 16)


def _split_hi_lo(x):
    hi = x.astype(BF16)
    lo = (x - hi.astype(F32)).astype(BF16)
    return jnp.concatenate([hi, lo], axis=1)


def _gla_chunk(q, k, v, g, st_ref, rev):
    c = GLA_C
    row = lax.broadcasted_iota(jnp.int32, (c, c), 0)
    col = lax.broadcasted_iota(jnp.int32, (c, c), 1)
    tri = (col >= row) if rev else (col <= row)
    cs = _bdot(jnp.where(tri, 1.0, 0.0).astype(BF16), _split_hi_lo(g))
    b = cs[:, :GLA_DK] + cs[:, GLA_DK:]
    causal = (col >= row) if rev else (col <= row)

    a = jnp.zeros((c, c), F32)
    rid = lax.broadcasted_iota(jnp.int32, (c, GLA_DK), 0)
    for m in GLA_LEVELS:
        nblk = c // (2 * m)
        if rev:
            bnd = [b[j * 2 * m + m:j * 2 * m + m + 1] for j in range(nblk)]
        else:
            bnd = [b[j * 2 * m + m - 1:j * 2 * m + m] for j in range(nblk)]
        ref = jnp.concatenate([jnp.broadcast_to(x, (2 * m, GLA_DK)) for x in bnd], axis=0)
        later = ((rid & m) == 0) if rev else ((rid & m) != 0)
        dq = jnp.minimum(b - ref, 0.0)
        dk_ = jnp.minimum(ref - b, 0.0)
        qh = jnp.where(later, q * jnp.exp(dq), 0.0).astype(BF16)
        kh = jnp.where(later, 0.0, k * jnp.exp(dk_)).astype(BF16)
        blk = lax.dot_general(qh, kh, _NT, preferred_element_type=F32)
        same = (row >> _log2(2 * m)) == (col >> _log2(2 * m))
        a = a + jnp.where(same, blk, 0.0)

    nsub = c // GLA_SUB
    lane_c = lax.broadcasted_iota(jnp.int32, (GLA_SUB, c), 1)
    srow = lax.broadcasted_iota(jnp.int32, (GLA_SUB, 1), 0)
    diag_rows = []
    for blk_i in range(nsub):
        r0 = blk_i * GLA_SUB
        qb = q[r0:r0 + GLA_SUB]
        bb = b[r0:r0 + GLA_SUB]
        acc = jnp.zeros((GLA_SUB, c), F32)
        for s in range(GLA_SUB):
            ks = k[r0 + s:r0 + s + 1]
            bs = b[r0 + s:r0 + s + 1]
            w = jnp.sum(qb * ks * jnp.exp(jnp.minimum(bb - bs, 0.0)), axis=-1, keepdims=True)
            keep = (srow <= s) if rev else (srow >= s)
            w = jnp.where(keep, w, 0.0)
            acc = jnp.where(lane_c == r0 + s, w, acc)
        diag_rows.append(acc)
    a = a + jnp.concatenate(diag_rows, axis=0)
    a = jnp.where(causal, a, 0.0)

    st = st_ref[...]
    inter = lax.dot_general((q * jnp.exp(b)).astype(BF16), st.astype(BF16), _NT,
                            preferred_element_type=F32)
    o = inter + _bdot(a.astype(BF16), v.astype(BF16))

    btot = b[0:1] if rev else b[c - 1:c]
    kd = (k * jnp.exp(btot - b)).astype(BF16)
    st_ref[...] = st * jnp.exp(btot) + lax.dot_general(v.astype(BF16), kd, _TN,
                                                       preferred_element_type=F32)
    return o


def _gla_kernel(*refs, n_tok, has_state):
    if has_state:
        (q_ref, k_ref, v_ref, r_ref, w2_ref, bg_ref, gn_ref, s0f_ref, s0b_ref,
         o_ref, lg_ref, of_ref, ob_ref, stf_ref, stb_ref) = refs
    else:
        (q_ref, k_ref, v_ref, r_ref, w2_ref, bg_ref, gn_ref,
         o_ref, sf_ref, sb_ref, lg_ref, of_ref, ob_ref, stf_ref, stb_ref) = refs
    nc = n_tok // GLA_C
    scale = GLA_DK ** -0.5

    rb = r_ref[...].astype(BF16)
    for z in range(2):
        x = _bdot(rb, w2_ref[z].astype(BF16)) + bg_ref[z]
        lg_ref[z] = (jnp.minimum(x, 0.0) - jnp.log1p(jnp.exp(-jnp.abs(x)))) * (1.0 / GLA_GATE_NORM)

    if has_state:
        stf_ref[...] = s0f_ref[0, 0]
        stb_ref[...] = s0b_ref[0, 0]
    else:
        stf_ref[...] = jnp.zeros_like(stf_ref)
        stb_ref[...] = jnp.zeros_like(stb_ref)

    def step(ci, carry):
        for rev in (False, True):
            cc = (nc - 1 - ci) if rev else ci
            sl = pl.ds(pl.multiple_of(cc * GLA_C, GLA_C), GLA_C)
            q = q_ref[sl, :] * scale
            o = _gla_chunk(q, k_ref[sl, :], v_ref[sl, :], lg_ref[1 if rev else 0, sl, :],
                           stb_ref if rev else stf_ref, rev)
            (ob_ref if rev else of_ref)[sl, :] = o
        return carry

    lax.fori_loop(0, nc, step, 0)

    o = of_ref[...] + ob_ref[...]
    o_ref[...] = o * lax.rsqrt(jnp.mean(o * o, axis=-1, keepdims=True) + EPS) * gn_ref[...]
    if not has_state:
        sf_ref[0, 0] = stf_ref[...]
        sb_ref[0, 0] = stb_ref[...]


def _gla(proj, w2, bg, gnorm, n_seq, n_tok, row_block0, states=None):
    qb = GLA_HK // GLA_DK
    spec = lambda width, off: pl.BlockSpec((n_tok, width), lambda b, h: (row_block0 + b, off + h))
    in_specs = [spec(GLA_DK, 0), spec(GLA_DK, qb), spec(GLA_DV, 2 * GLA_HK // GLA_DV),
                pl.BlockSpec((n_tok, 128), lambda b, h: (row_block0 + b, (2 * GLA_HK + 2 * GLA_HV) // 128)),
                pl.BlockSpec((2, 128, GLA_DK), lambda b, h: (0, 0, h)),
                pl.BlockSpec((2, 1, GLA_DK), lambda b, h: (0, 0, h)),
                pl.BlockSpec((1, GLA_DV), lambda b, h: (0, 0))]
    args = [proj, proj, proj, proj, w2, bg, gnorm.reshape(1, GLA_DV)]
    st_spec = pl.BlockSpec((1, 1, GLA_DV, GLA_DK), lambda b, h: (b, h, 0, 0))
    o_spec = pl.BlockSpec((n_tok, GLA_DV), lambda b, h: (b, h))
    o_shape = jax.ShapeDtypeStruct((n_seq * n_tok, GLA_HV), F32)
    if states is not None:
        in_specs += [st_spec, st_spec]
        args += list(states)
        out_specs, out_shape = o_spec, o_shape
    else:
        st_shape = jax.ShapeDtypeStruct((n_seq, GLA_HEADS, GLA_DV, GLA_DK), F32)
        out_specs, out_shape = [o_spec, st_spec, st_spec], [o_shape, st_shape, st_shape]
    return pl.pallas_call(
        functools.partial(_gla_kernel, n_tok=n_tok, has_state=states is not None),
        grid=(n_seq, GLA_HEADS),
        in_specs=in_specs,
        out_specs=out_specs,
        out_shape=out_shape,
        scratch_shapes=[pltpu.VMEM((2, n_tok, GLA_DK), F32),
                        pltpu.VMEM((n_tok, GLA_DV), F32),
                        pltpu.VMEM((n_tok, GLA_DV), F32),
                        pltpu.VMEM((GLA_DV, GLA_DK), F32),
                        pltpu.VMEM((GLA_DV, GLA_DK), F32)],
        compiler_params=_params(("arbitrary", "arbitrary")),
        name="gla_latent" if states is not None else "gla_prompt",
    )(*args)


DIFF_HB = 2
DIFF_QL = DIFF_HB * DIFF_HD
DIFF_VL = DIFF_HB * 2 * DIFF_HD
DIFF_TQ = 256
DIFF_SUB = 128


def _diff_lambda(lam_ref):
    l = lam_ref[...]
    a = jnp.sum(l[0:1] * l[1:2], axis=-1, keepdims=True)
    b = jnp.sum(l[2:3] * l[3:4], axis=-1, keepdims=True)
    return jnp.exp(a) - jnp.exp(b) + DIFF_LAMBDA_INIT


def _diff_finish(a, v, sn_ref):
    o2 = _bdot(a.astype(BF16), v)
    o = _unstack_heads(o2, DIFF_HB, 2 * DIFF_HD)
    return _group_rms(o, sn_ref[...], 2 * DIFF_HD) * (1.0 - DIFF_LAMBDA_INIT)


def _diff_prompt_kernel(q0_ref, q1_ref, k0_ref, k1_ref, v_ref, gq_ref, gk_ref, lam_ref, sn_ref,
                        o_ref, kn0_ref, kn1_ref):
    scale = DIFF_HD ** -0.5
    lam = _diff_lambda(lam_ref)
    ps = []
    for q_ref, k_ref, kn_ref in ((q0_ref, k0_ref, kn0_ref), (q1_ref, k1_ref, kn1_ref)):
        q = _group_rms(q_ref[...], gq_ref[...], DIFF_HD) * scale
        k = _group_rms(k_ref[...], gk_ref[...], DIFF_HD)
        kn_ref[...] = k
        q2 = _stack_heads(q, DIFF_HB, DIFF_HD).astype(BF16)
        s = lax.dot_general(q2, k.astype(BF16), _NT, preferred_element_type=F32)
        (p,), inv = _softmax_parts([s])
        ps.append(p * inv)
    o_ref[...] = _diff_finish(ps[0] - lam * ps[1], v_ref[...].astype(BF16), sn_ref)


def _diff_prompt(qkv, gq, gk, lam, sub_norm):
    nb = DIFF_HEADS // DIFF_HB
    qk = lambda off: pl.BlockSpec((SEQ, DIFF_QL), lambda b, j: (b, off + j))
    vec = lambda n: pl.BlockSpec((1, n), lambda b, j: (0, 0))
    v_spec = pl.BlockSpec((SEQ, DIFF_VL), lambda b, j: (b, 2 * D_MODEL // DIFF_VL + j))
    kn_shape = jax.ShapeDtypeStruct((N_PROMPT, DIFF_HEADS * DIFF_HD), F32)
    return pl.pallas_call(
        _diff_prompt_kernel,
        grid=(BATCH, nb),
        in_specs=[qk(0), qk(nb), qk(2 * nb), qk(3 * nb), v_spec, vec(DIFF_QL), vec(DIFF_QL),
                  pl.BlockSpec((4, DIFF_HD), lambda b, j: (0, 0)), vec(DIFF_VL)],
        out_specs=[pl.BlockSpec((SEQ, DIFF_VL), lambda b, j: (b, j)), qk(0), qk(0)],
        out_shape=[jax.ShapeDtypeStruct((N_PROMPT, D_MODEL), F32), kn_shape, kn_shape],
        compiler_params=_params(("arbitrary", "arbitrary")),
        name="diff_prompt",
    )(qkv, qkv, qkv, qkv, qkv, jnp.tile(gq, DIFF_HB).reshape(1, -1),
      jnp.tile(gk, DIFF_HB).reshape(1, -1), lam, jnp.tile(sub_norm, DIFF_HB).reshape(1, -1))


def _diff_latent_kernel(q0_ref, q1_ref, k0_ref, k1_ref, v_ref, kc0_ref, kc1_ref, vc_ref,
                        cos_ref, sin_ref, cosq_ref, sinq_ref, gq_ref, gk_ref, lam_ref, sn_ref,
                        o_ref, kb0_ref, kb1_ref, vb_ref):
    scale = DIFF_HD ** -0.5
    half = DIFF_HD // 4
    rows = 256

    @pl.when(pl.program_id(2) == 0)
    def _():
        for k_ref, kc_ref, kb_ref in ((k0_ref, kc0_ref, kb0_ref), (k1_ref, kc1_ref, kb1_ref)):
            kb_ref[0:PAST_LEN, :] = kc_ref[0].astype(BF16)

            def prep(r, carry):
                sl = pl.ds(pl.multiple_of(r * rows, rows), rows)
                k = _group_rms(k_ref[sl, :], gk_ref[...], DIFF_HD)
                k = _rope(k, cos_ref[sl, :], sin_ref[sl, :], half)
                kb_ref[pl.ds(pl.multiple_of(PAST_LEN + r * rows, rows), rows), :] = k.astype(BF16)
                return carry

            lax.fori_loop(0, DEC_SEQ // rows, prep, 0)
        vb_ref[0:PAST_LEN, :] = vc_ref[0].astype(BF16)
        vb_ref[PAST_LEN:, :] = v_ref[...].astype(BF16)

    lam = _diff_lambda(lam_ref)
    for r0 in range(0, DIFF_TQ, DIFF_SUB):
        sl = slice(r0, r0 + DIFF_SUB)
        ps = []
        for q_ref, kb_ref in ((q0_ref, kb0_ref), (q1_ref, kb1_ref)):
            q = _group_rms(q_ref[sl, :], gq_ref[...], DIFF_HD)
            q = _rope(q, cosq_ref[sl, :], sinq_ref[sl, :], half) * scale
            q2 = _stack_heads(q, DIFF_HB, DIFF_HD).astype(BF16)
            s = lax.dot_general(q2, kb_ref[...], _NT, preferred_element_type=F32)
            (p,), inv = _softmax_parts([s])
            ps.append(p * inv)
        o_ref[sl, :] = _diff_finish(ps[0] - lam * ps[1], vb_ref[...], sn_ref)


def _diff_latent(qkv, cache_k, cache_v, cos, sin, gq, gk, lam, sub_norm):
    nb = DIFF_HEADS // DIFF_HB
    nq = DEC_SEQ // DIFF_TQ
    q0 = N_PROMPT // DIFF_TQ
    lat0 = N_PROMPT // DEC_SEQ
    n_keys = PAST_LEN + DEC_SEQ
    q_spec = lambda off: pl.BlockSpec((DIFF_TQ, DIFF_QL), lambda b, j, t: (q0 + b * nq + t, off + j))
    k_spec = lambda off: pl.BlockSpec((DEC_SEQ, DIFF_QL), lambda b, j, t: (lat0 + b, off + j))
    v_spec = pl.BlockSpec((DEC_SEQ, DIFF_VL), lambda b, j, t: (lat0 + b, 2 * D_MODEL // DIFF_VL + j))
    kc_spec = lambda off: pl.BlockSpec((1, PAST_LEN, DIFF_QL), lambda b, j, t: (b, 0, off + j))
    vc_spec = pl.BlockSpec((1, PAST_LEN, DIFF_VL), lambda b, j, t: (b, 0, j))
    tab = pl.BlockSpec((DEC_SEQ, DIFF_QL), lambda b, j, t: (0, 0))
    tabq = pl.BlockSpec((DIFF_TQ, DIFF_QL), lambda b, j, t: (t, 0))
    vec = lambda n: pl.BlockSpec((1, n), lambda b, j, t: (0, 0))
    return pl.pallas_call(
        _diff_latent_kernel,
        grid=(DEC_BATCH, nb, nq),
        in_specs=[q_spec(0), q_spec(nb), k_spec(2 * nb), k_spec(3 * nb), v_spec,
                  kc_spec(0), kc_spec(nb), vc_spec, tab, tab, tabq, tabq,
                  vec(DIFF_QL), vec(DIFF_QL),
                  pl.BlockSpec((4, DIFF_HD), lambda b, j, t: (0, 0)), vec(DIFF_VL)],
        out_specs=pl.BlockSpec((DIFF_TQ, DIFF_VL), lambda b, j, t: (b * nq + t, j)),
        out_shape=jax.ShapeDtypeStruct((N_LATENT, D_MODEL), F32),
        scratch_shapes=[pltpu.VMEM((n_keys, DIFF_QL), BF16),
                        pltpu.VMEM((n_keys, DIFF_QL), BF16),
                        pltpu.VMEM((n_keys, DIFF_VL), BF16)],
        compiler_params=_params(("arbitrary", "arbitrary", "arbitrary")),
        name="diff_latent",
    )(qkv, qkv, qkv, qkv, qkv, _tokens_first(cache_k), _tokens_first(cache_k),
      _tokens_first(cache_v), cos, sin, cos, sin,
      jnp.tile(gq, DIFF_HB).reshape(1, -1), jnp.tile(gk, DIFF_HB).reshape(1, -1), lam,
      jnp.tile(sub_norm, DIFF_HB).reshape(1, -1))


MLA_HB = 2
MLA_HL = 128
MLA_LANES = MLA_HB * MLA_HL
MLA_TQ = 512
MLA_SUB = 128


def _mla_keys(kv, kr, gk):
    lane = lax.broadcasted_iota(jnp.int32, kv.shape, 1)
    kr2 = jnp.concatenate([kr] * MLA_HB, axis=1)
    k = jnp.where((lane & (MLA_HL - 1)) < MLA_NOPE, kv, kr2)
    return _group_rms(k, gk, MLA_HL, n_real=MLA_QK)


def _mla_out(o2):
    tq = o2.shape[0] // MLA_HB
    oa = pltpu.roll(o2[0:tq, 0:MLA_HL], MLA_HL - MLA_V, axis=1)
    ob = o2[tq:, MLA_HL:]
    lane = lax.broadcasted_iota(jnp.int32, oa.shape, 1)
    return jnp.where(lane < MLA_V, oa, ob)


def _mla_prompt_kernel(q_ref, kv_ref, kr_ref, gq_ref, gk_ref, o_ref):
    scale = MLA_QK ** -0.5
    q = _group_rms(q_ref[...], gq_ref[...], MLA_HL, n_real=MLA_QK) * scale
    kv = kv_ref[...]
    k = _mla_keys(kv, kr_ref[...], gk_ref[...])
    q2 = _stack_heads(q, MLA_HB, MLA_HL).astype(BF16)
    s = lax.dot_general(q2, k.astype(BF16), _NT, preferred_element_type=F32)
    (p,), inv = _softmax_parts([s])
    o_ref[...] = _mla_out(_bdot(p.astype(BF16), kv.astype(BF16)) * inv)


def _mla_prompt(qp, kvp, low, gq, gk):
    nb = MLA_HEADS // MLA_HB
    blk = pl.BlockSpec((SEQ, MLA_LANES), lambda b, j: (b, j))
    vec = pl.BlockSpec((1, MLA_LANES), lambda b, j: (0, 0))
    return pl.pallas_call(
        _mla_prompt_kernel,
        grid=(BATCH, nb),
        in_specs=[blk, blk, pl.BlockSpec((SEQ, MLA_HL), lambda b, j: (b, MLA_LOW_KR // MLA_HL)),
                  vec, vec],
        out_specs=pl.BlockSpec((SEQ, MLA_HB * MLA_V), lambda b, j: (b, j)),
        out_shape=jax.ShapeDtypeStruct((N_PROMPT, MLA_HEADS * MLA_V), F32),
        compiler_params=_params(("arbitrary", "arbitrary")),
        name="mla_prompt",
    )(qp, kvp, low, gq, gk)


def _mla_latent_kernel(q_ref, kv_ref, kr_ref, kvc_ref, krc_ref, cos_ref, sin_ref, cosq_ref,
                       sinq_ref, gq_ref, gk_ref, o_ref, kb_ref, vb_ref):
    scale = MLA_QK ** -0.5
    half = MLA_ROPE // 4
    rows = 256

    @pl.when(pl.program_id(2) == 0)
    def _():
        kvc = kvc_ref[...]
        kb_ref[0:PAST_LEN, :] = _mla_keys(kvc, krc_ref[...], gk_ref[...]).astype(BF16)
        vb_ref[0:PAST_LEN, :] = kvc.astype(BF16)

        def prep(r, carry):
            sl = pl.ds(pl.multiple_of(r * rows, rows), rows)
            dst = pl.ds(pl.multiple_of(PAST_LEN + r * rows, rows), rows)
            kv = kv_ref[sl, :]
            k = _mla_keys(kv, kr_ref[sl, :], gk_ref[...])
            k = _rope(k, cos_ref[sl, :], sin_ref[sl, :], half)
            kb_ref[dst, :] = k.astype(BF16)
            vb_ref[dst, :] = kv.astype(BF16)
            return carry

        lax.fori_loop(0, DEC_SEQ // rows, prep, 0)

    for r0 in range(0, MLA_TQ, MLA_SUB):
        sl = slice(r0, r0 + MLA_SUB)
        q = _group_rms(q_ref[sl, :], gq_ref[...], MLA_HL, n_real=MLA_QK)
        q = _rope(q, cosq_ref[sl, :], sinq_ref[sl, :], half) * scale
        q2 = _stack_heads(q, MLA_HB, MLA_HL).astype(BF16)
        s = lax.dot_general(q2, kb_ref[...], _NT, preferred_element_type=F32)
        (p,), inv = _softmax_parts([s])
        o_ref[sl, :] = _mla_out(_bdot(p.astype(BF16), vb_ref[...]) * inv)


def _mla_latent(qp, kvp, low, kvc, krc, cos, sin, gq, gk):
    nb = MLA_HEADS // MLA_HB
    nq = DEC_SEQ // MLA_TQ
    q0 = N_PROMPT // MLA_TQ
    lat0 = N_PROMPT // DEC_SEQ
    n_keys = PAST_LEN + DEC_SEQ
    tab = pl.BlockSpec((DEC_SEQ, MLA_LANES), lambda b, j, t: (0, 0))
    tabq = pl.BlockSpec((MLA_TQ, MLA_LANES), lambda b, j, t: (t, 0))
    vec = pl.BlockSpec((1, MLA_LANES), lambda b, j, t: (0, 0))
    return pl.pallas_call(
        _mla_latent_kernel,
        grid=(DEC_BATCH, nb, nq),
        in_specs=[pl.BlockSpec((MLA_TQ, MLA_LANES), lambda b, j, t: (q0 + b * nq + t, j)),
                  pl.BlockSpec((DEC_SEQ, MLA_LANES), lambda b, j, t: (lat0 + b, j)),
                  pl.BlockSpec((DEC_SEQ, MLA_HL), lambda b, j, t: (lat0 + b, MLA_LOW_KR // MLA_HL)),
                  pl.BlockSpec((PAST_LEN, MLA_LANES), lambda b, j, t: (b, j)),
                  pl.BlockSpec((PAST_LEN, MLA_HL), lambda b, j, t: (b, 0)),
                  tab, tab, tabq, tabq, vec, vec],
        out_specs=pl.BlockSpec((MLA_TQ, MLA_HB * MLA_V), lambda b, j, t: (b * nq + t, j)),
        out_shape=jax.ShapeDtypeStruct((N_LATENT, MLA_HEADS * MLA_V), F32),
        scratch_shapes=[pltpu.VMEM((n_keys, MLA_LANES), BF16),
                        pltpu.VMEM((n_keys, MLA_LANES), BF16)],
        compiler_params=_params(("arbitrary", "arbitrary", "arbitrary")),
        name="mla_latent",
    )(qp, kvp, low, kvc, krc, cos, sin, cos, sin, gq, gk)


MLA_LOW_Q = 0
MLA_LOW_KV = 512
MLA_LOW_KR = 768
MLA_LOW_N = 896


def _axial_tables(n_tok, rdim):
    nf = rdim // 4
    freqs = ROPE_BASE ** (-jnp.arange(nf, dtype=F32) / nf)
    t = jnp.arange(n_tok)
    rowp = (t // GRID_W).astype(F32)
    colp = (t % GRID_W).astype(F32)
    ang = jnp.stack([rowp[:, None] * freqs, colp[:, None] * freqs], axis=1)
    cos, sin = jnp.cos(ang), jnp.sin(ang)
    cos_l = jnp.stack([cos, cos], axis=2).reshape(n_tok, rdim)
    sin_l = jnp.stack([-sin, sin], axis=2).reshape(n_tok, rdim)
    return cos_l, sin_l


def _diff_rope_tables():
    cos, sin = _axial_tables(DEC_SEQ, DIFF_HD)
    return jnp.tile(cos, (1, DIFF_HB)), jnp.tile(sin, (1, DIFF_HB))


def _mla_rope_tables():
    cos, sin = _axial_tables(DEC_SEQ, MLA_ROPE)
    ones = jnp.ones((DEC_SEQ, MLA_NOPE), F32)
    pad1 = jnp.ones((DEC_SEQ, MLA_HL - MLA_QK), F32)
    cos_h = jnp.concatenate([ones, cos, pad1], axis=1)
    sin_h = jnp.concatenate([0 * ones, sin, 0 * pad1], axis=1)
    return jnp.tile(cos_h, (1, MLA_HB)), jnp.tile(sin_h, (1, MLA_HB))


def _heads_first(x, n_seq, n_tok, heads, hd):
    return jnp.transpose(x.reshape(n_seq, n_tok, heads, hd), (0, 2, 1, 3))


def _tokens_first(cache):
    b, h, l, d = cache.shape
    return jnp.transpose(cache, (0, 2, 1, 3)).reshape(b, l, h * d)


def _pad_heads(w, heads, hd, hl):
    k = w.shape[0]
    return jnp.pad(w.reshape(k, heads, hd), ((0, 0), (0, 0), (0, hl - hd))).reshape(k, heads * hl)


def kernel(x_prompt, x_sample, cache_l0_k, cache_l0_v, state_l1_fwd, state_l1_bwd, cache_l2_k,
           cache_l2_v, cache_l3_ckv, cache_l3_krope, c, c_ctx, ada_w, ada_b, norm_mix, norm_ffn,
           ffn_w_up, ffn_conv_w, ffn_conv_b, ffn_w_down, na_w_qkv, na_q_norm, na_k_norm, na_bias,
           na_w_o, gla_w_qkvg, gla_w_gate1, gla_w_gate2, gla_b_gate, gla_o_norm, gla_w_o,
           diff_w_qkv, diff_q_norm, diff_k_norm, diff_lambda, diff_sub_norm, diff_w_o, mla_w_dq,
           mla_q_a_norm, mla_w_uq, mla_w_dkv, mla_kv_a_norm, mla_w_ukv, mla_q_norm, mla_k_norm,
           mla_w_o):
    x = jnp.concatenate([x_prompt.reshape(N_PROMPT, D_MODEL),
                         x_sample.reshape(N_LATENT, D_MODEL)], axis=0)
    cvecs = jnp.concatenate([c_ctx[None], c, jnp.zeros((5, D_MODEL), F32)], axis=0)
    mods_all = _ada_mods(cvecs, ada_w, ada_b)

    mods = mods_all[0]
    qkv = _norm_mod_proj(x, norm_mix[0], mods, na_w_qkv, 512, "na_qkv")
    o_p, kn_p = _na_prompt(qkv, na_q_norm, na_k_norm)
    o_s = _na_latent(qkv, cache_l0_k, cache_l0_v, _na_bias_blocks(na_bias), na_q_norm, na_k_norm)
    new_l0_k = _heads_first(kn_p, BATCH, SEQ, NA_HEADS, NA_HD)
    new_l0_v = _heads_first(qkv[:N_PROMPT, 2 * D_MODEL:], BATCH, SEQ, NA_HEADS, NA_HD)
    x = _out_proj_residual(x, jnp.concatenate([o_p, o_s], axis=0), mods, na_w_o, "na_out")
    x = _ffn(x, norm_ffn[0], mods, ffn_w_up[0], ffn_conv_w[0], ffn_conv_b[0], ffn_w_down[0])

    mods = mods_all[1]
    n_gla = 2 * GLA_HK + 2 * GLA_HV
    w_cat = jnp.concatenate(
        [gla_w_qkvg, gla_w_gate1[0], gla_w_gate1[1],
         jnp.zeros((D_MODEL, 128 - 2 * GLA_GATE_RANK), F32)], axis=1)
    proj = _norm_mod_proj(x, norm_mix[1], mods, w_cat, 640, "gla_proj")
    w2 = jnp.zeros((2, 128, GLA_HK), F32)
    w2 = w2.at[0, :GLA_GATE_RANK].set(gla_w_gate2[0])
    w2 = w2.at[1, GLA_GATE_RANK:2 * GLA_GATE_RANK].set(gla_w_gate2[1])
    bg = gla_b_gate.reshape(2, 1, GLA_HK)
    o_p, st_f, st_b = _gla(proj, w2, bg, gla_o_norm, BATCH, SEQ, 0)
    s0 = (jnp.swapaxes(state_l1_fwd, 2, 3), jnp.swapaxes(state_l1_bwd, 2, 3))
    o_s = _gla(proj, w2, bg, gla_o_norm, DEC_BATCH, DEC_SEQ, N_PROMPT // DEC_SEQ, states=s0)
    new_l1_fwd = jnp.swapaxes(st_f, 2, 3)
    new_l1_bwd = jnp.swapaxes(st_b, 2, 3)
    x = _out_proj_residual(x, jnp.concatenate([o_p, o_s], axis=0), mods, gla_w_o, "gla_out",
                           gate=proj, gate_col_block=(2 * GLA_HK + GLA_HV) // GLA_HV)
    x = _ffn(x, norm_ffn[1], mods, ffn_w_up[1], ffn_conv_w[1], ffn_conv_b[1], ffn_w_down[1])

    mods = mods_all[2]
    qkv = _norm_mod_proj(x, norm_mix[2], mods, diff_w_qkv, 512, "diff_qkv")
    o_p, kn0, kn1 = _diff_prompt(qkv, diff_q_norm, diff_k_norm, diff_lambda, diff_sub_norm)
    kn_p = jnp.concatenate([kn0, kn1], axis=1)
    cos_d, sin_d = _diff_rope_tables()
    o_s = _diff_latent(qkv, cache_l2_k, cache_l2_v, cos_d, sin_d, diff_q_norm, diff_k_norm,
                       diff_lambda, diff_sub_norm)
    new_l2_k = _heads_first(kn_p, BATCH, SEQ, 2 * DIFF_HEADS, DIFF_HD)
    new_l2_v = _heads_first(qkv[:N_PROMPT, 2 * D_MODEL:], BATCH, SEQ, DIFF_HEADS, 2 * DIFF_HD)
    x = _out_proj_residual(x, jnp.concatenate([o_p, o_s], axis=0), mods, diff_w_o, "diff_out")
    x = _ffn(x, norm_ffn[2], mods, ffn_w_up[2], ffn_conv_w[2], ffn_conv_b[2], ffn_w_down[2])

    mods = mods_all[3]
    zc = lambda n: jnp.zeros((D_MODEL, n), F32)
    w_low = jnp.concatenate(
        [mla_w_dq, zc(MLA_LOW_KV - MLA_Q_RANK), mla_w_dkv[:, :MLA_KV_RANK],
         zc(MLA_NOPE), mla_w_dkv[:, MLA_KV_RANK:], zc(MLA_HL - MLA_QK)], axis=1)
    low = _norm_mod_proj(x, norm_mix[3], mods, w_low, MLA_LOW_N, "mla_down")
    w_uq = _pad_heads(mla_w_uq, MLA_HEADS, MLA_QK, MLA_HL)
    qp, _ = _rms_matmul(low, 0, MLA_Q_RANK, mla_q_a_norm, w_uq, True, "mla_uq")
    kvp, ckv = _rms_matmul(low, MLA_LOW_KV // MLA_KV_RANK, MLA_KV_RANK, mla_kv_a_norm, mla_w_ukv,
                           True, "mla_ukv")
    kvc, _ = _rms_matmul(cache_l3_ckv.reshape(DEC_BATCH * PAST_LEN, MLA_KV_RANK), 0, MLA_KV_RANK,
                         mla_kv_a_norm, mla_w_ukv, False, "mla_ukv_cache")
    krc = jnp.pad(cache_l3_krope.reshape(DEC_BATCH * PAST_LEN, MLA_ROPE),
                  ((0, 0), (MLA_NOPE, MLA_HL - MLA_QK)))
    pad_gain = lambda g: jnp.tile(jnp.pad(g, (0, MLA_HL - MLA_QK)), MLA_HB).reshape(1, -1)
    gq, gk = pad_gain(mla_q_norm), pad_gain(mla_k_norm)
    o_p = _mla_prompt(qp, kvp, low, gq, gk)
    cos_m, sin_m = _mla_rope_tables()
    o_s = _mla_latent(qp, kvp, low, kvc, krc, cos_m, sin_m, gq, gk)
    new_l3_ckv = ckv[:N_PROMPT].reshape(BATCH, SEQ, MLA_KV_RANK)
    new_l3_krope = low[:N_PROMPT, MLA_LOW_KR + MLA_NOPE:MLA_LOW_KR + MLA_QK].reshape(
        BATCH, SEQ, MLA_ROPE)
    x = _out_proj_residual(x, jnp.concatenate([o_p, o_s], axis=0), mods, mla_w_o, "mla_out")
    x = _ffn(x, norm_ffn[3], mods, ffn_w_up[3], ffn_conv_w[3], ffn_conv_b[3], ffn_w_down[3])

    y_prompt = x[:N_PROMPT].reshape(BATCH, SEQ, D_MODEL)
    y_sample = x[N_PROMPT:].reshape(DEC_BATCH, DEC_SEQ, D_MODEL)
    return (y_prompt, y_sample, new_l0_k, new_l0_v, new_l1_fwd, new_l1_bwd, new_l2_k, new_l2_v,
            new_l3_ckv, new_l3_krope)
```

```python
import functools
import math
from typing import NamedTuple

import jax
import jax.numpy as jnp
from jax import lax
from jax.experimental import pallas as pl
from jax.experimental.pallas import tpu as pltpu

F32 = jnp.float32
BF16 = jnp.bfloat16

D_MODEL = 1024
BATCH = 16
SEQ = 256
DEPTH = 4
DEC_BATCH = 2
DEC_SEQ = 2048
PAST_LEN = 256
GRID_W = 64
D_FF = 2816
EPS = 1e-6
ROPE_BASE = 10000.0

NA_HEADS = 16
NA_HD = 64
NA_WIN_R = 8
NA_WIN_C = 16

GLA_HEADS = 4
GLA_DK = 128
GLA_DV = 256
GLA_HK = GLA_HEADS * GLA_DK
GLA_HV = GLA_HEADS * GLA_DV
GLA_GATE_RANK = 16
GLA_GATE_NORM = 16.0

DIFF_HEADS = 8
DIFF_HD = 64
DIFF_LAMBDA_INIT = 0.8 - 0.6 * math.exp(-0.3 * 2)

MLA_HEADS = 16
MLA_Q_RANK = 384
MLA_KV_RANK = 256
MLA_NOPE = 64
MLA_ROPE = 32
MLA_V = 64
MLA_QK = MLA_NOPE + MLA_ROPE

N_PROMPT = BATCH * SEQ
N_LATENT = DEC_BATCH * DEC_SEQ
N_TOK = N_PROMPT + N_LATENT
TOK_TILE = 2048
N_TOK_TILES = N_TOK // TOK_TILE
FF_CHUNK = 256
N_FF_CHUNKS = D_FF // FF_CHUNK
NEG = -1e30

VMEM_LIMIT = 56 * 1024 * 1024

_NT = (((1,), (1,)), ((), ()))
_TN = (((0,), (0,)), ((), ()))


def _params(sem, vmem=VMEM_LIMIT):
    return pltpu.CompilerParams(dimension_semantics=sem, vmem_limit_bytes=vmem)


def _log2(n):
    assert n & (n - 1) == 0
    return n.bit_length() - 1


def _silu(x):
    return x / (1.0 + jnp.exp(-x))


def _bdot(a, b):
    return jnp.dot(a, b, preferred_element_type=F32)


def _softmax_parts(parts):
    m = parts[0].max(axis=-1, keepdims=True)
    for s in parts[1:]:
        m = jnp.maximum(m, s.max(axis=-1, keepdims=True))
    ps = [jnp.exp(s - m) for s in parts]
    l = ps[0].sum(axis=-1, keepdims=True)
    for p in ps[1:]:
        l = l + p.sum(axis=-1, keepdims=True)
    return ps, 1.0 / l


def _group_rms(x, gain, group, n_real=None, sums_on_mxu=False):
    lanes = x.shape[-1]
    n_real = n_real or group
    x2 = x * x
    if group == lanes:
        ms = jnp.sum(x2, axis=-1, keepdims=True)
    elif not sums_on_mxu:
        gid = lax.broadcasted_iota(jnp.int32, x.shape, 1) >> _log2(group)
        ms = jnp.zeros_like(x)
        for i in range(lanes // group):
            sel = gid == i
            si = jnp.sum(jnp.where(sel, x2, 0.0), axis=-1, keepdims=True)
            ms = jnp.where(sel, si, ms)
    else:
        r = lax.broadcasted_iota(jnp.int32, (lanes, lanes), 0) >> _log2(group)
        c = lax.broadcasted_iota(jnp.int32, (lanes, lanes), 1) >> _log2(group)
        ones = jnp.where(r == c, 1.0, 0.0).astype(BF16)
        hi = x2.astype(BF16)
        lo = (x2 - hi.astype(F32)).astype(BF16)
        ms = _bdot(hi, ones) + _bdot(lo, ones)
    return x * lax.rsqrt(ms * (1.0 / n_real) + EPS) * gain


def _rope(x, cos, sin, half):
    lanes = x.shape[-1]
    lane = lax.broadcasted_iota(jnp.int32, x.shape, 1)
    up = pltpu.roll(x, lanes - half, axis=1)
    dn = pltpu.roll(x, half, axis=1)
    swapped = jnp.where((lane & (2 * half - 1)) < half, up, dn)
    return x * cos + swapped * sin


def _stack_heads(q, n_heads, head_lanes):
    hid = lax.broadcasted_iota(jnp.int32, q.shape, 1) >> _log2(head_lanes)
    zero = jnp.zeros_like(q)
    return jnp.concatenate([jnp.where(hid == i, q, zero) for i in range(n_heads)], axis=0)


def _unstack_heads(o, n_heads, head_lanes):
    rows = o.shape[0] // n_heads
    hid = lax.broadcasted_iota(jnp.int32, (rows, o.shape[1]), 1) >> _log2(head_lanes)
    out = o[0:rows]
    for i in range(1, n_heads):
        out = jnp.where(hid == i, o[i * rows:(i + 1) * rows], out)
    return out


ADA_TN = 1536


def _ada_kernel(c_ref, w_ref, b_ref, o_ref):
    s = _silu(c_ref[...])
    o_ref[0] = jnp.dot(s, w_ref[0], preferred_element_type=F32,
                       precision=lax.Precision.HIGHEST) + b_ref[0]


def _ada_mods(cvecs, ada_w, ada_b):
    out = pl.pallas_call(
        _ada_kernel,
        grid=(DEPTH, 6 * D_MODEL // ADA_TN),
        in_specs=[pl.BlockSpec((8, D_MODEL), lambda l, j: (0, 0)),
                  pl.BlockSpec((1, D_MODEL, ADA_TN), lambda l, j: (l, 0, j)),
                  pl.BlockSpec((1, 1, ADA_TN), lambda l, j: (l, 0, j))],
        out_specs=pl.BlockSpec((1, 8, ADA_TN), lambda l, j: (l, 0, j)),
        out_shape=jax.ShapeDtypeStruct((DEPTH, 8, 6 * D_MODEL), F32),
        compiler_params=_params(("arbitrary", "arbitrary")),
        name="ada_mod",
    )(cvecs, ada_w, ada_b.reshape(DEPTH, 1, 6 * D_MODEL))
    return out.reshape(DEPTH, 8, 6, D_MODEL)[:, :3]


def _mod_group_of_tile(i):
    return jnp.maximum(i - (N_PROMPT // TOK_TILE - 1), 0)


def _norm_mod_rows(x_ref, g_ref, mod_ref, h_ref, shift_idx, scale_idx, rows=64):
    g = g_ref[...]
    sc = 1.0 + mod_ref[0, scale_idx:scale_idx + 1, :]
    sh = mod_ref[0, shift_idx:shift_idx + 1, :]

    def body(r, carry):
        sl = pl.ds(pl.multiple_of(r * rows, rows), rows)
        xf = x_ref[sl, :]
        ms = jnp.mean(xf * xf, axis=-1, keepdims=True)
        y = xf * lax.rsqrt(ms + EPS) * g
        h_ref[sl, :] = (y * sc + sh).astype(BF16)
        return carry

    lax.fori_loop(0, x_ref.shape[0] // rows, body, 0)


ROW_TM = 512


class _Rows(NamedTuple):
    prompt: jax.Array
    latent: jax.Array
    latent_row0: int


def _one_array(x):
    return _Rows(x, x, N_PROMPT)


def _row_specs(rows, width, col_block=0):
    n_p = N_PROMPT // ROW_TM
    l0 = rows.latent_row0 // ROW_TM
    return [pl.BlockSpec((ROW_TM, width), lambda t: (jnp.minimum(t, n_p - 1), col_block)),
            pl.BlockSpec((ROW_TM, width), lambda t: (l0 + jnp.maximum(t - n_p, 0), col_block))]


def _row_group(t):
    first_latent = N_PROMPT // ROW_TM
    return jnp.where(t < first_latent, 0, 1 + (t - first_latent) // (DEC_SEQ // ROW_TM))


def _is_prompt_tile():
    return pl.program_id(0) < N_PROMPT // ROW_TM


def _proj_kernel(xp_ref, xl_ref, g_ref, mod_ref, w_ref, o_ref, h_ref, wb_ref):
    @pl.when(pl.program_id(0) == 0)
    def _():
        wb_ref[...] = w_ref[...].astype(BF16)

    is_prompt = _is_prompt_tile()
    g = g_ref[...]
    sc = 1.0 + mod_ref[0, 1:2, :]
    sh = mod_ref[0, 0:1, :]
    rows = 64

    def body(r, carry):
        sl = pl.ds(pl.multiple_of(r * rows, rows), rows)
        xf = jnp.where(is_prompt, xp_ref[sl, :], xl_ref[sl, :])
        ms = jnp.mean(xf * xf, axis=-1, keepdims=True)
        y = xf * lax.rsqrt(ms + EPS) * g
        h_ref[sl, :] = (y * sc + sh).astype(BF16)
        return carry

    lax.fori_loop(0, ROW_TM // rows, body, 0)
    o_ref[...] = _bdot(h_ref[...], wb_ref[...])


def _norm_mod_proj(x, g, mods, w, name):
    n = w.shape[1]
    return pl.pallas_call(
        _proj_kernel,
        grid=(N_TOK // ROW_TM,),
        in_specs=_row_specs(x, D_MODEL) + [
            pl.BlockSpec((1, D_MODEL), lambda t: (0, 0)),
            pl.BlockSpec((1, 6, D_MODEL), lambda t: (_row_group(t), 0, 0)),
            pl.BlockSpec((D_MODEL, n), lambda t: (0, 0), pipeline_mode=pl.Buffered(1))],
        out_specs=pl.BlockSpec((ROW_TM, n), lambda t: (t, 0)),
        out_shape=jax.ShapeDtypeStruct((N_TOK, n), F32),
        scratch_shapes=[pltpu.VMEM((ROW_TM, D_MODEL), BF16),
                        pltpu.VMEM((D_MODEL, n), BF16)],
        compiler_params=_params(("arbitrary",)),
        name=name,
    )(x.prompt, x.latent, g.reshape(1, D_MODEL), mods, w)


def _rms_matmul_kernel(a_ref, g_ref, w_ref, o_ref, n_ref, *, normalise):
    a = a_ref[...]
    if normalise:
        a = a * lax.rsqrt(jnp.mean(a * a, axis=-1, keepdims=True) + EPS) * g_ref[...]
    n_ref[...] = a
    o_ref[...] = _bdot(a.astype(BF16), w_ref[...].astype(BF16))


def _rms_matmul(a, col_block, k, g, w, normalise, name, tm=512):
    rows, n = a.shape[0], w.shape[1]
    return pl.pallas_call(
        functools.partial(_rms_matmul_kernel, normalise=normalise),
        grid=(rows // tm,),
        in_specs=[pl.BlockSpec((tm, k), lambda i: (i, col_block)),
                  pl.BlockSpec((1, k), lambda i: (0, 0)),
                  pl.BlockSpec((k, n), lambda i: (0, 0))],
        out_specs=[pl.BlockSpec((tm, n), lambda i: (i, 0)),
                   pl.BlockSpec((tm, k), lambda i: (i, 0))],
        out_shape=[jax.ShapeDtypeStruct((rows, n), F32),
                   jax.ShapeDtypeStruct((rows, k), F32)],
        compiler_params=_params(("arbitrary",)),
        name=name,
    )(a, g.reshape(1, k), w)


def _oproj_kernel(*refs, gated):
    if gated:
        xp_ref, xl_ref, ap_ref, al_ref, g_ref, mod_ref, w_ref, o_ref, wb_ref = refs
    else:
        xp_ref, xl_ref, ap_ref, al_ref, mod_ref, w_ref, o_ref, wb_ref = refs

    @pl.when(pl.program_id(0) == 0)
    def _():
        wb_ref[...] = w_ref[...].astype(BF16)

    is_prompt = _is_prompt_tile()
    a = jnp.where(is_prompt, ap_ref[...], al_ref[...])
    if gated:
        a = a * _silu(g_ref[...])
    y = _bdot(a.astype(BF16), wb_ref[...])
    x = jnp.where(is_prompt, xp_ref[...], xl_ref[...])
    o_ref[...] = x + mod_ref[0, 2:3, :] * y


def _out_proj_residual(x, a, mods, w, name, gate=None, gate_col_block=0):
    k = w.shape[0]
    in_specs = _row_specs(x, D_MODEL) + _row_specs(a, k)
    args = [x.prompt, x.latent, a.prompt, a.latent]
    if gate is not None:
        in_specs.append(pl.BlockSpec((ROW_TM, k), lambda t: (t, gate_col_block)))
        args.append(gate)
    in_specs += [pl.BlockSpec((1, 6, D_MODEL), lambda t: (_row_group(t), 0, 0)),
                 pl.BlockSpec((k, D_MODEL), lambda t: (0, 0))]
    args += [mods, w]
    return pl.pallas_call(
        functools.partial(_oproj_kernel, gated=gate is not None),
        grid=(N_TOK // ROW_TM,),
        in_specs=in_specs,
        out_specs=pl.BlockSpec((ROW_TM, D_MODEL), lambda t: (t, 0)),
        out_shape=jax.ShapeDtypeStruct((N_TOK, D_MODEL), F32),
        scratch_shapes=[pltpu.VMEM((k, D_MODEL), BF16)],
        compiler_params=_params(("arbitrary",)),
        name=name,
    )(*args)


FFN_MM_ROWS = 512
FFN_ROWS = 64
FFN_PAD = 8


def _ffn_kernel(x_ref, g_ref, mod_ref, wg_ref, wv_ref, cwg_ref, cwv_ref, cbg_ref, cbv_ref,
                wd_ref, o_ref, h_ref, u_ref, act_ref, wup_ref, wdn_ref, *, tile0):
    i = tile0 + pl.program_id(0)
    c = pl.program_id(1)
    fc = FF_CHUNK

    @pl.when(c == 0)
    def _():
        _norm_mod_rows(x_ref, g_ref, mod_ref, h_ref, 3, 4)
        zeros = jnp.zeros((FFN_PAD, 2 * fc), F32)
        u_ref[0:FFN_PAD, :] = zeros
        u_ref[FFN_PAD + TOK_TILE:, :] = zeros
        o_ref[...] = jnp.zeros_like(o_ref)

    wup_ref[:, :fc] = wg_ref[...].astype(BF16)
    wup_ref[:, fc:] = wv_ref[...].astype(BF16)
    wdn_ref[...] = wd_ref[...].astype(BF16)
    cw_g, cw_v = cwg_ref[...], cwv_ref[...]
    cb_g, cb_v = cbg_ref[...], cbv_ref[...]
    seq_len = jnp.where(i < N_PROMPT // TOK_TILE, SEQ, DEC_SEQ)
    row = lax.broadcasted_iota(jnp.int32, (FFN_ROWS, 1), 0)

    def up(t):
        r0 = t * FFN_MM_ROWS
        u_ref[FFN_PAD + r0:FFN_PAD + r0 + FFN_MM_ROWS, :] = _bdot(
            h_ref[r0:r0 + FFN_MM_ROWS, :], wup_ref[...])

    def conv_act(t):
        for r0 in range(t * FFN_MM_ROWS, (t + 1) * FFN_MM_ROWS, FFN_ROWS):
            halves = []
            for lo, cw, cb in ((0, cw_g, cb_g), (fc, cw_v, cb_v)):
                p0 = FFN_PAD + r0
                prev = u_ref[p0 - 1:p0 - 1 + FFN_ROWS, lo:lo + fc]
                mid = u_ref[p0:p0 + FFN_ROWS, lo:lo + fc]
                nxt = u_ref[p0 + 1:p0 + 1 + FFN_ROWS, lo:lo + fc]
                if r0 % SEQ == 0:
                    prev = jnp.where(((r0 + row) & (seq_len - 1)) == 0, 0.0, prev)
                if (r0 + FFN_ROWS) % SEQ == 0:
                    nxt = jnp.where(((r0 + row) & (seq_len - 1)) == seq_len - 1, 0.0, nxt)
                halves.append(prev * cw[0:1] + mid * cw[1:2] + nxt * cw[2:3] + cb)
            act_ref[r0:r0 + FFN_ROWS, :] = (_silu(halves[0]) * halves[1]).astype(BF16)

    def down(t):
        r0 = t * FFN_MM_ROWS
        o_ref[r0:r0 + FFN_MM_ROWS, :] += _bdot(act_ref[r0:r0 + FFN_MM_ROWS, :], wdn_ref[...])

    n = TOK_TILE // FFN_MM_ROWS
    for s in range(n + 2):
        if s < n:
            up(s)
        if 1 <= s <= n:
            conv_act(s - 1)
        if s >= 2:
            down(s - 2)

    @pl.when(c == N_FF_CHUNKS - 1)
    def _():
        o_ref[...] = x_ref[...] + mod_ref[0, 5:6, :] * o_ref[...]


def _ffn(x, g, mods, w_up, conv_w, conv_b, w_down, tile0=0, n_tiles=N_TOK_TILES):
    fc = FF_CHUNK
    ncb = N_FF_CHUNKS
    return pl.pallas_call(
        functools.partial(_ffn_kernel, tile0=tile0),
        grid=(n_tiles, ncb),
        in_specs=[pl.BlockSpec((TOK_TILE, D_MODEL), lambda i, c: (tile0 + i, 0)),
                  pl.BlockSpec((1, D_MODEL), lambda i, c: (0, 0)),
                  pl.BlockSpec((1, 6, D_MODEL), lambda i, c: (_mod_group_of_tile(tile0 + i), 0, 0)),
                  pl.BlockSpec((D_MODEL, fc), lambda i, c: (0, c)),
                  pl.BlockSpec((D_MODEL, fc), lambda i, c: (0, ncb + c)),
                  pl.BlockSpec((3, fc), lambda i, c: (0, c)),
                  pl.BlockSpec((3, fc), lambda i, c: (0, ncb + c)),
                  pl.BlockSpec((1, fc), lambda i, c: (0, c)),
                  pl.BlockSpec((1, fc), lambda i, c: (0, ncb + c)),
                  pl.BlockSpec((fc, D_MODEL), lambda i, c: (c, 0))],
        out_specs=pl.BlockSpec((TOK_TILE, D_MODEL), lambda i, c: (i, 0)),
        out_shape=jax.ShapeDtypeStruct((n_tiles * TOK_TILE, D_MODEL), F32),
        scratch_shapes=[pltpu.VMEM((TOK_TILE, D_MODEL), BF16),
                        pltpu.VMEM((TOK_TILE + 2 * FFN_PAD, 2 * fc), F32),
                        pltpu.VMEM((TOK_TILE, fc), BF16),
                        pltpu.VMEM((D_MODEL, 2 * fc), BF16),
                        pltpu.VMEM((fc, D_MODEL), BF16)],
        compiler_params=_params(("arbitrary", "arbitrary")),
        name="conv_ffn",
    )(x, g.reshape(1, D_MODEL), mods, w_up, w_up, conv_w, conv_w,
      conv_b.reshape(1, -1), conv_b.reshape(1, -1), w_down)


NA_HB = 4
NA_LANES = NA_HB * NA_HD
NA_ROWS = DEC_SEQ // GRID_W
NA_KEYS = NA_WIN_R * GRID_W
PROMPT_SEQS = 4
NA_ROW_UNROLL = 2


def _store_heads(dst_ref, seq, x, n_heads, hd):
    for h in range(n_heads):
        dst_ref[seq, h] = x[:, h * hd:(h + 1) * hd]


def _na_prompt_kernel(q_ref, k_ref, v_ref, gq_ref, gk_ref, o_ref, kn_ref, vn_ref):
    scale = NA_HD ** -0.5
    for seq in range(PROMPT_SEQS):
        sl = slice(seq * SEQ, (seq + 1) * SEQ)
        q = _group_rms(q_ref[sl, :], gq_ref[...], NA_HD, sums_on_mxu=True) * scale
        k = _group_rms(k_ref[sl, :], gk_ref[...], NA_HD, sums_on_mxu=True)
        v = v_ref[sl, :]
        _store_heads(kn_ref, seq, k, NA_HB, NA_HD)
        _store_heads(vn_ref, seq, v, NA_HB, NA_HD)
        q4 = _stack_heads(q, NA_HB, NA_HD).astype(BF16)
        s = lax.dot_general(q4, k.astype(BF16), _NT, preferred_element_type=F32)
        (p,), inv = _softmax_parts([s])
        o4 = _bdot(p.astype(BF16), v.astype(BF16)) * inv
        o_ref[sl, :] = _unstack_heads(o4, NA_HB, NA_HD)


def _na_prompt(qkv, gq, gk):
    nb = NA_HEADS // NA_HB
    rows = PROMPT_SEQS * SEQ
    blk = lambda off: pl.BlockSpec((rows, NA_LANES), lambda b, j: (b, off + j))
    vec = pl.BlockSpec((1, NA_LANES), lambda b, j: (0, 0))
    cache = pl.BlockSpec((PROMPT_SEQS, NA_HB, SEQ, NA_HD), lambda b, j: (b, j, 0, 0))
    cache_shape = jax.ShapeDtypeStruct((BATCH, NA_HEADS, SEQ, NA_HD), F32)
    return pl.pallas_call(
        _na_prompt_kernel,
        grid=(BATCH // PROMPT_SEQS, nb),
        in_specs=[blk(0), blk(nb), blk(2 * nb), vec, vec],
        out_specs=[blk(0), cache, cache],
        out_shape=[jax.ShapeDtypeStruct((N_PROMPT, D_MODEL), F32), cache_shape, cache_shape],
        compiler_params=_params(("arbitrary", "arbitrary")),
        name="na_prompt",
    )(qkv, qkv, qkv, jnp.tile(gq, NA_HB).reshape(1, -1), jnp.tile(gk, NA_HB).reshape(1, -1))


def _na_latent_kernel(q_ref, k_ref, v_ref, kc_ref, vc_ref, bias_ref, gq_ref, gk_ref, o_ref,
                      qn_ref, kn_ref, vb_ref, kc4_ref, vc4_ref):
    scale = NA_HD ** -0.5
    rows = 256

    def prep(r, carry):
        sl = pl.ds(pl.multiple_of(r * rows, rows), rows)
        qn_ref[sl, :] = (_group_rms(q_ref[sl, :], gq_ref[...], NA_HD, sums_on_mxu=True)
                         * scale).astype(BF16)
        kn_ref[sl, :] = _group_rms(k_ref[sl, :], gk_ref[...], NA_HD,
                                   sums_on_mxu=True).astype(BF16)
        vb_ref[sl, :] = v_ref[sl, :].astype(BF16)
        return carry

    lax.fori_loop(0, DEC_SEQ // rows, prep, 0)
    kc4_ref[...] = kc_ref[0].astype(BF16)
    vc4_ref[...] = vc_ref[0].astype(BF16)

    def row(r):
        kr0 = jnp.clip(r - NA_WIN_R // 2, 0, NA_ROWS - NA_WIN_R)
        pat = kr0 - r + NA_WIN_R - 1
        qs = pl.ds(pl.multiple_of(r * GRID_W, GRID_W), GRID_W)
        ks = pl.ds(pl.multiple_of(kr0 * GRID_W, GRID_W), NA_KEYS)
        q4 = _stack_heads(qn_ref[qs, :], NA_HB, NA_HD)
        s_loc = lax.dot_general(q4, kn_ref[ks, :], _NT, preferred_element_type=F32)
        bias = jnp.concatenate([bias_ref[h, pat] for h in range(NA_HB)], axis=0)
        s_loc = s_loc + bias
        s_ctx = lax.dot_general(q4, kc4_ref[...], _NT, preferred_element_type=F32)
        (p_loc, p_ctx), inv = _softmax_parts([s_loc, s_ctx])
        o4 = _bdot(p_loc.astype(BF16), vb_ref[ks, :]) + _bdot(p_ctx.astype(BF16), vc4_ref[...])
        o_ref[qs, :] = _unstack_heads(o4 * inv, NA_HB, NA_HD)

    def rows_step(i, carry):
        for u in range(NA_ROW_UNROLL):
            row(i * NA_ROW_UNROLL + u)
        return carry

    lax.fori_loop(0, NA_ROWS // NA_ROW_UNROLL, rows_step, 0)


def _na_bias_blocks(bias_table):
    qc = jnp.arange(GRID_W)[:, None]
    kc = jnp.arange(GRID_W)[None, :]
    win0 = jnp.clip(qc - NA_WIN_C // 2, 0, GRID_W - NA_WIN_C)
    valid = (kc >= win0) & (kc < win0 + NA_WIN_C)
    n_ro, n_co = bias_table.shape[1:]
    c = NA_WIN_C - 1
    period = jnp.concatenate(
        [bias_table[..., c:], jnp.zeros((NA_HEADS, n_ro, 2 * GRID_W - n_co), F32),
         bias_table[..., :c]], axis=-1)
    flat = jnp.tile(period, (1, 1, GRID_W))[..., :GRID_W * (2 * GRID_W - 1)]
    t = flat.reshape(NA_HEADS, n_ro, GRID_W, 2 * GRID_W - 1)[..., :GRID_W]
    t = jnp.where(valid, t, NEG)
    b = jnp.stack([t[:, p:p + NA_WIN_R] for p in range(NA_WIN_R)], axis=1)
    b = jnp.transpose(b, (0, 1, 3, 2, 4))
    return b.reshape(NA_HEADS, NA_WIN_R, GRID_W, NA_KEYS)


def _na_latent(qkv, cache_k, cache_v, bias_blocks, gq, gk):
    nb = NA_HEADS // NA_HB
    lat0 = N_PROMPT // DEC_SEQ
    blk = lambda off: pl.BlockSpec((DEC_SEQ, NA_LANES), lambda b, j: (lat0 + b, off + j))
    vec = pl.BlockSpec((1, NA_LANES), lambda b, j: (0, 0))
    cache = pl.BlockSpec((1, PAST_LEN, NA_LANES), lambda b, j: (b, 0, j))
    return pl.pallas_call(
        _na_latent_kernel,
        grid=(DEC_BATCH, nb),
        in_specs=[blk(0), blk(nb), blk(2 * nb), cache, cache,
                  pl.BlockSpec((NA_HB, NA_WIN_R, GRID_W, NA_KEYS), lambda b, j: (j, 0, 0, 0)),
                  vec, vec],
        out_specs=pl.BlockSpec((DEC_SEQ, NA_LANES), lambda b, j: (b, j)),
        out_shape=jax.ShapeDtypeStruct((N_LATENT, D_MODEL), F32),
        scratch_shapes=[pltpu.VMEM((DEC_SEQ, NA_LANES), BF16),
                        pltpu.VMEM((DEC_SEQ, NA_LANES), BF16),
                        pltpu.VMEM((DEC_SEQ, NA_LANES), BF16),
                        pltpu.VMEM((PAST_LEN, NA_LANES), BF16),
                        pltpu.VMEM((PAST_LEN, NA_LANES), BF16)],
        compiler_params=_params(("arbitrary", "arbitrary")),
        name="na_latent",
    )(qkv, qkv, qkv, _tokens_first(cache_k), _tokens_first(cache_v), bias_blocks,
      jnp.tile(gq, NA_HB).reshape(1, -1), jnp.tile(gk, NA_HB).reshape(1, -1))


GLA_C = 128
GLA_SUB = 16
GLA_LEVELS = (64, 32, 16)


def _split_hi_lo(x):
    hi = x.astype(BF16)
    lo = (x - hi.astype(F32)).astype(BF16)
    return jnp.concatenate([hi, lo], axis=1)


def _gla_chunk(q, k, v, g, st_ref, rev):
    c = GLA_C
    row = lax.broadcasted_iota(jnp.int32, (c, c), 0)
    col = lax.broadcasted_iota(jnp.int32, (c, c), 1)
    tri = (col >= row) if rev else (col <= row)
    cs = _bdot(jnp.where(tri, 1.0, 0.0).astype(BF16), _split_hi_lo(g))
    b = cs[:, :GLA_DK] + cs[:, GLA_DK:]
    causal = (col >= row) if rev else (col <= row)

    a = jnp.zeros((c, c), F32)
    rid = lax.broadcasted_iota(jnp.int32, (c, GLA_DK), 0)
    for m in GLA_LEVELS:
        nblk = c // (2 * m)
        if rev:
            bnd = [b[j * 2 * m + m:j * 2 * m + m + 1] for j in range(nblk)]
        else:
            bnd = [b[j * 2 * m + m - 1:j * 2 * m + m] for j in range(nblk)]
        ref = jnp.concatenate([jnp.broadcast_to(x, (2 * m, GLA_DK)) for x in bnd], axis=0)
        later = ((rid & m) == 0) if rev else ((rid & m) != 0)
        dq = jnp.minimum(b - ref, 0.0)
        dk_ = jnp.minimum(ref - b, 0.0)
        qh = jnp.where(later, q * jnp.exp(dq), 0.0).astype(BF16)
        kh = jnp.where(later, 0.0, k * jnp.exp(dk_)).astype(BF16)
        blk = lax.dot_general(qh, kh, _NT, preferred_element_type=F32)
        same = (row >> _log2(2 * m)) == (col >> _log2(2 * m))
        a = a + jnp.where(same, blk, 0.0)

    nsub = c // GLA_SUB
    lane_c = lax.broadcasted_iota(jnp.int32, (GLA_SUB, c), 1)
    srow = lax.broadcasted_iota(jnp.int32, (GLA_SUB, 1), 0)
    diag_rows = []
    for blk_i in range(nsub):
        r0 = blk_i * GLA_SUB
        qb = q[r0:r0 + GLA_SUB]
        bb = b[r0:r0 + GLA_SUB]
        acc = jnp.zeros((GLA_SUB, c), F32)
        for s in range(GLA_SUB):
            ks = k[r0 + s:r0 + s + 1]
            bs = b[r0 + s:r0 + s + 1]
            w = jnp.sum(qb * ks * jnp.exp(jnp.minimum(bb - bs, 0.0)), axis=-1, keepdims=True)
            keep = (srow <= s) if rev else (srow >= s)
            w = jnp.where(keep, w, 0.0)
            acc = jnp.where(lane_c == r0 + s, w, acc)
        diag_rows.append(acc)
    a = a + jnp.concatenate(diag_rows, axis=0)
    a = jnp.where(causal, a, 0.0)

    st = st_ref[...]
    inter = lax.dot_general((q * jnp.exp(b)).astype(BF16), st.astype(BF16), _NT,
                            preferred_element_type=F32)
    o = inter + _bdot(a.astype(BF16), v.astype(BF16))

    btot = b[0:1] if rev else b[c - 1:c]
    kd = (k * jnp.exp(btot - b)).astype(BF16)
    st_ref[...] = st * jnp.exp(btot) + lax.dot_general(v.astype(BF16), kd, _TN,
                                                       preferred_element_type=F32)
    return o


def _gla_kernel(*refs, n_tok, has_state):
    if has_state:
        (q_ref, k_ref, v_ref, r_ref, w2_ref, bg_ref, gn_ref, s0f_ref, s0b_ref,
         o_ref, lg_ref, of_ref, ob_ref, stf_ref, stb_ref) = refs
    else:
        (q_ref, k_ref, v_ref, r_ref, w2_ref, bg_ref, gn_ref,
         o_ref, sf_ref, sb_ref, lg_ref, of_ref, ob_ref, stf_ref, stb_ref) = refs
    nc = n_tok // GLA_C
    scale = GLA_DK ** -0.5

    rb = r_ref[...].astype(BF16)
    for z in range(2):
        x = _bdot(rb, w2_ref[z].astype(BF16)) + bg_ref[z]
        lg_ref[z] = (jnp.minimum(x, 0.0) - jnp.log1p(jnp.exp(-jnp.abs(x)))) * (1.0 / GLA_GATE_NORM)

    if has_state:
        stf_ref[...] = s0f_ref[0, 0].T
        stb_ref[...] = s0b_ref[0, 0].T
    else:
        stf_ref[...] = jnp.zeros_like(stf_ref)
        stb_ref[...] = jnp.zeros_like(stb_ref)

    def step(ci, carry):
        for rev in (False, True):
            cc = (nc - 1 - ci) if rev else ci
            sl = pl.ds(pl.multiple_of(cc * GLA_C, GLA_C), GLA_C)
            q = q_ref[sl, :] * scale
            o = _gla_chunk(q, k_ref[sl, :], v_ref[sl, :], lg_ref[1 if rev else 0, sl, :],
                           stb_ref if rev else stf_ref, rev)
            (ob_ref if rev else of_ref)[sl, :] = o
        return carry

    lax.fori_loop(0, nc, step, 0)

    o = of_ref[...] + ob_ref[...]
    o_ref[...] = o * lax.rsqrt(jnp.mean(o * o, axis=-1, keepdims=True) + EPS) * gn_ref[...]
    if not has_state:
        sf_ref[0, 0] = stf_ref[...].T
        sb_ref[0, 0] = stb_ref[...].T


def _gla(proj, w2, bg, gnorm, n_seq, n_tok, row_block0, states=None):
    qb = GLA_HK // GLA_DK
    spec = lambda width, off: pl.BlockSpec((n_tok, width), lambda b, h: (row_block0 + b, off + h))
    in_specs = [spec(GLA_DK, 0), spec(GLA_DK, qb), spec(GLA_DV, 2 * GLA_HK // GLA_DV),
                pl.BlockSpec((n_tok, 128), lambda b, h: (row_block0 + b, (2 * GLA_HK + 2 * GLA_HV) // 128)),
                pl.BlockSpec((2, 128, GLA_DK), lambda b, h: (0, 0, h)),
                pl.BlockSpec((2, 1, GLA_DK), lambda b, h: (0, 0, h)),
                pl.BlockSpec((1, GLA_DV), lambda b, h: (0, 0))]
    args = [proj, proj, proj, proj, w2, bg, gnorm.reshape(1, GLA_DV)]
    st_spec = pl.BlockSpec((1, 1, GLA_DK, GLA_DV), lambda b, h: (b, h, 0, 0))
    o_spec = pl.BlockSpec((n_tok, GLA_DV), lambda b, h: (b, h))
    o_shape = jax.ShapeDtypeStruct((n_seq * n_tok, GLA_HV), F32)
    if states is not None:
        in_specs += [st_spec, st_spec]
        args += list(states)
        out_specs, out_shape = o_spec, o_shape
    else:
        st_shape = jax.ShapeDtypeStruct((n_seq, GLA_HEADS, GLA_DK, GLA_DV), F32)
        out_specs, out_shape = [o_spec, st_spec, st_spec], [o_shape, st_shape, st_shape]
    return pl.pallas_call(
        functools.partial(_gla_kernel, n_tok=n_tok, has_state=states is not None),
        grid=(n_seq, GLA_HEADS),
        in_specs=in_specs,
        out_specs=out_specs,
        out_shape=out_shape,
        scratch_shapes=[pltpu.VMEM((2, n_tok, GLA_DK), F32),
                        pltpu.VMEM((n_tok, GLA_DV), F32),
                        pltpu.VMEM((n_tok, GLA_DV), F32),
                        pltpu.VMEM((GLA_DV, GLA_DK), F32),
                        pltpu.VMEM((GLA_DV, GLA_DK), F32)],
        compiler_params=_params(("arbitrary", "arbitrary")),
        name="gla_latent" if states is not None else "gla_prompt",
    )(*args)


DIFF_HB = 2
DIFF_QL = DIFF_HB * DIFF_HD
DIFF_VL = DIFF_HB * 2 * DIFF_HD
DIFF_TQ = 256
DIFF_SUB = 128


def _diff_lambda(lam_ref):
    l = lam_ref[...]
    a = jnp.sum(l[0:1] * l[1:2], axis=-1, keepdims=True)
    b = jnp.sum(l[2:3] * l[3:4], axis=-1, keepdims=True)
    return jnp.exp(a) - jnp.exp(b) + DIFF_LAMBDA_INIT


def _diff_finish(a, v, sn_ref):
    o2 = _bdot(a.astype(BF16), v)
    o = _unstack_heads(o2, DIFF_HB, 2 * DIFF_HD)
    return _group_rms(o, sn_ref[...], 2 * DIFF_HD) * (1.0 - DIFF_LAMBDA_INIT)


def _diff_prompt_kernel(q0_ref, q1_ref, k0_ref, k1_ref, v_ref, gq_ref, gk_ref, lam_ref, sn_ref,
                        o_ref, kn_ref, vn_ref):
    scale = DIFF_HD ** -0.5
    lam = _diff_lambda(lam_ref)
    for seq in range(PROMPT_SEQS):
        sl = slice(seq * SEQ, (seq + 1) * SEQ)
        ps = []
        for comp, (q_ref, k_ref) in enumerate(((q0_ref, k0_ref), (q1_ref, k1_ref))):
            q = _group_rms(q_ref[sl, :], gq_ref[...], DIFF_HD, sums_on_mxu=True) * scale
            k = _group_rms(k_ref[sl, :], gk_ref[...], DIFF_HD, sums_on_mxu=True)
            for h in range(DIFF_HB):
                kn_ref[seq, comp, h] = k[:, h * DIFF_HD:(h + 1) * DIFF_HD]
            q2 = _stack_heads(q, DIFF_HB, DIFF_HD).astype(BF16)
            s = lax.dot_general(q2, k.astype(BF16), _NT, preferred_element_type=F32)
            (p,), inv = _softmax_parts([s])
            ps.append(p * inv)
        v = v_ref[sl, :]
        _store_heads(vn_ref, seq, v, DIFF_HB, 2 * DIFF_HD)
        o_ref[sl, :] = _diff_finish(ps[0] - lam * ps[1], v.astype(BF16), sn_ref)


def _diff_prompt(qkv, gq, gk, lam, sub_norm):
    nb = DIFF_HEADS // DIFF_HB
    rows = PROMPT_SEQS * SEQ
    qk = lambda off: pl.BlockSpec((rows, DIFF_QL), lambda b, j: (b, off + j))
    vec = lambda n: pl.BlockSpec((1, n), lambda b, j: (0, 0))
    v_spec = pl.BlockSpec((rows, DIFF_VL), lambda b, j: (b, 2 * D_MODEL // DIFF_VL + j))
    kn_spec = pl.BlockSpec((PROMPT_SEQS, 2, DIFF_HB, SEQ, DIFF_HD), lambda b, j: (b, 0, j, 0, 0))
    vn_spec = pl.BlockSpec((PROMPT_SEQS, DIFF_HB, SEQ, 2 * DIFF_HD), lambda b, j: (b, j, 0, 0))
    return pl.pallas_call(
        _diff_prompt_kernel,
        grid=(BATCH // PROMPT_SEQS, nb),
        in_specs=[qk(0), qk(nb), qk(2 * nb), qk(3 * nb), v_spec, vec(DIFF_QL), vec(DIFF_QL),
                  pl.BlockSpec((4, DIFF_HD), lambda b, j: (0, 0)), vec(DIFF_VL)],
        out_specs=[pl.BlockSpec((rows, DIFF_VL), lambda b, j: (b, j)), kn_spec, vn_spec],
        out_shape=[jax.ShapeDtypeStruct((N_PROMPT, D_MODEL), F32),
                   jax.ShapeDtypeStruct((BATCH, 2, DIFF_HEADS, SEQ, DIFF_HD), F32),
                   jax.ShapeDtypeStruct((BATCH, DIFF_HEADS, SEQ, 2 * DIFF_HD), F32)],
        compiler_params=_params(("arbitrary", "arbitrary")),
        name="diff_prompt",
    )(qkv, qkv, qkv, qkv, qkv, jnp.tile(gq, DIFF_HB).reshape(1, -1),
      jnp.tile(gk, DIFF_HB).reshape(1, -1), lam, jnp.tile(sub_norm, DIFF_HB).reshape(1, -1))


def _diff_latent_kernel(q0_ref, q1_ref, k0_ref, k1_ref, v_ref, kc0_ref, kc1_ref, vc_ref,
                        cos_ref, sin_ref, cosq_ref, sinq_ref, gq_ref, gk_ref, lam_ref, sn_ref,
                        o_ref, kb0_ref, kb1_ref, vb_ref):
    scale = DIFF_HD ** -0.5
    half = DIFF_HD // 4
    rows = 256

    @pl.when(pl.program_id(2) == 0)
    def _():
        for k_ref, kc_ref, kb_ref in ((k0_ref, kc0_ref, kb0_ref), (k1_ref, kc1_ref, kb1_ref)):
            kb_ref[0:PAST_LEN, :] = kc_ref[0].astype(BF16)

            def prep(r, carry):
                sl = pl.ds(pl.multiple_of(r * rows, rows), rows)
                k = _group_rms(k_ref[sl, :], gk_ref[...], DIFF_HD)
                k = _rope(k, cos_ref[sl, :], sin_ref[sl, :], half)
                kb_ref[pl.ds(pl.multiple_of(PAST_LEN + r * rows, rows), rows), :] = k.astype(BF16)
                return carry

            lax.fori_loop(0, DEC_SEQ // rows, prep, 0)
        vb_ref[0:PAST_LEN, :] = vc_ref[0].astype(BF16)
        vb_ref[PAST_LEN:, :] = v_ref[...].astype(BF16)

    lam = _diff_lambda(lam_ref)
    for r0 in range(0, DIFF_TQ, DIFF_SUB):
        sl = slice(r0, r0 + DIFF_SUB)
        ps = []
        for q_ref, kb_ref in ((q0_ref, kb0_ref), (q1_ref, kb1_ref)):
            q = _group_rms(q_ref[sl, :], gq_ref[...], DIFF_HD)
            q = _rope(q, cosq_ref[sl, :], sinq_ref[sl, :], half) * scale
            q2 = _stack_heads(q, DIFF_HB, DIFF_HD).astype(BF16)
            s = lax.dot_general(q2, kb_ref[...], _NT, preferred_element_type=F32)
            (p,), inv = _softmax_parts([s])
            ps.append(p * inv)
        o_ref[sl, :] = _diff_finish(ps[0] - lam * ps[1], vb_ref[...], sn_ref)


def _diff_latent(qkv, cache_k, cache_v, cos, sin, gq, gk, lam, sub_norm):
    nb = DIFF_HEADS // DIFF_HB
    nq = DEC_SEQ // DIFF_TQ
    q0 = N_PROMPT // DIFF_TQ
    lat0 = N_PROMPT // DEC_SEQ
    n_keys = PAST_LEN + DEC_SEQ
    q_spec = lambda off: pl.BlockSpec((DIFF_TQ, DIFF_QL), lambda b, j, t: (q0 + b * nq + t, off + j))
    k_spec = lambda off: pl.BlockSpec((DEC_SEQ, DIFF_QL), lambda b, j, t: (lat0 + b, off + j))
    v_spec = pl.BlockSpec((DEC_SEQ, DIFF_VL), lambda b, j, t: (lat0 + b, 2 * D_MODEL // DIFF_VL + j))
    kc_spec = lambda off: pl.BlockSpec((1, PAST_LEN, DIFF_QL), lambda b, j, t: (b, 0, off + j))
    vc_spec = pl.BlockSpec((1, PAST_LEN, DIFF_VL), lambda b, j, t: (b, 0, j))
    tab = pl.BlockSpec((DEC_SEQ, DIFF_QL), lambda b, j, t: (0, 0))
    tabq = pl.BlockSpec((DIFF_TQ, DIFF_QL), lambda b, j, t: (t, 0))
    vec = lambda n: pl.BlockSpec((1, n), lambda b, j, t: (0, 0))
    return pl.pallas_call(
        _diff_latent_kernel,
        grid=(DEC_BATCH, nb, nq),
        in_specs=[q_spec(0), q_spec(nb), k_spec(2 * nb), k_spec(3 * nb), v_spec,
                  kc_spec(0), kc_spec(nb), vc_spec, tab, tab, tabq, tabq,
                  vec(DIFF_QL), vec(DIFF_QL),
                  pl.BlockSpec((4, DIFF_HD), lambda b, j, t: (0, 0)), vec(DIFF_VL)],
        out_specs=pl.BlockSpec((DIFF_TQ, DIFF_VL), lambda b, j, t: (b * nq + t, j)),
        out_shape=jax.ShapeDtypeStruct((N_LATENT, D_MODEL), F32),
        scratch_shapes=[pltpu.VMEM((n_keys, DIFF_QL), BF16),
                        pltpu.VMEM((n_keys, DIFF_QL), BF16),
                        pltpu.VMEM((n_keys, DIFF_VL), BF16)],
        compiler_params=_params(("arbitrary", "arbitrary", "arbitrary")),
        name="diff_latent",
    )(qkv, qkv, qkv, qkv, qkv, _tokens_first(cache_k), _tokens_first(cache_k),
      _tokens_first(cache_v), cos, sin, cos, sin,
      jnp.tile(gq, DIFF_HB).reshape(1, -1), jnp.tile(gk, DIFF_HB).reshape(1, -1), lam,
      jnp.tile(sub_norm, DIFF_HB).reshape(1, -1))


MLA_HB = 2
MLA_HL = 128
MLA_LANES = MLA_HB * MLA_HL
MLA_TQ = 512
MLA_SUB = 128


def _mla_keys(kv, kr, gk):
    lane = lax.broadcasted_iota(jnp.int32, kv.shape, 1)
    kr2 = jnp.concatenate([kr] * MLA_HB, axis=1)
    k = jnp.where((lane & (MLA_HL - 1)) < MLA_NOPE, kv, kr2)
    return _group_rms(k, gk, MLA_HL, n_real=MLA_QK)


def _mla_out(o2):
    tq = o2.shape[0] // MLA_HB
    oa = pltpu.roll(o2[0:tq, 0:MLA_HL], MLA_HL - MLA_V, axis=1)
    ob = o2[tq:, MLA_HL:]
    lane = lax.broadcasted_iota(jnp.int32, oa.shape, 1)
    return jnp.where(lane < MLA_V, oa, ob)


def _mla_prompt_kernel(q_ref, kv_ref, kr_ref, gq_ref, gk_ref, o_ref):
    scale = MLA_QK ** -0.5
    for seq in range(PROMPT_SEQS):
        sl = slice(seq * SEQ, (seq + 1) * SEQ)
        q = _group_rms(q_ref[sl, :], gq_ref[...], MLA_HL, n_real=MLA_QK) * scale
        kv = kv_ref[sl, :]
        k = _mla_keys(kv, kr_ref[sl, :], gk_ref[...])
        q2 = _stack_heads(q, MLA_HB, MLA_HL).astype(BF16)
        s = lax.dot_general(q2, k.astype(BF16), _NT, preferred_element_type=F32)
        (p,), inv = _softmax_parts([s])
        o_ref[sl, :] = _mla_out(_bdot(p.astype(BF16), kv.astype(BF16)) * inv)


def _mla_prompt(qp, kvp, low, gq, gk):
    nb = MLA_HEADS // MLA_HB
    rows = PROMPT_SEQS * SEQ
    blk = pl.BlockSpec((rows, MLA_LANES), lambda b, j: (b, j))
    vec = pl.BlockSpec((1, MLA_LANES), lambda b, j: (0, 0))
    return pl.pallas_call(
        _mla_prompt_kernel,
        grid=(BATCH // PROMPT_SEQS, nb),
        in_specs=[blk, blk, pl.BlockSpec((rows, MLA_HL), lambda b, j: (b, MLA_LOW_KR // MLA_HL)),
                  vec, vec],
        out_specs=pl.BlockSpec((rows, MLA_HB * MLA_V), lambda b, j: (b, j)),
        out_shape=jax.ShapeDtypeStruct((N_PROMPT, MLA_HEADS * MLA_V), F32),
        compiler_params=_params(("arbitrary", "arbitrary")),
        name="mla_prompt",
    )(qp, kvp, low, gq, gk)


def _mla_latent_kernel(q_ref, kv_ref, kr_ref, kvc_ref, krc_ref, cos_ref, sin_ref, cosq_ref,
                       sinq_ref, gq_ref, gk_ref, o_ref, kb_ref, vb_ref):
    scale = MLA_QK ** -0.5
    half = MLA_ROPE // 4
    rows = 256

    @pl.when(pl.program_id(2) == 0)
    def _():
        kvc = kvc_ref[...]
        kb_ref[0:PAST_LEN, :] = _mla_keys(kvc, krc_ref[...], gk_ref[...]).astype(BF16)
        vb_ref[0:PAST_LEN, :] = kvc.astype(BF16)

        def prep(r, carry):
            sl = pl.ds(pl.multiple_of(r * rows, rows), rows)
            dst = pl.ds(pl.multiple_of(PAST_LEN + r * rows, rows), rows)
            kv = kv_ref[sl, :]
            k = _mla_keys(kv, kr_ref[sl, :], gk_ref[...])
            k = _rope(k, cos_ref[sl, :], sin_ref[sl, :], half)
            kb_ref[dst, :] = k.astype(BF16)
            vb_ref[dst, :] = kv.astype(BF16)
            return carry

        lax.fori_loop(0, DEC_SEQ // rows, prep, 0)

    for r0 in range(0, MLA_TQ, MLA_SUB):
        sl = slice(r0, r0 + MLA_SUB)
        q = _group_rms(q_ref[sl, :], gq_ref[...], MLA_HL, n_real=MLA_QK)
        q = _rope(q, cosq_ref[sl, :], sinq_ref[sl, :], half) * scale
        q2 = _stack_heads(q, MLA_HB, MLA_HL).astype(BF16)
        s = lax.dot_general(q2, kb_ref[...], _NT, preferred_element_type=F32)
        (p,), inv = _softmax_parts([s])
        o_ref[sl, :] = _mla_out(_bdot(p.astype(BF16), vb_ref[...]) * inv)


def _mla_latent(qp, kvp, low, kvc, krc, cos, sin, gq, gk):
    nb = MLA_HEADS // MLA_HB
    nq = DEC_SEQ // MLA_TQ
    q0 = N_PROMPT // MLA_TQ
    lat0 = N_PROMPT // DEC_SEQ
    n_keys = PAST_LEN + DEC_SEQ
    tab = pl.BlockSpec((DEC_SEQ, MLA_LANES), lambda b, j, t: (0, 0))
    tabq = pl.BlockSpec((MLA_TQ, MLA_LANES), lambda b, j, t: (t, 0))
    vec = pl.BlockSpec((1, MLA_LANES), lambda b, j, t: (0, 0))
    return pl.pallas_call(
        _mla_latent_kernel,
        grid=(DEC_BATCH, nb, nq),
        in_specs=[pl.BlockSpec((MLA_TQ, MLA_LANES), lambda b, j, t: (q0 + b * nq + t, j)),
                  pl.BlockSpec((DEC_SEQ, MLA_LANES), lambda b, j, t: (lat0 + b, j)),
                  pl.BlockSpec((DEC_SEQ, MLA_HL), lambda b, j, t: (lat0 + b, MLA_LOW_KR // MLA_HL)),
                  pl.BlockSpec((PAST_LEN, MLA_LANES), lambda b, j, t: (b, j)),
                  pl.BlockSpec((PAST_LEN, MLA_HL), lambda b, j, t: (b, 0)),
                  tab, tab, tabq, tabq, vec, vec],
        out_specs=pl.BlockSpec((MLA_TQ, MLA_HB * MLA_V), lambda b, j, t: (b * nq + t, j)),
        out_shape=jax.ShapeDtypeStruct((N_LATENT, MLA_HEADS * MLA_V), F32),
        scratch_shapes=[pltpu.VMEM((n_keys, MLA_LANES), BF16),
                        pltpu.VMEM((n_keys, MLA_LANES), BF16)],
        compiler_params=_params(("arbitrary", "arbitrary", "arbitrary")),
        name="mla_latent",
    )(qp, kvp, low, kvc, krc, cos, sin, cos, sin, gq, gk)


MLA_LOW_Q = 0
MLA_LOW_KV = 512
MLA_LOW_KR = 768
MLA_LOW_N = 896


def _axial_tables(n_tok, rdim):
    nf = rdim // 4
    freqs = ROPE_BASE ** (-jnp.arange(nf, dtype=F32) / nf)
    t = jnp.arange(n_tok)
    rowp = (t // GRID_W).astype(F32)
    colp = (t % GRID_W).astype(F32)
    ang = jnp.stack([rowp[:, None] * freqs, colp[:, None] * freqs], axis=1)
    cos, sin = jnp.cos(ang), jnp.sin(ang)
    cos_l = jnp.stack([cos, cos], axis=2).reshape(n_tok, rdim)
    sin_l = jnp.stack([-sin, sin], axis=2).reshape(n_tok, rdim)
    return cos_l, sin_l


def _diff_rope_tables():
    cos, sin = _axial_tables(DEC_SEQ, DIFF_HD)
    return jnp.tile(cos, (1, DIFF_HB)), jnp.tile(sin, (1, DIFF_HB))


def _mla_rope_tables():
    cos, sin = _axial_tables(DEC_SEQ, MLA_ROPE)
    ones = jnp.ones((DEC_SEQ, MLA_NOPE), F32)
    pad1 = jnp.ones((DEC_SEQ, MLA_HL - MLA_QK), F32)
    cos_h = jnp.concatenate([ones, cos, pad1], axis=1)
    sin_h = jnp.concatenate([0 * ones, sin, 0 * pad1], axis=1)
    return jnp.tile(cos_h, (1, MLA_HB)), jnp.tile(sin_h, (1, MLA_HB))


def _tokens_first(cache):
    b, h, l, d = cache.shape
    return jnp.transpose(cache, (0, 2, 1, 3)).reshape(b, l, h * d)


def _pad_heads(w, heads, hd, hl):
    k = w.shape[0]
    return jnp.pad(w.reshape(k, heads, hd), ((0, 0), (0, 0), (0, hl - hd))).reshape(k, heads * hl)


def kernel(x_prompt, x_sample, cache_l0_k, cache_l0_v, state_l1_fwd, state_l1_bwd, cache_l2_k,
           cache_l2_v, cache_l3_ckv, cache_l3_krope, c, c_ctx, ada_w, ada_b, norm_mix, norm_ffn,
           ffn_w_up, ffn_conv_w, ffn_conv_b, ffn_w_down, na_w_qkv, na_q_norm, na_k_norm, na_bias,
           na_w_o, gla_w_qkvg, gla_w_gate1, gla_w_gate2, gla_b_gate, gla_o_norm, gla_w_o,
           diff_w_qkv, diff_q_norm, diff_k_norm, diff_lambda, diff_sub_norm, diff_w_o, mla_w_dq,
           mla_q_a_norm, mla_w_uq, mla_w_dkv, mla_kv_a_norm, mla_w_ukv, mla_q_norm, mla_k_norm,
           mla_w_o):
    xr = _Rows(x_prompt.reshape(N_PROMPT, D_MODEL), x_sample.reshape(N_LATENT, D_MODEL), 0)
    cvecs = jnp.concatenate([c_ctx[None], c, jnp.zeros((5, D_MODEL), F32)], axis=0)
    mods_all = _ada_mods(cvecs, ada_w, ada_b)
    halves = lambda o_p, o_s: _Rows(o_p, o_s, 0)

    mods = mods_all[0]
    qkv = _norm_mod_proj(xr, norm_mix[0], mods, na_w_qkv, "na_qkv")
    o_p, new_l0_k, new_l0_v = _na_prompt(qkv, na_q_norm, na_k_norm)
    o_s = _na_latent(qkv, cache_l0_k, cache_l0_v, _na_bias_blocks(na_bias), na_q_norm, na_k_norm)
    x = _out_proj_residual(xr, halves(o_p, o_s), mods, na_w_o, "na_out")
    x = _ffn(x, norm_ffn[0], mods, ffn_w_up[0], ffn_conv_w[0], ffn_conv_b[0], ffn_w_down[0])
    xr = _one_array(x)

    mods = mods_all[1]
    w_cat = jnp.concatenate(
        [gla_w_qkvg, gla_w_gate1[0], gla_w_gate1[1],
         jnp.zeros((D_MODEL, 128 - 2 * GLA_GATE_RANK), F32)], axis=1)
    proj = _norm_mod_proj(xr, norm_mix[1], mods, w_cat, "gla_proj")
    w2 = jnp.zeros((2, 128, GLA_HK), F32)
    w2 = w2.at[0, :GLA_GATE_RANK].set(gla_w_gate2[0])
    w2 = w2.at[1, GLA_GATE_RANK:2 * GLA_GATE_RANK].set(gla_w_gate2[1])
    bg = gla_b_gate.reshape(2, 1, GLA_HK)
    o_p, new_l1_fwd, new_l1_bwd = _gla(proj, w2, bg, gla_o_norm, BATCH, SEQ, 0)
    o_s = _gla(proj, w2, bg, gla_o_norm, DEC_BATCH, DEC_SEQ, N_PROMPT // DEC_SEQ,
               states=(state_l1_fwd, state_l1_bwd))
    x = _out_proj_residual(xr, halves(o_p, o_s), mods, gla_w_o, "gla_out",
                           gate=proj, gate_col_block=(2 * GLA_HK + GLA_HV) // GLA_HV)
    x = _ffn(x, norm_ffn[1], mods, ffn_w_up[1], ffn_conv_w[1], ffn_conv_b[1], ffn_w_down[1])
    xr = _one_array(x)

    mods = mods_all[2]
    qkv = _norm_mod_proj(xr, norm_mix[2], mods, diff_w_qkv, "diff_qkv")
    o_p, kn_p, new_l2_v = _diff_prompt(qkv, diff_q_norm, diff_k_norm, diff_lambda, diff_sub_norm)
    new_l2_k = kn_p.reshape(BATCH, 2 * DIFF_HEADS, SEQ, DIFF_HD)
    cos_d, sin_d = _diff_rope_tables()
    o_s = _diff_latent(qkv, cache_l2_k, cache_l2_v, cos_d, sin_d, diff_q_norm, diff_k_norm,
                       diff_lambda, diff_sub_norm)
    x = _out_proj_residual(xr, halves(o_p, o_s), mods, diff_w_o, "diff_out")
    x = _ffn(x, norm_ffn[2], mods, ffn_w_up[2], ffn_conv_w[2], ffn_conv_b[2], ffn_w_down[2])
    xr = _one_array(x)

    mods = mods_all[3]
    zc = lambda n: jnp.zeros((D_MODEL, n), F32)
    w_low = jnp.concatenate(
        [mla_w_dq, zc(MLA_LOW_KV - MLA_Q_RANK), mla_w_dkv[:, :MLA_KV_RANK],
         zc(MLA_NOPE), mla_w_dkv[:, MLA_KV_RANK:], zc(MLA_HL - MLA_QK)], axis=1)
    low = _norm_mod_proj(xr, norm_mix[3], mods, w_low, "mla_down")
    w_uq = _pad_heads(mla_w_uq, MLA_HEADS, MLA_QK, MLA_HL)
    qp, _ = _rms_matmul(low, 0, MLA_Q_RANK, mla_q_a_norm, w_uq, True, "mla_uq")
    kvp, ckv = _rms_matmul(low, MLA_LOW_KV // MLA_KV_RANK, MLA_KV_RANK, mla_kv_a_norm, mla_w_ukv,
                           True, "mla_ukv")
    kvc, _ = _rms_matmul(cache_l3_ckv.reshape(DEC_BATCH * PAST_LEN, MLA_KV_RANK), 0, MLA_KV_RANK,
                         mla_kv_a_norm, mla_w_ukv, False, "mla_ukv_cache")
    krc = jnp.pad(cache_l3_krope.reshape(DEC_BATCH * PAST_LEN, MLA_ROPE),
                  ((0, 0), (MLA_NOPE, MLA_HL - MLA_QK)))
    pad_gain = lambda g: jnp.tile(jnp.pad(g, (0, MLA_HL - MLA_QK)), MLA_HB).reshape(1, -1)
    gq, gk = pad_gain(mla_q_norm), pad_gain(mla_k_norm)
    o_p = _mla_prompt(qp, kvp, low, gq, gk)
    cos_m, sin_m = _mla_rope_tables()
    o_s = _mla_latent(qp, kvp, low, kvc, krc, cos_m, sin_m, gq, gk)
    new_l3_ckv = ckv[:N_PROMPT].reshape(BATCH, SEQ, MLA_KV_RANK)
    new_l3_krope = low[:N_PROMPT, MLA_LOW_KR + MLA_NOPE:MLA_LOW_KR + MLA_QK].reshape(
        BATCH, SEQ, MLA_ROPE)
    x = _out_proj_residual(xr, halves(o_p, o_s), mods, mla_w_o, "mla_out")
    n_pt = N_PROMPT // TOK_TILE
    ffn3 = functools.partial(_ffn, x, norm_ffn[3], mods, ffn_w_up[3], ffn_conv_w[3],
                             ffn_conv_b[3], ffn_w_down[3])
    y_prompt = ffn3(tile0=0, n_tiles=n_pt).reshape(BATCH, SEQ, D_MODEL)
    y_sample = ffn3(tile0=n_pt, n_tiles=N_TOK_TILES - n_pt).reshape(DEC_BATCH, DEC_SEQ, D_MODEL)
    return (y_prompt, y_sample, new_l0_k, new_l0_v, new_l1_fwd, new_l1_bwd, new_l2_k, new_l2_v,
            new_l3_ckv, new_l3_krope)
```

```python
import functools
import math
from typing import NamedTuple

import jax
import jax.numpy as jnp
from jax import lax
from jax.experimental import pallas as pl
from jax.experimental.pallas import tpu as pltpu

F32 = jnp.float32
BF16 = jnp.bfloat16

D_MODEL = 1024
BATCH = 16
SEQ = 256
DEPTH = 4
DEC_BATCH = 2
DEC_SEQ = 2048
PAST_LEN = 256
GRID_W = 64
D_FF = 2816
EPS = 1e-6
ROPE_BASE = 10000.0

NA_HEADS = 16
NA_HD = 64
NA_WIN_R = 8
NA_WIN_C = 16

GLA_HEADS = 4
GLA_DK = 128
GLA_DV = 256
GLA_HK = GLA_HEADS * GLA_DK
GLA_HV = GLA_HEADS * GLA_DV
GLA_GATE_RANK = 16
GLA_GATE_NORM = 16.0

DIFF_HEADS = 8
DIFF_HD = 64
DIFF_LAMBDA_INIT = 0.8 - 0.6 * math.exp(-0.3 * 2)

MLA_HEADS = 16
MLA_Q_RANK = 384
MLA_KV_RANK = 256
MLA_NOPE = 64
MLA_ROPE = 32
MLA_V = 64
MLA_QK = MLA_NOPE + MLA_ROPE

N_PROMPT = BATCH * SEQ
N_LATENT = DEC_BATCH * DEC_SEQ
N_TOK = N_PROMPT + N_LATENT
TOK_TILE = 2048
N_TOK_TILES = N_TOK // TOK_TILE
FF_CHUNK = 256
N_FF_CHUNKS = D_FF // FF_CHUNK
NEG = -1e30
LOG2E = math.log2(math.e)

VMEM_LIMIT = 56 * 1024 * 1024

_NT = (((1,), (1,)), ((), ()))
_TN = (((0,), (0,)), ((), ()))


def _params(sem, vmem=VMEM_LIMIT):
    return pltpu.CompilerParams(dimension_semantics=sem, vmem_limit_bytes=vmem)


def _log2(n):
    assert n & (n - 1) == 0
    return n.bit_length() - 1


def _silu(x):
    return x / (1.0 + jnp.exp(-x))


def _bdot(a, b):
    return jnp.dot(a, b, preferred_element_type=F32)


def _softmax_parts(parts):
    m = parts[0].max(axis=-1, keepdims=True)
    for s in parts[1:]:
        m = jnp.maximum(m, s.max(axis=-1, keepdims=True))
    ps = [jnp.exp2(s - m) for s in parts]
    l = ps[0].sum(axis=-1, keepdims=True)
    for p in ps[1:]:
        l = l + p.sum(axis=-1, keepdims=True)
    return ps, 1.0 / l


def _group_rms(x, gain, group, n_real=None, sums_on_mxu=False):
    lanes = x.shape[-1]
    n_real = n_real or group
    x2 = x * x
    if group == lanes:
        ms = jnp.sum(x2, axis=-1, keepdims=True)
    elif not sums_on_mxu:
        gid = lax.broadcasted_iota(jnp.int32, x.shape, 1) >> _log2(group)
        ms = jnp.zeros_like(x)
        for i in range(lanes // group):
            sel = gid == i
            si = jnp.sum(jnp.where(sel, x2, 0.0), axis=-1, keepdims=True)
            ms = jnp.where(sel, si, ms)
    else:
        r = lax.broadcasted_iota(jnp.int32, (lanes, lanes), 0) >> _log2(group)
        c = lax.broadcasted_iota(jnp.int32, (lanes, lanes), 1) >> _log2(group)
        ones = jnp.where(r == c, 1.0, 0.0).astype(BF16)
        hi = x2.astype(BF16)
        lo = (x2 - hi.astype(F32)).astype(BF16)
        ms = _bdot(hi, ones) + _bdot(lo, ones)
    return x * lax.rsqrt(ms * (1.0 / n_real) + EPS) * gain


def _rope(x, cos, sin, half):
    lanes = x.shape[-1]
    lane = lax.broadcasted_iota(jnp.int32, x.shape, 1)
    up = pltpu.roll(x, lanes - half, axis=1)
    dn = pltpu.roll(x, half, axis=1)
    swapped = jnp.where((lane & (2 * half - 1)) < half, up, dn)
    return x * cos + swapped * sin


def _stack_heads(q, n_heads, head_lanes):
    hid = lax.broadcasted_iota(jnp.int32, q.shape, 1) >> _log2(head_lanes)
    zero = jnp.zeros_like(q)
    return jnp.concatenate([jnp.where(hid == i, q, zero) for i in range(n_heads)], axis=0)


def _unstack_heads(o, n_heads, head_lanes):
    rows = o.shape[0] // n_heads
    hid = lax.broadcasted_iota(jnp.int32, (rows, o.shape[1]), 1) >> _log2(head_lanes)
    out = o[0:rows]
    for i in range(1, n_heads):
        out = jnp.where(hid == i, o[i * rows:(i + 1) * rows], out)
    return out


ADA_TN = 1536


def _ada_kernel(c_ref, w_ref, b_ref, o_ref):
    s = _silu(c_ref[...])
    o_ref[0] = jnp.dot(s, w_ref[0], preferred_element_type=F32,
                       precision=lax.Precision.HIGHEST) + b_ref[0]


def _ada_mods(cvecs, ada_w, ada_b):
    out = pl.pallas_call(
        _ada_kernel,
        grid=(DEPTH, 6 * D_MODEL // ADA_TN),
        in_specs=[pl.BlockSpec((8, D_MODEL), lambda l, j: (0, 0)),
                  pl.BlockSpec((1, D_MODEL, ADA_TN), lambda l, j: (l, 0, j)),
                  pl.BlockSpec((1, 1, ADA_TN), lambda l, j: (l, 0, j))],
        out_specs=pl.BlockSpec((1, 8, ADA_TN), lambda l, j: (l, 0, j)),
        out_shape=jax.ShapeDtypeStruct((DEPTH, 8, 6 * D_MODEL), F32),
        compiler_params=_params(("arbitrary", "arbitrary")),
        name="ada_mod",
    )(cvecs, ada_w, ada_b.reshape(DEPTH, 1, 6 * D_MODEL))
    return out.reshape(DEPTH, 8, 6, D_MODEL)[:, :3]


def _mod_group_of_tile(i):
    return jnp.maximum(i - (N_PROMPT // TOK_TILE - 1), 0)


def _norm_mod_rows(x_ref, g_ref, mod_ref, h_ref, shift_idx, scale_idx, rows=64):
    g = g_ref[...]
    sc = 1.0 + mod_ref[0, scale_idx:scale_idx + 1, :]
    sh = mod_ref[0, shift_idx:shift_idx + 1, :]

    def body(r, carry):
        sl = pl.ds(pl.multiple_of(r * rows, rows), rows)
        xf = x_ref[sl, :]
        ms = jnp.mean(xf * xf, axis=-1, keepdims=True)
        y = xf * lax.rsqrt(ms + EPS) * g
        h_ref[sl, :] = (y * sc + sh).astype(BF16)
        return carry

    lax.fori_loop(0, x_ref.shape[0] // rows, body, 0)


ROW_TM = 512


class _Rows(NamedTuple):
    prompt: jax.Array
    latent: jax.Array
    latent_row0: int


def _one_array(x):
    return _Rows(x, x, N_PROMPT)


def _row_specs(rows, width, col_block=0):
    n_p = N_PROMPT // ROW_TM
    l0 = rows.latent_row0 // ROW_TM
    return [pl.BlockSpec((ROW_TM, width), lambda t: (jnp.minimum(t, n_p - 1), col_block)),
            pl.BlockSpec((ROW_TM, width), lambda t: (l0 + jnp.maximum(t - n_p, 0), col_block))]


def _row_group(t):
    first_latent = N_PROMPT // ROW_TM
    return jnp.where(t < first_latent, 0, 1 + (t - first_latent) // (DEC_SEQ // ROW_TM))


def _is_prompt_tile():
    return pl.program_id(0) < N_PROMPT // ROW_TM


def _proj_kernel(xp_ref, xl_ref, g_ref, mod_ref, w_ref, o_ref, h_ref, wb_ref):
    @pl.when(pl.program_id(0) == 0)
    def _():
        wb_ref[...] = w_ref[...].astype(BF16)

    is_prompt = _is_prompt_tile()
    g = g_ref[...]
    sc = 1.0 + mod_ref[0, 1:2, :]
    sh = mod_ref[0, 0:1, :]
    rows = 64

    def body(r, carry):
        sl = pl.ds(pl.multiple_of(r * rows, rows), rows)
        xf = jnp.where(is_prompt, xp_ref[sl, :], xl_ref[sl, :])
        ms = jnp.mean(xf * xf, axis=-1, keepdims=True)
        y = xf * lax.rsqrt(ms + EPS) * g
        h_ref[sl, :] = (y * sc + sh).astype(BF16)
        return carry

    lax.fori_loop(0, ROW_TM // rows, body, 0)
    o_ref[...] = _bdot(h_ref[...], wb_ref[...])


def _norm_mod_proj(x, g, mods, w, name):
    n = w.shape[1]
    return pl.pallas_call(
        _proj_kernel,
        grid=(N_TOK // ROW_TM,),
        in_specs=_row_specs(x, D_MODEL) + [
            pl.BlockSpec((1, D_MODEL), lambda t: (0, 0)),
            pl.BlockSpec((1, 6, D_MODEL), lambda t: (_row_group(t), 0, 0)),
            pl.BlockSpec((D_MODEL, n), lambda t: (0, 0), pipeline_mode=pl.Buffered(1))],
        out_specs=pl.BlockSpec((ROW_TM, n), lambda t: (t, 0)),
        out_shape=jax.ShapeDtypeStruct((N_TOK, n), F32),
        scratch_shapes=[pltpu.VMEM((ROW_TM, D_MODEL), BF16),
                        pltpu.VMEM((D_MODEL, n), BF16)],
        compiler_params=_params(("arbitrary",)),
        name=name,
    )(x.prompt, x.latent, g.reshape(1, D_MODEL), mods, w)


def _rms_matmul_kernel(a_ref, g_ref, w_ref, o_ref, n_ref, *, normalise):
    a = a_ref[...]
    if normalise:
        a = a * lax.rsqrt(jnp.mean(a * a, axis=-1, keepdims=True) + EPS) * g_ref[...]
    n_ref[...] = a
    o_ref[...] = _bdot(a.astype(BF16), w_ref[...].astype(BF16))


def _rms_matmul(a, col_block, k, g, w, normalise, name, tm=512):
    rows, n = a.shape[0], w.shape[1]
    return pl.pallas_call(
        functools.partial(_rms_matmul_kernel, normalise=normalise),
        grid=(rows // tm,),
        in_specs=[pl.BlockSpec((tm, k), lambda i: (i, col_block)),
                  pl.BlockSpec((1, k), lambda i: (0, 0)),
                  pl.BlockSpec((k, n), lambda i: (0, 0))],
        out_specs=[pl.BlockSpec((tm, n), lambda i: (i, 0)),
                   pl.BlockSpec((tm, k), lambda i: (i, 0))],
        out_shape=[jax.ShapeDtypeStruct((rows, n), F32),
                   jax.ShapeDtypeStruct((rows, k), F32)],
        compiler_params=_params(("arbitrary",)),
        name=name,
    )(a, g.reshape(1, k), w)


def _oproj_kernel(*refs, gated):
    if gated:
        xp_ref, xl_ref, ap_ref, al_ref, g_ref, mod_ref, w_ref, o_ref, wb_ref = refs
    else:
        xp_ref, xl_ref, ap_ref, al_ref, mod_ref, w_ref, o_ref, wb_ref = refs

    @pl.when(pl.program_id(0) == 0)
    def _():
        wb_ref[...] = w_ref[...].astype(BF16)

    is_prompt = _is_prompt_tile()
    a = jnp.where(is_prompt, ap_ref[...], al_ref[...])
    if gated:
        a = a * _silu(g_ref[...])
    y = _bdot(a.astype(BF16), wb_ref[...])
    x = jnp.where(is_prompt, xp_ref[...], xl_ref[...])
    o_ref[...] = x + mod_ref[0, 2:3, :] * y


def _out_proj_residual(x, a, mods, w, name, gate=None, gate_col_block=0):
    k = w.shape[0]
    in_specs = _row_specs(x, D_MODEL) + _row_specs(a, k)
    args = [x.prompt, x.latent, a.prompt, a.latent]
    if gate is not None:
        in_specs.append(pl.BlockSpec((ROW_TM, k), lambda t: (t, gate_col_block)))
        args.append(gate)
    in_specs += [pl.BlockSpec((1, 6, D_MODEL), lambda t: (_row_group(t), 0, 0)),
                 pl.BlockSpec((k, D_MODEL), lambda t: (0, 0))]
    args += [mods, w]
    return pl.pallas_call(
        functools.partial(_oproj_kernel, gated=gate is not None),
        grid=(N_TOK // ROW_TM,),
        in_specs=in_specs,
        out_specs=pl.BlockSpec((ROW_TM, D_MODEL), lambda t: (t, 0)),
        out_shape=jax.ShapeDtypeStruct((N_TOK, D_MODEL), F32),
        scratch_shapes=[pltpu.VMEM((k, D_MODEL), BF16)],
        compiler_params=_params(("arbitrary",)),
        name=name,
    )(*args)


FFN_MM_ROWS = 512
FFN_ROWS = 64
FFN_PAD = 8


def _ffn_kernel(x_ref, g_ref, mod_ref, wg_ref, wv_ref, cwg_ref, cwv_ref, cbg_ref, cbv_ref,
                wd_ref, o_ref, h_ref, u_ref, act_ref, wup_ref, wdn_ref, *, tile0):
    i = tile0 + pl.program_id(0)
    c = pl.program_id(1)
    fc = FF_CHUNK

    @pl.when(c == 0)
    def _():
        _norm_mod_rows(x_ref, g_ref, mod_ref, h_ref, 3, 4)
        zeros = jnp.zeros((FFN_PAD, 2 * fc), F32)
        u_ref[0:FFN_PAD, :] = zeros
        u_ref[FFN_PAD + TOK_TILE:, :] = zeros
        o_ref[...] = jnp.zeros_like(o_ref)

    wup_ref[:, :fc] = wg_ref[...].astype(BF16)
    wup_ref[:, fc:] = wv_ref[...].astype(BF16)
    wdn_ref[...] = wd_ref[...].astype(BF16)
    cw_g, cw_v = cwg_ref[...], cwv_ref[...]
    cb_g, cb_v = cbg_ref[...], cbv_ref[...]
    seq_len = jnp.where(i < N_PROMPT // TOK_TILE, SEQ, DEC_SEQ)
    row = lax.broadcasted_iota(jnp.int32, (FFN_ROWS, 1), 0)

    def up(t):
        r0 = t * FFN_MM_ROWS
        u_ref[FFN_PAD + r0:FFN_PAD + r0 + FFN_MM_ROWS, :] = _bdot(
            h_ref[r0:r0 + FFN_MM_ROWS, :], wup_ref[...])

    def conv_act(t):
        for r0 in range(t * FFN_MM_ROWS, (t + 1) * FFN_MM_ROWS, FFN_ROWS):
            halves = []
            for lo, cw, cb in ((0, cw_g, cb_g), (fc, cw_v, cb_v)):
                p0 = FFN_PAD + r0
                prev = u_ref[p0 - 1:p0 - 1 + FFN_ROWS, lo:lo + fc]
                mid = u_ref[p0:p0 + FFN_ROWS, lo:lo + fc]
                nxt = u_ref[p0 + 1:p0 + 1 + FFN_ROWS, lo:lo + fc]
                if r0 % SEQ == 0:
                    prev = jnp.where(((r0 + row) & (seq_len - 1)) == 0, 0.0, prev)
                if (r0 + FFN_ROWS) % SEQ == 0:
                    nxt = jnp.where(((r0 + row) & (seq_len - 1)) == seq_len - 1, 0.0, nxt)
                halves.append(prev * cw[0:1] + mid * cw[1:2] + nxt * cw[2:3] + cb)
            act_ref[r0:r0 + FFN_ROWS, :] = (_silu(halves[0]) * halves[1]).astype(BF16)

    def down(t):
        r0 = t * FFN_MM_ROWS
        o_ref[r0:r0 + FFN_MM_ROWS, :] += _bdot(act_ref[r0:r0 + FFN_MM_ROWS, :], wdn_ref[...])

    n = TOK_TILE // FFN_MM_ROWS
    for s in range(n + 2):
        if s < n:
            up(s)
        if 1 <= s <= n:
            conv_act(s - 1)
        if s >= 2:
            down(s - 2)

    @pl.when(c == N_FF_CHUNKS - 1)
    def _():
        o_ref[...] = x_ref[...] + mod_ref[0, 5:6, :] * o_ref[...]


def _ffn(x, g, mods, w_up, conv_w, conv_b, w_down, tile0=0, n_tiles=N_TOK_TILES):
    fc = FF_CHUNK
    ncb = N_FF_CHUNKS
    return pl.pallas_call(
        functools.partial(_ffn_kernel, tile0=tile0),
        grid=(n_tiles, ncb),
        in_specs=[pl.BlockSpec((TOK_TILE, D_MODEL), lambda i, c: (tile0 + i, 0)),
                  pl.BlockSpec((1, D_MODEL), lambda i, c: (0, 0)),
                  pl.BlockSpec((1, 6, D_MODEL), lambda i, c: (_mod_group_of_tile(tile0 + i), 0, 0)),
                  pl.BlockSpec((D_MODEL, fc), lambda i, c: (0, c)),
                  pl.BlockSpec((D_MODEL, fc), lambda i, c: (0, ncb + c)),
                  pl.BlockSpec((3, fc), lambda i, c: (0, c)),
                  pl.BlockSpec((3, fc), lambda i, c: (0, ncb + c)),
                  pl.BlockSpec((1, fc), lambda i, c: (0, c)),
                  pl.BlockSpec((1, fc), lambda i, c: (0, ncb + c)),
                  pl.BlockSpec((fc, D_MODEL), lambda i, c: (c, 0))],
        out_specs=pl.BlockSpec((TOK_TILE, D_MODEL), lambda i, c: (i, 0)),
        out_shape=jax.ShapeDtypeStruct((n_tiles * TOK_TILE, D_MODEL), F32),
        scratch_shapes=[pltpu.VMEM((TOK_TILE, D_MODEL), BF16),
                        pltpu.VMEM((TOK_TILE + 2 * FFN_PAD, 2 * fc), F32),
                        pltpu.VMEM((TOK_TILE, fc), BF16),
                        pltpu.VMEM((D_MODEL, 2 * fc), BF16),
                        pltpu.VMEM((fc, D_MODEL), BF16)],
        compiler_params=_params(("arbitrary", "arbitrary")),
        name="conv_ffn",
    )(x, g.reshape(1, D_MODEL), mods, w_up, w_up, conv_w, conv_w,
      conv_b.reshape(1, -1), conv_b.reshape(1, -1), w_down)


NA_HB = 4
NA_LANES = NA_HB * NA_HD
NA_ROWS = DEC_SEQ // GRID_W
NA_KEYS = NA_WIN_R * GRID_W
PROMPT_SEQS = 4
NA_ROW_UNROLL = 2


def _store_heads(dst_ref, seq, x, n_heads, hd):
    for h in range(n_heads):
        dst_ref[seq, h] = x[:, h * hd:(h + 1) * hd]


def _na_prompt_kernel(q_ref, k_ref, v_ref, gq_ref, gk_ref, o_ref, kn_ref, vn_ref):
    scale = NA_HD ** -0.5 * LOG2E
    for seq in range(PROMPT_SEQS):
        sl = slice(seq * SEQ, (seq + 1) * SEQ)
        q = _group_rms(q_ref[sl, :], gq_ref[...], NA_HD, sums_on_mxu=True) * scale
        k = _group_rms(k_ref[sl, :], gk_ref[...], NA_HD, sums_on_mxu=True)
        v = v_ref[sl, :]
        _store_heads(kn_ref, seq, k, NA_HB, NA_HD)
        _store_heads(vn_ref, seq, v, NA_HB, NA_HD)
        q4 = _stack_heads(q, NA_HB, NA_HD).astype(BF16)
        s = lax.dot_general(q4, k.astype(BF16), _NT, preferred_element_type=F32)
        (p,), inv = _softmax_parts([s])
        o4 = _bdot(p.astype(BF16), v.astype(BF16)) * inv
        o_ref[sl, :] = _unstack_heads(o4, NA_HB, NA_HD)


def _na_prompt(qkv, gq, gk):
    nb = NA_HEADS // NA_HB
    rows = PROMPT_SEQS * SEQ
    blk = lambda off: pl.BlockSpec((rows, NA_LANES), lambda b, j: (b, off + j))
    vec = pl.BlockSpec((1, NA_LANES), lambda b, j: (0, 0))
    cache = pl.BlockSpec((PROMPT_SEQS, NA_HB, SEQ, NA_HD), lambda b, j: (b, j, 0, 0))
    cache_shape = jax.ShapeDtypeStruct((BATCH, NA_HEADS, SEQ, NA_HD), F32)
    return pl.pallas_call(
        _na_prompt_kernel,
        grid=(BATCH // PROMPT_SEQS, nb),
        in_specs=[blk(0), blk(nb), blk(2 * nb), vec, vec],
        out_specs=[blk(0), cache, cache],
        out_shape=[jax.ShapeDtypeStruct((N_PROMPT, D_MODEL), F32), cache_shape, cache_shape],
        compiler_params=_params(("arbitrary", "arbitrary")),
        name="na_prompt",
    )(qkv, qkv, qkv, jnp.tile(gq, NA_HB).reshape(1, -1), jnp.tile(gk, NA_HB).reshape(1, -1))


def _na_latent_kernel(q_ref, k_ref, v_ref, kc_ref, vc_ref, t_ref, gq_ref, gk_ref, o_ref,
                      qn_ref, kn_ref, vb_ref, kc4_ref, vc4_ref, bias_ref):
    scale = NA_HD ** -0.5 * LOG2E
    rows = 256

    for h in range(NA_HB):
        for p in range(NA_WIN_R):
            for i in range(NA_WIN_R):
                bias_ref[h, p, :, i * GRID_W:(i + 1) * GRID_W] = t_ref[h, p + i]

    def prep(r, carry):
        sl = pl.ds(pl.multiple_of(r * rows, rows), rows)
        qn_ref[sl, :] = (_group_rms(q_ref[sl, :], gq_ref[...], NA_HD, sums_on_mxu=True)
                         * scale).astype(BF16)
        kn_ref[sl, :] = _group_rms(k_ref[sl, :], gk_ref[...], NA_HD,
                                   sums_on_mxu=True).astype(BF16)
        vb_ref[sl, :] = v_ref[sl, :].astype(BF16)
        return carry

    lax.fori_loop(0, DEC_SEQ // rows, prep, 0)
    kc4_ref[...] = kc_ref[0].astype(BF16)
    vc4_ref[...] = vc_ref[0].astype(BF16)

    def row(r):
        kr0 = jnp.clip(r - NA_WIN_R // 2, 0, NA_ROWS - NA_WIN_R)
        pat = kr0 - r + NA_WIN_R - 1
        qs = pl.ds(pl.multiple_of(r * GRID_W, GRID_W), GRID_W)
        ks = pl.ds(pl.multiple_of(kr0 * GRID_W, GRID_W), NA_KEYS)
        q4 = _stack_heads(qn_ref[qs, :], NA_HB, NA_HD)
        s_loc = lax.dot_general(q4, kn_ref[ks, :], _NT, preferred_element_type=F32)
        bias = jnp.concatenate([bias_ref[h, pat] for h in range(NA_HB)], axis=0)
        s_loc = s_loc + bias
        s_ctx = lax.dot_general(q4, kc4_ref[...], _NT, preferred_element_type=F32)
        (p_loc, p_ctx), inv = _softmax_parts([s_loc, s_ctx])
        o4 = _bdot(p_loc.astype(BF16), vb_ref[ks, :]) + _bdot(p_ctx.astype(BF16), vc4_ref[...])
        o_ref[qs, :] = _unstack_heads(o4 * inv, NA_HB, NA_HD)

    def rows_step(i, carry):
        for u in range(NA_ROW_UNROLL):
            row(i * NA_ROW_UNROLL + u)
        return carry

    lax.fori_loop(0, NA_ROWS // NA_ROW_UNROLL, rows_step, 0)


def _na_bias_blocks(bias_table):
    qc = jnp.arange(GRID_W)[:, None]
    kc = jnp.arange(GRID_W)[None, :]
    win0 = jnp.clip(qc - NA_WIN_C // 2, 0, GRID_W - NA_WIN_C)
    valid = (kc >= win0) & (kc < win0 + NA_WIN_C)
    n_ro, n_co = bias_table.shape[1:]
    c = NA_WIN_C - 1
    period = jnp.concatenate(
        [bias_table[..., c:], jnp.zeros((NA_HEADS, n_ro, 2 * GRID_W - n_co), F32),
         bias_table[..., :c]], axis=-1)
    flat = jnp.tile(period, (1, 1, GRID_W))[..., :GRID_W * (2 * GRID_W - 1)]
    t = flat.reshape(NA_HEADS, n_ro, GRID_W, 2 * GRID_W - 1)[..., :GRID_W]
    return jnp.where(valid, t * LOG2E, NEG)


def _na_latent(qkv, cache_k, cache_v, bias_blocks, gq, gk):
    nb = NA_HEADS // NA_HB
    lat0 = N_PROMPT // DEC_SEQ
    blk = lambda off: pl.BlockSpec((DEC_SEQ, NA_LANES), lambda b, j: (lat0 + b, off + j))
    vec = pl.BlockSpec((1, NA_LANES), lambda b, j: (0, 0))
    cache = pl.BlockSpec((1, PAST_LEN, NA_LANES), lambda b, j: (b, 0, j))
    return pl.pallas_call(
        _na_latent_kernel,
        grid=(DEC_BATCH, nb),
        in_specs=[blk(0), blk(nb), blk(2 * nb), cache, cache,
                  pl.BlockSpec((NA_HB, 2 * NA_WIN_R - 1, GRID_W, GRID_W), lambda b, j: (j, 0, 0, 0)),
                  vec, vec],
        out_specs=pl.BlockSpec((DEC_SEQ, NA_LANES), lambda b, j: (b, j)),
        out_shape=jax.ShapeDtypeStruct((N_LATENT, D_MODEL), F32),
        scratch_shapes=[pltpu.VMEM((DEC_SEQ, NA_LANES), BF16),
                        pltpu.VMEM((DEC_SEQ, NA_LANES), BF16),
                        pltpu.VMEM((DEC_SEQ, NA_LANES), BF16),
                        pltpu.VMEM((PAST_LEN, NA_LANES), BF16),
                        pltpu.VMEM((PAST_LEN, NA_LANES), BF16),
                        pltpu.VMEM((NA_HB, NA_WIN_R, GRID_W, NA_KEYS), F32)],
        compiler_params=_params(("arbitrary", "arbitrary")),
        name="na_latent",
    )(qkv, qkv, qkv, _tokens_first(cache_k), _tokens_first(cache_v), bias_blocks,
      jnp.tile(gq, NA_HB).reshape(1, -1), jnp.tile(gk, NA_HB).reshape(1, -1))


GLA_C = 128
GLA_SUB = 16
GLA_LEVELS = (64, 32, 16)


def _split_hi_lo(x):
    hi = x.astype(BF16)
    lo = (x - hi.astype(F32)).astype(BF16)
    return jnp.concatenate([hi, lo], axis=1)


def _gla_chunk(q, k, v, g, st_ref, rev):
    c = GLA_C
    row = lax.broadcasted_iota(jnp.int32, (c, c), 0)
    col = lax.broadcasted_iota(jnp.int32, (c, c), 1)
    tri = (col >= row) if rev else (col <= row)
    cs = _bdot(jnp.where(tri, 1.0, 0.0).astype(BF16), _split_hi_lo(g))
    b = cs[:, :GLA_DK] + cs[:, GLA_DK:]
    causal = (col >= row) if rev else (col <= row)

    a = jnp.zeros((c, c), F32)
    rid = lax.broadcasted_iota(jnp.int32, (c, GLA_DK), 0)
    for m in GLA_LEVELS:
        nblk = c // (2 * m)
        if rev:
            bnd = [b[j * 2 * m + m:j * 2 * m + m + 1] for j in range(nblk)]
        else:
            bnd = [b[j * 2 * m + m - 1:j * 2 * m + m] for j in range(nblk)]
        ref = jnp.concatenate([jnp.broadcast_to(x, (2 * m, GLA_DK)) for x in bnd], axis=0)
        later = ((rid & m) == 0) if rev else ((rid & m) != 0)
        dq = jnp.minimum(b - ref, 0.0)
        dk_ = jnp.minimum(ref - b, 0.0)
        qh = jnp.where(later, q * jnp.exp(dq), 0.0).astype(BF16)
        kh = jnp.where(later, 0.0, k * jnp.exp(dk_)).astype(BF16)
        blk = lax.dot_general(qh, kh, _NT, preferred_element_type=F32)
        same = (row >> _log2(2 * m)) == (col >> _log2(2 * m))
        a = a + jnp.where(same, blk, 0.0)

    nsub = c // GLA_SUB
    lane_c = lax.broadcasted_iota(jnp.int32, (GLA_SUB, c), 1)
    srow = lax.broadcasted_iota(jnp.int32, (GLA_SUB, 1), 0)
    diag_rows = []
    for blk_i in range(nsub):
        r0 = blk_i * GLA_SUB
        qb = q[r0:r0 + GLA_SUB]
        bb = b[r0:r0 + GLA_SUB]
        acc = jnp.zeros((GLA_SUB, c), F32)
        for s in range(GLA_SUB):
            ks = k[r0 + s:r0 + s + 1]
            bs = b[r0 + s:r0 + s + 1]
            w = jnp.sum(qb * ks * jnp.exp(jnp.minimum(bb - bs, 0.0)), axis=-1, keepdims=True)
            keep = (srow <= s) if rev else (srow >= s)
            w = jnp.where(keep, w, 0.0)
            acc = jnp.where(lane_c == r0 + s, w, acc)
        diag_rows.append(acc)
    a = a + jnp.concatenate(diag_rows, axis=0)
    a = jnp.where(causal, a, 0.0)

    st = st_ref[...]
    inter = lax.dot_general((q * jnp.exp(b)).astype(BF16), st.astype(BF16), _NT,
                            preferred_element_type=F32)
    o = inter + _bdot(a.astype(BF16), v.astype(BF16))

    btot = b[0:1] if rev else b[c - 1:c]
    kd = (k * jnp.exp(btot - b)).astype(BF16)
    st_ref[...] = st * jnp.exp(btot) + lax.dot_general(v.astype(BF16), kd, _TN,
                                                       preferred_element_type=F32)
    return o


def _gla_kernel(*refs, n_tok, has_state):
    if has_state:
        (q_ref, k_ref, v_ref, r_ref, w2_ref, bg_ref, gn_ref, s0f_ref, s0b_ref,
         o_ref, lg_ref, of_ref, ob_ref, stf_ref, stb_ref) = refs
    else:
        (q_ref, k_ref, v_ref, r_ref, w2_ref, bg_ref, gn_ref,
         o_ref, sf_ref, sb_ref, lg_ref, of_ref, ob_ref, stf_ref, stb_ref) = refs
    nc = n_tok // GLA_C
    scale = GLA_DK ** -0.5

    rb = r_ref[...].astype(BF16)
    for z in range(2):
        x = _bdot(rb, w2_ref[z].astype(BF16)) + bg_ref[z]
        lg_ref[z] = (jnp.minimum(x, 0.0) - jnp.log1p(jnp.exp(-jnp.abs(x)))) * (1.0 / GLA_GATE_NORM)

    if has_state:
        stf_ref[...] = s0f_ref[0, 0].T
        stb_ref[...] = s0b_ref[0, 0].T
    else:
        stf_ref[...] = jnp.zeros_like(stf_ref)
        stb_ref[...] = jnp.zeros_like(stb_ref)

    def step(ci, carry):
        for rev in (False, True):
            cc = (nc - 1 - ci) if rev else ci
            sl = pl.ds(pl.multiple_of(cc * GLA_C, GLA_C), GLA_C)
            q = q_ref[sl, :] * scale
            o = _gla_chunk(q, k_ref[sl, :], v_ref[sl, :], lg_ref[1 if rev else 0, sl, :],
                           stb_ref if rev else stf_ref, rev)
            (ob_ref if rev else of_ref)[sl, :] = o
        return carry

    lax.fori_loop(0, nc, step, 0)

    o = of_ref[...] + ob_ref[...]
    o_ref[...] = o * lax.rsqrt(jnp.mean(o * o, axis=-1, keepdims=True) + EPS) * gn_ref[...]
    if not has_state:
        sf_ref[0, 0] = stf_ref[...].T
        sb_ref[0, 0] = stb_ref[...].T


def _gla(proj, w2, bg, gnorm, n_seq, n_tok, row_block0, states=None):
    qb = GLA_HK // GLA_DK
    spec = lambda width, off: pl.BlockSpec((n_tok, width), lambda b, h: (row_block0 + b, off + h))
    in_specs = [spec(GLA_DK, 0), spec(GLA_DK, qb), spec(GLA_DV, 2 * GLA_HK // GLA_DV),
                pl.BlockSpec((n_tok, 128), lambda b, h: (row_block0 + b, (2 * GLA_HK + 2 * GLA_HV) // 128)),
                pl.BlockSpec((2, 128, GLA_DK), lambda b, h: (0, 0, h)),
                pl.BlockSpec((2, 1, GLA_DK), lambda b, h: (0, 0, h)),
                pl.BlockSpec((1, GLA_DV), lambda b, h: (0, 0))]
    args = [proj, proj, proj, proj, w2, bg, gnorm.reshape(1, GLA_DV)]
    st_spec = pl.BlockSpec((1, 1, GLA_DK, GLA_DV), lambda b, h: (b, h, 0, 0))
    o_spec = pl.BlockSpec((n_tok, GLA_DV), lambda b, h: (b, h))
    o_shape = jax.ShapeDtypeStruct((n_seq * n_tok, GLA_HV), F32)
    if states is not None:
        in_specs += [st_spec, st_spec]
        args += list(states)
        out_specs, out_shape = o_spec, o_shape
    else:
        st_shape = jax.ShapeDtypeStruct((n_seq, GLA_HEADS, GLA_DK, GLA_DV), F32)
        out_specs, out_shape = [o_spec, st_spec, st_spec], [o_shape, st_shape, st_shape]
    return pl.pallas_call(
        functools.partial(_gla_kernel, n_tok=n_tok, has_state=states is not None),
        grid=(n_seq, GLA_HEADS),
        in_specs=in_specs,
        out_specs=out_specs,
        out_shape=out_shape,
        scratch_shapes=[pltpu.VMEM((2, n_tok, GLA_DK), F32),
                        pltpu.VMEM((n_tok, GLA_DV), F32),
                        pltpu.VMEM((n_tok, GLA_DV), F32),
                        pltpu.VMEM((GLA_DV, GLA_DK), F32),
                        pltpu.VMEM((GLA_DV, GLA_DK), F32)],
        compiler_params=_params(("arbitrary", "arbitrary")),
        name="gla_latent" if states is not None else "gla_prompt",
    )(*args)


DIFF_HB = 2
DIFF_QL = DIFF_HB * DIFF_HD
DIFF_VL = DIFF_HB * 2 * DIFF_HD
DIFF_TQ = 512
DIFF_SUB = 128


def _diff_lambda(lam_ref):
    l = lam_ref[...]
    a = jnp.sum(l[0:1] * l[1:2], axis=-1, keepdims=True)
    b = jnp.sum(l[2:3] * l[3:4], axis=-1, keepdims=True)
    return jnp.exp(a) - jnp.exp(b) + DIFF_LAMBDA_INIT


def _diff_finish(ps, invs, lam, v, sn_ref):
    a = ps[0] - (lam * invs[1] / invs[0]) * ps[1]
    o2 = _bdot(a.astype(BF16), v) * invs[0]
    o = _unstack_heads(o2, DIFF_HB, 2 * DIFF_HD)
    return _group_rms(o, sn_ref[...], 2 * DIFF_HD) * (1.0 - DIFF_LAMBDA_INIT)


def _diff_prompt_kernel(q0_ref, q1_ref, k0_ref, k1_ref, v_ref, gq_ref, gk_ref, lam_ref, sn_ref,
                        o_ref, kn_ref, vn_ref):
    scale = DIFF_HD ** -0.5 * LOG2E
    lam = _diff_lambda(lam_ref)
    for seq in range(PROMPT_SEQS):
        sl = slice(seq * SEQ, (seq + 1) * SEQ)
        ps, invs = [], []
        for comp, (q_ref, k_ref) in enumerate(((q0_ref, k0_ref), (q1_ref, k1_ref))):
            q = _group_rms(q_ref[sl, :], gq_ref[...], DIFF_HD, sums_on_mxu=True) * scale
            k = _group_rms(k_ref[sl, :], gk_ref[...], DIFF_HD, sums_on_mxu=True)
            for h in range(DIFF_HB):
                kn_ref[seq, comp, h] = k[:, h * DIFF_HD:(h + 1) * DIFF_HD]
            q2 = _stack_heads(q, DIFF_HB, DIFF_HD).astype(BF16)
            s = lax.dot_general(q2, k.astype(BF16), _NT, preferred_element_type=F32)
            (p,), inv = _softmax_parts([s])
            ps.append(p)
            invs.append(inv)
        v = v_ref[sl, :]
        _store_heads(vn_ref, seq, v, DIFF_HB, 2 * DIFF_HD)
        o_ref[sl, :] = _diff_finish(ps, invs, lam, v.astype(BF16), sn_ref)


def _diff_prompt(qkv, gq, gk, lam, sub_norm):
    nb = DIFF_HEADS // DIFF_HB
    rows = PROMPT_SEQS * SEQ
    qk = lambda off: pl.BlockSpec((rows, DIFF_QL), lambda b, j: (b, off + j))
    vec = lambda n: pl.BlockSpec((1, n), lambda b, j: (0, 0))
    v_spec = pl.BlockSpec((rows, DIFF_VL), lambda b, j: (b, 2 * D_MODEL // DIFF_VL + j))
    kn_spec = pl.BlockSpec((PROMPT_SEQS, 2, DIFF_HB, SEQ, DIFF_HD), lambda b, j: (b, 0, j, 0, 0))
    vn_spec = pl.BlockSpec((PROMPT_SEQS, DIFF_HB, SEQ, 2 * DIFF_HD), lambda b, j: (b, j, 0, 0))
    return pl.pallas_call(
        _diff_prompt_kernel,
        grid=(BATCH // PROMPT_SEQS, nb),
        in_specs=[qk(0), qk(nb), qk(2 * nb), qk(3 * nb), v_spec, vec(DIFF_QL), vec(DIFF_QL),
                  pl.BlockSpec((4, DIFF_HD), lambda b, j: (0, 0)), vec(DIFF_VL)],
        out_specs=[pl.BlockSpec((rows, DIFF_VL), lambda b, j: (b, j)), kn_spec, vn_spec],
        out_shape=[jax.ShapeDtypeStruct((N_PROMPT, D_MODEL), F32),
                   jax.ShapeDtypeStruct((BATCH, 2, DIFF_HEADS, SEQ, DIFF_HD), F32),
                   jax.ShapeDtypeStruct((BATCH, DIFF_HEADS, SEQ, 2 * DIFF_HD), F32)],
        compiler_params=_params(("arbitrary", "arbitrary")),
        name="diff_prompt",
    )(qkv, qkv, qkv, qkv, qkv, jnp.tile(gq, DIFF_HB).reshape(1, -1),
      jnp.tile(gk, DIFF_HB).reshape(1, -1), lam, jnp.tile(sub_norm, DIFF_HB).reshape(1, -1))


def _diff_latent_kernel(q0_ref, q1_ref, k0_ref, k1_ref, v_ref, kc0_ref, kc1_ref, vc_ref,
                        cos_ref, sin_ref, cosq_ref, sinq_ref, gq_ref, gk_ref, lam_ref, sn_ref,
                        o_ref, kb0_ref, kb1_ref, vb_ref):
    scale = DIFF_HD ** -0.5 * LOG2E
    half = DIFF_HD // 4
    rows = 256

    @pl.when(pl.program_id(2) == 0)
    def _():
        for k_ref, kc_ref, kb_ref in ((k0_ref, kc0_ref, kb0_ref), (k1_ref, kc1_ref, kb1_ref)):
            kb_ref[0:PAST_LEN, :] = kc_ref[0].astype(BF16)

            def prep(r, carry):
                sl = pl.ds(pl.multiple_of(r * rows, rows), rows)
                k = _group_rms(k_ref[sl, :], gk_ref[...], DIFF_HD)
                k = _rope(k, cos_ref[sl, :], sin_ref[sl, :], half)
                kb_ref[pl.ds(pl.multiple_of(PAST_LEN + r * rows, rows), rows), :] = k.astype(BF16)
                return carry

            lax.fori_loop(0, DEC_SEQ // rows, prep, 0)
        vb_ref[0:PAST_LEN, :] = vc_ref[0].astype(BF16)
        vb_ref[PAST_LEN:, :] = v_ref[...].astype(BF16)

    lam = _diff_lambda(lam_ref)
    for r0 in range(0, DIFF_TQ, DIFF_SUB):
        sl = slice(r0, r0 + DIFF_SUB)
        ps, invs = [], []
        for q_ref, kb_ref in ((q0_ref, kb0_ref), (q1_ref, kb1_ref)):
            q = _group_rms(q_ref[sl, :], gq_ref[...], DIFF_HD)
            q = _rope(q, cosq_ref[sl, :], sinq_ref[sl, :], half) * scale
            q2 = _stack_heads(q, DIFF_HB, DIFF_HD).astype(BF16)
            s = lax.dot_general(q2, kb_ref[...], _NT, preferred_element_type=F32)
            (p,), inv = _softmax_parts([s])
            ps.append(p)
            invs.append(inv)
        o_ref[sl, :] = _diff_finish(ps, invs, lam, vb_ref[...], sn_ref)


def _diff_latent(qkv, cache_k, cache_v, cos, sin, gq, gk, lam, sub_norm):
    nb = DIFF_HEADS // DIFF_HB
    nq = DEC_SEQ // DIFF_TQ
    q0 = N_PROMPT // DIFF_TQ
    lat0 = N_PROMPT // DEC_SEQ
    n_keys = PAST_LEN + DEC_SEQ
    q_spec = lambda off: pl.BlockSpec((DIFF_TQ, DIFF_QL), lambda b, j, t: (q0 + b * nq + t, off + j))
    k_spec = lambda off: pl.BlockSpec((DEC_SEQ, DIFF_QL), lambda b, j, t: (lat0 + b, off + j))
    v_spec = pl.BlockSpec((DEC_SEQ, DIFF_VL), lambda b, j, t: (lat0 + b, 2 * D_MODEL // DIFF_VL + j))
    kc_spec = lambda off: pl.BlockSpec((1, PAST_LEN, DIFF_QL), lambda b, j, t: (b, 0, off + j))
    vc_spec = pl.BlockSpec((1, PAST_LEN, DIFF_VL), lambda b, j, t: (b, 0, j))
    tab = pl.BlockSpec((DEC_SEQ, DIFF_QL), lambda b, j, t: (0, 0))
    tabq = pl.BlockSpec((DIFF_TQ, DIFF_QL), lambda b, j, t: (t, 0))
    vec = lambda n: pl.BlockSpec((1, n), lambda b, j, t: (0, 0))
    return pl.pallas_call(
        _diff_latent_kernel,
        grid=(DEC_BATCH, nb, nq),
        in_specs=[q_spec(0), q_spec(nb), k_spec(2 * nb), k_spec(3 * nb), v_spec,
                  kc_spec(0), kc_spec(nb), vc_spec, tab, tab, tabq, tabq,
                  vec(DIFF_QL), vec(DIFF_QL),
                  pl.BlockSpec((4, DIFF_HD), lambda b, j, t: (0, 0)), vec(DIFF_VL)],
        out_specs=pl.BlockSpec((DIFF_TQ, DIFF_VL), lambda b, j, t: (b * nq + t, j)),
        out_shape=jax.ShapeDtypeStruct((N_LATENT, D_MODEL), F32),
        scratch_shapes=[pltpu.VMEM((n_keys, DIFF_QL), BF16),
                        pltpu.VMEM((n_keys, DIFF_QL), BF16),
                        pltpu.VMEM((n_keys, DIFF_VL), BF16)],
        compiler_params=_params(("arbitrary", "arbitrary", "arbitrary")),
        name="diff_latent",
    )(qkv, qkv, qkv, qkv, qkv, _tokens_first(cache_k), _tokens_first(cache_k),
      _tokens_first(cache_v), cos, sin, cos, sin,
      jnp.tile(gq, DIFF_HB).reshape(1, -1), jnp.tile(gk, DIFF_HB).reshape(1, -1), lam,
      jnp.tile(sub_norm, DIFF_HB).reshape(1, -1))


MLA_HB = 2
MLA_HL = 128
MLA_LANES = MLA_HB * MLA_HL
MLA_TQ = 512
MLA_SUB = 128


def _mla_keys(kv, kr, gk):
    lane = lax.broadcasted_iota(jnp.int32, kv.shape, 1)
    kr2 = jnp.concatenate([kr] * MLA_HB, axis=1)
    k = jnp.where((lane & (MLA_HL - 1)) < MLA_NOPE, kv, kr2)
    return _group_rms(k, gk, MLA_HL, n_real=MLA_QK)


def _mla_out(o2):
    tq = o2.shape[0] // MLA_HB
    oa = pltpu.roll(o2[0:tq, 0:MLA_HL], MLA_HL - MLA_V, axis=1)
    ob = o2[tq:, MLA_HL:]
    lane = lax.broadcasted_iota(jnp.int32, oa.shape, 1)
    return jnp.where(lane < MLA_V, oa, ob)


def _mla_prompt_kernel(q_ref, kv_ref, kr_ref, gq_ref, gk_ref, o_ref):
    scale = MLA_QK ** -0.5 * LOG2E
    for seq in range(PROMPT_SEQS):
        sl = slice(seq * SEQ, (seq + 1) * SEQ)
        q = _group_rms(q_ref[sl, :], gq_ref[...], MLA_HL, n_real=MLA_QK) * scale
        kv = kv_ref[sl, :]
        k = _mla_keys(kv, kr_ref[sl, :], gk_ref[...])
        q2 = _stack_heads(q, MLA_HB, MLA_HL).astype(BF16)
        s = lax.dot_general(q2, k.astype(BF16), _NT, preferred_element_type=F32)
        (p,), inv = _softmax_parts([s])
        o_ref[sl, :] = _mla_out(_bdot(p.astype(BF16), kv.astype(BF16)) * inv)


def _mla_prompt(qp, kvp, low, gq, gk):
    nb = MLA_HEADS // MLA_HB
    rows = PROMPT_SEQS * SEQ
    blk = pl.BlockSpec((rows, MLA_LANES), lambda b, j: (b, j))
    vec = pl.BlockSpec((1, MLA_LANES), lambda b, j: (0, 0))
    return pl.pallas_call(
        _mla_prompt_kernel,
        grid=(BATCH // PROMPT_SEQS, nb),
        in_specs=[blk, blk, pl.BlockSpec((rows, MLA_HL), lambda b, j: (b, MLA_LOW_KR // MLA_HL)),
                  vec, vec],
        out_specs=pl.BlockSpec((rows, MLA_HB * MLA_V), lambda b, j: (b, j)),
        out_shape=jax.ShapeDtypeStruct((N_PROMPT, MLA_HEADS * MLA_V), F32),
        compiler_params=_params(("arbitrary", "arbitrary")),
        name="mla_prompt",
    )(qp, kvp, low, gq, gk)


def _mla_latent_kernel(q_ref, kv_ref, kr_ref, kvc_ref, krc_ref, cos_ref, sin_ref, cosq_ref,
                       sinq_ref, gq_ref, gk_ref, o_ref, kb_ref, vb_ref):
    scale = MLA_QK ** -0.5 * LOG2E
    half = MLA_ROPE // 4
    rows = 256

    @pl.when(pl.program_id(2) == 0)
    def _():
        kvc = kvc_ref[...]
        kb_ref[0:PAST_LEN, :] = _mla_keys(kvc, krc_ref[...], gk_ref[...]).astype(BF16)
        vb_ref[0:PAST_LEN, :] = kvc.astype(BF16)

        def prep(r, carry):
            sl = pl.ds(pl.multiple_of(r * rows, rows), rows)
            dst = pl.ds(pl.multiple_of(PAST_LEN + r * rows, rows), rows)
            kv = kv_ref[sl, :]
            k = _mla_keys(kv, kr_ref[sl, :], gk_ref[...])
            k = _rope(k, cos_ref[sl, :], sin_ref[sl, :], half)
            kb_ref[dst, :] = k.astype(BF16)
            vb_ref[dst, :] = kv.astype(BF16)
            return carry

        lax.fori_loop(0, DEC_SEQ // rows, prep, 0)

    for r0 in range(0, MLA_TQ, MLA_SUB):
        sl = slice(r0, r0 + MLA_SUB)
        q = _group_rms(q_ref[sl, :], gq_ref[...], MLA_HL, n_real=MLA_QK)
        q = _rope(q, cosq_ref[sl, :], sinq_ref[sl, :], half) * scale
        q2 = _stack_heads(q, MLA_HB, MLA_HL).astype(BF16)
        s = lax.dot_general(q2, kb_ref[...], _NT, preferred_element_type=F32)
        (p,), inv = _softmax_parts([s])
        o_ref[sl, :] = _mla_out(_bdot(p.astype(BF16), vb_ref[...]) * inv)


def _mla_latent(qp, kvp, low, kvc, krc, cos, sin, gq, gk):
    nb = MLA_HEADS // MLA_HB
    nq = DEC_SEQ // MLA_TQ
    q0 = N_PROMPT // MLA_TQ
    lat0 = N_PROMPT // DEC_SEQ
    n_keys = PAST_LEN + DEC_SEQ
    tab = pl.BlockSpec((DEC_SEQ, MLA_LANES), lambda b, j, t: (0, 0))
    tabq = pl.BlockSpec((MLA_TQ, MLA_LANES), lambda b, j, t: (t, 0))
    vec = pl.BlockSpec((1, MLA_LANES), lambda b, j, t: (0, 0))
    return pl.pallas_call(
        _mla_latent_kernel,
        grid=(DEC_BATCH, nb, nq),
        in_specs=[pl.BlockSpec((MLA_TQ, MLA_LANES), lambda b, j, t: (q0 + b * nq + t, j)),
                  pl.BlockSpec((DEC_SEQ, MLA_LANES), lambda b, j, t: (lat0 + b, j)),
                  pl.BlockSpec((DEC_SEQ, MLA_HL), lambda b, j, t: (lat0 + b, MLA_LOW_KR // MLA_HL)),
                  pl.BlockSpec((PAST_LEN, MLA_LANES), lambda b, j, t: (b, j)),
                  pl.BlockSpec((PAST_LEN, MLA_HL), lambda b, j, t: (b, 0)),
                  tab, tab, tabq, tabq, vec, vec],
        out_specs=pl.BlockSpec((MLA_TQ, MLA_HB * MLA_V), lambda b, j, t: (b * nq + t, j)),
        out_shape=jax.ShapeDtypeStruct((N_LATENT, MLA_HEADS * MLA_V), F32),
        scratch_shapes=[pltpu.VMEM((n_keys, MLA_LANES), BF16),
                        pltpu.VMEM((n_keys, MLA_LANES), BF16)],
        compiler_params=_params(("arbitrary", "arbitrary", "arbitrary")),
        name="mla_latent",
    )(qp, kvp, low, kvc, krc, cos, sin, cos, sin, gq, gk)


MLA_LOW_Q = 0
MLA_LOW_KV = 512
MLA_LOW_KR = 768
MLA_LOW_N = 896


def _axial_tables(n_tok, rdim):
    nf = rdim // 4
    freqs = ROPE_BASE ** (-jnp.arange(nf, dtype=F32) / nf)
    t = jnp.arange(n_tok)
    rowp = (t // GRID_W).astype(F32)
    colp = (t % GRID_W).astype(F32)
    ang = jnp.stack([rowp[:, None] * freqs, colp[:, None] * freqs], axis=1)
    cos, sin = jnp.cos(ang), jnp.sin(ang)
    cos_l = jnp.stack([cos, cos], axis=2).reshape(n_tok, rdim)
    sin_l = jnp.stack([-sin, sin], axis=2).reshape(n_tok, rdim)
    return cos_l, sin_l


def _diff_rope_tables():
    cos, sin = _axial_tables(DEC_SEQ, DIFF_HD)
    return jnp.tile(cos, (1, DIFF_HB)), jnp.tile(sin, (1, DIFF_HB))


def _mla_rope_tables():
    cos, sin = _axial_tables(DEC_SEQ, MLA_ROPE)
    ones = jnp.ones((DEC_SEQ, MLA_NOPE), F32)
    pad1 = jnp.ones((DEC_SEQ, MLA_HL - MLA_QK), F32)
    cos_h = jnp.concatenate([ones, cos, pad1], axis=1)
    sin_h = jnp.concatenate([0 * ones, sin, 0 * pad1], axis=1)
    return jnp.tile(cos_h, (1, MLA_HB)), jnp.tile(sin_h, (1, MLA_HB))


def _tokens_first(cache):
    b, h, l, d = cache.shape
    return jnp.transpose(cache, (0, 2, 1, 3)).reshape(b, l, h * d)


def _pad_heads(w, heads, hd, hl):
    k = w.shape[0]
    return jnp.pad(w.reshape(k, heads, hd), ((0, 0), (0, 0), (0, hl - hd))).reshape(k, heads * hl)


def kernel(x_prompt, x_sample, cache_l0_k, cache_l0_v, state_l1_fwd, state_l1_bwd, cache_l2_k,
           cache_l2_v, cache_l3_ckv, cache_l3_krope, c, c_ctx, ada_w, ada_b, norm_mix, norm_ffn,
           ffn_w_up, ffn_conv_w, ffn_conv_b, ffn_w_down, na_w_qkv, na_q_norm, na_k_norm, na_bias,
           na_w_o, gla_w_qkvg, gla_w_gate1, gla_w_gate2, gla_b_gate, gla_o_norm, gla_w_o,
           diff_w_qkv, diff_q_norm, diff_k_norm, diff_lambda, diff_sub_norm, diff_w_o, mla_w_dq,
           mla_q_a_norm, mla_w_uq, mla_w_dkv, mla_kv_a_norm, mla_w_ukv, mla_q_norm, mla_k_norm,
           mla_w_o):
    xr = _Rows(x_prompt.reshape(N_PROMPT, D_MODEL), x_sample.reshape(N_LATENT, D_MODEL), 0)
    cvecs = jnp.concatenate([c_ctx[None], c, jnp.zeros((5, D_MODEL), F32)], axis=0)
    mods_all = _ada_mods(cvecs, ada_w, ada_b)
    halves = lambda o_p, o_s: _Rows(o_p, o_s, 0)

    mods = mods_all[0]
    qkv = _norm_mod_proj(xr, norm_mix[0], mods, na_w_qkv, "na_qkv")
    o_p, new_l0_k, new_l0_v = _na_prompt(qkv, na_q_norm, na_k_norm)
    o_s = _na_latent(qkv, cache_l0_k, cache_l0_v, _na_bias_blocks(na_bias), na_q_norm, na_k_norm)
    x = _out_proj_residual(xr, halves(o_p, o_s), mods, na_w_o, "na_out")
    x = _ffn(x, norm_ffn[0], mods, ffn_w_up[0], ffn_conv_w[0], ffn_conv_b[0], ffn_w_down[0])
    xr = _one_array(x)

    mods = mods_all[1]
    w_cat = jnp.concatenate(
        [gla_w_qkvg, gla_w_gate1[0], gla_w_gate1[1],
         jnp.zeros((D_MODEL, 128 - 2 * GLA_GATE_RANK), F32)], axis=1)
    proj = _norm_mod_proj(xr, norm_mix[1], mods, w_cat, "gla_proj")
    w2 = jnp.zeros((2, 128, GLA_HK), F32)
    w2 = w2.at[0, :GLA_GATE_RANK].set(gla_w_gate2[0])
    w2 = w2.at[1, GLA_GATE_RANK:2 * GLA_GATE_RANK].set(gla_w_gate2[1])
    bg = gla_b_gate.reshape(2, 1, GLA_HK)
    o_p, new_l1_fwd, new_l1_bwd = _gla(proj, w2, bg, gla_o_norm, BATCH, SEQ, 0)
    o_s = _gla(proj, w2, bg, gla_o_norm, DEC_BATCH, DEC_SEQ, N_PROMPT // DEC_SEQ,
               states=(state_l1_fwd, state_l1_bwd))
    x = _out_proj_residual(xr, halves(o_p, o_s), mods, gla_w_o, "gla_out",
                           gate=proj, gate_col_block=(2 * GLA_HK + GLA_HV) // GLA_HV)
    x = _ffn(x, norm_ffn[1], mods, ffn_w_up[1], ffn_conv_w[1], ffn_conv_b[1], ffn_w_down[1])
    xr = _one_array(x)

    mods = mods_all[2]
    qkv = _norm_mod_proj(xr, norm_mix[2], mods, diff_w_qkv, "diff_qkv")
    o_p, kn_p, new_l2_v = _diff_prompt(qkv, diff_q_norm, diff_k_norm, diff_lambda, diff_sub_norm)
    new_l2_k = kn_p.reshape(BATCH, 2 * DIFF_HEADS, SEQ, DIFF_HD)
    cos_d, sin_d = _diff_rope_tables()
    o_s = _diff_latent(qkv, cache_l2_k, cache_l2_v, cos_d, sin_d, diff_q_norm, diff_k_norm,
                       diff_lambda, diff_sub_norm)
    x = _out_proj_residual(xr, halves(o_p, o_s), mods, diff_w_o, "diff_out")
    x = _ffn(x, norm_ffn[2], mods, ffn_w_up[2], ffn_conv_w[2], ffn_conv_b[2], ffn_w_down[2])
    xr = _one_array(x)

    mods = mods_all[3]
    zc = lambda n: jnp.zeros((D_MODEL, n), F32)
    w_low = jnp.concatenate(
        [mla_w_dq, zc(MLA_LOW_KV - MLA_Q_RANK), mla_w_dkv[:, :MLA_KV_RANK],
         zc(MLA_NOPE), mla_w_dkv[:, MLA_KV_RANK:], zc(MLA_HL - MLA_QK)], axis=1)
    low = _norm_mod_proj(xr, norm_mix[3], mods, w_low, "mla_down")
    w_uq = _pad_heads(mla_w_uq, MLA_HEADS, MLA_QK, MLA_HL)
    qp, _ = _rms_matmul(low, 0, MLA_Q_RANK, mla_q_a_norm, w_uq, True, "mla_uq")
    kvp, ckv = _rms_matmul(low, MLA_LOW_KV // MLA_KV_RANK, MLA_KV_RANK, mla_kv_a_norm, mla_w_ukv,
                           True, "mla_ukv")
    kvc, _ = _rms_matmul(cache_l3_ckv.reshape(DEC_BATCH * PAST_LEN, MLA_KV_RANK), 0, MLA_KV_RANK,
                         mla_kv_a_norm, mla_w_ukv, False, "mla_ukv_cache")
    krc = jnp.pad(cache_l3_krope.reshape(DEC_BATCH * PAST_LEN, MLA_ROPE),
                  ((0, 0), (MLA_NOPE, MLA_HL - MLA_QK)))
    pad_gain = lambda g: jnp.tile(jnp.pad(g, (0, MLA_HL - MLA_QK)), MLA_HB).reshape(1, -1)
    gq, gk = pad_gain(mla_q_norm), pad_gain(mla_k_norm)
    o_p = _mla_prompt(qp, kvp, low, gq, gk)
    cos_m, sin_m = _mla_rope_tables()
    o_s = _mla_latent(qp, kvp, low, kvc, krc, cos_m, sin_m, gq, gk)
    new_l3_ckv = ckv[:N_PROMPT].reshape(BATCH, SEQ, MLA_KV_RANK)
    new_l3_krope = low[:N_PROMPT, MLA_LOW_KR + MLA_NOPE:MLA_LOW_KR + MLA_QK].reshape(
        BATCH, SEQ, MLA_ROPE)
    x = _out_proj_residual(xr, halves(o_p, o_s), mods, mla_w_o, "mla_out")
    n_pt = N_PROMPT // TOK_TILE
    ffn3 = functools.partial(_ffn, x, norm_ffn[3], mods, ffn_w_up[3], ffn_conv_w[3],
                             ffn_conv_b[3], ffn_w_down[3])
    y_prompt = ffn3(tile0=0, n_tiles=n_pt).reshape(BATCH, SEQ, D_MODEL)
    y_sample = ffn3(tile0=n_pt, n_tiles=N_TOK_TILES - n_pt).reshape(DEC_BATCH, DEC_SEQ, D_MODEL)
    return (y_prompt, y_sample, new_l0_k, new_l0_v, new_l1_fwd, new_l1_bwd, new_l2_k, new_l2_v,
            new_l3_ckv, new_l3_krope)
```

```python
import functools
import math
from typing import NamedTuple

import jax
import jax.numpy as jnp
from jax import lax
from jax.experimental import pallas as pl
from jax.experimental.pallas import tpu as pltpu

F32 = jnp.float32
BF16 = jnp.bfloat16

D_MODEL = 1024
BATCH = 16
SEQ = 256
DEPTH = 4
DEC_BATCH = 2
DEC_SEQ = 2048
PAST_LEN = 256
GRID_W = 64
D_FF = 2816
EPS = 1e-6
ROPE_BASE = 10000.0

NA_HEADS = 16
NA_HD = 64
NA_WIN_R = 8
NA_WIN_C = 16

GLA_HEADS = 4
GLA_DK = 128
GLA_DV = 256
GLA_HK = GLA_HEADS * GLA_DK
GLA_HV = GLA_HEADS * GLA_DV
GLA_GATE_RANK = 16
GLA_GATE_NORM = 16.0

DIFF_HEADS = 8
DIFF_HD = 64
DIFF_LAMBDA_INIT = 0.8 - 0.6 * math.exp(-0.3 * 2)

MLA_HEADS = 16
MLA_Q_RANK = 384
MLA_KV_RANK = 256
MLA_NOPE = 64
MLA_ROPE = 32
MLA_V = 64
MLA_QK = MLA_NOPE + MLA_ROPE

N_PROMPT = BATCH * SEQ
N_LATENT = DEC_BATCH * DEC_SEQ
N_TOK = N_PROMPT + N_LATENT
TOK_TILE = 2048
N_TOK_TILES = N_TOK // TOK_TILE
FF_CHUNK = 256
N_FF_CHUNKS = D_FF // FF_CHUNK
NEG = -1e30
LOG2E = math.log2(math.e)
SAFE_SCORE_BOUND = 48.0

VMEM_LIMIT = 56 * 1024 * 1024

_NT = (((1,), (1,)), ((), ()))
_TN = (((0,), (0,)), ((), ()))


def _params(sem, vmem=VMEM_LIMIT):
    return pltpu.CompilerParams(dimension_semantics=sem, vmem_limit_bytes=vmem)


def _log2(n):
    assert n & (n - 1) == 0
    return n.bit_length() - 1


def _silu(x):
    return x / (1.0 + jnp.exp(-x))


def _bdot(a, b):
    return jnp.dot(a, b, preferred_element_type=F32)


def _softmax_parts(parts, shift=None):
    m = shift
    if m is None:
        m = parts[0].max(axis=-1, keepdims=True)
        for s in parts[1:]:
            m = jnp.maximum(m, s.max(axis=-1, keepdims=True))
    ps = [jnp.exp2(s - m) for s in parts]
    l = ps[0].sum(axis=-1, keepdims=True)
    for p in ps[1:]:
        l = l + p.sum(axis=-1, keepdims=True)
    return ps, 1.0 / l


def _group_rms(x, gain, group, n_real=None, sums_on_mxu=False):
    lanes = x.shape[-1]
    n_real = n_real or group
    x2 = x * x
    if group == lanes:
        ms = jnp.sum(x2, axis=-1, keepdims=True)
    elif not sums_on_mxu:
        gid = lax.broadcasted_iota(jnp.int32, x.shape, 1) >> _log2(group)
        ms = jnp.zeros_like(x)
        for i in range(lanes // group):
            sel = gid == i
            si = jnp.sum(jnp.where(sel, x2, 0.0), axis=-1, keepdims=True)
            ms = jnp.where(sel, si, ms)
    else:
        r = lax.broadcasted_iota(jnp.int32, (lanes, lanes), 0) >> _log2(group)
        c = lax.broadcasted_iota(jnp.int32, (lanes, lanes), 1) >> _log2(group)
        ones = jnp.where(r == c, 1.0, 0.0).astype(BF16)
        hi = x2.astype(BF16)
        lo = (x2 - hi.astype(F32)).astype(BF16)
        ms = _bdot(hi, ones) + _bdot(lo, ones)
    return x * lax.rsqrt(ms * (1.0 / n_real) + EPS) * gain


def _with_score_bound(bound, body):
    ok = bound.max() < SAFE_SCORE_BOUND
    pl.when(ok)(lambda: body(bound))
    pl.when(jnp.logical_not(ok))(lambda: body(None))


def _rms_norm_bound(g_ref, n, head_lanes):
    g = jnp.abs(g_ref[...])
    heads = g.shape[-1] // head_lanes
    return n ** 0.5 * jnp.concatenate(
        [g[:, h * head_lanes:(h + 1) * head_lanes].max(axis=-1, keepdims=True)
         for h in range(heads)], axis=-1)


def _stacked_rows(per_head, rows):
    r = lax.broadcasted_iota(jnp.int32, (per_head.shape[-1] * rows, 1), 0)
    out = per_head[:, 0:1]
    for h in range(1, per_head.shape[-1]):
        out = jnp.where(r >= h * rows, per_head[:, h:h + 1], out)
    return out


def _max_head_norms(x, head_lanes):
    hid = lax.broadcasted_iota(jnp.int32, x.shape, 1) >> _log2(head_lanes)
    x2 = x * x
    sq = [jnp.sum(jnp.where(hid == h, x2, 0.0), axis=-1, keepdims=True).max(axis=0, keepdims=True)
          for h in range(x.shape[-1] // head_lanes)]
    return jnp.sqrt(jnp.concatenate(sq, axis=-1))


def _rope(x, cos, sin, half):
    lanes = x.shape[-1]
    lane = lax.broadcasted_iota(jnp.int32, x.shape, 1)
    up = pltpu.roll(x, lanes - half, axis=1)
    dn = pltpu.roll(x, half, axis=1)
    swapped = jnp.where((lane & (2 * half - 1)) < half, up, dn)
    return x * cos + swapped * sin


def _stack_heads(q, n_heads, head_lanes):
    hid = lax.broadcasted_iota(jnp.int32, q.shape, 1) >> _log2(head_lanes)
    zero = jnp.zeros_like(q)
    return jnp.concatenate([jnp.where(hid == i, q, zero) for i in range(n_heads)], axis=0)


def _unstack_heads(o, n_heads, head_lanes):
    rows = o.shape[0] // n_heads
    hid = lax.broadcasted_iota(jnp.int32, (rows, o.shape[1]), 1) >> _log2(head_lanes)
    out = o[0:rows]
    for i in range(1, n_heads):
        out = jnp.where(hid == i, o[i * rows:(i + 1) * rows], out)
    return out


ADA_TN = 1536


def _ada_kernel(c_ref, w_ref, b_ref, o_ref):
    s = _silu(c_ref[...])
    o_ref[0] = jnp.dot(s, w_ref[0], preferred_element_type=F32,
                       precision=lax.Precision.HIGHEST) + b_ref[0]


def _ada_mods(cvecs, ada_w, ada_b):
    out = pl.pallas_call(
        _ada_kernel,
        grid=(DEPTH, 6 * D_MODEL // ADA_TN),
        in_specs=[pl.BlockSpec((8, D_MODEL), lambda l, j: (0, 0)),
                  pl.BlockSpec((1, D_MODEL, ADA_TN), lambda l, j: (l, 0, j)),
                  pl.BlockSpec((1, 1, ADA_TN), lambda l, j: (l, 0, j))],
        out_specs=pl.BlockSpec((1, 8, ADA_TN), lambda l, j: (l, 0, j)),
        out_shape=jax.ShapeDtypeStruct((DEPTH, 8, 6 * D_MODEL), F32),
        compiler_params=_params(("arbitrary", "arbitrary")),
        name="ada_mod",
    )(cvecs, ada_w, ada_b.reshape(DEPTH, 1, 6 * D_MODEL))
    return out.reshape(DEPTH, 8, 6, D_MODEL)[:, :3]


def _mod_group_of_tile(i):
    return jnp.maximum(i - (N_PROMPT // TOK_TILE - 1), 0)


def _norm_mod_rows(x_ref, g_ref, mod_ref, h_ref, shift_idx, scale_idx, rows=64):
    g = g_ref[...]
    sc = 1.0 + mod_ref[0, scale_idx:scale_idx + 1, :]
    sh = mod_ref[0, shift_idx:shift_idx + 1, :]

    def body(r, carry):
        sl = pl.ds(pl.multiple_of(r * rows, rows), rows)
        xf = x_ref[sl, :]
        ms = jnp.mean(xf * xf, axis=-1, keepdims=True)
        y = xf * lax.rsqrt(ms + EPS) * g
        h_ref[sl, :] = (y * sc + sh).astype(BF16)
        return carry

    lax.fori_loop(0, x_ref.shape[0] // rows, body, 0)


ROW_TM = 512


class _Rows(NamedTuple):
    prompt: jax.Array
    latent: jax.Array
    latent_row0: int


def _one_array(x):
    return _Rows(x, x, N_PROMPT)


def _row_specs(rows, width, col_block=0):
    n_p = N_PROMPT // ROW_TM
    l0 = rows.latent_row0 // ROW_TM
    return [pl.BlockSpec((ROW_TM, width), lambda t: (jnp.minimum(t, n_p - 1), col_block)),
            pl.BlockSpec((ROW_TM, width), lambda t: (l0 + jnp.maximum(t - n_p, 0), col_block))]


def _row_group(t):
    first_latent = N_PROMPT // ROW_TM
    return jnp.where(t < first_latent, 0, 1 + (t - first_latent) // (DEC_SEQ // ROW_TM))


def _is_prompt_tile():
    return pl.program_id(0) < N_PROMPT // ROW_TM


def _proj_kernel(xp_ref, xl_ref, g_ref, mod_ref, *refs):
    *w_refs, o_ref, h_ref, wb_ref = refs

    @pl.when(pl.program_id(0) == 0)
    def _():
        off = 0
        for w_ref in w_refs:
            wb_ref[:, off:off + w_ref.shape[1]] = w_ref[...].astype(BF16)
            off += w_ref.shape[1]

    is_prompt = _is_prompt_tile()
    g = g_ref[...]
    sc = 1.0 + mod_ref[0, 1:2, :]
    sh = mod_ref[0, 0:1, :]
    rows = 64

    def body(r, carry):
        sl = pl.ds(pl.multiple_of(r * rows, rows), rows)
        xf = jnp.where(is_prompt, xp_ref[sl, :], xl_ref[sl, :])
        ms = jnp.mean(xf * xf, axis=-1, keepdims=True)
        y = xf * lax.rsqrt(ms + EPS) * g
        h_ref[sl, :] = (y * sc + sh).astype(BF16)
        return carry

    lax.fori_loop(0, ROW_TM // rows, body, 0)
    o_ref[...] = _bdot(h_ref[...], wb_ref[...])


def _norm_mod_proj(x, g, mods, ws, name):
    n = sum(w.shape[1] for w in ws)
    return pl.pallas_call(
        _proj_kernel,
        grid=(N_TOK // ROW_TM,),
        in_specs=_row_specs(x, D_MODEL) + [
            pl.BlockSpec((1, D_MODEL), lambda t: (0, 0)),
            pl.BlockSpec((1, 6, D_MODEL), lambda t: (_row_group(t), 0, 0))] + [
            pl.BlockSpec(w.shape, lambda t: (0, 0), pipeline_mode=pl.Buffered(1)) for w in ws],
        out_specs=pl.BlockSpec((ROW_TM, n), lambda t: (t, 0)),
        out_shape=jax.ShapeDtypeStruct((N_TOK, n), F32),
        scratch_shapes=[pltpu.VMEM((ROW_TM, D_MODEL), BF16),
                        pltpu.VMEM((D_MODEL, n), BF16)],
        compiler_params=_params(("arbitrary",)),
        name=name,
    )(x.prompt, x.latent, g.reshape(1, D_MODEL), mods, *ws)


def _rms_matmul_kernel(a_ref, g_ref, w_ref, o_ref, n_ref, *, normalise):
    a = a_ref[...]
    if normalise:
        a = a * lax.rsqrt(jnp.mean(a * a, axis=-1, keepdims=True) + EPS) * g_ref[...]
    n_ref[...] = a
    o_ref[...] = _bdot(a.astype(BF16), w_ref[...].astype(BF16))


def _rms_matmul(a, col_block, k, g, w, normalise, name, tm=512):
    rows, n = a.shape[0], w.shape[1]
    return pl.pallas_call(
        functools.partial(_rms_matmul_kernel, normalise=normalise),
        grid=(rows // tm,),
        in_specs=[pl.BlockSpec((tm, k), lambda i: (i, col_block)),
                  pl.BlockSpec((1, k), lambda i: (0, 0)),
                  pl.BlockSpec((k, n), lambda i: (0, 0))],
        out_specs=[pl.BlockSpec((tm, n), lambda i: (i, 0)),
                   pl.BlockSpec((tm, k), lambda i: (i, 0))],
        out_shape=[jax.ShapeDtypeStruct((rows, n), F32),
                   jax.ShapeDtypeStruct((rows, k), F32)],
        compiler_params=_params(("arbitrary",)),
        name=name,
    )(a, g.reshape(1, k), w)


def _oproj_kernel(*refs, gated):
    if gated:
        xp_ref, xl_ref, ap_ref, al_ref, g_ref, mod_ref, w_ref, o_ref, wb_ref = refs
    else:
        xp_ref, xl_ref, ap_ref, al_ref, mod_ref, w_ref, o_ref, wb_ref = refs

    @pl.when(pl.program_id(0) == 0)
    def _():
        wb_ref[...] = w_ref[...].astype(BF16)

    is_prompt = _is_prompt_tile()
    a = jnp.where(is_prompt, ap_ref[...], al_ref[...])
    if gated:
        a = a * _silu(g_ref[...])
    y = _bdot(a.astype(BF16), wb_ref[...])
    x = jnp.where(is_prompt, xp_ref[...], xl_ref[...])
    o_ref[...] = x + mod_ref[0, 2:3, :] * y


def _out_proj_residual(x, a, mods, w, name, gate=None, gate_col_block=0):
    k = w.shape[0]
    in_specs = _row_specs(x, D_MODEL) + _row_specs(a, k)
    args = [x.prompt, x.latent, a.prompt, a.latent]
    if gate is not None:
        in_specs.append(pl.BlockSpec((ROW_TM, k), lambda t: (t, gate_col_block)))
        args.append(gate)
    in_specs += [pl.BlockSpec((1, 6, D_MODEL), lambda t: (_row_group(t), 0, 0)),
                 pl.BlockSpec((k, D_MODEL), lambda t: (0, 0))]
    args += [mods, w]
    return pl.pallas_call(
        functools.partial(_oproj_kernel, gated=gate is not None),
        grid=(N_TOK // ROW_TM,),
        in_specs=in_specs,
        out_specs=pl.BlockSpec((ROW_TM, D_MODEL), lambda t: (t, 0)),
        out_shape=jax.ShapeDtypeStruct((N_TOK, D_MODEL), F32),
        scratch_shapes=[pltpu.VMEM((k, D_MODEL), BF16)],
        compiler_params=_params(("arbitrary",)),
        name=name,
    )(*args)


FFN_MM_ROWS = 512
FFN_ROWS = 64
FFN_PAD = 8


def _ffn_kernel(x_ref, g_ref, mod_ref, wg_ref, wv_ref, cwg_ref, cwv_ref, cbg_ref, cbv_ref,
                wd_ref, o_ref, h_ref, u_ref, act_ref, wup_ref, wdn_ref, *, tile0):
    i = tile0 + pl.program_id(0)
    c = pl.program_id(1)
    fc = FF_CHUNK

    @pl.when(c == 0)
    def _():
        _norm_mod_rows(x_ref, g_ref, mod_ref, h_ref, 3, 4)
        zeros = jnp.zeros((FFN_PAD, 2 * fc), F32)
        u_ref[0:FFN_PAD, :] = zeros
        u_ref[FFN_PAD + TOK_TILE:, :] = zeros
        o_ref[...] = jnp.zeros_like(o_ref)

    wup_ref[:, :fc] = wg_ref[...].astype(BF16)
    wup_ref[:, fc:] = wv_ref[...].astype(BF16)
    wdn_ref[...] = wd_ref[...].astype(BF16)
    cw_g, cw_v = cwg_ref[...], cwv_ref[...]
    cb_g, cb_v = cbg_ref[...], cbv_ref[...]
    seq_len = jnp.where(i < N_PROMPT // TOK_TILE, SEQ, DEC_SEQ)
    row = lax.broadcasted_iota(jnp.int32, (FFN_ROWS, 1), 0)

    def up(t):
        r0 = t * FFN_MM_ROWS
        u_ref[FFN_PAD + r0:FFN_PAD + r0 + FFN_MM_ROWS, :] = _bdot(
            h_ref[r0:r0 + FFN_MM_ROWS, :], wup_ref[...])

    def conv_act(t):
        for r0 in range(t * FFN_MM_ROWS, (t + 1) * FFN_MM_ROWS, FFN_ROWS):
            halves = []
            for lo, cw, cb in ((0, cw_g, cb_g), (fc, cw_v, cb_v)):
                p0 = FFN_PAD + r0
                prev = u_ref[p0 - 1:p0 - 1 + FFN_ROWS, lo:lo + fc]
                mid = u_ref[p0:p0 + FFN_ROWS, lo:lo + fc]
                nxt = u_ref[p0 + 1:p0 + 1 + FFN_ROWS, lo:lo + fc]
                if r0 % SEQ == 0:
                    prev = jnp.where(((r0 + row) & (seq_len - 1)) == 0, 0.0, prev)
                if (r0 + FFN_ROWS) % SEQ == 0:
                    nxt = jnp.where(((r0 + row) & (seq_len - 1)) == seq_len - 1, 0.0, nxt)
                halves.append(prev * cw[0:1] + mid * cw[1:2] + nxt * cw[2:3] + cb)
            act_ref[r0:r0 + FFN_ROWS, :] = (_silu(halves[0]) * halves[1]).astype(BF16)

    def down(t):
        r0 = t * FFN_MM_ROWS
        o_ref[r0:r0 + FFN_MM_ROWS, :] += _bdot(act_ref[r0:r0 + FFN_MM_ROWS, :], wdn_ref[...])

    n = TOK_TILE // FFN_MM_ROWS
    for s in range(n + 2):
        if s < n:
            up(s)
        if 1 <= s <= n:
            conv_act(s - 1)
        if s >= 2:
            down(s - 2)

    @pl.when(c == N_FF_CHUNKS - 1)
    def _():
        o_ref[...] = x_ref[...] + mod_ref[0, 5:6, :] * o_ref[...]


def _ffn(x, g, mods, layer, w_up, conv_w, conv_b, w_down, tile0=0, n_tiles=N_TOK_TILES):
    fc = FF_CHUNK
    ncb = N_FF_CHUNKS
    return pl.pallas_call(
        functools.partial(_ffn_kernel, tile0=tile0),
        grid=(n_tiles, ncb),
        in_specs=[pl.BlockSpec((TOK_TILE, D_MODEL), lambda i, c: (tile0 + i, 0)),
                  pl.BlockSpec((1, D_MODEL), lambda i, c: (0, 0)),
                  pl.BlockSpec((1, 6, D_MODEL), lambda i, c: (_mod_group_of_tile(tile0 + i), 0, 0)),
                  pl.BlockSpec((None, D_MODEL, fc), lambda i, c: (layer, 0, c)),
                  pl.BlockSpec((None, D_MODEL, fc), lambda i, c: (layer, 0, ncb + c)),
                  pl.BlockSpec((None, 3, fc), lambda i, c: (layer, 0, c)),
                  pl.BlockSpec((None, 3, fc), lambda i, c: (layer, 0, ncb + c)),
                  pl.BlockSpec((None, 1, fc), lambda i, c: (layer, 0, c)),
                  pl.BlockSpec((None, 1, fc), lambda i, c: (layer, 0, ncb + c)),
                  pl.BlockSpec((None, fc, D_MODEL), lambda i, c: (layer, c, 0))],
        out_specs=pl.BlockSpec((TOK_TILE, D_MODEL), lambda i, c: (i, 0)),
        out_shape=jax.ShapeDtypeStruct((n_tiles * TOK_TILE, D_MODEL), F32),
        scratch_shapes=[pltpu.VMEM((TOK_TILE, D_MODEL), BF16),
                        pltpu.VMEM((TOK_TILE + 2 * FFN_PAD, 2 * fc), F32),
                        pltpu.VMEM((TOK_TILE, fc), BF16),
                        pltpu.VMEM((D_MODEL, 2 * fc), BF16),
                        pltpu.VMEM((fc, D_MODEL), BF16)],
        compiler_params=_params(("arbitrary", "arbitrary")),
        name="conv_ffn",
    )(x, g.reshape(1, D_MODEL), mods, w_up, w_up, conv_w, conv_w,
      conv_b.reshape(DEPTH, 1, -1), conv_b.reshape(DEPTH, 1, -1), w_down)


NA_HB = 4
NA_LANES = NA_HB * NA_HD
NA_ROWS = DEC_SEQ // GRID_W
NA_KEYS = NA_WIN_R * GRID_W
PROMPT_SEQS = 4
NA_ROW_UNROLL = 2


def _store_heads(dst_ref, seq, x, n_heads, hd):
    for h in range(n_heads):
        dst_ref[seq, h] = x[:, h * hd:(h + 1) * hd]


def _na_prompt_kernel(q_ref, k_ref, v_ref, gq_ref, gk_ref, o_ref, kn_ref, vn_ref):
    scale = NA_HD ** -0.5 * LOG2E

    def chains(bound):
        shift = None if bound is None else _stacked_rows(bound, SEQ)
        for seq in range(PROMPT_SEQS):
            sl = slice(seq * SEQ, (seq + 1) * SEQ)
            q = _group_rms(q_ref[sl, :], gq_ref[...], NA_HD, sums_on_mxu=True) * scale
            k = _group_rms(k_ref[sl, :], gk_ref[...], NA_HD, sums_on_mxu=True)
            v = v_ref[sl, :]
            _store_heads(kn_ref, seq, k, NA_HB, NA_HD)
            _store_heads(vn_ref, seq, v, NA_HB, NA_HD)
            q4 = _stack_heads(q, NA_HB, NA_HD).astype(BF16)
            s = lax.dot_general(q4, k.astype(BF16), _NT, preferred_element_type=F32)
            (p,), inv = _softmax_parts([s], shift)
            o4 = _bdot(p.astype(BF16), v.astype(BF16)) * inv
            o_ref[sl, :] = _unstack_heads(o4, NA_HB, NA_HD)

    _with_score_bound(scale * _rms_norm_bound(gq_ref, NA_HD, NA_HD)
                      * _rms_norm_bound(gk_ref, NA_HD, NA_HD), chains)


def _na_prompt(qkv, gq, gk):
    nb = NA_HEADS // NA_HB
    rows = PROMPT_SEQS * SEQ
    blk = lambda off: pl.BlockSpec((rows, NA_LANES), lambda b, j: (b, off + j))
    vec = pl.BlockSpec((1, NA_LANES), lambda b, j: (0, 0))
    cache = pl.BlockSpec((PROMPT_SEQS, NA_HB, SEQ, NA_HD), lambda b, j: (b, j, 0, 0))
    cache_shape = jax.ShapeDtypeStruct((BATCH, NA_HEADS, SEQ, NA_HD), F32)
    return pl.pallas_call(
        _na_prompt_kernel,
        grid=(BATCH // PROMPT_SEQS, nb),
        in_specs=[blk(0), blk(nb), blk(2 * nb), vec, vec],
        out_specs=[blk(0), cache, cache],
        out_shape=[jax.ShapeDtypeStruct((N_PROMPT, D_MODEL), F32), cache_shape, cache_shape],
        compiler_params=_params(("arbitrary", "arbitrary")),
        name="na_prompt",
    )(qkv, qkv, qkv, jnp.tile(gq, NA_HB).reshape(1, -1), jnp.tile(gk, NA_HB).reshape(1, -1))


def _na_latent_kernel(q_ref, k_ref, v_ref, kc_ref, vc_ref, t_ref, gq_ref, gk_ref, o_ref,
                      qn_ref, kn_ref, vb_ref, kc4_ref, vc4_ref, bias_ref):
    scale = NA_HD ** -0.5 * LOG2E
    rows = 256

    def prep(r, carry):
        sl = pl.ds(pl.multiple_of(r * rows, rows), rows)
        qn_ref[sl, :] = (_group_rms(q_ref[sl, :], gq_ref[...], NA_HD, sums_on_mxu=True)
                         * scale).astype(BF16)
        kn_ref[sl, :] = _group_rms(k_ref[sl, :], gk_ref[...], NA_HD,
                                   sums_on_mxu=True).astype(BF16)
        vb_ref[sl, :] = v_ref[sl, :].astype(BF16)
        return carry

    lax.fori_loop(0, DEC_SEQ // rows, prep, 0)
    kc = kc_ref[0]
    kc4_ref[...] = kc.astype(BF16)
    vc4_ref[...] = vc_ref[0].astype(BF16)

    def attend(bound):
        for h in range(NA_HB):
            off = 0.0 if bound is None else bound[:, h:h + 1]
            for p in range(NA_WIN_R):
                for i in range(NA_WIN_R):
                    bias_ref[h, p, :, i * GRID_W:(i + 1) * GRID_W] = t_ref[h, p + i] - off
        shift = None if bound is None else _stacked_rows(bound, GRID_W)

        def row(r):
            kr0 = jnp.clip(r - NA_WIN_R // 2, 0, NA_ROWS - NA_WIN_R)
            pat = kr0 - r + NA_WIN_R - 1
            qs = pl.ds(pl.multiple_of(r * GRID_W, GRID_W), GRID_W)
            ks = pl.ds(pl.multiple_of(kr0 * GRID_W, GRID_W), NA_KEYS)
            q4 = _stack_heads(qn_ref[qs, :], NA_HB, NA_HD)
            s_loc = lax.dot_general(q4, kn_ref[ks, :], _NT, preferred_element_type=F32)
            s_loc = s_loc + jnp.concatenate([bias_ref[h, pat] for h in range(NA_HB)], axis=0)
            s_ctx = lax.dot_general(q4, kc4_ref[...], _NT, preferred_element_type=F32)
            if shift is None:
                (p_loc, p_ctx), inv = _softmax_parts([s_loc, s_ctx])
            else:
                p_loc, p_ctx = jnp.exp2(s_loc), jnp.exp2(s_ctx - shift)
                inv = 1.0 / (p_loc.sum(axis=-1, keepdims=True) + p_ctx.sum(axis=-1, keepdims=True))
            o4 = _bdot(p_loc.astype(BF16), vb_ref[ks, :]) + _bdot(p_ctx.astype(BF16), vc4_ref[...])
            o_ref[qs, :] = _unstack_heads(o4 * inv, NA_HB, NA_HD)

        def rows_step(i, carry):
            for u in range(NA_ROW_UNROLL):
                row(i * NA_ROW_UNROLL + u)
            return carry

        lax.fori_loop(0, NA_ROWS // NA_ROW_UNROLL, rows_step, 0)

    qmax = scale * _rms_norm_bound(gq_ref, NA_HD, NA_HD)
    bias_max = jnp.concatenate(
        [t_ref[h].max(axis=0).max(axis=0, keepdims=True).max(axis=1, keepdims=True)
         for h in range(NA_HB)], axis=-1)
    _with_score_bound(
        jnp.maximum(qmax * _rms_norm_bound(gk_ref, NA_HD, NA_HD) + bias_max,
                    qmax * _max_head_norms(kc, NA_HD)), attend)


def _na_bias_blocks(bias_table):
    qc = jnp.arange(GRID_W)[:, None]
    kc = jnp.arange(GRID_W)[None, :]
    win0 = jnp.clip(qc - NA_WIN_C // 2, 0, GRID_W - NA_WIN_C)
    valid = (kc >= win0) & (kc < win0 + NA_WIN_C)
    n_ro, n_co = bias_table.shape[1:]
    c = NA_WIN_C - 1
    period = jnp.concatenate(
        [bias_table[..., c:], jnp.zeros((NA_HEADS, n_ro, 2 * GRID_W - n_co), F32),
         bias_table[..., :c]], axis=-1)
    flat = jnp.tile(period, (1, 1, GRID_W))[..., :GRID_W * (2 * GRID_W - 1)]
    t = flat.reshape(NA_HEADS, n_ro, GRID_W, 2 * GRID_W - 1)[..., :GRID_W]
    return jnp.where(valid, t * LOG2E, NEG)


def _na_latent(qkv, cache_k, cache_v, bias_blocks, gq, gk):
    nb = NA_HEADS // NA_HB
    lat0 = N_PROMPT // DEC_SEQ
    blk = lambda off: pl.BlockSpec((DEC_SEQ, NA_LANES), lambda b, j: (lat0 + b, off + j))
    vec = pl.BlockSpec((1, NA_LANES), lambda b, j: (0, 0))
    cache = pl.BlockSpec((1, PAST_LEN, NA_LANES), lambda b, j: (b, 0, j))
    return pl.pallas_call(
        _na_latent_kernel,
        grid=(DEC_BATCH, nb),
        in_specs=[blk(0), blk(nb), blk(2 * nb), cache, cache,
                  pl.BlockSpec((NA_HB, 2 * NA_WIN_R - 1, GRID_W, GRID_W), lambda b, j: (j, 0, 0, 0)),
                  vec, vec],
        out_specs=pl.BlockSpec((DEC_SEQ, NA_LANES), lambda b, j: (b, j)),
        out_shape=jax.ShapeDtypeStruct((N_LATENT, D_MODEL), F32),
        scratch_shapes=[pltpu.VMEM((DEC_SEQ, NA_LANES), BF16),
                        pltpu.VMEM((DEC_SEQ, NA_LANES), BF16),
                        pltpu.VMEM((DEC_SEQ, NA_LANES), BF16),
                        pltpu.VMEM((PAST_LEN, NA_LANES), BF16),
                        pltpu.VMEM((PAST_LEN, NA_LANES), BF16),
                        pltpu.VMEM((NA_HB, NA_WIN_R, GRID_W, NA_KEYS), F32)],
        compiler_params=_params(("arbitrary", "arbitrary")),
        name="na_latent",
    )(qkv, qkv, qkv, _tokens_first(cache_k), _tokens_first(cache_v), bias_blocks,
      jnp.tile(gq, NA_HB).reshape(1, -1), jnp.tile(gk, NA_HB).reshape(1, -1))


GLA_C = 128
GLA_SUB = 16
GLA_LEVELS = (64, 32, 16)


def _split_hi_lo(x):
    hi = x.astype(BF16)
    lo = (x - hi.astype(F32)).astype(BF16)
    return jnp.concatenate([hi, lo], axis=1)


def _gla_chunk(q, k, v, g, st_ref, rev):
    c = GLA_C
    row = lax.broadcasted_iota(jnp.int32, (c, c), 0)
    col = lax.broadcasted_iota(jnp.int32, (c, c), 1)
    tri = (col >= row) if rev else (col <= row)
    cs = _bdot(jnp.where(tri, 1.0, 0.0).astype(BF16), _split_hi_lo(g))
    b = cs[:, :GLA_DK] + cs[:, GLA_DK:]
    causal = (col >= row) if rev else (col <= row)

    a = jnp.zeros((c, c), F32)
    rid = lax.broadcasted_iota(jnp.int32, (c, GLA_DK), 0)
    for m in GLA_LEVELS:
        nblk = c // (2 * m)
        if rev:
            bnd = [b[j * 2 * m + m:j * 2 * m + m + 1] for j in range(nblk)]
        else:
            bnd = [b[j * 2 * m + m - 1:j * 2 * m + m] for j in range(nblk)]
        ref = jnp.concatenate([jnp.broadcast_to(x, (2 * m, GLA_DK)) for x in bnd], axis=0)
        later = ((rid & m) == 0) if rev else ((rid & m) != 0)
        dq = jnp.minimum(b - ref, 0.0)
        dk_ = jnp.minimum(ref - b, 0.0)
        qh = jnp.where(later, q * jnp.exp(dq), 0.0).astype(BF16)
        kh = jnp.where(later, 0.0, k * jnp.exp(dk_)).astype(BF16)
        blk = lax.dot_general(qh, kh, _NT, preferred_element_type=F32)
        same = (row >> _log2(2 * m)) == (col >> _log2(2 * m))
        a = a + jnp.where(same, blk, 0.0)

    nsub = c // GLA_SUB
    lane_c = lax.broadcasted_iota(jnp.int32, (GLA_SUB, c), 1)
    srow = lax.broadcasted_iota(jnp.int32, (GLA_SUB, 1), 0)
    diag_rows = []
    for blk_i in range(nsub):
        r0 = blk_i * GLA_SUB
        qb = q[r0:r0 + GLA_SUB]
        bb = b[r0:r0 + GLA_SUB]
        acc = jnp.zeros((GLA_SUB, c), F32)
        for s in range(GLA_SUB):
            ks = k[r0 + s:r0 + s + 1]
            bs = b[r0 + s:r0 + s + 1]
            w = jnp.sum(qb * ks * jnp.exp(jnp.minimum(bb - bs, 0.0)), axis=-1, keepdims=True)
            keep = (srow <= s) if rev else (srow >= s)
            w = jnp.where(keep, w, 0.0)
            acc = jnp.where(lane_c == r0 + s, w, acc)
        diag_rows.append(acc)
    a = a + jnp.concatenate(diag_rows, axis=0)
    a = jnp.where(causal, a, 0.0)

    st = st_ref[...]
    inter = lax.dot_general((q * jnp.exp(b)).astype(BF16), st.astype(BF16), _NT,
                            preferred_element_type=F32)
    o = inter + _bdot(a.astype(BF16), v.astype(BF16))

    btot = b[0:1] if rev else b[c - 1:c]
    kd = (k * jnp.exp(btot - b)).astype(BF16)
    st_ref[...] = st * jnp.exp(btot) + lax.dot_general(v.astype(BF16), kd, _TN,
                                                       preferred_element_type=F32)
    return o


def _gla_kernel(*refs, n_tok, has_state):
    if has_state:
        (q_ref, k_ref, v_ref, r_ref, w2_ref, bg_ref, gn_ref, s0f_ref, s0b_ref,
         o_ref, lg_ref, of_ref, ob_ref, stf_ref, stb_ref) = refs
    else:
        (q_ref, k_ref, v_ref, r_ref, w2_ref, bg_ref, gn_ref,
         o_ref, sf_ref, sb_ref, lg_ref, of_ref, ob_ref, stf_ref, stb_ref) = refs
    nc = n_tok // GLA_C
    scale = GLA_DK ** -0.5

    rb = r_ref[...].astype(BF16)
    for z in range(2):
        x = _bdot(rb, w2_ref[z].astype(BF16)) + bg_ref[z]
        lg_ref[z] = (jnp.minimum(x, 0.0) - jnp.log1p(jnp.exp(-jnp.abs(x)))) * (1.0 / GLA_GATE_NORM)

    if has_state:
        stf_ref[...] = s0f_ref[0, 0].T
        stb_ref[...] = s0b_ref[0, 0].T
    else:
        stf_ref[...] = jnp.zeros_like(stf_ref)
        stb_ref[...] = jnp.zeros_like(stb_ref)

    def step(ci, carry):
        for rev in (False, True):
            cc = (nc - 1 - ci) if rev else ci
            sl = pl.ds(pl.multiple_of(cc * GLA_C, GLA_C), GLA_C)
            q = q_ref[sl, :] * scale
            o = _gla_chunk(q, k_ref[sl, :], v_ref[sl, :], lg_ref[1 if rev else 0, sl, :],
                           stb_ref if rev else stf_ref, rev)
            (ob_ref if rev else of_ref)[sl, :] = o
        return carry

    lax.fori_loop(0, nc, step, 0)

    o = of_ref[...] + ob_ref[...]
    o_ref[...] = o * lax.rsqrt(jnp.mean(o * o, axis=-1, keepdims=True) + EPS) * gn_ref[...]
    if not has_state:
        sf_ref[0, 0] = stf_ref[...].T
        sb_ref[0, 0] = stb_ref[...].T


def _gla(proj, w2, bg, gnorm, n_seq, n_tok, row_block0, states=None):
    qb = GLA_HK // GLA_DK
    spec = lambda width, off: pl.BlockSpec((n_tok, width), lambda b, h: (row_block0 + b, off + h))
    in_specs = [spec(GLA_DK, 0), spec(GLA_DK, qb), spec(GLA_DV, 2 * GLA_HK // GLA_DV),
                pl.BlockSpec((n_tok, 128), lambda b, h: (row_block0 + b, (2 * GLA_HK + 2 * GLA_HV) // 128)),
                pl.BlockSpec((2, 128, GLA_DK), lambda b, h: (0, 0, h)),
                pl.BlockSpec((2, 1, GLA_DK), lambda b, h: (0, 0, h)),
                pl.BlockSpec((1, GLA_DV), lambda b, h: (0, 0))]
    args = [proj, proj, proj, proj, w2, bg, gnorm.reshape(1, GLA_DV)]
    st_spec = pl.BlockSpec((1, 1, GLA_DK, GLA_DV), lambda b, h: (b, h, 0, 0))
    o_spec = pl.BlockSpec((n_tok, GLA_DV), lambda b, h: (b, h))
    o_shape = jax.ShapeDtypeStruct((n_seq * n_tok, GLA_HV), F32)
    if states is not None:
        in_specs += [st_spec, st_spec]
        args += list(states)
        out_specs, out_shape = o_spec, o_shape
    else:
        st_shape = jax.ShapeDtypeStruct((n_seq, GLA_HEADS, GLA_DK, GLA_DV), F32)
        out_specs, out_shape = [o_spec, st_spec, st_spec], [o_shape, st_shape, st_shape]
    return pl.pallas_call(
        functools.partial(_gla_kernel, n_tok=n_tok, has_state=states is not None),
        grid=(n_seq, GLA_HEADS),
        in_specs=in_specs,
        out_specs=out_specs,
        out_shape=out_shape,
        scratch_shapes=[pltpu.VMEM((2, n_tok, GLA_DK), F32),
                        pltpu.VMEM((n_tok, GLA_DV), F32),
                        pltpu.VMEM((n_tok, GLA_DV), F32),
                        pltpu.VMEM((GLA_DV, GLA_DK), F32),
                        pltpu.VMEM((GLA_DV, GLA_DK), F32)],
        compiler_params=_params(("arbitrary", "arbitrary")),
        name="gla_latent" if states is not None else "gla_prompt",
    )(*args)


DIFF_HB = 2
DIFF_QL = DIFF_HB * DIFF_HD
DIFF_VL = DIFF_HB * 2 * DIFF_HD
DIFF_TQ = 512
DIFF_SUB = 128


def _diff_lambda(lam_ref):
    l = lam_ref[...]
    a = jnp.sum(l[0:1] * l[1:2], axis=-1, keepdims=True)
    b = jnp.sum(l[2:3] * l[3:4], axis=-1, keepdims=True)
    return jnp.exp(a) - jnp.exp(b) + DIFF_LAMBDA_INIT


def _diff_finish(ps, invs, lam, v, sn_ref):
    a = ps[0] - (lam * invs[1] / invs[0]) * ps[1]
    o2 = _bdot(a.astype(BF16), v) * invs[0]
    o = _unstack_heads(o2, DIFF_HB, 2 * DIFF_HD)
    return _group_rms(o, sn_ref[...], 2 * DIFF_HD) * (1.0 - DIFF_LAMBDA_INIT)


def _diff_prompt_kernel(q0_ref, q1_ref, k0_ref, k1_ref, v_ref, gq_ref, gk_ref, lam_ref, sn_ref,
                        o_ref, kn_ref, vn_ref):
    scale = DIFF_HD ** -0.5 * LOG2E
    lam = _diff_lambda(lam_ref)

    def chains(bound):
        shift = None if bound is None else _stacked_rows(bound, SEQ)
        for seq in range(PROMPT_SEQS):
            sl = slice(seq * SEQ, (seq + 1) * SEQ)
            ps, invs = [], []
            for comp, (q_ref, k_ref) in enumerate(((q0_ref, k0_ref), (q1_ref, k1_ref))):
                q = _group_rms(q_ref[sl, :], gq_ref[...], DIFF_HD, sums_on_mxu=True) * scale
                k = _group_rms(k_ref[sl, :], gk_ref[...], DIFF_HD, sums_on_mxu=True)
                for h in range(DIFF_HB):
                    kn_ref[seq, comp, h] = k[:, h * DIFF_HD:(h + 1) * DIFF_HD]
                q2 = _stack_heads(q, DIFF_HB, DIFF_HD).astype(BF16)
                s = lax.dot_general(q2, k.astype(BF16), _NT, preferred_element_type=F32)
                (p,), inv = _softmax_parts([s], shift)
                ps.append(p)
                invs.append(inv)
            v = v_ref[sl, :]
            _store_heads(vn_ref, seq, v, DIFF_HB, 2 * DIFF_HD)
            o_ref[sl, :] = _diff_finish(ps, invs, lam, v.astype(BF16), sn_ref)

    _with_score_bound(scale * _rms_norm_bound(gq_ref, DIFF_HD, DIFF_HD)
                      * _rms_norm_bound(gk_ref, DIFF_HD, DIFF_HD), chains)


def _diff_prompt(qkv, gq, gk, lam, sub_norm):
    nb = DIFF_HEADS // DIFF_HB
    rows = PROMPT_SEQS * SEQ
    qk = lambda off: pl.BlockSpec((rows, DIFF_QL), lambda b, j: (b, off + j))
    vec = lambda n: pl.BlockSpec((1, n), lambda b, j: (0, 0))
    v_spec = pl.BlockSpec((rows, DIFF_VL), lambda b, j: (b, 2 * D_MODEL // DIFF_VL + j))
    kn_spec = pl.BlockSpec((PROMPT_SEQS, 2, DIFF_HB, SEQ, DIFF_HD), lambda b, j: (b, 0, j, 0, 0))
    vn_spec = pl.BlockSpec((PROMPT_SEQS, DIFF_HB, SEQ, 2 * DIFF_HD), lambda b, j: (b, j, 0, 0))
    return pl.pallas_call(
        _diff_prompt_kernel,
        grid=(BATCH // PROMPT_SEQS, nb),
        in_specs=[qk(0), qk(nb), qk(2 * nb), qk(3 * nb), v_spec, vec(DIFF_QL), vec(DIFF_QL),
                  pl.BlockSpec((4, DIFF_HD), lambda b, j: (0, 0)), vec(DIFF_VL)],
        out_specs=[pl.BlockSpec((rows, DIFF_VL), lambda b, j: (b, j)), kn_spec, vn_spec],
        out_shape=[jax.ShapeDtypeStruct((N_PROMPT, D_MODEL), F32),
                   jax.ShapeDtypeStruct((BATCH, 2, DIFF_HEADS, SEQ, DIFF_HD), F32),
                   jax.ShapeDtypeStruct((BATCH, DIFF_HEADS, SEQ, 2 * DIFF_HD), F32)],
        compiler_params=_params(("arbitrary", "arbitrary")),
        name="diff_prompt",
    )(qkv, qkv, qkv, qkv, qkv, jnp.tile(gq, DIFF_HB).reshape(1, -1),
      jnp.tile(gk, DIFF_HB).reshape(1, -1), lam, jnp.tile(sub_norm, DIFF_HB).reshape(1, -1))


def _diff_latent_kernel(q0_ref, q1_ref, k0_ref, k1_ref, v_ref, kc0_ref, kc1_ref, vc_ref,
                        cos_ref, sin_ref, cosq_ref, sinq_ref, gq_ref, gk_ref, lam_ref, sn_ref,
                        o_ref, kb0_ref, kb1_ref, vb_ref, kmax_ref):
    scale = DIFF_HD ** -0.5 * LOG2E
    half = DIFF_HD // 4
    rows = 256

    @pl.when(pl.program_id(2) == 0)
    def _():
        for comp, (k_ref, kc_ref, kb_ref) in enumerate(
                ((k0_ref, kc0_ref, kb0_ref), (k1_ref, kc1_ref, kb1_ref))):
            kc = kc_ref[0]
            kb_ref[0:PAST_LEN, :] = kc.astype(BF16)
            kmax_ref[comp:comp + 1, :] = jnp.maximum(
                _max_head_norms(kc, DIFF_HD), _rms_norm_bound(gk_ref, DIFF_HD, DIFF_HD))

            def prep(r, carry):
                sl = pl.ds(pl.multiple_of(r * rows, rows), rows)
                k = _group_rms(k_ref[sl, :], gk_ref[...], DIFF_HD)
                k = _rope(k, cos_ref[sl, :], sin_ref[sl, :], half)
                kb_ref[pl.ds(pl.multiple_of(PAST_LEN + r * rows, rows), rows), :] = k.astype(BF16)
                return carry

            lax.fori_loop(0, DEC_SEQ // rows, prep, 0)
        vb_ref[0:PAST_LEN, :] = vc_ref[0].astype(BF16)
        vb_ref[PAST_LEN:, :] = v_ref[...].astype(BF16)

    lam = _diff_lambda(lam_ref)

    def chains(bound):
        for r0 in range(0, DIFF_TQ, DIFF_SUB):
            sl = slice(r0, r0 + DIFF_SUB)
            ps, invs = [], []
            for comp, (q_ref, kb_ref) in enumerate(((q0_ref, kb0_ref), (q1_ref, kb1_ref))):
                shift = None if bound is None else _stacked_rows(
                    bound[:, comp * DIFF_HB:(comp + 1) * DIFF_HB], DIFF_SUB)
                q = _group_rms(q_ref[sl, :], gq_ref[...], DIFF_HD)
                q = _rope(q, cosq_ref[sl, :], sinq_ref[sl, :], half) * scale
                q2 = _stack_heads(q, DIFF_HB, DIFF_HD).astype(BF16)
                s = lax.dot_general(q2, kb_ref[...], _NT, preferred_element_type=F32)
                (p,), inv = _softmax_parts([s], shift)
                ps.append(p)
                invs.append(inv)
            o_ref[sl, :] = _diff_finish(ps, invs, lam, vb_ref[...], sn_ref)

    qmax = scale * _rms_norm_bound(gq_ref, DIFF_HD, DIFF_HD)
    _with_score_bound(jnp.concatenate([qmax * kmax_ref[0:1, :], qmax * kmax_ref[1:2, :]], axis=-1),
                      chains)


def _diff_latent(qkv, cache_k, cache_v, cos, sin, gq, gk, lam, sub_norm):
    nb = DIFF_HEADS // DIFF_HB
    nq = DEC_SEQ // DIFF_TQ
    q0 = N_PROMPT // DIFF_TQ
    lat0 = N_PROMPT // DEC_SEQ
    n_keys = PAST_LEN + DEC_SEQ
    q_spec = lambda off: pl.BlockSpec((DIFF_TQ, DIFF_QL), lambda b, j, t: (q0 + b * nq + t, off + j))
    k_spec = lambda off: pl.BlockSpec((DEC_SEQ, DIFF_QL), lambda b, j, t: (lat0 + b, off + j))
    v_spec = pl.BlockSpec((DEC_SEQ, DIFF_VL), lambda b, j, t: (lat0 + b, 2 * D_MODEL // DIFF_VL + j))
    kc_spec = lambda off: pl.BlockSpec((1, PAST_LEN, DIFF_QL), lambda b, j, t: (b, 0, off + j))
    vc_spec = pl.BlockSpec((1, PAST_LEN, DIFF_VL), lambda b, j, t: (b, 0, j))
    tab = pl.BlockSpec((DEC_SEQ, DIFF_QL), lambda b, j, t: (0, 0))
    tabq = pl.BlockSpec((DIFF_TQ, DIFF_QL), lambda b, j, t: (t, 0))
    vec = lambda n: pl.BlockSpec((1, n), lambda b, j, t: (0, 0))
    return pl.pallas_call(
        _diff_latent_kernel,
        grid=(DEC_BATCH, nb, nq),
        in_specs=[q_spec(0), q_spec(nb), k_spec(2 * nb), k_spec(3 * nb), v_spec,
                  kc_spec(0), kc_spec(nb), vc_spec, tab, tab, tabq, tabq,
                  vec(DIFF_QL), vec(DIFF_QL),
                  pl.BlockSpec((4, DIFF_HD), lambda b, j, t: (0, 0)), vec(DIFF_VL)],
        out_specs=pl.BlockSpec((DIFF_TQ, DIFF_VL), lambda b, j, t: (b * nq + t, j)),
        out_shape=jax.ShapeDtypeStruct((N_LATENT, D_MODEL), F32),
        scratch_shapes=[pltpu.VMEM((n_keys, DIFF_QL), BF16),
                        pltpu.VMEM((n_keys, DIFF_QL), BF16),
                        pltpu.VMEM((n_keys, DIFF_VL), BF16),
                        pltpu.VMEM((2, DIFF_HB), F32)],
        compiler_params=_params(("arbitrary", "arbitrary", "arbitrary")),
        name="diff_latent",
    )(qkv, qkv, qkv, qkv, qkv, _tokens_first(cache_k), _tokens_first(cache_k),
      _tokens_first(cache_v), cos, sin, cos, sin,
      jnp.tile(gq, DIFF_HB).reshape(1, -1), jnp.tile(gk, DIFF_HB).reshape(1, -1), lam,
      jnp.tile(sub_norm, DIFF_HB).reshape(1, -1))


MLA_HB = 2
MLA_HL = 128
MLA_LANES = MLA_HB * MLA_HL
MLA_TQ = 512
MLA_SUB = 128


def _mla_keys(kv, kr, gk, sums_on_mxu=False):
    lane = lax.broadcasted_iota(jnp.int32, kv.shape, 1)
    kr2 = jnp.concatenate([kr] * MLA_HB, axis=1)
    k = jnp.where((lane & (MLA_HL - 1)) < MLA_NOPE, kv, kr2)
    return _group_rms(k, gk, MLA_HL, n_real=MLA_QK, sums_on_mxu=sums_on_mxu)


def _mla_out(o2):
    tq = o2.shape[0] // MLA_HB
    oa = pltpu.roll(o2[0:tq, 0:MLA_HL], MLA_HL - MLA_V, axis=1)
    ob = o2[tq:, MLA_HL:]
    lane = lax.broadcasted_iota(jnp.int32, oa.shape, 1)
    return jnp.where(lane < MLA_V, oa, ob)


def _mla_prompt_kernel(q_ref, kv_ref, kr_ref, gq_ref, gk_ref, o_ref):
    scale = MLA_QK ** -0.5 * LOG2E

    def chains(bound):
        shift = None if bound is None else _stacked_rows(bound, SEQ)
        for seq in range(PROMPT_SEQS):
            sl = slice(seq * SEQ, (seq + 1) * SEQ)
            q = _group_rms(q_ref[sl, :], gq_ref[...], MLA_HL, n_real=MLA_QK) * scale
            kv = kv_ref[sl, :]
            k = _mla_keys(kv, kr_ref[sl, :], gk_ref[...])
            q2 = _stack_heads(q, MLA_HB, MLA_HL).astype(BF16)
            s = lax.dot_general(q2, k.astype(BF16), _NT, preferred_element_type=F32)
            (p,), inv = _softmax_parts([s], shift)
            o_ref[sl, :] = _mla_out(_bdot(p.astype(BF16), kv.astype(BF16)) * inv)

    _with_score_bound(scale * _rms_norm_bound(gq_ref, MLA_QK, MLA_HL)
                      * _rms_norm_bound(gk_ref, MLA_QK, MLA_HL), chains)


def _mla_prompt(qp, kvp, low, gq, gk):
    nb = MLA_HEADS // MLA_HB
    rows = PROMPT_SEQS * SEQ
    blk = pl.BlockSpec((rows, MLA_LANES), lambda b, j: (b, j))
    vec = pl.BlockSpec((1, MLA_LANES), lambda b, j: (0, 0))
    return pl.pallas_call(
        _mla_prompt_kernel,
        grid=(BATCH // PROMPT_SEQS, nb),
        in_specs=[blk, blk, pl.BlockSpec((rows, MLA_HL), lambda b, j: (b, MLA_LOW_KR // MLA_HL)),
                  vec, vec],
        out_specs=pl.BlockSpec((rows, MLA_HB * MLA_V), lambda b, j: (b, j)),
        out_shape=jax.ShapeDtypeStruct((N_PROMPT, MLA_HEADS * MLA_V), F32),
        compiler_params=_params(("arbitrary", "arbitrary")),
        name="mla_prompt",
    )(qp, kvp, low, gq, gk)


def _mla_latent_kernel(q_ref, kv_ref, kr_ref, kvc_ref, krc_ref, cos_ref, sin_ref, cosq_ref,
                       sinq_ref, gq_ref, gk_ref, o_ref, kb_ref, vb_ref, kmax_ref):
    scale = MLA_QK ** -0.5 * LOG2E
    half = MLA_ROPE // 4
    rows = 256

    @pl.when(pl.program_id(2) == 0)
    def _():
        kvc = kvc_ref[...]
        kc = _mla_keys(kvc, krc_ref[...], gk_ref[...])
        kb_ref[0:PAST_LEN, :] = kc.astype(BF16)
        vb_ref[0:PAST_LEN, :] = kvc.astype(BF16)

        def prep(r, carry):
            sl = pl.ds(pl.multiple_of(r * rows, rows), rows)
            dst = pl.ds(pl.multiple_of(PAST_LEN + r * rows, rows), rows)
            kv = kv_ref[sl, :]
            k = _mla_keys(kv, kr_ref[sl, :], gk_ref[...], sums_on_mxu=True)
            k = _rope(k, cos_ref[sl, :], sin_ref[sl, :], half)
            kb_ref[dst, :] = k.astype(BF16)
            vb_ref[dst, :] = kv.astype(BF16)
            return carry

        lax.fori_loop(0, DEC_SEQ // rows, prep, 0)
        kmax_ref[...] = jnp.maximum(_max_head_norms(kc, MLA_HL),
                                    _rms_norm_bound(gk_ref, MLA_QK, MLA_HL))

    def chains(bound):
        shift = None if bound is None else _stacked_rows(bound, MLA_SUB)
        for r0 in range(0, MLA_TQ, MLA_SUB):
            sl = slice(r0, r0 + MLA_SUB)
            q = _group_rms(q_ref[sl, :], gq_ref[...], MLA_HL, n_real=MLA_QK)
            q = _rope(q, cosq_ref[sl, :], sinq_ref[sl, :], half) * scale
            q2 = _stack_heads(q, MLA_HB, MLA_HL).astype(BF16)
            s = lax.dot_general(q2, kb_ref[...], _NT, preferred_element_type=F32)
            (p,), inv = _softmax_parts([s], shift)
            o_ref[sl, :] = _mla_out(_bdot(p.astype(BF16), vb_ref[...]) * inv)

    _with_score_bound(scale * _rms_norm_bound(gq_ref, MLA_QK, MLA_HL) * kmax_ref[...], chains)


def _mla_latent(qp, kvp, low, kvc, krc, cos, sin, gq, gk):
    nb = MLA_HEADS // MLA_HB
    nq = DEC_SEQ // MLA_TQ
    q0 = N_PROMPT // MLA_TQ
    lat0 = N_PROMPT // DEC_SEQ
    n_keys = PAST_LEN + DEC_SEQ
    tab = pl.BlockSpec((DEC_SEQ, MLA_LANES), lambda b, j, t: (0, 0))
    tabq = pl.BlockSpec((MLA_TQ, MLA_LANES), lambda b, j, t: (t, 0))
    vec = pl.BlockSpec((1, MLA_LANES), lambda b, j, t: (0, 0))
    return pl.pallas_call(
        _mla_latent_kernel,
        grid=(DEC_BATCH, nb, nq),
        in_specs=[pl.BlockSpec((MLA_TQ, MLA_LANES), lambda b, j, t: (q0 + b * nq + t, j)),
                  pl.BlockSpec((DEC_SEQ, MLA_LANES), lambda b, j, t: (lat0 + b, j)),
                  pl.BlockSpec((DEC_SEQ, MLA_HL), lambda b, j, t: (lat0 + b, MLA_LOW_KR // MLA_HL)),
                  pl.BlockSpec((PAST_LEN, MLA_LANES), lambda b, j, t: (b, j)),
                  pl.BlockSpec((PAST_LEN, MLA_HL), lambda b, j, t: (b, 0)),
                  tab, tab, tabq, tabq, vec, vec],
        out_specs=pl.BlockSpec((MLA_TQ, MLA_HB * MLA_V), lambda b, j, t: (b * nq + t, j)),
        out_shape=jax.ShapeDtypeStruct((N_LATENT, MLA_HEADS * MLA_V), F32),
        scratch_shapes=[pltpu.VMEM((n_keys, MLA_LANES), BF16),
                        pltpu.VMEM((n_keys, MLA_LANES), BF16),
                        pltpu.VMEM((1, MLA_HB), F32)],
        compiler_params=_params(("arbitrary", "arbitrary", "arbitrary")),
        name="mla_latent",
    )(qp, kvp, low, kvc, krc, cos, sin, cos, sin, gq, gk)


MLA_LOW_Q = 0
MLA_LOW_KV = 512
MLA_LOW_KR = 768
MLA_LOW_N = 896


def _axial_tables(n_tok, rdim):
    nf = rdim // 4
    freqs = ROPE_BASE ** (-jnp.arange(nf, dtype=F32) / nf)
    t = jnp.arange(n_tok)
    rowp = (t // GRID_W).astype(F32)
    colp = (t % GRID_W).astype(F32)
    ang = jnp.stack([rowp[:, None] * freqs, colp[:, None] * freqs], axis=1)
    cos, sin = jnp.cos(ang), jnp.sin(ang)
    cos_l = jnp.stack([cos, cos], axis=2).reshape(n_tok, rdim)
    sin_l = jnp.stack([-sin, sin], axis=2).reshape(n_tok, rdim)
    return cos_l, sin_l


def _diff_rope_tables():
    cos, sin = _axial_tables(DEC_SEQ, DIFF_HD)
    return jnp.tile(cos, (1, DIFF_HB)), jnp.tile(sin, (1, DIFF_HB))


def _mla_rope_tables():
    cos, sin = _axial_tables(DEC_SEQ, MLA_ROPE)
    ones = jnp.ones((DEC_SEQ, MLA_NOPE), F32)
    pad1 = jnp.ones((DEC_SEQ, MLA_HL - MLA_QK), F32)
    cos_h = jnp.concatenate([ones, cos, pad1], axis=1)
    sin_h = jnp.concatenate([0 * ones, sin, 0 * pad1], axis=1)
    return jnp.tile(cos_h, (1, MLA_HB)), jnp.tile(sin_h, (1, MLA_HB))


def _tokens_first(cache):
    b, h, l, d = cache.shape
    return jnp.transpose(cache, (0, 2, 1, 3)).reshape(b, l, h * d)


def _pad_heads(w, heads, hd, hl):
    k = w.shape[0]
    return jnp.pad(w.reshape(k, heads, hd), ((0, 0), (0, 0), (0, hl - hd))).reshape(k, heads * hl)


def kernel(x_prompt, x_sample, cache_l0_k, cache_l0_v, state_l1_fwd, state_l1_bwd, cache_l2_k,
           cache_l2_v, cache_l3_ckv, cache_l3_krope, c, c_ctx, ada_w, ada_b, norm_mix, norm_ffn,
           ffn_w_up, ffn_conv_w, ffn_conv_b, ffn_w_down, na_w_qkv, na_q_norm, na_k_norm, na_bias,
           na_w_o, gla_w_qkvg, gla_w_gate1, gla_w_gate2, gla_b_gate, gla_o_norm, gla_w_o,
           diff_w_qkv, diff_q_norm, diff_k_norm, diff_lambda, diff_sub_norm, diff_w_o, mla_w_dq,
           mla_q_a_norm, mla_w_uq, mla_w_dkv, mla_kv_a_norm, mla_w_ukv, mla_q_norm, mla_k_norm,
           mla_w_o):
    xr = _Rows(x_prompt.reshape(N_PROMPT, D_MODEL), x_sample.reshape(N_LATENT, D_MODEL), 0)
    cvecs = jnp.concatenate([c_ctx[None], c, jnp.zeros((5, D_MODEL), F32)], axis=0)
    mods_all = _ada_mods(cvecs, ada_w, ada_b)
    halves = lambda o_p, o_s: _Rows(o_p, o_s, 0)

    mods = mods_all[0]
    qkv = _norm_mod_proj(xr, norm_mix[0], mods, [na_w_qkv], "na_qkv")
    o_p, new_l0_k, new_l0_v = _na_prompt(qkv, na_q_norm, na_k_norm)
    o_s = _na_latent(qkv, cache_l0_k, cache_l0_v, _na_bias_blocks(na_bias), na_q_norm, na_k_norm)
    x = _out_proj_residual(xr, halves(o_p, o_s), mods, na_w_o, "na_out")
    ffn_weights = (ffn_w_up, ffn_conv_w, ffn_conv_b, ffn_w_down)
    x = _ffn(x, norm_ffn[0], mods, 0, *ffn_weights)
    xr = _one_array(x)

    mods = mods_all[1]
    w_decay = jnp.concatenate(
        [gla_w_gate1[0], gla_w_gate1[1],
         jnp.zeros((D_MODEL, 128 - 2 * GLA_GATE_RANK), F32)], axis=1)
    proj = _norm_mod_proj(xr, norm_mix[1], mods, [gla_w_qkvg, w_decay], "gla_proj")
    w2 = jnp.zeros((2, 128, GLA_HK), F32)
    w2 = w2.at[0, :GLA_GATE_RANK].set(gla_w_gate2[0])
    w2 = w2.at[1, GLA_GATE_RANK:2 * GLA_GATE_RANK].set(gla_w_gate2[1])
    bg = gla_b_gate.reshape(2, 1, GLA_HK)
    o_p, new_l1_fwd, new_l1_bwd = _gla(proj, w2, bg, gla_o_norm, BATCH, SEQ, 0)
    o_s = _gla(proj, w2, bg, gla_o_norm, DEC_BATCH, DEC_SEQ, N_PROMPT // DEC_SEQ,
               states=(state_l1_fwd, state_l1_bwd))
    x = _out_proj_residual(xr, halves(o_p, o_s), mods, gla_w_o, "gla_out",
                           gate=proj, gate_col_block=(2 * GLA_HK + GLA_HV) // GLA_HV)
    x = _ffn(x, norm_ffn[1], mods, 1, *ffn_weights)
    xr = _one_array(x)

    mods = mods_all[2]
    qkv = _norm_mod_proj(xr, norm_mix[2], mods, [diff_w_qkv], "diff_qkv")
    o_p, kn_p, new_l2_v = _diff_prompt(qkv, diff_q_norm, diff_k_norm, diff_lambda, diff_sub_norm)
    new_l2_k = kn_p.reshape(BATCH, 2 * DIFF_HEADS, SEQ, DIFF_HD)
    cos_d, sin_d = _diff_rope_tables()
    o_s = _diff_latent(qkv, cache_l2_k, cache_l2_v, cos_d, sin_d, diff_q_norm, diff_k_norm,
                       diff_lambda, diff_sub_norm)
    x = _out_proj_residual(xr, halves(o_p, o_s), mods, diff_w_o, "diff_out")
    x = _ffn(x, norm_ffn[2], mods, 2, *ffn_weights)
    xr = _one_array(x)

    mods = mods_all[3]
    zc = lambda n: jnp.zeros((D_MODEL, n), F32)
    w_low = jnp.concatenate(
        [mla_w_dq, zc(MLA_LOW_KV - MLA_Q_RANK), mla_w_dkv[:, :MLA_KV_RANK],
         zc(MLA_NOPE), mla_w_dkv[:, MLA_KV_RANK:], zc(MLA_HL - MLA_QK)], axis=1)
    low = _norm_mod_proj(xr, norm_mix[3], mods, [w_low], "mla_down")
    w_uq = _pad_heads(mla_w_uq, MLA_HEADS, MLA_QK, MLA_HL)
    qp, _ = _rms_matmul(low, 0, MLA_Q_RANK, mla_q_a_norm, w_uq, True, "mla_uq")
    kvp, ckv = _rms_matmul(low, MLA_LOW_KV // MLA_KV_RANK, MLA_KV_RANK, mla_kv_a_norm, mla_w_ukv,
                           True, "mla_ukv")
    kvc, _ = _rms_matmul(cache_l3_ckv.reshape(DEC_BATCH * PAST_LEN, MLA_KV_RANK), 0, MLA_KV_RANK,
                         mla_kv_a_norm, mla_w_ukv, False, "mla_ukv_cache")
    krc = jnp.pad(cache_l3_krope.reshape(DEC_BATCH * PAST_LEN, MLA_ROPE),
                  ((0, 0), (MLA_NOPE, MLA_HL - MLA_QK)))
    pad_gain = lambda g: jnp.tile(jnp.pad(g, (0, MLA_HL - MLA_QK)), MLA_HB).reshape(1, -1)
    gq, gk = pad_gain(mla_q_norm), pad_gain(mla_k_norm)
    o_p = _mla_prompt(qp, kvp, low, gq, gk)
    cos_m, sin_m = _mla_rope_tables()
    o_s = _mla_latent(qp, kvp, low, kvc, krc, cos_m, sin_m, gq, gk)
    new_l3_ckv = ckv[:N_PROMPT].reshape(BATCH, SEQ, MLA_KV_RANK)
    new_l3_krope = low[:N_PROMPT, MLA_LOW_KR + MLA_NOPE:MLA_LOW_KR + MLA_QK].reshape(
        BATCH, SEQ, MLA_ROPE)
    x = _out_proj_residual(xr, halves(o_p, o_s), mods, mla_w_o, "mla_out")
    n_pt = N_PROMPT // TOK_TILE
    ffn3 = functools.partial(_ffn, x, norm_ffn[3], mods, 3, *ffn_weights)
    y_prompt = ffn3(tile0=0, n_tiles=n_pt).reshape(BATCH, SEQ, D_MODEL)
    y_sample = ffn3(tile0=n_pt, n_tiles=N_TOK_TILES - n_pt).reshape(DEC_BATCH, DEC_SEQ, D_MODEL)
    return (y_prompt, y_sample, new_l0_k, new_l0_v, new_l1_fwd, new_l1_bwd, new_l2_k, new_l2_v,
            new_l3_ckv, new_l3_krope)
```

```python
import functools
import math
from typing import NamedTuple

import jax
import jax.numpy as jnp
from jax import lax
from jax.experimental import pallas as pl
from jax.experimental.pallas import tpu as pltpu

F32 = jnp.float32
BF16 = jnp.bfloat16

D_MODEL = 1024
BATCH = 16
SEQ = 256
DEPTH = 4
DEC_BATCH = 2
DEC_SEQ = 2048
PAST_LEN = 256
GRID_W = 64
D_FF = 2816
EPS = 1e-6
ROPE_BASE = 10000.0

NA_HEADS = 16
NA_HD = 64
NA_WIN_R = 8
NA_WIN_C = 16

GLA_HEADS = 4
GLA_DK = 128
GLA_DV = 256
GLA_HK = GLA_HEADS * GLA_DK
GLA_HV = GLA_HEADS * GLA_DV
GLA_GATE_RANK = 16
GLA_GATE_NORM = 16.0

DIFF_HEADS = 8
DIFF_HD = 64
DIFF_LAMBDA_INIT = 0.8 - 0.6 * math.exp(-0.3 * 2)

MLA_HEADS = 16
MLA_Q_RANK = 384
MLA_KV_RANK = 256
MLA_NOPE = 64
MLA_ROPE = 32
MLA_V = 64
MLA_QK = MLA_NOPE + MLA_ROPE

N_PROMPT = BATCH * SEQ
N_LATENT = DEC_BATCH * DEC_SEQ
N_TOK = N_PROMPT + N_LATENT
TOK_TILE = 2048
N_TOK_TILES = N_TOK // TOK_TILE
FF_CHUNK = 256
N_FF_CHUNKS = D_FF // FF_CHUNK
NEG = -1e30
LOG2E = math.log2(math.e)
SAFE_SCORE_BOUND = 48.0

VMEM_LIMIT = 56 * 1024 * 1024

_NT = (((1,), (1,)), ((), ()))
_TN = (((0,), (0,)), ((), ()))


def _params(sem, vmem=VMEM_LIMIT):
    return pltpu.CompilerParams(dimension_semantics=sem, vmem_limit_bytes=vmem)


def _log2(n):
    assert n & (n - 1) == 0
    return n.bit_length() - 1


def _silu(x):
    return x / (1.0 + jnp.exp(-x))


def _bdot(a, b):
    return jnp.dot(a, b, preferred_element_type=F32)


def _softmax_parts(parts, shift=None):
    m = shift
    if m is None:
        m = parts[0].max(axis=-1, keepdims=True)
        for s in parts[1:]:
            m = jnp.maximum(m, s.max(axis=-1, keepdims=True))
    ps = [jnp.exp2(s - m) for s in parts]
    l = ps[0].sum(axis=-1, keepdims=True)
    for p in ps[1:]:
        l = l + p.sum(axis=-1, keepdims=True)
    return ps, 1.0 / l


def _group_rms(x, gain, group, n_real=None, sums_on_mxu=False):
    lanes = x.shape[-1]
    n_real = n_real or group
    x2 = x * x
    if group == lanes:
        ms = jnp.sum(x2, axis=-1, keepdims=True)
    elif not sums_on_mxu:
        gid = lax.broadcasted_iota(jnp.int32, x.shape, 1) >> _log2(group)
        ms = jnp.zeros_like(x)
        for i in range(lanes // group):
            sel = gid == i
            si = jnp.sum(jnp.where(sel, x2, 0.0), axis=-1, keepdims=True)
            ms = jnp.where(sel, si, ms)
    else:
        r = lax.broadcasted_iota(jnp.int32, (lanes, lanes), 0) >> _log2(group)
        c = lax.broadcasted_iota(jnp.int32, (lanes, lanes), 1) >> _log2(group)
        ones = jnp.where(r == c, 1.0, 0.0).astype(BF16)
        hi = x2.astype(BF16)
        lo = (x2 - hi.astype(F32)).astype(BF16)
        ms = _bdot(hi, ones) + _bdot(lo, ones)
    return x * lax.rsqrt(ms * (1.0 / n_real) + EPS) * gain


def _with_score_bound(bound, body):
    ok = bound.max() < SAFE_SCORE_BOUND
    pl.when(ok)(lambda: body(bound))
    pl.when(jnp.logical_not(ok))(lambda: body(None))


def _rms_norm_bound(g_ref, n, head_lanes):
    g = jnp.abs(g_ref[...])
    heads = g.shape[-1] // head_lanes
    return n ** 0.5 * jnp.concatenate(
        [g[:, h * head_lanes:(h + 1) * head_lanes].max(axis=-1, keepdims=True)
         for h in range(heads)], axis=-1)


def _stacked_rows(per_head, rows):
    r = lax.broadcasted_iota(jnp.int32, (per_head.shape[-1] * rows, 1), 0)
    out = per_head[:, 0:1]
    for h in range(1, per_head.shape[-1]):
        out = jnp.where(r >= h * rows, per_head[:, h:h + 1], out)
    return out


def _max_head_norms(x, head_lanes):
    hid = lax.broadcasted_iota(jnp.int32, x.shape, 1) >> _log2(head_lanes)
    x2 = x * x
    sq = [jnp.sum(jnp.where(hid == h, x2, 0.0), axis=-1, keepdims=True).max(axis=0, keepdims=True)
          for h in range(x.shape[-1] // head_lanes)]
    return jnp.sqrt(jnp.concatenate(sq, axis=-1))


def _rope(x, cos, sin, half):
    lanes = x.shape[-1]
    lane = lax.broadcasted_iota(jnp.int32, x.shape, 1)
    up = pltpu.roll(x, lanes - half, axis=1)
    dn = pltpu.roll(x, half, axis=1)
    swapped = jnp.where((lane & (2 * half - 1)) < half, up, dn)
    return x * cos + swapped * sin


def _stack_heads(q, n_heads, head_lanes):
    hid = lax.broadcasted_iota(jnp.int32, q.shape, 1) >> _log2(head_lanes)
    zero = jnp.zeros_like(q)
    return jnp.concatenate([jnp.where(hid == i, q, zero) for i in range(n_heads)], axis=0)


def _unstack_heads(o, n_heads, head_lanes):
    rows = o.shape[0] // n_heads
    hid = lax.broadcasted_iota(jnp.int32, (rows, o.shape[1]), 1) >> _log2(head_lanes)
    out = o[0:rows]
    for i in range(1, n_heads):
        out = jnp.where(hid == i, o[i * rows:(i + 1) * rows], out)
    return out


ADA_TN = 1536


def _ada_kernel(c_ref, w_ref, b_ref, o_ref):
    s = _silu(c_ref[...])
    o_ref[0] = jnp.dot(s, w_ref[0], preferred_element_type=F32,
                       precision=lax.Precision.HIGHEST) + b_ref[0]


def _ada_mods(cvecs, ada_w, ada_b):
    out = pl.pallas_call(
        _ada_kernel,
        grid=(DEPTH, 6 * D_MODEL // ADA_TN),
        in_specs=[pl.BlockSpec((8, D_MODEL), lambda l, j: (0, 0)),
                  pl.BlockSpec((1, D_MODEL, ADA_TN), lambda l, j: (l, 0, j)),
                  pl.BlockSpec((1, 1, ADA_TN), lambda l, j: (l, 0, j))],
        out_specs=pl.BlockSpec((1, 8, ADA_TN), lambda l, j: (l, 0, j)),
        out_shape=jax.ShapeDtypeStruct((DEPTH, 8, 6 * D_MODEL), F32),
        compiler_params=_params(("arbitrary", "arbitrary")),
        name="ada_mod",
    )(cvecs, ada_w, ada_b.reshape(DEPTH, 1, 6 * D_MODEL))
    return out.reshape(DEPTH, 8, 6, D_MODEL)[:, :3]


def _mod_group_of_tile(i):
    return jnp.maximum(i - (N_PROMPT // TOK_TILE - 1), 0)


def _norm_mod_rows(x_ref, g_ref, mod_ref, h_ref, shift_idx, scale_idx, rows=64):
    g = g_ref[...]
    sc = 1.0 + mod_ref[0, scale_idx:scale_idx + 1, :]
    sh = mod_ref[0, shift_idx:shift_idx + 1, :]

    def body(r, carry):
        sl = pl.ds(pl.multiple_of(r * rows, rows), rows)
        xf = x_ref[sl, :]
        ms = jnp.mean(xf * xf, axis=-1, keepdims=True)
        y = xf * lax.rsqrt(ms + EPS) * g
        h_ref[sl, :] = (y * sc + sh).astype(BF16)
        return carry

    lax.fori_loop(0, x_ref.shape[0] // rows, body, 0)


ROW_TM = 512


class _Rows(NamedTuple):
    prompt: jax.Array
    latent: jax.Array
    latent_row0: int


def _one_array(x):
    return _Rows(x, x, N_PROMPT)


def _row_specs(rows, width, col_block=0):
    n_p = N_PROMPT // ROW_TM
    l0 = rows.latent_row0 // ROW_TM
    return [pl.BlockSpec((ROW_TM, width), lambda t: (jnp.minimum(t, n_p - 1), col_block)),
            pl.BlockSpec((ROW_TM, width), lambda t: (l0 + jnp.maximum(t - n_p, 0), col_block))]


def _row_group(t):
    first_latent = N_PROMPT // ROW_TM
    return jnp.where(t < first_latent, 0, 1 + (t - first_latent) // (DEC_SEQ // ROW_TM))


def _is_prompt_tile():
    return pl.program_id(0) < N_PROMPT // ROW_TM


def _proj_kernel(xp_ref, xl_ref, g_ref, mod_ref, *refs):
    *w_refs, o_ref, h_ref, wb_ref = refs

    @pl.when(pl.program_id(0) == 0)
    def _():
        off = 0
        for w_ref in w_refs:
            wb_ref[:, off:off + w_ref.shape[1]] = w_ref[...].astype(BF16)
            off += w_ref.shape[1]

    is_prompt = _is_prompt_tile()
    g = g_ref[...]
    sc = 1.0 + mod_ref[0, 1:2, :]
    sh = mod_ref[0, 0:1, :]
    rows = 64
    part = ROW_TM // 2
    for p0 in range(0, ROW_TM, part):
        for r0 in range(p0, p0 + part, rows):
            sl = slice(r0, r0 + rows)
            xf = jnp.where(is_prompt, xp_ref[sl, :], xl_ref[sl, :])
            ms = jnp.mean(xf * xf, axis=-1, keepdims=True)
            y = xf * lax.rsqrt(ms + EPS) * g
            h_ref[sl, :] = (y * sc + sh).astype(BF16)
        o_ref[p0:p0 + part, :] = _bdot(h_ref[p0:p0 + part, :], wb_ref[...])


def _norm_mod_proj(x, g, mods, ws, name):
    n = sum(w.shape[1] for w in ws)
    return pl.pallas_call(
        _proj_kernel,
        grid=(N_TOK // ROW_TM,),
        in_specs=_row_specs(x, D_MODEL) + [
            pl.BlockSpec((1, D_MODEL), lambda t: (0, 0)),
            pl.BlockSpec((1, 6, D_MODEL), lambda t: (_row_group(t), 0, 0))] + [
            pl.BlockSpec(w.shape, lambda t: (0, 0), pipeline_mode=pl.Buffered(1)) for w in ws],
        out_specs=pl.BlockSpec((ROW_TM, n), lambda t: (t, 0)),
        out_shape=jax.ShapeDtypeStruct((N_TOK, n), F32),
        scratch_shapes=[pltpu.VMEM((ROW_TM, D_MODEL), BF16),
                        pltpu.VMEM((D_MODEL, n), BF16)],
        compiler_params=_params(("arbitrary",)),
        name=name,
    )(x.prompt, x.latent, g.reshape(1, D_MODEL), mods, *ws)


def _rms_matmul_kernel(a_ref, g_ref, w_ref, o_ref, n_ref, *, normalise):
    a = a_ref[...]
    if normalise:
        a = a * lax.rsqrt(jnp.mean(a * a, axis=-1, keepdims=True) + EPS) * g_ref[...]
    n_ref[...] = a
    o_ref[...] = _bdot(a.astype(BF16), w_ref[...].astype(BF16))


def _rms_matmul(a, col_block, k, g, w, normalise, name, tm=512):
    rows, n = a.shape[0], w.shape[1]
    return pl.pallas_call(
        functools.partial(_rms_matmul_kernel, normalise=normalise),
        grid=(rows // tm,),
        in_specs=[pl.BlockSpec((tm, k), lambda i: (i, col_block)),
                  pl.BlockSpec((1, k), lambda i: (0, 0)),
                  pl.BlockSpec((k, n), lambda i: (0, 0))],
        out_specs=[pl.BlockSpec((tm, n), lambda i: (i, 0)),
                   pl.BlockSpec((tm, k), lambda i: (i, 0))],
        out_shape=[jax.ShapeDtypeStruct((rows, n), F32),
                   jax.ShapeDtypeStruct((rows, k), F32)],
        compiler_params=_params(("arbitrary",)),
        name=name,
    )(a, g.reshape(1, k), w)


def _oproj_kernel(*refs, gated):
    if gated:
        xp_ref, xl_ref, ap_ref, al_ref, g_ref, mod_ref, w_ref, o_ref, wb_ref = refs
    else:
        xp_ref, xl_ref, ap_ref, al_ref, mod_ref, w_ref, o_ref, wb_ref = refs

    @pl.when(pl.program_id(0) == 0)
    def _():
        wb_ref[...] = w_ref[...].astype(BF16)

    is_prompt = _is_prompt_tile()
    a = jnp.where(is_prompt, ap_ref[...], al_ref[...])
    if gated:
        a = a * _silu(g_ref[...])
    y = _bdot(a.astype(BF16), wb_ref[...])
    x = jnp.where(is_prompt, xp_ref[...], xl_ref[...])
    o_ref[...] = x + mod_ref[0, 2:3, :] * y


def _out_proj_residual(x, a, mods, w, name, gate=None, gate_col_block=0):
    k = w.shape[0]
    in_specs = _row_specs(x, D_MODEL) + _row_specs(a, k)
    args = [x.prompt, x.latent, a.prompt, a.latent]
    if gate is not None:
        in_specs.append(pl.BlockSpec((ROW_TM, k), lambda t: (t, gate_col_block)))
        args.append(gate)
    in_specs += [pl.BlockSpec((1, 6, D_MODEL), lambda t: (_row_group(t), 0, 0)),
                 pl.BlockSpec((k, D_MODEL), lambda t: (0, 0))]
    args += [mods, w]
    return pl.pallas_call(
        functools.partial(_oproj_kernel, gated=gate is not None),
        grid=(N_TOK // ROW_TM,),
        in_specs=in_specs,
        out_specs=pl.BlockSpec((ROW_TM, D_MODEL), lambda t: (t, 0)),
        out_shape=jax.ShapeDtypeStruct((N_TOK, D_MODEL), F32),
        scratch_shapes=[pltpu.VMEM((k, D_MODEL), BF16)],
        compiler_params=_params(("arbitrary",)),
        name=name,
    )(*args)


FFN_MM_ROWS = 512
FFN_ROWS = 64
FFN_PAD = 8


def _ffn_kernel(x_ref, g_ref, mod_ref, wg_ref, wv_ref, cwg_ref, cwv_ref, cbg_ref, cbv_ref,
                wd_ref, o_ref, h_ref, u_ref, act_ref, wup_ref, wdn_ref, *, tile0):
    i = tile0 + pl.program_id(0)
    c = pl.program_id(1)
    fc = FF_CHUNK

    @pl.when(c == 0)
    def _():
        _norm_mod_rows(x_ref, g_ref, mod_ref, h_ref, 3, 4)
        zeros = jnp.zeros((FFN_PAD, 2 * fc), F32)
        u_ref[0:FFN_PAD, :] = zeros
        u_ref[FFN_PAD + TOK_TILE:, :] = zeros
        o_ref[...] = jnp.zeros_like(o_ref)

    wup_ref[:, :fc] = wg_ref[...].astype(BF16)
    wup_ref[:, fc:] = wv_ref[...].astype(BF16)
    wdn_ref[...] = wd_ref[...].astype(BF16)
    taps = lambda cw_ref, cb_ref: [jnp.broadcast_to(cw_ref[j:j + 1, :], (FFN_ROWS, fc))
                                   for j in range(3)] + [
                                       jnp.broadcast_to(cb_ref[...], (FFN_ROWS, fc))]
    taps_g, taps_v = taps(cwg_ref, cbg_ref), taps(cwv_ref, cbv_ref)
    seq_len = jnp.where(i < N_PROMPT // TOK_TILE, SEQ, DEC_SEQ)
    row = lax.broadcasted_iota(jnp.int32, (FFN_ROWS, 1), 0)

    def up(t):
        r0 = t * FFN_MM_ROWS
        u_ref[FFN_PAD + r0:FFN_PAD + r0 + FFN_MM_ROWS, :] = _bdot(
            h_ref[r0:r0 + FFN_MM_ROWS, :], wup_ref[...])

    def conv_act(t):
        for r0 in range(t * FFN_MM_ROWS, (t + 1) * FFN_MM_ROWS, FFN_ROWS):
            halves = []
            for lo, (w0, w1, w2, bias) in ((0, taps_g), (fc, taps_v)):
                p0 = FFN_PAD + r0
                prev = u_ref[p0 - 1:p0 - 1 + FFN_ROWS, lo:lo + fc]
                mid = u_ref[p0:p0 + FFN_ROWS, lo:lo + fc]
                nxt = u_ref[p0 + 1:p0 + 1 + FFN_ROWS, lo:lo + fc]
                if r0 % SEQ == 0:
                    prev = jnp.where(((r0 + row) & (seq_len - 1)) == 0, 0.0, prev)
                if (r0 + FFN_ROWS) % SEQ == 0:
                    nxt = jnp.where(((r0 + row) & (seq_len - 1)) == seq_len - 1, 0.0, nxt)
                halves.append(prev * w0 + mid * w1 + nxt * w2 + bias)
            act_ref[r0:r0 + FFN_ROWS, :] = (_silu(halves[0]) * halves[1]).astype(BF16)

    def down(t):
        r0 = t * FFN_MM_ROWS
        o_ref[r0:r0 + FFN_MM_ROWS, :] += _bdot(act_ref[r0:r0 + FFN_MM_ROWS, :], wdn_ref[...])

    n = TOK_TILE // FFN_MM_ROWS
    for s in range(n + 2):
        if s < n:
            up(s)
        if 1 <= s <= n:
            conv_act(s - 1)
        if s >= 2:
            down(s - 2)

    @pl.when(c == N_FF_CHUNKS - 1)
    def _():
        o_ref[...] = x_ref[...] + mod_ref[0, 5:6, :] * o_ref[...]


def _ffn(x, g, mods, layer, w_up, conv_w, conv_b, w_down, tile0=0, n_tiles=N_TOK_TILES):
    fc = FF_CHUNK
    ncb = N_FF_CHUNKS
    return pl.pallas_call(
        functools.partial(_ffn_kernel, tile0=tile0),
        grid=(n_tiles, ncb),
        in_specs=[pl.BlockSpec((TOK_TILE, D_MODEL), lambda i, c: (tile0 + i, 0)),
                  pl.BlockSpec((1, D_MODEL), lambda i, c: (0, 0)),
                  pl.BlockSpec((1, 6, D_MODEL), lambda i, c: (_mod_group_of_tile(tile0 + i), 0, 0)),
                  pl.BlockSpec((None, D_MODEL, fc), lambda i, c: (layer, 0, c)),
                  pl.BlockSpec((None, D_MODEL, fc), lambda i, c: (layer, 0, ncb + c)),
                  pl.BlockSpec((None, 3, fc), lambda i, c: (layer, 0, c)),
                  pl.BlockSpec((None, 3, fc), lambda i, c: (layer, 0, ncb + c)),
                  pl.BlockSpec((None, 1, fc), lambda i, c: (layer, 0, c)),
                  pl.BlockSpec((None, 1, fc), lambda i, c: (layer, 0, ncb + c)),
                  pl.BlockSpec((None, fc, D_MODEL), lambda i, c: (layer, c, 0))],
        out_specs=pl.BlockSpec((TOK_TILE, D_MODEL), lambda i, c: (i, 0)),
        out_shape=jax.ShapeDtypeStruct((n_tiles * TOK_TILE, D_MODEL), F32),
        scratch_shapes=[pltpu.VMEM((TOK_TILE, D_MODEL), BF16),
                        pltpu.VMEM((TOK_TILE + 2 * FFN_PAD, 2 * fc), F32),
                        pltpu.VMEM((TOK_TILE, fc), BF16),
                        pltpu.VMEM((D_MODEL, 2 * fc), BF16),
                        pltpu.VMEM((fc, D_MODEL), BF16)],
        compiler_params=_params(("arbitrary", "arbitrary")),
        name="conv_ffn",
    )(x, g.reshape(1, D_MODEL), mods, w_up, w_up, conv_w, conv_w,
      conv_b.reshape(DEPTH, 1, -1), conv_b.reshape(DEPTH, 1, -1), w_down)


NA_HB = 4
NA_LANES = NA_HB * NA_HD
NA_ROWS = DEC_SEQ // GRID_W
NA_KEYS = NA_WIN_R * GRID_W
PROMPT_SEQS = 4
NA_ROW_UNROLL = 4


def _store_heads(dst_ref, seq, x, n_heads, hd):
    for h in range(n_heads):
        dst_ref[seq, h] = x[:, h * hd:(h + 1) * hd]


def _na_prompt_kernel(q_ref, k_ref, v_ref, gq_ref, gk_ref, o_ref, kn_ref, vn_ref):
    scale = NA_HD ** -0.5 * LOG2E

    def chains(bound):
        shift = None if bound is None else _stacked_rows(bound, SEQ)
        for seq in range(PROMPT_SEQS):
            sl = slice(seq * SEQ, (seq + 1) * SEQ)
            q = _group_rms(q_ref[sl, :], gq_ref[...], NA_HD, sums_on_mxu=True) * scale
            k = _group_rms(k_ref[sl, :], gk_ref[...], NA_HD, sums_on_mxu=True)
            v = v_ref[sl, :]
            _store_heads(kn_ref, seq, k, NA_HB, NA_HD)
            _store_heads(vn_ref, seq, v, NA_HB, NA_HD)
            q4 = _stack_heads(q, NA_HB, NA_HD).astype(BF16)
            s = lax.dot_general(q4, k.astype(BF16), _NT, preferred_element_type=F32)
            (p,), inv = _softmax_parts([s], shift)
            o4 = _bdot(p.astype(BF16), v.astype(BF16)) * inv
            o_ref[sl, :] = _unstack_heads(o4, NA_HB, NA_HD)

    _with_score_bound(scale * _rms_norm_bound(gq_ref, NA_HD, NA_HD)
                      * _rms_norm_bound(gk_ref, NA_HD, NA_HD), chains)


def _na_prompt(qkv, gq, gk):
    nb = NA_HEADS // NA_HB
    rows = PROMPT_SEQS * SEQ
    blk = lambda off: pl.BlockSpec((rows, NA_LANES), lambda b, j: (b, off + j))
    vec = pl.BlockSpec((1, NA_LANES), lambda b, j: (0, 0))
    cache = pl.BlockSpec((PROMPT_SEQS, NA_HB, SEQ, NA_HD), lambda b, j: (b, j, 0, 0))
    cache_shape = jax.ShapeDtypeStruct((BATCH, NA_HEADS, SEQ, NA_HD), F32)
    return pl.pallas_call(
        _na_prompt_kernel,
        grid=(BATCH // PROMPT_SEQS, nb),
        in_specs=[blk(0), blk(nb), blk(2 * nb), vec, vec],
        out_specs=[blk(0), cache, cache],
        out_shape=[jax.ShapeDtypeStruct((N_PROMPT, D_MODEL), F32), cache_shape, cache_shape],
        compiler_params=_params(("arbitrary", "arbitrary")),
        name="na_prompt",
    )(qkv, qkv, qkv, jnp.tile(gq, NA_HB).reshape(1, -1), jnp.tile(gk, NA_HB).reshape(1, -1))


def _na_latent_kernel(q_ref, k_ref, v_ref, kc_ref, vc_ref, t_ref, gq_ref, gk_ref, o_ref,
                      qn_ref, kn_ref, vb_ref, kc4_ref, vc4_ref, bias_ref):
    scale = NA_HD ** -0.5 * LOG2E
    rows = 256

    def prep(r, carry):
        sl = pl.ds(pl.multiple_of(r * rows, rows), rows)
        qn_ref[sl, :] = (_group_rms(q_ref[sl, :], gq_ref[...], NA_HD, sums_on_mxu=True)
                         * scale).astype(BF16)
        kn_ref[sl, :] = _group_rms(k_ref[sl, :], gk_ref[...], NA_HD,
                                   sums_on_mxu=True).astype(BF16)
        vb_ref[sl, :] = v_ref[sl, :].astype(BF16)
        return carry

    lax.fori_loop(0, DEC_SEQ // rows, prep, 0)
    kc = kc_ref[0]
    kc4_ref[...] = kc.astype(BF16)
    vc4_ref[...] = vc_ref[0].astype(BF16)

    def attend(bound):
        for h in range(NA_HB):
            off = 0.0 if bound is None else bound[:, h:h + 1]
            for p in range(NA_WIN_R):
                for i in range(NA_WIN_R):
                    bias_ref[h, p, :, i * GRID_W:(i + 1) * GRID_W] = t_ref[h, p + i] - off
        shift = None if bound is None else _stacked_rows(bound, GRID_W)

        def row(r):
            kr0 = jnp.clip(r - NA_WIN_R // 2, 0, NA_ROWS - NA_WIN_R)
            pat = kr0 - r + NA_WIN_R - 1
            qs = pl.ds(pl.multiple_of(r * GRID_W, GRID_W), GRID_W)
            ks = pl.ds(pl.multiple_of(kr0 * GRID_W, GRID_W), NA_KEYS)
            q4 = _stack_heads(qn_ref[qs, :], NA_HB, NA_HD)
            s_loc = lax.dot_general(q4, kn_ref[ks, :], _NT, preferred_element_type=F32)
            s_loc = s_loc + jnp.concatenate([bias_ref[h, pat] for h in range(NA_HB)], axis=0)
            s_ctx = lax.dot_general(q4, kc4_ref[...], _NT, preferred_element_type=F32)
            if shift is None:
                (p_loc, p_ctx), inv = _softmax_parts([s_loc, s_ctx])
            else:
                p_loc, p_ctx = jnp.exp2(s_loc), jnp.exp2(s_ctx - shift)
                inv = 1.0 / (p_loc.sum(axis=-1, keepdims=True) + p_ctx.sum(axis=-1, keepdims=True))
            o4 = _bdot(p_loc.astype(BF16), vb_ref[ks, :]) + _bdot(p_ctx.astype(BF16), vc4_ref[...])
            o_ref[qs, :] = _unstack_heads(o4 * inv, NA_HB, NA_HD)

        def rows_step(i, carry):
            for u in range(NA_ROW_UNROLL):
                row(i * NA_ROW_UNROLL + u)
            return carry

        lax.fori_loop(0, NA_ROWS // NA_ROW_UNROLL, rows_step, 0)

    qmax = scale * _rms_norm_bound(gq_ref, NA_HD, NA_HD)
    bias_max = jnp.concatenate(
        [t_ref[h].max(axis=0).max(axis=0, keepdims=True).max(axis=1, keepdims=True)
         for h in range(NA_HB)], axis=-1)
    _with_score_bound(
        jnp.maximum(qmax * _rms_norm_bound(gk_ref, NA_HD, NA_HD) + bias_max,
                    qmax * _max_head_norms(kc, NA_HD)), attend)


def _na_bias_blocks(bias_table):
    qc = jnp.arange(GRID_W)[:, None]
    kc = jnp.arange(GRID_W)[None, :]
    win0 = jnp.clip(qc - NA_WIN_C // 2, 0, GRID_W - NA_WIN_C)
    valid = (kc >= win0) & (kc < win0 + NA_WIN_C)
    n_ro, n_co = bias_table.shape[1:]
    c = NA_WIN_C - 1
    period = jnp.concatenate(
        [bias_table[..., c:], jnp.zeros((NA_HEADS, n_ro, 2 * GRID_W - n_co), F32),
         bias_table[..., :c]], axis=-1)
    flat = jnp.tile(period, (1, 1, GRID_W))[..., :GRID_W * (2 * GRID_W - 1)]
    t = flat.reshape(NA_HEADS, n_ro, GRID_W, 2 * GRID_W - 1)[..., :GRID_W]
    return jnp.where(valid, t * LOG2E, NEG)


def _na_latent(qkv, cache_k, cache_v, bias_blocks, gq, gk):
    nb = NA_HEADS // NA_HB
    lat0 = N_PROMPT // DEC_SEQ
    blk = lambda off: pl.BlockSpec((DEC_SEQ, NA_LANES), lambda b, j: (lat0 + b, off + j))
    vec = pl.BlockSpec((1, NA_LANES), lambda b, j: (0, 0))
    cache = pl.BlockSpec((1, PAST_LEN, NA_LANES), lambda b, j: (b, 0, j))
    return pl.pallas_call(
        _na_latent_kernel,
        grid=(DEC_BATCH, nb),
        in_specs=[blk(0), blk(nb), blk(2 * nb), cache, cache,
                  pl.BlockSpec((NA_HB, 2 * NA_WIN_R - 1, GRID_W, GRID_W), lambda b, j: (j, 0, 0, 0)),
                  vec, vec],
        out_specs=pl.BlockSpec((DEC_SEQ, NA_LANES), lambda b, j: (b, j)),
        out_shape=jax.ShapeDtypeStruct((N_LATENT, D_MODEL), F32),
        scratch_shapes=[pltpu.VMEM((DEC_SEQ, NA_LANES), BF16),
                        pltpu.VMEM((DEC_SEQ, NA_LANES), BF16),
                        pltpu.VMEM((DEC_SEQ, NA_LANES), BF16),
                        pltpu.VMEM((PAST_LEN, NA_LANES), BF16),
                        pltpu.VMEM((PAST_LEN, NA_LANES), BF16),
                        pltpu.VMEM((NA_HB, NA_WIN_R, GRID_W, NA_KEYS), F32)],
        compiler_params=_params(("arbitrary", "arbitrary")),
        name="na_latent",
    )(qkv, qkv, qkv, _tokens_first(cache_k), _tokens_first(cache_v), bias_blocks,
      jnp.tile(gq, NA_HB).reshape(1, -1), jnp.tile(gk, NA_HB).reshape(1, -1))


GLA_C = 128
GLA_HPS = 2
GLA_SUB = 8
GLA_LEVELS = (64, 32, 16, 8)


def _split_hi_lo(x):
    hi = x.astype(BF16)
    lo = (x - hi.astype(F32)).astype(BF16)
    return jnp.concatenate([hi, lo], axis=1)


class _GlaMasks(NamedTuple):
    tri: jax.Array
    later: tuple
    pair: tuple
    diag: jax.Array


def _gla_masks(rev):
    c = GLA_C
    row = lax.broadcasted_iota(jnp.int32, (c, c), 0)
    col = lax.broadcasted_iota(jnp.int32, (c, c), 1)
    rid = lax.broadcasted_iota(jnp.int32, (c, GLA_DK), 0)
    causal = (col >= row) if rev else (col <= row)
    later, pair = [], []
    for m in GLA_LEVELS:
        later.append(((rid & m) == 0) if rev else ((rid & m) != 0))
        same = (row >> _log2(2 * m)) == (col >> _log2(2 * m))
        crossing = ((row & m) != (col & m))
        pair.append(same & crossing & causal)
    diag = ((row >> _log2(GLA_SUB)) == (col >> _log2(GLA_SUB))) & causal
    return _GlaMasks(jnp.where(causal, 1.0, 0.0).astype(BF16), tuple(later), tuple(pair), diag)


def _gla_chunk(q, k, v, g, st_ref, rev, masks):
    c = GLA_C
    cs = _bdot(masks.tri, _split_hi_lo(g))
    b = cs[:, :GLA_DK] + cs[:, GLA_DK:]

    a = None
    for m, later, pair in zip(GLA_LEVELS, masks.later, masks.pair):
        nblk = c // (2 * m)
        if rev:
            bnd = [b[j * 2 * m + m:j * 2 * m + m + 1] for j in range(nblk)]
        else:
            bnd = [b[j * 2 * m + m - 1:j * 2 * m + m] for j in range(nblk)]
        ref = jnp.concatenate([jnp.broadcast_to(x, (2 * m, GLA_DK)) for x in bnd], axis=0)
        d = b - ref
        d = jnp.minimum(jnp.where(later, d, -d), 0.0)
        x = (jnp.where(later, q, k) * jnp.exp(d)).astype(BF16)
        blk = lax.dot_general(x, x, _NT, preferred_element_type=F32)
        a = jnp.where(pair, blk, 0.0 if a is None else a)

    nsub = c // GLA_SUB
    lane_c = lax.broadcasted_iota(jnp.int32, (GLA_SUB, c), 1)
    diag_rows = []
    for blk_i in range(nsub):
        r0 = blk_i * GLA_SUB
        qb = q[r0:r0 + GLA_SUB]
        bb = b[r0:r0 + GLA_SUB]
        acc = jnp.zeros((GLA_SUB, c), F32)
        for s in range(GLA_SUB):
            ks = k[r0 + s:r0 + s + 1]
            bs = b[r0 + s:r0 + s + 1]
            w = jnp.sum(qb * ks * jnp.exp(jnp.minimum(bb - bs, 0.0)), axis=-1, keepdims=True)
            acc = jnp.where(lane_c == r0 + s, w, acc)
        diag_rows.append(acc)
    a = jnp.where(masks.diag, jnp.concatenate(diag_rows, axis=0), a)

    st = st_ref[...]
    inter = lax.dot_general((q * jnp.exp(b)).astype(BF16), st.astype(BF16), _NT,
                            preferred_element_type=F32)
    o = inter + _bdot(a.astype(BF16), v.astype(BF16))

    btot = b[0:1] if rev else b[c - 1:c]
    kd = (k * jnp.exp(btot - b)).astype(BF16)
    st_ref[...] = st * jnp.exp(btot) + lax.dot_general(v.astype(BF16), kd, _TN,
                                                       preferred_element_type=F32)
    return o


def _gla_kernel(*refs, n_tok, has_state):
    if has_state:
        (q_ref, k_ref, v_ref, r_ref, w2_ref, bg_ref, gn_ref, s0f_ref, s0b_ref,
         o_ref, lg_ref, of_ref, ob_ref, stf_ref, stb_ref) = refs
    else:
        (q_ref, k_ref, v_ref, r_ref, w2_ref, bg_ref, gn_ref,
         o_ref, sf_ref, sb_ref, lg_ref, of_ref, ob_ref, stf_ref, stb_ref) = refs
    nc = n_tok // GLA_C
    scale = GLA_DK ** -0.5

    rb = r_ref[...].astype(BF16)
    for z in range(2):
        x = _bdot(rb, w2_ref[z].astype(BF16)) + bg_ref[z]
        lg_ref[z] = (jnp.minimum(x, 0.0) - jnp.log1p(jnp.exp(-jnp.abs(x)))) * (1.0 / GLA_GATE_NORM)

    for hh in range(GLA_HPS):
        if has_state:
            stf_ref[hh] = s0f_ref[0, hh].T
            stb_ref[hh] = s0b_ref[0, hh].T
        else:
            stf_ref[hh] = jnp.zeros((GLA_DV, GLA_DK), F32)
            stb_ref[hh] = jnp.zeros((GLA_DV, GLA_DK), F32)

    masks = {rev: _gla_masks(rev) for rev in (False, True)}

    def step(ci, carry):
        for hh in range(GLA_HPS):
            kq = slice(hh * GLA_DK, (hh + 1) * GLA_DK)
            vv = slice(hh * GLA_DV, (hh + 1) * GLA_DV)
            for rev in (False, True):
                cc = (nc - 1 - ci) if rev else ci
                sl = pl.ds(pl.multiple_of(cc * GLA_C, GLA_C), GLA_C)
                o = _gla_chunk(q_ref[sl, kq] * scale, k_ref[sl, kq], v_ref[sl, vv],
                               lg_ref[1 if rev else 0, sl, kq],
                               (stb_ref if rev else stf_ref).at[hh], rev, masks[rev])
                (ob_ref if rev else of_ref)[sl, vv] = o
        return carry

    lax.fori_loop(0, nc, step, 0)

    for hh in range(GLA_HPS):
        vv = slice(hh * GLA_DV, (hh + 1) * GLA_DV)
        o = of_ref[:, vv] + ob_ref[:, vv]
        o_ref[:, vv] = o * lax.rsqrt(jnp.mean(o * o, axis=-1, keepdims=True) + EPS) * gn_ref[...]
        if not has_state:
            sf_ref[0, hh] = stf_ref[hh].T
            sb_ref[0, hh] = stb_ref[hh].T


def _gla(proj, w2, bg, gnorm, n_seq, n_tok, row_block0, states=None):
    kw, vw = GLA_HPS * GLA_DK, GLA_HPS * GLA_DV
    spec = lambda width, off: pl.BlockSpec((n_tok, width), lambda b, h: (row_block0 + b, off + h))
    in_specs = [spec(kw, 0), spec(kw, GLA_HK // kw), spec(vw, 2 * GLA_HK // vw),
                pl.BlockSpec((n_tok, 128), lambda b, h: (row_block0 + b, (2 * GLA_HK + 2 * GLA_HV) // 128)),
                pl.BlockSpec((2, 128, kw), lambda b, h: (0, 0, h)),
                pl.BlockSpec((2, 1, kw), lambda b, h: (0, 0, h)),
                pl.BlockSpec((1, GLA_DV), lambda b, h: (0, 0))]
    args = [proj, proj, proj, proj, w2, bg, gnorm.reshape(1, GLA_DV)]
    st_spec = pl.BlockSpec((1, GLA_HPS, GLA_DK, GLA_DV), lambda b, h: (b, h, 0, 0))
    o_spec = pl.BlockSpec((n_tok, vw), lambda b, h: (b, h))
    o_shape = jax.ShapeDtypeStruct((n_seq * n_tok, GLA_HV), F32)
    if states is not None:
        in_specs += [st_spec, st_spec]
        args += list(states)
        out_specs, out_shape = o_spec, o_shape
    else:
        st_shape = jax.ShapeDtypeStruct((n_seq, GLA_HEADS, GLA_DK, GLA_DV), F32)
        out_specs, out_shape = [o_spec, st_spec, st_spec], [o_shape, st_shape, st_shape]
    return pl.pallas_call(
        functools.partial(_gla_kernel, n_tok=n_tok, has_state=states is not None),
        grid=(n_seq, GLA_HEADS // GLA_HPS),
        in_specs=in_specs,
        out_specs=out_specs,
        out_shape=out_shape,
        scratch_shapes=[pltpu.VMEM((2, n_tok, kw), F32),
                        pltpu.VMEM((n_tok, vw), F32),
                        pltpu.VMEM((n_tok, vw), F32),
                        pltpu.VMEM((GLA_HPS, GLA_DV, GLA_DK), F32),
                        pltpu.VMEM((GLA_HPS, GLA_DV, GLA_DK), F32)],
        compiler_params=_params(("arbitrary", "arbitrary")),
        name="gla_latent" if states is not None else "gla_prompt",
    )(*args)


DIFF_HB = 2
DIFF_QL = DIFF_HB * DIFF_HD
DIFF_VL = DIFF_HB * 2 * DIFF_HD
DIFF_TQ = 512
DIFF_SUB = 128


def _diff_lambda(lam_ref):
    l = lam_ref[...]
    a = jnp.sum(l[0:1] * l[1:2], axis=-1, keepdims=True)
    b = jnp.sum(l[2:3] * l[3:4], axis=-1, keepdims=True)
    return jnp.exp(a) - jnp.exp(b) + DIFF_LAMBDA_INIT


def _diff_finish(ps, invs, lam, v, sn_ref):
    a = ps[0] - (lam * invs[1] / invs[0]) * ps[1]
    o2 = _bdot(a.astype(BF16), v) * invs[0]
    o = _unstack_heads(o2, DIFF_HB, 2 * DIFF_HD)
    return _group_rms(o, sn_ref[...], 2 * DIFF_HD) * (1.0 - DIFF_LAMBDA_INIT)


def _diff_prompt_kernel(q0_ref, q1_ref, k0_ref, k1_ref, v_ref, gq_ref, gk_ref, lam_ref, sn_ref,
                        o_ref, kn_ref, vn_ref):
    scale = DIFF_HD ** -0.5 * LOG2E
    lam = _diff_lambda(lam_ref)

    def chains(bound):
        shift = None if bound is None else _stacked_rows(bound, SEQ)
        for seq in range(PROMPT_SEQS):
            sl = slice(seq * SEQ, (seq + 1) * SEQ)
            ps, invs = [], []
            for comp, (q_ref, k_ref) in enumerate(((q0_ref, k0_ref), (q1_ref, k1_ref))):
                q = _group_rms(q_ref[sl, :], gq_ref[...], DIFF_HD, sums_on_mxu=True) * scale
                k = _group_rms(k_ref[sl, :], gk_ref[...], DIFF_HD, sums_on_mxu=True)
                for h in range(DIFF_HB):
                    kn_ref[seq, comp, h] = k[:, h * DIFF_HD:(h + 1) * DIFF_HD]
                q2 = _stack_heads(q, DIFF_HB, DIFF_HD).astype(BF16)
                s = lax.dot_general(q2, k.astype(BF16), _NT, preferred_element_type=F32)
                (p,), inv = _softmax_parts([s], shift)
                ps.append(p)
                invs.append(inv)
            v = v_ref[sl, :]
            _store_heads(vn_ref, seq, v, DIFF_HB, 2 * DIFF_HD)
            o_ref[sl, :] = _diff_finish(ps, invs, lam, v.astype(BF16), sn_ref)

    _with_score_bound(scale * _rms_norm_bound(gq_ref, DIFF_HD, DIFF_HD)
                      * _rms_norm_bound(gk_ref, DIFF_HD, DIFF_HD), chains)


def _diff_prompt(qkv, gq, gk, lam, sub_norm):
    nb = DIFF_HEADS // DIFF_HB
    rows = PROMPT_SEQS * SEQ
    qk = lambda off: pl.BlockSpec((rows, DIFF_QL), lambda b, j: (b, off + j))
    vec = lambda n: pl.BlockSpec((1, n), lambda b, j: (0, 0))
    v_spec = pl.BlockSpec((rows, DIFF_VL), lambda b, j: (b, 2 * D_MODEL // DIFF_VL + j))
    kn_spec = pl.BlockSpec((PROMPT_SEQS, 2, DIFF_HB, SEQ, DIFF_HD), lambda b, j: (b, 0, j, 0, 0))
    vn_spec = pl.BlockSpec((PROMPT_SEQS, DIFF_HB, SEQ, 2 * DIFF_HD), lambda b, j: (b, j, 0, 0))
    return pl.pallas_call(
        _diff_prompt_kernel,
        grid=(BATCH // PROMPT_SEQS, nb),
        in_specs=[qk(0), qk(nb), qk(2 * nb), qk(3 * nb), v_spec, vec(DIFF_QL), vec(DIFF_QL),
                  pl.BlockSpec((4, DIFF_HD), lambda b, j: (0, 0)), vec(DIFF_VL)],
        out_specs=[pl.BlockSpec((rows, DIFF_VL), lambda b, j: (b, j)), kn_spec, vn_spec],
        out_shape=[jax.ShapeDtypeStruct((N_PROMPT, D_MODEL), F32),
                   jax.ShapeDtypeStruct((BATCH, 2, DIFF_HEADS, SEQ, DIFF_HD), F32),
                   jax.ShapeDtypeStruct((BATCH, DIFF_HEADS, SEQ, 2 * DIFF_HD), F32)],
        compiler_params=_params(("arbitrary", "arbitrary")),
        name="diff_prompt",
    )(qkv, qkv, qkv, qkv, qkv, jnp.tile(gq, DIFF_HB).reshape(1, -1),
      jnp.tile(gk, DIFF_HB).reshape(1, -1), lam, jnp.tile(sub_norm, DIFF_HB).reshape(1, -1))


def _diff_latent_kernel(q0_ref, q1_ref, k0_ref, k1_ref, v_ref, kc0_ref, kc1_ref, vc_ref,
                        cos_ref, sin_ref, cosq_ref, sinq_ref, gq_ref, gk_ref, lam_ref, sn_ref,
                        o_ref, kb0_ref, kb1_ref, vb_ref, kmax_ref):
    scale = DIFF_HD ** -0.5 * LOG2E
    half = DIFF_HD // 4
    rows = 256

    @pl.when(pl.program_id(2) == 0)
    def _():
        for comp, (k_ref, kc_ref, kb_ref) in enumerate(
                ((k0_ref, kc0_ref, kb0_ref), (k1_ref, kc1_ref, kb1_ref))):
            kc = kc_ref[0]
            kb_ref[0:PAST_LEN, :] = kc.astype(BF16)
            kmax_ref[comp:comp + 1, :] = jnp.maximum(
                _max_head_norms(kc, DIFF_HD), _rms_norm_bound(gk_ref, DIFF_HD, DIFF_HD))

            def prep(r, carry):
                sl = pl.ds(pl.multiple_of(r * rows, rows), rows)
                k = _group_rms(k_ref[sl, :], gk_ref[...], DIFF_HD)
                k = _rope(k, cos_ref[sl, :], sin_ref[sl, :], half)
                kb_ref[pl.ds(pl.multiple_of(PAST_LEN + r * rows, rows), rows), :] = k.astype(BF16)
                return carry

            lax.fori_loop(0, DEC_SEQ // rows, prep, 0)
        vb_ref[0:PAST_LEN, :] = vc_ref[0].astype(BF16)
        vb_ref[PAST_LEN:, :] = v_ref[...].astype(BF16)

    lam = _diff_lambda(lam_ref)

    def chains(bound):
        for r0 in range(0, DIFF_TQ, DIFF_SUB):
            sl = slice(r0, r0 + DIFF_SUB)
            ps, invs = [], []
            for comp, (q_ref, kb_ref) in enumerate(((q0_ref, kb0_ref), (q1_ref, kb1_ref))):
                shift = None if bound is None else _stacked_rows(
                    bound[:, comp * DIFF_HB:(comp + 1) * DIFF_HB], DIFF_SUB)
                q = _group_rms(q_ref[sl, :], gq_ref[...], DIFF_HD)
                q = _rope(q, cosq_ref[sl, :], sinq_ref[sl, :], half) * scale
                q2 = _stack_heads(q, DIFF_HB, DIFF_HD).astype(BF16)
                s = lax.dot_general(q2, kb_ref[...], _NT, preferred_element_type=F32)
                (p,), inv = _softmax_parts([s], shift)
                ps.append(p)
                invs.append(inv)
            o_ref[sl, :] = _diff_finish(ps, invs, lam, vb_ref[...], sn_ref)

    qmax = scale * _rms_norm_bound(gq_ref, DIFF_HD, DIFF_HD)
    _with_score_bound(jnp.concatenate([qmax * kmax_ref[0:1, :], qmax * kmax_ref[1:2, :]], axis=-1),
                      chains)


def _diff_latent(qkv, cache_k, cache_v, cos, sin, gq, gk, lam, sub_norm):
    nb = DIFF_HEADS // DIFF_HB
    nq = DEC_SEQ // DIFF_TQ
    q0 = N_PROMPT // DIFF_TQ
    lat0 = N_PROMPT // DEC_SEQ
    n_keys = PAST_LEN + DEC_SEQ
    q_spec = lambda off: pl.BlockSpec((DIFF_TQ, DIFF_QL), lambda b, j, t: (q0 + b * nq + t, off + j))
    k_spec = lambda off: pl.BlockSpec((DEC_SEQ, DIFF_QL), lambda b, j, t: (lat0 + b, off + j))
    v_spec = pl.BlockSpec((DEC_SEQ, DIFF_VL), lambda b, j, t: (lat0 + b, 2 * D_MODEL // DIFF_VL + j))
    kc_spec = lambda off: pl.BlockSpec((1, PAST_LEN, DIFF_QL), lambda b, j, t: (b, 0, off + j))
    vc_spec = pl.BlockSpec((1, PAST_LEN, DIFF_VL), lambda b, j, t: (b, 0, j))
    tab = pl.BlockSpec((DEC_SEQ, DIFF_QL), lambda b, j, t: (0, 0))
    tabq = pl.BlockSpec((DIFF_TQ, DIFF_QL), lambda b, j, t: (t, 0))
    vec = lambda n: pl.BlockSpec((1, n), lambda b, j, t: (0, 0))
    return pl.pallas_call(
        _diff_latent_kernel,
        grid=(DEC_BATCH, nb, nq),
        in_specs=[q_spec(0), q_spec(nb), k_spec(2 * nb), k_spec(3 * nb), v_spec,
                  kc_spec(0), kc_spec(nb), vc_spec, tab, tab, tabq, tabq,
                  vec(DIFF_QL), vec(DIFF_QL),
                  pl.BlockSpec((4, DIFF_HD), lambda b, j, t: (0, 0)), vec(DIFF_VL)],
        out_specs=pl.BlockSpec((DIFF_TQ, DIFF_VL), lambda b, j, t: (b * nq + t, j)),
        out_shape=jax.ShapeDtypeStruct((N_LATENT, D_MODEL), F32),
        scratch_shapes=[pltpu.VMEM((n_keys, DIFF_QL), BF16),
                        pltpu.VMEM((n_keys, DIFF_QL), BF16),
                        pltpu.VMEM((n_keys, DIFF_VL), BF16),
                        pltpu.VMEM((2, DIFF_HB), F32)],
        compiler_params=_params(("arbitrary", "arbitrary", "arbitrary")),
        name="diff_latent",
    )(qkv, qkv, qkv, qkv, qkv, _tokens_first(cache_k), _tokens_first(cache_k),
      _tokens_first(cache_v), cos, sin, cos, sin,
      jnp.tile(gq, DIFF_HB).reshape(1, -1), jnp.tile(gk, DIFF_HB).reshape(1, -1), lam,
      jnp.tile(sub_norm, DIFF_HB).reshape(1, -1))


MLA_HB = 2
MLA_HL = 128
MLA_LANES = MLA_HB * MLA_HL
MLA_TQ = 512
MLA_SUB = 128


def _mla_keys(kv, kr, gk, sums_on_mxu=False):
    lane = lax.broadcasted_iota(jnp.int32, kv.shape, 1)
    kr2 = jnp.concatenate([kr] * MLA_HB, axis=1)
    k = jnp.where((lane & (MLA_HL - 1)) < MLA_NOPE, kv, kr2)
    return _group_rms(k, gk, MLA_HL, n_real=MLA_QK, sums_on_mxu=sums_on_mxu)


def _mla_out(o2):
    tq = o2.shape[0] // MLA_HB
    oa = pltpu.roll(o2[0:tq, 0:MLA_HL], MLA_HL - MLA_V, axis=1)
    ob = o2[tq:, MLA_HL:]
    lane = lax.broadcasted_iota(jnp.int32, oa.shape, 1)
    return jnp.where(lane < MLA_V, oa, ob)


def _mla_prompt_kernel(q_ref, kv_ref, kr_ref, gq_ref, gk_ref, o_ref):
    scale = MLA_QK ** -0.5 * LOG2E

    def chains(bound):
        shift = None if bound is None else _stacked_rows(bound, SEQ)
        for seq in range(PROMPT_SEQS):
            sl = slice(seq * SEQ, (seq + 1) * SEQ)
            q = _group_rms(q_ref[sl, :], gq_ref[...], MLA_HL, n_real=MLA_QK) * scale
            kv = kv_ref[sl, :]
            k = _mla_keys(kv, kr_ref[sl, :], gk_ref[...])
            q2 = _stack_heads(q, MLA_HB, MLA_HL).astype(BF16)
            s = lax.dot_general(q2, k.astype(BF16), _NT, preferred_element_type=F32)
            (p,), inv = _softmax_parts([s], shift)
            o_ref[sl, :] = _mla_out(_bdot(p.astype(BF16), kv.astype(BF16)) * inv)

    _with_score_bound(scale * _rms_norm_bound(gq_ref, MLA_QK, MLA_HL)
                      * _rms_norm_bound(gk_ref, MLA_QK, MLA_HL), chains)


def _mla_prompt(qp, kvp, low, gq, gk):
    nb = MLA_HEADS // MLA_HB
    rows = PROMPT_SEQS * SEQ
    blk = pl.BlockSpec((rows, MLA_LANES), lambda b, j: (b, j))
    vec = pl.BlockSpec((1, MLA_LANES), lambda b, j: (0, 0))
    return pl.pallas_call(
        _mla_prompt_kernel,
        grid=(BATCH // PROMPT_SEQS, nb),
        in_specs=[blk, blk, pl.BlockSpec((rows, MLA_HL), lambda b, j: (b, MLA_LOW_KR // MLA_HL)),
                  vec, vec],
        out_specs=pl.BlockSpec((rows, MLA_HB * MLA_V), lambda b, j: (b, j)),
        out_shape=jax.ShapeDtypeStruct((N_PROMPT, MLA_HEADS * MLA_V), F32),
        compiler_params=_params(("arbitrary", "arbitrary")),
        name="mla_prompt",
    )(qp, kvp, low, gq, gk)


def _mla_latent_kernel(q_ref, kv_ref, kr_ref, kvc_ref, krc_ref, cos_ref, sin_ref, cosq_ref,
                       sinq_ref, gq_ref, gk_ref, o_ref, kb_ref, vb_ref, kmax_ref):
    scale = MLA_QK ** -0.5 * LOG2E
    half = MLA_ROPE // 4
    rows = 256

    @pl.when(pl.program_id(2) == 0)
    def _():
        kvc = kvc_ref[...]
        kc = _mla_keys(kvc, krc_ref[...], gk_ref[...])
        kb_ref[0:PAST_LEN, :] = kc.astype(BF16)
        vb_ref[0:PAST_LEN, :] = kvc.astype(BF16)

        def prep(r, carry):
            sl = pl.ds(pl.multiple_of(r * rows, rows), rows)
            dst = pl.ds(pl.multiple_of(PAST_LEN + r * rows, rows), rows)
            kv = kv_ref[sl, :]
            k = _mla_keys(kv, kr_ref[sl, :], gk_ref[...], sums_on_mxu=True)
            k = _rope(k, cos_ref[sl, :], sin_ref[sl, :], half)
            kb_ref[dst, :] = k.astype(BF16)
            vb_ref[dst, :] = kv.astype(BF16)
            return carry

        lax.fori_loop(0, DEC_SEQ // rows, prep, 0)
        kmax_ref[...] = jnp.maximum(_max_head_norms(kc, MLA_HL),
                                    _rms_norm_bound(gk_ref, MLA_QK, MLA_HL))

    def chains(bound):
        shift = None if bound is None else _stacked_rows(bound, MLA_SUB)
        for r0 in range(0, MLA_TQ, MLA_SUB):
            sl = slice(r0, r0 + MLA_SUB)
            q = _group_rms(q_ref[sl, :], gq_ref[...], MLA_HL, n_real=MLA_QK)
            q = _rope(q, cosq_ref[sl, :], sinq_ref[sl, :], half) * scale
            q2 = _stack_heads(q, MLA_HB, MLA_HL).astype(BF16)
            s = lax.dot_general(q2, kb_ref[...], _NT, preferred_element_type=F32)
            (p,), inv = _softmax_parts([s], shift)
            o_ref[sl, :] = _mla_out(_bdot(p.astype(BF16), vb_ref[...]) * inv)

    _with_score_bound(scale * _rms_norm_bound(gq_ref, MLA_QK, MLA_HL) * kmax_ref[...], chains)


def _mla_latent(qp, kvp, low, kvc, krc, cos, sin, gq, gk):
    nb = MLA_HEADS // MLA_HB
    nq = DEC_SEQ // MLA_TQ
    q0 = N_PROMPT // MLA_TQ
    lat0 = N_PROMPT // DEC_SEQ
    n_keys = PAST_LEN + DEC_SEQ
    tab = pl.BlockSpec((DEC_SEQ, MLA_LANES), lambda b, j, t: (0, 0))
    tabq = pl.BlockSpec((MLA_TQ, MLA_LANES), lambda b, j, t: (t, 0))
    vec = pl.BlockSpec((1, MLA_LANES), lambda b, j, t: (0, 0))
    return pl.pallas_call(
        _mla_latent_kernel,
        grid=(DEC_BATCH, nb, nq),
        in_specs=[pl.BlockSpec((MLA_TQ, MLA_LANES), lambda b, j, t: (q0 + b * nq + t, j)),
                  pl.BlockSpec((DEC_SEQ, MLA_LANES), lambda b, j, t: (lat0 + b, j)),
                  pl.BlockSpec((DEC_SEQ, MLA_HL), lambda b, j, t: (lat0 + b, MLA_LOW_KR // MLA_HL)),
                  pl.BlockSpec((PAST_LEN, MLA_LANES), lambda b, j, t: (b, j)),
                  pl.BlockSpec((PAST_LEN, MLA_HL), lambda b, j, t: (b, 0)),
                  tab, tab, tabq, tabq, vec, vec],
        out_specs=pl.BlockSpec((MLA_TQ, MLA_HB * MLA_V), lambda b, j, t: (b * nq + t, j)),
        out_shape=jax.ShapeDtypeStruct((N_LATENT, MLA_HEADS * MLA_V), F32),
        scratch_shapes=[pltpu.VMEM((n_keys, MLA_LANES), BF16),
                        pltpu.VMEM((n_keys, MLA_LANES), BF16),
                        pltpu.VMEM((1, MLA_HB), F32)],
        compiler_params=_params(("arbitrary", "arbitrary", "arbitrary")),
        name="mla_latent",
    )(qp, kvp, low, kvc, krc, cos, sin, cos, sin, gq, gk)


MLA_LOW_Q = 0
MLA_LOW_KV = 512
MLA_LOW_KR = 768
MLA_LOW_N = 896


def _axial_tables(n_tok, rdim):
    nf = rdim // 4
    freqs = ROPE_BASE ** (-jnp.arange(nf, dtype=F32) / nf)
    t = jnp.arange(n_tok)
    rowp = (t // GRID_W).astype(F32)
    colp = (t % GRID_W).astype(F32)
    ang = jnp.stack([rowp[:, None] * freqs, colp[:, None] * freqs], axis=1)
    cos, sin = jnp.cos(ang), jnp.sin(ang)
    cos_l = jnp.stack([cos, cos], axis=2).reshape(n_tok, rdim)
    sin_l = jnp.stack([-sin, sin], axis=2).reshape(n_tok, rdim)
    return cos_l, sin_l


def _diff_rope_tables():
    cos, sin = _axial_tables(DEC_SEQ, DIFF_HD)
    return jnp.tile(cos, (1, DIFF_HB)), jnp.tile(sin, (1, DIFF_HB))


def _mla_rope_tables():
    cos, sin = _axial_tables(DEC_SEQ, MLA_ROPE)
    ones = jnp.ones((DEC_SEQ, MLA_NOPE), F32)
    pad1 = jnp.ones((DEC_SEQ, MLA_HL - MLA_QK), F32)
    cos_h = jnp.concatenate([ones, cos, pad1], axis=1)
    sin_h = jnp.concatenate([0 * ones, sin, 0 * pad1], axis=1)
    return jnp.tile(cos_h, (1, MLA_HB)), jnp.tile(sin_h, (1, MLA_HB))


def _tokens_first(cache):
    b, h, l, d = cache.shape
    return jnp.transpose(cache, (0, 2, 1, 3)).reshape(b, l, h * d)


def _pad_heads(w, heads, hd, hl):
    k = w.shape[0]
    return jnp.pad(w.reshape(k, heads, hd), ((0, 0), (0, 0), (0, hl - hd))).reshape(k, heads * hl)


def kernel(x_prompt, x_sample, cache_l0_k, cache_l0_v, state_l1_fwd, state_l1_bwd, cache_l2_k,
           cache_l2_v, cache_l3_ckv, cache_l3_krope, c, c_ctx, ada_w, ada_b, norm_mix, norm_ffn,
           ffn_w_up, ffn_conv_w, ffn_conv_b, ffn_w_down, na_w_qkv, na_q_norm, na_k_norm, na_bias,
           na_w_o, gla_w_qkvg, gla_w_gate1, gla_w_gate2, gla_b_gate, gla_o_norm, gla_w_o,
           diff_w_qkv, diff_q_norm, diff_k_norm, diff_lambda, diff_sub_norm, diff_w_o, mla_w_dq,
           mla_q_a_norm, mla_w_uq, mla_w_dkv, mla_kv_a_norm, mla_w_ukv, mla_q_norm, mla_k_norm,
           mla_w_o):
    xr = _Rows(x_prompt.reshape(N_PROMPT, D_MODEL), x_sample.reshape(N_LATENT, D_MODEL), 0)
    cvecs = jnp.concatenate([c_ctx[None], c, jnp.zeros((5, D_MODEL), F32)], axis=0)
    mods_all = _ada_mods(cvecs, ada_w, ada_b)
    halves = lambda o_p, o_s: _Rows(o_p, o_s, 0)

    mods = mods_all[0]
    qkv = _norm_mod_proj(xr, norm_mix[0], mods, [na_w_qkv], "na_qkv")
    o_p, new_l0_k, new_l0_v = _na_prompt(qkv, na_q_norm, na_k_norm)
    o_s = _na_latent(qkv, cache_l0_k, cache_l0_v, _na_bias_blocks(na_bias), na_q_norm, na_k_norm)
    x = _out_proj_residual(xr, halves(o_p, o_s), mods, na_w_o, "na_out")
    ffn_weights = (ffn_w_up, ffn_conv_w, ffn_conv_b, ffn_w_down)
    x = _ffn(x, norm_ffn[0], mods, 0, *ffn_weights)
    xr = _one_array(x)

    mods = mods_all[1]
    w_decay = jnp.concatenate(
        [gla_w_gate1[0], gla_w_gate1[1],
         jnp.zeros((D_MODEL, 128 - 2 * GLA_GATE_RANK), F32)], axis=1)
    proj = _norm_mod_proj(xr, norm_mix[1], mods, [gla_w_qkvg, w_decay], "gla_proj")
    w2 = jnp.zeros((2, 128, GLA_HK), F32)
    w2 = w2.at[0, :GLA_GATE_RANK].set(gla_w_gate2[0])
    w2 = w2.at[1, GLA_GATE_RANK:2 * GLA_GATE_RANK].set(gla_w_gate2[1])
    bg = gla_b_gate.reshape(2, 1, GLA_HK)
    o_p, new_l1_fwd, new_l1_bwd = _gla(proj, w2, bg, gla_o_norm, BATCH, SEQ, 0)
    o_s = _gla(proj, w2, bg, gla_o_norm, DEC_BATCH, DEC_SEQ, N_PROMPT // DEC_SEQ,
               states=(state_l1_fwd, state_l1_bwd))
    x = _out_proj_residual(xr, halves(o_p, o_s), mods, gla_w_o, "gla_out",
                           gate=proj, gate_col_block=(2 * GLA_HK + GLA_HV) // GLA_HV)
    x = _ffn(x, norm_ffn[1], mods, 1, *ffn_weights)
    xr = _one_array(x)

    mods = mods_all[2]
    qkv = _norm_mod_proj(xr, norm_mix[2], mods, [diff_w_qkv], "diff_qkv")
    o_p, kn_p, new_l2_v = _diff_prompt(qkv, diff_q_norm, diff_k_norm, diff_lambda, diff_sub_norm)
    new_l2_k = kn_p.reshape(BATCH, 2 * DIFF_HEADS, SEQ, DIFF_HD)
    cos_d, sin_d = _diff_rope_tables()
    o_s = _diff_latent(qkv, cache_l2_k, cache_l2_v, cos_d, sin_d, diff_q_norm, diff_k_norm,
                       diff_lambda, diff_sub_norm)
    x = _out_proj_residual(xr, halves(o_p, o_s), mods, diff_w_o, "diff_out")
    x = _ffn(x, norm_ffn[2], mods, 2, *ffn_weights)
    xr = _one_array(x)

    mods = mods_all[3]
    zc = lambda n: jnp.zeros((D_MODEL, n), F32)
    w_low = jnp.concatenate(
        [mla_w_dq, zc(MLA_LOW_KV - MLA_Q_RANK), mla_w_dkv[:, :MLA_KV_RANK],
         zc(MLA_NOPE), mla_w_dkv[:, MLA_KV_RANK:], zc(MLA_HL - MLA_QK)], axis=1)
    low = _norm_mod_proj(xr, norm_mix[3], mods, [w_low], "mla_down")
    w_uq = _pad_heads(mla_w_uq, MLA_HEADS, MLA_QK, MLA_HL)
    qp, _ = _rms_matmul(low, 0, MLA_Q_RANK, mla_q_a_norm, w_uq, True, "mla_uq")
    kvp, ckv = _rms_matmul(low, MLA_LOW_KV // MLA_KV_RANK, MLA_KV_RANK, mla_kv_a_norm, mla_w_ukv,
                           True, "mla_ukv")
    kvc, _ = _rms_matmul(cache_l3_ckv.reshape(DEC_BATCH * PAST_LEN, MLA_KV_RANK), 0, MLA_KV_RANK,
                         mla_kv_a_norm, mla_w_ukv, False, "mla_ukv_cache")
    krc = jnp.pad(cache_l3_krope.reshape(DEC_BATCH * PAST_LEN, MLA_ROPE),
                  ((0, 0), (MLA_NOPE, MLA_HL - MLA_QK)))
    pad_gain = lambda g: jnp.tile(jnp.pad(g, (0, MLA_HL - MLA_QK)), MLA_HB).reshape(1, -1)
    gq, gk = pad_gain(mla_q_norm), pad_gain(mla_k_norm)
    o_p = _mla_prompt(qp, kvp, low, gq, gk)
    cos_m, sin_m = _mla_rope_tables()
    o_s = _mla_latent(qp, kvp, low, kvc, krc, cos_m, sin_m, gq, gk)
    new_l3_ckv = ckv[:N_PROMPT].reshape(BATCH, SEQ, MLA_KV_RANK)
    new_l3_krope = low[:N_PROMPT, MLA_LOW_KR + MLA_NOPE:MLA_LOW_KR + MLA_QK].reshape(
        BATCH, SEQ, MLA_ROPE)
    x = _out_proj_residual(xr, halves(o_p, o_s), mods, mla_w_o, "mla_out")
    n_pt = N_PROMPT // TOK_TILE
    ffn3 = functools.partial(_ffn, x, norm_ffn[3], mods, 3, *ffn_weights)
    y_prompt = ffn3(tile0=0, n_tiles=n_pt).reshape(BATCH, SEQ, D_MODEL)
    y_sample = ffn3(tile0=n_pt, n_tiles=N_TOK_TILES - n_pt).reshape(DEC_BATCH, DEC_SEQ, D_MODEL)
    return (y_prompt, y_sample, new_l0_k, new_l0_v, new_l1_fwd, new_l1_bwd, new_l2_k, new_l2_v,
            new_l3_ckv, new_l3_krope)
```

```python
import functools
import math
from typing import NamedTuple

import jax
import jax.numpy as jnp
from jax import lax
from jax.experimental import pallas as pl
from jax.experimental.pallas import tpu as pltpu

F32 = jnp.float32
BF16 = jnp.bfloat16

D_MODEL = 1024
BATCH = 16
SEQ = 256
DEPTH = 4
DEC_BATCH = 2
DEC_SEQ = 2048
PAST_LEN = 256
GRID_W = 64
D_FF = 2816
EPS = 1e-6
ROPE_BASE = 10000.0

NA_HEADS = 16
NA_HD = 64
NA_WIN_R = 8
NA_WIN_C = 16

GLA_HEADS = 4
GLA_DK = 128
GLA_DV = 256
GLA_HK = GLA_HEADS * GLA_DK
GLA_HV = GLA_HEADS * GLA_DV
GLA_GATE_RANK = 16
GLA_GATE_NORM = 16.0

DIFF_HEADS = 8
DIFF_HD = 64
DIFF_LAMBDA_INIT = 0.8 - 0.6 * math.exp(-0.3 * 2)

MLA_HEADS = 16
MLA_Q_RANK = 384
MLA_KV_RANK = 256
MLA_NOPE = 64
MLA_ROPE = 32
MLA_V = 64
MLA_QK = MLA_NOPE + MLA_ROPE

N_PROMPT = BATCH * SEQ
N_LATENT = DEC_BATCH * DEC_SEQ
N_TOK = N_PROMPT + N_LATENT
TOK_TILE = 2048
N_TOK_TILES = N_TOK // TOK_TILE
FF_CHUNK = 256
N_FF_CHUNKS = D_FF // FF_CHUNK
NEG = -1e30
LOG2E = math.log2(math.e)
SAFE_SCORE_BOUND = 48.0

VMEM_LIMIT = 56 * 1024 * 1024

_NT = (((1,), (1,)), ((), ()))
_TN = (((0,), (0,)), ((), ()))


def _params(sem, vmem=VMEM_LIMIT):
    return pltpu.CompilerParams(dimension_semantics=sem, vmem_limit_bytes=vmem)


def _log2(n):
    assert n & (n - 1) == 0
    return n.bit_length() - 1


def _silu(x):
    return x / (1.0 + jnp.exp(-x))


def _bdot(a, b):
    return jnp.dot(a, b, preferred_element_type=F32)


def _softmax_parts(parts, shift=None):
    m = shift
    if m is None:
        m = parts[0].max(axis=-1, keepdims=True)
        for s in parts[1:]:
            m = jnp.maximum(m, s.max(axis=-1, keepdims=True))
    ps = [jnp.exp2(s - m) for s in parts]
    l = ps[0].sum(axis=-1, keepdims=True)
    for p in ps[1:]:
        l = l + p.sum(axis=-1, keepdims=True)
    return ps, 1.0 / l


def _group_rms(x, gain, group, n_real=None, sums_on_mxu=False):
    lanes = x.shape[-1]
    n_real = n_real or group
    x2 = x * x
    if group == lanes:
        ms = jnp.sum(x2, axis=-1, keepdims=True)
    elif not sums_on_mxu:
        gid = lax.broadcasted_iota(jnp.int32, x.shape, 1) >> _log2(group)
        ms = jnp.zeros_like(x)
        for i in range(lanes // group):
            sel = gid == i
            si = jnp.sum(jnp.where(sel, x2, 0.0), axis=-1, keepdims=True)
            ms = jnp.where(sel, si, ms)
    else:
        r = lax.broadcasted_iota(jnp.int32, (lanes, lanes), 0) >> _log2(group)
        c = lax.broadcasted_iota(jnp.int32, (lanes, lanes), 1) >> _log2(group)
        ones = jnp.where(r == c, 1.0, 0.0).astype(BF16)
        hi = x2.astype(BF16)
        lo = (x2 - hi.astype(F32)).astype(BF16)
        ms = _bdot(hi, ones) + _bdot(lo, ones)
    return x * lax.rsqrt(ms * (1.0 / n_real) + EPS) * gain


def _with_score_bound(bound, body):
    ok = bound.max() < SAFE_SCORE_BOUND
    pl.when(ok)(lambda: body(bound))
    pl.when(jnp.logical_not(ok))(lambda: body(None))


def _rms_norm_bound(g_ref, n, head_lanes):
    g = jnp.abs(g_ref[...])
    heads = g.shape[-1] // head_lanes
    return n ** 0.5 * jnp.concatenate(
        [g[:, h * head_lanes:(h + 1) * head_lanes].max(axis=-1, keepdims=True)
         for h in range(heads)], axis=-1)


def _stacked_rows(per_head, rows):
    r = lax.broadcasted_iota(jnp.int32, (per_head.shape[-1] * rows, 1), 0)
    out = per_head[:, 0:1]
    for h in range(1, per_head.shape[-1]):
        out = jnp.where(r >= h * rows, per_head[:, h:h + 1], out)
    return out


def _max_head_norms(x, head_lanes):
    hid = lax.broadcasted_iota(jnp.int32, x.shape, 1) >> _log2(head_lanes)
    x2 = x * x
    sq = [jnp.sum(jnp.where(hid == h, x2, 0.0), axis=-1, keepdims=True).max(axis=0, keepdims=True)
          for h in range(x.shape[-1] // head_lanes)]
    return jnp.sqrt(jnp.concatenate(sq, axis=-1))


def _rope(x, cos, sin, half):
    lanes = x.shape[-1]
    lane = lax.broadcasted_iota(jnp.int32, x.shape, 1)
    up = pltpu.roll(x, lanes - half, axis=1)
    dn = pltpu.roll(x, half, axis=1)
    swapped = jnp.where((lane & (2 * half - 1)) < half, up, dn)
    return x * cos + swapped * sin


def _stack_heads(q, n_heads, head_lanes):
    hid = lax.broadcasted_iota(jnp.int32, q.shape, 1) >> _log2(head_lanes)
    zero = jnp.zeros_like(q)
    return jnp.concatenate([jnp.where(hid == i, q, zero) for i in range(n_heads)], axis=0)


def _unstack_heads(o, n_heads, head_lanes):
    rows = o.shape[0] // n_heads
    hid = lax.broadcasted_iota(jnp.int32, (rows, o.shape[1]), 1) >> _log2(head_lanes)
    out = o[0:rows]
    for i in range(1, n_heads):
        out = jnp.where(hid == i, o[i * rows:(i + 1) * rows], out)
    return out


ADA_TN = 1536


def _ada_kernel(c_ref, w_ref, b_ref, o_ref):
    s = _silu(c_ref[...])
    w = w_ref[0]
    s_hi, w_hi = s.astype(BF16), w.astype(BF16)
    s_lo = (s - s_hi.astype(F32)).astype(BF16)
    w_lo = (w - w_hi.astype(F32)).astype(BF16)
    both = _bdot(jnp.concatenate([s_hi, s_lo], axis=0), w_hi)
    o_ref[0] = both[0:8] + both[8:16] + _bdot(s_hi, w_lo) + b_ref[0]


def _ada_mods(cvecs, ada_w, ada_b):
    out = pl.pallas_call(
        _ada_kernel,
        grid=(DEPTH, 6 * D_MODEL // ADA_TN),
        in_specs=[pl.BlockSpec((8, D_MODEL), lambda l, j: (0, 0)),
                  pl.BlockSpec((1, D_MODEL, ADA_TN), lambda l, j: (l, 0, j)),
                  pl.BlockSpec((1, 1, ADA_TN), lambda l, j: (l, 0, j))],
        out_specs=pl.BlockSpec((1, 8, ADA_TN), lambda l, j: (l, 0, j)),
        out_shape=jax.ShapeDtypeStruct((DEPTH, 8, 6 * D_MODEL), F32),
        compiler_params=_params(("arbitrary", "arbitrary")),
        name="ada_mod",
    )(cvecs, ada_w, ada_b.reshape(DEPTH, 1, 6 * D_MODEL))
    return out.reshape(DEPTH, 8, 6, D_MODEL)[:, :3]


def _mod_group_of_tile(i):
    return jnp.maximum(i - (N_PROMPT // TOK_TILE - 1), 0)


def _norm_mod_rows(x_ref, g_ref, mod_ref, h_ref, shift_idx, scale_idx, rows=64):
    g = g_ref[...]
    sc = 1.0 + mod_ref[0, scale_idx:scale_idx + 1, :]
    sh = mod_ref[0, shift_idx:shift_idx + 1, :]

    def body(r, carry):
        sl = pl.ds(pl.multiple_of(r * rows, rows), rows)
        xf = x_ref[sl, :]
        ms = jnp.mean(xf * xf, axis=-1, keepdims=True)
        y = xf * lax.rsqrt(ms + EPS) * g
        h_ref[sl, :] = (y * sc + sh).astype(BF16)
        return carry

    lax.fori_loop(0, x_ref.shape[0] // rows, body, 0)


ROW_TM = 512


class _Rows(NamedTuple):
    prompt: jax.Array
    latent: jax.Array
    latent_row0: int


def _one_array(x):
    return _Rows(x, x, N_PROMPT)


def _row_specs(rows, width, col_block=0):
    n_p = N_PROMPT // ROW_TM
    l0 = rows.latent_row0 // ROW_TM
    return [pl.BlockSpec((ROW_TM, width), lambda t: (jnp.minimum(t, n_p - 1), col_block)),
            pl.BlockSpec((ROW_TM, width), lambda t: (l0 + jnp.maximum(t - n_p, 0), col_block))]


def _row_group(t):
    first_latent = N_PROMPT // ROW_TM
    return jnp.where(t < first_latent, 0, 1 + (t - first_latent) // (DEC_SEQ // ROW_TM))


def _is_prompt_tile():
    return pl.program_id(0) < N_PROMPT // ROW_TM


def _proj_kernel(xp_ref, xl_ref, g_ref, mod_ref, *refs):
    *w_refs, o_ref, h_ref, wb_ref = refs

    @pl.when(pl.program_id(0) == 0)
    def _():
        off = 0
        for w_ref in w_refs:
            wb_ref[:, off:off + w_ref.shape[1]] = w_ref[...].astype(BF16)
            off += w_ref.shape[1]

    is_prompt = _is_prompt_tile()
    g = g_ref[...]
    sc = 1.0 + mod_ref[0, 1:2, :]
    sh = mod_ref[0, 0:1, :]
    rows = 64
    part = ROW_TM // 2
    for p0 in range(0, ROW_TM, part):
        for r0 in range(p0, p0 + part, rows):
            sl = slice(r0, r0 + rows)
            xf = jnp.where(is_prompt, xp_ref[sl, :], xl_ref[sl, :])
            ms = jnp.mean(xf * xf, axis=-1, keepdims=True)
            y = xf * lax.rsqrt(ms + EPS) * g
            h_ref[sl, :] = (y * sc + sh).astype(BF16)
        o_ref[p0:p0 + part, :] = _bdot(h_ref[p0:p0 + part, :], wb_ref[...])


def _norm_mod_proj(x, g, mods, ws, name):
    n = sum(w.shape[1] for w in ws)
    return pl.pallas_call(
        _proj_kernel,
        grid=(N_TOK // ROW_TM,),
        in_specs=_row_specs(x, D_MODEL) + [
            pl.BlockSpec((1, D_MODEL), lambda t: (0, 0)),
            pl.BlockSpec((1, 6, D_MODEL), lambda t: (_row_group(t), 0, 0))] + [
            pl.BlockSpec(w.shape, lambda t: (0, 0), pipeline_mode=pl.Buffered(1)) for w in ws],
        out_specs=pl.BlockSpec((ROW_TM, n), lambda t: (t, 0)),
        out_shape=jax.ShapeDtypeStruct((N_TOK, n), F32),
        scratch_shapes=[pltpu.VMEM((ROW_TM, D_MODEL), BF16),
                        pltpu.VMEM((D_MODEL, n), BF16)],
        compiler_params=_params(("arbitrary",)),
        name=name,
    )(x.prompt, x.latent, g.reshape(1, D_MODEL), mods, *ws)


def _rms_matmul_kernel(a_ref, g_ref, w_ref, o_ref, n_ref, *, normalise):
    a = a_ref[...]
    if normalise:
        a = a * lax.rsqrt(jnp.mean(a * a, axis=-1, keepdims=True) + EPS) * g_ref[...]
    n_ref[...] = a
    o_ref[...] = _bdot(a.astype(BF16), w_ref[...].astype(BF16))


def _rms_matmul(a, col_block, k, g, w, normalise, name, tm=512):
    rows, n = a.shape[0], w.shape[1]
    return pl.pallas_call(
        functools.partial(_rms_matmul_kernel, normalise=normalise),
        grid=(rows // tm,),
        in_specs=[pl.BlockSpec((tm, k), lambda i: (i, col_block)),
                  pl.BlockSpec((1, k), lambda i: (0, 0)),
                  pl.BlockSpec((k, n), lambda i: (0, 0))],
        out_specs=[pl.BlockSpec((tm, n), lambda i: (i, 0)),
                   pl.BlockSpec((tm, k), lambda i: (i, 0))],
        out_shape=[jax.ShapeDtypeStruct((rows, n), F32),
                   jax.ShapeDtypeStruct((rows, k), F32)],
        compiler_params=_params(("arbitrary",)),
        name=name,
    )(a, g.reshape(1, k), w)


def _oproj_kernel(*refs, gated):
    if gated:
        xp_ref, xl_ref, ap_ref, al_ref, g_ref, mod_ref, w_ref, o_ref, wb_ref = refs
    else:
        xp_ref, xl_ref, ap_ref, al_ref, mod_ref, w_ref, o_ref, wb_ref = refs

    @pl.when(pl.program_id(0) == 0)
    def _():
        wb_ref[...] = w_ref[...].astype(BF16)

    is_prompt = _is_prompt_tile()
    a = jnp.where(is_prompt, ap_ref[...], al_ref[...])
    if gated:
        a = a * _silu(g_ref[...])
    y = _bdot(a.astype(BF16), wb_ref[...])
    x = jnp.where(is_prompt, xp_ref[...], xl_ref[...])
    o_ref[...] = x + mod_ref[0, 2:3, :] * y


def _out_proj_residual(x, a, mods, w, name, gate=None, gate_col_block=0):
    k = w.shape[0]
    in_specs = _row_specs(x, D_MODEL) + _row_specs(a, k)
    args = [x.prompt, x.latent, a.prompt, a.latent]
    if gate is not None:
        in_specs.append(pl.BlockSpec((ROW_TM, k), lambda t: (t, gate_col_block)))
        args.append(gate)
    in_specs += [pl.BlockSpec((1, 6, D_MODEL), lambda t: (_row_group(t), 0, 0)),
                 pl.BlockSpec((k, D_MODEL), lambda t: (0, 0))]
    args += [mods, w]
    return pl.pallas_call(
        functools.partial(_oproj_kernel, gated=gate is not None),
        grid=(N_TOK // ROW_TM,),
        in_specs=in_specs,
        out_specs=pl.BlockSpec((ROW_TM, D_MODEL), lambda t: (t, 0)),
        out_shape=jax.ShapeDtypeStruct((N_TOK, D_MODEL), F32),
        scratch_shapes=[pltpu.VMEM((k, D_MODEL), BF16)],
        compiler_params=_params(("arbitrary",)),
        name=name,
    )(*args)


FFN_MM_ROWS = 512
FFN_ROWS = 64
FFN_PAD = 8


def _ffn_kernel(x_ref, g_ref, mod_ref, wg_ref, wv_ref, cwg_ref, cwv_ref, cbg_ref, cbv_ref,
                wd_ref, o_ref, h_ref, u_ref, act_ref, wup_ref, wdn_ref, *, tile0):
    i = tile0 + pl.program_id(0)
    c = pl.program_id(1)
    fc = FF_CHUNK
    n = TOK_TILE // FFN_MM_ROWS
    seq_len = jnp.where(i < N_PROMPT // TOK_TILE, SEQ, DEC_SEQ)
    row = lax.broadcasted_iota(jnp.int32, (FFN_ROWS, 1), 0)

    def up(u_ref, t):
        r0 = t * FFN_MM_ROWS
        u_ref[FFN_PAD + r0:FFN_PAD + r0 + FFN_MM_ROWS, :] = _bdot(
            h_ref[r0:r0 + FFN_MM_ROWS, :], wup_ref[...])

    def conv_act(u_ref, t):
        for r0 in range(t * FFN_MM_ROWS, (t + 1) * FFN_MM_ROWS, FFN_ROWS):
            halves = []
            for lo, cw_ref, cb_ref in ((0, cwg_ref, cbg_ref), (fc, cwv_ref, cbv_ref)):
                p0 = FFN_PAD + r0
                prev = u_ref[p0 - 1:p0 - 1 + FFN_ROWS, lo:lo + fc]
                mid = u_ref[p0:p0 + FFN_ROWS, lo:lo + fc]
                nxt = u_ref[p0 + 1:p0 + 1 + FFN_ROWS, lo:lo + fc]
                if r0 % SEQ == 0:
                    prev = jnp.where(((r0 + row) & (seq_len - 1)) == 0, 0.0, prev)
                if (r0 + FFN_ROWS) % SEQ == 0:
                    nxt = jnp.where(((r0 + row) & (seq_len - 1)) == seq_len - 1, 0.0, nxt)
                halves.append(prev * cw_ref[0:1, :] + mid * cw_ref[1:2, :] + nxt * cw_ref[2:3, :]
                              + cb_ref[...])
            act_ref[r0:r0 + FFN_ROWS, :] = (_silu(halves[0]) * halves[1]).astype(BF16)

    def down(t):
        r0 = t * FFN_MM_ROWS
        o_ref[r0:r0 + FFN_MM_ROWS, :] += _bdot(act_ref[r0:r0 + FFN_MM_ROWS, :], wdn_ref[...])

    @pl.when(c == 0)
    def _():
        _norm_mod_rows(x_ref, g_ref, mod_ref, h_ref, 3, 4)
        zeros = jnp.zeros((FFN_PAD, 2 * fc), F32)
        u_ref[0:FFN_PAD, :] = zeros
        u_ref[FFN_PAD + TOK_TILE:, :] = zeros
        o_ref[...] = jnp.zeros_like(o_ref)

    wup_ref[:, :fc] = wg_ref[...].astype(BF16)
    wup_ref[:, fc:] = wv_ref[...].astype(BF16)
    wdn_ref[...] = wd_ref[...].astype(BF16)
    for s in range(n + 2):
        if s < n:
            up(u_ref, s)
        if 1 <= s <= n:
            conv_act(u_ref, s - 1)
        if s >= 2:
            down(s - 2)

    @pl.when(c == N_FF_CHUNKS - 1)
    def _():
        o_ref[...] = x_ref[...] + mod_ref[0, 5:6, :] * o_ref[...]


def _ffn(x, g, mods, layer, w_up, conv_w, conv_b, w_down, tile0=0, n_tiles=N_TOK_TILES):
    fc = FF_CHUNK
    ncb = N_FF_CHUNKS
    up_c = cv_c = lambda c: c
    u_buf = pltpu.VMEM((TOK_TILE + 2 * FFN_PAD, 2 * fc), F32)
    return pl.pallas_call(
        functools.partial(_ffn_kernel, tile0=tile0),
        grid=(n_tiles, ncb),
        in_specs=[pl.BlockSpec((TOK_TILE, D_MODEL), lambda i, c: (tile0 + i, 0)),
                  pl.BlockSpec((1, D_MODEL), lambda i, c: (0, 0)),
                  pl.BlockSpec((1, 6, D_MODEL), lambda i, c: (_mod_group_of_tile(tile0 + i), 0, 0)),
                  pl.BlockSpec((None, D_MODEL, fc), lambda i, c: (layer, 0, up_c(c))),
                  pl.BlockSpec((None, D_MODEL, fc), lambda i, c: (layer, 0, ncb + up_c(c))),
                  pl.BlockSpec((None, 3, fc), lambda i, c: (layer, 0, cv_c(c))),
                  pl.BlockSpec((None, 3, fc), lambda i, c: (layer, 0, ncb + cv_c(c))),
                  pl.BlockSpec((None, 1, fc), lambda i, c: (layer, 0, cv_c(c))),
                  pl.BlockSpec((None, 1, fc), lambda i, c: (layer, 0, ncb + cv_c(c))),
                  pl.BlockSpec((None, fc, D_MODEL), lambda i, c: (layer, cv_c(c), 0))],
        out_specs=pl.BlockSpec((TOK_TILE, D_MODEL), lambda i, c: (i, 0)),
        out_shape=jax.ShapeDtypeStruct((n_tiles * TOK_TILE, D_MODEL), F32),
        scratch_shapes=[pltpu.VMEM((TOK_TILE, D_MODEL), BF16),
                        u_buf,
                        pltpu.VMEM((TOK_TILE, fc), BF16),
                        pltpu.VMEM((D_MODEL, 2 * fc), BF16),
                        pltpu.VMEM((fc, D_MODEL), BF16)],
        compiler_params=_params(("arbitrary", "arbitrary")),
        name="conv_ffn",
    )(x, g.reshape(1, D_MODEL), mods, w_up, w_up, conv_w, conv_w,
      conv_b.reshape(DEPTH, 1, -1), conv_b.reshape(DEPTH, 1, -1), w_down)


NA_HB = 4
NA_LANES = NA_HB * NA_HD
NA_ROWS = DEC_SEQ // GRID_W
NA_KEYS = NA_WIN_R * GRID_W
PROMPT_SEQS = 4
NA_ROW_UNROLL = 4


def _store_heads(dst_ref, seq, x, n_heads, hd):
    for h in range(n_heads):
        dst_ref[seq, h] = x[:, h * hd:(h + 1) * hd]


def _na_prompt_kernel(q_ref, k_ref, v_ref, gq_ref, gk_ref, o_ref, kn_ref, vn_ref):
    scale = NA_HD ** -0.5 * LOG2E

    def chains(bound):
        shift = None if bound is None else _stacked_rows(bound, SEQ)
        for seq in range(PROMPT_SEQS):
            sl = slice(seq * SEQ, (seq + 1) * SEQ)
            q = _group_rms(q_ref[sl, :], gq_ref[...], NA_HD, sums_on_mxu=True) * scale
            k = _group_rms(k_ref[sl, :], gk_ref[...], NA_HD, sums_on_mxu=True)
            v = v_ref[sl, :]
            _store_heads(kn_ref, seq, k, NA_HB, NA_HD)
            _store_heads(vn_ref, seq, v, NA_HB, NA_HD)
            q4 = _stack_heads(q, NA_HB, NA_HD).astype(BF16)
            s = lax.dot_general(q4, k.astype(BF16), _NT, preferred_element_type=F32)
            (p,), inv = _softmax_parts([s], shift)
            o4 = _bdot(p.astype(BF16), v.astype(BF16)) * inv
            o_ref[sl, :] = _unstack_heads(o4, NA_HB, NA_HD)

    _with_score_bound(scale * _rms_norm_bound(gq_ref, NA_HD, NA_HD)
                      * _rms_norm_bound(gk_ref, NA_HD, NA_HD), chains)


def _na_prompt(qkv, gq, gk):
    nb = NA_HEADS // NA_HB
    rows = PROMPT_SEQS * SEQ
    blk = lambda off: pl.BlockSpec((rows, NA_LANES), lambda b, j: (b, off + j))
    vec = pl.BlockSpec((1, NA_LANES), lambda b, j: (0, 0))
    cache = pl.BlockSpec((PROMPT_SEQS, NA_HB, SEQ, NA_HD), lambda b, j: (b, j, 0, 0))
    cache_shape = jax.ShapeDtypeStruct((BATCH, NA_HEADS, SEQ, NA_HD), F32)
    return pl.pallas_call(
        _na_prompt_kernel,
        grid=(BATCH // PROMPT_SEQS, nb),
        in_specs=[blk(0), blk(nb), blk(2 * nb), vec, vec],
        out_specs=[blk(0), cache, cache],
        out_shape=[jax.ShapeDtypeStruct((N_PROMPT, D_MODEL), F32), cache_shape, cache_shape],
        compiler_params=_params(("arbitrary", "arbitrary")),
        name="na_prompt",
    )(qkv, qkv, qkv, jnp.tile(gq, NA_HB).reshape(1, -1), jnp.tile(gk, NA_HB).reshape(1, -1))


def _na_latent_kernel(q_ref, k_ref, v_ref, kc_ref, vc_ref, t_ref, gq_ref, gk_ref, o_ref,
                      qn_ref, kn_ref, vb_ref, kc4_ref, vc4_ref, bias_ref):
    scale = NA_HD ** -0.5 * LOG2E
    rows = 256

    def prep(r, carry):
        sl = pl.ds(pl.multiple_of(r * rows, rows), rows)
        qn_ref[sl, :] = (_group_rms(q_ref[sl, :], gq_ref[...], NA_HD, sums_on_mxu=True)
                         * scale).astype(BF16)
        kn_ref[sl, :] = _group_rms(k_ref[sl, :], gk_ref[...], NA_HD,
                                   sums_on_mxu=True).astype(BF16)
        vb_ref[sl, :] = v_ref[sl, :].astype(BF16)
        return carry

    lax.fori_loop(0, DEC_SEQ // rows, prep, 0)
    kc = kc_ref[0]
    kc4_ref[...] = kc.astype(BF16)
    vc4_ref[...] = vc_ref[0].astype(BF16)

    def attend(bound):
        for h in range(NA_HB):
            off = 0.0 if bound is None else bound[:, h:h + 1]
            for p in range(NA_WIN_R):
                for i in range(NA_WIN_R):
                    bias_ref[h, p, :, i * GRID_W:(i + 1) * GRID_W] = t_ref[h, p + i] - off
        shift = None if bound is None else _stacked_rows(bound, GRID_W)

        def row(r):
            kr0 = jnp.clip(r - NA_WIN_R // 2, 0, NA_ROWS - NA_WIN_R)
            pat = kr0 - r + NA_WIN_R - 1
            qs = pl.ds(pl.multiple_of(r * GRID_W, GRID_W), GRID_W)
            ks = pl.ds(pl.multiple_of(kr0 * GRID_W, GRID_W), NA_KEYS)
            q4 = _stack_heads(qn_ref[qs, :], NA_HB, NA_HD)
            s_loc = lax.dot_general(q4, kn_ref[ks, :], _NT, preferred_element_type=F32)
            s_loc = s_loc + jnp.concatenate([bias_ref[h, pat] for h in range(NA_HB)], axis=0)
            s_ctx = lax.dot_general(q4, kc4_ref[...], _NT, preferred_element_type=F32)
            if shift is None:
                (p_loc, p_ctx), inv = _softmax_parts([s_loc, s_ctx])
            else:
                p_loc, p_ctx = jnp.exp2(s_loc), jnp.exp2(s_ctx - shift)
                inv = 1.0 / (p_loc.sum(axis=-1, keepdims=True) + p_ctx.sum(axis=-1, keepdims=True))
            o4 = _bdot(p_loc.astype(BF16), vb_ref[ks, :]) + _bdot(p_ctx.astype(BF16), vc4_ref[...])
            o_ref[qs, :] = _unstack_heads(o4 * inv, NA_HB, NA_HD)

        def rows_step(i, carry):
            for u in range(NA_ROW_UNROLL):
                row(i * NA_ROW_UNROLL + u)
            return carry

        lax.fori_loop(0, NA_ROWS // NA_ROW_UNROLL, rows_step, 0)

    qmax = scale * _rms_norm_bound(gq_ref, NA_HD, NA_HD)
    bias_max = jnp.concatenate(
        [t_ref[h].max(axis=0).max(axis=0, keepdims=True).max(axis=1, keepdims=True)
         for h in range(NA_HB)], axis=-1)
    _with_score_bound(
        jnp.maximum(qmax * _rms_norm_bound(gk_ref, NA_HD, NA_HD) + bias_max,
                    qmax * _max_head_norms(kc, NA_HD)), attend)


def _na_bias_blocks(bias_table):
    qc = jnp.arange(GRID_W)[:, None]
    kc = jnp.arange(GRID_W)[None, :]
    win0 = jnp.clip(qc - NA_WIN_C // 2, 0, GRID_W - NA_WIN_C)
    valid = (kc >= win0) & (kc < win0 + NA_WIN_C)
    n_ro, n_co = bias_table.shape[1:]
    c = NA_WIN_C - 1
    period = jnp.concatenate(
        [bias_table[..., c:], jnp.zeros((NA_HEADS, n_ro, 2 * GRID_W - n_co), F32),
         bias_table[..., :c]], axis=-1)
    flat = jnp.tile(period, (1, 1, GRID_W))[..., :GRID_W * (2 * GRID_W - 1)]
    t = flat.reshape(NA_HEADS, n_ro, GRID_W, 2 * GRID_W - 1)[..., :GRID_W]
    return jnp.where(valid, t * LOG2E, NEG)


def _na_latent(qkv, cache_k, cache_v, bias_blocks, gq, gk):
    nb = NA_HEADS // NA_HB
    lat0 = N_PROMPT // DEC_SEQ
    blk = lambda off: pl.BlockSpec((DEC_SEQ, NA_LANES), lambda b, j: (lat0 + b, off + j))
    vec = pl.BlockSpec((1, NA_LANES), lambda b, j: (0, 0))
    cache = pl.BlockSpec((1, PAST_LEN, NA_LANES), lambda b, j: (b, 0, j))
    return pl.pallas_call(
        _na_latent_kernel,
        grid=(DEC_BATCH, nb),
        in_specs=[blk(0), blk(nb), blk(2 * nb), cache, cache,
                  pl.BlockSpec((NA_HB, 2 * NA_WIN_R - 1, GRID_W, GRID_W), lambda b, j: (j, 0, 0, 0)),
                  vec, vec],
        out_specs=pl.BlockSpec((DEC_SEQ, NA_LANES), lambda b, j: (b, j)),
        out_shape=jax.ShapeDtypeStruct((N_LATENT, D_MODEL), F32),
        scratch_shapes=[pltpu.VMEM((DEC_SEQ, NA_LANES), BF16),
                        pltpu.VMEM((DEC_SEQ, NA_LANES), BF16),
                        pltpu.VMEM((DEC_SEQ, NA_LANES), BF16),
                        pltpu.VMEM((PAST_LEN, NA_LANES), BF16),
                        pltpu.VMEM((PAST_LEN, NA_LANES), BF16),
                        pltpu.VMEM((NA_HB, NA_WIN_R, GRID_W, NA_KEYS), F32)],
        compiler_params=_params(("arbitrary", "arbitrary")),
        name="na_latent",
    )(qkv, qkv, qkv, _tokens_first(cache_k), _tokens_first(cache_v), bias_blocks,
      jnp.tile(gq, NA_HB).reshape(1, -1), jnp.tile(gk, NA_HB).reshape(1, -1))


GLA_C = 128
GLA_HPS = 2
GLA_SUB = 8
GLA_LEVELS = (64, 32, 16, 8)


def _split_hi_lo(x):
    hi = x.astype(BF16)
    lo = (x - hi.astype(F32)).astype(BF16)
    return jnp.concatenate([hi, lo], axis=1)


class _GlaMasks(NamedTuple):
    tri: jax.Array
    later: tuple
    sign: tuple
    pair: tuple
    diag: jax.Array


def _gla_masks(rev):
    c = GLA_C
    row = lax.broadcasted_iota(jnp.int32, (c, c), 0)
    col = lax.broadcasted_iota(jnp.int32, (c, c), 1)
    rid = lax.broadcasted_iota(jnp.int32, (c, GLA_DK), 0)
    causal = (col >= row) if rev else (col <= row)
    later, pair = [], []
    for m in GLA_LEVELS:
        later.append(((rid & m) == 0) if rev else ((rid & m) != 0))
        same = (row >> _log2(2 * m)) == (col >> _log2(2 * m))
        crossing = ((row & m) != (col & m))
        pair.append(same & crossing & causal)
    diag = ((row >> _log2(GLA_SUB)) == (col >> _log2(GLA_SUB))) & causal
    sign = tuple(jnp.where(l, 1.0, -1.0) for l in later)
    return _GlaMasks(jnp.where(causal, 1.0, 0.0).astype(BF16), tuple(later), sign, tuple(pair), diag)


def _gla_chunk(q, k, v, g, st_ref, rev, masks):
    c = GLA_C
    cs = _bdot(masks.tri, _split_hi_lo(g))
    b = cs[:, :GLA_DK] + cs[:, GLA_DK:]

    a = None
    for m, later, sign, pair in zip(GLA_LEVELS, masks.later, masks.sign, masks.pair):
        nblk = c // (2 * m)
        if rev:
            bnd = [b[j * 2 * m + m:j * 2 * m + m + 1] for j in range(nblk)]
        else:
            bnd = [b[j * 2 * m + m - 1:j * 2 * m + m] for j in range(nblk)]
        ref = jnp.concatenate([jnp.broadcast_to(x, (2 * m, GLA_DK)) for x in bnd], axis=0)
        x = (jnp.where(later, q, k) * jnp.exp((b - ref) * sign)).astype(BF16)
        blk = lax.dot_general(x, x, _NT, preferred_element_type=F32)
        a = jnp.where(pair, blk, 0.0 if a is None else a)

    nsub = c // GLA_SUB
    lane_c = lax.broadcasted_iota(jnp.int32, (GLA_SUB, c), 1)
    diag_rows = []
    for blk_i in range(nsub):
        r0 = blk_i * GLA_SUB
        qb = q[r0:r0 + GLA_SUB]
        bb = b[r0:r0 + GLA_SUB]
        acc = jnp.zeros((GLA_SUB, c), F32)
        for s in range(GLA_SUB):
            ks = k[r0 + s:r0 + s + 1]
            bs = b[r0 + s:r0 + s + 1]
            w = jnp.sum(qb * ks * jnp.exp(jnp.minimum(bb - bs, 0.0)), axis=-1, keepdims=True)
            acc = jnp.where(lane_c == r0 + s, w, acc)
        diag_rows.append(acc)
    a = jnp.where(masks.diag, jnp.concatenate(diag_rows, axis=0), a)

    st = st_ref[...]
    inter = lax.dot_general((q * jnp.exp(b)).astype(BF16), st.astype(BF16), _NT,
                            preferred_element_type=F32)
    o = inter + _bdot(a.astype(BF16), v.astype(BF16))

    btot = b[0:1] if rev else b[c - 1:c]
    kd = (k * jnp.exp(btot - b)).astype(BF16)
    st_ref[...] = st * jnp.exp(btot) + lax.dot_general(v.astype(BF16), kd, _TN,
                                                       preferred_element_type=F32)
    return o


def _gla_kernel(*refs, n_tok, has_state):
    if has_state:
        (q_ref, k_ref, v_ref, r_ref, w2_ref, bg_ref, gn_ref, s0f_ref, s0b_ref,
         o_ref, lg_ref, of_ref, ob_ref, stf_ref, stb_ref) = refs
    else:
        (q_ref, k_ref, v_ref, r_ref, w2_ref, bg_ref, gn_ref,
         o_ref, sf_ref, sb_ref, lg_ref, of_ref, ob_ref, stf_ref, stb_ref) = refs
    nc = n_tok // GLA_C
    scale = GLA_DK ** -0.5

    rb = r_ref[...].astype(BF16)
    for z in range(2):
        x = _bdot(rb, w2_ref[z].astype(BF16)) + bg_ref[z]
        lg_ref[z] = (jnp.minimum(x, 0.0) - jnp.log1p(jnp.exp(-jnp.abs(x)))) * (1.0 / GLA_GATE_NORM)

    for hh in range(GLA_HPS):
        if has_state:
            stf_ref[hh] = s0f_ref[0, hh].T
            stb_ref[hh] = s0b_ref[0, hh].T
        else:
            stf_ref[hh] = jnp.zeros((GLA_DV, GLA_DK), F32)
            stb_ref[hh] = jnp.zeros((GLA_DV, GLA_DK), F32)

    masks = {rev: _gla_masks(rev) for rev in (False, True)}

    def step(ci, carry):
        for hh in range(GLA_HPS):
            kq = slice(hh * GLA_DK, (hh + 1) * GLA_DK)
            vv = slice(hh * GLA_DV, (hh + 1) * GLA_DV)
            for rev in (False, True):
                cc = (nc - 1 - ci) if rev else ci
                sl = pl.ds(pl.multiple_of(cc * GLA_C, GLA_C), GLA_C)
                o = _gla_chunk(q_ref[sl, kq] * scale, k_ref[sl, kq], v_ref[sl, vv],
                               lg_ref[1 if rev else 0, sl, kq],
                               (stb_ref if rev else stf_ref).at[hh], rev, masks[rev])
                (ob_ref if rev else of_ref)[sl, vv] = o
        return carry

    lax.fori_loop(0, nc, step, 0)

    for hh in range(GLA_HPS):
        vv = slice(hh * GLA_DV, (hh + 1) * GLA_DV)
        o = of_ref[:, vv] + ob_ref[:, vv]
        o_ref[:, vv] = o * lax.rsqrt(jnp.mean(o * o, axis=-1, keepdims=True) + EPS) * gn_ref[...]
        if not has_state:
            sf_ref[0, hh] = stf_ref[hh].T
            sb_ref[0, hh] = stb_ref[hh].T


def _gla(proj, w2, bg, gnorm, n_seq, n_tok, row_block0, states=None):
    kw, vw = GLA_HPS * GLA_DK, GLA_HPS * GLA_DV
    spec = lambda width, off: pl.BlockSpec((n_tok, width), lambda b, h: (row_block0 + b, off + h))
    in_specs = [spec(kw, 0), spec(kw, GLA_HK // kw), spec(vw, 2 * GLA_HK // vw),
                pl.BlockSpec((n_tok, 128), lambda b, h: (row_block0 + b, (2 * GLA_HK + 2 * GLA_HV) // 128)),
                pl.BlockSpec((2, 128, kw), lambda b, h: (0, 0, h)),
                pl.BlockSpec((2, 1, kw), lambda b, h: (0, 0, h)),
                pl.BlockSpec((1, GLA_DV), lambda b, h: (0, 0))]
    args = [proj, proj, proj, proj, w2, bg, gnorm.reshape(1, GLA_DV)]
    st_spec = pl.BlockSpec((1, GLA_HPS, GLA_DK, GLA_DV), lambda b, h: (b, h, 0, 0))
    o_spec = pl.BlockSpec((n_tok, vw), lambda b, h: (b, h))
    o_shape = jax.ShapeDtypeStruct((n_seq * n_tok, GLA_HV), F32)
    if states is not None:
        in_specs += [st_spec, st_spec]
        args += list(states)
        out_specs, out_shape = o_spec, o_shape
    else:
        st_shape = jax.ShapeDtypeStruct((n_seq, GLA_HEADS, GLA_DK, GLA_DV), F32)
        out_specs, out_shape = [o_spec, st_spec, st_spec], [o_shape, st_shape, st_shape]
    return pl.pallas_call(
        functools.partial(_gla_kernel, n_tok=n_tok, has_state=states is not None),
        grid=(n_seq, GLA_HEADS // GLA_HPS),
        in_specs=in_specs,
        out_specs=out_specs,
        out_shape=out_shape,
        scratch_shapes=[pltpu.VMEM((2, n_tok, kw), F32),
                        pltpu.VMEM((n_tok, vw), F32),
                        pltpu.VMEM((n_tok, vw), F32),
                        pltpu.VMEM((GLA_HPS, GLA_DV, GLA_DK), F32),
                        pltpu.VMEM((GLA_HPS, GLA_DV, GLA_DK), F32)],
        compiler_params=_params(("arbitrary", "arbitrary")),
        name="gla_latent" if states is not None else "gla_prompt",
    )(*args)


DIFF_HB = 2
DIFF_QL = DIFF_HB * DIFF_HD
DIFF_VL = DIFF_HB * 2 * DIFF_HD
DIFF_TQ = 512
DIFF_SUB = 128


def _diff_lambda(lam_ref):
    l = lam_ref[...]
    a = jnp.sum(l[0:1] * l[1:2], axis=-1, keepdims=True)
    b = jnp.sum(l[2:3] * l[3:4], axis=-1, keepdims=True)
    return jnp.exp(a) - jnp.exp(b) + DIFF_LAMBDA_INIT


def _diff_finish(ps, invs, lam, v, sn_ref):
    a = ps[0] * invs[0] - (lam * invs[1]) * ps[1]
    o2 = _bdot(a.astype(BF16), v)
    o = _unstack_heads(o2, DIFF_HB, 2 * DIFF_HD)
    return _group_rms(o, sn_ref[...], 2 * DIFF_HD) * (1.0 - DIFF_LAMBDA_INIT)


def _diff_prompt_kernel(q0_ref, q1_ref, k0_ref, k1_ref, v_ref, gq_ref, gk_ref, lam_ref, sn_ref,
                        o_ref, kn_ref, vn_ref):
    scale = DIFF_HD ** -0.5 * LOG2E
    lam = _diff_lambda(lam_ref)

    def chains(bound):
        shift = None if bound is None else _stacked_rows(bound, SEQ)
        for seq in range(PROMPT_SEQS):
            sl = slice(seq * SEQ, (seq + 1) * SEQ)
            ps, invs = [], []
            for comp, (q_ref, k_ref) in enumerate(((q0_ref, k0_ref), (q1_ref, k1_ref))):
                q = _group_rms(q_ref[sl, :], gq_ref[...], DIFF_HD, sums_on_mxu=True) * scale
                k = _group_rms(k_ref[sl, :], gk_ref[...], DIFF_HD, sums_on_mxu=True)
                for h in range(DIFF_HB):
                    kn_ref[seq, comp, h] = k[:, h * DIFF_HD:(h + 1) * DIFF_HD]
                q2 = _stack_heads(q, DIFF_HB, DIFF_HD).astype(BF16)
                s = lax.dot_general(q2, k.astype(BF16), _NT, preferred_element_type=F32)
                (p,), inv = _softmax_parts([s], shift)
                ps.append(p)
                invs.append(inv)
            v = v_ref[sl, :]
            _store_heads(vn_ref, seq, v, DIFF_HB, 2 * DIFF_HD)
            o_ref[sl, :] = _diff_finish(ps, invs, lam, v.astype(BF16), sn_ref)

    _with_score_bound(scale * _rms_norm_bound(gq_ref, DIFF_HD, DIFF_HD)
                      * _rms_norm_bound(gk_ref, DIFF_HD, DIFF_HD), chains)


def _diff_prompt(qkv, gq, gk, lam, sub_norm):
    nb = DIFF_HEADS // DIFF_HB
    rows = PROMPT_SEQS * SEQ
    qk = lambda off: pl.BlockSpec((rows, DIFF_QL), lambda b, j: (b, off + j))
    vec = lambda n: pl.BlockSpec((1, n), lambda b, j: (0, 0))
    v_spec = pl.BlockSpec((rows, DIFF_VL), lambda b, j: (b, 2 * D_MODEL // DIFF_VL + j))
    kn_spec = pl.BlockSpec((PROMPT_SEQS, 2, DIFF_HB, SEQ, DIFF_HD), lambda b, j: (b, 0, j, 0, 0))
    vn_spec = pl.BlockSpec((PROMPT_SEQS, DIFF_HB, SEQ, 2 * DIFF_HD), lambda b, j: (b, j, 0, 0))
    return pl.pallas_call(
        _diff_prompt_kernel,
        grid=(BATCH // PROMPT_SEQS, nb),
        in_specs=[qk(0), qk(nb), qk(2 * nb), qk(3 * nb), v_spec, vec(DIFF_QL), vec(DIFF_QL),
                  pl.BlockSpec((4, DIFF_HD), lambda b, j: (0, 0)), vec(DIFF_VL)],
        out_specs=[pl.BlockSpec((rows, DIFF_VL), lambda b, j: (b, j)), kn_spec, vn_spec],
        out_shape=[jax.ShapeDtypeStruct((N_PROMPT, D_MODEL), F32),
                   jax.ShapeDtypeStruct((BATCH, 2, DIFF_HEADS, SEQ, DIFF_HD), F32),
                   jax.ShapeDtypeStruct((BATCH, DIFF_HEADS, SEQ, 2 * DIFF_HD), F32)],
        compiler_params=_params(("arbitrary", "arbitrary")),
        name="diff_prompt",
    )(qkv, qkv, qkv, qkv, qkv, jnp.tile(gq, DIFF_HB).reshape(1, -1),
      jnp.tile(gk, DIFF_HB).reshape(1, -1), lam, jnp.tile(sub_norm, DIFF_HB).reshape(1, -1))


def _diff_latent_kernel(q0_ref, q1_ref, k0_ref, k1_ref, v_ref, kc0_ref, kc1_ref, vc_ref,
                        cos_ref, sin_ref, cosq_ref, sinq_ref, gq_ref, gk_ref, lam_ref, sn_ref,
                        o_ref, kb0_ref, kb1_ref, vb_ref, kmax_ref):
    scale = DIFF_HD ** -0.5 * LOG2E
    half = DIFF_HD // 4
    rows = 256

    @pl.when(pl.program_id(2) == 0)
    def _():
        for comp, (k_ref, kc_ref, kb_ref) in enumerate(
                ((k0_ref, kc0_ref, kb0_ref), (k1_ref, kc1_ref, kb1_ref))):
            kc = kc_ref[0]
            kb_ref[0:PAST_LEN, :] = kc.astype(BF16)
            kmax_ref[comp:comp + 1, :] = jnp.maximum(
                _max_head_norms(kc, DIFF_HD), _rms_norm_bound(gk_ref, DIFF_HD, DIFF_HD))

            def prep(r, carry):
                sl = pl.ds(pl.multiple_of(r * rows, rows), rows)
                k = _group_rms(k_ref[sl, :], gk_ref[...], DIFF_HD)
                k = _rope(k, cos_ref[sl, :], sin_ref[sl, :], half)
                kb_ref[pl.ds(pl.multiple_of(PAST_LEN + r * rows, rows), rows), :] = k.astype(BF16)
                return carry

            lax.fori_loop(0, DEC_SEQ // rows, prep, 0)
        vb_ref[0:PAST_LEN, :] = vc_ref[0].astype(BF16)
        vb_ref[PAST_LEN:, :] = v_ref[...].astype(BF16)

    lam = _diff_lambda(lam_ref)

    def chains(bound):
        for r0 in range(0, DIFF_TQ, DIFF_SUB):
            sl = slice(r0, r0 + DIFF_SUB)
            ps, invs = [], []
            for comp, (q_ref, kb_ref) in enumerate(((q0_ref, kb0_ref), (q1_ref, kb1_ref))):
                shift = None if bound is None else _stacked_rows(
                    bound[:, comp * DIFF_HB:(comp + 1) * DIFF_HB], DIFF_SUB)
                q = _group_rms(q_ref[sl, :], gq_ref[...], DIFF_HD)
                q = _rope(q, cosq_ref[sl, :], sinq_ref[sl, :], half) * scale
                q2 = _stack_heads(q, DIFF_HB, DIFF_HD).astype(BF16)
                s = lax.dot_general(q2, kb_ref[...], _NT, preferred_element_type=F32)
                (p,), inv = _softmax_parts([s], shift)
                ps.append(p)
                invs.append(inv)
            o_ref[sl, :] = _diff_finish(ps, invs, lam, vb_ref[...], sn_ref)

    qmax = scale * _rms_norm_bound(gq_ref, DIFF_HD, DIFF_HD)
    _with_score_bound(jnp.concatenate([qmax * kmax_ref[0:1, :], qmax * kmax_ref[1:2, :]], axis=-1),
                      chains)


def _diff_latent(qkv, cache_k, cache_v, cos, sin, gq, gk, lam, sub_norm):
    nb = DIFF_HEADS // DIFF_HB
    nq = DEC_SEQ // DIFF_TQ
    q0 = N_PROMPT // DIFF_TQ
    lat0 = N_PROMPT // DEC_SEQ
    n_keys = PAST_LEN + DEC_SEQ
    q_spec = lambda off: pl.BlockSpec((DIFF_TQ, DIFF_QL), lambda b, j, t: (q0 + b * nq + t, off + j))
    k_spec = lambda off: pl.BlockSpec((DEC_SEQ, DIFF_QL), lambda b, j, t: (lat0 + b, off + j))
    v_spec = pl.BlockSpec((DEC_SEQ, DIFF_VL), lambda b, j, t: (lat0 + b, 2 * D_MODEL // DIFF_VL + j))
    kc_spec = lambda off: pl.BlockSpec((1, PAST_LEN, DIFF_QL), lambda b, j, t: (b, 0, off + j))
    vc_spec = pl.BlockSpec((1, PAST_LEN, DIFF_VL), lambda b, j, t: (b, 0, j))
    tab = pl.BlockSpec((DEC_SEQ, DIFF_QL), lambda b, j, t: (0, 0))
    tabq = pl.BlockSpec((DIFF_TQ, DIFF_QL), lambda b, j, t: (t, 0))
    vec = lambda n: pl.BlockSpec((1, n), lambda b, j, t: (0, 0))
    return pl.pallas_call(
        _diff_latent_kernel,
        grid=(DEC_BATCH, nb, nq),
        in_specs=[q_spec(0), q_spec(nb), k_spec(2 * nb), k_spec(3 * nb), v_spec,
                  kc_spec(0), kc_spec(nb), vc_spec, tab, tab, tabq, tabq,
                  vec(DIFF_QL), vec(DIFF_QL),
                  pl.BlockSpec((4, DIFF_HD), lambda b, j, t: (0, 0)), vec(DIFF_VL)],
        out_specs=pl.BlockSpec((DIFF_TQ, DIFF_VL), lambda b, j, t: (b * nq + t, j)),
        out_shape=jax.ShapeDtypeStruct((N_LATENT, D_MODEL), F32),
        scratch_shapes=[pltpu.VMEM((n_keys, DIFF_QL), BF16),
                        pltpu.VMEM((n_keys, DIFF_QL), BF16),
                        pltpu.VMEM((n_keys, DIFF_VL), BF16),
                        pltpu.VMEM((2, DIFF_HB), F32)],
        compiler_params=_params(("arbitrary", "arbitrary", "arbitrary")),
        name="diff_latent",
    )(qkv, qkv, qkv, qkv, qkv, _tokens_first(cache_k), _tokens_first(cache_k),
      _tokens_first(cache_v), cos, sin, cos, sin,
      jnp.tile(gq, DIFF_HB).reshape(1, -1), jnp.tile(gk, DIFF_HB).reshape(1, -1), lam,
      jnp.tile(sub_norm, DIFF_HB).reshape(1, -1))


MLA_HB = 2
MLA_HL = 128
MLA_LANES = MLA_HB * MLA_HL
MLA_TQ = 512
MLA_SUB = 128


def _mla_keys(kv, kr, gk, sums_on_mxu=False):
    lane = lax.broadcasted_iota(jnp.int32, kv.shape, 1)
    kr2 = jnp.concatenate([kr] * MLA_HB, axis=1)
    k = jnp.where((lane & (MLA_HL - 1)) < MLA_NOPE, kv, kr2)
    return _group_rms(k, gk, MLA_HL, n_real=MLA_QK, sums_on_mxu=sums_on_mxu)


def _mla_out(o2):
    tq = o2.shape[0] // MLA_HB
    oa = pltpu.roll(o2[0:tq, 0:MLA_HL], MLA_HL - MLA_V, axis=1)
    ob = o2[tq:, MLA_HL:]
    lane = lax.broadcasted_iota(jnp.int32, oa.shape, 1)
    return jnp.where(lane < MLA_V, oa, ob)


def _mla_prompt_kernel(q_ref, kv_ref, kr_ref, gq_ref, gk_ref, o_ref):
    scale = MLA_QK ** -0.5 * LOG2E

    def chains(bound):
        shift = None if bound is None else _stacked_rows(bound, SEQ)
        for seq in range(PROMPT_SEQS):
            sl = slice(seq * SEQ, (seq + 1) * SEQ)
            q = _group_rms(q_ref[sl, :], gq_ref[...], MLA_HL, n_real=MLA_QK) * scale
            kv = kv_ref[sl, :]
            k = _mla_keys(kv, kr_ref[sl, :], gk_ref[...])
            q2 = _stack_heads(q, MLA_HB, MLA_HL).astype(BF16)
            s = lax.dot_general(q2, k.astype(BF16), _NT, preferred_element_type=F32)
            (p,), inv = _softmax_parts([s], shift)
            o_ref[sl, :] = _mla_out(_bdot(p.astype(BF16), kv.astype(BF16)) * inv)

    _with_score_bound(scale * _rms_norm_bound(gq_ref, MLA_QK, MLA_HL)
                      * _rms_norm_bound(gk_ref, MLA_QK, MLA_HL), chains)


def _mla_prompt(qp, kvp, low, gq, gk):
    nb = MLA_HEADS // MLA_HB
    rows = PROMPT_SEQS * SEQ
    blk = pl.BlockSpec((rows, MLA_LANES), lambda b, j: (b, j))
    vec = pl.BlockSpec((1, MLA_LANES), lambda b, j: (0, 0))
    return pl.pallas_call(
        _mla_prompt_kernel,
        grid=(BATCH // PROMPT_SEQS, nb),
        in_specs=[blk, blk, pl.BlockSpec((rows, MLA_HL), lambda b, j: (b, MLA_LOW_KR // MLA_HL)),
                  vec, vec],
        out_specs=pl.BlockSpec((rows, MLA_HB * MLA_V), lambda b, j: (b, j)),
        out_shape=jax.ShapeDtypeStruct((N_PROMPT, MLA_HEADS * MLA_V), F32),
        compiler_params=_params(("arbitrary", "arbitrary")),
        name="mla_prompt",
    )(qp, kvp, low, gq, gk)


def _mla_latent_kernel(q_ref, kv_ref, kr_ref, kvc_ref, krc_ref, cos_ref, sin_ref, cosq_ref,
                       sinq_ref, gq_ref, gk_ref, o_ref, kb_ref, vb_ref, kmax_ref):
    scale = MLA_QK ** -0.5 * LOG2E
    half = MLA_ROPE // 4
    rows = 256

    @pl.when(pl.program_id(2) == 0)
    def _():
        kvc = kvc_ref[...]
        kc = _mla_keys(kvc, krc_ref[...], gk_ref[...])
        kb_ref[0:PAST_LEN, :] = kc.astype(BF16)
        vb_ref[0:PAST_LEN, :] = kvc.astype(BF16)

        def prep(r, carry):
            sl = pl.ds(pl.multiple_of(r * rows, rows), rows)
            dst = pl.ds(pl.multiple_of(PAST_LEN + r * rows, rows), rows)
            kv = kv_ref[sl, :]
            k = _mla_keys(kv, kr_ref[sl, :], gk_ref[...], sums_on_mxu=True)
            k = _rope(k, cos_ref[sl, :], sin_ref[sl, :], half)
            kb_ref[dst, :] = k.astype(BF16)
            vb_ref[dst, :] = kv.astype(BF16)
            return carry

        lax.fori_loop(0, DEC_SEQ // rows, prep, 0)
        kmax_ref[...] = jnp.maximum(_max_head_norms(kc, MLA_HL),
                                    _rms_norm_bound(gk_ref, MLA_QK, MLA_HL))

    def chains(bound):
        shift = None if bound is None else _stacked_rows(bound, MLA_SUB)
        for r0 in range(0, MLA_TQ, MLA_SUB):
            sl = slice(r0, r0 + MLA_SUB)
            q = _group_rms(q_ref[sl, :], gq_ref[...], MLA_HL, n_real=MLA_QK)
            q = _rope(q, cosq_ref[sl, :], sinq_ref[sl, :], half) * scale
            q2 = _stack_heads(q, MLA_HB, MLA_HL).astype(BF16)
            s = lax.dot_general(q2, kb_ref[...], _NT, preferred_element_type=F32)
            (p,), inv = _softmax_parts([s], shift)
            o_ref[sl, :] = _mla_out(_bdot(p.astype(BF16), vb_ref[...]) * inv)

    _with_score_bound(scale * _rms_norm_bound(gq_ref, MLA_QK, MLA_HL) * kmax_ref[...], chains)


def _mla_latent(qp, kvp, low, kvc, krc, cos, sin, gq, gk):
    nb = MLA_HEADS // MLA_HB
    nq = DEC_SEQ // MLA_TQ
    q0 = N_PROMPT // MLA_TQ
    lat0 = N_PROMPT // DEC_SEQ
    n_keys = PAST_LEN + DEC_SEQ
    tab = pl.BlockSpec((DEC_SEQ, MLA_LANES), lambda b, j, t: (0, 0))
    tabq = pl.BlockSpec((MLA_TQ, MLA_LANES), lambda b, j, t: (t, 0))
    vec = pl.BlockSpec((1, MLA_LANES), lambda b, j, t: (0, 0))
    return pl.pallas_call(
        _mla_latent_kernel,
        grid=(DEC_BATCH, nb, nq),
        in_specs=[pl.BlockSpec((MLA_TQ, MLA_LANES), lambda b, j, t: (q0 + b * nq + t, j)),
                  pl.BlockSpec((DEC_SEQ, MLA_LANES), lambda b, j, t: (lat0 + b, j)),
                  pl.BlockSpec((DEC_SEQ, MLA_HL), lambda b, j, t: (lat0 + b, MLA_LOW_KR // MLA_HL)),
                  pl.BlockSpec((PAST_LEN, MLA_LANES), lambda b, j, t: (b, j)),
                  pl.BlockSpec((PAST_LEN, MLA_HL), lambda b, j, t: (b, 0)),
                  tab, tab, tabq, tabq, vec, vec],
        out_specs=pl.BlockSpec((MLA_TQ, MLA_HB * MLA_V), lambda b, j, t: (b * nq + t, j)),
        out_shape=jax.ShapeDtypeStruct((N_LATENT, MLA_HEADS * MLA_V), F32),
        scratch_shapes=[pltpu.VMEM((n_keys, MLA_LANES), BF16),
                        pltpu.VMEM((n_keys, MLA_LANES), BF16),
                        pltpu.VMEM((1, MLA_HB), F32)],
        compiler_params=_params(("arbitrary", "arbitrary", "arbitrary")),
        name="mla_latent",
    )(qp, kvp, low, kvc, krc, cos, sin, cos, sin, gq, gk)


MLA_LOW_Q = 0
MLA_LOW_KV = 512
MLA_LOW_KR = 768
MLA_LOW_N = 896


def _axial_tables(n_tok, rdim):
    nf = rdim // 4
    freqs = ROPE_BASE ** (-jnp.arange(nf, dtype=F32) / nf)
    t = jnp.arange(n_tok)
    rowp = (t // GRID_W).astype(F32)
    colp = (t % GRID_W).astype(F32)
    ang = jnp.stack([rowp[:, None] * freqs, colp[:, None] * freqs], axis=1)
    cos, sin = jnp.cos(ang), jnp.sin(ang)
    cos_l = jnp.stack([cos, cos], axis=2).reshape(n_tok, rdim)
    sin_l = jnp.stack([-sin, sin], axis=2).reshape(n_tok, rdim)
    return cos_l, sin_l


def _diff_rope_tables():
    cos, sin = _axial_tables(DEC_SEQ, DIFF_HD)
    return jnp.tile(cos, (1, DIFF_HB)), jnp.tile(sin, (1, DIFF_HB))


def _mla_rope_tables():
    cos, sin = _axial_tables(DEC_SEQ, MLA_ROPE)
    ones = jnp.ones((DEC_SEQ, MLA_NOPE), F32)
    pad1 = jnp.ones((DEC_SEQ, MLA_HL - MLA_QK), F32)
    cos_h = jnp.concatenate([ones, cos, pad1], axis=1)
    sin_h = jnp.concatenate([0 * ones, sin, 0 * pad1], axis=1)
    return jnp.tile(cos_h, (1, MLA_HB)), jnp.tile(sin_h, (1, MLA_HB))


def _tokens_first(cache):
    b, h, l, d = cache.shape
    return jnp.transpose(cache, (0, 2, 1, 3)).reshape(b, l, h * d)


def _pad_heads(w, heads, hd, hl):
    k = w.shape[0]
    return jnp.pad(w.reshape(k, heads, hd), ((0, 0), (0, 0), (0, hl - hd))).reshape(k, heads * hl)


def kernel(x_prompt, x_sample, cache_l0_k, cache_l0_v, state_l1_fwd, state_l1_bwd, cache_l2_k,
           cache_l2_v, cache_l3_ckv, cache_l3_krope, c, c_ctx, ada_w, ada_b, norm_mix, norm_ffn,
           ffn_w_up, ffn_conv_w, ffn_conv_b, ffn_w_down, na_w_qkv, na_q_norm, na_k_norm, na_bias,
           na_w_o, gla_w_qkvg, gla_w_gate1, gla_w_gate2, gla_b_gate, gla_o_norm, gla_w_o,
           diff_w_qkv, diff_q_norm, diff_k_norm, diff_lambda, diff_sub_norm, diff_w_o, mla_w_dq,
           mla_q_a_norm, mla_w_uq, mla_w_dkv, mla_kv_a_norm, mla_w_ukv, mla_q_norm, mla_k_norm,
           mla_w_o):
    xr = _Rows(x_prompt.reshape(N_PROMPT, D_MODEL), x_sample.reshape(N_LATENT, D_MODEL), 0)
    cvecs = jnp.concatenate([c_ctx[None], c, jnp.zeros((5, D_MODEL), F32)], axis=0)
    mods_all = _ada_mods(cvecs, ada_w, ada_b)
    halves = lambda o_p, o_s: _Rows(o_p, o_s, 0)

    mods = mods_all[0]
    qkv = _norm_mod_proj(xr, norm_mix[0], mods, [na_w_qkv], "na_qkv")
    o_p, new_l0_k, new_l0_v = _na_prompt(qkv, na_q_norm, na_k_norm)
    o_s = _na_latent(qkv, cache_l0_k, cache_l0_v, _na_bias_blocks(na_bias), na_q_norm, na_k_norm)
    x = _out_proj_residual(xr, halves(o_p, o_s), mods, na_w_o, "na_out")
    ffn_weights = (ffn_w_up, ffn_conv_w, ffn_conv_b, ffn_w_down)
    x = _ffn(x, norm_ffn[0], mods, 0, *ffn_weights)
    xr = _one_array(x)

    mods = mods_all[1]
    w_decay = jnp.concatenate(
        [gla_w_gate1[0], gla_w_gate1[1],
         jnp.zeros((D_MODEL, 128 - 2 * GLA_GATE_RANK), F32)], axis=1)
    proj = _norm_mod_proj(xr, norm_mix[1], mods, [gla_w_qkvg, w_decay], "gla_proj")
    w2 = jnp.zeros((2, 128, GLA_HK), F32)
    w2 = w2.at[0, :GLA_GATE_RANK].set(gla_w_gate2[0])
    w2 = w2.at[1, GLA_GATE_RANK:2 * GLA_GATE_RANK].set(gla_w_gate2[1])
    bg = gla_b_gate.reshape(2, 1, GLA_HK)
    o_p, new_l1_fwd, new_l1_bwd = _gla(proj, w2, bg, gla_o_norm, BATCH, SEQ, 0)
    o_s = _gla(proj, w2, bg, gla_o_norm, DEC_BATCH, DEC_SEQ, N_PROMPT // DEC_SEQ,
               states=(state_l1_fwd, state_l1_bwd))
    x = _out_proj_residual(xr, halves(o_p, o_s), mods, gla_w_o, "gla_out",
                           gate=proj, gate_col_block=(2 * GLA_HK + GLA_HV) // GLA_HV)
    x = _ffn(x, norm_ffn[1], mods, 1, *ffn_weights)
    xr = _one_array(x)

    mods = mods_all[2]
    qkv = _norm_mod_proj(xr, norm_mix[2], mods, [diff_w_qkv], "diff_qkv")
    o_p, kn_p, new_l2_v = _diff_prompt(qkv, diff_q_norm, diff_k_norm, diff_lambda, diff_sub_norm)
    new_l2_k = kn_p.reshape(BATCH, 2 * DIFF_HEADS, SEQ, DIFF_HD)
    cos_d, sin_d = _diff_rope_tables()
    o_s = _diff_latent(qkv, cache_l2_k, cache_l2_v, cos_d, sin_d, diff_q_norm, diff_k_norm,
                       diff_lambda, diff_sub_norm)
    x = _out_proj_residual(xr, halves(o_p, o_s), mods, diff_w_o, "diff_out")
    x = _ffn(x, norm_ffn[2], mods, 2, *ffn_weights)
    xr = _one_array(x)

    mods = mods_all[3]
    zc = lambda n: jnp.zeros((D_MODEL, n), F32)
    w_low = jnp.concatenate(
        [mla_w_dq, zc(MLA_LOW_KV - MLA_Q_RANK), mla_w_dkv[:, :MLA_KV_RANK],
         zc(MLA_NOPE), mla_w_dkv[:, MLA_KV_RANK:], zc(MLA_HL - MLA_QK)], axis=1)
    low = _norm_mod_proj(xr, norm_mix[3], mods, [w_low], "mla_down")
    w_uq = _pad_heads(mla_w_uq, MLA_HEADS, MLA_QK, MLA_HL)
    qp, _ = _rms_matmul(low, 0, MLA_Q_RANK, mla_q_a_norm, w_uq, True, "mla_uq")
    kvp, ckv = _rms_matmul(low, MLA_LOW_KV // MLA_KV_RANK, MLA_KV_RANK, mla_kv_a_norm, mla_w_ukv,
                           True, "mla_ukv")
    kvc, _ = _rms_matmul(cache_l3_ckv.reshape(DEC_BATCH * PAST_LEN, MLA_KV_RANK), 0, MLA_KV_RANK,
                         mla_kv_a_norm, mla_w_ukv, False, "mla_ukv_cache")
    krc = jnp.pad(cache_l3_krope.reshape(DEC_BATCH * PAST_LEN, MLA_ROPE),
                  ((0, 0), (MLA_NOPE, MLA_HL - MLA_QK)))
    pad_gain = lambda g: jnp.tile(jnp.pad(g, (0, MLA_HL - MLA_QK)), MLA_HB).reshape(1, -1)
    gq, gk = pad_gain(mla_q_norm), pad_gain(mla_k_norm)
    o_p = _mla_prompt(qp, kvp, low, gq, gk)
    cos_m, sin_m = _mla_rope_tables()
    o_s = _mla_latent(qp, kvp, low, kvc, krc, cos_m, sin_m, gq, gk)
    new_l3_ckv = ckv[:N_PROMPT].reshape(BATCH, SEQ, MLA_KV_RANK)
    new_l3_krope = low[:N_PROMPT, MLA_LOW_KR + MLA_NOPE:MLA_LOW_KR + MLA_QK].reshape(
        BATCH, SEQ, MLA_ROPE)
    x = _out_proj_residual(xr, halves(o_p, o_s), mods, mla_w_o, "mla_out")
    n_pt = N_PROMPT // TOK_TILE
    ffn3 = functools.partial(_ffn, x, norm_ffn[3], mods, 3, *ffn_weights)
    y_prompt = ffn3(tile0=0, n_tiles=n_pt).reshape(BATCH, SEQ, D_MODEL)
    y_sample = ffn3(tile0=n_pt, n_tiles=N_TOK_TILES - n_pt).reshape(DEC_BATCH, DEC_SEQ, D_MODEL)
    return (y_prompt, y_sample, new_l0_k, new_l0_v, new_l1_fwd, new_l1_bwd, new_l2_k, new_l2_v,
            new_l3_ckv, new_l3_krope)
```

```python
import functools
import math
from typing import NamedTuple

import jax
import jax.numpy as jnp
from jax import lax
from jax.experimental import pallas as pl
from jax.experimental.pallas import tpu as pltpu

F32 = jnp.float32
BF16 = jnp.bfloat16

D_MODEL = 1024
BATCH = 16
SEQ = 256
DEPTH = 4
DEC_BATCH = 2
DEC_SEQ = 2048
PAST_LEN = 256
GRID_W = 64
D_FF = 2816
EPS = 1e-6
ROPE_BASE = 10000.0

NA_HEADS = 16
NA_HD = 64
NA_WIN_R = 8
NA_WIN_C = 16

GLA_HEADS = 4
GLA_DK = 128
GLA_DV = 256
GLA_HK = GLA_HEADS * GLA_DK
GLA_HV = GLA_HEADS * GLA_DV
GLA_GATE_RANK = 16
GLA_GATE_NORM = 16.0

DIFF_HEADS = 8
DIFF_HD = 64
DIFF_LAMBDA_INIT = 0.8 - 0.6 * math.exp(-0.3 * 2)

MLA_HEADS = 16
MLA_Q_RANK = 384
MLA_KV_RANK = 256
MLA_NOPE = 64
MLA_ROPE = 32
MLA_V = 64
MLA_QK = MLA_NOPE + MLA_ROPE

N_PROMPT = BATCH * SEQ
N_LATENT = DEC_BATCH * DEC_SEQ
N_TOK = N_PROMPT + N_LATENT
TOK_TILE = 2048
N_TOK_TILES = N_TOK // TOK_TILE
FF_CHUNK = 256
N_FF_CHUNKS = D_FF // FF_CHUNK
NEG = -1e30
LOG2E = math.log2(math.e)
SAFE_SCORE_BOUND = 48.0

VMEM_LIMIT = 56 * 1024 * 1024

_NT = (((1,), (1,)), ((), ()))
_TN = (((0,), (0,)), ((), ()))


def _params(sem, vmem=VMEM_LIMIT):
    return pltpu.CompilerParams(dimension_semantics=sem, vmem_limit_bytes=vmem)


def _log2(n):
    assert n & (n - 1) == 0
    return n.bit_length() - 1


def _silu(x):
    return x / (1.0 + jnp.exp(-x))


def _bdot(a, b):
    return jnp.dot(a, b, preferred_element_type=F32)


def _softmax_parts(parts, shift=None):
    m = shift
    if m is None:
        m = parts[0].max(axis=-1, keepdims=True)
        for s in parts[1:]:
            m = jnp.maximum(m, s.max(axis=-1, keepdims=True))
    ps = [jnp.exp2(s - m) for s in parts]
    l = ps[0].sum(axis=-1, keepdims=True)
    for p in ps[1:]:
        l = l + p.sum(axis=-1, keepdims=True)
    return ps, 1.0 / l


def _group_rms(x, gain, group, n_real=None, sums_on_mxu=False):
    lanes = x.shape[-1]
    n_real = n_real or group
    x2 = x * x
    if group == lanes:
        ms = jnp.sum(x2, axis=-1, keepdims=True)
    elif not sums_on_mxu:
        gid = lax.broadcasted_iota(jnp.int32, x.shape, 1) >> _log2(group)
        ms = jnp.zeros_like(x)
        for i in range(lanes // group):
            sel = gid == i
            si = jnp.sum(jnp.where(sel, x2, 0.0), axis=-1, keepdims=True)
            ms = jnp.where(sel, si, ms)
    else:
        r = lax.broadcasted_iota(jnp.int32, (lanes, lanes), 0) >> _log2(group)
        c = lax.broadcasted_iota(jnp.int32, (lanes, lanes), 1) >> _log2(group)
        ones = jnp.where(r == c, 1.0, 0.0).astype(BF16)
        hi = x2.astype(BF16)
        lo = (x2 - hi.astype(F32)).astype(BF16)
        ms = _bdot(hi, ones) + _bdot(lo, ones)
    return x * lax.rsqrt(ms * (1.0 / n_real) + EPS) * gain


def _with_score_bound(bound, body):
    ok = bound.max() < SAFE_SCORE_BOUND
    pl.when(ok)(lambda: body(bound))
    pl.when(jnp.logical_not(ok))(lambda: body(None))


def _rms_norm_bound(g_ref, n, head_lanes):
    g = jnp.abs(g_ref[...])
    heads = g.shape[-1] // head_lanes
    return n ** 0.5 * jnp.concatenate(
        [g[:, h * head_lanes:(h + 1) * head_lanes].max(axis=-1, keepdims=True)
         for h in range(heads)], axis=-1)


def _stacked_rows(per_head, rows):
    r = lax.broadcasted_iota(jnp.int32, (per_head.shape[-1] * rows, 1), 0)
    out = per_head[:, 0:1]
    for h in range(1, per_head.shape[-1]):
        out = jnp.where(r >= h * rows, per_head[:, h:h + 1], out)
    return out


def _max_head_norms(x, head_lanes):
    hid = lax.broadcasted_iota(jnp.int32, x.shape, 1) >> _log2(head_lanes)
    x2 = x * x
    sq = [jnp.sum(jnp.where(hid == h, x2, 0.0), axis=-1, keepdims=True).max(axis=0, keepdims=True)
          for h in range(x.shape[-1] // head_lanes)]
    return jnp.sqrt(jnp.concatenate(sq, axis=-1))


def _rope(x, cos, sin, half):
    lanes = x.shape[-1]
    lane = lax.broadcasted_iota(jnp.int32, x.shape, 1)
    up = pltpu.roll(x, lanes - half, axis=1)
    dn = pltpu.roll(x, half, axis=1)
    swapped = jnp.where((lane & (2 * half - 1)) < half, up, dn)
    return x * cos + swapped * sin


def _stack_heads(q, n_heads, head_lanes):
    hid = lax.broadcasted_iota(jnp.int32, q.shape, 1) >> _log2(head_lanes)
    zero = jnp.zeros_like(q)
    return jnp.concatenate([jnp.where(hid == i, q, zero) for i in range(n_heads)], axis=0)


def _unstack_heads(o, n_heads, head_lanes):
    rows = o.shape[0] // n_heads
    hid = lax.broadcasted_iota(jnp.int32, (rows, o.shape[1]), 1) >> _log2(head_lanes)
    out = o[0:rows]
    for i in range(1, n_heads):
        out = jnp.where(hid == i, o[i * rows:(i + 1) * rows], out)
    return out


ADA_TN = 1536


def _ada_kernel(c_ref, w_ref, b_ref, o_ref):
    s = _silu(c_ref[...])
    w = w_ref[0]
    s_hi, w_hi = s.astype(BF16), w.astype(BF16)
    s_lo = (s - s_hi.astype(F32)).astype(BF16)
    w_lo = (w - w_hi.astype(F32)).astype(BF16)
    both = _bdot(jnp.concatenate([s_hi, s_lo], axis=0), w_hi)
    o_ref[0] = both[0:8] + both[8:16] + _bdot(s_hi, w_lo) + b_ref[0]


def _ada_mods(cvecs, ada_w, ada_b):
    out = pl.pallas_call(
        _ada_kernel,
        grid=(DEPTH, 6 * D_MODEL // ADA_TN),
        in_specs=[pl.BlockSpec((8, D_MODEL), lambda l, j: (0, 0)),
                  pl.BlockSpec((1, D_MODEL, ADA_TN), lambda l, j: (l, 0, j)),
                  pl.BlockSpec((1, 1, ADA_TN), lambda l, j: (l, 0, j))],
        out_specs=pl.BlockSpec((1, 8, ADA_TN), lambda l, j: (l, 0, j)),
        out_shape=jax.ShapeDtypeStruct((DEPTH, 8, 6 * D_MODEL), F32),
        compiler_params=_params(("arbitrary", "arbitrary")),
        name="ada_mod",
    )(cvecs, ada_w, ada_b.reshape(DEPTH, 1, 6 * D_MODEL))
    return out.reshape(DEPTH, 8, 6, D_MODEL)[:, :3]


def _mod_group_of_tile(i):
    return jnp.maximum(i - (N_PROMPT // TOK_TILE - 1), 0)


def _norm_mod_rows(x_ref, g_ref, mod_ref, h_ref, shift_idx, scale_idx, rows=64):
    g = g_ref[...]
    sc = 1.0 + mod_ref[0, scale_idx:scale_idx + 1, :]
    sh = mod_ref[0, shift_idx:shift_idx + 1, :]

    def body(r, carry):
        sl = pl.ds(pl.multiple_of(r * rows, rows), rows)
        xf = x_ref[sl, :]
        ms = jnp.mean(xf * xf, axis=-1, keepdims=True)
        y = xf * lax.rsqrt(ms + EPS) * g
        h_ref[sl, :] = (y * sc + sh).astype(BF16)
        return carry

    lax.fori_loop(0, x_ref.shape[0] // rows, body, 0)


ROW_TM = 512


class _Rows(NamedTuple):
    prompt: jax.Array
    latent: jax.Array
    latent_row0: int


def _one_array(x):
    return _Rows(x, x, N_PROMPT)


def _row_specs(rows, width, col_block=0):
    n_p = N_PROMPT // ROW_TM
    l0 = rows.latent_row0 // ROW_TM
    return [pl.BlockSpec((ROW_TM, width), lambda t: (jnp.minimum(t, n_p - 1), col_block)),
            pl.BlockSpec((ROW_TM, width), lambda t: (l0 + jnp.maximum(t - n_p, 0), col_block))]


def _row_group(t):
    first_latent = N_PROMPT // ROW_TM
    return jnp.where(t < first_latent, 0, 1 + (t - first_latent) // (DEC_SEQ // ROW_TM))


def _is_prompt_tile():
    return pl.program_id(0) < N_PROMPT // ROW_TM


def _proj_kernel(xp_ref, xl_ref, g_ref, mod_ref, *refs):
    *w_refs, o_ref, h_ref, wb_ref = refs

    @pl.when(pl.program_id(0) == 0)
    def _():
        off = 0
        for w_ref in w_refs:
            wb_ref[:, off:off + w_ref.shape[1]] = w_ref[...].astype(BF16)
            off += w_ref.shape[1]

    is_prompt = _is_prompt_tile()
    g = g_ref[...]
    sc = 1.0 + mod_ref[0, 1:2, :]
    sh = mod_ref[0, 0:1, :]
    rows = 64
    part = ROW_TM // 2
    for p0 in range(0, ROW_TM, part):
        for r0 in range(p0, p0 + part, rows):
            sl = slice(r0, r0 + rows)
            xf = jnp.where(is_prompt, xp_ref[sl, :], xl_ref[sl, :])
            ms = jnp.mean(xf * xf, axis=-1, keepdims=True)
            y = xf * lax.rsqrt(ms + EPS) * g
            h_ref[sl, :] = (y * sc + sh).astype(BF16)
        o_ref[p0:p0 + part, :] = _bdot(h_ref[p0:p0 + part, :], wb_ref[...])


def _norm_mod_proj(x, g, mods, ws, name):
    n = sum(w.shape[1] for w in ws)
    return pl.pallas_call(
        _proj_kernel,
        grid=(N_TOK // ROW_TM,),
        in_specs=_row_specs(x, D_MODEL) + [
            pl.BlockSpec((1, D_MODEL), lambda t: (0, 0)),
            pl.BlockSpec((1, 6, D_MODEL), lambda t: (_row_group(t), 0, 0))] + [
            pl.BlockSpec(w.shape, lambda t: (0, 0), pipeline_mode=pl.Buffered(1)) for w in ws],
        out_specs=pl.BlockSpec((ROW_TM, n), lambda t: (t, 0)),
        out_shape=jax.ShapeDtypeStruct((N_TOK, n), F32),
        scratch_shapes=[pltpu.VMEM((ROW_TM, D_MODEL), BF16),
                        pltpu.VMEM((D_MODEL, n), BF16)],
        compiler_params=_params(("arbitrary",)),
        name=name,
    )(x.prompt, x.latent, g.reshape(1, D_MODEL), mods, *ws)


def _rms_matmul_kernel(a_ref, g_ref, w_ref, o_ref, n_ref, *, normalise):
    a = a_ref[...]
    if normalise:
        a = a * lax.rsqrt(jnp.mean(a * a, axis=-1, keepdims=True) + EPS) * g_ref[...]
    n_ref[...] = a
    o_ref[...] = _bdot(a.astype(BF16), w_ref[...].astype(BF16))


def _rms_matmul(a, col_block, k, g, w, normalise, name, tm=512):
    rows, n = a.shape[0], w.shape[1]
    return pl.pallas_call(
        functools.partial(_rms_matmul_kernel, normalise=normalise),
        grid=(rows // tm,),
        in_specs=[pl.BlockSpec((tm, k), lambda i: (i, col_block)),
                  pl.BlockSpec((1, k), lambda i: (0, 0)),
                  pl.BlockSpec((k, n), lambda i: (0, 0))],
        out_specs=[pl.BlockSpec((tm, n), lambda i: (i, 0)),
                   pl.BlockSpec((tm, k), lambda i: (i, 0))],
        out_shape=[jax.ShapeDtypeStruct((rows, n), F32),
                   jax.ShapeDtypeStruct((rows, k), F32)],
        compiler_params=_params(("arbitrary",)),
        name=name,
    )(a, g.reshape(1, k), w)


def _oproj_kernel(*refs, gated):
    if gated:
        xp_ref, xl_ref, ap_ref, al_ref, g_ref, mod_ref, w_ref, o_ref, wb_ref = refs
    else:
        xp_ref, xl_ref, ap_ref, al_ref, mod_ref, w_ref, o_ref, wb_ref = refs

    @pl.when(pl.program_id(0) == 0)
    def _():
        wb_ref[...] = w_ref[...].astype(BF16)

    is_prompt = _is_prompt_tile()
    a = jnp.where(is_prompt, ap_ref[...], al_ref[...])
    if gated:
        a = a * _silu(g_ref[...])
    y = _bdot(a.astype(BF16), wb_ref[...])
    x = jnp.where(is_prompt, xp_ref[...], xl_ref[...])
    o_ref[...] = x + mod_ref[0, 2:3, :] * y


def _out_proj_residual(x, a, mods, w, name, gate=None, gate_col_block=0):
    k = w.shape[0]
    in_specs = _row_specs(x, D_MODEL) + _row_specs(a, k)
    args = [x.prompt, x.latent, a.prompt, a.latent]
    if gate is not None:
        in_specs.append(pl.BlockSpec((ROW_TM, k), lambda t: (t, gate_col_block)))
        args.append(gate)
    in_specs += [pl.BlockSpec((1, 6, D_MODEL), lambda t: (_row_group(t), 0, 0)),
                 pl.BlockSpec((k, D_MODEL), lambda t: (0, 0))]
    args += [mods, w]
    return pl.pallas_call(
        functools.partial(_oproj_kernel, gated=gate is not None),
        grid=(N_TOK // ROW_TM,),
        in_specs=in_specs,
        out_specs=pl.BlockSpec((ROW_TM, D_MODEL), lambda t: (t, 0)),
        out_shape=jax.ShapeDtypeStruct((N_TOK, D_MODEL), F32),
        scratch_shapes=[pltpu.VMEM((k, D_MODEL), BF16)],
        compiler_params=_params(("arbitrary",)),
        name=name,
    )(*args)


FFN_MM_ROWS = 512
FFN_ROWS = 64
FFN_PAD = 8


def _ffn_kernel(x_ref, g_ref, mod_ref, wg_ref, wv_ref, cwg_ref, cwv_ref, cbg_ref, cbv_ref,
                wd_ref, o_ref, h_ref, u_ref, act_ref, wup_ref, wdn_ref, *, tile0):
    i = tile0 + pl.program_id(0)
    c = pl.program_id(1)
    fc = FF_CHUNK
    n = TOK_TILE // FFN_MM_ROWS
    seq_len = jnp.where(i < N_PROMPT // TOK_TILE, SEQ, DEC_SEQ)
    row = lax.broadcasted_iota(jnp.int32, (FFN_ROWS, 1), 0)
    taps = lambda cw_ref, cb_ref: [jnp.broadcast_to(cw_ref[j:j + 1, :], (FFN_ROWS, fc))
                                   for j in range(3)] + [
                                       jnp.broadcast_to(cb_ref[...], (FFN_ROWS, fc))]
    taps_g, taps_v = taps(cwg_ref, cbg_ref), taps(cwv_ref, cbv_ref)

    def up(u_ref, t):
        r0 = t * FFN_MM_ROWS
        u_ref[FFN_PAD + r0:FFN_PAD + r0 + FFN_MM_ROWS, :] = _bdot(
            h_ref[r0:r0 + FFN_MM_ROWS, :], wup_ref[...])

    def conv_act(u_ref, t):
        for r0 in range(t * FFN_MM_ROWS, (t + 1) * FFN_MM_ROWS, FFN_ROWS):
            halves = []
            for lo, (w0, w1, w2, bias) in ((0, taps_g), (fc, taps_v)):
                p0 = FFN_PAD + r0
                prev = u_ref[p0 - 1:p0 - 1 + FFN_ROWS, lo:lo + fc]
                mid = u_ref[p0:p0 + FFN_ROWS, lo:lo + fc]
                nxt = u_ref[p0 + 1:p0 + 1 + FFN_ROWS, lo:lo + fc]
                if r0 % SEQ == 0:
                    prev = jnp.where(((r0 + row) & (seq_len - 1)) == 0, 0.0, prev)
                if (r0 + FFN_ROWS) % SEQ == 0:
                    nxt = jnp.where(((r0 + row) & (seq_len - 1)) == seq_len - 1, 0.0, nxt)
                halves.append(prev * w0 + mid * w1 + nxt * w2 + bias)
            act_ref[r0:r0 + FFN_ROWS, :] = (_silu(halves[0]) * halves[1]).astype(BF16)

    def down(t):
        r0 = t * FFN_MM_ROWS
        o_ref[r0:r0 + FFN_MM_ROWS, :] += _bdot(act_ref[r0:r0 + FFN_MM_ROWS, :], wdn_ref[...])

    @pl.when(c == 0)
    def _():
        _norm_mod_rows(x_ref, g_ref, mod_ref, h_ref, 3, 4)
        zeros = jnp.zeros((FFN_PAD, 2 * fc), F32)
        u_ref[0:FFN_PAD, :] = zeros
        u_ref[FFN_PAD + TOK_TILE:, :] = zeros
        o_ref[...] = jnp.zeros_like(o_ref)

    wup_ref[:, :fc] = wg_ref[...].astype(BF16)
    wup_ref[:, fc:] = wv_ref[...].astype(BF16)
    wdn_ref[...] = wd_ref[...].astype(BF16)
    for s in range(n + 2):
        if s < n:
            up(u_ref, s)
        if 1 <= s <= n:
            conv_act(u_ref, s - 1)
        if s >= 2:
            down(s - 2)

    @pl.when(c == N_FF_CHUNKS - 1)
    def _():
        o_ref[...] = x_ref[...] + mod_ref[0, 5:6, :] * o_ref[...]


def _ffn(x, g, mods, layer, w_up, conv_w, conv_b, w_down, tile0=0, n_tiles=N_TOK_TILES,
         x_tile0=None):
    fc = FF_CHUNK
    ncb = N_FF_CHUNKS
    x_tile0 = tile0 if x_tile0 is None else x_tile0
    up_c = cv_c = lambda c: c
    u_buf = pltpu.VMEM((TOK_TILE + 2 * FFN_PAD, 2 * fc), F32)
    return pl.pallas_call(
        functools.partial(_ffn_kernel, tile0=tile0),
        grid=(n_tiles, ncb),
        in_specs=[pl.BlockSpec((TOK_TILE, D_MODEL), lambda i, c: (x_tile0 + i, 0)),
                  pl.BlockSpec((1, D_MODEL), lambda i, c: (0, 0)),
                  pl.BlockSpec((1, 6, D_MODEL), lambda i, c: (_mod_group_of_tile(tile0 + i), 0, 0)),
                  pl.BlockSpec((None, D_MODEL, fc), lambda i, c: (layer, 0, up_c(c))),
                  pl.BlockSpec((None, D_MODEL, fc), lambda i, c: (layer, 0, ncb + up_c(c))),
                  pl.BlockSpec((None, 3, fc), lambda i, c: (layer, 0, cv_c(c))),
                  pl.BlockSpec((None, 3, fc), lambda i, c: (layer, 0, ncb + cv_c(c))),
                  pl.BlockSpec((None, 1, fc), lambda i, c: (layer, 0, cv_c(c))),
                  pl.BlockSpec((None, 1, fc), lambda i, c: (layer, 0, ncb + cv_c(c))),
                  pl.BlockSpec((None, fc, D_MODEL), lambda i, c: (layer, cv_c(c), 0))],
        out_specs=pl.BlockSpec((TOK_TILE, D_MODEL), lambda i, c: (i, 0)),
        out_shape=jax.ShapeDtypeStruct((n_tiles * TOK_TILE, D_MODEL), F32),
        scratch_shapes=[pltpu.VMEM((TOK_TILE, D_MODEL), BF16),
                        u_buf,
                        pltpu.VMEM((TOK_TILE, fc), BF16),
                        pltpu.VMEM((D_MODEL, 2 * fc), BF16),
                        pltpu.VMEM((fc, D_MODEL), BF16)],
        compiler_params=_params(("arbitrary", "arbitrary")),
        name="conv_ffn",
    )(x, g.reshape(1, D_MODEL), mods, w_up, w_up, conv_w, conv_w,
      conv_b.reshape(DEPTH, 1, -1), conv_b.reshape(DEPTH, 1, -1), w_down)


NA_HB = 4
NA_LANES = NA_HB * NA_HD
NA_ROWS = DEC_SEQ // GRID_W
NA_KEYS = NA_WIN_R * GRID_W
PROMPT_SEQS = 4
NA_ROW_UNROLL = 4


def _store_heads(dst_ref, seq, x, n_heads, hd):
    for h in range(n_heads):
        dst_ref[seq, h] = x[:, h * hd:(h + 1) * hd]


def _residual_begin(x1_ref, j):
    @pl.when(j == 0)
    def _():
        x1_ref[...] = jnp.zeros_like(x1_ref)


def _residual_end(x1_ref, x_ref, mod_ref, j, n_j):
    @pl.when(j == n_j - 1)
    def _():
        x1_ref[...] = x_ref[...] + mod_ref[0, 2:3, :] * x1_ref[...]


def _na_prompt_kernel(q_ref, k_ref, v_ref, gq_ref, gk_ref, x_ref, mod_ref, wo_ref,
                      x1_ref, kn_ref, vn_ref):
    scale = NA_HD ** -0.5 * LOG2E
    j = pl.program_id(1)
    _residual_begin(x1_ref, j)
    wo = wo_ref[...].astype(BF16)

    def chains(bound):
        shift = None if bound is None else _stacked_rows(bound, SEQ)
        for seq in range(PROMPT_SEQS):
            sl = slice(seq * SEQ, (seq + 1) * SEQ)
            q = _group_rms(q_ref[sl, :], gq_ref[...], NA_HD, sums_on_mxu=True) * scale
            k = _group_rms(k_ref[sl, :], gk_ref[...], NA_HD, sums_on_mxu=True)
            v = v_ref[sl, :]
            _store_heads(kn_ref, seq, k, NA_HB, NA_HD)
            _store_heads(vn_ref, seq, v, NA_HB, NA_HD)
            q4 = _stack_heads(q, NA_HB, NA_HD).astype(BF16)
            s = lax.dot_general(q4, k.astype(BF16), _NT, preferred_element_type=F32)
            (p,), inv = _softmax_parts([s], shift)
            o4 = _bdot(p.astype(BF16), v.astype(BF16)) * inv
            x1_ref[sl, :] += _bdot(_unstack_heads(o4, NA_HB, NA_HD).astype(BF16), wo)

    _with_score_bound(scale * _rms_norm_bound(gq_ref, NA_HD, NA_HD)
                      * _rms_norm_bound(gk_ref, NA_HD, NA_HD), chains)
    _residual_end(x1_ref, x_ref, mod_ref, j, NA_HEADS // NA_HB)


def _na_prompt(qkv, gq, gk, x, mods, w_o):
    nb = NA_HEADS // NA_HB
    rows = PROMPT_SEQS * SEQ
    blk = lambda off: pl.BlockSpec((rows, NA_LANES), lambda b, j: (b, off + j))
    vec = pl.BlockSpec((1, NA_LANES), lambda b, j: (0, 0))
    x_spec = pl.BlockSpec((rows, D_MODEL), lambda b, j: (b, 0))
    cache = pl.BlockSpec((PROMPT_SEQS, NA_HB, SEQ, NA_HD), lambda b, j: (b, j, 0, 0))
    cache_shape = jax.ShapeDtypeStruct((BATCH, NA_HEADS, SEQ, NA_HD), F32)
    return pl.pallas_call(
        _na_prompt_kernel,
        grid=(BATCH // PROMPT_SEQS, nb),
        in_specs=[blk(0), blk(nb), blk(2 * nb), vec, vec, x_spec,
                  pl.BlockSpec((1, 6, D_MODEL), lambda b, j: (0, 0, 0)),
                  pl.BlockSpec((NA_LANES, D_MODEL), lambda b, j: (j, 0))],
        out_specs=[x_spec, cache, cache],
        out_shape=[jax.ShapeDtypeStruct((N_PROMPT, D_MODEL), F32), cache_shape, cache_shape],
        compiler_params=_params(("arbitrary", "arbitrary")),
        name="na_prompt",
    )(qkv, qkv, qkv, jnp.tile(gq, NA_HB).reshape(1, -1), jnp.tile(gk, NA_HB).reshape(1, -1),
      x, mods, w_o)


def _na_latent_kernel(q_ref, k_ref, v_ref, kc_ref, vc_ref, t_ref, gq_ref, gk_ref, x_ref, mod_ref,
                      wo_ref, x1_ref, qn_ref, kn_ref, vb_ref, kc4_ref, vc4_ref, bias_ref, o_ref):
    scale = NA_HD ** -0.5 * LOG2E
    rows = 256
    j = pl.program_id(1)
    _residual_begin(x1_ref, j)

    def prep(r, carry):
        sl = pl.ds(pl.multiple_of(r * rows, rows), rows)
        qn_ref[sl, :] = (_group_rms(q_ref[sl, :], gq_ref[...], NA_HD, sums_on_mxu=True)
                         * scale).astype(BF16)
        kn_ref[sl, :] = _group_rms(k_ref[sl, :], gk_ref[...], NA_HD,
                                   sums_on_mxu=True).astype(BF16)
        vb_ref[sl, :] = v_ref[sl, :].astype(BF16)
        return carry

    lax.fori_loop(0, DEC_SEQ // rows, prep, 0)
    kc = kc_ref[0]
    kc4_ref[...] = kc.astype(BF16)
    vc4_ref[...] = vc_ref[0].astype(BF16)

    def attend(bound):
        for h in range(NA_HB):
            off = 0.0 if bound is None else bound[:, h:h + 1]
            for p in range(NA_WIN_R):
                for i in range(NA_WIN_R):
                    bias_ref[h, p, :, i * GRID_W:(i + 1) * GRID_W] = t_ref[h, p + i] - off
        shift = None if bound is None else _stacked_rows(bound, GRID_W)

        def row(r):
            kr0 = jnp.clip(r - NA_WIN_R // 2, 0, NA_ROWS - NA_WIN_R)
            pat = kr0 - r + NA_WIN_R - 1
            qs = pl.ds(pl.multiple_of(r * GRID_W, GRID_W), GRID_W)
            ks = pl.ds(pl.multiple_of(kr0 * GRID_W, GRID_W), NA_KEYS)
            q4 = _stack_heads(qn_ref[qs, :], NA_HB, NA_HD)
            s_loc = lax.dot_general(q4, kn_ref[ks, :], _NT, preferred_element_type=F32)
            s_loc = s_loc + jnp.concatenate([bias_ref[h, pat] for h in range(NA_HB)], axis=0)
            s_ctx = lax.dot_general(q4, kc4_ref[...], _NT, preferred_element_type=F32)
            if shift is None:
                (p_loc, p_ctx), inv = _softmax_parts([s_loc, s_ctx])
            else:
                p_loc, p_ctx = jnp.exp2(s_loc), jnp.exp2(s_ctx - shift)
                inv = 1.0 / (p_loc.sum(axis=-1, keepdims=True) + p_ctx.sum(axis=-1, keepdims=True))
            o4 = _bdot(p_loc.astype(BF16), vb_ref[ks, :]) + _bdot(p_ctx.astype(BF16), vc4_ref[...])
            o_ref[qs, :] = _unstack_heads(o4 * inv, NA_HB, NA_HD).astype(BF16)

        def rows_step(i, carry):
            for u in range(NA_ROW_UNROLL):
                row(i * NA_ROW_UNROLL + u)
            return carry

        lax.fori_loop(0, NA_ROWS // NA_ROW_UNROLL, rows_step, 0)

    qmax = scale * _rms_norm_bound(gq_ref, NA_HD, NA_HD)
    bias_max = jnp.concatenate(
        [t_ref[h].max(axis=0).max(axis=0, keepdims=True).max(axis=1, keepdims=True)
         for h in range(NA_HB)], axis=-1)
    _with_score_bound(
        jnp.maximum(qmax * _rms_norm_bound(gk_ref, NA_HD, NA_HD) + bias_max,
                    qmax * _max_head_norms(kc, NA_HD)), attend)
    x1_ref[...] += _bdot(o_ref[...], wo_ref[...].astype(BF16))
    _residual_end(x1_ref, x_ref, mod_ref, j, NA_HEADS // NA_HB)


def _na_bias_blocks(bias_table):
    qc = jnp.arange(GRID_W)[:, None]
    kc = jnp.arange(GRID_W)[None, :]
    win0 = jnp.clip(qc - NA_WIN_C // 2, 0, GRID_W - NA_WIN_C)
    valid = (kc >= win0) & (kc < win0 + NA_WIN_C)
    n_co = bias_table.shape[-1]
    onehot = (kc - qc + NA_WIN_C - 1)[None] == jnp.arange(n_co)[:, None, None]
    t = jnp.einsum('hrd,dqk->hrqk', bias_table.astype(F32), onehot.astype(F32),
                   precision=lax.Precision.HIGHEST)
    return jnp.where(valid, t * LOG2E, NEG)


def _na_latent(qkv, cache_k, cache_v, bias_blocks, gq, gk, x, mods, w_o):
    nb = NA_HEADS // NA_HB
    lat0 = N_PROMPT // DEC_SEQ
    blk = lambda off: pl.BlockSpec((DEC_SEQ, NA_LANES), lambda b, j: (lat0 + b, off + j))
    vec = pl.BlockSpec((1, NA_LANES), lambda b, j: (0, 0))
    cache = pl.BlockSpec((1, PAST_LEN, NA_LANES), lambda b, j: (b, 0, j))
    return pl.pallas_call(
        _na_latent_kernel,
        grid=(DEC_BATCH, nb),
        in_specs=[blk(0), blk(nb), blk(2 * nb), cache, cache,
                  pl.BlockSpec((NA_HB, 2 * NA_WIN_R - 1, GRID_W, GRID_W), lambda b, j: (j, 0, 0, 0)),
                  vec, vec,
                  pl.BlockSpec((DEC_SEQ, D_MODEL), lambda b, j: (b, 0), pipeline_mode=pl.Buffered(1)),
                  pl.BlockSpec((1, 6, D_MODEL), lambda b, j: (1 + b, 0, 0)),
                  pl.BlockSpec((NA_LANES, D_MODEL), lambda b, j: (j, 0))],
        out_specs=pl.BlockSpec((DEC_SEQ, D_MODEL), lambda b, j: (b, 0)),
        out_shape=jax.ShapeDtypeStruct((N_LATENT, D_MODEL), F32),
        scratch_shapes=[pltpu.VMEM((DEC_SEQ, NA_LANES), BF16),
                        pltpu.VMEM((DEC_SEQ, NA_LANES), BF16),
                        pltpu.VMEM((DEC_SEQ, NA_LANES), BF16),
                        pltpu.VMEM((PAST_LEN, NA_LANES), BF16),
                        pltpu.VMEM((PAST_LEN, NA_LANES), BF16),
                        pltpu.VMEM((NA_HB, NA_WIN_R, GRID_W, NA_KEYS), F32),
                        pltpu.VMEM((DEC_SEQ, NA_LANES), BF16)],
        compiler_params=_params(("arbitrary", "arbitrary")),
        name="na_latent",
    )(qkv, qkv, qkv, _tokens_first(cache_k), _tokens_first(cache_v), bias_blocks,
      jnp.tile(gq, NA_HB).reshape(1, -1), jnp.tile(gk, NA_HB).reshape(1, -1), x, mods, w_o)


GLA_C = 128
GLA_HPS = 2
GLA_SUB = 8
GLA_LEVELS = (64, 32, 16, 8)


def _split_hi_lo(x):
    hi = x.astype(BF16)
    lo = (x - hi.astype(F32)).astype(BF16)
    return jnp.concatenate([hi, lo], axis=1)


class _GlaMasks(NamedTuple):
    tri: jax.Array
    later: tuple
    sign: tuple
    pair: tuple
    diag: jax.Array


def _gla_masks(rev):
    c = GLA_C
    row = lax.broadcasted_iota(jnp.int32, (c, c), 0)
    col = lax.broadcasted_iota(jnp.int32, (c, c), 1)
    rid = lax.broadcasted_iota(jnp.int32, (c, GLA_DK), 0)
    causal = (col >= row) if rev else (col <= row)
    later, pair = [], []
    for m in GLA_LEVELS:
        later.append(((rid & m) == 0) if rev else ((rid & m) != 0))
        same = (row >> _log2(2 * m)) == (col >> _log2(2 * m))
        crossing = ((row & m) != (col & m))
        pair.append(same & crossing & causal)
    diag = ((row >> _log2(GLA_SUB)) == (col >> _log2(GLA_SUB))) & causal
    sign = tuple(jnp.where(l, 1.0, -1.0) for l in later)
    return _GlaMasks(jnp.where(causal, 1.0, 0.0).astype(BF16), tuple(later), sign, tuple(pair), diag)


def _gla_chunk(q, k, v, g, st_ref, rev, masks):
    c = GLA_C
    cs = _bdot(masks.tri, _split_hi_lo(g))
    b = cs[:, :GLA_DK] + cs[:, GLA_DK:]

    a = None
    for m, later, sign, pair in zip(GLA_LEVELS, masks.later, masks.sign, masks.pair):
        nblk = c // (2 * m)
        if rev:
            bnd = [b[j * 2 * m + m:j * 2 * m + m + 1] for j in range(nblk)]
        else:
            bnd = [b[j * 2 * m + m - 1:j * 2 * m + m] for j in range(nblk)]
        ref = jnp.concatenate([jnp.broadcast_to(x, (2 * m, GLA_DK)) for x in bnd], axis=0)
        x = (jnp.where(later, q, k) * jnp.exp((b - ref) * sign)).astype(BF16)
        blk = lax.dot_general(x, x, _NT, preferred_element_type=F32)
        a = jnp.where(pair, blk, 0.0 if a is None else a)

    nsub = c // GLA_SUB
    lane_c = lax.broadcasted_iota(jnp.int32, (GLA_SUB, c), 1)
    diag_rows = []
    for blk_i in range(nsub):
        r0 = blk_i * GLA_SUB
        qb = q[r0:r0 + GLA_SUB]
        bb = b[r0:r0 + GLA_SUB]
        acc = jnp.zeros((GLA_SUB, c), F32)
        for s in range(GLA_SUB):
            ks = k[r0 + s:r0 + s + 1]
            bs = b[r0 + s:r0 + s + 1]
            w = jnp.sum(qb * ks * jnp.exp(jnp.minimum(bb - bs, 0.0)), axis=-1, keepdims=True)
            acc = jnp.where(lane_c == r0 + s, w, acc)
        diag_rows.append(acc)
    a = jnp.where(masks.diag, jnp.concatenate(diag_rows, axis=0), a)

    st = st_ref[...]
    inter = lax.dot_general((q * jnp.exp(b)).astype(BF16), st.astype(BF16), _NT,
                            preferred_element_type=F32)
    o = inter + _bdot(a.astype(BF16), v.astype(BF16))

    btot = b[0:1] if rev else b[c - 1:c]
    kd = (k * jnp.exp(btot - b)).astype(BF16)
    st_ref[...] = st * jnp.exp(btot) + lax.dot_general(v.astype(BF16), kd, _TN,
                                                       preferred_element_type=F32)
    return o


def _gla_kernel(*refs, n_tok, has_state):
    if has_state:
        (q_ref, k_ref, v_ref, r_ref, w2_ref, bg_ref, gn_ref, s0f_ref, s0b_ref,
         o_ref, lg_ref, of_ref, ob_ref, stf_ref, stb_ref) = refs
    else:
        (q_ref, k_ref, v_ref, r_ref, w2_ref, bg_ref, gn_ref,
         o_ref, sf_ref, sb_ref, lg_ref, of_ref, ob_ref, stf_ref, stb_ref) = refs
    nc = n_tok // GLA_C
    scale = GLA_DK ** -0.5

    rb = r_ref[...].astype(BF16)
    for z in range(2):
        x = _bdot(rb, w2_ref[z].astype(BF16)) + bg_ref[z]
        lg_ref[z] = (jnp.minimum(x, 0.0) - jnp.log1p(jnp.exp(-jnp.abs(x)))) * (1.0 / GLA_GATE_NORM)

    for hh in range(GLA_HPS):
        if has_state:
            stf_ref[hh] = s0f_ref[0, hh].T
            stb_ref[hh] = s0b_ref[0, hh].T
        else:
            stf_ref[hh] = jnp.zeros((GLA_DV, GLA_DK), F32)
            stb_ref[hh] = jnp.zeros((GLA_DV, GLA_DK), F32)

    masks = {rev: _gla_masks(rev) for rev in (False, True)}

    def step(ci, carry):
        for hh in range(GLA_HPS):
            kq = slice(hh * GLA_DK, (hh + 1) * GLA_DK)
            vv = slice(hh * GLA_DV, (hh + 1) * GLA_DV)
            for rev in (False, True):
                cc = (nc - 1 - ci) if rev else ci
                sl = pl.ds(pl.multiple_of(cc * GLA_C, GLA_C), GLA_C)
                o = _gla_chunk(q_ref[sl, kq] * scale, k_ref[sl, kq], v_ref[sl, vv],
                               lg_ref[1 if rev else 0, sl, kq],
                               (stb_ref if rev else stf_ref).at[hh], rev, masks[rev])
                (ob_ref if rev else of_ref)[sl, vv] = o
        return carry

    lax.fori_loop(0, nc, step, 0)

    for hh in range(GLA_HPS):
        vv = slice(hh * GLA_DV, (hh + 1) * GLA_DV)
        o = of_ref[:, vv] + ob_ref[:, vv]
        o_ref[:, vv] = o * lax.rsqrt(jnp.mean(o * o, axis=-1, keepdims=True) + EPS) * gn_ref[...]
        if not has_state:
            sf_ref[0, hh] = stf_ref[hh].T
            sb_ref[0, hh] = stb_ref[hh].T


def _gla(proj, w2, bg, gnorm, n_seq, n_tok, row_block0, states=None):
    kw, vw = GLA_HPS * GLA_DK, GLA_HPS * GLA_DV
    spec = lambda width, off: pl.BlockSpec((n_tok, width), lambda b, h: (row_block0 + b, off + h))
    in_specs = [spec(kw, 0), spec(kw, GLA_HK // kw), spec(vw, 2 * GLA_HK // vw),
                pl.BlockSpec((n_tok, 128), lambda b, h: (row_block0 + b, (2 * GLA_HK + 2 * GLA_HV) // 128)),
                pl.BlockSpec((2, 128, kw), lambda b, h: (0, 0, h)),
                pl.BlockSpec((2, 1, kw), lambda b, h: (0, 0, h)),
                pl.BlockSpec((1, GLA_DV), lambda b, h: (0, 0))]
    args = [proj, proj, proj, proj, w2, bg, gnorm.reshape(1, GLA_DV)]
    st_spec = pl.BlockSpec((1, GLA_HPS, GLA_DK, GLA_DV), lambda b, h: (b, h, 0, 0))
    o_spec = pl.BlockSpec((n_tok, vw), lambda b, h: (b, h))
    o_shape = jax.ShapeDtypeStruct((n_seq * n_tok, GLA_HV), F32)
    if states is not None:
        in_specs += [st_spec, st_spec]
        args += list(states)
        out_specs, out_shape = o_spec, o_shape
    else:
        st_shape = jax.ShapeDtypeStruct((n_seq, GLA_HEADS, GLA_DK, GLA_DV), F32)
        out_specs, out_shape = [o_spec, st_spec, st_spec], [o_shape, st_shape, st_shape]
    return pl.pallas_call(
        functools.partial(_gla_kernel, n_tok=n_tok, has_state=states is not None),
        grid=(n_seq, GLA_HEADS // GLA_HPS),
        in_specs=in_specs,
        out_specs=out_specs,
        out_shape=out_shape,
        scratch_shapes=[pltpu.VMEM((2, n_tok, kw), F32),
                        pltpu.VMEM((n_tok, vw), F32),
                        pltpu.VMEM((n_tok, vw), F32),
                        pltpu.VMEM((GLA_HPS, GLA_DV, GLA_DK), F32),
                        pltpu.VMEM((GLA_HPS, GLA_DV, GLA_DK), F32)],
        compiler_params=_params(("arbitrary", "arbitrary")),
        name="gla_latent" if states is not None else "gla_prompt",
    )(*args)


DIFF_HB = 2
DIFF_QL = DIFF_HB * DIFF_HD
DIFF_VL = DIFF_HB * 2 * DIFF_HD
DIFF_TQ = 512
DIFF_SUB = 128


def _diff_lambda(lam_ref):
    l = lam_ref[...]
    a = jnp.sum(l[0:1] * l[1:2], axis=-1, keepdims=True)
    b = jnp.sum(l[2:3] * l[3:4], axis=-1, keepdims=True)
    return jnp.exp(a) - jnp.exp(b) + DIFF_LAMBDA_INIT


def _diff_finish(ps, invs, lam, v, sn_ref):
    a = ps[0] * invs[0] - (lam * invs[1]) * ps[1]
    o2 = _bdot(a.astype(BF16), v)
    o = _unstack_heads(o2, DIFF_HB, 2 * DIFF_HD)
    return _group_rms(o, sn_ref[...], 2 * DIFF_HD) * (1.0 - DIFF_LAMBDA_INIT)


def _diff_prompt_kernel(q0_ref, q1_ref, k0_ref, k1_ref, v_ref, gq_ref, gk_ref, lam_ref, sn_ref,
                        o_ref, kn_ref, vn_ref):
    scale = DIFF_HD ** -0.5 * LOG2E
    lam = _diff_lambda(lam_ref)

    def chains(bound):
        shift = None if bound is None else _stacked_rows(bound, SEQ)
        for seq in range(PROMPT_SEQS):
            sl = slice(seq * SEQ, (seq + 1) * SEQ)
            ps, invs = [], []
            for comp, (q_ref, k_ref) in enumerate(((q0_ref, k0_ref), (q1_ref, k1_ref))):
                q = _group_rms(q_ref[sl, :], gq_ref[...], DIFF_HD, sums_on_mxu=True) * scale
                k = _group_rms(k_ref[sl, :], gk_ref[...], DIFF_HD, sums_on_mxu=True)
                for h in range(DIFF_HB):
                    kn_ref[seq, comp, h] = k[:, h * DIFF_HD:(h + 1) * DIFF_HD]
                q2 = _stack_heads(q, DIFF_HB, DIFF_HD).astype(BF16)
                s = lax.dot_general(q2, k.astype(BF16), _NT, preferred_element_type=F32)
                (p,), inv = _softmax_parts([s], shift)
                ps.append(p)
                invs.append(inv)
            v = v_ref[sl, :]
            _store_heads(vn_ref, seq, v, DIFF_HB, 2 * DIFF_HD)
            o_ref[sl, :] = _diff_finish(ps, invs, lam, v.astype(BF16), sn_ref)

    _with_score_bound(scale * _rms_norm_bound(gq_ref, DIFF_HD, DIFF_HD)
                      * _rms_norm_bound(gk_ref, DIFF_HD, DIFF_HD), chains)


def _diff_prompt(qkv, gq, gk, lam, sub_norm):
    nb = DIFF_HEADS // DIFF_HB
    rows = PROMPT_SEQS * SEQ
    qk = lambda off: pl.BlockSpec((rows, DIFF_QL), lambda b, j: (b, off + j))
    vec = lambda n: pl.BlockSpec((1, n), lambda b, j: (0, 0))
    v_spec = pl.BlockSpec((rows, DIFF_VL), lambda b, j: (b, 2 * D_MODEL // DIFF_VL + j))
    kn_spec = pl.BlockSpec((PROMPT_SEQS, 2, DIFF_HB, SEQ, DIFF_HD), lambda b, j: (b, 0, j, 0, 0))
    vn_spec = pl.BlockSpec((PROMPT_SEQS, DIFF_HB, SEQ, 2 * DIFF_HD), lambda b, j: (b, j, 0, 0))
    return pl.pallas_call(
        _diff_prompt_kernel,
        grid=(BATCH // PROMPT_SEQS, nb),
        in_specs=[qk(0), qk(nb), qk(2 * nb), qk(3 * nb), v_spec, vec(DIFF_QL), vec(DIFF_QL),
                  pl.BlockSpec((4, DIFF_HD), lambda b, j: (0, 0)), vec(DIFF_VL)],
        out_specs=[pl.BlockSpec((rows, DIFF_VL), lambda b, j: (b, j)), kn_spec, vn_spec],
        out_shape=[jax.ShapeDtypeStruct((N_PROMPT, D_MODEL), F32),
                   jax.ShapeDtypeStruct((BATCH, 2, DIFF_HEADS, SEQ, DIFF_HD), F32),
                   jax.ShapeDtypeStruct((BATCH, DIFF_HEADS, SEQ, 2 * DIFF_HD), F32)],
        compiler_params=_params(("arbitrary", "arbitrary")),
        name="diff_prompt",
    )(qkv, qkv, qkv, qkv, qkv, jnp.tile(gq, DIFF_HB).reshape(1, -1),
      jnp.tile(gk, DIFF_HB).reshape(1, -1), lam, jnp.tile(sub_norm, DIFF_HB).reshape(1, -1))


def _diff_latent_kernel(q0_ref, q1_ref, k0_ref, k1_ref, v_ref, kc0_ref, kc1_ref, vc_ref,
                        cos_ref, sin_ref, cosq_ref, sinq_ref, gq_ref, gk_ref, lam_ref, sn_ref,
                        o_ref, kb0_ref, kb1_ref, vb_ref, kmax_ref):
    scale = DIFF_HD ** -0.5 * LOG2E
    half = DIFF_HD // 4
    rows = 256

    @pl.when(pl.program_id(2) == 0)
    def _():
        for comp, (k_ref, kc_ref, kb_ref) in enumerate(
                ((k0_ref, kc0_ref, kb0_ref), (k1_ref, kc1_ref, kb1_ref))):
            kc = kc_ref[0]
            kb_ref[0:PAST_LEN, :] = kc.astype(BF16)
            kmax_ref[comp:comp + 1, :] = jnp.maximum(
                _max_head_norms(kc, DIFF_HD), _rms_norm_bound(gk_ref, DIFF_HD, DIFF_HD))

            def prep(r, carry):
                sl = pl.ds(pl.multiple_of(r * rows, rows), rows)
                k = _group_rms(k_ref[sl, :], gk_ref[...], DIFF_HD)
                k = _rope(k, cos_ref[sl, :], sin_ref[sl, :], half)
                kb_ref[pl.ds(pl.multiple_of(PAST_LEN + r * rows, rows), rows), :] = k.astype(BF16)
                return carry

            lax.fori_loop(0, DEC_SEQ // rows, prep, 0)
        vb_ref[0:PAST_LEN, :] = vc_ref[0].astype(BF16)
        vb_ref[PAST_LEN:, :] = v_ref[...].astype(BF16)

    lam = _diff_lambda(lam_ref)

    def chains(bound):
        for r0 in range(0, DIFF_TQ, DIFF_SUB):
            sl = slice(r0, r0 + DIFF_SUB)
            ps, invs = [], []
            for comp, (q_ref, kb_ref) in enumerate(((q0_ref, kb0_ref), (q1_ref, kb1_ref))):
                shift = None if bound is None else _stacked_rows(
                    bound[:, comp * DIFF_HB:(comp + 1) * DIFF_HB], DIFF_SUB)
                q = _group_rms(q_ref[sl, :], gq_ref[...], DIFF_HD)
                q = _rope(q, cosq_ref[sl, :], sinq_ref[sl, :], half) * scale
                q2 = _stack_heads(q, DIFF_HB, DIFF_HD).astype(BF16)
                s = lax.dot_general(q2, kb_ref[...], _NT, preferred_element_type=F32)
                (p,), inv = _softmax_parts([s], shift)
                ps.append(p)
                invs.append(inv)
            o_ref[sl, :] = _diff_finish(ps, invs, lam, vb_ref[...], sn_ref)

    qmax = scale * _rms_norm_bound(gq_ref, DIFF_HD, DIFF_HD)
    _with_score_bound(jnp.concatenate([qmax * kmax_ref[0:1, :], qmax * kmax_ref[1:2, :]], axis=-1),
                      chains)


def _diff_latent(qkv, cache_k, cache_v, cos, sin, gq, gk, lam, sub_norm):
    nb = DIFF_HEADS // DIFF_HB
    nq = DEC_SEQ // DIFF_TQ
    q0 = N_PROMPT // DIFF_TQ
    lat0 = N_PROMPT // DEC_SEQ
    n_keys = PAST_LEN + DEC_SEQ
    q_spec = lambda off: pl.BlockSpec((DIFF_TQ, DIFF_QL), lambda b, j, t: (q0 + b * nq + t, off + j))
    k_spec = lambda off: pl.BlockSpec((DEC_SEQ, DIFF_QL), lambda b, j, t: (lat0 + b, off + j))
    v_spec = pl.BlockSpec((DEC_SEQ, DIFF_VL), lambda b, j, t: (lat0 + b, 2 * D_MODEL // DIFF_VL + j))
    kc_spec = lambda off: pl.BlockSpec((1, PAST_LEN, DIFF_QL), lambda b, j, t: (b, 0, off + j))
    vc_spec = pl.BlockSpec((1, PAST_LEN, DIFF_VL), lambda b, j, t: (b, 0, j))
    tab = pl.BlockSpec((DEC_SEQ, DIFF_QL), lambda b, j, t: (0, 0))
    tabq = pl.BlockSpec((DIFF_TQ, DIFF_QL), lambda b, j, t: (t, 0))
    vec = lambda n: pl.BlockSpec((1, n), lambda b, j, t: (0, 0))
    return pl.pallas_call(
        _diff_latent_kernel,
        grid=(DEC_BATCH, nb, nq),
        in_specs=[q_spec(0), q_spec(nb), k_spec(2 * nb), k_spec(3 * nb), v_spec,
                  kc_spec(0), kc_spec(nb), vc_spec, tab, tab, tabq, tabq,
                  vec(DIFF_QL), vec(DIFF_QL),
                  pl.BlockSpec((4, DIFF_HD), lambda b, j, t: (0, 0)), vec(DIFF_VL)],
        out_specs=pl.BlockSpec((DIFF_TQ, DIFF_VL), lambda b, j, t: (b * nq + t, j)),
        out_shape=jax.ShapeDtypeStruct((N_LATENT, D_MODEL), F32),
        scratch_shapes=[pltpu.VMEM((n_keys, DIFF_QL), BF16),
                        pltpu.VMEM((n_keys, DIFF_QL), BF16),
                        pltpu.VMEM((n_keys, DIFF_VL), BF16),
                        pltpu.VMEM((2, DIFF_HB), F32)],
        compiler_params=_params(("arbitrary", "arbitrary", "arbitrary")),
        name="diff_latent",
    )(qkv, qkv, qkv, qkv, qkv, _tokens_first(cache_k), _tokens_first(cache_k),
      _tokens_first(cache_v), cos, sin, cos, sin,
      jnp.tile(gq, DIFF_HB).reshape(1, -1), jnp.tile(gk, DIFF_HB).reshape(1, -1), lam,
      jnp.tile(sub_norm, DIFF_HB).reshape(1, -1))


MLA_HB = 2
MLA_HL = 128
MLA_LANES = MLA_HB * MLA_HL
MLA_TQ = 512
MLA_SUB = 128


def _mla_keys(kv, kr, gk, sums_on_mxu=False):
    lane = lax.broadcasted_iota(jnp.int32, kv.shape, 1)
    kr2 = jnp.concatenate([kr] * MLA_HB, axis=1)
    k = jnp.where((lane & (MLA_HL - 1)) < MLA_NOPE, kv, kr2)
    return _group_rms(k, gk, MLA_HL, n_real=MLA_QK, sums_on_mxu=sums_on_mxu)


def _mla_out(o2):
    tq = o2.shape[0] // MLA_HB
    oa = pltpu.roll(o2[0:tq, 0:MLA_HL], MLA_HL - MLA_V, axis=1)
    ob = o2[tq:, MLA_HL:]
    lane = lax.broadcasted_iota(jnp.int32, oa.shape, 1)
    return jnp.where(lane < MLA_V, oa, ob)


def _mla_prompt_kernel(q_ref, kv_ref, kr_ref, gq_ref, gk_ref, o_ref):
    scale = MLA_QK ** -0.5 * LOG2E

    def chains(bound):
        shift = None if bound is None else _stacked_rows(bound, SEQ)
        for seq in range(PROMPT_SEQS):
            sl = slice(seq * SEQ, (seq + 1) * SEQ)
            q = _group_rms(q_ref[sl, :], gq_ref[...], MLA_HL, n_real=MLA_QK) * scale
            kv = kv_ref[sl, :]
            k = _mla_keys(kv, kr_ref[sl, :], gk_ref[...])
            q2 = _stack_heads(q, MLA_HB, MLA_HL).astype(BF16)
            s = lax.dot_general(q2, k.astype(BF16), _NT, preferred_element_type=F32)
            (p,), inv = _softmax_parts([s], shift)
            o_ref[sl, :] = _mla_out(_bdot(p.astype(BF16), kv.astype(BF16)) * inv)

    _with_score_bound(scale * _rms_norm_bound(gq_ref, MLA_QK, MLA_HL)
                      * _rms_norm_bound(gk_ref, MLA_QK, MLA_HL), chains)


def _mla_prompt(qp, kvp, low, gq, gk):
    nb = MLA_HEADS // MLA_HB
    rows = PROMPT_SEQS * SEQ
    blk = pl.BlockSpec((rows, MLA_LANES), lambda b, j: (b, j))
    vec = pl.BlockSpec((1, MLA_LANES), lambda b, j: (0, 0))
    return pl.pallas_call(
        _mla_prompt_kernel,
        grid=(BATCH // PROMPT_SEQS, nb),
        in_specs=[blk, blk, pl.BlockSpec((rows, MLA_HL), lambda b, j: (b, MLA_LOW_KR // MLA_HL)),
                  vec, vec],
        out_specs=pl.BlockSpec((rows, MLA_HB * MLA_V), lambda b, j: (b, j)),
        out_shape=jax.ShapeDtypeStruct((N_PROMPT, MLA_HEADS * MLA_V), F32),
        compiler_params=_params(("arbitrary", "arbitrary")),
        name="mla_prompt",
    )(qp, kvp, low, gq, gk)


def _mla_latent_kernel(q_ref, kv_ref, kr_ref, kvc_ref, krc_ref, cos_ref, sin_ref, cosq_ref,
                       sinq_ref, gq_ref, gk_ref, o_ref, kb_ref, vb_ref, kmax_ref):
    scale = MLA_QK ** -0.5 * LOG2E
    half = MLA_ROPE // 4
    rows = 256

    @pl.when(pl.program_id(2) == 0)
    def _():
        kvc = kvc_ref[...]
        kc = _mla_keys(kvc, krc_ref[...], gk_ref[...])
        kb_ref[0:PAST_LEN, :] = kc.astype(BF16)
        vb_ref[0:PAST_LEN, :] = kvc.astype(BF16)

        def prep(r, carry):
            sl = pl.ds(pl.multiple_of(r * rows, rows), rows)
            dst = pl.ds(pl.multiple_of(PAST_LEN + r * rows, rows), rows)
            kv = kv_ref[sl, :]
            k = _mla_keys(kv, kr_ref[sl, :], gk_ref[...], sums_on_mxu=True)
            k = _rope(k, cos_ref[sl, :], sin_ref[sl, :], half)
            kb_ref[dst, :] = k.astype(BF16)
            vb_ref[dst, :] = kv.astype(BF16)
            return carry

        lax.fori_loop(0, DEC_SEQ // rows, prep, 0)
        kmax_ref[...] = jnp.maximum(_max_head_norms(kc, MLA_HL),
                                    _rms_norm_bound(gk_ref, MLA_QK, MLA_HL))

    def chains(bound):
        shift = None if bound is None else _stacked_rows(bound, MLA_SUB)
        for r0 in range(0, MLA_TQ, MLA_SUB):
            sl = slice(r0, r0 + MLA_SUB)
            q = _group_rms(q_ref[sl, :], gq_ref[...], MLA_HL, n_real=MLA_QK)
            q = _rope(q, cosq_ref[sl, :], sinq_ref[sl, :], half) * scale
            q2 = _stack_heads(q, MLA_HB, MLA_HL).astype(BF16)
            s = lax.dot_general(q2, kb_ref[...], _NT, preferred_element_type=F32)
            (p,), inv = _softmax_parts([s], shift)
            o_ref[sl, :] = _mla_out(_bdot(p.astype(BF16), vb_ref[...]) * inv)

    _with_score_bound(scale * _rms_norm_bound(gq_ref, MLA_QK, MLA_HL) * kmax_ref[...], chains)


def _mla_latent(qp, kvp, low, kvc, krc, cos, sin, gq, gk):
    nb = MLA_HEADS // MLA_HB
    nq = DEC_SEQ // MLA_TQ
    q0 = N_PROMPT // MLA_TQ
    lat0 = N_PROMPT // DEC_SEQ
    n_keys = PAST_LEN + DEC_SEQ
    tab = pl.BlockSpec((DEC_SEQ, MLA_LANES), lambda b, j, t: (0, 0))
    tabq = pl.BlockSpec((MLA_TQ, MLA_LANES), lambda b, j, t: (t, 0))
    vec = pl.BlockSpec((1, MLA_LANES), lambda b, j, t: (0, 0))
    return pl.pallas_call(
        _mla_latent_kernel,
        grid=(DEC_BATCH, nb, nq),
        in_specs=[pl.BlockSpec((MLA_TQ, MLA_LANES), lambda b, j, t: (q0 + b * nq + t, j)),
                  pl.BlockSpec((DEC_SEQ, MLA_LANES), lambda b, j, t: (lat0 + b, j)),
                  pl.BlockSpec((DEC_SEQ, MLA_HL), lambda b, j, t: (lat0 + b, MLA_LOW_KR // MLA_HL)),
                  pl.BlockSpec((PAST_LEN, MLA_LANES), lambda b, j, t: (b, j)),
                  pl.BlockSpec((PAST_LEN, MLA_HL), lambda b, j, t: (b, 0)),
                  tab, tab, tabq, tabq, vec, vec],
        out_specs=pl.BlockSpec((MLA_TQ, MLA_HB * MLA_V), lambda b, j, t: (b * nq + t, j)),
        out_shape=jax.ShapeDtypeStruct((N_LATENT, MLA_HEADS * MLA_V), F32),
        scratch_shapes=[pltpu.VMEM((n_keys, MLA_LANES), BF16),
                        pltpu.VMEM((n_keys, MLA_LANES), BF16),
                        pltpu.VMEM((1, MLA_HB), F32)],
        compiler_params=_params(("arbitrary", "arbitrary", "arbitrary")),
        name="mla_latent",
    )(qp, kvp, low, kvc, krc, cos, sin, cos, sin, gq, gk)


MLA_LOW_Q = 0
MLA_LOW_KV = 512
MLA_LOW_KR = 768
MLA_LOW_N = 896


def _axial_tables(n_tok, rdim):
    nf = rdim // 4
    freqs = ROPE_BASE ** (-jnp.arange(nf, dtype=F32) / nf)
    t = jnp.arange(n_tok)
    rowp = (t // GRID_W).astype(F32)
    colp = (t % GRID_W).astype(F32)
    ang = jnp.stack([rowp[:, None] * freqs, colp[:, None] * freqs], axis=1)
    cos, sin = jnp.cos(ang), jnp.sin(ang)
    cos_l = jnp.stack([cos, cos], axis=2).reshape(n_tok, rdim)
    sin_l = jnp.stack([-sin, sin], axis=2).reshape(n_tok, rdim)
    return cos_l, sin_l


def _diff_rope_tables():
    cos, sin = _axial_tables(DEC_SEQ, DIFF_HD)
    return jnp.tile(cos, (1, DIFF_HB)), jnp.tile(sin, (1, DIFF_HB))


def _mla_rope_tables():
    cos, sin = _axial_tables(DEC_SEQ, MLA_ROPE)
    ones = jnp.ones((DEC_SEQ, MLA_NOPE), F32)
    pad1 = jnp.ones((DEC_SEQ, MLA_HL - MLA_QK), F32)
    cos_h = jnp.concatenate([ones, cos, pad1], axis=1)
    sin_h = jnp.concatenate([0 * ones, sin, 0 * pad1], axis=1)
    return jnp.tile(cos_h, (1, MLA_HB)), jnp.tile(sin_h, (1, MLA_HB))


def _tokens_first(cache):
    b, h, l, d = cache.shape
    return jnp.transpose(cache, (0, 2, 1, 3)).reshape(b, l, h * d)


def _pad_heads(w, heads, hd, hl):
    k = w.shape[0]
    return jnp.pad(w.reshape(k, heads, hd), ((0, 0), (0, 0), (0, hl - hd))).reshape(k, heads * hl)


def kernel(x_prompt, x_sample, cache_l0_k, cache_l0_v, state_l1_fwd, state_l1_bwd, cache_l2_k,
           cache_l2_v, cache_l3_ckv, cache_l3_krope, c, c_ctx, ada_w, ada_b, norm_mix, norm_ffn,
           ffn_w_up, ffn_conv_w, ffn_conv_b, ffn_w_down, na_w_qkv, na_q_norm, na_k_norm, na_bias,
           na_w_o, gla_w_qkvg, gla_w_gate1, gla_w_gate2, gla_b_gate, gla_o_norm, gla_w_o,
           diff_w_qkv, diff_q_norm, diff_k_norm, diff_lambda, diff_sub_norm, diff_w_o, mla_w_dq,
           mla_q_a_norm, mla_w_uq, mla_w_dkv, mla_kv_a_norm, mla_w_ukv, mla_q_norm, mla_k_norm,
           mla_w_o):
    xr = _Rows(x_prompt.reshape(N_PROMPT, D_MODEL), x_sample.reshape(N_LATENT, D_MODEL), 0)
    cvecs = jnp.concatenate([c_ctx[None], c, jnp.zeros((5, D_MODEL), F32)], axis=0)
    mods_all = _ada_mods(cvecs, ada_w, ada_b)
    halves = lambda o_p, o_s: _Rows(o_p, o_s, 0)

    mods = mods_all[0]
    qkv = _norm_mod_proj(xr, norm_mix[0], mods, [na_w_qkv], "na_qkv")
    x_p, new_l0_k, new_l0_v = _na_prompt(qkv, na_q_norm, na_k_norm, xr.prompt, mods, na_w_o)
    x_s = _na_latent(qkv, cache_l0_k, cache_l0_v, _na_bias_blocks(na_bias), na_q_norm, na_k_norm,
                     xr.latent, mods, na_w_o)
    ffn_weights = (ffn_w_up, ffn_conv_w, ffn_conv_b, ffn_w_down)
    n_pt = N_PROMPT // TOK_TILE
    ffn0 = functools.partial(_ffn, g=norm_ffn[0], mods=mods, layer=0, w_up=ffn_w_up,
                             conv_w=ffn_conv_w, conv_b=ffn_conv_b, w_down=ffn_w_down)
    xr = _Rows(ffn0(x_p, tile0=0, n_tiles=n_pt, x_tile0=0),
               ffn0(x_s, tile0=n_pt, n_tiles=N_TOK_TILES - n_pt, x_tile0=0), 0)

    mods = mods_all[1]
    w_decay = jnp.concatenate(
        [gla_w_gate1[0], gla_w_gate1[1],
         jnp.zeros((D_MODEL, 128 - 2 * GLA_GATE_RANK), F32)], axis=1)
    proj = _norm_mod_proj(xr, norm_mix[1], mods, [gla_w_qkvg, w_decay], "gla_proj")
    w2 = jnp.zeros((2, 128, GLA_HK), F32)
    w2 = w2.at[0, :GLA_GATE_RANK].set(gla_w_gate2[0])
    w2 = w2.at[1, GLA_GATE_RANK:2 * GLA_GATE_RANK].set(gla_w_gate2[1])
    bg = gla_b_gate.reshape(2, 1, GLA_HK)
    o_p, new_l1_fwd, new_l1_bwd = _gla(proj, w2, bg, gla_o_norm, BATCH, SEQ, 0)
    o_s = _gla(proj, w2, bg, gla_o_norm, DEC_BATCH, DEC_SEQ, N_PROMPT // DEC_SEQ,
               states=(state_l1_fwd, state_l1_bwd))
    x = _out_proj_residual(xr, halves(o_p, o_s), mods, gla_w_o, "gla_out",
                           gate=proj, gate_col_block=(2 * GLA_HK + GLA_HV) // GLA_HV)
    x = _ffn(x, norm_ffn[1], mods, 1, *ffn_weights)
    xr = _one_array(x)

    mods = mods_all[2]
    qkv = _norm_mod_proj(xr, norm_mix[2], mods, [diff_w_qkv], "diff_qkv")
    o_p, kn_p, new_l2_v = _diff_prompt(qkv, diff_q_norm, diff_k_norm, diff_lambda, diff_sub_norm)
    new_l2_k = kn_p.reshape(BATCH, 2 * DIFF_HEADS, SEQ, DIFF_HD)
    cos_d, sin_d = _diff_rope_tables()
    o_s = _diff_latent(qkv, cache_l2_k, cache_l2_v, cos_d, sin_d, diff_q_norm, diff_k_norm,
                       diff_lambda, diff_sub_norm)
    x = _out_proj_residual(xr, halves(o_p, o_s), mods, diff_w_o, "diff_out")
    x = _ffn(x, norm_ffn[2], mods, 2, *ffn_weights)
    xr = _one_array(x)

    mods = mods_all[3]
    zc = lambda n: jnp.zeros((D_MODEL, n), F32)
    w_low = jnp.concatenate(
        [mla_w_dq, zc(MLA_LOW_KV - MLA_Q_RANK), mla_w_dkv[:, :MLA_KV_RANK],
         zc(MLA_NOPE), mla_w_dkv[:, MLA_KV_RANK:], zc(MLA_HL - MLA_QK)], axis=1)
    low = _norm_mod_proj(xr, norm_mix[3], mods, [w_low], "mla_down")
    w_uq = _pad_heads(mla_w_uq, MLA_HEADS, MLA_QK, MLA_HL)
    qp, _ = _rms_matmul(low, 0, MLA_Q_RANK, mla_q_a_norm, w_uq, True, "mla_uq")
    kvp, ckv = _rms_matmul(low, MLA_LOW_KV // MLA_KV_RANK, MLA_KV_RANK, mla_kv_a_norm, mla_w_ukv,
                           True, "mla_ukv")
    kvc, _ = _rms_matmul(cache_l3_ckv.reshape(DEC_BATCH * PAST_LEN, MLA_KV_RANK), 0, MLA_KV_RANK,
                         mla_kv_a_norm, mla_w_ukv, False, "mla_ukv_cache")
    krc = jnp.pad(cache_l3_krope.reshape(DEC_BATCH * PAST_LEN, MLA_ROPE),
                  ((0, 0), (MLA_NOPE, MLA_HL - MLA_QK)))
    pad_gain = lambda g: jnp.tile(jnp.pad(g, (0, MLA_HL - MLA_QK)), MLA_HB).reshape(1, -1)
    gq, gk = pad_gain(mla_q_norm), pad_gain(mla_k_norm)
    o_p = _mla_prompt(qp, kvp, low, gq, gk)
    cos_m, sin_m = _mla_rope_tables()
    o_s = _mla_latent(qp, kvp, low, kvc, krc, cos_m, sin_m, gq, gk)
    new_l3_ckv = ckv[:N_PROMPT].reshape(BATCH, SEQ, MLA_KV_RANK)
    new_l3_krope = low[:N_PROMPT, MLA_LOW_KR + MLA_NOPE:MLA_LOW_KR + MLA_QK].reshape(
        BATCH, SEQ, MLA_ROPE)
    x = _out_proj_residual(xr, halves(o_p, o_s), mods, mla_w_o, "mla_out")
    n_pt = N_PROMPT // TOK_TILE
    ffn3 = functools.partial(_ffn, x, norm_ffn[3], mods, 3, *ffn_weights)
    y_prompt = ffn3(tile0=0, n_tiles=n_pt).reshape(BATCH, SEQ, D_MODEL)
    y_sample = ffn3(tile0=n_pt, n_tiles=N_TOK_TILES - n_pt).reshape(DEC_BATCH, DEC_SEQ, D_MODEL)
    return (y_prompt, y_sample, new_l0_k, new_l0_v, new_l1_fwd, new_l1_bwd, new_l2_k, new_l2_v,
            new_l3_ckv, new_l3_krope)
```

```python
import functools
import math
from typing import NamedTuple

import jax
import jax.numpy as jnp
from jax import lax
from jax.experimental import pallas as pl
from jax.experimental.pallas import tpu as pltpu

F32 = jnp.float32
BF16 = jnp.bfloat16

D_MODEL = 1024
BATCH = 16
SEQ = 256
DEPTH = 4
DEC_BATCH = 2
DEC_SEQ = 2048
PAST_LEN = 256
GRID_W = 64
D_FF = 2816
EPS = 1e-6
ROPE_BASE = 10000.0

NA_HEADS = 16
NA_HD = 64
NA_WIN_R = 8
NA_WIN_C = 16

GLA_HEADS = 4
GLA_DK = 128
GLA_DV = 256
GLA_HK = GLA_HEADS * GLA_DK
GLA_HV = GLA_HEADS * GLA_DV
GLA_GATE_RANK = 16
GLA_GATE_NORM = 16.0

DIFF_HEADS = 8
DIFF_HD = 64
DIFF_LAMBDA_INIT = 0.8 - 0.6 * math.exp(-0.3 * 2)

MLA_HEADS = 16
MLA_Q_RANK = 384
MLA_KV_RANK = 256
MLA_NOPE = 64
MLA_ROPE = 32
MLA_V = 64
MLA_QK = MLA_NOPE + MLA_ROPE

N_PROMPT = BATCH * SEQ
N_LATENT = DEC_BATCH * DEC_SEQ
N_TOK = N_PROMPT + N_LATENT
TOK_TILE = 2048
N_TOK_TILES = N_TOK // TOK_TILE
FF_CHUNK = 256
N_FF_CHUNKS = D_FF // FF_CHUNK
NEG = -1e30
LOG2E = math.log2(math.e)
SAFE_SCORE_BOUND = 48.0

VMEM_LIMIT = 56 * 1024 * 1024

_NT = (((1,), (1,)), ((), ()))
_TN = (((0,), (0,)), ((), ()))


def _params(sem, vmem=VMEM_LIMIT):
    return pltpu.CompilerParams(dimension_semantics=sem, vmem_limit_bytes=vmem)


def _log2(n):
    assert n & (n - 1) == 0
    return n.bit_length() - 1


def _silu(x):
    return x / (1.0 + jnp.exp(-x))


def _bdot(a, b):
    return jnp.dot(a, b, preferred_element_type=F32)


def _softmax_parts(parts, shift=None):
    m = shift
    if m is None:
        m = parts[0].max(axis=-1, keepdims=True)
        for s in parts[1:]:
            m = jnp.maximum(m, s.max(axis=-1, keepdims=True))
    ps = [jnp.exp2(s - m) for s in parts]
    l = ps[0].sum(axis=-1, keepdims=True)
    for p in ps[1:]:
        l = l + p.sum(axis=-1, keepdims=True)
    return ps, 1.0 / l


def _group_rms(x, gain, group, n_real=None, sums_on_mxu=False):
    lanes = x.shape[-1]
    n_real = n_real or group
    x2 = x * x
    if group == lanes:
        ms = jnp.sum(x2, axis=-1, keepdims=True)
    elif not sums_on_mxu:
        gid = lax.broadcasted_iota(jnp.int32, x.shape, 1) >> _log2(group)
        ms = jnp.zeros_like(x)
        for i in range(lanes // group):
            sel = gid == i
            si = jnp.sum(jnp.where(sel, x2, 0.0), axis=-1, keepdims=True)
            ms = jnp.where(sel, si, ms)
    else:
        r = lax.broadcasted_iota(jnp.int32, (lanes, lanes), 0) >> _log2(group)
        c = lax.broadcasted_iota(jnp.int32, (lanes, lanes), 1) >> _log2(group)
        ones = jnp.where(r == c, 1.0, 0.0).astype(BF16)
        hi = x2.astype(BF16)
        lo = (x2 - hi.astype(F32)).astype(BF16)
        ms = _bdot(hi, ones) + _bdot(lo, ones)
    return x * lax.rsqrt(ms * (1.0 / n_real) + EPS) * gain


def _with_score_bound(bound, body):
    ok = bound.max() < SAFE_SCORE_BOUND
    pl.when(ok)(lambda: body(bound))
    pl.when(jnp.logical_not(ok))(lambda: body(None))


def _rms_norm_bound(g_ref, n, head_lanes):
    g = jnp.abs(g_ref[...])
    heads = g.shape[-1] // head_lanes
    return n ** 0.5 * jnp.concatenate(
        [g[:, h * head_lanes:(h + 1) * head_lanes].max(axis=-1, keepdims=True)
         for h in range(heads)], axis=-1)


def _stacked_rows(per_head, rows):
    r = lax.broadcasted_iota(jnp.int32, (per_head.shape[-1] * rows, 1), 0)
    out = per_head[:, 0:1]
    for h in range(1, per_head.shape[-1]):
        out = jnp.where(r >= h * rows, per_head[:, h:h + 1], out)
    return out


def _max_head_norms(x, head_lanes):
    hid = lax.broadcasted_iota(jnp.int32, x.shape, 1) >> _log2(head_lanes)
    x2 = x * x
    sq = [jnp.sum(jnp.where(hid == h, x2, 0.0), axis=-1, keepdims=True).max(axis=0, keepdims=True)
          for h in range(x.shape[-1] // head_lanes)]
    return jnp.sqrt(jnp.concatenate(sq, axis=-1))


def _rope(x, cos, sin, half):
    lanes = x.shape[-1]
    lane = lax.broadcasted_iota(jnp.int32, x.shape, 1)
    up = pltpu.roll(x, lanes - half, axis=1)
    dn = pltpu.roll(x, half, axis=1)
    swapped = jnp.where((lane & (2 * half - 1)) < half, up, dn)
    return x * cos + swapped * sin


def _stack_heads(q, n_heads, head_lanes):
    hid = lax.broadcasted_iota(jnp.int32, q.shape, 1) >> _log2(head_lanes)
    zero = jnp.zeros_like(q)
    return jnp.concatenate([jnp.where(hid == i, q, zero) for i in range(n_heads)], axis=0)


def _unstack_heads(o, n_heads, head_lanes):
    rows = o.shape[0] // n_heads
    hid = lax.broadcasted_iota(jnp.int32, (rows, o.shape[1]), 1) >> _log2(head_lanes)
    out = o[0:rows]
    for i in range(1, n_heads):
        out = jnp.where(hid == i, o[i * rows:(i + 1) * rows], out)
    return out


ADA_TN = 1536


def _ada_kernel(c_ref, w_ref, b_ref, o_ref):
    s = _silu(c_ref[...])
    w = w_ref[0]
    s_hi, w_hi = s.astype(BF16), w.astype(BF16)
    s_lo = (s - s_hi.astype(F32)).astype(BF16)
    w_lo = (w - w_hi.astype(F32)).astype(BF16)
    both = _bdot(jnp.concatenate([s_hi, s_lo], axis=0), w_hi)
    o_ref[0] = both[0:8] + both[8:16] + _bdot(s_hi, w_lo) + b_ref[0]


def _ada_mods(cvecs, ada_w, ada_b):
    out = pl.pallas_call(
        _ada_kernel,
        grid=(DEPTH, 6 * D_MODEL // ADA_TN),
        in_specs=[pl.BlockSpec((8, D_MODEL), lambda l, j: (0, 0)),
                  pl.BlockSpec((1, D_MODEL, ADA_TN), lambda l, j: (l, 0, j)),
                  pl.BlockSpec((1, 1, ADA_TN), lambda l, j: (l, 0, j))],
        out_specs=pl.BlockSpec((1, 8, ADA_TN), lambda l, j: (l, 0, j)),
        out_shape=jax.ShapeDtypeStruct((DEPTH, 8, 6 * D_MODEL), F32),
        compiler_params=_params(("arbitrary", "arbitrary")),
        name="ada_mod",
    )(cvecs, ada_w, ada_b.reshape(DEPTH, 1, 6 * D_MODEL))
    return out.reshape(DEPTH, 8, 6, D_MODEL)[:, :3]


def _mod_group_of_tile(i):
    return jnp.maximum(i - (N_PROMPT // TOK_TILE - 1), 0)


def _norm_mod_rows(x_ref, g_ref, mod_ref, h_ref, shift_idx, scale_idx, rows=64):
    g = g_ref[...]
    sc = 1.0 + mod_ref[0, scale_idx:scale_idx + 1, :]
    sh = mod_ref[0, shift_idx:shift_idx + 1, :]

    def body(r, carry):
        sl = pl.ds(pl.multiple_of(r * rows, rows), rows)
        xf = x_ref[sl, :]
        ms = jnp.mean(xf * xf, axis=-1, keepdims=True)
        y = xf * lax.rsqrt(ms + EPS) * g
        h_ref[sl, :] = (y * sc + sh).astype(BF16)
        return carry

    lax.fori_loop(0, x_ref.shape[0] // rows, body, 0)


ROW_TM = 512


class _Rows(NamedTuple):
    prompt: jax.Array
    latent: jax.Array
    latent_row0: int


def _one_array(x):
    return _Rows(x, x, N_PROMPT)


def _row_specs(rows, width, col_block=0):
    n_p = N_PROMPT // ROW_TM
    l0 = rows.latent_row0 // ROW_TM
    return [pl.BlockSpec((ROW_TM, width), lambda t: (jnp.minimum(t, n_p - 1), col_block)),
            pl.BlockSpec((ROW_TM, width), lambda t: (l0 + jnp.maximum(t - n_p, 0), col_block))]


def _row_group(t):
    first_latent = N_PROMPT // ROW_TM
    return jnp.where(t < first_latent, 0, 1 + (t - first_latent) // (DEC_SEQ // ROW_TM))


def _is_prompt_tile():
    return pl.program_id(0) < N_PROMPT // ROW_TM


def _proj_kernel(xp_ref, xl_ref, g_ref, mod_ref, *refs):
    *w_refs, o_ref, h_ref, wb_ref = refs

    @pl.when(pl.program_id(0) == 0)
    def _():
        off = 0
        for w_ref in w_refs:
            wb_ref[:, off:off + w_ref.shape[1]] = w_ref[...].astype(BF16)
            off += w_ref.shape[1]

    is_prompt = _is_prompt_tile()
    g = g_ref[...]
    sc = 1.0 + mod_ref[0, 1:2, :]
    sh = mod_ref[0, 0:1, :]
    rows = 64
    part = ROW_TM // 2
    for p0 in range(0, ROW_TM, part):
        for r0 in range(p0, p0 + part, rows):
            sl = slice(r0, r0 + rows)
            xf = jnp.where(is_prompt, xp_ref[sl, :], xl_ref[sl, :])
            ms = jnp.mean(xf * xf, axis=-1, keepdims=True)
            y = xf * lax.rsqrt(ms + EPS) * g
            h_ref[sl, :] = (y * sc + sh).astype(BF16)
        o_ref[p0:p0 + part, :] = _bdot(h_ref[p0:p0 + part, :], wb_ref[...])


def _norm_mod_proj(x, g, mods, ws, name):
    n = sum(w.shape[1] for w in ws)
    return pl.pallas_call(
        _proj_kernel,
        grid=(N_TOK // ROW_TM,),
        in_specs=_row_specs(x, D_MODEL) + [
            pl.BlockSpec((1, D_MODEL), lambda t: (0, 0)),
            pl.BlockSpec((1, 6, D_MODEL), lambda t: (_row_group(t), 0, 0))] + [
            pl.BlockSpec(w.shape, lambda t: (0, 0), pipeline_mode=pl.Buffered(1)) for w in ws],
        out_specs=pl.BlockSpec((ROW_TM, n), lambda t: (t, 0)),
        out_shape=jax.ShapeDtypeStruct((N_TOK, n), F32),
        scratch_shapes=[pltpu.VMEM((ROW_TM, D_MODEL), BF16),
                        pltpu.VMEM((D_MODEL, n), BF16)],
        compiler_params=_params(("arbitrary",)),
        name=name,
    )(x.prompt, x.latent, g.reshape(1, D_MODEL), mods, *ws)


def _rms(a, g):
    return a * lax.rsqrt(jnp.mean(a * a, axis=-1, keepdims=True) + EPS) * g


def _mla_front_kernel(xp_ref, xl_ref, g_ref, mod_ref, wl_ref, gqa_ref, wuq_ref, gkv_ref, wukv_ref,
                      qp_ref, kvp_ref, ckv_ref, kr_ref, h_ref, wlb_ref, wuqb_ref, wukvb_ref):
    @pl.when(pl.program_id(0) == 0)
    def _():
        wlb_ref[...] = wl_ref[...].astype(BF16)
        wuqb_ref[...] = wuq_ref[...].astype(BF16)
        wukvb_ref[...] = wukv_ref[...].astype(BF16)

    is_prompt = _is_prompt_tile()
    g = g_ref[...]
    sc = 1.0 + mod_ref[0, 1:2, :]
    sh = mod_ref[0, 0:1, :]
    rows = 64
    part = ROW_TM // 2
    for p0 in range(0, ROW_TM, part):
        for r0 in range(p0, p0 + part, rows):
            sl = slice(r0, r0 + rows)
            xf = jnp.where(is_prompt, xp_ref[sl, :], xl_ref[sl, :])
            h_ref[sl, :] = (_rms(xf, g) * sc + sh).astype(BF16)
        ps = slice(p0, p0 + part)
        low = _bdot(h_ref[ps, :], wlb_ref[...])
        qa = _rms(low[:, MLA_LOW_Q:MLA_LOW_Q + MLA_Q_RANK], gqa_ref[...])
        qp_ref[ps, :] = _bdot(qa.astype(BF16), wuqb_ref[...])
        ckv = _rms(low[:, MLA_LOW_KV:MLA_LOW_KV + MLA_KV_RANK], gkv_ref[...])
        ckv_ref[ps, :] = ckv
        kvp_ref[ps, :] = _bdot(ckv.astype(BF16), wukvb_ref[...])
        kr_ref[ps, :] = low[:, MLA_LOW_KR:MLA_LOW_KR + MLA_HL]


def _mla_front(x, g, mods, w_low, g_qa, w_uq, g_kva, w_ukv):
    n_q, n_kv = w_uq.shape[1], w_ukv.shape[1]
    const = lambda shape: pl.BlockSpec(shape, lambda t: (0, 0), pipeline_mode=pl.Buffered(1))
    out = lambda n: pl.BlockSpec((ROW_TM, n), lambda t: (t, 0))
    shape = lambda n: jax.ShapeDtypeStruct((N_TOK, n), F32)
    return pl.pallas_call(
        _mla_front_kernel,
        grid=(N_TOK // ROW_TM,),
        in_specs=_row_specs(x, D_MODEL) + [
            pl.BlockSpec((1, D_MODEL), lambda t: (0, 0)),
            pl.BlockSpec((1, 6, D_MODEL), lambda t: (_row_group(t), 0, 0)),
            const(w_low.shape), const((1, MLA_Q_RANK)), const(w_uq.shape),
            const((1, MLA_KV_RANK)), const(w_ukv.shape)],
        out_specs=[out(n_q), out(n_kv), out(MLA_KV_RANK), out(MLA_HL)],
        out_shape=[shape(n_q), shape(n_kv), shape(MLA_KV_RANK), shape(MLA_HL)],
        scratch_shapes=[pltpu.VMEM((ROW_TM, D_MODEL), BF16),
                        pltpu.VMEM(w_low.shape, BF16),
                        pltpu.VMEM(w_uq.shape, BF16),
                        pltpu.VMEM(w_ukv.shape, BF16)],
        compiler_params=_params(("arbitrary",)),
        name="mla_front",
    )(x.prompt, x.latent, g.reshape(1, D_MODEL), mods, w_low, g_qa.reshape(1, -1), w_uq,
      g_kva.reshape(1, -1), w_ukv)


def _matmul_kernel(a_ref, w_ref, o_ref):
    o_ref[...] = _bdot(a_ref[...].astype(BF16), w_ref[...].astype(BF16))


def _matmul(a, w, name):
    rows, n = a.shape[0], w.shape[1]
    return pl.pallas_call(
        _matmul_kernel,
        out_shape=jax.ShapeDtypeStruct((rows, n), F32),
        compiler_params=_params(()),
        name=name,
    )(a, w)


def _oproj_kernel(*refs, gated):
    if gated:
        xp_ref, xl_ref, ap_ref, al_ref, g_ref, mod_ref, w_ref, o_ref, wb_ref = refs
    else:
        xp_ref, xl_ref, ap_ref, al_ref, mod_ref, w_ref, o_ref, wb_ref = refs

    @pl.when(pl.program_id(0) == 0)
    def _():
        wb_ref[...] = w_ref[...].astype(BF16)

    is_prompt = _is_prompt_tile()
    a = jnp.where(is_prompt, ap_ref[...], al_ref[...])
    if gated:
        a = a * _silu(g_ref[...])
    y = _bdot(a.astype(BF16), wb_ref[...])
    x = jnp.where(is_prompt, xp_ref[...], xl_ref[...])
    o_ref[...] = x + mod_ref[0, 2:3, :] * y


def _out_proj_residual(x, a, mods, w, name, gate=None, gate_col_block=0):
    k = w.shape[0]
    in_specs = _row_specs(x, D_MODEL) + _row_specs(a, k)
    args = [x.prompt, x.latent, a.prompt, a.latent]
    if gate is not None:
        in_specs.append(pl.BlockSpec((ROW_TM, k), lambda t: (t, gate_col_block)))
        args.append(gate)
    in_specs += [pl.BlockSpec((1, 6, D_MODEL), lambda t: (_row_group(t), 0, 0)),
                 pl.BlockSpec((k, D_MODEL), lambda t: (0, 0))]
    args += [mods, w]
    return pl.pallas_call(
        functools.partial(_oproj_kernel, gated=gate is not None),
        grid=(N_TOK // ROW_TM,),
        in_specs=in_specs,
        out_specs=pl.BlockSpec((ROW_TM, D_MODEL), lambda t: (t, 0)),
        out_shape=jax.ShapeDtypeStruct((N_TOK, D_MODEL), F32),
        scratch_shapes=[pltpu.VMEM((k, D_MODEL), BF16)],
        compiler_params=_params(("arbitrary",)),
        name=name,
    )(*args)


FFN_MM_ROWS = 512
FFN_ROWS = 64
FFN_PAD = 8


def _ffn_kernel(x_ref, g_ref, mod_ref, wg_ref, wv_ref, cwg_ref, cwv_ref, cbg_ref, cbv_ref,
                wd_ref, o_ref, h_ref, u_ref, act_ref, wup_ref, wdn_ref, *, tile0):
    i = tile0 + pl.program_id(0)
    c = pl.program_id(1)
    fc = FF_CHUNK
    n = TOK_TILE // FFN_MM_ROWS
    seq_len = jnp.where(i < N_PROMPT // TOK_TILE, SEQ, DEC_SEQ)
    row = lax.broadcasted_iota(jnp.int32, (FFN_ROWS, 1), 0)
    taps = lambda cw_ref, cb_ref: [jnp.broadcast_to(cw_ref[j:j + 1, :], (FFN_ROWS, fc))
                                   for j in range(3)] + [
                                       jnp.broadcast_to(cb_ref[...], (FFN_ROWS, fc))]
    taps_g, taps_v = taps(cwg_ref, cbg_ref), taps(cwv_ref, cbv_ref)

    def up(u_ref, t):
        r0 = t * FFN_MM_ROWS
        u_ref[FFN_PAD + r0:FFN_PAD + r0 + FFN_MM_ROWS, :] = _bdot(
            h_ref[r0:r0 + FFN_MM_ROWS, :], wup_ref[...])

    def conv_act(u_ref, t):
        for r0 in range(t * FFN_MM_ROWS, (t + 1) * FFN_MM_ROWS, FFN_ROWS):
            halves = []
            for lo, (w0, w1, w2, bias) in ((0, taps_g), (fc, taps_v)):
                p0 = FFN_PAD + r0
                prev = u_ref[p0 - 1:p0 - 1 + FFN_ROWS, lo:lo + fc]
                mid = u_ref[p0:p0 + FFN_ROWS, lo:lo + fc]
                nxt = u_ref[p0 + 1:p0 + 1 + FFN_ROWS, lo:lo + fc]
                if r0 % SEQ == 0:
                    prev = jnp.where(((r0 + row) & (seq_len - 1)) == 0, 0.0, prev)
                if (r0 + FFN_ROWS) % SEQ == 0:
                    nxt = jnp.where(((r0 + row) & (seq_len - 1)) == seq_len - 1, 0.0, nxt)
                halves.append(prev * w0 + mid * w1 + nxt * w2 + bias)
            act_ref[r0:r0 + FFN_ROWS, :] = (_silu(halves[0]) * halves[1]).astype(BF16)

    def down(t):
        r0 = t * FFN_MM_ROWS
        o_ref[r0:r0 + FFN_MM_ROWS, :] += _bdot(act_ref[r0:r0 + FFN_MM_ROWS, :], wdn_ref[...])

    @pl.when(c == 0)
    def _():
        _norm_mod_rows(x_ref, g_ref, mod_ref, h_ref, 3, 4)
        zeros = jnp.zeros((FFN_PAD, 2 * fc), F32)
        u_ref[0:FFN_PAD, :] = zeros
        u_ref[FFN_PAD + TOK_TILE:, :] = zeros
        o_ref[...] = jnp.zeros_like(o_ref)

    wup_ref[:, :fc] = wg_ref[...].astype(BF16)
    wup_ref[:, fc:] = wv_ref[...].astype(BF16)
    wdn_ref[...] = wd_ref[...].astype(BF16)
    for s in range(n + 2):
        if s < n:
            up(u_ref, s)
        if 1 <= s <= n:
            conv_act(u_ref, s - 1)
        if s >= 2:
            down(s - 2)

    @pl.when(c == N_FF_CHUNKS - 1)
    def _():
        o_ref[...] = x_ref[...] + mod_ref[0, 5:6, :] * o_ref[...]


def _ffn(x, g, mods, layer, w_up, conv_w, conv_b, w_down, tile0=0, n_tiles=N_TOK_TILES):
    fc = FF_CHUNK
    ncb = N_FF_CHUNKS
    return pl.pallas_call(
        functools.partial(_ffn_kernel, tile0=tile0),
        grid=(n_tiles, ncb),
        in_specs=[pl.BlockSpec((TOK_TILE, D_MODEL), lambda i, c: (tile0 + i, 0)),
                  pl.BlockSpec((1, D_MODEL), lambda i, c: (0, 0)),
                  pl.BlockSpec((1, 6, D_MODEL), lambda i, c: (_mod_group_of_tile(tile0 + i), 0, 0)),
                  pl.BlockSpec((None, D_MODEL, fc), lambda i, c: (layer, 0, c)),
                  pl.BlockSpec((None, D_MODEL, fc), lambda i, c: (layer, 0, ncb + c)),
                  pl.BlockSpec((None, 3, fc), lambda i, c: (layer, 0, c)),
                  pl.BlockSpec((None, 3, fc), lambda i, c: (layer, 0, ncb + c)),
                  pl.BlockSpec((None, 1, fc), lambda i, c: (layer, 0, c)),
                  pl.BlockSpec((None, 1, fc), lambda i, c: (layer, 0, ncb + c)),
                  pl.BlockSpec((None, fc, D_MODEL), lambda i, c: (layer, c, 0))],
        out_specs=pl.BlockSpec((TOK_TILE, D_MODEL), lambda i, c: (i, 0)),
        out_shape=jax.ShapeDtypeStruct((n_tiles * TOK_TILE, D_MODEL), F32),
        scratch_shapes=[pltpu.VMEM((TOK_TILE, D_MODEL), BF16),
                        pltpu.VMEM((TOK_TILE + 2 * FFN_PAD, 2 * fc), F32),
                        pltpu.VMEM((TOK_TILE, fc), BF16),
                        pltpu.VMEM((D_MODEL, 2 * fc), BF16),
                        pltpu.VMEM((fc, D_MODEL), BF16)],
        compiler_params=_params(("arbitrary", "arbitrary")),
        name="conv_ffn",
    )(x, g.reshape(1, D_MODEL), mods, w_up, w_up, conv_w, conv_w,
      conv_b.reshape(DEPTH, 1, -1), conv_b.reshape(DEPTH, 1, -1), w_down)


NA_HB = 4
NA_LANES = NA_HB * NA_HD
NA_ROWS = DEC_SEQ // GRID_W
NA_KEYS = NA_WIN_R * GRID_W
PROMPT_SEQS = 4
NA_ROW_UNROLL = 4


def _store_heads(dst_ref, seq, x, n_heads, hd):
    for h in range(n_heads):
        dst_ref[seq, h] = x[:, h * hd:(h + 1) * hd]


def _na_prompt_kernel(q_ref, k_ref, v_ref, gq_ref, gk_ref, o_ref, kn_ref, vn_ref):
    scale = NA_HD ** -0.5 * LOG2E

    def chains(bound):
        shift = None if bound is None else _stacked_rows(bound, SEQ)
        for seq in range(PROMPT_SEQS):
            sl = slice(seq * SEQ, (seq + 1) * SEQ)
            q = _group_rms(q_ref[sl, :], gq_ref[...], NA_HD, sums_on_mxu=True) * scale
            k = _group_rms(k_ref[sl, :], gk_ref[...], NA_HD, sums_on_mxu=True)
            v = v_ref[sl, :]
            _store_heads(kn_ref, seq, k, NA_HB, NA_HD)
            _store_heads(vn_ref, seq, v, NA_HB, NA_HD)
            q4 = _stack_heads(q, NA_HB, NA_HD).astype(BF16)
            s = lax.dot_general(q4, k.astype(BF16), _NT, preferred_element_type=F32)
            (p,), inv = _softmax_parts([s], shift)
            o4 = _bdot(p.astype(BF16), v.astype(BF16)) * inv
            o_ref[sl, :] = _unstack_heads(o4, NA_HB, NA_HD)

    _with_score_bound(scale * _rms_norm_bound(gq_ref, NA_HD, NA_HD)
                      * _rms_norm_bound(gk_ref, NA_HD, NA_HD), chains)


def _na_prompt(qkv, gq, gk):
    nb = NA_HEADS // NA_HB
    rows = PROMPT_SEQS * SEQ
    blk = lambda off: pl.BlockSpec((rows, NA_LANES), lambda b, j: (b, off + j))
    vec = pl.BlockSpec((1, NA_LANES), lambda b, j: (0, 0))
    cache = pl.BlockSpec((PROMPT_SEQS, NA_HB, SEQ, NA_HD), lambda b, j: (b, j, 0, 0))
    cache_shape = jax.ShapeDtypeStruct((BATCH, NA_HEADS, SEQ, NA_HD), F32)
    return pl.pallas_call(
        _na_prompt_kernel,
        grid=(BATCH // PROMPT_SEQS, nb),
        in_specs=[blk(0), blk(nb), blk(2 * nb), vec, vec],
        out_specs=[blk(0), cache, cache],
        out_shape=[jax.ShapeDtypeStruct((N_PROMPT, D_MODEL), F32), cache_shape, cache_shape],
        compiler_params=_params(("arbitrary", "arbitrary")),
        name="na_prompt",
    )(qkv, qkv, qkv, jnp.tile(gq, NA_HB).reshape(1, -1), jnp.tile(gk, NA_HB).reshape(1, -1))


def _na_latent_kernel(q_ref, k_ref, v_ref, kc_ref, vc_ref, t_ref, gq_ref, gk_ref, o_ref,
                      qn_ref, kn_ref, vb_ref, kc4_ref, vc4_ref, bias_ref):
    scale = NA_HD ** -0.5 * LOG2E
    rows = 256

    def prep(r, carry):
        sl = pl.ds(pl.multiple_of(r * rows, rows), rows)
        qn_ref[sl, :] = (_group_rms(q_ref[sl, :], gq_ref[...], NA_HD, sums_on_mxu=True)
                         * scale).astype(BF16)
        kn_ref[sl, :] = _group_rms(k_ref[sl, :], gk_ref[...], NA_HD,
                                   sums_on_mxu=True).astype(BF16)
        vb_ref[sl, :] = v_ref[sl, :].astype(BF16)
        return carry

    lax.fori_loop(0, DEC_SEQ // rows, prep, 0)
    kc = kc_ref[0]
    kc4_ref[...] = kc.astype(BF16)
    vc4_ref[...] = vc_ref[0].astype(BF16)

    def attend(bound):
        for h in range(NA_HB):
            off = 0.0 if bound is None else bound[:, h:h + 1]
            for p in range(NA_WIN_R):
                for i in range(NA_WIN_R):
                    bias_ref[h, p, :, i * GRID_W:(i + 1) * GRID_W] = t_ref[h, p + i] - off
        shift = None if bound is None else _stacked_rows(bound, GRID_W)

        def row(r):
            kr0 = jnp.clip(r - NA_WIN_R // 2, 0, NA_ROWS - NA_WIN_R)
            pat = kr0 - r + NA_WIN_R - 1
            qs = pl.ds(pl.multiple_of(r * GRID_W, GRID_W), GRID_W)
            ks = pl.ds(pl.multiple_of(kr0 * GRID_W, GRID_W), NA_KEYS)
            q4 = _stack_heads(qn_ref[qs, :], NA_HB, NA_HD)
            s_loc = lax.dot_general(q4, kn_ref[ks, :], _NT, preferred_element_type=F32)
            s_loc = s_loc + jnp.concatenate([bias_ref[h, pat] for h in range(NA_HB)], axis=0)
            s_ctx = lax.dot_general(q4, kc4_ref[...], _NT, preferred_element_type=F32)
            if shift is None:
                (p_loc, p_ctx), inv = _softmax_parts([s_loc, s_ctx])
            else:
                p_loc, p_ctx = jnp.exp2(s_loc), jnp.exp2(s_ctx - shift)
                inv = 1.0 / (p_loc.sum(axis=-1, keepdims=True) + p_ctx.sum(axis=-1, keepdims=True))
            o4 = _bdot(p_loc.astype(BF16), vb_ref[ks, :]) + _bdot(p_ctx.astype(BF16), vc4_ref[...])
            o_ref[qs, :] = _unstack_heads(o4 * inv, NA_HB, NA_HD)

        def rows_step(i, carry):
            for u in range(NA_ROW_UNROLL):
                row(i * NA_ROW_UNROLL + u)
            return carry

        lax.fori_loop(0, NA_ROWS // NA_ROW_UNROLL, rows_step, 0)

    qmax = scale * _rms_norm_bound(gq_ref, NA_HD, NA_HD)
    bias_max = jnp.concatenate(
        [t_ref[h].max(axis=0).max(axis=0, keepdims=True).max(axis=1, keepdims=True)
         for h in range(NA_HB)], axis=-1)
    _with_score_bound(
        jnp.maximum(qmax * _rms_norm_bound(gk_ref, NA_HD, NA_HD) + bias_max,
                    qmax * _max_head_norms(kc, NA_HD)), attend)


def _na_bias_blocks(bias_table):
    qc = jnp.arange(GRID_W)[:, None]
    kc = jnp.arange(GRID_W)[None, :]
    win0 = jnp.clip(qc - NA_WIN_C // 2, 0, GRID_W - NA_WIN_C)
    valid = (kc >= win0) & (kc < win0 + NA_WIN_C)
    n_co = bias_table.shape[-1]
    onehot = (kc - qc + NA_WIN_C - 1)[None] == jnp.arange(n_co)[:, None, None]
    t = jnp.einsum('hrd,dqk->hrqk', bias_table.astype(F32), onehot.astype(F32),
                   precision=lax.Precision.HIGHEST)
    return jnp.where(valid, t * LOG2E, NEG)


def _na_latent(qkv, cache_k, cache_v, bias_blocks, gq, gk):
    nb = NA_HEADS // NA_HB
    lat0 = N_PROMPT // DEC_SEQ
    blk = lambda off: pl.BlockSpec((DEC_SEQ, NA_LANES), lambda b, j: (lat0 + b, off + j))
    vec = pl.BlockSpec((1, NA_LANES), lambda b, j: (0, 0))
    cache = pl.BlockSpec((1, PAST_LEN, NA_LANES), lambda b, j: (b, 0, j))
    return pl.pallas_call(
        _na_latent_kernel,
        grid=(DEC_BATCH, nb),
        in_specs=[blk(0), blk(nb), blk(2 * nb), cache, cache,
                  pl.BlockSpec((NA_HB, 2 * NA_WIN_R - 1, GRID_W, GRID_W), lambda b, j: (j, 0, 0, 0)),
                  vec, vec],
        out_specs=pl.BlockSpec((DEC_SEQ, NA_LANES), lambda b, j: (b, j)),
        out_shape=jax.ShapeDtypeStruct((N_LATENT, D_MODEL), F32),
        scratch_shapes=[pltpu.VMEM((DEC_SEQ, NA_LANES), BF16),
                        pltpu.VMEM((DEC_SEQ, NA_LANES), BF16),
                        pltpu.VMEM((DEC_SEQ, NA_LANES), BF16),
                        pltpu.VMEM((PAST_LEN, NA_LANES), BF16),
                        pltpu.VMEM((PAST_LEN, NA_LANES), BF16),
                        pltpu.VMEM((NA_HB, NA_WIN_R, GRID_W, NA_KEYS), F32)],
        compiler_params=_params(("arbitrary", "arbitrary")),
        name="na_latent",
    )(qkv, qkv, qkv, _tokens_first(cache_k), _tokens_first(cache_v), bias_blocks,
      jnp.tile(gq, NA_HB).reshape(1, -1), jnp.tile(gk, NA_HB).reshape(1, -1))


GLA_C = 128
GLA_HPS = 2
GLA_SUB = 8
GLA_LEVELS = (64, 32, 16, 8)


def _split_hi_lo(x):
    hi = x.astype(BF16)
    lo = (x - hi.astype(F32)).astype(BF16)
    return jnp.concatenate([hi, lo], axis=1)


class _GlaMasks(NamedTuple):
    tri: jax.Array
    later: tuple
    sign: tuple
    pair: tuple
    diag: jax.Array


def _gla_masks(rev):
    c = GLA_C
    row = lax.broadcasted_iota(jnp.int32, (c, c), 0)
    col = lax.broadcasted_iota(jnp.int32, (c, c), 1)
    rid = lax.broadcasted_iota(jnp.int32, (c, GLA_DK), 0)
    causal = (col >= row) if rev else (col <= row)
    later, pair = [], []
    for m in GLA_LEVELS:
        later.append(((rid & m) == 0) if rev else ((rid & m) != 0))
        same = (row >> _log2(2 * m)) == (col >> _log2(2 * m))
        crossing = ((row & m) != (col & m))
        pair.append(same & crossing & causal)
    diag = ((row >> _log2(GLA_SUB)) == (col >> _log2(GLA_SUB))) & causal
    sign = tuple(jnp.where(l, 1.0, -1.0) for l in later)
    return _GlaMasks(jnp.where(causal, 1.0, 0.0).astype(BF16), tuple(later), sign, tuple(pair), diag)


def _gla_chunk(q, k, v, g, st_ref, rev, masks):
    c = GLA_C
    cs = _bdot(masks.tri, _split_hi_lo(g))
    b = cs[:, :GLA_DK] + cs[:, GLA_DK:]

    a = None
    for m, later, sign, pair in zip(GLA_LEVELS, masks.later, masks.sign, masks.pair):
        nblk = c // (2 * m)
        if rev:
            bnd = [b[j * 2 * m + m:j * 2 * m + m + 1] for j in range(nblk)]
        else:
            bnd = [b[j * 2 * m + m - 1:j * 2 * m + m] for j in range(nblk)]
        ref = jnp.concatenate([jnp.broadcast_to(x, (2 * m, GLA_DK)) for x in bnd], axis=0)
        x = (jnp.where(later, q, k) * jnp.exp((b - ref) * sign)).astype(BF16)
        blk = lax.dot_general(x, x, _NT, preferred_element_type=F32)
        a = jnp.where(pair, blk, 0.0 if a is None else a)

    nsub = c // GLA_SUB
    lane_c = lax.broadcasted_iota(jnp.int32, (GLA_SUB, c), 1)
    diag_rows = []
    for blk_i in range(nsub):
        r0 = blk_i * GLA_SUB
        qb = q[r0:r0 + GLA_SUB]
        bb = b[r0:r0 + GLA_SUB]
        acc = jnp.zeros((GLA_SUB, c), F32)
        for s in range(GLA_SUB):
            ks = k[r0 + s:r0 + s + 1]
            bs = b[r0 + s:r0 + s + 1]
            w = jnp.sum(qb * ks * jnp.exp(jnp.minimum(bb - bs, 0.0)), axis=-1, keepdims=True)
            acc = jnp.where(lane_c == r0 + s, w, acc)
        diag_rows.append(acc)
    a = jnp.where(masks.diag, jnp.concatenate(diag_rows, axis=0), a)

    st = st_ref[...]
    inter = lax.dot_general((q * jnp.exp(b)).astype(BF16), st.astype(BF16), _NT,
                            preferred_element_type=F32)
    o = inter + _bdot(a.astype(BF16), v.astype(BF16))

    btot = b[0:1] if rev else b[c - 1:c]
    kd = (k * jnp.exp(btot - b)).astype(BF16)
    st_ref[...] = st * jnp.exp(btot) + lax.dot_general(v.astype(BF16), kd, _TN,
                                                       preferred_element_type=F32)
    return o


def _gla_kernel(*refs, n_tok, has_state):
    if has_state:
        (q_ref, k_ref, v_ref, r_ref, w2_ref, bg_ref, gn_ref, s0f_ref, s0b_ref,
         o_ref, lg_ref, of_ref, ob_ref, stf_ref, stb_ref) = refs
    else:
        (q_ref, k_ref, v_ref, r_ref, w2_ref, bg_ref, gn_ref,
         o_ref, sf_ref, sb_ref, lg_ref, of_ref, ob_ref, stf_ref, stb_ref) = refs
    nc = n_tok // GLA_C
    scale = GLA_DK ** -0.5

    rb = r_ref[...].astype(BF16)
    for z in range(2):
        x = _bdot(rb, w2_ref[z].astype(BF16)) + bg_ref[z]
        lg_ref[z] = (jnp.minimum(x, 0.0) - jnp.log1p(jnp.exp(-jnp.abs(x)))) * (1.0 / GLA_GATE_NORM)

    for hh in range(GLA_HPS):
        if has_state:
            stf_ref[hh] = s0f_ref[0, hh].T
            stb_ref[hh] = s0b_ref[0, hh].T
        else:
            stf_ref[hh] = jnp.zeros((GLA_DV, GLA_DK), F32)
            stb_ref[hh] = jnp.zeros((GLA_DV, GLA_DK), F32)

    masks = {rev: _gla_masks(rev) for rev in (False, True)}

    def step(ci, carry):
        for hh in range(GLA_HPS):
            kq = slice(hh * GLA_DK, (hh + 1) * GLA_DK)
            vv = slice(hh * GLA_DV, (hh + 1) * GLA_DV)
            for rev in (False, True):
                cc = (nc - 1 - ci) if rev else ci
                sl = pl.ds(pl.multiple_of(cc * GLA_C, GLA_C), GLA_C)
                o = _gla_chunk(q_ref[sl, kq] * scale, k_ref[sl, kq], v_ref[sl, vv],
                               lg_ref[1 if rev else 0, sl, kq],
                               (stb_ref if rev else stf_ref).at[hh], rev, masks[rev])
                (ob_ref if rev else of_ref)[sl, vv] = o
        return carry

    lax.fori_loop(0, nc, step, 0)

    for hh in range(GLA_HPS):
        vv = slice(hh * GLA_DV, (hh + 1) * GLA_DV)
        o = of_ref[:, vv] + ob_ref[:, vv]
        o_ref[:, vv] = o * lax.rsqrt(jnp.mean(o * o, axis=-1, keepdims=True) + EPS) * gn_ref[...]
        if not has_state:
            sf_ref[0, hh] = stf_ref[hh].T
            sb_ref[0, hh] = stb_ref[hh].T


def _gla(proj, w2, bg, gnorm, n_seq, n_tok, row_block0, states=None):
    kw, vw = GLA_HPS * GLA_DK, GLA_HPS * GLA_DV
    spec = lambda width, off: pl.BlockSpec((n_tok, width), lambda b, h: (row_block0 + b, off + h))
    in_specs = [spec(kw, 0), spec(kw, GLA_HK // kw), spec(vw, 2 * GLA_HK // vw),
                pl.BlockSpec((n_tok, 128), lambda b, h: (row_block0 + b, (2 * GLA_HK + 2 * GLA_HV) // 128)),
                pl.BlockSpec((2, 128, kw), lambda b, h: (0, 0, h)),
                pl.BlockSpec((2, 1, kw), lambda b, h: (0, 0, h)),
                pl.BlockSpec((1, GLA_DV), lambda b, h: (0, 0))]
    args = [proj, proj, proj, proj, w2, bg, gnorm.reshape(1, GLA_DV)]
    st_spec = pl.BlockSpec((1, GLA_HPS, GLA_DK, GLA_DV), lambda b, h: (b, h, 0, 0))
    o_spec = pl.BlockSpec((n_tok, vw), lambda b, h: (b, h))
    o_shape = jax.ShapeDtypeStruct((n_seq * n_tok, GLA_HV), F32)
    if states is not None:
        in_specs += [st_spec, st_spec]
        args += list(states)
        out_specs, out_shape = o_spec, o_shape
    else:
        st_shape = jax.ShapeDtypeStruct((n_seq, GLA_HEADS, GLA_DK, GLA_DV), F32)
        out_specs, out_shape = [o_spec, st_spec, st_spec], [o_shape, st_shape, st_shape]
    return pl.pallas_call(
        functools.partial(_gla_kernel, n_tok=n_tok, has_state=states is not None),
        grid=(n_seq, GLA_HEADS // GLA_HPS),
        in_specs=in_specs,
        out_specs=out_specs,
        out_shape=out_shape,
        scratch_shapes=[pltpu.VMEM((2, n_tok, kw), F32),
                        pltpu.VMEM((n_tok, vw), F32),
                        pltpu.VMEM((n_tok, vw), F32),
                        pltpu.VMEM((GLA_HPS, GLA_DV, GLA_DK), F32),
                        pltpu.VMEM((GLA_HPS, GLA_DV, GLA_DK), F32)],
        compiler_params=_params(("arbitrary", "arbitrary")),
        name="gla_latent" if states is not None else "gla_prompt",
    )(*args)


DIFF_HB = 2
DIFF_QL = DIFF_HB * DIFF_HD
DIFF_VL = DIFF_HB * 2 * DIFF_HD
DIFF_TQ = 512
DIFF_SUB = 128


def _diff_lambda(lam_ref):
    l = lam_ref[...]
    a = jnp.sum(l[0:1] * l[1:2], axis=-1, keepdims=True)
    b = jnp.sum(l[2:3] * l[3:4], axis=-1, keepdims=True)
    return jnp.exp(a) - jnp.exp(b) + DIFF_LAMBDA_INIT


def _diff_finish(ps, invs, lam, v, sn_ref):
    a = ps[0] * invs[0] - (lam * invs[1]) * ps[1]
    o2 = _bdot(a.astype(BF16), v)
    o = _unstack_heads(o2, DIFF_HB, 2 * DIFF_HD)
    return _group_rms(o, sn_ref[...], 2 * DIFF_HD) * (1.0 - DIFF_LAMBDA_INIT)


def _diff_prompt_kernel(q0_ref, q1_ref, k0_ref, k1_ref, v_ref, gq_ref, gk_ref, lam_ref, sn_ref,
                        o_ref, kn_ref, vn_ref):
    scale = DIFF_HD ** -0.5 * LOG2E
    lam = _diff_lambda(lam_ref)

    def chains(bound):
        shift = None if bound is None else _stacked_rows(bound, SEQ)
        for seq in range(PROMPT_SEQS):
            sl = slice(seq * SEQ, (seq + 1) * SEQ)
            ps, invs = [], []
            for comp, (q_ref, k_ref) in enumerate(((q0_ref, k0_ref), (q1_ref, k1_ref))):
                q = _group_rms(q_ref[sl, :], gq_ref[...], DIFF_HD, sums_on_mxu=True) * scale
                k = _group_rms(k_ref[sl, :], gk_ref[...], DIFF_HD, sums_on_mxu=True)
                for h in range(DIFF_HB):
                    kn_ref[seq, comp, h] = k[:, h * DIFF_HD:(h + 1) * DIFF_HD]
                q2 = _stack_heads(q, DIFF_HB, DIFF_HD).astype(BF16)
                s = lax.dot_general(q2, k.astype(BF16), _NT, preferred_element_type=F32)
                (p,), inv = _softmax_parts([s], shift)
                ps.append(p)
                invs.append(inv)
            v = v_ref[sl, :]
            _store_heads(vn_ref, seq, v, DIFF_HB, 2 * DIFF_HD)
            o_ref[sl, :] = _diff_finish(ps, invs, lam, v.astype(BF16), sn_ref)

    _with_score_bound(scale * _rms_norm_bound(gq_ref, DIFF_HD, DIFF_HD)
                      * _rms_norm_bound(gk_ref, DIFF_HD, DIFF_HD), chains)


def _diff_prompt(qkv, gq, gk, lam, sub_norm):
    nb = DIFF_HEADS // DIFF_HB
    rows = PROMPT_SEQS * SEQ
    qk = lambda off: pl.BlockSpec((rows, DIFF_QL), lambda b, j: (b, off + j))
    vec = lambda n: pl.BlockSpec((1, n), lambda b, j: (0, 0))
    v_spec = pl.BlockSpec((rows, DIFF_VL), lambda b, j: (b, 2 * D_MODEL // DIFF_VL + j))
    kn_spec = pl.BlockSpec((PROMPT_SEQS, 2, DIFF_HB, SEQ, DIFF_HD), lambda b, j: (b, 0, j, 0, 0))
    vn_spec = pl.BlockSpec((PROMPT_SEQS, DIFF_HB, SEQ, 2 * DIFF_HD), lambda b, j: (b, j, 0, 0))
    return pl.pallas_call(
        _diff_prompt_kernel,
        grid=(BATCH // PROMPT_SEQS, nb),
        in_specs=[qk(0), qk(nb), qk(2 * nb), qk(3 * nb), v_spec, vec(DIFF_QL), vec(DIFF_QL),
                  pl.BlockSpec((4, DIFF_HD), lambda b, j: (0, 0)), vec(DIFF_VL)],
        out_specs=[pl.BlockSpec((rows, DIFF_VL), lambda b, j: (b, j)), kn_spec, vn_spec],
        out_shape=[jax.ShapeDtypeStruct((N_PROMPT, D_MODEL), F32),
                   jax.ShapeDtypeStruct((BATCH, 2, DIFF_HEADS, SEQ, DIFF_HD), F32),
                   jax.ShapeDtypeStruct((BATCH, DIFF_HEADS, SEQ, 2 * DIFF_HD), F32)],
        compiler_params=_params(("arbitrary", "arbitrary")),
        name="diff_prompt",
    )(qkv, qkv, qkv, qkv, qkv, jnp.tile(gq, DIFF_HB).reshape(1, -1),
      jnp.tile(gk, DIFF_HB).reshape(1, -1), lam, jnp.tile(sub_norm, DIFF_HB).reshape(1, -1))


def _diff_latent_kernel(q0_ref, q1_ref, k0_ref, k1_ref, v_ref, kc0_ref, kc1_ref, vc_ref,
                        cos_ref, sin_ref, cosq_ref, sinq_ref, gq_ref, gk_ref, lam_ref, sn_ref,
                        o_ref, kb0_ref, kb1_ref, vb_ref, kmax_ref):
    scale = DIFF_HD ** -0.5 * LOG2E
    half = DIFF_HD // 4
    rows = 256

    @pl.when(pl.program_id(2) == 0)
    def _():
        for comp, (k_ref, kc_ref, kb_ref) in enumerate(
                ((k0_ref, kc0_ref, kb0_ref), (k1_ref, kc1_ref, kb1_ref))):
            kc = kc_ref[0]
            kb_ref[0:PAST_LEN, :] = kc.astype(BF16)
            kmax_ref[comp:comp + 1, :] = jnp.maximum(
                _max_head_norms(kc, DIFF_HD), _rms_norm_bound(gk_ref, DIFF_HD, DIFF_HD))

            def prep(r, carry):
                sl = pl.ds(pl.multiple_of(r * rows, rows), rows)
                k = _group_rms(k_ref[sl, :], gk_ref[...], DIFF_HD)
                k = _rope(k, cos_ref[sl, :], sin_ref[sl, :], half)
                kb_ref[pl.ds(pl.multiple_of(PAST_LEN + r * rows, rows), rows), :] = k.astype(BF16)
                return carry

            lax.fori_loop(0, DEC_SEQ // rows, prep, 0)
        vb_ref[0:PAST_LEN, :] = vc_ref[0].astype(BF16)
        vb_ref[PAST_LEN:, :] = v_ref[...].astype(BF16)

    lam = _diff_lambda(lam_ref)

    def chains(bound):
        for r0 in range(0, DIFF_TQ, DIFF_SUB):
            sl = slice(r0, r0 + DIFF_SUB)
            ps, invs = [], []
            for comp, (q_ref, kb_ref) in enumerate(((q0_ref, kb0_ref), (q1_ref, kb1_ref))):
                shift = None if bound is None else _stacked_rows(
                    bound[:, comp * DIFF_HB:(comp + 1) * DIFF_HB], DIFF_SUB)
                q = _group_rms(q_ref[sl, :], gq_ref[...], DIFF_HD)
                q = _rope(q, cosq_ref[sl, :], sinq_ref[sl, :], half) * scale
                q2 = _stack_heads(q, DIFF_HB, DIFF_HD).astype(BF16)
                s = lax.dot_general(q2, kb_ref[...], _NT, preferred_element_type=F32)
                (p,), inv = _softmax_parts([s], shift)
                ps.append(p)
                invs.append(inv)
            o_ref[sl, :] = _diff_finish(ps, invs, lam, vb_ref[...], sn_ref)

    qmax = scale * _rms_norm_bound(gq_ref, DIFF_HD, DIFF_HD)
    _with_score_bound(jnp.concatenate([qmax * kmax_ref[0:1, :], qmax * kmax_ref[1:2, :]], axis=-1),
                      chains)


def _diff_latent(qkv, cache_k, cache_v, cos, sin, gq, gk, lam, sub_norm):
    nb = DIFF_HEADS // DIFF_HB
    nq = DEC_SEQ // DIFF_TQ
    q0 = N_PROMPT // DIFF_TQ
    lat0 = N_PROMPT // DEC_SEQ
    n_keys = PAST_LEN + DEC_SEQ
    q_spec = lambda off: pl.BlockSpec((DIFF_TQ, DIFF_QL), lambda b, j, t: (q0 + b * nq + t, off + j))
    k_spec = lambda off: pl.BlockSpec((DEC_SEQ, DIFF_QL), lambda b, j, t: (lat0 + b, off + j))
    v_spec = pl.BlockSpec((DEC_SEQ, DIFF_VL), lambda b, j, t: (lat0 + b, 2 * D_MODEL // DIFF_VL + j))
    kc_spec = lambda off: pl.BlockSpec((1, PAST_LEN, DIFF_QL), lambda b, j, t: (b, 0, off + j))
    vc_spec = pl.BlockSpec((1, PAST_LEN, DIFF_VL), lambda b, j, t: (b, 0, j))
    tab = pl.BlockSpec((DEC_SEQ, DIFF_QL), lambda b, j, t: (0, 0))
    tabq = pl.BlockSpec((DIFF_TQ, DIFF_QL), lambda b, j, t: (t, 0))
    vec = lambda n: pl.BlockSpec((1, n), lambda b, j, t: (0, 0))
    return pl.pallas_call(
        _diff_latent_kernel,
        grid=(DEC_BATCH, nb, nq),
        in_specs=[q_spec(0), q_spec(nb), k_spec(2 * nb), k_spec(3 * nb), v_spec,
                  kc_spec(0), kc_spec(nb), vc_spec, tab, tab, tabq, tabq,
                  vec(DIFF_QL), vec(DIFF_QL),
                  pl.BlockSpec((4, DIFF_HD), lambda b, j, t: (0, 0)), vec(DIFF_VL)],
        out_specs=pl.BlockSpec((DIFF_TQ, DIFF_VL), lambda b, j, t: (b * nq + t, j)),
        out_shape=jax.ShapeDtypeStruct((N_LATENT, D_MODEL), F32),
        scratch_shapes=[pltpu.VMEM((n_keys, DIFF_QL), BF16),
                        pltpu.VMEM((n_keys, DIFF_QL), BF16),
                        pltpu.VMEM((n_keys, DIFF_VL), BF16),
                        pltpu.VMEM((2, DIFF_HB), F32)],
        compiler_params=_params(("arbitrary", "arbitrary", "arbitrary")),
        name="diff_latent",
    )(qkv, qkv, qkv, qkv, qkv, _tokens_first(cache_k), _tokens_first(cache_k),
      _tokens_first(cache_v), cos, sin, cos, sin,
      jnp.tile(gq, DIFF_HB).reshape(1, -1), jnp.tile(gk, DIFF_HB).reshape(1, -1), lam,
      jnp.tile(sub_norm, DIFF_HB).reshape(1, -1))


MLA_HB = 2
MLA_HL = 128
MLA_LANES = MLA_HB * MLA_HL
MLA_TQ = 512
MLA_SUB = 128


def _mla_keys(kv, kr, gk, sums_on_mxu=False):
    lane = lax.broadcasted_iota(jnp.int32, kv.shape, 1)
    kr2 = jnp.concatenate([kr] * MLA_HB, axis=1)
    k = jnp.where((lane & (MLA_HL - 1)) < MLA_NOPE, kv, kr2)
    return _group_rms(k, gk, MLA_HL, n_real=MLA_QK, sums_on_mxu=sums_on_mxu)


def _mla_out(o2):
    tq = o2.shape[0] // MLA_HB
    oa = pltpu.roll(o2[0:tq, 0:MLA_HL], MLA_HL - MLA_V, axis=1)
    ob = o2[tq:, MLA_HL:]
    lane = lax.broadcasted_iota(jnp.int32, oa.shape, 1)
    return jnp.where(lane < MLA_V, oa, ob)


def _mla_prompt_kernel(q_ref, kv_ref, kr_ref, gq_ref, gk_ref, o_ref):
    scale = MLA_QK ** -0.5 * LOG2E

    def chains(bound):
        shift = None if bound is None else _stacked_rows(bound, SEQ)
        for seq in range(PROMPT_SEQS):
            sl = slice(seq * SEQ, (seq + 1) * SEQ)
            q = _group_rms(q_ref[sl, :], gq_ref[...], MLA_HL, n_real=MLA_QK) * scale
            kv = kv_ref[sl, :]
            k = _mla_keys(kv, kr_ref[sl, :], gk_ref[...])
            q2 = _stack_heads(q, MLA_HB, MLA_HL).astype(BF16)
            s = lax.dot_general(q2, k.astype(BF16), _NT, preferred_element_type=F32)
            (p,), inv = _softmax_parts([s], shift)
            o_ref[sl, :] = _mla_out(_bdot(p.astype(BF16), kv.astype(BF16)) * inv)

    _with_score_bound(scale * _rms_norm_bound(gq_ref, MLA_QK, MLA_HL)
                      * _rms_norm_bound(gk_ref, MLA_QK, MLA_HL), chains)


def _mla_prompt(qp, kvp, low, gq, gk):
    nb = MLA_HEADS // MLA_HB
    rows = PROMPT_SEQS * SEQ
    blk = pl.BlockSpec((rows, MLA_LANES), lambda b, j: (b, j))
    vec = pl.BlockSpec((1, MLA_LANES), lambda b, j: (0, 0))
    return pl.pallas_call(
        _mla_prompt_kernel,
        grid=(BATCH // PROMPT_SEQS, nb),
        in_specs=[blk, blk, pl.BlockSpec((rows, MLA_HL), lambda b, j: (b, 0)),
                  vec, vec],
        out_specs=pl.BlockSpec((rows, MLA_HB * MLA_V), lambda b, j: (b, j)),
        out_shape=jax.ShapeDtypeStruct((N_PROMPT, MLA_HEADS * MLA_V), F32),
        compiler_params=_params(("arbitrary", "arbitrary")),
        name="mla_prompt",
    )(qp, kvp, low, gq, gk)


def _mla_latent_kernel(q_ref, kv_ref, kr_ref, kvc_ref, krc_ref, cos_ref, sin_ref, cosq_ref,
                       sinq_ref, gq_ref, gk_ref, o_ref, kb_ref, vb_ref, kmax_ref):
    scale = MLA_QK ** -0.5 * LOG2E
    half = MLA_ROPE // 4
    rows = 256

    @pl.when(pl.program_id(2) == 0)
    def _():
        kvc = kvc_ref[...]
        kc = _mla_keys(kvc, krc_ref[...], gk_ref[...])
        kb_ref[0:PAST_LEN, :] = kc.astype(BF16)
        vb_ref[0:PAST_LEN, :] = kvc.astype(BF16)

        def prep(r, carry):
            sl = pl.ds(pl.multiple_of(r * rows, rows), rows)
            dst = pl.ds(pl.multiple_of(PAST_LEN + r * rows, rows), rows)
            kv = kv_ref[sl, :]
            k = _mla_keys(kv, kr_ref[sl, :], gk_ref[...], sums_on_mxu=True)
            k = _rope(k, cos_ref[sl, :], sin_ref[sl, :], half)
            kb_ref[dst, :] = k.astype(BF16)
            vb_ref[dst, :] = kv.astype(BF16)
            return carry

        lax.fori_loop(0, DEC_SEQ // rows, prep, 0)
        kmax_ref[...] = jnp.maximum(_max_head_norms(kc, MLA_HL),
                                    _rms_norm_bound(gk_ref, MLA_QK, MLA_HL))

    def chains(bound):
        shift = None if bound is None else _stacked_rows(bound, MLA_SUB)
        for r0 in range(0, MLA_TQ, MLA_SUB):
            sl = slice(r0, r0 + MLA_SUB)
            q = _group_rms(q_ref[sl, :], gq_ref[...], MLA_HL, n_real=MLA_QK)
            q = _rope(q, cosq_ref[sl, :], sinq_ref[sl, :], half) * scale
            q2 = _stack_heads(q, MLA_HB, MLA_HL).astype(BF16)
            s = lax.dot_general(q2, kb_ref[...], _NT, preferred_element_type=F32)
            (p,), inv = _softmax_parts([s], shift)
            o_ref[sl, :] = _mla_out(_bdot(p.astype(BF16), vb_ref[...]) * inv)

    _with_score_bound(scale * _rms_norm_bound(gq_ref, MLA_QK, MLA_HL) * kmax_ref[...], chains)


def _mla_latent(qp, kvp, low, kvc, krc, cos, sin, gq, gk):
    nb = MLA_HEADS // MLA_HB
    nq = DEC_SEQ // MLA_TQ
    q0 = N_PROMPT // MLA_TQ
    lat0 = N_PROMPT // DEC_SEQ
    n_keys = PAST_LEN + DEC_SEQ
    tab = pl.BlockSpec((DEC_SEQ, MLA_LANES), lambda b, j, t: (0, 0))
    tabq = pl.BlockSpec((MLA_TQ, MLA_LANES), lambda b, j, t: (t, 0))
    vec = pl.BlockSpec((1, MLA_LANES), lambda b, j, t: (0, 0))
    return pl.pallas_call(
        _mla_latent_kernel,
        grid=(DEC_BATCH, nb, nq),
        in_specs=[pl.BlockSpec((MLA_TQ, MLA_LANES), lambda b, j, t: (q0 + b * nq + t, j)),
                  pl.BlockSpec((DEC_SEQ, MLA_LANES), lambda b, j, t: (lat0 + b, j)),
                  pl.BlockSpec((DEC_SEQ, MLA_HL), lambda b, j, t: (lat0 + b, 0)),
                  pl.BlockSpec((PAST_LEN, MLA_LANES), lambda b, j, t: (b, j)),
                  pl.BlockSpec((PAST_LEN, MLA_HL), lambda b, j, t: (b, 0)),
                  tab, tab, tabq, tabq, vec, vec],
        out_specs=pl.BlockSpec((MLA_TQ, MLA_HB * MLA_V), lambda b, j, t: (b * nq + t, j)),
        out_shape=jax.ShapeDtypeStruct((N_LATENT, MLA_HEADS * MLA_V), F32),
        scratch_shapes=[pltpu.VMEM((n_keys, MLA_LANES), BF16),
                        pltpu.VMEM((n_keys, MLA_LANES), BF16),
                        pltpu.VMEM((1, MLA_HB), F32)],
        compiler_params=_params(("arbitrary", "arbitrary", "arbitrary")),
        name="mla_latent",
    )(qp, kvp, low, kvc, krc, cos, sin, cos, sin, gq, gk)


MLA_LOW_Q = 0
MLA_LOW_KV = 512
MLA_LOW_KR = 768
MLA_LOW_N = 896


def _axial_tables(n_tok, rdim):
    nf = rdim // 4
    freqs = ROPE_BASE ** (-jnp.arange(nf, dtype=F32) / nf)
    t = jnp.arange(n_tok)
    rowp = (t // GRID_W).astype(F32)
    colp = (t % GRID_W).astype(F32)
    ang = jnp.stack([rowp[:, None] * freqs, colp[:, None] * freqs], axis=1)
    cos, sin = jnp.cos(ang), jnp.sin(ang)
    cos_l = jnp.stack([cos, cos], axis=2).reshape(n_tok, rdim)
    sin_l = jnp.stack([-sin, sin], axis=2).reshape(n_tok, rdim)
    return cos_l, sin_l


def _diff_rope_tables():
    cos, sin = _axial_tables(DEC_SEQ, DIFF_HD)
    return jnp.tile(cos, (1, DIFF_HB)), jnp.tile(sin, (1, DIFF_HB))


def _mla_rope_tables():
    cos, sin = _axial_tables(DEC_SEQ, MLA_ROPE)
    ones = jnp.ones((DEC_SEQ, MLA_NOPE), F32)
    pad1 = jnp.ones((DEC_SEQ, MLA_HL - MLA_QK), F32)
    cos_h = jnp.concatenate([ones, cos, pad1], axis=1)
    sin_h = jnp.concatenate([0 * ones, sin, 0 * pad1], axis=1)
    return jnp.tile(cos_h, (1, MLA_HB)), jnp.tile(sin_h, (1, MLA_HB))


def _tokens_first(cache):
    b, h, l, d = cache.shape
    return jnp.transpose(cache, (0, 2, 1, 3)).reshape(b, l, h * d)


def _pad_heads(w, heads, hd, hl):
    k = w.shape[0]
    return jnp.pad(w.reshape(k, heads, hd), ((0, 0), (0, 0), (0, hl - hd))).reshape(k, heads * hl)


def kernel(x_prompt, x_sample, cache_l0_k, cache_l0_v, state_l1_fwd, state_l1_bwd, cache_l2_k,
           cache_l2_v, cache_l3_ckv, cache_l3_krope, c, c_ctx, ada_w, ada_b, norm_mix, norm_ffn,
           ffn_w_up, ffn_conv_w, ffn_conv_b, ffn_w_down, na_w_qkv, na_q_norm, na_k_norm, na_bias,
           na_w_o, gla_w_qkvg, gla_w_gate1, gla_w_gate2, gla_b_gate, gla_o_norm, gla_w_o,
           diff_w_qkv, diff_q_norm, diff_k_norm, diff_lambda, diff_sub_norm, diff_w_o, mla_w_dq,
           mla_q_a_norm, mla_w_uq, mla_w_dkv, mla_kv_a_norm, mla_w_ukv, mla_q_norm, mla_k_norm,
           mla_w_o):
    xr = _Rows(x_prompt.reshape(N_PROMPT, D_MODEL), x_sample.reshape(N_LATENT, D_MODEL), 0)
    cvecs = jnp.concatenate([c_ctx[None], c, jnp.zeros((5, D_MODEL), F32)], axis=0)
    mods_all = _ada_mods(cvecs, ada_w, ada_b)
    halves = lambda o_p, o_s: _Rows(o_p, o_s, 0)

    mods = mods_all[0]
    qkv = _norm_mod_proj(xr, norm_mix[0], mods, [na_w_qkv], "na_qkv")
    o_p, new_l0_k, new_l0_v = _na_prompt(qkv, na_q_norm, na_k_norm)
    o_s = _na_latent(qkv, cache_l0_k, cache_l0_v, _na_bias_blocks(na_bias), na_q_norm, na_k_norm)
    x = _out_proj_residual(xr, halves(o_p, o_s), mods, na_w_o, "na_out")
    ffn_weights = (ffn_w_up, ffn_conv_w, ffn_conv_b, ffn_w_down)
    x = _ffn(x, norm_ffn[0], mods, 0, *ffn_weights)
    xr = _one_array(x)

    mods = mods_all[1]
    w_decay = jnp.concatenate(
        [gla_w_gate1[0], gla_w_gate1[1],
         jnp.zeros((D_MODEL, 128 - 2 * GLA_GATE_RANK), F32)], axis=1)
    proj = _norm_mod_proj(xr, norm_mix[1], mods, [gla_w_qkvg, w_decay], "gla_proj")
    w2 = jnp.zeros((2, 128, GLA_HK), F32)
    w2 = w2.at[0, :GLA_GATE_RANK].set(gla_w_gate2[0])
    w2 = w2.at[1, GLA_GATE_RANK:2 * GLA_GATE_RANK].set(gla_w_gate2[1])
    bg = gla_b_gate.reshape(2, 1, GLA_HK)
    o_p, new_l1_fwd, new_l1_bwd = _gla(proj, w2, bg, gla_o_norm, BATCH, SEQ, 0)
    o_s = _gla(proj, w2, bg, gla_o_norm, DEC_BATCH, DEC_SEQ, N_PROMPT // DEC_SEQ,
               states=(state_l1_fwd, state_l1_bwd))
    x = _out_proj_residual(xr, halves(o_p, o_s), mods, gla_w_o, "gla_out",
                           gate=proj, gate_col_block=(2 * GLA_HK + GLA_HV) // GLA_HV)
    x = _ffn(x, norm_ffn[1], mods, 1, *ffn_weights)
    xr = _one_array(x)

    mods = mods_all[2]
    qkv = _norm_mod_proj(xr, norm_mix[2], mods, [diff_w_qkv], "diff_qkv")
    o_p, kn_p, new_l2_v = _diff_prompt(qkv, diff_q_norm, diff_k_norm, diff_lambda, diff_sub_norm)
    new_l2_k = kn_p.reshape(BATCH, 2 * DIFF_HEADS, SEQ, DIFF_HD)
    cos_d, sin_d = _diff_rope_tables()
    o_s = _diff_latent(qkv, cache_l2_k, cache_l2_v, cos_d, sin_d, diff_q_norm, diff_k_norm,
                       diff_lambda, diff_sub_norm)
    x = _out_proj_residual(xr, halves(o_p, o_s), mods, diff_w_o, "diff_out")
    x = _ffn(x, norm_ffn[2], mods, 2, *ffn_weights)
    xr = _one_array(x)

    mods = mods_all[3]
    zc = lambda n: jnp.zeros((D_MODEL, n), F32)
    w_low = jnp.concatenate(
        [mla_w_dq, zc(MLA_LOW_KV - MLA_Q_RANK), mla_w_dkv[:, :MLA_KV_RANK],
         zc(MLA_NOPE), mla_w_dkv[:, MLA_KV_RANK:], zc(MLA_HL - MLA_QK)], axis=1)
    w_uq = _pad_heads(mla_w_uq, MLA_HEADS, MLA_QK, MLA_HL)
    qp, kvp, ckv, kr = _mla_front(xr, norm_mix[3], mods, w_low, mla_q_a_norm, w_uq,
                                  mla_kv_a_norm, mla_w_ukv)
    kvc = _matmul(cache_l3_ckv.reshape(DEC_BATCH * PAST_LEN, MLA_KV_RANK), mla_w_ukv,
                  "mla_ukv_cache")
    krc = jnp.pad(cache_l3_krope.reshape(DEC_BATCH * PAST_LEN, MLA_ROPE),
                  ((0, 0), (MLA_NOPE, MLA_HL - MLA_QK)))
    pad_gain = lambda g: jnp.tile(jnp.pad(g, (0, MLA_HL - MLA_QK)), MLA_HB).reshape(1, -1)
    gq, gk = pad_gain(mla_q_norm), pad_gain(mla_k_norm)
    o_p = _mla_prompt(qp, kvp, kr, gq, gk)
    cos_m, sin_m = _mla_rope_tables()
    o_s = _mla_latent(qp, kvp, kr, kvc, krc, cos_m, sin_m, gq, gk)
    new_l3_ckv = ckv[:N_PROMPT].reshape(BATCH, SEQ, MLA_KV_RANK)
    new_l3_krope = kr[:N_PROMPT, MLA_NOPE:MLA_QK].reshape(BATCH, SEQ, MLA_ROPE)
    x = _out_proj_residual(xr, halves(o_p, o_s), mods, mla_w_o, "mla_out")
    n_pt = N_PROMPT // TOK_TILE
    ffn3 = functools.partial(_ffn, x, norm_ffn[3], mods, 3, *ffn_weights)
    y_prompt = ffn3(tile0=0, n_tiles=n_pt).reshape(BATCH, SEQ, D_MODEL)
    y_sample = ffn3(tile0=n_pt, n_tiles=N_TOK_TILES - n_pt).reshape(DEC_BATCH, DEC_SEQ, D_MODEL)
    return (y_prompt, y_sample, new_l0_k, new_l0_v, new_l1_fwd, new_l1_bwd, new_l2_k, new_l2_v,
            new_l3_ckv, new_l3_krope)
```

```python
import functools
import math
from typing import NamedTuple

import jax
import jax.numpy as jnp
from jax import lax
from jax.experimental import pallas as pl
from jax.experimental.pallas import tpu as pltpu

F32 = jnp.float32
BF16 = jnp.bfloat16

D_MODEL = 1024
BATCH = 16
SEQ = 256
DEPTH = 4
DEC_BATCH = 2
DEC_SEQ = 2048
PAST_LEN = 256
GRID_W = 64
D_FF = 2816
EPS = 1e-6
ROPE_BASE = 10000.0

NA_HEADS = 16
NA_HD = 64
NA_WIN_R = 8
NA_WIN_C = 16

GLA_HEADS = 4
GLA_DK = 128
GLA_DV = 256
GLA_HK = GLA_HEADS * GLA_DK
GLA_HV = GLA_HEADS * GLA_DV
GLA_GATE_RANK = 16
GLA_GATE_NORM = 16.0

DIFF_HEADS = 8
DIFF_HD = 64
DIFF_LAMBDA_INIT = 0.8 - 0.6 * math.exp(-0.3 * 2)

MLA_HEADS = 16
MLA_Q_RANK = 384
MLA_KV_RANK = 256
MLA_NOPE = 64
MLA_ROPE = 32
MLA_V = 64
MLA_QK = MLA_NOPE + MLA_ROPE

N_PROMPT = BATCH * SEQ
N_LATENT = DEC_BATCH * DEC_SEQ
N_TOK = N_PROMPT + N_LATENT
TOK_TILE = 2048
N_TOK_TILES = N_TOK // TOK_TILE
FF_CHUNK = 256
N_FF_CHUNKS = D_FF // FF_CHUNK
NEG = -1e30
LOG2E = math.log2(math.e)
SAFE_SCORE_BOUND = 48.0

VMEM_LIMIT = 56 * 1024 * 1024

_NT = (((1,), (1,)), ((), ()))
_TN = (((0,), (0,)), ((), ()))


def _params(sem, vmem=VMEM_LIMIT):
    return pltpu.CompilerParams(dimension_semantics=sem, vmem_limit_bytes=vmem)


def _log2(n):
    assert n & (n - 1) == 0
    return n.bit_length() - 1


def _silu(x):
    return x / (1.0 + jnp.exp(-x))


def _bdot(a, b):
    return jnp.dot(a, b, preferred_element_type=F32)


def _softmax_parts(parts, shift=None):
    m = shift
    if m is None:
        m = parts[0].max(axis=-1, keepdims=True)
        for s in parts[1:]:
            m = jnp.maximum(m, s.max(axis=-1, keepdims=True))
    ps = [jnp.exp2(s - m) for s in parts]
    l = ps[0].sum(axis=-1, keepdims=True)
    for p in ps[1:]:
        l = l + p.sum(axis=-1, keepdims=True)
    return ps, 1.0 / l


def _group_rms(x, gain, group, n_real=None, sums_on_mxu=False):
    lanes = x.shape[-1]
    n_real = n_real or group
    x2 = x * x
    if group == lanes:
        ms = jnp.sum(x2, axis=-1, keepdims=True)
    elif not sums_on_mxu:
        gid = lax.broadcasted_iota(jnp.int32, x.shape, 1) >> _log2(group)
        ms = jnp.zeros_like(x)
        for i in range(lanes // group):
            sel = gid == i
            si = jnp.sum(jnp.where(sel, x2, 0.0), axis=-1, keepdims=True)
            ms = jnp.where(sel, si, ms)
    else:
        r = lax.broadcasted_iota(jnp.int32, (lanes, lanes), 0) >> _log2(group)
        c = lax.broadcasted_iota(jnp.int32, (lanes, lanes), 1) >> _log2(group)
        ones = jnp.where(r == c, 1.0, 0.0).astype(BF16)
        hi = x2.astype(BF16)
        lo = (x2 - hi.astype(F32)).astype(BF16)
        ms = _bdot(hi, ones) + _bdot(lo, ones)
    return x * lax.rsqrt(ms * (1.0 / n_real) + EPS) * gain


def _with_score_bound(bound, body):
    ok = bound.max() < SAFE_SCORE_BOUND
    pl.when(ok)(lambda: body(bound))
    pl.when(jnp.logical_not(ok))(lambda: body(None))


def _rms_norm_bound(g_ref, n, head_lanes):
    g = jnp.abs(g_ref[...])
    heads = g.shape[-1] // head_lanes
    return n ** 0.5 * jnp.concatenate(
        [g[:, h * head_lanes:(h + 1) * head_lanes].max(axis=-1, keepdims=True)
         for h in range(heads)], axis=-1)


def _stacked_rows(per_head, rows):
    r = lax.broadcasted_iota(jnp.int32, (per_head.shape[-1] * rows, 1), 0)
    out = per_head[:, 0:1]
    for h in range(1, per_head.shape[-1]):
        out = jnp.where(r >= h * rows, per_head[:, h:h + 1], out)
    return out


def _max_head_norms(x, head_lanes):
    hid = lax.broadcasted_iota(jnp.int32, x.shape, 1) >> _log2(head_lanes)
    x2 = x * x
    sq = [jnp.sum(jnp.where(hid == h, x2, 0.0), axis=-1, keepdims=True).max(axis=0, keepdims=True)
          for h in range(x.shape[-1] // head_lanes)]
    return jnp.sqrt(jnp.concatenate(sq, axis=-1))


def _rope(x, cos, sin, half):
    lanes = x.shape[-1]
    lane = lax.broadcasted_iota(jnp.int32, x.shape, 1)
    up = pltpu.roll(x, lanes - half, axis=1)
    dn = pltpu.roll(x, half, axis=1)
    swapped = jnp.where((lane & (2 * half - 1)) < half, up, dn)
    return x * cos + swapped * sin


def _stack_heads(q, n_heads, head_lanes):
    hid = lax.broadcasted_iota(jnp.int32, q.shape, 1) >> _log2(head_lanes)
    zero = jnp.zeros_like(q)
    return jnp.concatenate([jnp.where(hid == i, q, zero) for i in range(n_heads)], axis=0)


def _unstack_heads(o, n_heads, head_lanes):
    rows = o.shape[0] // n_heads
    hid = lax.broadcasted_iota(jnp.int32, (rows, o.shape[1]), 1) >> _log2(head_lanes)
    out = o[0:rows]
    for i in range(1, n_heads):
        out = jnp.where(hid == i, o[i * rows:(i + 1) * rows], out)
    return out


ADA_TN = 1536


def _ada_kernel(c_ref, w_ref, b_ref, o_ref):
    s = _silu(c_ref[...])
    w = w_ref[0]
    s_hi, w_hi = s.astype(BF16), w.astype(BF16)
    s_lo = (s - s_hi.astype(F32)).astype(BF16)
    w_lo = (w - w_hi.astype(F32)).astype(BF16)
    both = _bdot(jnp.concatenate([s_hi, s_lo], axis=0), w_hi)
    o_ref[0] = both[0:8] + both[8:16] + _bdot(s_hi, w_lo) + b_ref[0]


def _ada_mods(cvecs, ada_w, ada_b):
    out = pl.pallas_call(
        _ada_kernel,
        grid=(DEPTH, 6 * D_MODEL // ADA_TN),
        in_specs=[pl.BlockSpec((8, D_MODEL), lambda l, j: (0, 0)),
                  pl.BlockSpec((1, D_MODEL, ADA_TN), lambda l, j: (l, 0, j)),
                  pl.BlockSpec((1, 1, ADA_TN), lambda l, j: (l, 0, j))],
        out_specs=pl.BlockSpec((1, 8, ADA_TN), lambda l, j: (l, 0, j)),
        out_shape=jax.ShapeDtypeStruct((DEPTH, 8, 6 * D_MODEL), F32),
        compiler_params=_params(("arbitrary", "arbitrary")),
        name="ada_mod",
    )(cvecs, ada_w, ada_b.reshape(DEPTH, 1, 6 * D_MODEL))
    return out.reshape(DEPTH, 8, 6, D_MODEL)[:, :3]


def _mod_group_of_tile(i):
    return jnp.maximum(i - (N_PROMPT // TOK_TILE - 1), 0)


def _norm_mod_rows(x_ref, g_ref, mod_ref, h_ref, shift_idx, scale_idx, rows=64):
    g = g_ref[...]
    sc = 1.0 + mod_ref[0, scale_idx:scale_idx + 1, :]
    sh = mod_ref[0, shift_idx:shift_idx + 1, :]

    def body(r, carry):
        sl = pl.ds(pl.multiple_of(r * rows, rows), rows)
        xf = x_ref[sl, :]
        ms = jnp.mean(xf * xf, axis=-1, keepdims=True)
        y = xf * lax.rsqrt(ms + EPS) * g
        h_ref[sl, :] = (y * sc + sh).astype(BF16)
        return carry

    lax.fori_loop(0, x_ref.shape[0] // rows, body, 0)


ROW_TM = 512


class _Rows(NamedTuple):
    prompt: jax.Array
    latent: jax.Array
    latent_row0: int


def _one_array(x):
    return _Rows(x, x, N_PROMPT)


def _row_specs(rows, width, col_block=0):
    n_p = N_PROMPT // ROW_TM
    l0 = rows.latent_row0 // ROW_TM
    return [pl.BlockSpec((ROW_TM, width), lambda t: (jnp.minimum(t, n_p - 1), col_block)),
            pl.BlockSpec((ROW_TM, width), lambda t: (l0 + jnp.maximum(t - n_p, 0), col_block))]


def _row_group(t):
    first_latent = N_PROMPT // ROW_TM
    return jnp.where(t < first_latent, 0, 1 + (t - first_latent) // (DEC_SEQ // ROW_TM))


def _is_prompt_tile():
    return pl.program_id(0) < N_PROMPT // ROW_TM


def _proj_kernel(xp_ref, xl_ref, g_ref, mod_ref, *refs):
    *w_refs, o_ref, h_ref, wb_ref = refs

    @pl.when(pl.program_id(0) == 0)
    def _():
        off = 0
        for w_ref in w_refs:
            wb_ref[:, off:off + w_ref.shape[1]] = w_ref[...].astype(BF16)
            off += w_ref.shape[1]

    is_prompt = _is_prompt_tile()
    g = g_ref[...]
    sc = 1.0 + mod_ref[0, 1:2, :]
    sh = mod_ref[0, 0:1, :]
    rows = 64
    part = ROW_TM // 2
    for p0 in range(0, ROW_TM, part):
        for r0 in range(p0, p0 + part, rows):
            sl = slice(r0, r0 + rows)
            xf = jnp.where(is_prompt, xp_ref[sl, :], xl_ref[sl, :])
            ms = jnp.mean(xf * xf, axis=-1, keepdims=True)
            y = xf * lax.rsqrt(ms + EPS) * g
            h_ref[sl, :] = (y * sc + sh).astype(BF16)
        o_ref[p0:p0 + part, :] = _bdot(h_ref[p0:p0 + part, :], wb_ref[...])


def _norm_mod_proj(x, g, mods, ws, name):
    n = sum(w.shape[1] for w in ws)
    return pl.pallas_call(
        _proj_kernel,
        grid=(N_TOK // ROW_TM,),
        in_specs=_row_specs(x, D_MODEL) + [
            pl.BlockSpec((1, D_MODEL), lambda t: (0, 0)),
            pl.BlockSpec((1, 6, D_MODEL), lambda t: (_row_group(t), 0, 0))] + [
            pl.BlockSpec(w.shape, lambda t: (0, 0), pipeline_mode=pl.Buffered(1)) for w in ws],
        out_specs=pl.BlockSpec((ROW_TM, n), lambda t: (t, 0)),
        out_shape=jax.ShapeDtypeStruct((N_TOK, n), F32),
        scratch_shapes=[pltpu.VMEM((ROW_TM, D_MODEL), BF16),
                        pltpu.VMEM((D_MODEL, n), BF16)],
        compiler_params=_params(("arbitrary",)),
        name=name,
    )(x.prompt, x.latent, g.reshape(1, D_MODEL), mods, *ws)


def _rms(a, g):
    return a * lax.rsqrt(jnp.mean(a * a, axis=-1, keepdims=True) + EPS) * g


def _mla_front_kernel(xp_ref, xl_ref, g_ref, mod_ref, wl_ref, gqa_ref, wuq_ref, gkv_ref, wukv_ref,
                      qp_ref, kvp_ref, ckv_ref, kr_ref, h_ref, wlb_ref, wuqb_ref, wukvb_ref):
    @pl.when(pl.program_id(0) == 0)
    def _():
        wlb_ref[...] = wl_ref[...].astype(BF16)
        wuqb_ref[...] = wuq_ref[...].astype(BF16)
        wukvb_ref[...] = wukv_ref[...].astype(BF16)

    is_prompt = _is_prompt_tile()
    g = g_ref[...]
    sc = 1.0 + mod_ref[0, 1:2, :]
    sh = mod_ref[0, 0:1, :]
    rows = 64
    part = ROW_TM // 2
    for p0 in range(0, ROW_TM, part):
        for r0 in range(p0, p0 + part, rows):
            sl = slice(r0, r0 + rows)
            xf = jnp.where(is_prompt, xp_ref[sl, :], xl_ref[sl, :])
            h_ref[sl, :] = (_rms(xf, g) * sc + sh).astype(BF16)
        ps = slice(p0, p0 + part)
        low = _bdot(h_ref[ps, :], wlb_ref[...])
        qa = _rms(low[:, MLA_LOW_Q:MLA_LOW_Q + MLA_Q_RANK], gqa_ref[...])
        qp_ref[ps, :] = _bdot(qa.astype(BF16), wuqb_ref[...])
        ckv = _rms(low[:, MLA_LOW_KV:MLA_LOW_KV + MLA_KV_RANK], gkv_ref[...])
        ckv_ref[ps, :] = ckv
        kvp_ref[ps, :] = _bdot(ckv.astype(BF16), wukvb_ref[...])
        kr_ref[ps, :] = low[:, MLA_LOW_KR:MLA_LOW_KR + MLA_HL]


def _mla_front(x, g, mods, w_low, g_qa, w_uq, g_kva, w_ukv):
    n_q, n_kv = w_uq.shape[1], w_ukv.shape[1]
    const = lambda shape: pl.BlockSpec(shape, lambda t: (0, 0), pipeline_mode=pl.Buffered(1))
    out = lambda n: pl.BlockSpec((ROW_TM, n), lambda t: (t, 0))
    shape = lambda n: jax.ShapeDtypeStruct((N_TOK, n), F32)
    return pl.pallas_call(
        _mla_front_kernel,
        grid=(N_TOK // ROW_TM,),
        in_specs=_row_specs(x, D_MODEL) + [
            pl.BlockSpec((1, D_MODEL), lambda t: (0, 0)),
            pl.BlockSpec((1, 6, D_MODEL), lambda t: (_row_group(t), 0, 0)),
            const(w_low.shape), const((1, MLA_Q_RANK)), const(w_uq.shape),
            const((1, MLA_KV_RANK)), const(w_ukv.shape)],
        out_specs=[out(n_q), out(n_kv), out(MLA_KV_RANK), out(MLA_HL)],
        out_shape=[shape(n_q), shape(n_kv), shape(MLA_KV_RANK), shape(MLA_HL)],
        scratch_shapes=[pltpu.VMEM((ROW_TM, D_MODEL), BF16),
                        pltpu.VMEM(w_low.shape, BF16),
                        pltpu.VMEM(w_uq.shape, BF16),
                        pltpu.VMEM(w_ukv.shape, BF16)],
        compiler_params=_params(("arbitrary",)),
        name="mla_front",
    )(x.prompt, x.latent, g.reshape(1, D_MODEL), mods, w_low, g_qa.reshape(1, -1), w_uq,
      g_kva.reshape(1, -1), w_ukv)


def _matmul_kernel(a_ref, w_ref, o_ref):
    o_ref[...] = _bdot(a_ref[...].astype(BF16), w_ref[...].astype(BF16))


def _matmul(a, w, name):
    rows, n = a.shape[0], w.shape[1]
    return pl.pallas_call(
        _matmul_kernel,
        out_shape=jax.ShapeDtypeStruct((rows, n), F32),
        compiler_params=_params(()),
        name=name,
    )(a, w)


def _oproj_kernel(*refs, gated):
    if gated:
        xp_ref, xl_ref, ap_ref, al_ref, g_ref, mod_ref, w_ref, o_ref, wb_ref = refs
    else:
        xp_ref, xl_ref, ap_ref, al_ref, mod_ref, w_ref, o_ref, wb_ref = refs

    @pl.when(pl.program_id(0) == 0)
    def _():
        wb_ref[...] = w_ref[...].astype(BF16)

    is_prompt = _is_prompt_tile()
    a = jnp.where(is_prompt, ap_ref[...], al_ref[...])
    if gated:
        a = a * _silu(g_ref[...])
    y = _bdot(a.astype(BF16), wb_ref[...])
    x = jnp.where(is_prompt, xp_ref[...], xl_ref[...])
    o_ref[...] = x + mod_ref[0, 2:3, :] * y


def _out_proj_residual(x, a, mods, w, name, gate=None, gate_col_block=0):
    k = w.shape[0]
    in_specs = _row_specs(x, D_MODEL) + _row_specs(a, k)
    args = [x.prompt, x.latent, a.prompt, a.latent]
    if gate is not None:
        in_specs.append(pl.BlockSpec((ROW_TM, k), lambda t: (t, gate_col_block)))
        args.append(gate)
    in_specs += [pl.BlockSpec((1, 6, D_MODEL), lambda t: (_row_group(t), 0, 0)),
                 pl.BlockSpec((k, D_MODEL), lambda t: (0, 0))]
    args += [mods, w]
    return pl.pallas_call(
        functools.partial(_oproj_kernel, gated=gate is not None),
        grid=(N_TOK // ROW_TM,),
        in_specs=in_specs,
        out_specs=pl.BlockSpec((ROW_TM, D_MODEL), lambda t: (t, 0)),
        out_shape=jax.ShapeDtypeStruct((N_TOK, D_MODEL), F32),
        scratch_shapes=[pltpu.VMEM((k, D_MODEL), BF16)],
        compiler_params=_params(("arbitrary",)),
        name=name,
    )(*args)


FFN_MM_ROWS = 512
FFN_ROWS = 64
FFN_PAD = 8


def _ffn_kernel(x_ref, g_ref, mod_ref, wg_ref, wv_ref, cwg_ref, cwv_ref, cbg_ref, cbv_ref,
                wd_ref, o_ref, h_ref, u_ref, act_ref, wup_ref, wdn_ref, *, tile0):
    i = tile0 + pl.program_id(0)
    c = pl.program_id(1)
    fc = FF_CHUNK
    n = TOK_TILE // FFN_MM_ROWS
    seq_len = jnp.where(i < N_PROMPT // TOK_TILE, SEQ, DEC_SEQ)
    row = lax.broadcasted_iota(jnp.int32, (FFN_ROWS, 1), 0)
    taps = lambda cw_ref, cb_ref: [jnp.broadcast_to(cw_ref[j:j + 1, :], (FFN_ROWS, fc))
                                   for j in range(3)] + [
                                       jnp.broadcast_to(cb_ref[...], (FFN_ROWS, fc))]
    taps_g, taps_v = taps(cwg_ref, cbg_ref), taps(cwv_ref, cbv_ref)

    def up(u_ref, t):
        r0 = t * FFN_MM_ROWS
        u_ref[FFN_PAD + r0:FFN_PAD + r0 + FFN_MM_ROWS, :] = _bdot(
            h_ref[r0:r0 + FFN_MM_ROWS, :], wup_ref[...])

    def conv_act(u_ref, t):
        for r0 in range(t * FFN_MM_ROWS, (t + 1) * FFN_MM_ROWS, FFN_ROWS):
            halves = []
            for lo, (w0, w1, w2, bias) in ((0, taps_g), (fc, taps_v)):
                p0 = FFN_PAD + r0
                prev = u_ref[p0 - 1:p0 - 1 + FFN_ROWS, lo:lo + fc]
                mid = u_ref[p0:p0 + FFN_ROWS, lo:lo + fc]
                nxt = u_ref[p0 + 1:p0 + 1 + FFN_ROWS, lo:lo + fc]
                if r0 % SEQ == 0:
                    prev = jnp.where(((r0 + row) & (seq_len - 1)) == 0, 0.0, prev)
                if (r0 + FFN_ROWS) % SEQ == 0:
                    nxt = jnp.where(((r0 + row) & (seq_len - 1)) == seq_len - 1, 0.0, nxt)
                halves.append(prev * w0 + mid * w1 + nxt * w2 + bias)
            act_ref[r0:r0 + FFN_ROWS, :] = (_silu(halves[0]) * halves[1]).astype(BF16)

    def down(t):
        r0 = t * FFN_MM_ROWS
        o_ref[r0:r0 + FFN_MM_ROWS, :] += _bdot(act_ref[r0:r0 + FFN_MM_ROWS, :], wdn_ref[...])

    @pl.when(c == 0)
    def _():
        _norm_mod_rows(x_ref, g_ref, mod_ref, h_ref, 3, 4)
        zeros = jnp.zeros((FFN_PAD, 2 * fc), F32)
        u_ref[0:FFN_PAD, :] = zeros
        u_ref[FFN_PAD + TOK_TILE:, :] = zeros
        o_ref[...] = jnp.zeros_like(o_ref)

    wup_ref[:, :fc] = wg_ref[...].astype(BF16)
    wup_ref[:, fc:] = wv_ref[...].astype(BF16)
    wdn_ref[...] = wd_ref[...].astype(BF16)
    for s in range(n + 2):
        if s < n:
            up(u_ref, s)
        if 1 <= s <= n:
            conv_act(u_ref, s - 1)
        if s >= 2:
            down(s - 2)

    @pl.when(c == N_FF_CHUNKS - 1)
    def _():
        o_ref[...] = x_ref[...] + mod_ref[0, 5:6, :] * o_ref[...]


def _ffn(x, g, mods, layer, w_up, conv_w, conv_b, w_down, tile0=0, n_tiles=N_TOK_TILES):
    fc = FF_CHUNK
    ncb = N_FF_CHUNKS
    return pl.pallas_call(
        functools.partial(_ffn_kernel, tile0=tile0),
        grid=(n_tiles, ncb),
        in_specs=[pl.BlockSpec((TOK_TILE, D_MODEL), lambda i, c: (tile0 + i, 0)),
                  pl.BlockSpec((1, D_MODEL), lambda i, c: (0, 0)),
                  pl.BlockSpec((1, 6, D_MODEL), lambda i, c: (_mod_group_of_tile(tile0 + i), 0, 0)),
                  pl.BlockSpec((None, D_MODEL, fc), lambda i, c: (layer, 0, c)),
                  pl.BlockSpec((None, D_MODEL, fc), lambda i, c: (layer, 0, ncb + c)),
                  pl.BlockSpec((None, 3, fc), lambda i, c: (layer, 0, c)),
                  pl.BlockSpec((None, 3, fc), lambda i, c: (layer, 0, ncb + c)),
                  pl.BlockSpec((None, 1, fc), lambda i, c: (layer, 0, c)),
                  pl.BlockSpec((None, 1, fc), lambda i, c: (layer, 0, ncb + c)),
                  pl.BlockSpec((None, fc, D_MODEL), lambda i, c: (layer, c, 0))],
        out_specs=pl.BlockSpec((TOK_TILE, D_MODEL), lambda i, c: (i, 0)),
        out_shape=jax.ShapeDtypeStruct((n_tiles * TOK_TILE, D_MODEL), F32),
        scratch_shapes=[pltpu.VMEM((TOK_TILE, D_MODEL), BF16),
                        pltpu.VMEM((TOK_TILE + 2 * FFN_PAD, 2 * fc), F32),
                        pltpu.VMEM((TOK_TILE, fc), BF16),
                        pltpu.VMEM((D_MODEL, 2 * fc), BF16),
                        pltpu.VMEM((fc, D_MODEL), BF16)],
        compiler_params=_params(("arbitrary", "arbitrary")),
        name="conv_ffn",
    )(x, g.reshape(1, D_MODEL), mods, w_up, w_up, conv_w, conv_w,
      conv_b.reshape(DEPTH, 1, -1), conv_b.reshape(DEPTH, 1, -1), w_down)


NA_HB = 4
NA_LANES = NA_HB * NA_HD
NA_ROWS = DEC_SEQ // GRID_W
NA_KEYS = NA_WIN_R * GRID_W
PROMPT_SEQS = 4
NA_ROW_UNROLL = 4


def _store_heads(dst_ref, seq, x, n_heads, hd):
    for h in range(n_heads):
        dst_ref[seq, h] = x[:, h * hd:(h + 1) * hd]


def _na_prompt_kernel(q_ref, k_ref, v_ref, gq_ref, gk_ref, o_ref, kn_ref, vn_ref):
    scale = NA_HD ** -0.5 * LOG2E

    def chains(bound):
        shift = None if bound is None else _stacked_rows(bound, SEQ)
        for seq in range(PROMPT_SEQS):
            sl = slice(seq * SEQ, (seq + 1) * SEQ)
            q = _group_rms(q_ref[sl, :], gq_ref[...], NA_HD, sums_on_mxu=True) * scale
            k = _group_rms(k_ref[sl, :], gk_ref[...], NA_HD, sums_on_mxu=True)
            v = v_ref[sl, :]
            _store_heads(kn_ref, seq, k, NA_HB, NA_HD)
            _store_heads(vn_ref, seq, v, NA_HB, NA_HD)
            q4 = _stack_heads(q, NA_HB, NA_HD).astype(BF16)
            s = lax.dot_general(q4, k.astype(BF16), _NT, preferred_element_type=F32)
            (p,), inv = _softmax_parts([s], shift)
            o4 = _bdot(p.astype(BF16), v.astype(BF16)) * inv
            o_ref[sl, :] = _unstack_heads(o4, NA_HB, NA_HD).astype(BF16)

    _with_score_bound(scale * _rms_norm_bound(gq_ref, NA_HD, NA_HD)
                      * _rms_norm_bound(gk_ref, NA_HD, NA_HD), chains)


def _na_prompt(qkv, gq, gk):
    nb = NA_HEADS // NA_HB
    rows = PROMPT_SEQS * SEQ
    blk = lambda off: pl.BlockSpec((rows, NA_LANES), lambda b, j: (b, off + j))
    vec = pl.BlockSpec((1, NA_LANES), lambda b, j: (0, 0))
    cache = pl.BlockSpec((PROMPT_SEQS, NA_HB, SEQ, NA_HD), lambda b, j: (b, j, 0, 0))
    cache_shape = jax.ShapeDtypeStruct((BATCH, NA_HEADS, SEQ, NA_HD), F32)
    return pl.pallas_call(
        _na_prompt_kernel,
        grid=(BATCH // PROMPT_SEQS, nb),
        in_specs=[blk(0), blk(nb), blk(2 * nb), vec, vec],
        out_specs=[blk(0), cache, cache],
        out_shape=[jax.ShapeDtypeStruct((N_PROMPT, D_MODEL), BF16), cache_shape, cache_shape],
        compiler_params=_params(("arbitrary", "arbitrary")),
        name="na_prompt",
    )(qkv, qkv, qkv, jnp.tile(gq, NA_HB).reshape(1, -1), jnp.tile(gk, NA_HB).reshape(1, -1))


def _na_latent_kernel(q_ref, k_ref, v_ref, kc_ref, vc_ref, t_ref, gq_ref, gk_ref, o_ref,
                      qn_ref, kn_ref, vb_ref, kc4_ref, vc4_ref, bias_ref):
    scale = NA_HD ** -0.5 * LOG2E
    rows = 256

    def prep(r, carry):
        sl = pl.ds(pl.multiple_of(r * rows, rows), rows)
        qn_ref[sl, :] = (_group_rms(q_ref[sl, :], gq_ref[...], NA_HD, sums_on_mxu=True)
                         * scale).astype(BF16)
        kn_ref[sl, :] = _group_rms(k_ref[sl, :], gk_ref[...], NA_HD,
                                   sums_on_mxu=True).astype(BF16)
        vb_ref[sl, :] = v_ref[sl, :].astype(BF16)
        return carry

    lax.fori_loop(0, DEC_SEQ // rows, prep, 0, unroll=2)
    kc = kc_ref[0]
    kc4_ref[...] = kc.astype(BF16)
    vc4_ref[...] = vc_ref[0].astype(BF16)

    def attend(bound):
        for h in range(NA_HB):
            off = 0.0 if bound is None else bound[:, h:h + 1]
            for p in range(NA_WIN_R):
                for i in range(NA_WIN_R):
                    bias_ref[h, p, :, i * GRID_W:(i + 1) * GRID_W] = t_ref[h, p + i] - off
        shift = None if bound is None else _stacked_rows(bound, GRID_W)

        def row(r):
            kr0 = jnp.clip(r - NA_WIN_R // 2, 0, NA_ROWS - NA_WIN_R)
            pat = kr0 - r + NA_WIN_R - 1
            qs = pl.ds(pl.multiple_of(r * GRID_W, GRID_W), GRID_W)
            ks = pl.ds(pl.multiple_of(kr0 * GRID_W, GRID_W), NA_KEYS)
            q4 = _stack_heads(qn_ref[qs, :], NA_HB, NA_HD)
            s_loc = lax.dot_general(q4, kn_ref[ks, :], _NT, preferred_element_type=F32)
            s_loc = s_loc + jnp.concatenate([bias_ref[h, pat] for h in range(NA_HB)], axis=0)
            s_ctx = lax.dot_general(q4, kc4_ref[...], _NT, preferred_element_type=F32)
            if shift is None:
                (p_loc, p_ctx), inv = _softmax_parts([s_loc, s_ctx])
            else:
                p_loc, p_ctx = jnp.exp2(s_loc), jnp.exp2(s_ctx - shift)
                inv = 1.0 / (p_loc.sum(axis=-1, keepdims=True) + p_ctx.sum(axis=-1, keepdims=True))
            o4 = _bdot(p_loc.astype(BF16), vb_ref[ks, :]) + _bdot(p_ctx.astype(BF16), vc4_ref[...])
            o_ref[qs, :] = _unstack_heads(o4 * inv, NA_HB, NA_HD).astype(BF16)

        def rows_step(i, carry):
            for u in range(NA_ROW_UNROLL):
                row(i * NA_ROW_UNROLL + u)
            return carry

        lax.fori_loop(0, NA_ROWS // NA_ROW_UNROLL, rows_step, 0)

    qmax = scale * _rms_norm_bound(gq_ref, NA_HD, NA_HD)
    bias_max = jnp.concatenate(
        [t_ref[h].max(axis=0).max(axis=0, keepdims=True).max(axis=1, keepdims=True)
         for h in range(NA_HB)], axis=-1)
    _with_score_bound(
        jnp.maximum(qmax * _rms_norm_bound(gk_ref, NA_HD, NA_HD) + bias_max,
                    qmax * _max_head_norms(kc, NA_HD)), attend)


def _na_bias_blocks(bias_table):
    qc = jnp.arange(GRID_W)[:, None]
    kc = jnp.arange(GRID_W)[None, :]
    win0 = jnp.clip(qc - NA_WIN_C // 2, 0, GRID_W - NA_WIN_C)
    valid = (kc >= win0) & (kc < win0 + NA_WIN_C)
    n_co = bias_table.shape[-1]
    onehot = (kc - qc + NA_WIN_C - 1)[None] == jnp.arange(n_co)[:, None, None]
    t = jnp.einsum('hrd,dqk->hrqk', bias_table.astype(F32), onehot.astype(F32),
                   precision=lax.Precision.HIGHEST)
    return jnp.where(valid, t * LOG2E, NEG)


def _na_latent(qkv, cache_k, cache_v, bias_blocks, gq, gk):
    nb = NA_HEADS // NA_HB
    lat0 = N_PROMPT // DEC_SEQ
    blk = lambda off: pl.BlockSpec((DEC_SEQ, NA_LANES), lambda b, j: (lat0 + b, off + j))
    vec = pl.BlockSpec((1, NA_LANES), lambda b, j: (0, 0))
    cache = pl.BlockSpec((1, PAST_LEN, NA_LANES), lambda b, j: (b, 0, j))
    return pl.pallas_call(
        _na_latent_kernel,
        grid=(DEC_BATCH, nb),
        in_specs=[blk(0), blk(nb), blk(2 * nb), cache, cache,
                  pl.BlockSpec((NA_HB, 2 * NA_WIN_R - 1, GRID_W, GRID_W), lambda b, j: (j, 0, 0, 0)),
                  vec, vec],
        out_specs=pl.BlockSpec((DEC_SEQ, NA_LANES), lambda b, j: (b, j)),
        out_shape=jax.ShapeDtypeStruct((N_LATENT, D_MODEL), BF16),
        scratch_shapes=[pltpu.VMEM((DEC_SEQ, NA_LANES), BF16),
                        pltpu.VMEM((DEC_SEQ, NA_LANES), BF16),
                        pltpu.VMEM((DEC_SEQ, NA_LANES), BF16),
                        pltpu.VMEM((PAST_LEN, NA_LANES), BF16),
                        pltpu.VMEM((PAST_LEN, NA_LANES), BF16),
                        pltpu.VMEM((NA_HB, NA_WIN_R, GRID_W, NA_KEYS), F32)],
        compiler_params=_params(("arbitrary", "arbitrary")),
        name="na_latent",
    )(qkv, qkv, qkv, _tokens_first(cache_k), _tokens_first(cache_v), bias_blocks,
      jnp.tile(gq, NA_HB).reshape(1, -1), jnp.tile(gk, NA_HB).reshape(1, -1))


GLA_C = 128
GLA_HPS = 2
GLA_SUB = 8
GLA_LEVELS = (64, 32, 16, 8)


def _split_hi_lo(x):
    hi = x.astype(BF16)
    lo = (x - hi.astype(F32)).astype(BF16)
    return jnp.concatenate([hi, lo], axis=1)


class _GlaMasks(NamedTuple):
    tri: jax.Array
    later: tuple
    sign: tuple
    pair: tuple
    diag: jax.Array


def _gla_masks(rev):
    c = GLA_C
    row = lax.broadcasted_iota(jnp.int32, (c, c), 0)
    col = lax.broadcasted_iota(jnp.int32, (c, c), 1)
    rid = lax.broadcasted_iota(jnp.int32, (c, GLA_DK), 0)
    causal = (col >= row) if rev else (col <= row)
    later, pair = [], []
    for m in GLA_LEVELS:
        later.append(((rid & m) == 0) if rev else ((rid & m) != 0))
        same = (row >> _log2(2 * m)) == (col >> _log2(2 * m))
        crossing = ((row & m) != (col & m))
        pair.append(same & crossing & causal)
    diag = ((row >> _log2(GLA_SUB)) == (col >> _log2(GLA_SUB))) & causal
    sign = tuple(jnp.where(l, 1.0, -1.0) for l in later)
    return _GlaMasks(jnp.where(causal, 1.0, 0.0).astype(BF16), tuple(later), sign, tuple(pair), diag)


def _gla_chunk(q, k, v, g, st_ref, rev, masks):
    c = GLA_C
    cs = _bdot(masks.tri, _split_hi_lo(g))
    b = cs[:, :GLA_DK] + cs[:, GLA_DK:]

    a = None
    for m, later, sign, pair in zip(GLA_LEVELS, masks.later, masks.sign, masks.pair):
        nblk = c // (2 * m)
        if rev:
            bnd = [b[j * 2 * m + m:j * 2 * m + m + 1] for j in range(nblk)]
        else:
            bnd = [b[j * 2 * m + m - 1:j * 2 * m + m] for j in range(nblk)]
        ref = jnp.concatenate([jnp.broadcast_to(x, (2 * m, GLA_DK)) for x in bnd], axis=0)
        x = (jnp.where(later, q, k) * jnp.exp((b - ref) * sign)).astype(BF16)
        blk = lax.dot_general(x, x, _NT, preferred_element_type=F32)
        a = jnp.where(pair, blk, 0.0 if a is None else a)

    nsub = c // GLA_SUB
    lane_c = lax.broadcasted_iota(jnp.int32, (GLA_SUB, c), 1)
    diag_rows = []
    for blk_i in range(nsub):
        r0 = blk_i * GLA_SUB
        qb = q[r0:r0 + GLA_SUB]
        bb = b[r0:r0 + GLA_SUB]
        acc = jnp.zeros((GLA_SUB, c), F32)
        for s in range(GLA_SUB):
            ks = k[r0 + s:r0 + s + 1]
            bs = b[r0 + s:r0 + s + 1]
            w = jnp.sum(qb * ks * jnp.exp(jnp.minimum(bb - bs, 0.0)), axis=-1, keepdims=True)
            acc = jnp.where(lane_c == r0 + s, w, acc)
        diag_rows.append(acc)
    a = jnp.where(masks.diag, jnp.concatenate(diag_rows, axis=0), a)

    st = st_ref[...]
    inter = lax.dot_general((q * jnp.exp(b)).astype(BF16), st.astype(BF16), _NT,
                            preferred_element_type=F32)
    o = inter + _bdot(a.astype(BF16), v.astype(BF16))

    btot = b[0:1] if rev else b[c - 1:c]
    kd = (k * jnp.exp(btot - b)).astype(BF16)
    st_ref[...] = st * jnp.exp(btot) + lax.dot_general(v.astype(BF16), kd, _TN,
                                                       preferred_element_type=F32)
    return o


def _gla_kernel(*refs, n_tok, has_state):
    if has_state:
        (q_ref, k_ref, v_ref, r_ref, w2_ref, bg_ref, gn_ref, s0f_ref, s0b_ref,
         o_ref, lg_ref, of_ref, ob_ref, stf_ref, stb_ref) = refs
    else:
        (q_ref, k_ref, v_ref, r_ref, w2_ref, bg_ref, gn_ref,
         o_ref, sf_ref, sb_ref, lg_ref, of_ref, ob_ref, stf_ref, stb_ref) = refs
    nc = n_tok // GLA_C
    scale = GLA_DK ** -0.5

    rb = r_ref[...].astype(BF16)
    for z in range(2):
        x = _bdot(rb, w2_ref[z].astype(BF16)) + bg_ref[z]
        lg_ref[z] = (jnp.minimum(x, 0.0) - jnp.log1p(jnp.exp(-jnp.abs(x)))) * (1.0 / GLA_GATE_NORM)

    for hh in range(GLA_HPS):
        if has_state:
            stf_ref[hh] = s0f_ref[0, hh].T
            stb_ref[hh] = s0b_ref[0, hh].T
        else:
            stf_ref[hh] = jnp.zeros((GLA_DV, GLA_DK), F32)
            stb_ref[hh] = jnp.zeros((GLA_DV, GLA_DK), F32)

    masks = {rev: _gla_masks(rev) for rev in (False, True)}

    def step(ci, carry):
        for hh in range(GLA_HPS):
            kq = slice(hh * GLA_DK, (hh + 1) * GLA_DK)
            vv = slice(hh * GLA_DV, (hh + 1) * GLA_DV)
            for rev in (False, True):
                cc = (nc - 1 - ci) if rev else ci
                sl = pl.ds(pl.multiple_of(cc * GLA_C, GLA_C), GLA_C)
                o = _gla_chunk(q_ref[sl, kq] * scale, k_ref[sl, kq], v_ref[sl, vv],
                               lg_ref[1 if rev else 0, sl, kq],
                               (stb_ref if rev else stf_ref).at[hh], rev, masks[rev])
                (ob_ref if rev else of_ref)[sl, vv] = o
        return carry

    lax.fori_loop(0, nc, step, 0)

    for hh in range(GLA_HPS):
        vv = slice(hh * GLA_DV, (hh + 1) * GLA_DV)
        o = of_ref[:, vv] + ob_ref[:, vv]
        o_ref[:, vv] = o * lax.rsqrt(jnp.mean(o * o, axis=-1, keepdims=True) + EPS) * gn_ref[...]
        if not has_state:
            sf_ref[0, hh] = stf_ref[hh].T
            sb_ref[0, hh] = stb_ref[hh].T


def _gla(proj, w2, bg, gnorm, n_seq, n_tok, row_block0, states=None):
    kw, vw = GLA_HPS * GLA_DK, GLA_HPS * GLA_DV
    spec = lambda width, off: pl.BlockSpec((n_tok, width), lambda b, h: (row_block0 + b, off + h))
    in_specs = [spec(kw, 0), spec(kw, GLA_HK // kw), spec(vw, 2 * GLA_HK // vw),
                pl.BlockSpec((n_tok, 128), lambda b, h: (row_block0 + b, (2 * GLA_HK + 2 * GLA_HV) // 128)),
                pl.BlockSpec((2, 128, kw), lambda b, h: (0, 0, h)),
                pl.BlockSpec((2, 1, kw), lambda b, h: (0, 0, h)),
                pl.BlockSpec((1, GLA_DV), lambda b, h: (0, 0))]
    args = [proj, proj, proj, proj, w2, bg, gnorm.reshape(1, GLA_DV)]
    st_spec = pl.BlockSpec((1, GLA_HPS, GLA_DK, GLA_DV), lambda b, h: (b, h, 0, 0))
    o_spec = pl.BlockSpec((n_tok, vw), lambda b, h: (b, h))
    o_shape = jax.ShapeDtypeStruct((n_seq * n_tok, GLA_HV), F32)
    if states is not None:
        in_specs += [st_spec, st_spec]
        args += list(states)
        out_specs, out_shape = o_spec, o_shape
    else:
        st_shape = jax.ShapeDtypeStruct((n_seq, GLA_HEADS, GLA_DK, GLA_DV), F32)
        out_specs, out_shape = [o_spec, st_spec, st_spec], [o_shape, st_shape, st_shape]
    return pl.pallas_call(
        functools.partial(_gla_kernel, n_tok=n_tok, has_state=states is not None),
        grid=(n_seq, GLA_HEADS // GLA_HPS),
        in_specs=in_specs,
        out_specs=out_specs,
        out_shape=out_shape,
        scratch_shapes=[pltpu.VMEM((2, n_tok, kw), F32),
                        pltpu.VMEM((n_tok, vw), F32),
                        pltpu.VMEM((n_tok, vw), F32),
                        pltpu.VMEM((GLA_HPS, GLA_DV, GLA_DK), F32),
                        pltpu.VMEM((GLA_HPS, GLA_DV, GLA_DK), F32)],
        compiler_params=_params(("arbitrary", "arbitrary")),
        name="gla_latent" if states is not None else "gla_prompt",
    )(*args)


DIFF_HB = 2
DIFF_QL = DIFF_HB * DIFF_HD
DIFF_VL = DIFF_HB * 2 * DIFF_HD
DIFF_TQ = 512
DIFF_SUB = 128


def _diff_lambda(lam_ref):
    l = lam_ref[...]
    a = jnp.sum(l[0:1] * l[1:2], axis=-1, keepdims=True)
    b = jnp.sum(l[2:3] * l[3:4], axis=-1, keepdims=True)
    return jnp.exp(a) - jnp.exp(b) + DIFF_LAMBDA_INIT


def _diff_finish(ps, invs, lam, v, sn_ref):
    a = ps[0] * invs[0] - (lam * invs[1]) * ps[1]
    o2 = _bdot(a.astype(BF16), v)
    o = _unstack_heads(o2, DIFF_HB, 2 * DIFF_HD)
    return _group_rms(o, sn_ref[...], 2 * DIFF_HD) * (1.0 - DIFF_LAMBDA_INIT)


def _diff_prompt_kernel(q0_ref, q1_ref, k0_ref, k1_ref, v_ref, gq_ref, gk_ref, lam_ref, sn_ref,
                        o_ref, kn_ref, vn_ref):
    scale = DIFF_HD ** -0.5 * LOG2E
    lam = _diff_lambda(lam_ref)

    def chains(bound):
        shift = None if bound is None else _stacked_rows(bound, SEQ)
        for seq in range(PROMPT_SEQS):
            sl = slice(seq * SEQ, (seq + 1) * SEQ)
            ps, invs = [], []
            for comp, (q_ref, k_ref) in enumerate(((q0_ref, k0_ref), (q1_ref, k1_ref))):
                q = _group_rms(q_ref[sl, :], gq_ref[...], DIFF_HD, sums_on_mxu=True) * scale
                k = _group_rms(k_ref[sl, :], gk_ref[...], DIFF_HD, sums_on_mxu=True)
                for h in range(DIFF_HB):
                    kn_ref[seq, comp, h] = k[:, h * DIFF_HD:(h + 1) * DIFF_HD]
                q2 = _stack_heads(q, DIFF_HB, DIFF_HD).astype(BF16)
                s = lax.dot_general(q2, k.astype(BF16), _NT, preferred_element_type=F32)
                (p,), inv = _softmax_parts([s], shift)
                ps.append(p)
                invs.append(inv)
            v = v_ref[sl, :]
            _store_heads(vn_ref, seq, v, DIFF_HB, 2 * DIFF_HD)
            o_ref[sl, :] = _diff_finish(ps, invs, lam, v.astype(BF16), sn_ref).astype(BF16)

    _with_score_bound(scale * _rms_norm_bound(gq_ref, DIFF_HD, DIFF_HD)
                      * _rms_norm_bound(gk_ref, DIFF_HD, DIFF_HD), chains)


def _diff_prompt(qkv, gq, gk, lam, sub_norm):
    nb = DIFF_HEADS // DIFF_HB
    rows = PROMPT_SEQS * SEQ
    qk = lambda off: pl.BlockSpec((rows, DIFF_QL), lambda b, j: (b, off + j))
    vec = lambda n: pl.BlockSpec((1, n), lambda b, j: (0, 0))
    v_spec = pl.BlockSpec((rows, DIFF_VL), lambda b, j: (b, 2 * D_MODEL // DIFF_VL + j))
    kn_spec = pl.BlockSpec((PROMPT_SEQS, 2, DIFF_HB, SEQ, DIFF_HD), lambda b, j: (b, 0, j, 0, 0))
    vn_spec = pl.BlockSpec((PROMPT_SEQS, DIFF_HB, SEQ, 2 * DIFF_HD), lambda b, j: (b, j, 0, 0))
    return pl.pallas_call(
        _diff_prompt_kernel,
        grid=(BATCH // PROMPT_SEQS, nb),
        in_specs=[qk(0), qk(nb), qk(2 * nb), qk(3 * nb), v_spec, vec(DIFF_QL), vec(DIFF_QL),
                  pl.BlockSpec((4, DIFF_HD), lambda b, j: (0, 0)), vec(DIFF_VL)],
        out_specs=[pl.BlockSpec((rows, DIFF_VL), lambda b, j: (b, j)), kn_spec, vn_spec],
        out_shape=[jax.ShapeDtypeStruct((N_PROMPT, D_MODEL), BF16),
                   jax.ShapeDtypeStruct((BATCH, 2, DIFF_HEADS, SEQ, DIFF_HD), F32),
                   jax.ShapeDtypeStruct((BATCH, DIFF_HEADS, SEQ, 2 * DIFF_HD), F32)],
        compiler_params=_params(("arbitrary", "arbitrary")),
        name="diff_prompt",
    )(qkv, qkv, qkv, qkv, qkv, jnp.tile(gq, DIFF_HB).reshape(1, -1),
      jnp.tile(gk, DIFF_HB).reshape(1, -1), lam, jnp.tile(sub_norm, DIFF_HB).reshape(1, -1))


def _diff_latent_kernel(q0_ref, q1_ref, k0_ref, k1_ref, v_ref, kc0_ref, kc1_ref, vc_ref,
                        cos_ref, sin_ref, cosq_ref, sinq_ref, gq_ref, gk_ref, lam_ref, sn_ref,
                        o_ref, kb0_ref, kb1_ref, vb_ref, kmax_ref):
    scale = DIFF_HD ** -0.5 * LOG2E
    half = DIFF_HD // 4
    rows = 256

    @pl.when(pl.program_id(2) == 0)
    def _():
        for comp, (k_ref, kc_ref, kb_ref) in enumerate(
                ((k0_ref, kc0_ref, kb0_ref), (k1_ref, kc1_ref, kb1_ref))):
            kc = kc_ref[0]
            kb_ref[0:PAST_LEN, :] = kc.astype(BF16)
            kmax_ref[comp:comp + 1, :] = jnp.maximum(
                _max_head_norms(kc, DIFF_HD), _rms_norm_bound(gk_ref, DIFF_HD, DIFF_HD))

            def prep(r, carry):
                sl = pl.ds(pl.multiple_of(r * rows, rows), rows)
                k = _group_rms(k_ref[sl, :], gk_ref[...], DIFF_HD)
                k = _rope(k, cos_ref[sl, :], sin_ref[sl, :], half)
                kb_ref[pl.ds(pl.multiple_of(PAST_LEN + r * rows, rows), rows), :] = k.astype(BF16)
                return carry

            lax.fori_loop(0, DEC_SEQ // rows, prep, 0, unroll=2)
        vb_ref[0:PAST_LEN, :] = vc_ref[0].astype(BF16)
        vb_ref[PAST_LEN:, :] = v_ref[...].astype(BF16)

    lam = _diff_lambda(lam_ref)

    def chains(bound):
        for r0 in range(0, DIFF_TQ, DIFF_SUB):
            sl = slice(r0, r0 + DIFF_SUB)
            ps, invs = [], []
            for comp, (q_ref, kb_ref) in enumerate(((q0_ref, kb0_ref), (q1_ref, kb1_ref))):
                shift = None if bound is None else _stacked_rows(
                    bound[:, comp * DIFF_HB:(comp + 1) * DIFF_HB], DIFF_SUB)
                q = _group_rms(q_ref[sl, :], gq_ref[...], DIFF_HD)
                q = _rope(q, cosq_ref[sl, :], sinq_ref[sl, :], half) * scale
                q2 = _stack_heads(q, DIFF_HB, DIFF_HD).astype(BF16)
                s = lax.dot_general(q2, kb_ref[...], _NT, preferred_element_type=F32)
                (p,), inv = _softmax_parts([s], shift)
                ps.append(p)
                invs.append(inv)
            o_ref[sl, :] = _diff_finish(ps, invs, lam, vb_ref[...], sn_ref).astype(BF16)

    qmax = scale * _rms_norm_bound(gq_ref, DIFF_HD, DIFF_HD)
    _with_score_bound(jnp.concatenate([qmax * kmax_ref[0:1, :], qmax * kmax_ref[1:2, :]], axis=-1),
                      chains)


def _diff_latent(qkv, cache_k, cache_v, cos, sin, gq, gk, lam, sub_norm):
    nb = DIFF_HEADS // DIFF_HB
    nq = DEC_SEQ // DIFF_TQ
    q0 = N_PROMPT // DIFF_TQ
    lat0 = N_PROMPT // DEC_SEQ
    n_keys = PAST_LEN + DEC_SEQ
    q_spec = lambda off: pl.BlockSpec((DIFF_TQ, DIFF_QL), lambda b, j, t: (q0 + b * nq + t, off + j))
    k_spec = lambda off: pl.BlockSpec((DEC_SEQ, DIFF_QL), lambda b, j, t: (lat0 + b, off + j))
    v_spec = pl.BlockSpec((DEC_SEQ, DIFF_VL), lambda b, j, t: (lat0 + b, 2 * D_MODEL // DIFF_VL + j))
    kc_spec = lambda off: pl.BlockSpec((1, PAST_LEN, DIFF_QL), lambda b, j, t: (b, 0, off + j))
    vc_spec = pl.BlockSpec((1, PAST_LEN, DIFF_VL), lambda b, j, t: (b, 0, j))
    tab = pl.BlockSpec((DEC_SEQ, DIFF_QL), lambda b, j, t: (0, 0))
    tabq = pl.BlockSpec((DIFF_TQ, DIFF_QL), lambda b, j, t: (t, 0))
    vec = lambda n: pl.BlockSpec((1, n), lambda b, j, t: (0, 0))
    return pl.pallas_call(
        _diff_latent_kernel,
        grid=(DEC_BATCH, nb, nq),
        in_specs=[q_spec(0), q_spec(nb), k_spec(2 * nb), k_spec(3 * nb), v_spec,
                  kc_spec(0), kc_spec(nb), vc_spec, tab, tab, tabq, tabq,
                  vec(DIFF_QL), vec(DIFF_QL),
                  pl.BlockSpec((4, DIFF_HD), lambda b, j, t: (0, 0)), vec(DIFF_VL)],
        out_specs=pl.BlockSpec((DIFF_TQ, DIFF_VL), lambda b, j, t: (b * nq + t, j)),
        out_shape=jax.ShapeDtypeStruct((N_LATENT, D_MODEL), BF16),
        scratch_shapes=[pltpu.VMEM((n_keys, DIFF_QL), BF16),
                        pltpu.VMEM((n_keys, DIFF_QL), BF16),
                        pltpu.VMEM((n_keys, DIFF_VL), BF16),
                        pltpu.VMEM((2, DIFF_HB), F32)],
        compiler_params=_params(("arbitrary", "arbitrary", "arbitrary")),
        name="diff_latent",
    )(qkv, qkv, qkv, qkv, qkv, _tokens_first(cache_k), _tokens_first(cache_k),
      _tokens_first(cache_v), cos, sin, cos, sin,
      jnp.tile(gq, DIFF_HB).reshape(1, -1), jnp.tile(gk, DIFF_HB).reshape(1, -1), lam,
      jnp.tile(sub_norm, DIFF_HB).reshape(1, -1))


MLA_HB = 2
MLA_HL = 128
MLA_LANES = MLA_HB * MLA_HL
MLA_TQ = 512
MLA_SUB = 128


def _mla_keys(kv, kr, gk, sums_on_mxu=False):
    lane = lax.broadcasted_iota(jnp.int32, kv.shape, 1)
    kr2 = jnp.concatenate([kr] * MLA_HB, axis=1)
    k = jnp.where((lane & (MLA_HL - 1)) < MLA_NOPE, kv, kr2)
    return _group_rms(k, gk, MLA_HL, n_real=MLA_QK, sums_on_mxu=sums_on_mxu)


def _mla_out(o2):
    tq = o2.shape[0] // MLA_HB
    oa = pltpu.roll(o2[0:tq, 0:MLA_HL], MLA_HL - MLA_V, axis=1)
    ob = o2[tq:, MLA_HL:]
    lane = lax.broadcasted_iota(jnp.int32, oa.shape, 1)
    return jnp.where(lane < MLA_V, oa, ob)


def _mla_prompt_kernel(q_ref, kv_ref, kr_ref, gq_ref, gk_ref, o_ref):
    scale = MLA_QK ** -0.5 * LOG2E

    def chains(bound):
        shift = None if bound is None else _stacked_rows(bound, SEQ)
        for seq in range(PROMPT_SEQS):
            sl = slice(seq * SEQ, (seq + 1) * SEQ)
            q = _group_rms(q_ref[sl, :], gq_ref[...], MLA_HL, n_real=MLA_QK) * scale
            kv = kv_ref[sl, :]
            k = _mla_keys(kv, kr_ref[sl, :], gk_ref[...])
            q2 = _stack_heads(q, MLA_HB, MLA_HL).astype(BF16)
            s = lax.dot_general(q2, k.astype(BF16), _NT, preferred_element_type=F32)
            (p,), inv = _softmax_parts([s], shift)
            o_ref[sl, :] = _mla_out(_bdot(p.astype(BF16), kv.astype(BF16)) * inv).astype(BF16)

    _with_score_bound(scale * _rms_norm_bound(gq_ref, MLA_QK, MLA_HL)
                      * _rms_norm_bound(gk_ref, MLA_QK, MLA_HL), chains)


def _mla_prompt(qp, kvp, low, gq, gk):
    nb = MLA_HEADS // MLA_HB
    rows = PROMPT_SEQS * SEQ
    blk = pl.BlockSpec((rows, MLA_LANES), lambda b, j: (b, j))
    vec = pl.BlockSpec((1, MLA_LANES), lambda b, j: (0, 0))
    return pl.pallas_call(
        _mla_prompt_kernel,
        grid=(BATCH // PROMPT_SEQS, nb),
        in_specs=[blk, blk, pl.BlockSpec((rows, MLA_HL), lambda b, j: (b, 0)),
                  vec, vec],
        out_specs=pl.BlockSpec((rows, MLA_HB * MLA_V), lambda b, j: (b, j)),
        out_shape=jax.ShapeDtypeStruct((N_PROMPT, MLA_HEADS * MLA_V), BF16),
        compiler_params=_params(("arbitrary", "arbitrary")),
        name="mla_prompt",
    )(qp, kvp, low, gq, gk)


def _mla_latent_kernel(q_ref, kv_ref, kr_ref, kvc_ref, krc_ref, cos_ref, sin_ref, cosq_ref,
                       sinq_ref, gq_ref, gk_ref, o_ref, kb_ref, vb_ref, kmax_ref):
    scale = MLA_QK ** -0.5 * LOG2E
    half = MLA_ROPE // 4
    rows = 256

    @pl.when(pl.program_id(2) == 0)
    def _():
        kvc = kvc_ref[...]
        kc = _mla_keys(kvc, krc_ref[...], gk_ref[...])
        kb_ref[0:PAST_LEN, :] = kc.astype(BF16)
        vb_ref[0:PAST_LEN, :] = kvc.astype(BF16)

        def prep(r, carry):
            sl = pl.ds(pl.multiple_of(r * rows, rows), rows)
            dst = pl.ds(pl.multiple_of(PAST_LEN + r * rows, rows), rows)
            kv = kv_ref[sl, :]
            k = _mla_keys(kv, kr_ref[sl, :], gk_ref[...], sums_on_mxu=True)
            k = _rope(k, cos_ref[sl, :], sin_ref[sl, :], half)
            kb_ref[dst, :] = k.astype(BF16)
            vb_ref[dst, :] = kv.astype(BF16)
            return carry

        lax.fori_loop(0, DEC_SEQ // rows, prep, 0, unroll=2)
        kmax_ref[...] = jnp.maximum(_max_head_norms(kc, MLA_HL),
                                    _rms_norm_bound(gk_ref, MLA_QK, MLA_HL))

    def chains(bound):
        shift = None if bound is None else _stacked_rows(bound, MLA_SUB)
        for r0 in range(0, MLA_TQ, MLA_SUB):
            sl = slice(r0, r0 + MLA_SUB)
            q = _group_rms(q_ref[sl, :], gq_ref[...], MLA_HL, n_real=MLA_QK)
            q = _rope(q, cosq_ref[sl, :], sinq_ref[sl, :], half) * scale
            q2 = _stack_heads(q, MLA_HB, MLA_HL).astype(BF16)
            s = lax.dot_general(q2, kb_ref[...], _NT, preferred_element_type=F32)
            (p,), inv = _softmax_parts([s], shift)
            o_ref[sl, :] = _mla_out(_bdot(p.astype(BF16), vb_ref[...]) * inv).astype(BF16)

    _with_score_bound(scale * _rms_norm_bound(gq_ref, MLA_QK, MLA_HL) * kmax_ref[...], chains)


def _mla_latent(qp, kvp, low, kvc, krc, cos, sin, gq, gk):
    nb = MLA_HEADS // MLA_HB
    nq = DEC_SEQ // MLA_TQ
    q0 = N_PROMPT // MLA_TQ
    lat0 = N_PROMPT // DEC_SEQ
    n_keys = PAST_LEN + DEC_SEQ
    tab = pl.BlockSpec((DEC_SEQ, MLA_LANES), lambda b, j, t: (0, 0))
    tabq = pl.BlockSpec((MLA_TQ, MLA_LANES), lambda b, j, t: (t, 0))
    vec = pl.BlockSpec((1, MLA_LANES), lambda b, j, t: (0, 0))
    return pl.pallas_call(
        _mla_latent_kernel,
        grid=(DEC_BATCH, nb, nq),
        in_specs=[pl.BlockSpec((MLA_TQ, MLA_LANES), lambda b, j, t: (q0 + b * nq + t, j)),
                  pl.BlockSpec((DEC_SEQ, MLA_LANES), lambda b, j, t: (lat0 + b, j)),
                  pl.BlockSpec((DEC_SEQ, MLA_HL), lambda b, j, t: (lat0 + b, 0)),
                  pl.BlockSpec((PAST_LEN, MLA_LANES), lambda b, j, t: (b, j)),
                  pl.BlockSpec((PAST_LEN, MLA_HL), lambda b, j, t: (b, 0)),
                  tab, tab, tabq, tabq, vec, vec],
        out_specs=pl.BlockSpec((MLA_TQ, MLA_HB * MLA_V), lambda b, j, t: (b * nq + t, j)),
        out_shape=jax.ShapeDtypeStruct((N_LATENT, MLA_HEADS * MLA_V), BF16),
        scratch_shapes=[pltpu.VMEM((n_keys, MLA_LANES), BF16),
                        pltpu.VMEM((n_keys, MLA_LANES), BF16),
                        pltpu.VMEM((1, MLA_HB), F32)],
        compiler_params=_params(("arbitrary", "arbitrary", "arbitrary")),
        name="mla_latent",
    )(qp, kvp, low, kvc, krc, cos, sin, cos, sin, gq, gk)


MLA_LOW_Q = 0
MLA_LOW_KV = 512
MLA_LOW_KR = 768
MLA_LOW_N = 896


def _axial_tables(n_tok, rdim):
    nf = rdim // 4
    freqs = ROPE_BASE ** (-jnp.arange(nf, dtype=F32) / nf)
    t = jnp.arange(n_tok)
    rowp = (t // GRID_W).astype(F32)
    colp = (t % GRID_W).astype(F32)
    ang = jnp.stack([rowp[:, None] * freqs, colp[:, None] * freqs], axis=1)
    cos, sin = jnp.cos(ang), jnp.sin(ang)
    cos_l = jnp.stack([cos, cos], axis=2).reshape(n_tok, rdim)
    sin_l = jnp.stack([-sin, sin], axis=2).reshape(n_tok, rdim)
    return cos_l, sin_l


def _diff_rope_tables():
    cos, sin = _axial_tables(DEC_SEQ, DIFF_HD)
    return jnp.tile(cos, (1, DIFF_HB)), jnp.tile(sin, (1, DIFF_HB))


def _mla_rope_tables():
    cos, sin = _axial_tables(DEC_SEQ, MLA_ROPE)
    ones = jnp.ones((DEC_SEQ, MLA_NOPE), F32)
    pad1 = jnp.ones((DEC_SEQ, MLA_HL - MLA_QK), F32)
    cos_h = jnp.concatenate([ones, cos, pad1], axis=1)
    sin_h = jnp.concatenate([0 * ones, sin, 0 * pad1], axis=1)
    return jnp.tile(cos_h, (1, MLA_HB)), jnp.tile(sin_h, (1, MLA_HB))


def _tokens_first(cache):
    b, h, l, d = cache.shape
    return jnp.transpose(cache, (0, 2, 1, 3)).reshape(b, l, h * d)


def _pad_heads(w, heads, hd, hl):
    k = w.shape[0]
    return jnp.pad(w.reshape(k, heads, hd), ((0, 0), (0, 0), (0, hl - hd))).reshape(k, heads * hl)


def kernel(x_prompt, x_sample, cache_l0_k, cache_l0_v, state_l1_fwd, state_l1_bwd, cache_l2_k,
           cache_l2_v, cache_l3_ckv, cache_l3_krope, c, c_ctx, ada_w, ada_b, norm_mix, norm_ffn,
           ffn_w_up, ffn_conv_w, ffn_conv_b, ffn_w_down, na_w_qkv, na_q_norm, na_k_norm, na_bias,
           na_w_o, gla_w_qkvg, gla_w_gate1, gla_w_gate2, gla_b_gate, gla_o_norm, gla_w_o,
           diff_w_qkv, diff_q_norm, diff_k_norm, diff_lambda, diff_sub_norm, diff_w_o, mla_w_dq,
           mla_q_a_norm, mla_w_uq, mla_w_dkv, mla_kv_a_norm, mla_w_ukv, mla_q_norm, mla_k_norm,
           mla_w_o):
    xr = _Rows(x_prompt.reshape(N_PROMPT, D_MODEL), x_sample.reshape(N_LATENT, D_MODEL), 0)
    cvecs = jnp.concatenate([c_ctx[None], c, jnp.zeros((5, D_MODEL), F32)], axis=0)
    mods_all = _ada_mods(cvecs, ada_w, ada_b)
    halves = lambda o_p, o_s: _Rows(o_p, o_s, 0)

    mods = mods_all[0]
    qkv = _norm_mod_proj(xr, norm_mix[0], mods, [na_w_qkv], "na_qkv")
    o_p, new_l0_k, new_l0_v = _na_prompt(qkv, na_q_norm, na_k_norm)
    o_s = _na_latent(qkv, cache_l0_k, cache_l0_v, _na_bias_blocks(na_bias), na_q_norm, na_k_norm)
    x = _out_proj_residual(xr, halves(o_p, o_s), mods, na_w_o, "na_out")
    ffn_weights = (ffn_w_up, ffn_conv_w, ffn_conv_b, ffn_w_down)
    x = _ffn(x, norm_ffn[0], mods, 0, *ffn_weights)
    xr = _one_array(x)

    mods = mods_all[1]
    w_decay = jnp.concatenate(
        [gla_w_gate1[0], gla_w_gate1[1],
         jnp.zeros((D_MODEL, 128 - 2 * GLA_GATE_RANK), F32)], axis=1)
    proj = _norm_mod_proj(xr, norm_mix[1], mods, [gla_w_qkvg, w_decay], "gla_proj")
    w2 = jnp.zeros((2, 128, GLA_HK), F32)
    w2 = w2.at[0, :GLA_GATE_RANK].set(gla_w_gate2[0])
    w2 = w2.at[1, GLA_GATE_RANK:2 * GLA_GATE_RANK].set(gla_w_gate2[1])
    bg = gla_b_gate.reshape(2, 1, GLA_HK)
    o_p, new_l1_fwd, new_l1_bwd = _gla(proj, w2, bg, gla_o_norm, BATCH, SEQ, 0)
    o_s = _gla(proj, w2, bg, gla_o_norm, DEC_BATCH, DEC_SEQ, N_PROMPT // DEC_SEQ,
               states=(state_l1_fwd, state_l1_bwd))
    x = _out_proj_residual(xr, halves(o_p, o_s), mods, gla_w_o, "gla_out",
                           gate=proj, gate_col_block=(2 * GLA_HK + GLA_HV) // GLA_HV)
    x = _ffn(x, norm_ffn[1], mods, 1, *ffn_weights)
    xr = _one_array(x)

    mods = mods_all[2]
    qkv = _norm_mod_proj(xr, norm_mix[2], mods, [diff_w_qkv], "diff_qkv")
    o_p, kn_p, new_l2_v = _diff_prompt(qkv, diff_q_norm, diff_k_norm, diff_lambda, diff_sub_norm)
    new_l2_k = kn_p.reshape(BATCH, 2 * DIFF_HEADS, SEQ, DIFF_HD)
    cos_d, sin_d = _diff_rope_tables()
    o_s = _diff_latent(qkv, cache_l2_k, cache_l2_v, cos_d, sin_d, diff_q_norm, diff_k_norm,
                       diff_lambda, diff_sub_norm)
    x = _out_proj_residual(xr, halves(o_p, o_s), mods, diff_w_o, "diff_out")
    x = _ffn(x, norm_ffn[2], mods, 2, *ffn_weights)
    xr = _one_array(x)

    mods = mods_all[3]
    zc = lambda n: jnp.zeros((D_MODEL, n), F32)
    w_low = jnp.concatenate(
        [mla_w_dq, zc(MLA_LOW_KV - MLA_Q_RANK), mla_w_dkv[:, :MLA_KV_RANK],
         zc(MLA_NOPE), mla_w_dkv[:, MLA_KV_RANK:], zc(MLA_HL - MLA_QK)], axis=1)
    w_uq = _pad_heads(mla_w_uq, MLA_HEADS, MLA_QK, MLA_HL)
    qp, kvp, ckv, kr = _mla_front(xr, norm_mix[3], mods, w_low, mla_q_a_norm, w_uq,
                                  mla_kv_a_norm, mla_w_ukv)
    kvc = _matmul(cache_l3_ckv.reshape(DEC_BATCH * PAST_LEN, MLA_KV_RANK), mla_w_ukv,
                  "mla_ukv_cache")
    krc = jnp.pad(cache_l3_krope.reshape(DEC_BATCH * PAST_LEN, MLA_ROPE),
                  ((0, 0), (MLA_NOPE, MLA_HL - MLA_QK)))
    pad_gain = lambda g: jnp.tile(jnp.pad(g, (0, MLA_HL - MLA_QK)), MLA_HB).reshape(1, -1)
    gq, gk = pad_gain(mla_q_norm), pad_gain(mla_k_norm)
    o_p = _mla_prompt(qp, kvp, kr, gq, gk)
    cos_m, sin_m = _mla_rope_tables()
    o_s = _mla_latent(qp, kvp, kr, kvc, krc, cos_m, sin_m, gq, gk)
    new_l3_ckv = ckv[:N_PROMPT].reshape(BATCH, SEQ, MLA_KV_RANK)
    new_l3_krope = kr[:N_PROMPT, MLA_NOPE:MLA_QK].reshape(BATCH, SEQ, MLA_ROPE)
    x = _out_proj_residual(xr, halves(o_p, o_s), mods, mla_w_o, "mla_out")
    n_pt = N_PROMPT // TOK_TILE
    ffn3 = functools.partial(_ffn, x, norm_ffn[3], mods, 3, *ffn_weights)
    y_prompt = ffn3(tile0=0, n_tiles=n_pt).reshape(BATCH, SEQ, D_MODEL)
    y_sample = ffn3(tile0=n_pt, n_tiles=N_TOK_TILES - n_pt).reshape(DEC_BATCH, DEC_SEQ, D_MODEL)
    return (y_prompt, y_sample, new_l0_k, new_l0_v, new_l1_fwd, new_l1_bwd, new_l2_k, new_l2_v,
            new_l3_ckv, new_l3_krope)
```

```python
import functools
import math
from typing import NamedTuple

import jax
import jax.numpy as jnp
from jax import lax
from jax.experimental import pallas as pl
from jax.experimental.pallas import tpu as pltpu

F32 = jnp.float32
BF16 = jnp.bfloat16

D_MODEL = 1024
BATCH = 16
SEQ = 256
DEPTH = 4
DEC_BATCH = 2
DEC_SEQ = 2048
PAST_LEN = 256
GRID_W = 64
D_FF = 2816
EPS = 1e-6
ROPE_BASE = 10000.0

NA_HEADS = 16
NA_HD = 64
NA_WIN_R = 8
NA_WIN_C = 16

GLA_HEADS = 4
GLA_DK = 128
GLA_DV = 256
GLA_HK = GLA_HEADS * GLA_DK
GLA_HV = GLA_HEADS * GLA_DV
GLA_GATE_RANK = 16
GLA_GATE_NORM = 16.0

DIFF_HEADS = 8
DIFF_HD = 64
DIFF_LAMBDA_INIT = 0.8 - 0.6 * math.exp(-0.3 * 2)

MLA_HEADS = 16
MLA_Q_RANK = 384
MLA_KV_RANK = 256
MLA_NOPE = 64
MLA_ROPE = 32
MLA_V = 64
MLA_QK = MLA_NOPE + MLA_ROPE

N_PROMPT = BATCH * SEQ
N_LATENT = DEC_BATCH * DEC_SEQ
N_TOK = N_PROMPT + N_LATENT
TOK_TILE = 2048
N_TOK_TILES = N_TOK // TOK_TILE
FF_CHUNK = 256
N_FF_CHUNKS = D_FF // FF_CHUNK
NEG = -1e30
LOG2E = math.log2(math.e)
SAFE_SCORE_BOUND = 48.0

VMEM_LIMIT = 56 * 1024 * 1024

_NT = (((1,), (1,)), ((), ()))
_TN = (((0,), (0,)), ((), ()))


def _params(sem, vmem=VMEM_LIMIT):
    return pltpu.CompilerParams(dimension_semantics=sem, vmem_limit_bytes=vmem)


def _log2(n):
    assert n & (n - 1) == 0
    return n.bit_length() - 1


def _silu(x):
    return x / (1.0 + jnp.exp(-x))


def _bdot(a, b):
    return jnp.dot(a, b, preferred_element_type=F32)


def _softmax_parts(parts, shift=None):
    m = shift
    if m is None:
        m = parts[0].max(axis=-1, keepdims=True)
        for s in parts[1:]:
            m = jnp.maximum(m, s.max(axis=-1, keepdims=True))
    ps = [jnp.exp2(s - m) for s in parts]
    l = ps[0].sum(axis=-1, keepdims=True)
    for p in ps[1:]:
        l = l + p.sum(axis=-1, keepdims=True)
    return ps, 1.0 / l


def _group_rms(x, gain, group, n_real=None, sums_on_mxu=False):
    lanes = x.shape[-1]
    n_real = n_real or group
    x2 = x * x
    if group == lanes:
        ms = jnp.sum(x2, axis=-1, keepdims=True)
    elif not sums_on_mxu:
        gid = lax.broadcasted_iota(jnp.int32, x.shape, 1) >> _log2(group)
        ms = jnp.zeros_like(x)
        for i in range(lanes // group):
            sel = gid == i
            si = jnp.sum(jnp.where(sel, x2, 0.0), axis=-1, keepdims=True)
            ms = jnp.where(sel, si, ms)
    else:
        r = lax.broadcasted_iota(jnp.int32, (lanes, lanes), 0) >> _log2(group)
        c = lax.broadcasted_iota(jnp.int32, (lanes, lanes), 1) >> _log2(group)
        ones = jnp.where(r == c, 1.0, 0.0).astype(BF16)
        hi = x2.astype(BF16)
        lo = (x2 - hi.astype(F32)).astype(BF16)
        ms = _bdot(hi, ones) + _bdot(lo, ones)
    return x * lax.rsqrt(ms * (1.0 / n_real) + EPS) * gain


def _with_score_bound(bound, body):
    ok = bound.max() < SAFE_SCORE_BOUND
    pl.when(ok)(lambda: body(bound))
    pl.when(jnp.logical_not(ok))(lambda: body(None))


def _rms_norm_bound(g_ref, n, head_lanes):
    g = jnp.abs(g_ref[...])
    heads = g.shape[-1] // head_lanes
    return n ** 0.5 * jnp.concatenate(
        [g[:, h * head_lanes:(h + 1) * head_lanes].max(axis=-1, keepdims=True)
         for h in range(heads)], axis=-1)


def _stacked_rows(per_head, rows):
    r = lax.broadcasted_iota(jnp.int32, (per_head.shape[-1] * rows, 1), 0)
    out = per_head[:, 0:1]
    for h in range(1, per_head.shape[-1]):
        out = jnp.where(r >= h * rows, per_head[:, h:h + 1], out)
    return out


def _max_head_norms(x, head_lanes):
    hid = lax.broadcasted_iota(jnp.int32, x.shape, 1) >> _log2(head_lanes)
    x2 = x * x
    sq = [jnp.sum(jnp.where(hid == h, x2, 0.0), axis=-1, keepdims=True).max(axis=0, keepdims=True)
          for h in range(x.shape[-1] // head_lanes)]
    return jnp.sqrt(jnp.concatenate(sq, axis=-1))


def _rope(x, cos, sin, half):
    lanes = x.shape[-1]
    lane = lax.broadcasted_iota(jnp.int32, x.shape, 1)
    up = pltpu.roll(x, lanes - half, axis=1)
    dn = pltpu.roll(x, half, axis=1)
    swapped = jnp.where((lane & (2 * half - 1)) < half, up, dn)
    return x * cos + swapped * sin


def _stack_heads(q, n_heads, head_lanes):
    hid = lax.broadcasted_iota(jnp.int32, q.shape, 1) >> _log2(head_lanes)
    zero = jnp.zeros_like(q)
    return jnp.concatenate([jnp.where(hid == i, q, zero) for i in range(n_heads)], axis=0)


def _unstack_heads(o, n_heads, head_lanes):
    rows = o.shape[0] // n_heads
    hid = lax.broadcasted_iota(jnp.int32, (rows, o.shape[1]), 1) >> _log2(head_lanes)
    out = o[0:rows]
    for i in range(1, n_heads):
        out = jnp.where(hid == i, o[i * rows:(i + 1) * rows], out)
    return out


ADA_TN = 1536


def _ada_kernel(c_ref, w_ref, b_ref, o_ref):
    s = _silu(c_ref[...])
    w = w_ref[0]
    s_hi, w_hi = s.astype(BF16), w.astype(BF16)
    s_lo = (s - s_hi.astype(F32)).astype(BF16)
    w_lo = (w - w_hi.astype(F32)).astype(BF16)
    both = _bdot(jnp.concatenate([s_hi, s_lo], axis=0), w_hi)
    o_ref[0] = both[0:8] + both[8:16] + _bdot(s_hi, w_lo) + b_ref[0]


def _ada_mods(cvecs, ada_w, ada_b):
    out = pl.pallas_call(
        _ada_kernel,
        grid=(DEPTH, 6 * D_MODEL // ADA_TN),
        in_specs=[pl.BlockSpec((8, D_MODEL), lambda l, j: (0, 0)),
                  pl.BlockSpec((1, D_MODEL, ADA_TN), lambda l, j: (l, 0, j)),
                  pl.BlockSpec((1, 1, ADA_TN), lambda l, j: (l, 0, j))],
        out_specs=pl.BlockSpec((1, 8, ADA_TN), lambda l, j: (l, 0, j)),
        out_shape=jax.ShapeDtypeStruct((DEPTH, 8, 6 * D_MODEL), F32),
        compiler_params=_params(("arbitrary", "arbitrary")),
        name="ada_mod",
    )(cvecs, ada_w, ada_b.reshape(DEPTH, 1, 6 * D_MODEL))
    return out.reshape(DEPTH, 8, 6, D_MODEL)[:, :3]


def _mod_group_of_tile(i):
    return jnp.maximum(i - (N_PROMPT // TOK_TILE - 1), 0)


def _norm_mod_rows(x_ref, g_ref, mod_ref, h_ref, shift_idx, scale_idx, rows=64):
    g = g_ref[...]
    sc = 1.0 + mod_ref[0, scale_idx:scale_idx + 1, :]
    sh = mod_ref[0, shift_idx:shift_idx + 1, :]

    def body(r, carry):
        sl = pl.ds(pl.multiple_of(r * rows, rows), rows)
        xf = x_ref[sl, :]
        ms = jnp.mean(xf * xf, axis=-1, keepdims=True)
        y = xf * lax.rsqrt(ms + EPS) * g
        h_ref[sl, :] = (y * sc + sh).astype(BF16)
        return carry

    lax.fori_loop(0, x_ref.shape[0] // rows, body, 0)


ROW_TM = 512


class _Rows(NamedTuple):
    prompt: jax.Array
    latent: jax.Array
    latent_row0: int


def _one_array(x):
    return _Rows(x, x, N_PROMPT)


def _row_specs(rows, width, col_block=0):
    n_p = N_PROMPT // ROW_TM
    l0 = rows.latent_row0 // ROW_TM
    return [pl.BlockSpec((ROW_TM, width), lambda t: (jnp.minimum(t, n_p - 1), col_block)),
            pl.BlockSpec((ROW_TM, width), lambda t: (l0 + jnp.maximum(t - n_p, 0), col_block))]


def _row_group(t):
    first_latent = N_PROMPT // ROW_TM
    return jnp.where(t < first_latent, 0, 1 + (t - first_latent) // (DEC_SEQ // ROW_TM))


def _is_prompt_tile():
    return pl.program_id(0) < N_PROMPT // ROW_TM


def _proj_kernel(xp_ref, xl_ref, g_ref, mod_ref, *refs):
    *w_refs, o_ref, h_ref, wb_ref = refs

    @pl.when(pl.program_id(0) == 0)
    def _():
        off = 0
        for w_ref in w_refs:
            wb_ref[:, off:off + w_ref.shape[1]] = w_ref[...].astype(BF16)
            off += w_ref.shape[1]

    is_prompt = _is_prompt_tile()
    g = g_ref[...]
    sc = 1.0 + mod_ref[0, 1:2, :]
    sh = mod_ref[0, 0:1, :]
    rows = 64
    part = ROW_TM // 2
    for p0 in range(0, ROW_TM, part):
        for r0 in range(p0, p0 + part, rows):
            sl = slice(r0, r0 + rows)
            xf = jnp.where(is_prompt, xp_ref[sl, :], xl_ref[sl, :])
            ms = jnp.mean(xf * xf, axis=-1, keepdims=True)
            y = xf * lax.rsqrt(ms + EPS) * g
            h_ref[sl, :] = (y * sc + sh).astype(BF16)
        o_ref[p0:p0 + part, :] = _bdot(h_ref[p0:p0 + part, :], wb_ref[...])


def _norm_mod_proj(x, g, mods, ws, name):
    n = sum(w.shape[1] for w in ws)
    return pl.pallas_call(
        _proj_kernel,
        grid=(N_TOK // ROW_TM,),
        in_specs=_row_specs(x, D_MODEL) + [
            pl.BlockSpec((1, D_MODEL), lambda t: (0, 0)),
            pl.BlockSpec((1, 6, D_MODEL), lambda t: (_row_group(t), 0, 0))] + [
            pl.BlockSpec(w.shape, lambda t: (0, 0), pipeline_mode=pl.Buffered(1)) for w in ws],
        out_specs=pl.BlockSpec((ROW_TM, n), lambda t: (t, 0)),
        out_shape=jax.ShapeDtypeStruct((N_TOK, n), F32),
        scratch_shapes=[pltpu.VMEM((ROW_TM, D_MODEL), BF16),
                        pltpu.VMEM((D_MODEL, n), BF16)],
        compiler_params=_params(("arbitrary",)),
        name=name,
    )(x.prompt, x.latent, g.reshape(1, D_MODEL), mods, *ws)


def _rms(a, g):
    return a * lax.rsqrt(jnp.mean(a * a, axis=-1, keepdims=True) + EPS) * g


def _mla_front_kernel(xp_ref, xl_ref, g_ref, mod_ref, wl_ref, gqa_ref, wuq_ref, gkv_ref, wukv_ref,
                      qp_ref, kvp_ref, ckv_ref, kr_ref, h_ref, wlb_ref, wuqb_ref, wukvb_ref):
    @pl.when(pl.program_id(0) == 0)
    def _():
        wlb_ref[...] = wl_ref[...].astype(BF16)
        wuqb_ref[...] = wuq_ref[...].astype(BF16)
        wukvb_ref[...] = wukv_ref[...].astype(BF16)

    is_prompt = _is_prompt_tile()
    g = g_ref[...]
    sc = 1.0 + mod_ref[0, 1:2, :]
    sh = mod_ref[0, 0:1, :]
    rows = 64
    part = ROW_TM // 2
    for p0 in range(0, ROW_TM, part):
        for r0 in range(p0, p0 + part, rows):
            sl = slice(r0, r0 + rows)
            xf = jnp.where(is_prompt, xp_ref[sl, :], xl_ref[sl, :])
            h_ref[sl, :] = (_rms(xf, g) * sc + sh).astype(BF16)
        ps = slice(p0, p0 + part)
        low = _bdot(h_ref[ps, :], wlb_ref[...])
        qa = _rms(low[:, MLA_LOW_Q:MLA_LOW_Q + MLA_Q_RANK], gqa_ref[...])
        qp_ref[ps, :] = _bdot(qa.astype(BF16), wuqb_ref[...])
        ckv = _rms(low[:, MLA_LOW_KV:MLA_LOW_KV + MLA_KV_RANK], gkv_ref[...])
        ckv_ref[ps, :] = ckv
        kvp_ref[ps, :] = _bdot(ckv.astype(BF16), wukvb_ref[...])
        kr_ref[ps, :] = low[:, MLA_LOW_KR:MLA_LOW_KR + MLA_HL]


def _mla_front(x, g, mods, w_low, g_qa, w_uq, g_kva, w_ukv):
    n_q, n_kv = w_uq.shape[1], w_ukv.shape[1]
    const = lambda shape: pl.BlockSpec(shape, lambda t: (0, 0), pipeline_mode=pl.Buffered(1))
    out = lambda n: pl.BlockSpec((ROW_TM, n), lambda t: (t, 0))
    shape = lambda n: jax.ShapeDtypeStruct((N_TOK, n), F32)
    return pl.pallas_call(
        _mla_front_kernel,
        grid=(N_TOK // ROW_TM,),
        in_specs=_row_specs(x, D_MODEL) + [
            pl.BlockSpec((1, D_MODEL), lambda t: (0, 0)),
            pl.BlockSpec((1, 6, D_MODEL), lambda t: (_row_group(t), 0, 0)),
            const(w_low.shape), const((1, MLA_Q_RANK)), const(w_uq.shape),
            const((1, MLA_KV_RANK)), const(w_ukv.shape)],
        out_specs=[out(n_q), out(n_kv), out(MLA_KV_RANK), out(MLA_HL)],
        out_shape=[shape(n_q), shape(n_kv), shape(MLA_KV_RANK), shape(MLA_HL)],
        scratch_shapes=[pltpu.VMEM((ROW_TM, D_MODEL), BF16),
                        pltpu.VMEM(w_low.shape, BF16),
                        pltpu.VMEM(w_uq.shape, BF16),
                        pltpu.VMEM(w_ukv.shape, BF16)],
        compiler_params=_params(("arbitrary",)),
        name="mla_front",
    )(x.prompt, x.latent, g.reshape(1, D_MODEL), mods, w_low, g_qa.reshape(1, -1), w_uq,
      g_kva.reshape(1, -1), w_ukv)


def _matmul_kernel(a_ref, w_ref, o_ref):
    o_ref[...] = _bdot(a_ref[...].astype(BF16), w_ref[...].astype(BF16))


def _matmul(a, w, name):
    rows, n = a.shape[0], w.shape[1]
    return pl.pallas_call(
        _matmul_kernel,
        out_shape=jax.ShapeDtypeStruct((rows, n), F32),
        compiler_params=_params(()),
        name=name,
    )(a, w)


def _oproj_kernel(*refs, gated):
    if gated:
        xp_ref, xl_ref, ap_ref, al_ref, g_ref, mod_ref, w_ref, o_ref, wb_ref = refs
    else:
        xp_ref, xl_ref, ap_ref, al_ref, mod_ref, w_ref, o_ref, wb_ref = refs

    @pl.when(pl.program_id(0) == 0)
    def _():
        wb_ref[...] = w_ref[...].astype(BF16)

    is_prompt = _is_prompt_tile()
    a = jnp.where(is_prompt, ap_ref[...], al_ref[...])
    if gated:
        a = a * _silu(g_ref[...])
    y = _bdot(a.astype(BF16), wb_ref[...])
    x = jnp.where(is_prompt, xp_ref[...], xl_ref[...])
    o_ref[...] = x + mod_ref[0, 2:3, :] * y


def _out_proj_residual(x, a, mods, w, name, gate=None, gate_col_block=0):
    k = w.shape[0]
    in_specs = _row_specs(x, D_MODEL) + _row_specs(a, k)
    args = [x.prompt, x.latent, a.prompt, a.latent]
    if gate is not None:
        in_specs.append(pl.BlockSpec((ROW_TM, k), lambda t: (t, gate_col_block)))
        args.append(gate)
    in_specs += [pl.BlockSpec((1, 6, D_MODEL), lambda t: (_row_group(t), 0, 0)),
                 pl.BlockSpec((k, D_MODEL), lambda t: (0, 0))]
    args += [mods, w]
    return pl.pallas_call(
        functools.partial(_oproj_kernel, gated=gate is not None),
        grid=(N_TOK // ROW_TM,),
        in_specs=in_specs,
        out_specs=pl.BlockSpec((ROW_TM, D_MODEL), lambda t: (t, 0)),
        out_shape=jax.ShapeDtypeStruct((N_TOK, D_MODEL), F32),
        scratch_shapes=[pltpu.VMEM((k, D_MODEL), BF16)],
        compiler_params=_params(("arbitrary",)),
        name=name,
    )(*args)


FFN_MM_ROWS = 512
FFN_ROWS = 64
FFN_PAD = 8


def _ffn_kernel(x_ref, g_ref, mod_ref, wg_ref, wv_ref, cwg_ref, cwv_ref, cbg_ref, cbv_ref,
                wd_ref, o_ref, h_ref, u_ref, act_ref, wup_ref, wdn_ref, *, tile0):
    i = tile0 + pl.program_id(0)
    c = pl.program_id(1)
    fc = FF_CHUNK
    n = TOK_TILE // FFN_MM_ROWS
    seq_len = jnp.where(i < N_PROMPT // TOK_TILE, SEQ, DEC_SEQ)
    row = lax.broadcasted_iota(jnp.int32, (FFN_ROWS, 1), 0)
    taps = lambda cw_ref, cb_ref: [jnp.broadcast_to(cw_ref[j:j + 1, :], (FFN_ROWS, fc))
                                   for j in range(3)] + [
                                       jnp.broadcast_to(cb_ref[...], (FFN_ROWS, fc))]
    taps_g, taps_v = taps(cwg_ref, cbg_ref), taps(cwv_ref, cbv_ref)

    def up(u_ref, t):
        r0 = t * FFN_MM_ROWS
        u_ref[FFN_PAD + r0:FFN_PAD + r0 + FFN_MM_ROWS, :] = _bdot(
            h_ref[r0:r0 + FFN_MM_ROWS, :], wup_ref[...])

    def conv_act(u_ref, t):
        for r0 in range(t * FFN_MM_ROWS, (t + 1) * FFN_MM_ROWS, FFN_ROWS):
            halves = []
            for lo, (w0, w1, w2, bias) in ((0, taps_g), (fc, taps_v)):
                p0 = FFN_PAD + r0
                prev = u_ref[p0 - 1:p0 - 1 + FFN_ROWS, lo:lo + fc]
                mid = u_ref[p0:p0 + FFN_ROWS, lo:lo + fc]
                nxt = u_ref[p0 + 1:p0 + 1 + FFN_ROWS, lo:lo + fc]
                if r0 % SEQ == 0:
                    prev = jnp.where(((r0 + row) & (seq_len - 1)) == 0, 0.0, prev)
                if (r0 + FFN_ROWS) % SEQ == 0:
                    nxt = jnp.where(((r0 + row) & (seq_len - 1)) == seq_len - 1, 0.0, nxt)
                halves.append(prev * w0 + mid * w1 + nxt * w2 + bias)
            act_ref[r0:r0 + FFN_ROWS, :] = (_silu(halves[0]) * halves[1]).astype(BF16)

    def down(t):
        r0 = t * FFN_MM_ROWS
        o_ref[r0:r0 + FFN_MM_ROWS, :] += _bdot(act_ref[r0:r0 + FFN_MM_ROWS, :], wdn_ref[...])

    @pl.when(c == 0)
    def _():
        _norm_mod_rows(x_ref, g_ref, mod_ref, h_ref, 3, 4)
        zeros = jnp.zeros((FFN_PAD, 2 * fc), F32)
        u_ref[0:FFN_PAD, :] = zeros
        u_ref[FFN_PAD + TOK_TILE:, :] = zeros
        o_ref[...] = jnp.zeros_like(o_ref)

    wup_ref[:, :fc] = wg_ref[...].astype(BF16)
    wup_ref[:, fc:] = wv_ref[...].astype(BF16)
    wdn_ref[...] = wd_ref[...].astype(BF16)
    for s in range(n + 2):
        if s < n:
            up(u_ref, s)
        if 1 <= s <= n:
            conv_act(u_ref, s - 1)
        if s >= 2:
            down(s - 2)

    @pl.when(c == N_FF_CHUNKS - 1)
    def _():
        o_ref[...] = x_ref[...] + mod_ref[0, 5:6, :] * o_ref[...]


def _ffn(x, g, mods, layer, w_up, conv_w, conv_b, w_down, tile0=0, n_tiles=N_TOK_TILES):
    fc = FF_CHUNK
    ncb = N_FF_CHUNKS
    return pl.pallas_call(
        functools.partial(_ffn_kernel, tile0=tile0),
        grid=(n_tiles, ncb),
        in_specs=[pl.BlockSpec((TOK_TILE, D_MODEL), lambda i, c: (tile0 + i, 0)),
                  pl.BlockSpec((1, D_MODEL), lambda i, c: (0, 0)),
                  pl.BlockSpec((1, 6, D_MODEL), lambda i, c: (_mod_group_of_tile(tile0 + i), 0, 0)),
                  pl.BlockSpec((None, D_MODEL, fc), lambda i, c: (layer, 0, c)),
                  pl.BlockSpec((None, D_MODEL, fc), lambda i, c: (layer, 0, ncb + c)),
                  pl.BlockSpec((None, 3, fc), lambda i, c: (layer, 0, c)),
                  pl.BlockSpec((None, 3, fc), lambda i, c: (layer, 0, ncb + c)),
                  pl.BlockSpec((None, 1, fc), lambda i, c: (layer, 0, c)),
                  pl.BlockSpec((None, 1, fc), lambda i, c: (layer, 0, ncb + c)),
                  pl.BlockSpec((None, fc, D_MODEL), lambda i, c: (layer, c, 0))],
        out_specs=pl.BlockSpec((TOK_TILE, D_MODEL), lambda i, c: (i, 0)),
        out_shape=jax.ShapeDtypeStruct((n_tiles * TOK_TILE, D_MODEL), F32),
        scratch_shapes=[pltpu.VMEM((TOK_TILE, D_MODEL), BF16),
                        pltpu.VMEM((TOK_TILE + 2 * FFN_PAD, 2 * fc), F32),
                        pltpu.VMEM((TOK_TILE, fc), BF16),
                        pltpu.VMEM((D_MODEL, 2 * fc), BF16),
                        pltpu.VMEM((fc, D_MODEL), BF16)],
        compiler_params=_params(("arbitrary", "arbitrary")),
        name="conv_ffn",
    )(x, g.reshape(1, D_MODEL), mods, w_up, w_up, conv_w, conv_w,
      conv_b.reshape(DEPTH, 1, -1), conv_b.reshape(DEPTH, 1, -1), w_down)


NA_HB = 4
NA_LANES = NA_HB * NA_HD
NA_ROWS = DEC_SEQ // GRID_W
NA_KEYS = NA_WIN_R * GRID_W
PROMPT_SEQS = 4
NA_ROW_UNROLL = 8


def _store_heads(dst_ref, seq, x, n_heads, hd):
    for h in range(n_heads):
        dst_ref[seq, h] = x[:, h * hd:(h + 1) * hd]


def _na_prompt_kernel(q_ref, k_ref, v_ref, gq_ref, gk_ref, o_ref, kn_ref, vn_ref):
    scale = NA_HD ** -0.5 * LOG2E

    def chains(bound):
        shift = None if bound is None else _stacked_rows(bound, SEQ)
        for seq in range(PROMPT_SEQS):
            sl = slice(seq * SEQ, (seq + 1) * SEQ)
            q = _group_rms(q_ref[sl, :], gq_ref[...], NA_HD, sums_on_mxu=True) * scale
            k = _group_rms(k_ref[sl, :], gk_ref[...], NA_HD, sums_on_mxu=True)
            v = v_ref[sl, :]
            _store_heads(kn_ref, seq, k, NA_HB, NA_HD)
            _store_heads(vn_ref, seq, v, NA_HB, NA_HD)
            q4 = _stack_heads(q, NA_HB, NA_HD).astype(BF16)
            s = lax.dot_general(q4, k.astype(BF16), _NT, preferred_element_type=F32)
            (p,), inv = _softmax_parts([s], shift)
            o4 = _bdot(p.astype(BF16), v.astype(BF16)) * inv
            o_ref[sl, :] = _unstack_heads(o4, NA_HB, NA_HD).astype(BF16)

    _with_score_bound(scale * _rms_norm_bound(gq_ref, NA_HD, NA_HD)
                      * _rms_norm_bound(gk_ref, NA_HD, NA_HD), chains)


def _na_prompt(qkv, gq, gk):
    nb = NA_HEADS // NA_HB
    rows = PROMPT_SEQS * SEQ
    blk = lambda off: pl.BlockSpec((rows, NA_LANES), lambda b, j: (b, off + j))
    vec = pl.BlockSpec((1, NA_LANES), lambda b, j: (0, 0))
    cache = pl.BlockSpec((PROMPT_SEQS, NA_HB, SEQ, NA_HD), lambda b, j: (b, j, 0, 0))
    cache_shape = jax.ShapeDtypeStruct((BATCH, NA_HEADS, SEQ, NA_HD), F32)
    return pl.pallas_call(
        _na_prompt_kernel,
        grid=(BATCH // PROMPT_SEQS, nb),
        in_specs=[blk(0), blk(nb), blk(2 * nb), vec, vec],
        out_specs=[blk(0), cache, cache],
        out_shape=[jax.ShapeDtypeStruct((N_PROMPT, D_MODEL), BF16), cache_shape, cache_shape],
        compiler_params=_params(("arbitrary", "arbitrary")),
        name="na_prompt",
    )(qkv, qkv, qkv, jnp.tile(gq, NA_HB).reshape(1, -1), jnp.tile(gk, NA_HB).reshape(1, -1))


def _na_latent_kernel(q_ref, k_ref, v_ref, kc_ref, vc_ref, t_ref, gq_ref, gk_ref, o_ref,
                      qn_ref, kn_ref, vb_ref, kc4_ref, vc4_ref, bias_ref):
    scale = NA_HD ** -0.5 * LOG2E
    rows = 256

    def prep(r, carry):
        sl = pl.ds(pl.multiple_of(r * rows, rows), rows)
        qn_ref[sl, :] = (_group_rms(q_ref[sl, :], gq_ref[...], NA_HD, sums_on_mxu=True)
                         * scale).astype(BF16)
        kn_ref[sl, :] = _group_rms(k_ref[sl, :], gk_ref[...], NA_HD,
                                   sums_on_mxu=True).astype(BF16)
        vb_ref[sl, :] = v_ref[sl, :].astype(BF16)
        return carry

    lax.fori_loop(0, DEC_SEQ // rows, prep, 0, unroll=2)
    kc = kc_ref[0]
    kc4_ref[...] = kc.astype(BF16)
    vc4_ref[...] = vc_ref[0].astype(BF16)

    def attend(bound):
        for h in range(NA_HB):
            off = 0.0 if bound is None else bound[:, h:h + 1]
            for p in range(NA_WIN_R):
                for i in range(NA_WIN_R):
                    bias_ref[h, p, :, i * GRID_W:(i + 1) * GRID_W] = t_ref[h, p + i] - off
        shift = None if bound is None else _stacked_rows(bound, GRID_W)

        def row(r):
            kr0 = jnp.clip(r - NA_WIN_R // 2, 0, NA_ROWS - NA_WIN_R)
            pat = kr0 - r + NA_WIN_R - 1
            qs = pl.ds(pl.multiple_of(r * GRID_W, GRID_W), GRID_W)
            ks = pl.ds(pl.multiple_of(kr0 * GRID_W, GRID_W), NA_KEYS)
            q4 = _stack_heads(qn_ref[qs, :], NA_HB, NA_HD)
            s_loc = lax.dot_general(q4, kn_ref[ks, :], _NT, preferred_element_type=F32)
            s_loc = s_loc + jnp.concatenate([bias_ref[h, pat] for h in range(NA_HB)], axis=0)
            s_ctx = lax.dot_general(q4, kc4_ref[...], _NT, preferred_element_type=F32)
            if shift is None:
                (p_loc, p_ctx), inv = _softmax_parts([s_loc, s_ctx])
            else:
                p_loc, p_ctx = jnp.exp2(s_loc), jnp.exp2(s_ctx - shift)
                inv = 1.0 / (p_loc.sum(axis=-1, keepdims=True) + p_ctx.sum(axis=-1, keepdims=True))
            o4 = _bdot(p_loc.astype(BF16), vb_ref[ks, :]) + _bdot(p_ctx.astype(BF16), vc4_ref[...])
            o_ref[qs, :] = _unstack_heads(o4 * inv, NA_HB, NA_HD).astype(BF16)

        def rows_step(i, carry):
            for u in range(NA_ROW_UNROLL):
                row(i * NA_ROW_UNROLL + u)
            return carry

        lax.fori_loop(0, NA_ROWS // NA_ROW_UNROLL, rows_step, 0)

    qmax = scale * _rms_norm_bound(gq_ref, NA_HD, NA_HD)
    bias_max = jnp.concatenate(
        [t_ref[h].max(axis=0).max(axis=0, keepdims=True).max(axis=1, keepdims=True)
         for h in range(NA_HB)], axis=-1)
    _with_score_bound(
        jnp.maximum(qmax * _rms_norm_bound(gk_ref, NA_HD, NA_HD) + bias_max,
                    qmax * _max_head_norms(kc, NA_HD)), attend)


def _na_bias_blocks(bias_table):
    qc = jnp.arange(GRID_W)[:, None]
    kc = jnp.arange(GRID_W)[None, :]
    win0 = jnp.clip(qc - NA_WIN_C // 2, 0, GRID_W - NA_WIN_C)
    valid = (kc >= win0) & (kc < win0 + NA_WIN_C)
    n_co = bias_table.shape[-1]
    onehot = (kc - qc + NA_WIN_C - 1)[None] == jnp.arange(n_co)[:, None, None]
    t = jnp.einsum('hrd,dqk->hrqk', bias_table.astype(F32), onehot.astype(F32),
                   precision=lax.Precision.HIGHEST)
    return jnp.where(valid, t * LOG2E, NEG)


def _na_latent(qkv, cache_k, cache_v, bias_blocks, gq, gk):
    nb = NA_HEADS // NA_HB
    lat0 = N_PROMPT // DEC_SEQ
    blk = lambda off: pl.BlockSpec((DEC_SEQ, NA_LANES), lambda b, j: (lat0 + b, off + j))
    vec = pl.BlockSpec((1, NA_LANES), lambda b, j: (0, 0))
    cache = pl.BlockSpec((1, PAST_LEN, NA_LANES), lambda b, j: (b, 0, j))
    return pl.pallas_call(
        _na_latent_kernel,
        grid=(DEC_BATCH, nb),
        in_specs=[blk(0), blk(nb), blk(2 * nb), cache, cache,
                  pl.BlockSpec((NA_HB, 2 * NA_WIN_R - 1, GRID_W, GRID_W), lambda b, j: (j, 0, 0, 0)),
                  vec, vec],
        out_specs=pl.BlockSpec((DEC_SEQ, NA_LANES), lambda b, j: (b, j)),
        out_shape=jax.ShapeDtypeStruct((N_LATENT, D_MODEL), BF16),
        scratch_shapes=[pltpu.VMEM((DEC_SEQ, NA_LANES), BF16),
                        pltpu.VMEM((DEC_SEQ, NA_LANES), BF16),
                        pltpu.VMEM((DEC_SEQ, NA_LANES), BF16),
                        pltpu.VMEM((PAST_LEN, NA_LANES), BF16),
                        pltpu.VMEM((PAST_LEN, NA_LANES), BF16),
                        pltpu.VMEM((NA_HB, NA_WIN_R, GRID_W, NA_KEYS), F32)],
        compiler_params=_params(("arbitrary", "arbitrary")),
        name="na_latent",
    )(qkv, qkv, qkv, _tokens_first(cache_k), _tokens_first(cache_v), bias_blocks,
      jnp.tile(gq, NA_HB).reshape(1, -1), jnp.tile(gk, NA_HB).reshape(1, -1))


GLA_C = 128
GLA_HPS = 2
GLA_SUB = 8
GLA_LEVELS = (64, 32, 16, 8)


def _split_hi_lo(x):
    hi = x.astype(BF16)
    lo = (x - hi.astype(F32)).astype(BF16)
    return jnp.concatenate([hi, lo], axis=1)


class _GlaMasks(NamedTuple):
    tri: jax.Array
    later: tuple
    sign: tuple
    pair: tuple
    diag: jax.Array


def _gla_masks(rev):
    c = GLA_C
    row = lax.broadcasted_iota(jnp.int32, (c, c), 0)
    col = lax.broadcasted_iota(jnp.int32, (c, c), 1)
    rid = lax.broadcasted_iota(jnp.int32, (c, GLA_DK), 0)
    causal = (col >= row) if rev else (col <= row)
    later, pair = [], []
    for m in GLA_LEVELS:
        later.append(((rid & m) == 0) if rev else ((rid & m) != 0))
        same = (row >> _log2(2 * m)) == (col >> _log2(2 * m))
        crossing = ((row & m) != (col & m))
        pair.append(same & crossing & causal)
    diag = ((row >> _log2(GLA_SUB)) == (col >> _log2(GLA_SUB))) & causal
    sign = tuple(jnp.where(l, 1.0, -1.0) for l in later)
    return _GlaMasks(jnp.where(causal, 1.0, 0.0).astype(BF16), tuple(later), sign, tuple(pair), diag)


def _gla_chunk(q, k, v, g, st_ref, rev, masks):
    c = GLA_C
    cs = _bdot(masks.tri, _split_hi_lo(g))
    b = cs[:, :GLA_DK] + cs[:, GLA_DK:]

    a = None
    for m, later, sign, pair in zip(GLA_LEVELS, masks.later, masks.sign, masks.pair):
        nblk = c // (2 * m)
        if rev:
            bnd = [b[j * 2 * m + m:j * 2 * m + m + 1] for j in range(nblk)]
        else:
            bnd = [b[j * 2 * m + m - 1:j * 2 * m + m] for j in range(nblk)]
        ref = jnp.concatenate([jnp.broadcast_to(x, (2 * m, GLA_DK)) for x in bnd], axis=0)
        x = (jnp.where(later, q, k) * jnp.exp((b - ref) * sign)).astype(BF16)
        blk = lax.dot_general(x, x, _NT, preferred_element_type=F32)
        a = jnp.where(pair, blk, 0.0 if a is None else a)

    nsub = c // GLA_SUB
    lane_c = lax.broadcasted_iota(jnp.int32, (GLA_SUB, c), 1)
    diag_rows = []
    for blk_i in range(nsub):
        r0 = blk_i * GLA_SUB
        qb = q[r0:r0 + GLA_SUB]
        bb = b[r0:r0 + GLA_SUB]
        acc = jnp.zeros((GLA_SUB, c), F32)
        for s in range(GLA_SUB):
            ks = k[r0 + s:r0 + s + 1]
            bs = b[r0 + s:r0 + s + 1]
            w = jnp.sum(qb * ks * jnp.exp(jnp.minimum(bb - bs, 0.0)), axis=-1, keepdims=True)
            acc = jnp.where(lane_c == r0 + s, w, acc)
        diag_rows.append(acc)
    a = jnp.where(masks.diag, jnp.concatenate(diag_rows, axis=0), a)

    st = st_ref[...]
    inter = lax.dot_general((q * jnp.exp(b)).astype(BF16), st.astype(BF16), _NT,
                            preferred_element_type=F32)
    o = inter + _bdot(a.astype(BF16), v.astype(BF16))

    btot = b[0:1] if rev else b[c - 1:c]
    kd = (k * jnp.exp(btot - b)).astype(BF16)
    st_ref[...] = st * jnp.exp(btot) + lax.dot_general(v.astype(BF16), kd, _TN,
                                                       preferred_element_type=F32)
    return o


def _gla_kernel(*refs, n_tok, has_state):
    if has_state:
        (q_ref, k_ref, v_ref, r_ref, w2_ref, bg_ref, gn_ref, s0f_ref, s0b_ref,
         o_ref, lg_ref, of_ref, ob_ref, stf_ref, stb_ref) = refs
    else:
        (q_ref, k_ref, v_ref, r_ref, w2_ref, bg_ref, gn_ref,
         o_ref, sf_ref, sb_ref, lg_ref, of_ref, ob_ref, stf_ref, stb_ref) = refs
    nc = n_tok // GLA_C
    scale = GLA_DK ** -0.5

    rb = r_ref[...].astype(BF16)
    for z in range(2):
        x = _bdot(rb, w2_ref[z].astype(BF16)) + bg_ref[z]
        lg_ref[z] = (jnp.minimum(x, 0.0) - jnp.log1p(jnp.exp(-jnp.abs(x)))) * (1.0 / GLA_GATE_NORM)

    for hh in range(GLA_HPS):
        if has_state:
            stf_ref[hh] = s0f_ref[0, hh].T
            stb_ref[hh] = s0b_ref[0, hh].T
        else:
            stf_ref[hh] = jnp.zeros((GLA_DV, GLA_DK), F32)
            stb_ref[hh] = jnp.zeros((GLA_DV, GLA_DK), F32)

    masks = {rev: _gla_masks(rev) for rev in (False, True)}

    def step(ci, carry):
        for hh in range(GLA_HPS):
            kq = slice(hh * GLA_DK, (hh + 1) * GLA_DK)
            vv = slice(hh * GLA_DV, (hh + 1) * GLA_DV)
            for rev in (False, True):
                cc = (nc - 1 - ci) if rev else ci
                sl = pl.ds(pl.multiple_of(cc * GLA_C, GLA_C), GLA_C)
                o = _gla_chunk(q_ref[sl, kq] * scale, k_ref[sl, kq], v_ref[sl, vv],
                               lg_ref[1 if rev else 0, sl, kq],
                               (stb_ref if rev else stf_ref).at[hh], rev, masks[rev])
                (ob_ref if rev else of_ref)[sl, vv] = o
        return carry

    lax.fori_loop(0, nc, step, 0)

    for hh in range(GLA_HPS):
        vv = slice(hh * GLA_DV, (hh + 1) * GLA_DV)
        o = of_ref[:, vv] + ob_ref[:, vv]
        o_ref[:, vv] = o * lax.rsqrt(jnp.mean(o * o, axis=-1, keepdims=True) + EPS) * gn_ref[...]
        if not has_state:
            sf_ref[0, hh] = stf_ref[hh].T
            sb_ref[0, hh] = stb_ref[hh].T


def _gla(proj, w2, bg, gnorm, n_seq, n_tok, row_block0, states=None):
    kw, vw = GLA_HPS * GLA_DK, GLA_HPS * GLA_DV
    spec = lambda width, off: pl.BlockSpec((n_tok, width), lambda b, h: (row_block0 + b, off + h))
    in_specs = [spec(kw, 0), spec(kw, GLA_HK // kw), spec(vw, 2 * GLA_HK // vw),
                pl.BlockSpec((n_tok, 128), lambda b, h: (row_block0 + b, (2 * GLA_HK + 2 * GLA_HV) // 128)),
                pl.BlockSpec((2, 128, kw), lambda b, h: (0, 0, h)),
                pl.BlockSpec((2, 1, kw), lambda b, h: (0, 0, h)),
                pl.BlockSpec((1, GLA_DV), lambda b, h: (0, 0))]
    args = [proj, proj, proj, proj, w2, bg, gnorm.reshape(1, GLA_DV)]
    st_spec = pl.BlockSpec((1, GLA_HPS, GLA_DK, GLA_DV), lambda b, h: (b, h, 0, 0))
    o_spec = pl.BlockSpec((n_tok, vw), lambda b, h: (b, h))
    o_shape = jax.ShapeDtypeStruct((n_seq * n_tok, GLA_HV), F32)
    if states is not None:
        in_specs += [st_spec, st_spec]
        args += list(states)
        out_specs, out_shape = o_spec, o_shape
    else:
        st_shape = jax.ShapeDtypeStruct((n_seq, GLA_HEADS, GLA_DK, GLA_DV), F32)
        out_specs, out_shape = [o_spec, st_spec, st_spec], [o_shape, st_shape, st_shape]
    return pl.pallas_call(
        functools.partial(_gla_kernel, n_tok=n_tok, has_state=states is not None),
        grid=(n_seq, GLA_HEADS // GLA_HPS),
        in_specs=in_specs,
        out_specs=out_specs,
        out_shape=out_shape,
        scratch_shapes=[pltpu.VMEM((2, n_tok, kw), F32),
                        pltpu.VMEM((n_tok, vw), F32),
                        pltpu.VMEM((n_tok, vw), F32),
                        pltpu.VMEM((GLA_HPS, GLA_DV, GLA_DK), F32),
                        pltpu.VMEM((GLA_HPS, GLA_DV, GLA_DK), F32)],
        compiler_params=_params(("arbitrary", "arbitrary")),
        name="gla_latent" if states is not None else "gla_prompt",
    )(*args)


DIFF_HB = 2
DIFF_QL = DIFF_HB * DIFF_HD
DIFF_VL = DIFF_HB * 2 * DIFF_HD
DIFF_TQ = 1024
DIFF_SUB = 128


def _diff_lambda(lam_ref):
    l = lam_ref[...]
    a = jnp.sum(l[0:1] * l[1:2], axis=-1, keepdims=True)
    b = jnp.sum(l[2:3] * l[3:4], axis=-1, keepdims=True)
    return jnp.exp(a) - jnp.exp(b) + DIFF_LAMBDA_INIT


def _diff_finish(ps, invs, lam, v, sn_ref):
    a = ps[0] * invs[0] - (lam * invs[1]) * ps[1]
    o2 = _bdot(a.astype(BF16), v)
    o = _unstack_heads(o2, DIFF_HB, 2 * DIFF_HD)
    return _group_rms(o, sn_ref[...], 2 * DIFF_HD) * (1.0 - DIFF_LAMBDA_INIT)


def _diff_prompt_kernel(q0_ref, q1_ref, k0_ref, k1_ref, v_ref, gq_ref, gk_ref, lam_ref, sn_ref,
                        o_ref, kn_ref, vn_ref):
    scale = DIFF_HD ** -0.5 * LOG2E
    lam = _diff_lambda(lam_ref)

    def chains(bound):
        shift = None if bound is None else _stacked_rows(bound, SEQ)
        for seq in range(PROMPT_SEQS):
            sl = slice(seq * SEQ, (seq + 1) * SEQ)
            ps, invs = [], []
            for comp, (q_ref, k_ref) in enumerate(((q0_ref, k0_ref), (q1_ref, k1_ref))):
                q = _group_rms(q_ref[sl, :], gq_ref[...], DIFF_HD, sums_on_mxu=True) * scale
                k = _group_rms(k_ref[sl, :], gk_ref[...], DIFF_HD, sums_on_mxu=True)
                for h in range(DIFF_HB):
                    kn_ref[seq, comp, h] = k[:, h * DIFF_HD:(h + 1) * DIFF_HD]
                q2 = _stack_heads(q, DIFF_HB, DIFF_HD).astype(BF16)
                s = lax.dot_general(q2, k.astype(BF16), _NT, preferred_element_type=F32)
                (p,), inv = _softmax_parts([s], shift)
                ps.append(p)
                invs.append(inv)
            v = v_ref[sl, :]
            _store_heads(vn_ref, seq, v, DIFF_HB, 2 * DIFF_HD)
            o_ref[sl, :] = _diff_finish(ps, invs, lam, v.astype(BF16), sn_ref).astype(BF16)

    _with_score_bound(scale * _rms_norm_bound(gq_ref, DIFF_HD, DIFF_HD)
                      * _rms_norm_bound(gk_ref, DIFF_HD, DIFF_HD), chains)


def _diff_prompt(qkv, gq, gk, lam, sub_norm):
    nb = DIFF_HEADS // DIFF_HB
    rows = PROMPT_SEQS * SEQ
    qk = lambda off: pl.BlockSpec((rows, DIFF_QL), lambda b, j: (b, off + j))
    vec = lambda n: pl.BlockSpec((1, n), lambda b, j: (0, 0))
    v_spec = pl.BlockSpec((rows, DIFF_VL), lambda b, j: (b, 2 * D_MODEL // DIFF_VL + j))
    kn_spec = pl.BlockSpec((PROMPT_SEQS, 2, DIFF_HB, SEQ, DIFF_HD), lambda b, j: (b, 0, j, 0, 0))
    vn_spec = pl.BlockSpec((PROMPT_SEQS, DIFF_HB, SEQ, 2 * DIFF_HD), lambda b, j: (b, j, 0, 0))
    return pl.pallas_call(
        _diff_prompt_kernel,
        grid=(BATCH // PROMPT_SEQS, nb),
        in_specs=[qk(0), qk(nb), qk(2 * nb), qk(3 * nb), v_spec, vec(DIFF_QL), vec(DIFF_QL),
                  pl.BlockSpec((4, DIFF_HD), lambda b, j: (0, 0)), vec(DIFF_VL)],
        out_specs=[pl.BlockSpec((rows, DIFF_VL), lambda b, j: (b, j)), kn_spec, vn_spec],
        out_shape=[jax.ShapeDtypeStruct((N_PROMPT, D_MODEL), BF16),
                   jax.ShapeDtypeStruct((BATCH, 2, DIFF_HEADS, SEQ, DIFF_HD), F32),
                   jax.ShapeDtypeStruct((BATCH, DIFF_HEADS, SEQ, 2 * DIFF_HD), F32)],
        compiler_params=_params(("arbitrary", "arbitrary")),
        name="diff_prompt",
    )(qkv, qkv, qkv, qkv, qkv, jnp.tile(gq, DIFF_HB).reshape(1, -1),
      jnp.tile(gk, DIFF_HB).reshape(1, -1), lam, jnp.tile(sub_norm, DIFF_HB).reshape(1, -1))


def _diff_latent_kernel(q0_ref, q1_ref, k0_ref, k1_ref, v_ref, kc0_ref, kc1_ref, vc_ref,
                        cos_ref, sin_ref, cosq_ref, sinq_ref, gq_ref, gk_ref, lam_ref, sn_ref,
                        o_ref, kb0_ref, kb1_ref, vb_ref, kmax_ref):
    scale = DIFF_HD ** -0.5 * LOG2E
    half = DIFF_HD // 4
    rows = 256

    @pl.when(pl.program_id(2) == 0)
    def _():
        for comp, (k_ref, kc_ref, kb_ref) in enumerate(
                ((k0_ref, kc0_ref, kb0_ref), (k1_ref, kc1_ref, kb1_ref))):
            kc = kc_ref[0]
            kb_ref[0:PAST_LEN, :] = kc.astype(BF16)
            kmax_ref[comp:comp + 1, :] = jnp.maximum(
                _max_head_norms(kc, DIFF_HD), _rms_norm_bound(gk_ref, DIFF_HD, DIFF_HD))

            def prep(r, carry):
                sl = pl.ds(pl.multiple_of(r * rows, rows), rows)
                k = _group_rms(k_ref[sl, :], gk_ref[...], DIFF_HD)
                k = _rope(k, cos_ref[sl, :], sin_ref[sl, :], half)
                kb_ref[pl.ds(pl.multiple_of(PAST_LEN + r * rows, rows), rows), :] = k.astype(BF16)
                return carry

            lax.fori_loop(0, DEC_SEQ // rows, prep, 0, unroll=2)
        vb_ref[0:PAST_LEN, :] = vc_ref[0].astype(BF16)
        vb_ref[PAST_LEN:, :] = v_ref[...].astype(BF16)

    lam = _diff_lambda(lam_ref)

    def chains(bound):
        for r0 in range(0, DIFF_TQ, DIFF_SUB):
            sl = slice(r0, r0 + DIFF_SUB)
            ps, invs = [], []
            for comp, (q_ref, kb_ref) in enumerate(((q0_ref, kb0_ref), (q1_ref, kb1_ref))):
                shift = None if bound is None else _stacked_rows(
                    bound[:, comp * DIFF_HB:(comp + 1) * DIFF_HB], DIFF_SUB)
                q = _group_rms(q_ref[sl, :], gq_ref[...], DIFF_HD)
                q = _rope(q, cosq_ref[sl, :], sinq_ref[sl, :], half) * scale
                q2 = _stack_heads(q, DIFF_HB, DIFF_HD).astype(BF16)
                s = lax.dot_general(q2, kb_ref[...], _NT, preferred_element_type=F32)
                (p,), inv = _softmax_parts([s], shift)
                ps.append(p)
                invs.append(inv)
            o_ref[sl, :] = _diff_finish(ps, invs, lam, vb_ref[...], sn_ref).astype(BF16)

    qmax = scale * _rms_norm_bound(gq_ref, DIFF_HD, DIFF_HD)
    _with_score_bound(jnp.concatenate([qmax * kmax_ref[0:1, :], qmax * kmax_ref[1:2, :]], axis=-1),
                      chains)


def _diff_latent(qkv, cache_k, cache_v, cos, sin, gq, gk, lam, sub_norm):
    nb = DIFF_HEADS // DIFF_HB
    nq = DEC_SEQ // DIFF_TQ
    q0 = N_PROMPT // DIFF_TQ
    lat0 = N_PROMPT // DEC_SEQ
    n_keys = PAST_LEN + DEC_SEQ
    q_spec = lambda off: pl.BlockSpec((DIFF_TQ, DIFF_QL), lambda b, j, t: (q0 + b * nq + t, off + j))
    k_spec = lambda off: pl.BlockSpec((DEC_SEQ, DIFF_QL), lambda b, j, t: (lat0 + b, off + j))
    v_spec = pl.BlockSpec((DEC_SEQ, DIFF_VL), lambda b, j, t: (lat0 + b, 2 * D_MODEL // DIFF_VL + j))
    kc_spec = lambda off: pl.BlockSpec((1, PAST_LEN, DIFF_QL), lambda b, j, t: (b, 0, off + j))
    vc_spec = pl.BlockSpec((1, PAST_LEN, DIFF_VL), lambda b, j, t: (b, 0, j))
    tab = pl.BlockSpec((DEC_SEQ, DIFF_QL), lambda b, j, t: (0, 0))
    tabq = pl.BlockSpec((DIFF_TQ, DIFF_QL), lambda b, j, t: (t, 0))
    vec = lambda n: pl.BlockSpec((1, n), lambda b, j, t: (0, 0))
    return pl.pallas_call(
        _diff_latent_kernel,
        grid=(DEC_BATCH, nb, nq),
        in_specs=[q_spec(0), q_spec(nb), k_spec(2 * nb), k_spec(3 * nb), v_spec,
                  kc_spec(0), kc_spec(nb), vc_spec, tab, tab, tabq, tabq,
                  vec(DIFF_QL), vec(DIFF_QL),
                  pl.BlockSpec((4, DIFF_HD), lambda b, j, t: (0, 0)), vec(DIFF_VL)],
        out_specs=pl.BlockSpec((DIFF_TQ, DIFF_VL), lambda b, j, t: (b * nq + t, j)),
        out_shape=jax.ShapeDtypeStruct((N_LATENT, D_MODEL), BF16),
        scratch_shapes=[pltpu.VMEM((n_keys, DIFF_QL), BF16),
                        pltpu.VMEM((n_keys, DIFF_QL), BF16),
                        pltpu.VMEM((n_keys, DIFF_VL), BF16),
                        pltpu.VMEM((2, DIFF_HB), F32)],
        compiler_params=_params(("arbitrary", "arbitrary", "arbitrary")),
        name="diff_latent",
    )(qkv, qkv, qkv, qkv, qkv, _tokens_first(cache_k), _tokens_first(cache_k),
      _tokens_first(cache_v), cos, sin, cos, sin,
      jnp.tile(gq, DIFF_HB).reshape(1, -1), jnp.tile(gk, DIFF_HB).reshape(1, -1), lam,
      jnp.tile(sub_norm, DIFF_HB).reshape(1, -1))


MLA_HB = 2
MLA_HL = 128
MLA_LANES = MLA_HB * MLA_HL
MLA_TQ = 1024
MLA_SUB = 128


def _mla_keys(kv, kr, gk, sums_on_mxu=False):
    lane = lax.broadcasted_iota(jnp.int32, kv.shape, 1)
    kr2 = jnp.concatenate([kr] * MLA_HB, axis=1)
    k = jnp.where((lane & (MLA_HL - 1)) < MLA_NOPE, kv, kr2)
    return _group_rms(k, gk, MLA_HL, n_real=MLA_QK, sums_on_mxu=sums_on_mxu)


def _mla_out(o2):
    tq = o2.shape[0] // MLA_HB
    oa = pltpu.roll(o2[0:tq, 0:MLA_HL], MLA_HL - MLA_V, axis=1)
    ob = o2[tq:, MLA_HL:]
    lane = lax.broadcasted_iota(jnp.int32, oa.shape, 1)
    return jnp.where(lane < MLA_V, oa, ob)


def _mla_prompt_kernel(q_ref, kv_ref, kr_ref, gq_ref, gk_ref, o_ref):
    scale = MLA_QK ** -0.5 * LOG2E

    def chains(bound):
        shift = None if bound is None else _stacked_rows(bound, SEQ)
        for seq in range(PROMPT_SEQS):
            sl = slice(seq * SEQ, (seq + 1) * SEQ)
            q = _group_rms(q_ref[sl, :], gq_ref[...], MLA_HL, n_real=MLA_QK) * scale
            kv = kv_ref[sl, :]
            k = _mla_keys(kv, kr_ref[sl, :], gk_ref[...])
            q2 = _stack_heads(q, MLA_HB, MLA_HL).astype(BF16)
            s = lax.dot_general(q2, k.astype(BF16), _NT, preferred_element_type=F32)
            (p,), inv = _softmax_parts([s], shift)
            o_ref[sl, :] = _mla_out(_bdot(p.astype(BF16), kv.astype(BF16)) * inv).astype(BF16)

    _with_score_bound(scale * _rms_norm_bound(gq_ref, MLA_QK, MLA_HL)
                      * _rms_norm_bound(gk_ref, MLA_QK, MLA_HL), chains)


def _mla_prompt(qp, kvp, low, gq, gk):
    nb = MLA_HEADS // MLA_HB
    rows = PROMPT_SEQS * SEQ
    blk = pl.BlockSpec((rows, MLA_LANES), lambda b, j: (b, j))
    vec = pl.BlockSpec((1, MLA_LANES), lambda b, j: (0, 0))
    return pl.pallas_call(
        _mla_prompt_kernel,
        grid=(BATCH // PROMPT_SEQS, nb),
        in_specs=[blk, blk, pl.BlockSpec((rows, MLA_HL), lambda b, j: (b, 0)),
                  vec, vec],
        out_specs=pl.BlockSpec((rows, MLA_HB * MLA_V), lambda b, j: (b, j)),
        out_shape=jax.ShapeDtypeStruct((N_PROMPT, MLA_HEADS * MLA_V), BF16),
        compiler_params=_params(("arbitrary", "arbitrary")),
        name="mla_prompt",
    )(qp, kvp, low, gq, gk)


def _mla_latent_kernel(q_ref, kv_ref, kr_ref, kvc_ref, krc_ref, cos_ref, sin_ref, cosq_ref,
                       sinq_ref, gq_ref, gk_ref, o_ref, kb_ref, vb_ref, kmax_ref):
    scale = MLA_QK ** -0.5 * LOG2E
    half = MLA_ROPE // 4
    rows = 256

    @pl.when(pl.program_id(2) == 0)
    def _():
        kvc = kvc_ref[...]
        kc = _mla_keys(kvc, krc_ref[...], gk_ref[...])
        kb_ref[0:PAST_LEN, :] = kc.astype(BF16)
        vb_ref[0:PAST_LEN, :] = kvc.astype(BF16)

        def prep(r, carry):
            sl = pl.ds(pl.multiple_of(r * rows, rows), rows)
            dst = pl.ds(pl.multiple_of(PAST_LEN + r * rows, rows), rows)
            kv = kv_ref[sl, :]
            k = _mla_keys(kv, kr_ref[sl, :], gk_ref[...], sums_on_mxu=True)
            k = _rope(k, cos_ref[sl, :], sin_ref[sl, :], half)
            kb_ref[dst, :] = k.astype(BF16)
            vb_ref[dst, :] = kv.astype(BF16)
            return carry

        lax.fori_loop(0, DEC_SEQ // rows, prep, 0, unroll=2)
        kmax_ref[...] = jnp.maximum(_max_head_norms(kc, MLA_HL),
                                    _rms_norm_bound(gk_ref, MLA_QK, MLA_HL))

    def chains(bound):
        shift = None if bound is None else _stacked_rows(bound, MLA_SUB)
        for r0 in range(0, MLA_TQ, MLA_SUB):
            sl = slice(r0, r0 + MLA_SUB)
            q = _group_rms(q_ref[sl, :], gq_ref[...], MLA_HL, n_real=MLA_QK)
            q = _rope(q, cosq_ref[sl, :], sinq_ref[sl, :], half) * scale
            q2 = _stack_heads(q, MLA_HB, MLA_HL).astype(BF16)
            s = lax.dot_general(q2, kb_ref[...], _NT, preferred_element_type=F32)
            (p,), inv = _softmax_parts([s], shift)
            o_ref[sl, :] = _mla_out(_bdot(p.astype(BF16), vb_ref[...]) * inv).astype(BF16)

    _with_score_bound(scale * _rms_norm_bound(gq_ref, MLA_QK, MLA_HL) * kmax_ref[...], chains)


def _mla_latent(qp, kvp, low, kvc, krc, cos, sin, gq, gk):
    nb = MLA_HEADS // MLA_HB
    nq = DEC_SEQ // MLA_TQ
    q0 = N_PROMPT // MLA_TQ
    lat0 = N_PROMPT // DEC_SEQ
    n_keys = PAST_LEN + DEC_SEQ
    tab = pl.BlockSpec((DEC_SEQ, MLA_LANES), lambda b, j, t: (0, 0))
    tabq = pl.BlockSpec((MLA_TQ, MLA_LANES), lambda b, j, t: (t, 0))
    vec = pl.BlockSpec((1, MLA_LANES), lambda b, j, t: (0, 0))
    return pl.pallas_call(
        _mla_latent_kernel,
        grid=(DEC_BATCH, nb, nq),
        in_specs=[pl.BlockSpec((MLA_TQ, MLA_LANES), lambda b, j, t: (q0 + b * nq + t, j)),
                  pl.BlockSpec((DEC_SEQ, MLA_LANES), lambda b, j, t: (lat0 + b, j)),
                  pl.BlockSpec((DEC_SEQ, MLA_HL), lambda b, j, t: (lat0 + b, 0)),
                  pl.BlockSpec((PAST_LEN, MLA_LANES), lambda b, j, t: (b, j)),
                  pl.BlockSpec((PAST_LEN, MLA_HL), lambda b, j, t: (b, 0)),
                  tab, tab, tabq, tabq, vec, vec],
        out_specs=pl.BlockSpec((MLA_TQ, MLA_HB * MLA_V), lambda b, j, t: (b * nq + t, j)),
        out_shape=jax.ShapeDtypeStruct((N_LATENT, MLA_HEADS * MLA_V), BF16),
        scratch_shapes=[pltpu.VMEM((n_keys, MLA_LANES), BF16),
                        pltpu.VMEM((n_keys, MLA_LANES), BF16),
                        pltpu.VMEM((1, MLA_HB), F32)],
        compiler_params=_params(("arbitrary", "arbitrary", "arbitrary")),
        name="mla_latent",
    )(qp, kvp, low, kvc, krc, cos, sin, cos, sin, gq, gk)


MLA_LOW_Q = 0
MLA_LOW_KV = 512
MLA_LOW_KR = 768
MLA_LOW_N = 896


def _axial_tables(n_tok, rdim):
    nf = rdim // 4
    freqs = ROPE_BASE ** (-jnp.arange(nf, dtype=F32) / nf)
    t = jnp.arange(n_tok)
    rowp = (t // GRID_W).astype(F32)
    colp = (t % GRID_W).astype(F32)
    ang = jnp.stack([rowp[:, None] * freqs, colp[:, None] * freqs], axis=1)
    cos, sin = jnp.cos(ang), jnp.sin(ang)
    cos_l = jnp.stack([cos, cos], axis=2).reshape(n_tok, rdim)
    sin_l = jnp.stack([-sin, sin], axis=2).reshape(n_tok, rdim)
    return cos_l, sin_l


def _diff_rope_tables():
    cos, sin = _axial_tables(DEC_SEQ, DIFF_HD)
    return jnp.tile(cos, (1, DIFF_HB)), jnp.tile(sin, (1, DIFF_HB))


def _mla_rope_tables():
    cos, sin = _axial_tables(DEC_SEQ, MLA_ROPE)
    ones = jnp.ones((DEC_SEQ, MLA_NOPE), F32)
    pad1 = jnp.ones((DEC_SEQ, MLA_HL - MLA_QK), F32)
    cos_h = jnp.concatenate([ones, cos, pad1], axis=1)
    sin_h = jnp.concatenate([0 * ones, sin, 0 * pad1], axis=1)
    return jnp.tile(cos_h, (1, MLA_HB)), jnp.tile(sin_h, (1, MLA_HB))


def _tokens_first(cache):
    b, h, l, d = cache.shape
    return jnp.transpose(cache, (0, 2, 1, 3)).reshape(b, l, h * d)


def _pad_heads(w, heads, hd, hl):
    k = w.shape[0]
    return jnp.pad(w.reshape(k, heads, hd), ((0, 0), (0, 0), (0, hl - hd))).reshape(k, heads * hl)


def kernel(x_prompt, x_sample, cache_l0_k, cache_l0_v, state_l1_fwd, state_l1_bwd, cache_l2_k,
           cache_l2_v, cache_l3_ckv, cache_l3_krope, c, c_ctx, ada_w, ada_b, norm_mix, norm_ffn,
           ffn_w_up, ffn_conv_w, ffn_conv_b, ffn_w_down, na_w_qkv, na_q_norm, na_k_norm, na_bias,
           na_w_o, gla_w_qkvg, gla_w_gate1, gla_w_gate2, gla_b_gate, gla_o_norm, gla_w_o,
           diff_w_qkv, diff_q_norm, diff_k_norm, diff_lambda, diff_sub_norm, diff_w_o, mla_w_dq,
           mla_q_a_norm, mla_w_uq, mla_w_dkv, mla_kv_a_norm, mla_w_ukv, mla_q_norm, mla_k_norm,
           mla_w_o):
    xr = _Rows(x_prompt.reshape(N_PROMPT, D_MODEL), x_sample.reshape(N_LATENT, D_MODEL), 0)
    cvecs = jnp.concatenate([c_ctx[None], c, jnp.zeros((5, D_MODEL), F32)], axis=0)
    mods_all = _ada_mods(cvecs, ada_w, ada_b)
    halves = lambda o_p, o_s: _Rows(o_p, o_s, 0)

    mods = mods_all[0]
    qkv = _norm_mod_proj(xr, norm_mix[0], mods, [na_w_qkv], "na_qkv")
    o_p, new_l0_k, new_l0_v = _na_prompt(qkv, na_q_norm, na_k_norm)
    o_s = _na_latent(qkv, cache_l0_k, cache_l0_v, _na_bias_blocks(na_bias), na_q_norm, na_k_norm)
    x = _out_proj_residual(xr, halves(o_p, o_s), mods, na_w_o, "na_out")
    ffn_weights = (ffn_w_up, ffn_conv_w, ffn_conv_b, ffn_w_down)
    x = _ffn(x, norm_ffn[0], mods, 0, *ffn_weights)
    xr = _one_array(x)

    mods = mods_all[1]
    w_decay = jnp.concatenate(
        [gla_w_gate1[0], gla_w_gate1[1],
         jnp.zeros((D_MODEL, 128 - 2 * GLA_GATE_RANK), F32)], axis=1)
    proj = _norm_mod_proj(xr, norm_mix[1], mods, [gla_w_qkvg, w_decay], "gla_proj")
    w2 = jnp.zeros((2, 128, GLA_HK), F32)
    w2 = w2.at[0, :GLA_GATE_RANK].set(gla_w_gate2[0])
    w2 = w2.at[1, GLA_GATE_RANK:2 * GLA_GATE_RANK].set(gla_w_gate2[1])
    bg = gla_b_gate.reshape(2, 1, GLA_HK)
    o_p, new_l1_fwd, new_l1_bwd = _gla(proj, w2, bg, gla_o_norm, BATCH, SEQ, 0)
    o_s = _gla(proj, w2, bg, gla_o_norm, DEC_BATCH, DEC_SEQ, N_PROMPT // DEC_SEQ,
               states=(state_l1_fwd, state_l1_bwd))
    x = _out_proj_residual(xr, halves(o_p, o_s), mods, gla_w_o, "gla_out",
                           gate=proj, gate_col_block=(2 * GLA_HK + GLA_HV) // GLA_HV)
    x = _ffn(x, norm_ffn[1], mods, 1, *ffn_weights)
    xr = _one_array(x)

    mods = mods_all[2]
    qkv = _norm_mod_proj(xr, norm_mix[2], mods, [diff_w_qkv], "diff_qkv")
    o_p, kn_p, new_l2_v = _diff_prompt(qkv, diff_q_norm, diff_k_norm, diff_lambda, diff_sub_norm)
    new_l2_k = kn_p.reshape(BATCH, 2 * DIFF_HEADS, SEQ, DIFF_HD)
    cos_d, sin_d = _diff_rope_tables()
    o_s = _diff_latent(qkv, cache_l2_k, cache_l2_v, cos_d, sin_d, diff_q_norm, diff_k_norm,
                       diff_lambda, diff_sub_norm)
    x = _out_proj_residual(xr, halves(o_p, o_s), mods, diff_w_o, "diff_out")
    x = _ffn(x, norm_ffn[2], mods, 2, *ffn_weights)
    xr = _one_array(x)

    mods = mods_all[3]
    zc = lambda n: jnp.zeros((D_MODEL, n), F32)
    w_low = jnp.concatenate(
        [mla_w_dq, zc(MLA_LOW_KV - MLA_Q_RANK), mla_w_dkv[:, :MLA_KV_RANK],
         zc(MLA_NOPE), mla_w_dkv[:, MLA_KV_RANK:], zc(MLA_HL - MLA_QK)], axis=1)
    w_uq = _pad_heads(mla_w_uq, MLA_HEADS, MLA_QK, MLA_HL)
    qp, kvp, ckv, kr = _mla_front(xr, norm_mix[3], mods, w_low, mla_q_a_norm, w_uq,
                                  mla_kv_a_norm, mla_w_ukv)
    kvc = _matmul(cache_l3_ckv.reshape(DEC_BATCH * PAST_LEN, MLA_KV_RANK), mla_w_ukv,
                  "mla_ukv_cache")
    krc = jnp.pad(cache_l3_krope.reshape(DEC_BATCH * PAST_LEN, MLA_ROPE),
                  ((0, 0), (MLA_NOPE, MLA_HL - MLA_QK)))
    pad_gain = lambda g: jnp.tile(jnp.pad(g, (0, MLA_HL - MLA_QK)), MLA_HB).reshape(1, -1)
    gq, gk = pad_gain(mla_q_norm), pad_gain(mla_k_norm)
    o_p = _mla_prompt(qp, kvp, kr, gq, gk)
    cos_m, sin_m = _mla_rope_tables()
    o_s = _mla_latent(qp, kvp, kr, kvc, krc, cos_m, sin_m, gq, gk)
    new_l3_ckv = ckv[:N_PROMPT].reshape(BATCH, SEQ, MLA_KV_RANK)
    new_l3_krope = kr[:N_PROMPT, MLA_NOPE:MLA_QK].reshape(BATCH, SEQ, MLA_ROPE)
    x = _out_proj_residual(xr, halves(o_p, o_s), mods, mla_w_o, "mla_out")
    n_pt = N_PROMPT // TOK_TILE
    ffn3 = functools.partial(_ffn, x, norm_ffn[3], mods, 3, *ffn_weights)
    y_prompt = ffn3(tile0=0, n_tiles=n_pt).reshape(BATCH, SEQ, D_MODEL)
    y_sample = ffn3(tile0=n_pt, n_tiles=N_TOK_TILES - n_pt).reshape(DEC_BATCH, DEC_SEQ, D_MODEL)
    return (y_prompt, y_sample, new_l0_k, new_l0_v, new_l1_fwd, new_l1_bwd, new_l2_k, new_l2_v,
            new_l3_ckv, new_l3_krope)
```

```python
import functools
import math
from typing import NamedTuple

import jax
import jax.numpy as jnp
from jax import lax
from jax.experimental import pallas as pl
from jax.experimental.pallas import tpu as pltpu

F32 = jnp.float32
BF16 = jnp.bfloat16

D_MODEL = 1024
BATCH = 16
SEQ = 256
DEPTH = 4
DEC_BATCH = 2
DEC_SEQ = 2048
PAST_LEN = 256
GRID_W = 64
D_FF = 2816
EPS = 1e-6
ROPE_BASE = 10000.0

NA_HEADS = 16
NA_HD = 64
NA_WIN_R = 8
NA_WIN_C = 16

GLA_HEADS = 4
GLA_DK = 128
GLA_DV = 256
GLA_HK = GLA_HEADS * GLA_DK
GLA_HV = GLA_HEADS * GLA_DV
GLA_GATE_RANK = 16
GLA_GATE_NORM = 16.0

DIFF_HEADS = 8
DIFF_HD = 64
DIFF_LAMBDA_INIT = 0.8 - 0.6 * math.exp(-0.3 * 2)

MLA_HEADS = 16
MLA_Q_RANK = 384
MLA_KV_RANK = 256
MLA_NOPE = 64
MLA_ROPE = 32
MLA_V = 64
MLA_QK = MLA_NOPE + MLA_ROPE

N_PROMPT = BATCH * SEQ
N_LATENT = DEC_BATCH * DEC_SEQ
N_TOK = N_PROMPT + N_LATENT
TOK_TILE = 2048
N_TOK_TILES = N_TOK // TOK_TILE
FF_CHUNK = 256
N_FF_CHUNKS = D_FF // FF_CHUNK
NEG = -1e30
LOG2E = math.log2(math.e)
SAFE_SCORE_BOUND = 48.0

VMEM_LIMIT = 56 * 1024 * 1024

_NT = (((1,), (1,)), ((), ()))
_TN = (((0,), (0,)), ((), ()))


def _params(sem, vmem=VMEM_LIMIT):
    return pltpu.CompilerParams(dimension_semantics=sem, vmem_limit_bytes=vmem)


def _log2(n):
    assert n & (n - 1) == 0
    return n.bit_length() - 1


def _silu(x):
    return x / (1.0 + jnp.exp(-x))


def _bdot(a, b):
    return jnp.dot(a, b, preferred_element_type=F32)


def _softmax_parts(parts, shift=None):
    m = shift
    if m is None:
        m = parts[0].max(axis=-1, keepdims=True)
        for s in parts[1:]:
            m = jnp.maximum(m, s.max(axis=-1, keepdims=True))
    ps = [jnp.exp2(s - m) for s in parts]
    l = ps[0].sum(axis=-1, keepdims=True)
    for p in ps[1:]:
        l = l + p.sum(axis=-1, keepdims=True)
    return ps, 1.0 / l


def _group_rms(x, gain, group, n_real=None, sums_on_mxu=False):
    lanes = x.shape[-1]
    n_real = n_real or group
    x2 = x * x
    if group == lanes:
        ms = jnp.sum(x2, axis=-1, keepdims=True)
    elif not sums_on_mxu:
        gid = lax.broadcasted_iota(jnp.int32, x.shape, 1) >> _log2(group)
        ms = jnp.zeros_like(x)
        for i in range(lanes // group):
            sel = gid == i
            si = jnp.sum(jnp.where(sel, x2, 0.0), axis=-1, keepdims=True)
            ms = jnp.where(sel, si, ms)
    else:
        r = lax.broadcasted_iota(jnp.int32, (lanes, lanes), 0) >> _log2(group)
        c = lax.broadcasted_iota(jnp.int32, (lanes, lanes), 1) >> _log2(group)
        ones = jnp.where(r == c, 1.0, 0.0).astype(BF16)
        hi = x2.astype(BF16)
        lo = (x2 - hi.astype(F32)).astype(BF16)
        ms = _bdot(hi, ones) + _bdot(lo, ones)
    return x * lax.rsqrt(ms * (1.0 / n_real) + EPS) * gain


def _with_score_bound(bound, body):
    ok = bound.max() < SAFE_SCORE_BOUND
    pl.when(ok)(lambda: body(bound))
    pl.when(jnp.logical_not(ok))(lambda: body(None))


def _rms_norm_bound(g_ref, n, head_lanes):
    g = jnp.abs(g_ref[...])
    heads = g.shape[-1] // head_lanes
    return n ** 0.5 * jnp.concatenate(
        [g[:, h * head_lanes:(h + 1) * head_lanes].max(axis=-1, keepdims=True)
         for h in range(heads)], axis=-1)


def _stacked_rows(per_head, rows):
    r = lax.broadcasted_iota(jnp.int32, (per_head.shape[-1] * rows, 1), 0)
    out = per_head[:, 0:1]
    for h in range(1, per_head.shape[-1]):
        out = jnp.where(r >= h * rows, per_head[:, h:h + 1], out)
    return out


def _max_head_norms(x, head_lanes):
    hid = lax.broadcasted_iota(jnp.int32, x.shape, 1) >> _log2(head_lanes)
    x2 = x * x
    sq = [jnp.sum(jnp.where(hid == h, x2, 0.0), axis=-1, keepdims=True).max(axis=0, keepdims=True)
          for h in range(x.shape[-1] // head_lanes)]
    return jnp.sqrt(jnp.concatenate(sq, axis=-1))


def _rope(x, cos, sin, half):
    lanes = x.shape[-1]
    lane = lax.broadcasted_iota(jnp.int32, x.shape, 1)
    up = pltpu.roll(x, lanes - half, axis=1)
    dn = pltpu.roll(x, half, axis=1)
    swapped = jnp.where((lane & (2 * half - 1)) < half, up, dn)
    return x * cos + swapped * sin


def _stack_heads(q, n_heads, head_lanes):
    hid = lax.broadcasted_iota(jnp.int32, q.shape, 1) >> _log2(head_lanes)
    zero = jnp.zeros_like(q)
    return jnp.concatenate([jnp.where(hid == i, q, zero) for i in range(n_heads)], axis=0)


def _unstack_heads(o, n_heads, head_lanes):
    rows = o.shape[0] // n_heads
    hid = lax.broadcasted_iota(jnp.int32, (rows, o.shape[1]), 1) >> _log2(head_lanes)
    out = o[0:rows]
    for i in range(1, n_heads):
        out = jnp.where(hid == i, o[i * rows:(i + 1) * rows], out)
    return out


ADA_TN = 1536


def _ada_kernel(c_ref, w_ref, b_ref, o_ref):
    s = _silu(c_ref[...])
    w = w_ref[0]
    s_hi, w_hi = s.astype(BF16), w.astype(BF16)
    s_lo = (s - s_hi.astype(F32)).astype(BF16)
    w_lo = (w - w_hi.astype(F32)).astype(BF16)
    both = _bdot(jnp.concatenate([s_hi, s_lo], axis=0), w_hi)
    o_ref[0] = both[0:8] + both[8:16] + _bdot(s_hi, w_lo) + b_ref[0]


def _ada_mods(cvecs, ada_w, ada_b):
    out = pl.pallas_call(
        _ada_kernel,
        grid=(DEPTH, 6 * D_MODEL // ADA_TN),
        in_specs=[pl.BlockSpec((8, D_MODEL), lambda l, j: (0, 0)),
                  pl.BlockSpec((1, D_MODEL, ADA_TN), lambda l, j: (l, 0, j)),
                  pl.BlockSpec((1, 1, ADA_TN), lambda l, j: (l, 0, j))],
        out_specs=pl.BlockSpec((1, 8, ADA_TN), lambda l, j: (l, 0, j)),
        out_shape=jax.ShapeDtypeStruct((DEPTH, 8, 6 * D_MODEL), F32),
        compiler_params=_params(("arbitrary", "arbitrary")),
        name="ada_mod",
    )(cvecs, ada_w, ada_b.reshape(DEPTH, 1, 6 * D_MODEL))
    return out.reshape(DEPTH, 8, 6, D_MODEL)[:, :3]


def _mod_group_of_tile(i):
    return jnp.maximum(i - (N_PROMPT // TOK_TILE - 1), 0)


def _norm_mod_rows(x_ref, g_ref, mod_ref, h_ref, shift_idx, scale_idx, rows=64):
    g = g_ref[...]
    sc = 1.0 + mod_ref[0, scale_idx:scale_idx + 1, :]
    sh = mod_ref[0, shift_idx:shift_idx + 1, :]

    def body(r, carry):
        sl = pl.ds(pl.multiple_of(r * rows, rows), rows)
        xf = x_ref[sl, :]
        ms = jnp.mean(xf * xf, axis=-1, keepdims=True)
        y = xf * lax.rsqrt(ms + EPS) * g
        h_ref[sl, :] = (y * sc + sh).astype(BF16)
        return carry

    lax.fori_loop(0, x_ref.shape[0] // rows, body, 0)


ROW_TM = 512


class _Rows(NamedTuple):
    prompt: jax.Array
    latent: jax.Array
    latent_row0: int


def _one_array(x):
    return _Rows(x, x, N_PROMPT)


def _row_specs(rows, width, col_block=0):
    n_p = N_PROMPT // ROW_TM
    l0 = rows.latent_row0 // ROW_TM
    return [pl.BlockSpec((ROW_TM, width), lambda t: (jnp.minimum(t, n_p - 1), col_block)),
            pl.BlockSpec((ROW_TM, width), lambda t: (l0 + jnp.maximum(t - n_p, 0), col_block))]


def _row_group(t):
    first_latent = N_PROMPT // ROW_TM
    return jnp.where(t < first_latent, 0, 1 + (t - first_latent) // (DEC_SEQ // ROW_TM))


def _is_prompt_tile():
    return pl.program_id(0) < N_PROMPT // ROW_TM


def _proj_kernel(xp_ref, xl_ref, g_ref, mod_ref, *refs):
    *w_refs, o_ref, h_ref, wb_ref = refs

    @pl.when(pl.program_id(0) == 0)
    def _():
        off = 0
        for w_ref in w_refs:
            wb_ref[:, off:off + w_ref.shape[1]] = w_ref[...].astype(BF16)
            off += w_ref.shape[1]

    is_prompt = _is_prompt_tile()
    g = g_ref[...]
    sc = 1.0 + mod_ref[0, 1:2, :]
    sh = mod_ref[0, 0:1, :]
    rows = 64
    part = ROW_TM // 2
    for p0 in range(0, ROW_TM, part):
        for r0 in range(p0, p0 + part, rows):
            sl = slice(r0, r0 + rows)
            xf = jnp.where(is_prompt, xp_ref[sl, :], xl_ref[sl, :])
            ms = jnp.mean(xf * xf, axis=-1, keepdims=True)
            y = xf * lax.rsqrt(ms + EPS) * g
            h_ref[sl, :] = (y * sc + sh).astype(BF16)
        o_ref[p0:p0 + part, :] = _bdot(h_ref[p0:p0 + part, :], wb_ref[...])


def _norm_mod_proj(x, g, mods, ws, name):
    n = sum(w.shape[1] for w in ws)
    return pl.pallas_call(
        _proj_kernel,
        grid=(N_TOK // ROW_TM,),
        in_specs=_row_specs(x, D_MODEL) + [
            pl.BlockSpec((1, D_MODEL), lambda t: (0, 0)),
            pl.BlockSpec((1, 6, D_MODEL), lambda t: (_row_group(t), 0, 0))] + [
            pl.BlockSpec(w.shape, lambda t: (0, 0), pipeline_mode=pl.Buffered(1)) for w in ws],
        out_specs=pl.BlockSpec((ROW_TM, n), lambda t: (t, 0)),
        out_shape=jax.ShapeDtypeStruct((N_TOK, n), F32),
        scratch_shapes=[pltpu.VMEM((ROW_TM, D_MODEL), BF16),
                        pltpu.VMEM((D_MODEL, n), BF16)],
        compiler_params=_params(("arbitrary",)),
        name=name,
    )(x.prompt, x.latent, g.reshape(1, D_MODEL), mods, *ws)


def _rms(a, g):
    return a * lax.rsqrt(jnp.mean(a * a, axis=-1, keepdims=True) + EPS) * g


def _mla_front_kernel(xp_ref, xl_ref, g_ref, mod_ref, wl_ref, gqa_ref, wuq_ref, gkv_ref, wukv_ref,
                      qp_ref, kvp_ref, ckv_ref, kr_ref, h_ref, wlb_ref, wuqb_ref, wukvb_ref):
    @pl.when(pl.program_id(0) == 0)
    def _():
        wlb_ref[...] = wl_ref[...].astype(BF16)
        wuqb_ref[...] = wuq_ref[...].astype(BF16)
        wukvb_ref[...] = wukv_ref[...].astype(BF16)

    is_prompt = _is_prompt_tile()
    g = g_ref[...]
    sc = 1.0 + mod_ref[0, 1:2, :]
    sh = mod_ref[0, 0:1, :]
    rows = 64
    part = ROW_TM // 2
    for p0 in range(0, ROW_TM, part):
        for r0 in range(p0, p0 + part, rows):
            sl = slice(r0, r0 + rows)
            xf = jnp.where(is_prompt, xp_ref[sl, :], xl_ref[sl, :])
            h_ref[sl, :] = (_rms(xf, g) * sc + sh).astype(BF16)
        ps = slice(p0, p0 + part)
        low = _bdot(h_ref[ps, :], wlb_ref[...])
        qa = _rms(low[:, MLA_LOW_Q:MLA_LOW_Q + MLA_Q_RANK], gqa_ref[...])
        qp_ref[ps, :] = _bdot(qa.astype(BF16), wuqb_ref[...])
        ckv = _rms(low[:, MLA_LOW_KV:MLA_LOW_KV + MLA_KV_RANK], gkv_ref[...])
        ckv_ref[ps, :] = ckv
        kvp_ref[ps, :] = _bdot(ckv.astype(BF16), wukvb_ref[...])
        kr_ref[ps, :] = low[:, MLA_LOW_KR:MLA_LOW_KR + MLA_HL]


def _mla_front(x, g, mods, w_low, g_qa, w_uq, g_kva, w_ukv):
    n_q, n_kv = w_uq.shape[1], w_ukv.shape[1]
    const = lambda shape: pl.BlockSpec(shape, lambda t: (0, 0), pipeline_mode=pl.Buffered(1))
    out = lambda n: pl.BlockSpec((ROW_TM, n), lambda t: (t, 0))
    shape = lambda n: jax.ShapeDtypeStruct((N_TOK, n), F32)
    return pl.pallas_call(
        _mla_front_kernel,
        grid=(N_TOK // ROW_TM,),
        in_specs=_row_specs(x, D_MODEL) + [
            pl.BlockSpec((1, D_MODEL), lambda t: (0, 0)),
            pl.BlockSpec((1, 6, D_MODEL), lambda t: (_row_group(t), 0, 0)),
            const(w_low.shape), const((1, MLA_Q_RANK)), const(w_uq.shape),
            const((1, MLA_KV_RANK)), const(w_ukv.shape)],
        out_specs=[out(n_q), out(n_kv), out(MLA_KV_RANK), out(MLA_HL)],
        out_shape=[shape(n_q), shape(n_kv), shape(MLA_KV_RANK), shape(MLA_HL)],
        scratch_shapes=[pltpu.VMEM((ROW_TM, D_MODEL), BF16),
                        pltpu.VMEM(w_low.shape, BF16),
                        pltpu.VMEM(w_uq.shape, BF16),
                        pltpu.VMEM(w_ukv.shape, BF16)],
        compiler_params=_params(("arbitrary",)),
        name="mla_front",
    )(x.prompt, x.latent, g.reshape(1, D_MODEL), mods, w_low, g_qa.reshape(1, -1), w_uq,
      g_kva.reshape(1, -1), w_ukv)


def _matmul_kernel(a_ref, w_ref, o_ref):
    o_ref[...] = _bdot(a_ref[...].astype(BF16), w_ref[...].astype(BF16))


def _matmul(a, w, name):
    rows, n = a.shape[0], w.shape[1]
    return pl.pallas_call(
        _matmul_kernel,
        out_shape=jax.ShapeDtypeStruct((rows, n), F32),
        compiler_params=_params(()),
        name=name,
    )(a, w)


def _oproj_kernel(*refs, gated):
    if gated:
        xp_ref, xl_ref, ap_ref, al_ref, g_ref, mod_ref, w_ref, o_ref, wb_ref = refs
    else:
        xp_ref, xl_ref, ap_ref, al_ref, mod_ref, w_ref, o_ref, wb_ref = refs

    @pl.when(pl.program_id(0) == 0)
    def _():
        wb_ref[...] = w_ref[...].astype(BF16)

    is_prompt = _is_prompt_tile()
    a = jnp.where(is_prompt, ap_ref[...], al_ref[...])
    if gated:
        a = a * _silu(g_ref[...])
    y = _bdot(a.astype(BF16), wb_ref[...])
    x = jnp.where(is_prompt, xp_ref[...], xl_ref[...])
    o_ref[...] = x + mod_ref[0, 2:3, :] * y


def _out_proj_residual(x, a, mods, w, name, gate=None, gate_col_block=0):
    k = w.shape[0]
    in_specs = _row_specs(x, D_MODEL) + _row_specs(a, k)
    args = [x.prompt, x.latent, a.prompt, a.latent]
    if gate is not None:
        in_specs.append(pl.BlockSpec((ROW_TM, k), lambda t: (t, gate_col_block)))
        args.append(gate)
    in_specs += [pl.BlockSpec((1, 6, D_MODEL), lambda t: (_row_group(t), 0, 0)),
                 pl.BlockSpec((k, D_MODEL), lambda t: (0, 0))]
    args += [mods, w]
    return pl.pallas_call(
        functools.partial(_oproj_kernel, gated=gate is not None),
        grid=(N_TOK // ROW_TM,),
        in_specs=in_specs,
        out_specs=pl.BlockSpec((ROW_TM, D_MODEL), lambda t: (t, 0)),
        out_shape=jax.ShapeDtypeStruct((N_TOK, D_MODEL), F32),
        scratch_shapes=[pltpu.VMEM((k, D_MODEL), BF16)],
        compiler_params=_params(("arbitrary",)),
        name=name,
    )(*args)


FFN_MM_ROWS = 512
FFN_ROWS = 64
FFN_PAD = 8


def _ffn_kernel(x_ref, g_ref, mod_ref, wg_ref, wv_ref, cwg_ref, cwv_ref, cbg_ref, cbv_ref,
                wd_ref, o_ref, h_ref, u_ref, act_ref, wup_ref, wdn_ref, *, tile0):
    i = tile0 + pl.program_id(0)
    c = pl.program_id(1)
    fc = FF_CHUNK
    n = TOK_TILE // FFN_MM_ROWS
    seq_len = jnp.where(i < N_PROMPT // TOK_TILE, SEQ, DEC_SEQ)
    row = lax.broadcasted_iota(jnp.int32, (FFN_ROWS, 1), 0)
    taps = lambda cw_ref, cb_ref: [jnp.broadcast_to(cw_ref[j:j + 1, :], (FFN_ROWS, fc))
                                   for j in range(3)] + [
                                       jnp.broadcast_to(cb_ref[...], (FFN_ROWS, fc))]
    taps_g, taps_v = taps(cwg_ref, cbg_ref), taps(cwv_ref, cbv_ref)

    def up(u_ref, t):
        r0 = t * FFN_MM_ROWS
        u_ref[FFN_PAD + r0:FFN_PAD + r0 + FFN_MM_ROWS, :] = _bdot(
            h_ref[r0:r0 + FFN_MM_ROWS, :], wup_ref[...])

    def conv_act(u_ref, t):
        for r0 in range(t * FFN_MM_ROWS, (t + 1) * FFN_MM_ROWS, FFN_ROWS):
            halves = []
            for lo, (w0, w1, w2, bias) in ((0, taps_g), (fc, taps_v)):
                p0 = FFN_PAD + r0
                prev = u_ref[p0 - 1:p0 - 1 + FFN_ROWS, lo:lo + fc]
                mid = u_ref[p0:p0 + FFN_ROWS, lo:lo + fc]
                nxt = u_ref[p0 + 1:p0 + 1 + FFN_ROWS, lo:lo + fc]
                if r0 % SEQ == 0:
                    prev = jnp.where(((r0 + row) & (seq_len - 1)) == 0, 0.0, prev)
                if (r0 + FFN_ROWS) % SEQ == 0:
                    nxt = jnp.where(((r0 + row) & (seq_len - 1)) == seq_len - 1, 0.0, nxt)
                halves.append(prev * w0 + mid * w1 + nxt * w2 + bias)
            act_ref[r0:r0 + FFN_ROWS, :] = (_silu(halves[0]) * halves[1]).astype(BF16)

    def down(t):
        r0 = t * FFN_MM_ROWS
        o_ref[r0:r0 + FFN_MM_ROWS, :] += _bdot(act_ref[r0:r0 + FFN_MM_ROWS, :], wdn_ref[...])

    @pl.when(c == 0)
    def _():
        _norm_mod_rows(x_ref, g_ref, mod_ref, h_ref, 3, 4)
        zeros = jnp.zeros((FFN_PAD, 2 * fc), F32)
        u_ref[0:FFN_PAD, :] = zeros
        u_ref[FFN_PAD + TOK_TILE:, :] = zeros
        o_ref[...] = jnp.zeros_like(o_ref)

    wup_ref[:, :fc] = wg_ref[...].astype(BF16)
    wup_ref[:, fc:] = wv_ref[...].astype(BF16)
    wdn_ref[...] = wd_ref[...].astype(BF16)
    for s in range(n + 2):
        if s < n:
            up(u_ref, s)
        if 1 <= s <= n:
            conv_act(u_ref, s - 1)
        if s >= 2:
            down(s - 2)

    @pl.when(c == N_FF_CHUNKS - 1)
    def _():
        o_ref[...] = x_ref[...] + mod_ref[0, 5:6, :] * o_ref[...]


def _ffn(x, g, mods, layer, w_up, conv_w, conv_b, w_down, tile0=0, n_tiles=N_TOK_TILES):
    fc = FF_CHUNK
    ncb = N_FF_CHUNKS
    return pl.pallas_call(
        functools.partial(_ffn_kernel, tile0=tile0),
        grid=(n_tiles, ncb),
        in_specs=[pl.BlockSpec((TOK_TILE, D_MODEL), lambda i, c: (tile0 + i, 0)),
                  pl.BlockSpec((1, D_MODEL), lambda i, c: (0, 0)),
                  pl.BlockSpec((1, 6, D_MODEL), lambda i, c: (_mod_group_of_tile(tile0 + i), 0, 0)),
                  pl.BlockSpec((None, D_MODEL, fc), lambda i, c: (layer, 0, c)),
                  pl.BlockSpec((None, D_MODEL, fc), lambda i, c: (layer, 0, ncb + c)),
                  pl.BlockSpec((None, 3, fc), lambda i, c: (layer, 0, c)),
                  pl.BlockSpec((None, 3, fc), lambda i, c: (layer, 0, ncb + c)),
                  pl.BlockSpec((None, 1, fc), lambda i, c: (layer, 0, c)),
                  pl.BlockSpec((None, 1, fc), lambda i, c: (layer, 0, ncb + c)),
                  pl.BlockSpec((None, fc, D_MODEL), lambda i, c: (layer, c, 0))],
        out_specs=pl.BlockSpec((TOK_TILE, D_MODEL), lambda i, c: (i, 0)),
        out_shape=jax.ShapeDtypeStruct((n_tiles * TOK_TILE, D_MODEL), F32),
        scratch_shapes=[pltpu.VMEM((TOK_TILE, D_MODEL), BF16),
                        pltpu.VMEM((TOK_TILE + 2 * FFN_PAD, 2 * fc), F32),
                        pltpu.VMEM((TOK_TILE, fc), BF16),
                        pltpu.VMEM((D_MODEL, 2 * fc), BF16),
                        pltpu.VMEM((fc, D_MODEL), BF16)],
        compiler_params=_params(("arbitrary", "arbitrary")),
        name="conv_ffn",
    )(x, g.reshape(1, D_MODEL), mods, w_up, w_up, conv_w, conv_w,
      conv_b.reshape(DEPTH, 1, -1), conv_b.reshape(DEPTH, 1, -1), w_down)


NA_HB = 4
NA_LANES = NA_HB * NA_HD
NA_ROWS = DEC_SEQ // GRID_W
NA_KEYS = NA_WIN_R * GRID_W
PROMPT_SEQS = 4
NA_ROW_UNROLL = 8


def _store_heads(dst_ref, seq, x, n_heads, hd):
    for h in range(n_heads):
        dst_ref[seq, h] = x[:, h * hd:(h + 1) * hd]


def _na_prompt_kernel(q_ref, k_ref, v_ref, gq_ref, gk_ref, o_ref, kn_ref, vn_ref):
    scale = NA_HD ** -0.5 * LOG2E

    def chains(bound):
        shift = None if bound is None else _stacked_rows(bound, SEQ)
        for seq in range(PROMPT_SEQS):
            sl = slice(seq * SEQ, (seq + 1) * SEQ)
            q = _group_rms(q_ref[sl, :], gq_ref[...], NA_HD, sums_on_mxu=True) * scale
            k = _group_rms(k_ref[sl, :], gk_ref[...], NA_HD, sums_on_mxu=True)
            v = v_ref[sl, :]
            _store_heads(kn_ref, seq, k, NA_HB, NA_HD)
            _store_heads(vn_ref, seq, v, NA_HB, NA_HD)
            q4 = _stack_heads(q, NA_HB, NA_HD).astype(BF16)
            s = lax.dot_general(q4, k.astype(BF16), _NT, preferred_element_type=F32)
            (p,), inv = _softmax_parts([s], shift)
            o4 = _bdot(p.astype(BF16), v.astype(BF16)) * inv
            o_ref[sl, :] = _unstack_heads(o4, NA_HB, NA_HD).astype(BF16)

    _with_score_bound(scale * _rms_norm_bound(gq_ref, NA_HD, NA_HD)
                      * _rms_norm_bound(gk_ref, NA_HD, NA_HD), chains)


def _na_prompt(qkv, gq, gk):
    nb = NA_HEADS // NA_HB
    rows = PROMPT_SEQS * SEQ
    blk = lambda off: pl.BlockSpec((rows, NA_LANES), lambda b, j: (b, off + j))
    vec = pl.BlockSpec((1, NA_LANES), lambda b, j: (0, 0))
    cache = pl.BlockSpec((PROMPT_SEQS, NA_HB, SEQ, NA_HD), lambda b, j: (b, j, 0, 0))
    cache_shape = jax.ShapeDtypeStruct((BATCH, NA_HEADS, SEQ, NA_HD), F32)
    return pl.pallas_call(
        _na_prompt_kernel,
        grid=(BATCH // PROMPT_SEQS, nb),
        in_specs=[blk(0), blk(nb), blk(2 * nb), vec, vec],
        out_specs=[blk(0), cache, cache],
        out_shape=[jax.ShapeDtypeStruct((N_PROMPT, D_MODEL), BF16), cache_shape, cache_shape],
        compiler_params=_params(("arbitrary", "arbitrary")),
        name="na_prompt",
    )(qkv, qkv, qkv, jnp.tile(gq, NA_HB).reshape(1, -1), jnp.tile(gk, NA_HB).reshape(1, -1))


def _na_latent_kernel(q_ref, k_ref, v_ref, kc_ref, vc_ref, t_ref, gq_ref, gk_ref, o_ref,
                      qn_ref, kn_ref, vb_ref, kc4_ref, vc4_ref, bias_ref):
    scale = NA_HD ** -0.5 * LOG2E
    rows = 256

    def prep(r, carry):
        sl = pl.ds(pl.multiple_of(r * rows, rows), rows)
        qn_ref[sl, :] = (_group_rms(q_ref[sl, :], gq_ref[...], NA_HD, sums_on_mxu=True)
                         * scale).astype(BF16)
        kn_ref[sl, :] = _group_rms(k_ref[sl, :], gk_ref[...], NA_HD,
                                   sums_on_mxu=True).astype(BF16)
        vb_ref[sl, :] = v_ref[sl, :].astype(BF16)
        return carry

    lax.fori_loop(0, DEC_SEQ // rows, prep, 0, unroll=2)
    kc = kc_ref[0]
    kc4_ref[...] = kc.astype(BF16)
    vc4_ref[...] = vc_ref[0].astype(BF16)

    def attend(bound):
        for h in range(NA_HB):
            off = 0.0 if bound is None else bound[:, h:h + 1]
            for p in range(NA_WIN_R):
                for i in range(NA_WIN_R):
                    bias_ref[h, p, :, i * GRID_W:(i + 1) * GRID_W] = t_ref[h, p + i] - off
        shift = None if bound is None else _stacked_rows(bound, GRID_W)

        def row(r):
            kr0 = jnp.clip(r - NA_WIN_R // 2, 0, NA_ROWS - NA_WIN_R)
            pat = kr0 - r + NA_WIN_R - 1
            qs = pl.ds(pl.multiple_of(r * GRID_W, GRID_W), GRID_W)
            ks = pl.ds(pl.multiple_of(kr0 * GRID_W, GRID_W), NA_KEYS)
            q4 = _stack_heads(qn_ref[qs, :], NA_HB, NA_HD)
            s_loc = lax.dot_general(q4, kn_ref[ks, :], _NT, preferred_element_type=F32)
            s_loc = s_loc + jnp.concatenate([bias_ref[h, pat] for h in range(NA_HB)], axis=0)
            s_ctx = lax.dot_general(q4, kc4_ref[...], _NT, preferred_element_type=F32)
            if shift is None:
                (p_loc, p_ctx), inv = _softmax_parts([s_loc, s_ctx])
            else:
                p_loc, p_ctx = jnp.exp2(s_loc), jnp.exp2(s_ctx - shift)
                inv = 1.0 / (p_loc.sum(axis=-1, keepdims=True) + p_ctx.sum(axis=-1, keepdims=True))
            o4 = _bdot(p_loc.astype(BF16), vb_ref[ks, :]) + _bdot(p_ctx.astype(BF16), vc4_ref[...])
            o_ref[qs, :] = _unstack_heads(o4 * inv, NA_HB, NA_HD).astype(BF16)

        def rows_step(i, carry):
            for u in range(NA_ROW_UNROLL):
                row(i * NA_ROW_UNROLL + u)
            return carry

        lax.fori_loop(0, NA_ROWS // NA_ROW_UNROLL, rows_step, 0)

    qmax = scale * _rms_norm_bound(gq_ref, NA_HD, NA_HD)
    bias_max = jnp.concatenate(
        [t_ref[h].max(axis=0).max(axis=0, keepdims=True).max(axis=1, keepdims=True)
         for h in range(NA_HB)], axis=-1)
    _with_score_bound(
        jnp.maximum(qmax * _rms_norm_bound(gk_ref, NA_HD, NA_HD) + bias_max,
                    qmax * _max_head_norms(kc, NA_HD)), attend)


def _na_bias_blocks(bias_table):
    qc = jnp.arange(GRID_W)[:, None]
    kc = jnp.arange(GRID_W)[None, :]
    win0 = jnp.clip(qc - NA_WIN_C // 2, 0, GRID_W - NA_WIN_C)
    valid = (kc >= win0) & (kc < win0 + NA_WIN_C)
    n_co = bias_table.shape[-1]
    onehot = (kc - qc + NA_WIN_C - 1)[None] == jnp.arange(n_co)[:, None, None]
    t = jnp.einsum('hrd,dqk->hrqk', bias_table.astype(F32), onehot.astype(F32),
                   precision=lax.Precision.HIGHEST)
    return jnp.where(valid, t * LOG2E, NEG)


def _na_latent(qkv, cache_k, cache_v, bias_blocks, gq, gk):
    nb = NA_HEADS // NA_HB
    lat0 = N_PROMPT // DEC_SEQ
    blk = lambda off: pl.BlockSpec((DEC_SEQ, NA_LANES), lambda b, j: (lat0 + b, off + j))
    vec = pl.BlockSpec((1, NA_LANES), lambda b, j: (0, 0))
    cache = pl.BlockSpec((1, PAST_LEN, NA_LANES), lambda b, j: (b, 0, j))
    return pl.pallas_call(
        _na_latent_kernel,
        grid=(DEC_BATCH, nb),
        in_specs=[blk(0), blk(nb), blk(2 * nb), cache, cache,
                  pl.BlockSpec((NA_HB, 2 * NA_WIN_R - 1, GRID_W, GRID_W), lambda b, j: (j, 0, 0, 0)),
                  vec, vec],
        out_specs=pl.BlockSpec((DEC_SEQ, NA_LANES), lambda b, j: (b, j)),
        out_shape=jax.ShapeDtypeStruct((N_LATENT, D_MODEL), BF16),
        scratch_shapes=[pltpu.VMEM((DEC_SEQ, NA_LANES), BF16),
                        pltpu.VMEM((DEC_SEQ, NA_LANES), BF16),
                        pltpu.VMEM((DEC_SEQ, NA_LANES), BF16),
                        pltpu.VMEM((PAST_LEN, NA_LANES), BF16),
                        pltpu.VMEM((PAST_LEN, NA_LANES), BF16),
                        pltpu.VMEM((NA_HB, NA_WIN_R, GRID_W, NA_KEYS), F32)],
        compiler_params=_params(("arbitrary", "arbitrary")),
        name="na_latent",
    )(qkv, qkv, qkv, _tokens_first(cache_k), _tokens_first(cache_v), bias_blocks,
      jnp.tile(gq, NA_HB).reshape(1, -1), jnp.tile(gk, NA_HB).reshape(1, -1))


GLA_C = 128
GLA_SUB = 8
GLA_LEVELS = (64, 32, 16, 8)


def _split_hi_lo(x):
    hi = x.astype(BF16)
    lo = (x - hi.astype(F32)).astype(BF16)
    return jnp.concatenate([hi, lo], axis=1)


class _GlaMasks(NamedTuple):
    tri: jax.Array
    later: tuple
    sign: tuple
    pair: tuple
    diag: jax.Array


def _gla_masks(rev):
    c = GLA_C
    row = lax.broadcasted_iota(jnp.int32, (c, c), 0)
    col = lax.broadcasted_iota(jnp.int32, (c, c), 1)
    rid = lax.broadcasted_iota(jnp.int32, (c, GLA_DK), 0)
    causal = (col >= row) if rev else (col <= row)
    later, pair = [], []
    for m in GLA_LEVELS:
        later.append(((rid & m) == 0) if rev else ((rid & m) != 0))
        same = (row >> _log2(2 * m)) == (col >> _log2(2 * m))
        crossing = ((row & m) != (col & m))
        pair.append(same & crossing & causal)
    diag = ((row >> _log2(GLA_SUB)) == (col >> _log2(GLA_SUB))) & causal
    sign = tuple(jnp.where(l, 1.0, -1.0) for l in later)
    return _GlaMasks(jnp.where(causal, 1.0, 0.0).astype(BF16), tuple(later), sign, tuple(pair), diag)


def _gla_chunk(q, k, v, g, st_ref, rev, masks):
    c = GLA_C
    cs = _bdot(masks.tri, _split_hi_lo(g))
    b = cs[:, :GLA_DK] + cs[:, GLA_DK:]

    a = None
    for m, later, sign, pair in zip(GLA_LEVELS, masks.later, masks.sign, masks.pair):
        nblk = c // (2 * m)
        if rev:
            bnd = [b[j * 2 * m + m:j * 2 * m + m + 1] for j in range(nblk)]
        else:
            bnd = [b[j * 2 * m + m - 1:j * 2 * m + m] for j in range(nblk)]
        ref = jnp.concatenate([jnp.broadcast_to(x, (2 * m, GLA_DK)) for x in bnd], axis=0)
        x = (jnp.where(later, q, k) * jnp.exp((b - ref) * sign)).astype(BF16)
        blk = lax.dot_general(x, x, _NT, preferred_element_type=F32)
        a = jnp.where(pair, blk, 0.0 if a is None else a)

    nsub = c // GLA_SUB
    lane_c = lax.broadcasted_iota(jnp.int32, (GLA_SUB, c), 1)
    diag_rows = []
    for blk_i in range(nsub):
        r0 = blk_i * GLA_SUB
        qb = q[r0:r0 + GLA_SUB]
        bb = b[r0:r0 + GLA_SUB]
        acc = jnp.zeros((GLA_SUB, c), F32)
        for s in range(GLA_SUB):
            ks = k[r0 + s:r0 + s + 1]
            bs = b[r0 + s:r0 + s + 1]
            w = jnp.sum(qb * ks * jnp.exp(jnp.minimum(bb - bs, 0.0)), axis=-1, keepdims=True)
            acc = jnp.where(lane_c == r0 + s, w, acc)
        diag_rows.append(acc)
    a = jnp.where(masks.diag, jnp.concatenate(diag_rows, axis=0), a)

    st = st_ref[...]
    inter = lax.dot_general((q * jnp.exp(b)).astype(BF16), st.astype(BF16), _NT,
                            preferred_element_type=F32)
    o = inter + _bdot(a.astype(BF16), v.astype(BF16))

    btot = b[0:1] if rev else b[c - 1:c]
    kd = (k * jnp.exp(btot - b)).astype(BF16)
    st_ref[...] = st * jnp.exp(btot) + lax.dot_general(v.astype(BF16), kd, _TN,
                                                       preferred_element_type=F32)
    return o


def _gla_kernel(*refs, n_tok, has_state, hps):
    if has_state:
        (q_ref, k_ref, v_ref, r_ref, w2_ref, bg_ref, gn_ref, s0f_ref, s0b_ref,
         o_ref, lg_ref, of_ref, ob_ref, stf_ref, stb_ref) = refs
    else:
        (q_ref, k_ref, v_ref, r_ref, w2_ref, bg_ref, gn_ref,
         o_ref, sf_ref, sb_ref, lg_ref, of_ref, ob_ref, stf_ref, stb_ref) = refs
    nc = n_tok // GLA_C
    scale = GLA_DK ** -0.5

    rb = r_ref[...].astype(BF16)
    for z in range(2):
        x = _bdot(rb, w2_ref[z].astype(BF16)) + bg_ref[z]
        lg_ref[z] = (jnp.minimum(x, 0.0) - jnp.log1p(jnp.exp(-jnp.abs(x)))) * (1.0 / GLA_GATE_NORM)

    for hh in range(hps):
        if has_state:
            stf_ref[hh] = s0f_ref[0, hh].T
            stb_ref[hh] = s0b_ref[0, hh].T
        else:
            stf_ref[hh] = jnp.zeros((GLA_DV, GLA_DK), F32)
            stb_ref[hh] = jnp.zeros((GLA_DV, GLA_DK), F32)

    masks = {rev: _gla_masks(rev) for rev in (False, True)}

    def step(ci, carry):
        for hh in range(hps):
            kq = slice(hh * GLA_DK, (hh + 1) * GLA_DK)
            vv = slice(hh * GLA_DV, (hh + 1) * GLA_DV)
            for rev in (False, True):
                cc = (nc - 1 - ci) if rev else ci
                sl = pl.ds(pl.multiple_of(cc * GLA_C, GLA_C), GLA_C)
                o = _gla_chunk(q_ref[sl, kq] * scale, k_ref[sl, kq], v_ref[sl, vv],
                               lg_ref[1 if rev else 0, sl, kq],
                               (stb_ref if rev else stf_ref).at[hh], rev, masks[rev])
                (ob_ref if rev else of_ref)[sl, vv] = o
        return carry

    lax.fori_loop(0, nc, step, 0)

    for hh in range(hps):
        vv = slice(hh * GLA_DV, (hh + 1) * GLA_DV)
        o = of_ref[:, vv] + ob_ref[:, vv]
        o_ref[:, vv] = o * lax.rsqrt(jnp.mean(o * o, axis=-1, keepdims=True) + EPS) * gn_ref[...]
        if not has_state:
            sf_ref[0, hh] = stf_ref[hh].T
            sb_ref[0, hh] = stb_ref[hh].T


def _gla(proj, w2, bg, gnorm, n_seq, n_tok, row_block0, hps, states=None):
    kw, vw = hps * GLA_DK, hps * GLA_DV
    spec = lambda width, off: pl.BlockSpec((n_tok, width), lambda b, h: (row_block0 + b, off + h))
    in_specs = [spec(kw, 0), spec(kw, GLA_HK // kw), spec(vw, 2 * GLA_HK // vw),
                pl.BlockSpec((n_tok, 128), lambda b, h: (row_block0 + b, (2 * GLA_HK + 2 * GLA_HV) // 128)),
                pl.BlockSpec((2, 128, kw), lambda b, h: (0, 0, h)),
                pl.BlockSpec((2, 1, kw), lambda b, h: (0, 0, h)),
                pl.BlockSpec((1, GLA_DV), lambda b, h: (0, 0))]
    args = [proj, proj, proj, proj, w2, bg, gnorm.reshape(1, GLA_DV)]
    st_spec = pl.BlockSpec((1, hps, GLA_DK, GLA_DV), lambda b, h: (b, h, 0, 0))
    o_spec = pl.BlockSpec((n_tok, vw), lambda b, h: (b, h))
    o_shape = jax.ShapeDtypeStruct((n_seq * n_tok, GLA_HV), F32)
    if states is not None:
        in_specs += [st_spec, st_spec]
        args += list(states)
        out_specs, out_shape = o_spec, o_shape
    else:
        st_shape = jax.ShapeDtypeStruct((n_seq, GLA_HEADS, GLA_DK, GLA_DV), F32)
        out_specs, out_shape = [o_spec, st_spec, st_spec], [o_shape, st_shape, st_shape]
    return pl.pallas_call(
        functools.partial(_gla_kernel, n_tok=n_tok, has_state=states is not None, hps=hps),
        grid=(n_seq, GLA_HEADS // hps),
        in_specs=in_specs,
        out_specs=out_specs,
        out_shape=out_shape,
        scratch_shapes=[pltpu.VMEM((2, n_tok, kw), F32),
                        pltpu.VMEM((n_tok, vw), F32),
                        pltpu.VMEM((n_tok, vw), F32),
                        pltpu.VMEM((hps, GLA_DV, GLA_DK), F32),
                        pltpu.VMEM((hps, GLA_DV, GLA_DK), F32)],
        compiler_params=_params(("arbitrary", "arbitrary")),
        name="gla_latent" if states is not None else "gla_prompt",
    )(*args)


DIFF_HB = 2
DIFF_QL = DIFF_HB * DIFF_HD
DIFF_VL = DIFF_HB * 2 * DIFF_HD
DIFF_TQ = 512
DIFF_SUB = 128


def _diff_lambda(lam_ref):
    l = lam_ref[...]
    a = jnp.sum(l[0:1] * l[1:2], axis=-1, keepdims=True)
    b = jnp.sum(l[2:3] * l[3:4], axis=-1, keepdims=True)
    return jnp.exp(a) - jnp.exp(b) + DIFF_LAMBDA_INIT


def _diff_finish(ps, invs, lam, v, sn_ref):
    a = ps[0] * invs[0] - (lam * invs[1]) * ps[1]
    o2 = _bdot(a.astype(BF16), v)
    o = _unstack_heads(o2, DIFF_HB, 2 * DIFF_HD)
    return _group_rms(o, sn_ref[...], 2 * DIFF_HD) * (1.0 - DIFF_LAMBDA_INIT)


def _diff_prompt_kernel(q0_ref, q1_ref, k0_ref, k1_ref, v_ref, gq_ref, gk_ref, lam_ref, sn_ref,
                        o_ref, kn_ref, vn_ref):
    scale = DIFF_HD ** -0.5 * LOG2E
    lam = _diff_lambda(lam_ref)

    def chains(bound):
        shift = None if bound is None else _stacked_rows(bound, SEQ)
        for seq in range(PROMPT_SEQS):
            sl = slice(seq * SEQ, (seq + 1) * SEQ)
            ps, invs = [], []
            for comp, (q_ref, k_ref) in enumerate(((q0_ref, k0_ref), (q1_ref, k1_ref))):
                q = _group_rms(q_ref[sl, :], gq_ref[...], DIFF_HD, sums_on_mxu=True) * scale
                k = _group_rms(k_ref[sl, :], gk_ref[...], DIFF_HD, sums_on_mxu=True)
                for h in range(DIFF_HB):
                    kn_ref[seq, comp, h] = k[:, h * DIFF_HD:(h + 1) * DIFF_HD]
                q2 = _stack_heads(q, DIFF_HB, DIFF_HD).astype(BF16)
                s = lax.dot_general(q2, k.astype(BF16), _NT, preferred_element_type=F32)
                (p,), inv = _softmax_parts([s], shift)
                ps.append(p)
                invs.append(inv)
            v = v_ref[sl, :]
            _store_heads(vn_ref, seq, v, DIFF_HB, 2 * DIFF_HD)
            o_ref[sl, :] = _diff_finish(ps, invs, lam, v.astype(BF16), sn_ref).astype(BF16)

    _with_score_bound(scale * _rms_norm_bound(gq_ref, DIFF_HD, DIFF_HD)
                      * _rms_norm_bound(gk_ref, DIFF_HD, DIFF_HD), chains)


def _diff_prompt(qkv, gq, gk, lam, sub_norm):
    nb = DIFF_HEADS // DIFF_HB
    rows = PROMPT_SEQS * SEQ
    qk = lambda off: pl.BlockSpec((rows, DIFF_QL), lambda b, j: (b, off + j))
    vec = lambda n: pl.BlockSpec((1, n), lambda b, j: (0, 0))
    v_spec = pl.BlockSpec((rows, DIFF_VL), lambda b, j: (b, 2 * D_MODEL // DIFF_VL + j))
    kn_spec = pl.BlockSpec((PROMPT_SEQS, 2, DIFF_HB, SEQ, DIFF_HD), lambda b, j: (b, 0, j, 0, 0))
    vn_spec = pl.BlockSpec((PROMPT_SEQS, DIFF_HB, SEQ, 2 * DIFF_HD), lambda b, j: (b, j, 0, 0))
    return pl.pallas_call(
        _diff_prompt_kernel,
        grid=(BATCH // PROMPT_SEQS, nb),
        in_specs=[qk(0), qk(nb), qk(2 * nb), qk(3 * nb), v_spec, vec(DIFF_QL), vec(DIFF_QL),
                  pl.BlockSpec((4, DIFF_HD), lambda b, j: (0, 0)), vec(DIFF_VL)],
        out_specs=[pl.BlockSpec((rows, DIFF_VL), lambda b, j: (b, j)), kn_spec, vn_spec],
        out_shape=[jax.ShapeDtypeStruct((N_PROMPT, D_MODEL), BF16),
                   jax.ShapeDtypeStruct((BATCH, 2, DIFF_HEADS, SEQ, DIFF_HD), F32),
                   jax.ShapeDtypeStruct((BATCH, DIFF_HEADS, SEQ, 2 * DIFF_HD), F32)],
        compiler_params=_params(("arbitrary", "arbitrary")),
        name="diff_prompt",
    )(qkv, qkv, qkv, qkv, qkv, jnp.tile(gq, DIFF_HB).reshape(1, -1),
      jnp.tile(gk, DIFF_HB).reshape(1, -1), lam, jnp.tile(sub_norm, DIFF_HB).reshape(1, -1))


def _diff_latent_kernel(q0_ref, q1_ref, k0_ref, k1_ref, v_ref, kc0_ref, kc1_ref, vc_ref,
                        cos_ref, sin_ref, cosq_ref, sinq_ref, gq_ref, gk_ref, lam_ref, sn_ref,
                        o_ref, kb0_ref, kb1_ref, vb_ref, kmax_ref):
    scale = DIFF_HD ** -0.5 * LOG2E
    half = DIFF_HD // 4
    rows = 256

    @pl.when(pl.program_id(2) == 0)
    def _():
        for comp, (k_ref, kc_ref, kb_ref) in enumerate(
                ((k0_ref, kc0_ref, kb0_ref), (k1_ref, kc1_ref, kb1_ref))):
            kc = kc_ref[0]
            kb_ref[0:PAST_LEN, :] = kc.astype(BF16)
            kmax_ref[comp:comp + 1, :] = jnp.maximum(
                _max_head_norms(kc, DIFF_HD), _rms_norm_bound(gk_ref, DIFF_HD, DIFF_HD))

            def prep(r, carry):
                sl = pl.ds(pl.multiple_of(r * rows, rows), rows)
                k = _group_rms(k_ref[sl, :], gk_ref[...], DIFF_HD)
                k = _rope(k, cos_ref[sl, :], sin_ref[sl, :], half)
                kb_ref[pl.ds(pl.multiple_of(PAST_LEN + r * rows, rows), rows), :] = k.astype(BF16)
                return carry

            lax.fori_loop(0, DEC_SEQ // rows, prep, 0, unroll=2)
        vb_ref[0:PAST_LEN, :] = vc_ref[0].astype(BF16)
        vb_ref[PAST_LEN:, :] = v_ref[...].astype(BF16)

    lam = _diff_lambda(lam_ref)

    def chains(bound):
        for r0 in range(0, DIFF_TQ, DIFF_SUB):
            sl = slice(r0, r0 + DIFF_SUB)
            ps, invs = [], []
            for comp, (q_ref, kb_ref) in enumerate(((q0_ref, kb0_ref), (q1_ref, kb1_ref))):
                shift = None if bound is None else _stacked_rows(
                    bound[:, comp * DIFF_HB:(comp + 1) * DIFF_HB], DIFF_SUB)
                q = _group_rms(q_ref[sl, :], gq_ref[...], DIFF_HD)
                q = _rope(q, cosq_ref[sl, :], sinq_ref[sl, :], half) * scale
                q2 = _stack_heads(q, DIFF_HB, DIFF_HD).astype(BF16)
                s = lax.dot_general(q2, kb_ref[...], _NT, preferred_element_type=F32)
                (p,), inv = _softmax_parts([s], shift)
                ps.append(p)
                invs.append(inv)
            o_ref[sl, :] = _diff_finish(ps, invs, lam, vb_ref[...], sn_ref).astype(BF16)

    qmax = scale * _rms_norm_bound(gq_ref, DIFF_HD, DIFF_HD)
    _with_score_bound(jnp.concatenate([qmax * kmax_ref[0:1, :], qmax * kmax_ref[1:2, :]], axis=-1),
                      chains)


def _diff_latent(qkv, cache_k, cache_v, cos, sin, gq, gk, lam, sub_norm):
    nb = DIFF_HEADS // DIFF_HB
    nq = DEC_SEQ // DIFF_TQ
    q0 = N_PROMPT // DIFF_TQ
    lat0 = N_PROMPT // DEC_SEQ
    n_keys = PAST_LEN + DEC_SEQ
    q_spec = lambda off: pl.BlockSpec((DIFF_TQ, DIFF_QL), lambda b, j, t: (q0 + b * nq + t, off + j))
    k_spec = lambda off: pl.BlockSpec((DEC_SEQ, DIFF_QL), lambda b, j, t: (lat0 + b, off + j))
    v_spec = pl.BlockSpec((DEC_SEQ, DIFF_VL), lambda b, j, t: (lat0 + b, 2 * D_MODEL // DIFF_VL + j))
    kc_spec = lambda off: pl.BlockSpec((1, PAST_LEN, DIFF_QL), lambda b, j, t: (b, 0, off + j))
    vc_spec = pl.BlockSpec((1, PAST_LEN, DIFF_VL), lambda b, j, t: (b, 0, j))
    tab = pl.BlockSpec((DEC_SEQ, DIFF_QL), lambda b, j, t: (0, 0))
    tabq = pl.BlockSpec((DIFF_TQ, DIFF_QL), lambda b, j, t: (t, 0))
    vec = lambda n: pl.BlockSpec((1, n), lambda b, j, t: (0, 0))
    return pl.pallas_call(
        _diff_latent_kernel,
        grid=(DEC_BATCH, nb, nq),
        in_specs=[q_spec(0), q_spec(nb), k_spec(2 * nb), k_spec(3 * nb), v_spec,
                  kc_spec(0), kc_spec(nb), vc_spec, tab, tab, tabq, tabq,
                  vec(DIFF_QL), vec(DIFF_QL),
                  pl.BlockSpec((4, DIFF_HD), lambda b, j, t: (0, 0)), vec(DIFF_VL)],
        out_specs=pl.BlockSpec((DIFF_TQ, DIFF_VL), lambda b, j, t: (b * nq + t, j)),
        out_shape=jax.ShapeDtypeStruct((N_LATENT, D_MODEL), BF16),
        scratch_shapes=[pltpu.VMEM((n_keys, DIFF_QL), BF16),
                        pltpu.VMEM((n_keys, DIFF_QL), BF16),
                        pltpu.VMEM((n_keys, DIFF_VL), BF16),
                        pltpu.VMEM((2, DIFF_HB), F32)],
        compiler_params=_params(("arbitrary", "arbitrary", "arbitrary")),
        name="diff_latent",
    )(qkv, qkv, qkv, qkv, qkv, _tokens_first(cache_k), _tokens_first(cache_k),
      _tokens_first(cache_v), cos, sin, cos, sin,
      jnp.tile(gq, DIFF_HB).reshape(1, -1), jnp.tile(gk, DIFF_HB).reshape(1, -1), lam,
      jnp.tile(sub_norm, DIFF_HB).reshape(1, -1))


MLA_HB = 2
MLA_HL = 128
MLA_LANES = MLA_HB * MLA_HL
MLA_TQ = 1024
MLA_SUB = 128


def _mla_keys(kv, kr, gk, sums_on_mxu=False):
    lane = lax.broadcasted_iota(jnp.int32, kv.shape, 1)
    kr2 = jnp.concatenate([kr] * MLA_HB, axis=1)
    k = jnp.where((lane & (MLA_HL - 1)) < MLA_NOPE, kv, kr2)
    return _group_rms(k, gk, MLA_HL, n_real=MLA_QK, sums_on_mxu=sums_on_mxu)


def _mla_out(o2):
    tq = o2.shape[0] // MLA_HB
    oa = pltpu.roll(o2[0:tq, 0:MLA_HL], MLA_HL - MLA_V, axis=1)
    ob = o2[tq:, MLA_HL:]
    lane = lax.broadcasted_iota(jnp.int32, oa.shape, 1)
    return jnp.where(lane < MLA_V, oa, ob)


def _mla_prompt_kernel(q_ref, kv_ref, kr_ref, gq_ref, gk_ref, o_ref):
    scale = MLA_QK ** -0.5 * LOG2E

    def chains(bound):
        shift = None if bound is None else _stacked_rows(bound, SEQ)
        for seq in range(PROMPT_SEQS):
            sl = slice(seq * SEQ, (seq + 1) * SEQ)
            q = _group_rms(q_ref[sl, :], gq_ref[...], MLA_HL, n_real=MLA_QK) * scale
            kv = kv_ref[sl, :]
            k = _mla_keys(kv, kr_ref[sl, :], gk_ref[...])
            q2 = _stack_heads(q, MLA_HB, MLA_HL).astype(BF16)
            s = lax.dot_general(q2, k.astype(BF16), _NT, preferred_element_type=F32)
            (p,), inv = _softmax_parts([s], shift)
            o_ref[sl, :] = _mla_out(_bdot(p.astype(BF16), kv.astype(BF16)) * inv).astype(BF16)

    _with_score_bound(scale * _rms_norm_bound(gq_ref, MLA_QK, MLA_HL)
                      * _rms_norm_bound(gk_ref, MLA_QK, MLA_HL), chains)


def _mla_prompt(qp, kvp, low, gq, gk):
    nb = MLA_HEADS // MLA_HB
    rows = PROMPT_SEQS * SEQ
    blk = pl.BlockSpec((rows, MLA_LANES), lambda b, j: (b, j))
    vec = pl.BlockSpec((1, MLA_LANES), lambda b, j: (0, 0))
    return pl.pallas_call(
        _mla_prompt_kernel,
        grid=(BATCH // PROMPT_SEQS, nb),
        in_specs=[blk, blk, pl.BlockSpec((rows, MLA_HL), lambda b, j: (b, 0)),
                  vec, vec],
        out_specs=pl.BlockSpec((rows, MLA_HB * MLA_V), lambda b, j: (b, j)),
        out_shape=jax.ShapeDtypeStruct((N_PROMPT, MLA_HEADS * MLA_V), BF16),
        compiler_params=_params(("arbitrary", "arbitrary")),
        name="mla_prompt",
    )(qp, kvp, low, gq, gk)


def _mla_latent_kernel(q_ref, kv_ref, kr_ref, kvc_ref, krc_ref, cos_ref, sin_ref, cosq_ref,
                       sinq_ref, gq_ref, gk_ref, o_ref, kb_ref, vb_ref, kmax_ref):
    scale = MLA_QK ** -0.5 * LOG2E
    half = MLA_ROPE // 4
    rows = 256

    @pl.when(pl.program_id(2) == 0)
    def _():
        kvc = kvc_ref[...]
        kc = _mla_keys(kvc, krc_ref[...], gk_ref[...])
        kb_ref[0:PAST_LEN, :] = kc.astype(BF16)
        vb_ref[0:PAST_LEN, :] = kvc.astype(BF16)

        def prep(r, carry):
            sl = pl.ds(pl.multiple_of(r * rows, rows), rows)
            dst = pl.ds(pl.multiple_of(PAST_LEN + r * rows, rows), rows)
            kv = kv_ref[sl, :]
            k = _mla_keys(kv, kr_ref[sl, :], gk_ref[...], sums_on_mxu=True)
            k = _rope(k, cos_ref[sl, :], sin_ref[sl, :], half)
            kb_ref[dst, :] = k.astype(BF16)
            vb_ref[dst, :] = kv.astype(BF16)
            return carry

        lax.fori_loop(0, DEC_SEQ // rows, prep, 0, unroll=2)
        kmax_ref[...] = jnp.maximum(_max_head_norms(kc, MLA_HL),
                                    _rms_norm_bound(gk_ref, MLA_QK, MLA_HL))

    def chains(bound):
        shift = None if bound is None else _stacked_rows(bound, MLA_SUB)
        for r0 in range(0, MLA_TQ, MLA_SUB):
            sl = slice(r0, r0 + MLA_SUB)
            q = _group_rms(q_ref[sl, :], gq_ref[...], MLA_HL, n_real=MLA_QK)
            q = _rope(q, cosq_ref[sl, :], sinq_ref[sl, :], half) * scale
            q2 = _stack_heads(q, MLA_HB, MLA_HL).astype(BF16)
            s = lax.dot_general(q2, kb_ref[...], _NT, preferred_element_type=F32)
            (p,), inv = _softmax_parts([s], shift)
            o_ref[sl, :] = _mla_out(_bdot(p.astype(BF16), vb_ref[...]) * inv).astype(BF16)

    _with_score_bound(scale * _rms_norm_bound(gq_ref, MLA_QK, MLA_HL) * kmax_ref[...], chains)


def _mla_latent(qp, kvp, low, kvc, krc, cos, sin, gq, gk):
    nb = MLA_HEADS // MLA_HB
    nq = DEC_SEQ // MLA_TQ
    q0 = N_PROMPT // MLA_TQ
    lat0 = N_PROMPT // DEC_SEQ
    n_keys = PAST_LEN + DEC_SEQ
    tab = pl.BlockSpec((DEC_SEQ, MLA_LANES), lambda b, j, t: (0, 0))
    tabq = pl.BlockSpec((MLA_TQ, MLA_LANES), lambda b, j, t: (t, 0))
    vec = pl.BlockSpec((1, MLA_LANES), lambda b, j, t: (0, 0))
    return pl.pallas_call(
        _mla_latent_kernel,
        grid=(DEC_BATCH, nb, nq),
        in_specs=[pl.BlockSpec((MLA_TQ, MLA_LANES), lambda b, j, t: (q0 + b * nq + t, j)),
                  pl.BlockSpec((DEC_SEQ, MLA_LANES), lambda b, j, t: (lat0 + b, j)),
                  pl.BlockSpec((DEC_SEQ, MLA_HL), lambda b, j, t: (lat0 + b, 0)),
                  pl.BlockSpec((PAST_LEN, MLA_LANES), lambda b, j, t: (b, j)),
                  pl.BlockSpec((PAST_LEN, MLA_HL), lambda b, j, t: (b, 0)),
                  tab, tab, tabq, tabq, vec, vec],
        out_specs=pl.BlockSpec((MLA_TQ, MLA_HB * MLA_V), lambda b, j, t: (b * nq + t, j)),
        out_shape=jax.ShapeDtypeStruct((N_LATENT, MLA_HEADS * MLA_V), BF16),
        scratch_shapes=[pltpu.VMEM((n_keys, MLA_LANES), BF16),
                        pltpu.VMEM((n_keys, MLA_LANES), BF16),
                        pltpu.VMEM((1, MLA_HB), F32)],
        compiler_params=_params(("arbitrary", "arbitrary", "arbitrary")),
        name="mla_latent",
    )(qp, kvp, low, kvc, krc, cos, sin, cos, sin, gq, gk)


MLA_LOW_Q = 0
MLA_LOW_KV = 512
MLA_LOW_KR = 768
MLA_LOW_N = 896


def _axial_tables(n_tok, rdim):
    nf = rdim // 4
    freqs = ROPE_BASE ** (-jnp.arange(nf, dtype=F32) / nf)
    t = jnp.arange(n_tok)
    rowp = (t // GRID_W).astype(F32)
    colp = (t % GRID_W).astype(F32)
    ang = jnp.stack([rowp[:, None] * freqs, colp[:, None] * freqs], axis=1)
    cos, sin = jnp.cos(ang), jnp.sin(ang)
    cos_l = jnp.stack([cos, cos], axis=2).reshape(n_tok, rdim)
    sin_l = jnp.stack([-sin, sin], axis=2).reshape(n_tok, rdim)
    return cos_l, sin_l


def _diff_rope_tables():
    cos, sin = _axial_tables(DEC_SEQ, DIFF_HD)
    return jnp.tile(cos, (1, DIFF_HB)), jnp.tile(sin, (1, DIFF_HB))


def _mla_rope_tables():
    cos, sin = _axial_tables(DEC_SEQ, MLA_ROPE)
    ones = jnp.ones((DEC_SEQ, MLA_NOPE), F32)
    pad1 = jnp.ones((DEC_SEQ, MLA_HL - MLA_QK), F32)
    cos_h = jnp.concatenate([ones, cos, pad1], axis=1)
    sin_h = jnp.concatenate([0 * ones, sin, 0 * pad1], axis=1)
    return jnp.tile(cos_h, (1, MLA_HB)), jnp.tile(sin_h, (1, MLA_HB))


def _tokens_first(cache):
    b, h, l, d = cache.shape
    return jnp.transpose(cache, (0, 2, 1, 3)).reshape(b, l, h * d)


def _pad_heads(w, heads, hd, hl):
    k = w.shape[0]
    return jnp.pad(w.reshape(k, heads, hd), ((0, 0), (0, 0), (0, hl - hd))).reshape(k, heads * hl)


def kernel(x_prompt, x_sample, cache_l0_k, cache_l0_v, state_l1_fwd, state_l1_bwd, cache_l2_k,
           cache_l2_v, cache_l3_ckv, cache_l3_krope, c, c_ctx, ada_w, ada_b, norm_mix, norm_ffn,
           ffn_w_up, ffn_conv_w, ffn_conv_b, ffn_w_down, na_w_qkv, na_q_norm, na_k_norm, na_bias,
           na_w_o, gla_w_qkvg, gla_w_gate1, gla_w_gate2, gla_b_gate, gla_o_norm, gla_w_o,
           diff_w_qkv, diff_q_norm, diff_k_norm, diff_lambda, diff_sub_norm, diff_w_o, mla_w_dq,
           mla_q_a_norm, mla_w_uq, mla_w_dkv, mla_kv_a_norm, mla_w_ukv, mla_q_norm, mla_k_norm,
           mla_w_o):
    xr = _Rows(x_prompt.reshape(N_PROMPT, D_MODEL), x_sample.reshape(N_LATENT, D_MODEL), 0)
    cvecs = jnp.concatenate([c_ctx[None], c, jnp.zeros((5, D_MODEL), F32)], axis=0)
    mods_all = _ada_mods(cvecs, ada_w, ada_b)
    halves = lambda o_p, o_s: _Rows(o_p, o_s, 0)

    mods = mods_all[0]
    qkv = _norm_mod_proj(xr, norm_mix[0], mods, [na_w_qkv], "na_qkv")
    o_p, new_l0_k, new_l0_v = _na_prompt(qkv, na_q_norm, na_k_norm)
    o_s = _na_latent(qkv, cache_l0_k, cache_l0_v, _na_bias_blocks(na_bias), na_q_norm, na_k_norm)
    x = _out_proj_residual(xr, halves(o_p, o_s), mods, na_w_o, "na_out")
    ffn_weights = (ffn_w_up, ffn_conv_w, ffn_conv_b, ffn_w_down)
    x = _ffn(x, norm_ffn[0], mods, 0, *ffn_weights)
    xr = _one_array(x)

    mods = mods_all[1]
    w_decay = jnp.concatenate(
        [gla_w_gate1[0], gla_w_gate1[1],
         jnp.zeros((D_MODEL, 128 - 2 * GLA_GATE_RANK), F32)], axis=1)
    proj = _norm_mod_proj(xr, norm_mix[1], mods, [gla_w_qkvg, w_decay], "gla_proj")
    w2 = jnp.zeros((2, 128, GLA_HK), F32)
    w2 = w2.at[0, :GLA_GATE_RANK].set(gla_w_gate2[0])
    w2 = w2.at[1, GLA_GATE_RANK:2 * GLA_GATE_RANK].set(gla_w_gate2[1])
    bg = gla_b_gate.reshape(2, 1, GLA_HK)
    o_p, new_l1_fwd, new_l1_bwd = _gla(proj, w2, bg, gla_o_norm, BATCH, SEQ, 0, hps=4)
    o_s = _gla(proj, w2, bg, gla_o_norm, DEC_BATCH, DEC_SEQ, N_PROMPT // DEC_SEQ, hps=2,
               states=(state_l1_fwd, state_l1_bwd))
    x = _out_proj_residual(xr, halves(o_p, o_s), mods, gla_w_o, "gla_out",
                           gate=proj, gate_col_block=(2 * GLA_HK + GLA_HV) // GLA_HV)
    x = _ffn(x, norm_ffn[1], mods, 1, *ffn_weights)
    xr = _one_array(x)

    mods = mods_all[2]
    qkv = _norm_mod_proj(xr, norm_mix[2], mods, [diff_w_qkv], "diff_qkv")
    o_p, kn_p, new_l2_v = _diff_prompt(qkv, diff_q_norm, diff_k_norm, diff_lambda, diff_sub_norm)
    new_l2_k = kn_p.reshape(BATCH, 2 * DIFF_HEADS, SEQ, DIFF_HD)
    cos_d, sin_d = _diff_rope_tables()
    o_s = _diff_latent(qkv, cache_l2_k, cache_l2_v, cos_d, sin_d, diff_q_norm, diff_k_norm,
                       diff_lambda, diff_sub_norm)
    x = _out_proj_residual(xr, halves(o_p, o_s), mods, diff_w_o, "diff_out")
    x = _ffn(x, norm_ffn[2], mods, 2, *ffn_weights)
    xr = _one_array(x)

    mods = mods_all[3]
    zc = lambda n: jnp.zeros((D_MODEL, n), F32)
    w_low = jnp.concatenate(
        [mla_w_dq, zc(MLA_LOW_KV - MLA_Q_RANK), mla_w_dkv[:, :MLA_KV_RANK],
         zc(MLA_NOPE), mla_w_dkv[:, MLA_KV_RANK:], zc(MLA_HL - MLA_QK)], axis=1)
    w_uq = _pad_heads(mla_w_uq, MLA_HEADS, MLA_QK, MLA_HL)
    qp, kvp, ckv, kr = _mla_front(xr, norm_mix[3], mods, w_low, mla_q_a_norm, w_uq,
                                  mla_kv_a_norm, mla_w_ukv)
    kvc = _matmul(cache_l3_ckv.reshape(DEC_BATCH * PAST_LEN, MLA_KV_RANK), mla_w_ukv,
                  "mla_ukv_cache")
    krc = jnp.pad(cache_l3_krope.reshape(DEC_BATCH * PAST_LEN, MLA_ROPE),
                  ((0, 0), (MLA_NOPE, MLA_HL - MLA_QK)))
    pad_gain = lambda g: jnp.tile(jnp.pad(g, (0, MLA_HL - MLA_QK)), MLA_HB).reshape(1, -1)
    gq, gk = pad_gain(mla_q_norm), pad_gain(mla_k_norm)
    o_p = _mla_prompt(qp, kvp, kr, gq, gk)
    cos_m, sin_m = _mla_rope_tables()
    o_s = _mla_latent(qp, kvp, kr, kvc, krc, cos_m, sin_m, gq, gk)
    new_l3_ckv = ckv[:N_PROMPT].reshape(BATCH, SEQ, MLA_KV_RANK)
    new_l3_krope = kr[:N_PROMPT, MLA_NOPE:MLA_QK].reshape(BATCH, SEQ, MLA_ROPE)
    x = _out_proj_residual(xr, halves(o_p, o_s), mods, mla_w_o, "mla_out")
    n_pt = N_PROMPT // TOK_TILE
    ffn3 = functools.partial(_ffn, x, norm_ffn[3], mods, 3, *ffn_weights)
    y_prompt = ffn3(tile0=0, n_tiles=n_pt).reshape(BATCH, SEQ, D_MODEL)
    y_sample = ffn3(tile0=n_pt, n_tiles=N_TOK_TILES - n_pt).reshape(DEC_BATCH, DEC_SEQ, D_MODEL)
    return (y_prompt, y_sample, new_l0_k, new_l0_v, new_l1_fwd, new_l1_bwd, new_l2_k, new_l2_v,
            new_l3_ckv, new_l3_krope)
```

```python
import functools
import math
from typing import NamedTuple

import jax
import jax.numpy as jnp
from jax import lax
from jax.experimental import pallas as pl
from jax.experimental.pallas import tpu as pltpu

F32 = jnp.float32
BF16 = jnp.bfloat16

D_MODEL = 1024
BATCH = 16
SEQ = 256
DEPTH = 4
DEC_BATCH = 2
DEC_SEQ = 2048
PAST_LEN = 256
GRID_W = 64
D_FF = 2816
EPS = 1e-6
ROPE_BASE = 10000.0

NA_HEADS = 16
NA_HD = 64
NA_WIN_R = 8
NA_WIN_C = 16

GLA_HEADS = 4
GLA_DK = 128
GLA_DV = 256
GLA_HK = GLA_HEADS * GLA_DK
GLA_HV = GLA_HEADS * GLA_DV
GLA_GATE_RANK = 16
GLA_GATE_NORM = 16.0

DIFF_HEADS = 8
DIFF_HD = 64
DIFF_LAMBDA_INIT = 0.8 - 0.6 * math.exp(-0.3 * 2)

MLA_HEADS = 16
MLA_Q_RANK = 384
MLA_KV_RANK = 256
MLA_NOPE = 64
MLA_ROPE = 32
MLA_V = 64
MLA_QK = MLA_NOPE + MLA_ROPE

N_PROMPT = BATCH * SEQ
N_LATENT = DEC_BATCH * DEC_SEQ
N_TOK = N_PROMPT + N_LATENT
TOK_TILE = 2048
N_TOK_TILES = N_TOK // TOK_TILE
FF_CHUNK = 256
N_FF_CHUNKS = D_FF // FF_CHUNK
NEG = -1e30
LOG2E = math.log2(math.e)
SAFE_SCORE_BOUND = 48.0

VMEM_LIMIT = 56 * 1024 * 1024

_NT = (((1,), (1,)), ((), ()))
_TN = (((0,), (0,)), ((), ()))


def _params(sem, vmem=VMEM_LIMIT):
    return pltpu.CompilerParams(dimension_semantics=sem, vmem_limit_bytes=vmem)


def _log2(n):
    assert n & (n - 1) == 0
    return n.bit_length() - 1


def _silu(x):
    return x / (1.0 + jnp.exp(-x))


def _bdot(a, b):
    return jnp.dot(a, b, preferred_element_type=F32)


def _softmax_parts(parts, shift=None):
    m = shift
    if m is None:
        m = parts[0].max(axis=-1, keepdims=True)
        for s in parts[1:]:
            m = jnp.maximum(m, s.max(axis=-1, keepdims=True))
    ps = [jnp.exp2(s - m) for s in parts]
    l = ps[0].sum(axis=-1, keepdims=True)
    for p in ps[1:]:
        l = l + p.sum(axis=-1, keepdims=True)
    return ps, 1.0 / l


def _group_rms(x, gain, group, n_real=None, sums_on_mxu=False):
    lanes = x.shape[-1]
    n_real = n_real or group
    x2 = x * x
    if group == lanes:
        ms = jnp.sum(x2, axis=-1, keepdims=True)
    elif not sums_on_mxu:
        gid = lax.broadcasted_iota(jnp.int32, x.shape, 1) >> _log2(group)
        ms = jnp.zeros_like(x)
        for i in range(lanes // group):
            sel = gid == i
            si = jnp.sum(jnp.where(sel, x2, 0.0), axis=-1, keepdims=True)
            ms = jnp.where(sel, si, ms)
    else:
        r = lax.broadcasted_iota(jnp.int32, (lanes, lanes), 0) >> _log2(group)
        c = lax.broadcasted_iota(jnp.int32, (lanes, lanes), 1) >> _log2(group)
        ones = jnp.where(r == c, 1.0, 0.0).astype(BF16)
        hi = x2.astype(BF16)
        lo = (x2 - hi.astype(F32)).astype(BF16)
        ms = _bdot(hi, ones) + _bdot(lo, ones)
    return x * lax.rsqrt(ms * (1.0 / n_real) + EPS) * gain


def _with_score_bound(bound, body):
    ok = bound.max() < SAFE_SCORE_BOUND
    pl.when(ok)(lambda: body(bound))
    pl.when(jnp.logical_not(ok))(lambda: body(None))


def _rms_norm_bound(g_ref, n, head_lanes):
    g = jnp.abs(g_ref[...])
    heads = g.shape[-1] // head_lanes
    return n ** 0.5 * jnp.concatenate(
        [g[:, h * head_lanes:(h + 1) * head_lanes].max(axis=-1, keepdims=True)
         for h in range(heads)], axis=-1)


def _stacked_rows(per_head, rows):
    r = lax.broadcasted_iota(jnp.int32, (per_head.shape[-1] * rows, 1), 0)
    out = per_head[:, 0:1]
    for h in range(1, per_head.shape[-1]):
        out = jnp.where(r >= h * rows, per_head[:, h:h + 1], out)
    return out


def _max_head_norms(x, head_lanes):
    hid = lax.broadcasted_iota(jnp.int32, x.shape, 1) >> _log2(head_lanes)
    x2 = x * x
    sq = [jnp.sum(jnp.where(hid == h, x2, 0.0), axis=-1, keepdims=True).max(axis=0, keepdims=True)
          for h in range(x.shape[-1] // head_lanes)]
    return jnp.sqrt(jnp.concatenate(sq, axis=-1))


def _rope(x, cos, sin, half):
    lanes = x.shape[-1]
    lane = lax.broadcasted_iota(jnp.int32, x.shape, 1)
    up = pltpu.roll(x, lanes - half, axis=1)
    dn = pltpu.roll(x, half, axis=1)
    swapped = jnp.where((lane & (2 * half - 1)) < half, up, dn)
    return x * cos + swapped * sin


def _stack_heads(q, n_heads, head_lanes):
    hid = lax.broadcasted_iota(jnp.int32, q.shape, 1) >> _log2(head_lanes)
    zero = jnp.zeros_like(q)
    return jnp.concatenate([jnp.where(hid == i, q, zero) for i in range(n_heads)], axis=0)


def _unstack_heads(o, n_heads, head_lanes):
    rows = o.shape[0] // n_heads
    hid = lax.broadcasted_iota(jnp.int32, (rows, o.shape[1]), 1) >> _log2(head_lanes)
    out = o[0:rows]
    for i in range(1, n_heads):
        out = jnp.where(hid == i, o[i * rows:(i + 1) * rows], out)
    return out


ADA_TN = 1536


def _ada_kernel(c_ref, w_ref, b_ref, o_ref):
    s = _silu(c_ref[...])
    w = w_ref[0]
    s_hi, w_hi = s.astype(BF16), w.astype(BF16)
    s_lo = (s - s_hi.astype(F32)).astype(BF16)
    w_lo = (w - w_hi.astype(F32)).astype(BF16)
    both = _bdot(jnp.concatenate([s_hi, s_lo], axis=0), w_hi)
    o_ref[0] = both[0:8] + both[8:16] + _bdot(s_hi, w_lo) + b_ref[0]


def _ada_mods(cvecs, ada_w, ada_b):
    out = pl.pallas_call(
        _ada_kernel,
        grid=(DEPTH, 6 * D_MODEL // ADA_TN),
        in_specs=[pl.BlockSpec((8, D_MODEL), lambda l, j: (0, 0)),
                  pl.BlockSpec((1, D_MODEL, ADA_TN), lambda l, j: (l, 0, j)),
                  pl.BlockSpec((1, 1, ADA_TN), lambda l, j: (l, 0, j))],
        out_specs=pl.BlockSpec((1, 8, ADA_TN), lambda l, j: (l, 0, j)),
        out_shape=jax.ShapeDtypeStruct((DEPTH, 8, 6 * D_MODEL), F32),
        compiler_params=_params(("arbitrary", "arbitrary")),
        name="ada_mod",
    )(cvecs, ada_w, ada_b.reshape(DEPTH, 1, 6 * D_MODEL))
    return out.reshape(DEPTH, 8, 6, D_MODEL)[:, :3]


def _mod_group_of_tile(i):
    return jnp.maximum(i - (N_PROMPT // TOK_TILE - 1), 0)


def _norm_mod_rows(x_ref, g_ref, mod_ref, h_ref, shift_idx, scale_idx, rows=64):
    g = g_ref[...]
    sc = 1.0 + mod_ref[0, scale_idx:scale_idx + 1, :]
    sh = mod_ref[0, shift_idx:shift_idx + 1, :]

    def body(r, carry):
        sl = pl.ds(pl.multiple_of(r * rows, rows), rows)
        xf = x_ref[sl, :]
        ms = jnp.mean(xf * xf, axis=-1, keepdims=True)
        y = xf * lax.rsqrt(ms + EPS) * g
        h_ref[sl, :] = (y * sc + sh).astype(BF16)
        return carry

    lax.fori_loop(0, x_ref.shape[0] // rows, body, 0, unroll=4)


ROW_TM = 512


class _Rows(NamedTuple):
    prompt: jax.Array
    latent: jax.Array
    latent_row0: int


def _one_array(x):
    return _Rows(x, x, N_PROMPT)


def _row_specs(rows, width, col_block=0):
    n_p = N_PROMPT // ROW_TM
    l0 = rows.latent_row0 // ROW_TM
    return [pl.BlockSpec((ROW_TM, width), lambda t: (jnp.minimum(t, n_p - 1), col_block)),
            pl.BlockSpec((ROW_TM, width), lambda t: (l0 + jnp.maximum(t - n_p, 0), col_block))]


def _row_group(t):
    first_latent = N_PROMPT // ROW_TM
    return jnp.where(t < first_latent, 0, 1 + (t - first_latent) // (DEC_SEQ // ROW_TM))


def _is_prompt_tile():
    return pl.program_id(0) < N_PROMPT // ROW_TM


def _proj_kernel(xp_ref, xl_ref, g_ref, mod_ref, *refs):
    *w_refs, o_ref, h_ref, wb_ref = refs

    @pl.when(pl.program_id(0) == 0)
    def _():
        off = 0
        for w_ref in w_refs:
            wb_ref[:, off:off + w_ref.shape[1]] = w_ref[...].astype(BF16)
            off += w_ref.shape[1]

    is_prompt = _is_prompt_tile()
    g = g_ref[...]
    sc = 1.0 + mod_ref[0, 1:2, :]
    sh = mod_ref[0, 0:1, :]
    rows = 64
    part = ROW_TM // 2
    for p0 in range(0, ROW_TM, part):
        for r0 in range(p0, p0 + part, rows):
            sl = slice(r0, r0 + rows)
            xf = jnp.where(is_prompt, xp_ref[sl, :], xl_ref[sl, :])
            ms = jnp.mean(xf * xf, axis=-1, keepdims=True)
            y = xf * lax.rsqrt(ms + EPS) * g
            h_ref[sl, :] = (y * sc + sh).astype(BF16)
        o_ref[p0:p0 + part, :] = _bdot(h_ref[p0:p0 + part, :], wb_ref[...])


def _norm_mod_proj(x, g, mods, ws, name):
    n = sum(w.shape[1] for w in ws)
    return pl.pallas_call(
        _proj_kernel,
        grid=(N_TOK // ROW_TM,),
        in_specs=_row_specs(x, D_MODEL) + [
            pl.BlockSpec((1, D_MODEL), lambda t: (0, 0)),
            pl.BlockSpec((1, 6, D_MODEL), lambda t: (_row_group(t), 0, 0))] + [
            pl.BlockSpec(w.shape, lambda t: (0, 0), pipeline_mode=pl.Buffered(1)) for w in ws],
        out_specs=pl.BlockSpec((ROW_TM, n), lambda t: (t, 0)),
        out_shape=jax.ShapeDtypeStruct((N_TOK, n), F32),
        scratch_shapes=[pltpu.VMEM((ROW_TM, D_MODEL), BF16),
                        pltpu.VMEM((D_MODEL, n), BF16)],
        compiler_params=_params(("arbitrary",)),
        name=name,
    )(x.prompt, x.latent, g.reshape(1, D_MODEL), mods, *ws)


def _rms(a, g):
    return a * lax.rsqrt(jnp.mean(a * a, axis=-1, keepdims=True) + EPS) * g


def _mla_front_kernel(xp_ref, xl_ref, g_ref, mod_ref, wl_ref, gqa_ref, wuq_ref, gkv_ref, wukv_ref,
                      qp_ref, kvp_ref, ckv_ref, kr_ref, h_ref, wlb_ref, wuqb_ref, wukvb_ref):
    @pl.when(pl.program_id(0) == 0)
    def _():
        wlb_ref[...] = wl_ref[...].astype(BF16)
        wuqb_ref[...] = wuq_ref[...].astype(BF16)
        wukvb_ref[...] = wukv_ref[...].astype(BF16)

    is_prompt = _is_prompt_tile()
    g = g_ref[...]
    sc = 1.0 + mod_ref[0, 1:2, :]
    sh = mod_ref[0, 0:1, :]
    rows = 64
    part = ROW_TM // 2
    for p0 in range(0, ROW_TM, part):
        for r0 in range(p0, p0 + part, rows):
            sl = slice(r0, r0 + rows)
            xf = jnp.where(is_prompt, xp_ref[sl, :], xl_ref[sl, :])
            h_ref[sl, :] = (_rms(xf, g) * sc + sh).astype(BF16)
        ps = slice(p0, p0 + part)
        low = _bdot(h_ref[ps, :], wlb_ref[...])
        qa = _rms(low[:, MLA_LOW_Q:MLA_LOW_Q + MLA_Q_RANK], gqa_ref[...])
        qp_ref[ps, :] = _bdot(qa.astype(BF16), wuqb_ref[...])
        ckv = _rms(low[:, MLA_LOW_KV:MLA_LOW_KV + MLA_KV_RANK], gkv_ref[...])
        ckv_ref[ps, :] = ckv
        kvp_ref[ps, :] = _bdot(ckv.astype(BF16), wukvb_ref[...])
        kr_ref[ps, :] = low[:, MLA_LOW_KR:MLA_LOW_KR + MLA_HL]


def _mla_front(x, g, mods, w_low, g_qa, w_uq, g_kva, w_ukv):
    n_q, n_kv = w_uq.shape[1], w_ukv.shape[1]
    const = lambda shape: pl.BlockSpec(shape, lambda t: (0, 0), pipeline_mode=pl.Buffered(1))
    out = lambda n: pl.BlockSpec((ROW_TM, n), lambda t: (t, 0))
    shape = lambda n: jax.ShapeDtypeStruct((N_TOK, n), F32)
    return pl.pallas_call(
        _mla_front_kernel,
        grid=(N_TOK // ROW_TM,),
        in_specs=_row_specs(x, D_MODEL) + [
            pl.BlockSpec((1, D_MODEL), lambda t: (0, 0)),
            pl.BlockSpec((1, 6, D_MODEL), lambda t: (_row_group(t), 0, 0)),
            const(w_low.shape), const((1, MLA_Q_RANK)), const(w_uq.shape),
            const((1, MLA_KV_RANK)), const(w_ukv.shape)],
        out_specs=[out(n_q), out(n_kv), out(MLA_KV_RANK), out(MLA_HL)],
        out_shape=[shape(n_q), shape(n_kv), shape(MLA_KV_RANK), shape(MLA_HL)],
        scratch_shapes=[pltpu.VMEM((ROW_TM, D_MODEL), BF16),
                        pltpu.VMEM(w_low.shape, BF16),
                        pltpu.VMEM(w_uq.shape, BF16),
                        pltpu.VMEM(w_ukv.shape, BF16)],
        compiler_params=_params(("arbitrary",)),
        name="mla_front",
    )(x.prompt, x.latent, g.reshape(1, D_MODEL), mods, w_low, g_qa.reshape(1, -1), w_uq,
      g_kva.reshape(1, -1), w_ukv)


def _matmul_kernel(a_ref, w_ref, o_ref):
    o_ref[...] = _bdot(a_ref[...].astype(BF16), w_ref[...].astype(BF16))


def _matmul(a, w, name):
    rows, n = a.shape[0], w.shape[1]
    return pl.pallas_call(
        _matmul_kernel,
        out_shape=jax.ShapeDtypeStruct((rows, n), F32),
        compiler_params=_params(()),
        name=name,
    )(a, w)


def _oproj_kernel(*refs, gated):
    if gated:
        xp_ref, xl_ref, ap_ref, al_ref, g_ref, mod_ref, w_ref, o_ref, wb_ref = refs
    else:
        xp_ref, xl_ref, ap_ref, al_ref, mod_ref, w_ref, o_ref, wb_ref = refs

    @pl.when(pl.program_id(0) == 0)
    def _():
        wb_ref[...] = w_ref[...].astype(BF16)

    is_prompt = _is_prompt_tile()
    a = jnp.where(is_prompt, ap_ref[...], al_ref[...])
    if gated:
        a = a * _silu(g_ref[...])
    y = _bdot(a.astype(BF16), wb_ref[...])
    x = jnp.where(is_prompt, xp_ref[...], xl_ref[...])
    o_ref[...] = x + mod_ref[0, 2:3, :] * y


def _out_proj_residual(x, a, mods, w, name, gate=None, gate_col_block=0):
    k = w.shape[0]
    in_specs = _row_specs(x, D_MODEL) + _row_specs(a, k)
    args = [x.prompt, x.latent, a.prompt, a.latent]
    if gate is not None:
        in_specs.append(pl.BlockSpec((ROW_TM, k), lambda t: (t, gate_col_block)))
        args.append(gate)
    in_specs += [pl.BlockSpec((1, 6, D_MODEL), lambda t: (_row_group(t), 0, 0)),
                 pl.BlockSpec((k, D_MODEL), lambda t: (0, 0))]
    args += [mods, w]
    return pl.pallas_call(
        functools.partial(_oproj_kernel, gated=gate is not None),
        grid=(N_TOK // ROW_TM,),
        in_specs=in_specs,
        out_specs=pl.BlockSpec((ROW_TM, D_MODEL), lambda t: (t, 0)),
        out_shape=jax.ShapeDtypeStruct((N_TOK, D_MODEL), F32),
        scratch_shapes=[pltpu.VMEM((k, D_MODEL), BF16)],
        compiler_params=_params(("arbitrary",)),
        name=name,
    )(*args)


FFN_MM_ROWS = 512
FFN_ROWS = 64
FFN_PAD = 8


def _ffn_kernel(x_ref, g_ref, mod_ref, wg_ref, wv_ref, cwg_ref, cwv_ref, cbg_ref, cbv_ref,
                wd_ref, o_ref, h_ref, u_ref, act_ref, wup_ref, wdn_ref, *, tile0):
    i = tile0 + pl.program_id(0)
    c = pl.program_id(1)
    fc = FF_CHUNK
    n = TOK_TILE // FFN_MM_ROWS
    seq_len = jnp.where(i < N_PROMPT // TOK_TILE, SEQ, DEC_SEQ)
    row = lax.broadcasted_iota(jnp.int32, (FFN_ROWS, 1), 0)
    taps = lambda cw_ref, cb_ref: [jnp.broadcast_to(cw_ref[j:j + 1, :], (FFN_ROWS, fc))
                                   for j in range(3)] + [
                                       jnp.broadcast_to(cb_ref[...], (FFN_ROWS, fc))]
    taps_g, taps_v = taps(cwg_ref, cbg_ref), taps(cwv_ref, cbv_ref)

    def up(u_ref, t):
        r0 = t * FFN_MM_ROWS
        u_ref[FFN_PAD + r0:FFN_PAD + r0 + FFN_MM_ROWS, :] = _bdot(
            h_ref[r0:r0 + FFN_MM_ROWS, :], wup_ref[...])

    def conv_act(u_ref, t):
        for r0 in range(t * FFN_MM_ROWS, (t + 1) * FFN_MM_ROWS, FFN_ROWS):
            halves = []
            for lo, (w0, w1, w2, bias) in ((0, taps_g), (fc, taps_v)):
                p0 = FFN_PAD + r0
                prev = u_ref[p0 - 1:p0 - 1 + FFN_ROWS, lo:lo + fc]
                mid = u_ref[p0:p0 + FFN_ROWS, lo:lo + fc]
                nxt = u_ref[p0 + 1:p0 + 1 + FFN_ROWS, lo:lo + fc]
                if r0 % SEQ == 0:
                    prev = jnp.where(((r0 + row) & (seq_len - 1)) == 0, 0.0, prev)
                if (r0 + FFN_ROWS) % SEQ == 0:
                    nxt = jnp.where(((r0 + row) & (seq_len - 1)) == seq_len - 1, 0.0, nxt)
                halves.append(prev * w0 + mid * w1 + nxt * w2 + bias)
            act_ref[r0:r0 + FFN_ROWS, :] = (_silu(halves[0]) * halves[1]).astype(BF16)

    def down(t):
        r0 = t * FFN_MM_ROWS
        o_ref[r0:r0 + FFN_MM_ROWS, :] += _bdot(act_ref[r0:r0 + FFN_MM_ROWS, :], wdn_ref[...])

    @pl.when(c == 0)
    def _():
        _norm_mod_rows(x_ref, g_ref, mod_ref, h_ref, 3, 4)
        zeros = jnp.zeros((FFN_PAD, 2 * fc), F32)
        u_ref[0:FFN_PAD, :] = zeros
        u_ref[FFN_PAD + TOK_TILE:, :] = zeros
        o_ref[...] = jnp.zeros_like(o_ref)

    wup_ref[:, :fc] = wg_ref[...].astype(BF16)
    wup_ref[:, fc:] = wv_ref[...].astype(BF16)
    wdn_ref[...] = wd_ref[...].astype(BF16)
    for s in range(n + 2):
        if s < n:
            up(u_ref, s)
        if 1 <= s <= n:
            conv_act(u_ref, s - 1)
        if s >= 2:
            down(s - 2)

    @pl.when(c == N_FF_CHUNKS - 1)
    def _():
        o_ref[...] = x_ref[...] + mod_ref[0, 5:6, :] * o_ref[...]


def _ffn(x, g, mods, layer, w_up, conv_w, conv_b, w_down, tile0=0, n_tiles=N_TOK_TILES):
    fc = FF_CHUNK
    ncb = N_FF_CHUNKS
    return pl.pallas_call(
        functools.partial(_ffn_kernel, tile0=tile0),
        grid=(n_tiles, ncb),
        in_specs=[pl.BlockSpec((TOK_TILE, D_MODEL), lambda i, c: (tile0 + i, 0)),
                  pl.BlockSpec((1, D_MODEL), lambda i, c: (0, 0)),
                  pl.BlockSpec((1, 6, D_MODEL), lambda i, c: (_mod_group_of_tile(tile0 + i), 0, 0)),
                  pl.BlockSpec((None, D_MODEL, fc), lambda i, c: (layer, 0, c)),
                  pl.BlockSpec((None, D_MODEL, fc), lambda i, c: (layer, 0, ncb + c)),
                  pl.BlockSpec((None, 3, fc), lambda i, c: (layer, 0, c)),
                  pl.BlockSpec((None, 3, fc), lambda i, c: (layer, 0, ncb + c)),
                  pl.BlockSpec((None, 1, fc), lambda i, c: (layer, 0, c)),
                  pl.BlockSpec((None, 1, fc), lambda i, c: (layer, 0, ncb + c)),
                  pl.BlockSpec((None, fc, D_MODEL), lambda i, c: (layer, c, 0))],
        out_specs=pl.BlockSpec((TOK_TILE, D_MODEL), lambda i, c: (i, 0)),
        out_shape=jax.ShapeDtypeStruct((n_tiles * TOK_TILE, D_MODEL), F32),
        scratch_shapes=[pltpu.VMEM((TOK_TILE, D_MODEL), BF16),
                        pltpu.VMEM((TOK_TILE + 2 * FFN_PAD, 2 * fc), F32),
                        pltpu.VMEM((TOK_TILE, fc), BF16),
                        pltpu.VMEM((D_MODEL, 2 * fc), BF16),
                        pltpu.VMEM((fc, D_MODEL), BF16)],
        compiler_params=_params(("arbitrary", "arbitrary")),
        name="conv_ffn",
    )(x, g.reshape(1, D_MODEL), mods, w_up, w_up, conv_w, conv_w,
      conv_b.reshape(DEPTH, 1, -1), conv_b.reshape(DEPTH, 1, -1), w_down)


NA_HB = 4
NA_LANES = NA_HB * NA_HD
NA_ROWS = DEC_SEQ // GRID_W
NA_KEYS = NA_WIN_R * GRID_W
PROMPT_SEQS = 4
NA_ROW_UNROLL = 8


def _store_heads(dst_ref, seq, x, n_heads, hd):
    for h in range(n_heads):
        dst_ref[seq, h] = x[:, h * hd:(h + 1) * hd]


def _na_prompt_kernel(q_ref, k_ref, v_ref, gq_ref, gk_ref, o_ref, kn_ref, vn_ref):
    scale = NA_HD ** -0.5 * LOG2E

    def chains(bound):
        shift = None if bound is None else _stacked_rows(bound, SEQ)
        for seq in range(PROMPT_SEQS):
            sl = slice(seq * SEQ, (seq + 1) * SEQ)
            q = _group_rms(q_ref[sl, :], gq_ref[...], NA_HD, sums_on_mxu=True) * scale
            k = _group_rms(k_ref[sl, :], gk_ref[...], NA_HD, sums_on_mxu=True)
            v = v_ref[sl, :]
            _store_heads(kn_ref, seq, k, NA_HB, NA_HD)
            _store_heads(vn_ref, seq, v, NA_HB, NA_HD)
            q4 = _stack_heads(q, NA_HB, NA_HD).astype(BF16)
            s = lax.dot_general(q4, k.astype(BF16), _NT, preferred_element_type=F32)
            (p,), inv = _softmax_parts([s], shift)
            o4 = _bdot(p.astype(BF16), v.astype(BF16)) * inv
            o_ref[sl, :] = _unstack_heads(o4, NA_HB, NA_HD).astype(BF16)

    _with_score_bound(scale * _rms_norm_bound(gq_ref, NA_HD, NA_HD)
                      * _rms_norm_bound(gk_ref, NA_HD, NA_HD), chains)


def _na_prompt(qkv, gq, gk):
    nb = NA_HEADS // NA_HB
    rows = PROMPT_SEQS * SEQ
    blk = lambda off: pl.BlockSpec((rows, NA_LANES), lambda b, j: (b, off + j))
    vec = pl.BlockSpec((1, NA_LANES), lambda b, j: (0, 0))
    cache = pl.BlockSpec((PROMPT_SEQS, NA_HB, SEQ, NA_HD), lambda b, j: (b, j, 0, 0))
    cache_shape = jax.ShapeDtypeStruct((BATCH, NA_HEADS, SEQ, NA_HD), F32)
    return pl.pallas_call(
        _na_prompt_kernel,
        grid=(BATCH // PROMPT_SEQS, nb),
        in_specs=[blk(0), blk(nb), blk(2 * nb), vec, vec],
        out_specs=[blk(0), cache, cache],
        out_shape=[jax.ShapeDtypeStruct((N_PROMPT, D_MODEL), BF16), cache_shape, cache_shape],
        compiler_params=_params(("arbitrary", "arbitrary")),
        name="na_prompt",
    )(qkv, qkv, qkv, jnp.tile(gq, NA_HB).reshape(1, -1), jnp.tile(gk, NA_HB).reshape(1, -1))


def _na_latent_kernel(q_ref, k_ref, v_ref, kc_ref, vc_ref, t_ref, gq_ref, gk_ref, o_ref,
                      qn_ref, kn_ref, vb_ref, kc4_ref, vc4_ref, bias_ref):
    scale = NA_HD ** -0.5 * LOG2E
    rows = 256

    def prep(r, carry):
        sl = pl.ds(pl.multiple_of(r * rows, rows), rows)
        qn_ref[sl, :] = (_group_rms(q_ref[sl, :], gq_ref[...], NA_HD, sums_on_mxu=True)
                         * scale).astype(BF16)
        kn_ref[sl, :] = _group_rms(k_ref[sl, :], gk_ref[...], NA_HD,
                                   sums_on_mxu=True).astype(BF16)
        vb_ref[sl, :] = v_ref[sl, :].astype(BF16)
        return carry

    lax.fori_loop(0, DEC_SEQ // rows, prep, 0, unroll=2)
    kc = kc_ref[0]
    kc4_ref[...] = kc.astype(BF16)
    vc4_ref[...] = vc_ref[0].astype(BF16)

    def attend(bound):
        for h in range(NA_HB):
            off = 0.0 if bound is None else bound[:, h:h + 1]
            for p in range(NA_WIN_R):
                for i in range(NA_WIN_R):
                    bias_ref[h, p, :, i * GRID_W:(i + 1) * GRID_W] = t_ref[h, p + i] - off
        shift = None if bound is None else _stacked_rows(bound, GRID_W)

        def row(r):
            kr0 = jnp.clip(r - NA_WIN_R // 2, 0, NA_ROWS - NA_WIN_R)
            pat = kr0 - r + NA_WIN_R - 1
            qs = pl.ds(pl.multiple_of(r * GRID_W, GRID_W), GRID_W)
            ks = pl.ds(pl.multiple_of(kr0 * GRID_W, GRID_W), NA_KEYS)
            q4 = _stack_heads(qn_ref[qs, :], NA_HB, NA_HD)
            s_loc = lax.dot_general(q4, kn_ref[ks, :], _NT, preferred_element_type=F32)
            s_loc = s_loc + jnp.concatenate([bias_ref[h, pat] for h in range(NA_HB)], axis=0)
            s_ctx = lax.dot_general(q4, kc4_ref[...], _NT, preferred_element_type=F32)
            if shift is None:
                (p_loc, p_ctx), inv = _softmax_parts([s_loc, s_ctx])
            else:
                p_loc, p_ctx = jnp.exp2(s_loc), jnp.exp2(s_ctx - shift)
                inv = 1.0 / (p_loc.sum(axis=-1, keepdims=True) + p_ctx.sum(axis=-1, keepdims=True))
            o4 = _bdot(p_loc.astype(BF16), vb_ref[ks, :]) + _bdot(p_ctx.astype(BF16), vc4_ref[...])
            o_ref[qs, :] = _unstack_heads(o4 * inv, NA_HB, NA_HD).astype(BF16)

        def rows_step(i, carry):
            for u in range(NA_ROW_UNROLL):
                row(i * NA_ROW_UNROLL + u)
            return carry

        lax.fori_loop(0, NA_ROWS // NA_ROW_UNROLL, rows_step, 0)

    qmax = scale * _rms_norm_bound(gq_ref, NA_HD, NA_HD)
    bias_max = jnp.concatenate(
        [t_ref[h].max(axis=0).max(axis=0, keepdims=True).max(axis=1, keepdims=True)
         for h in range(NA_HB)], axis=-1)
    _with_score_bound(
        jnp.maximum(qmax * _rms_norm_bound(gk_ref, NA_HD, NA_HD) + bias_max,
                    qmax * _max_head_norms(kc, NA_HD)), attend)


def _na_bias_blocks(bias_table):
    qc = jnp.arange(GRID_W)[:, None]
    kc = jnp.arange(GRID_W)[None, :]
    win0 = jnp.clip(qc - NA_WIN_C // 2, 0, GRID_W - NA_WIN_C)
    valid = (kc >= win0) & (kc < win0 + NA_WIN_C)
    n_co = bias_table.shape[-1]
    onehot = (kc - qc + NA_WIN_C - 1)[None] == jnp.arange(n_co)[:, None, None]
    t = jnp.einsum('hrd,dqk->hrqk', bias_table.astype(F32), onehot.astype(F32),
                   precision=lax.Precision.HIGHEST)
    return jnp.where(valid, t * LOG2E, NEG)


def _na_latent(qkv, cache_k, cache_v, bias_blocks, gq, gk):
    nb = NA_HEADS // NA_HB
    lat0 = N_PROMPT // DEC_SEQ
    blk = lambda off: pl.BlockSpec((DEC_SEQ, NA_LANES), lambda b, j: (lat0 + b, off + j))
    vec = pl.BlockSpec((1, NA_LANES), lambda b, j: (0, 0))
    cache = pl.BlockSpec((1, PAST_LEN, NA_LANES), lambda b, j: (b, 0, j))
    return pl.pallas_call(
        _na_latent_kernel,
        grid=(DEC_BATCH, nb),
        in_specs=[blk(0), blk(nb), blk(2 * nb), cache, cache,
                  pl.BlockSpec((NA_HB, 2 * NA_WIN_R - 1, GRID_W, GRID_W), lambda b, j: (j, 0, 0, 0)),
                  vec, vec],
        out_specs=pl.BlockSpec((DEC_SEQ, NA_LANES), lambda b, j: (b, j)),
        out_shape=jax.ShapeDtypeStruct((N_LATENT, D_MODEL), BF16),
        scratch_shapes=[pltpu.VMEM((DEC_SEQ, NA_LANES), BF16),
                        pltpu.VMEM((DEC_SEQ, NA_LANES), BF16),
                        pltpu.VMEM((DEC_SEQ, NA_LANES), BF16),
                        pltpu.VMEM((PAST_LEN, NA_LANES), BF16),
                        pltpu.VMEM((PAST_LEN, NA_LANES), BF16),
                        pltpu.VMEM((NA_HB, NA_WIN_R, GRID_W, NA_KEYS), F32)],
        compiler_params=_params(("arbitrary", "arbitrary")),
        name="na_latent",
    )(qkv, qkv, qkv, _tokens_first(cache_k), _tokens_first(cache_v), bias_blocks,
      jnp.tile(gq, NA_HB).reshape(1, -1), jnp.tile(gk, NA_HB).reshape(1, -1))


GLA_C = 128
GLA_SUB = 8
GLA_LEVELS = (64, 32, 16, 8)


def _split_hi_lo(x):
    hi = x.astype(BF16)
    lo = (x - hi.astype(F32)).astype(BF16)
    return jnp.concatenate([hi, lo], axis=1)


class _GlaMasks(NamedTuple):
    tri: jax.Array
    later: tuple
    sign: tuple
    pair: tuple
    diag: jax.Array


def _gla_masks(rev):
    c = GLA_C
    row = lax.broadcasted_iota(jnp.int32, (c, c), 0)
    col = lax.broadcasted_iota(jnp.int32, (c, c), 1)
    rid = lax.broadcasted_iota(jnp.int32, (c, GLA_DK), 0)
    causal = (col >= row) if rev else (col <= row)
    later, pair = [], []
    for m in GLA_LEVELS:
        later.append(((rid & m) == 0) if rev else ((rid & m) != 0))
        same = (row >> _log2(2 * m)) == (col >> _log2(2 * m))
        crossing = ((row & m) != (col & m))
        pair.append(same & crossing & causal)
    diag = ((row >> _log2(GLA_SUB)) == (col >> _log2(GLA_SUB))) & causal
    sign = tuple(jnp.where(l, 1.0, -1.0) for l in later)
    return _GlaMasks(jnp.where(causal, 1.0, 0.0).astype(BF16), tuple(later), sign, tuple(pair), diag)


def _gla_chunk(q, k, v, g, st_ref, rev, masks):
    c = GLA_C
    cs = _bdot(masks.tri, _split_hi_lo(g))
    b = cs[:, :GLA_DK] + cs[:, GLA_DK:]

    a = None
    for m, later, sign, pair in zip(GLA_LEVELS, masks.later, masks.sign, masks.pair):
        nblk = c // (2 * m)
        if rev:
            bnd = [b[j * 2 * m + m:j * 2 * m + m + 1] for j in range(nblk)]
        else:
            bnd = [b[j * 2 * m + m - 1:j * 2 * m + m] for j in range(nblk)]
        ref = jnp.concatenate([jnp.broadcast_to(x, (2 * m, GLA_DK)) for x in bnd], axis=0)
        x = (jnp.where(later, q, k) * jnp.exp((b - ref) * sign)).astype(BF16)
        blk = lax.dot_general(x, x, _NT, preferred_element_type=F32)
        a = jnp.where(pair, blk, 0.0 if a is None else a)

    nsub = c // GLA_SUB
    lane_c = lax.broadcasted_iota(jnp.int32, (GLA_SUB, c), 1)
    diag_rows = []
    for blk_i in range(nsub):
        r0 = blk_i * GLA_SUB
        qb = q[r0:r0 + GLA_SUB]
        bb = b[r0:r0 + GLA_SUB]
        acc = jnp.zeros((GLA_SUB, c), F32)
        for s in range(GLA_SUB):
            ks = k[r0 + s:r0 + s + 1]
            bs = b[r0 + s:r0 + s + 1]
            w = jnp.sum(qb * ks * jnp.exp(jnp.minimum(bb - bs, 0.0)), axis=-1, keepdims=True)
            acc = jnp.where(lane_c == r0 + s, w, acc)
        diag_rows.append(acc)
    a = jnp.where(masks.diag, jnp.concatenate(diag_rows, axis=0), a)

    st = st_ref[...]
    inter = lax.dot_general((q * jnp.exp(b)).astype(BF16), st.astype(BF16), _NT,
                            preferred_element_type=F32)
    o = inter + _bdot(a.astype(BF16), v.astype(BF16))

    btot = b[0:1] if rev else b[c - 1:c]
    kd = (k * jnp.exp(btot - b)).astype(BF16)
    st_ref[...] = st * jnp.exp(btot) + lax.dot_general(v.astype(BF16), kd, _TN,
                                                       preferred_element_type=F32)
    return o


def _gla_kernel(*refs, n_tok, has_state, hps):
    if has_state:
        (q_ref, k_ref, v_ref, r_ref, w2_ref, bg_ref, gn_ref, s0f_ref, s0b_ref,
         o_ref, lg_ref, of_ref, ob_ref, stf_ref, stb_ref) = refs
    else:
        (q_ref, k_ref, v_ref, r_ref, w2_ref, bg_ref, gn_ref,
         o_ref, sf_ref, sb_ref, lg_ref, of_ref, ob_ref, stf_ref, stb_ref) = refs
    nc = n_tok // GLA_C
    scale = GLA_DK ** -0.5

    rb = r_ref[...].astype(BF16)
    for z in range(2):
        x = _bdot(rb, w2_ref[z].astype(BF16)) + bg_ref[z]
        lg_ref[z] = (jnp.minimum(x, 0.0) - jnp.log1p(jnp.exp(-jnp.abs(x)))) * (1.0 / GLA_GATE_NORM)

    for hh in range(hps):
        if has_state:
            stf_ref[hh] = s0f_ref[0, hh].T
            stb_ref[hh] = s0b_ref[0, hh].T
        else:
            stf_ref[hh] = jnp.zeros((GLA_DV, GLA_DK), F32)
            stb_ref[hh] = jnp.zeros((GLA_DV, GLA_DK), F32)

    masks = {rev: _gla_masks(rev) for rev in (False, True)}

    def step(ci, carry):
        for hh in range(hps):
            kq = slice(hh * GLA_DK, (hh + 1) * GLA_DK)
            vv = slice(hh * GLA_DV, (hh + 1) * GLA_DV)
            for rev in (False, True):
                cc = (nc - 1 - ci) if rev else ci
                sl = pl.ds(pl.multiple_of(cc * GLA_C, GLA_C), GLA_C)
                o = _gla_chunk(q_ref[sl, kq] * scale, k_ref[sl, kq], v_ref[sl, vv],
                               lg_ref[1 if rev else 0, sl, kq],
                               (stb_ref if rev else stf_ref).at[hh], rev, masks[rev])
                (ob_ref if rev else of_ref)[sl, vv] = o
        return carry

    lax.fori_loop(0, nc, step, 0)

    for hh in range(hps):
        vv = slice(hh * GLA_DV, (hh + 1) * GLA_DV)
        o = of_ref[:, vv] + ob_ref[:, vv]
        o_ref[:, vv] = o * lax.rsqrt(jnp.mean(o * o, axis=-1, keepdims=True) + EPS) * gn_ref[...]
        if not has_state:
            sf_ref[0, hh] = stf_ref[hh].T
            sb_ref[0, hh] = stb_ref[hh].T


def _gla(proj, w2, bg, gnorm, n_seq, n_tok, row_block0, hps, states=None):
    kw, vw = hps * GLA_DK, hps * GLA_DV
    spec = lambda width, off: pl.BlockSpec((n_tok, width), lambda b, h: (row_block0 + b, off + h))
    in_specs = [spec(kw, 0), spec(kw, GLA_HK // kw), spec(vw, 2 * GLA_HK // vw),
                pl.BlockSpec((n_tok, 128), lambda b, h: (row_block0 + b, (2 * GLA_HK + 2 * GLA_HV) // 128)),
                pl.BlockSpec((2, 128, kw), lambda b, h: (0, 0, h)),
                pl.BlockSpec((2, 1, kw), lambda b, h: (0, 0, h)),
                pl.BlockSpec((1, GLA_DV), lambda b, h: (0, 0))]
    args = [proj, proj, proj, proj, w2, bg, gnorm.reshape(1, GLA_DV)]
    st_spec = pl.BlockSpec((1, hps, GLA_DK, GLA_DV), lambda b, h: (b, h, 0, 0))
    o_spec = pl.BlockSpec((n_tok, vw), lambda b, h: (b, h))
    o_shape = jax.ShapeDtypeStruct((n_seq * n_tok, GLA_HV), F32)
    if states is not None:
        in_specs += [st_spec, st_spec]
        args += list(states)
        out_specs, out_shape = o_spec, o_shape
    else:
        st_shape = jax.ShapeDtypeStruct((n_seq, GLA_HEADS, GLA_DK, GLA_DV), F32)
        out_specs, out_shape = [o_spec, st_spec, st_spec], [o_shape, st_shape, st_shape]
    return pl.pallas_call(
        functools.partial(_gla_kernel, n_tok=n_tok, has_state=states is not None, hps=hps),
        grid=(n_seq, GLA_HEADS // hps),
        in_specs=in_specs,
        out_specs=out_specs,
        out_shape=out_shape,
        scratch_shapes=[pltpu.VMEM((2, n_tok, kw), F32),
                        pltpu.VMEM((n_tok, vw), F32),
                        pltpu.VMEM((n_tok, vw), F32),
                        pltpu.VMEM((hps, GLA_DV, GLA_DK), F32),
                        pltpu.VMEM((hps, GLA_DV, GLA_DK), F32)],
        compiler_params=_params(("arbitrary", "arbitrary")),
        name="gla_latent" if states is not None else "gla_prompt",
    )(*args)


DIFF_HB = 2
DIFF_QL = DIFF_HB * DIFF_HD
DIFF_VL = DIFF_HB * 2 * DIFF_HD
DIFF_TQ = 512
DIFF_SUB = 128


def _diff_lambda(lam_ref):
    l = lam_ref[...]
    a = jnp.sum(l[0:1] * l[1:2], axis=-1, keepdims=True)
    b = jnp.sum(l[2:3] * l[3:4], axis=-1, keepdims=True)
    return jnp.exp(a) - jnp.exp(b) + DIFF_LAMBDA_INIT


def _diff_finish(ps, invs, lam, v, sn_ref):
    a = ps[0] * invs[0] - (lam * invs[1]) * ps[1]
    o2 = _bdot(a.astype(BF16), v)
    o = _unstack_heads(o2, DIFF_HB, 2 * DIFF_HD)
    return _group_rms(o, sn_ref[...], 2 * DIFF_HD) * (1.0 - DIFF_LAMBDA_INIT)


def _diff_prompt_kernel(q0_ref, q1_ref, k0_ref, k1_ref, v_ref, gq_ref, gk_ref, lam_ref, sn_ref,
                        o_ref, kn_ref, vn_ref):
    scale = DIFF_HD ** -0.5 * LOG2E
    lam = _diff_lambda(lam_ref)

    def chains(bound):
        shift = None if bound is None else _stacked_rows(bound, SEQ)
        for seq in range(PROMPT_SEQS):
            sl = slice(seq * SEQ, (seq + 1) * SEQ)
            ps, invs = [], []
            for comp, (q_ref, k_ref) in enumerate(((q0_ref, k0_ref), (q1_ref, k1_ref))):
                q = _group_rms(q_ref[sl, :], gq_ref[...], DIFF_HD, sums_on_mxu=True) * scale
                k = _group_rms(k_ref[sl, :], gk_ref[...], DIFF_HD, sums_on_mxu=True)
                for h in range(DIFF_HB):
                    kn_ref[seq, comp, h] = k[:, h * DIFF_HD:(h + 1) * DIFF_HD]
                q2 = _stack_heads(q, DIFF_HB, DIFF_HD).astype(BF16)
                s = lax.dot_general(q2, k.astype(BF16), _NT, preferred_element_type=F32)
                (p,), inv = _softmax_parts([s], shift)
                ps.append(p)
                invs.append(inv)
            v = v_ref[sl, :]
            _store_heads(vn_ref, seq, v, DIFF_HB, 2 * DIFF_HD)
            o_ref[sl, :] = _diff_finish(ps, invs, lam, v.astype(BF16), sn_ref).astype(BF16)

    _with_score_bound(scale * _rms_norm_bound(gq_ref, DIFF_HD, DIFF_HD)
                      * _rms_norm_bound(gk_ref, DIFF_HD, DIFF_HD), chains)


def _diff_prompt(qkv, gq, gk, lam, sub_norm):
    nb = DIFF_HEADS // DIFF_HB
    rows = PROMPT_SEQS * SEQ
    qk = lambda off: pl.BlockSpec((rows, DIFF_QL), lambda b, j: (b, off + j))
    vec = lambda n: pl.BlockSpec((1, n), lambda b, j: (0, 0))
    v_spec = pl.BlockSpec((rows, DIFF_VL), lambda b, j: (b, 2 * D_MODEL // DIFF_VL + j))
    kn_spec = pl.BlockSpec((PROMPT_SEQS, 2, DIFF_HB, SEQ, DIFF_HD), lambda b, j: (b, 0, j, 0, 0))
    vn_spec = pl.BlockSpec((PROMPT_SEQS, DIFF_HB, SEQ, 2 * DIFF_HD), lambda b, j: (b, j, 0, 0))
    return pl.pallas_call(
        _diff_prompt_kernel,
        grid=(BATCH // PROMPT_SEQS, nb),
        in_specs=[qk(0), qk(nb), qk(2 * nb), qk(3 * nb), v_spec, vec(DIFF_QL), vec(DIFF_QL),
                  pl.BlockSpec((4, DIFF_HD), lambda b, j: (0, 0)), vec(DIFF_VL)],
        out_specs=[pl.BlockSpec((rows, DIFF_VL), lambda b, j: (b, j)), kn_spec, vn_spec],
        out_shape=[jax.ShapeDtypeStruct((N_PROMPT, D_MODEL), BF16),
                   jax.ShapeDtypeStruct((BATCH, 2, DIFF_HEADS, SEQ, DIFF_HD), F32),
                   jax.ShapeDtypeStruct((BATCH, DIFF_HEADS, SEQ, 2 * DIFF_HD), F32)],
        compiler_params=_params(("arbitrary", "arbitrary")),
        name="diff_prompt",
    )(qkv, qkv, qkv, qkv, qkv, jnp.tile(gq, DIFF_HB).reshape(1, -1),
      jnp.tile(gk, DIFF_HB).reshape(1, -1), lam, jnp.tile(sub_norm, DIFF_HB).reshape(1, -1))


def _diff_latent_kernel(q0_ref, q1_ref, k0_ref, k1_ref, v_ref, kc0_ref, kc1_ref, vc_ref,
                        cos_ref, sin_ref, cosq_ref, sinq_ref, gq_ref, gk_ref, lam_ref, sn_ref,
                        o_ref, kb0_ref, kb1_ref, vb_ref, kmax_ref):
    scale = DIFF_HD ** -0.5 * LOG2E
    half = DIFF_HD // 4
    rows = 256

    @pl.when(pl.program_id(2) == 0)
    def _():
        for comp, (k_ref, kc_ref, kb_ref) in enumerate(
                ((k0_ref, kc0_ref, kb0_ref), (k1_ref, kc1_ref, kb1_ref))):
            kc = kc_ref[0]
            kb_ref[0:PAST_LEN, :] = kc.astype(BF16)
            kmax_ref[comp:comp + 1, :] = jnp.maximum(
                _max_head_norms(kc, DIFF_HD), _rms_norm_bound(gk_ref, DIFF_HD, DIFF_HD))

            def prep(r, carry):
                sl = pl.ds(pl.multiple_of(r * rows, rows), rows)
                k = _group_rms(k_ref[sl, :], gk_ref[...], DIFF_HD)
                k = _rope(k, cos_ref[sl, :], sin_ref[sl, :], half)
                kb_ref[pl.ds(pl.multiple_of(PAST_LEN + r * rows, rows), rows), :] = k.astype(BF16)
                return carry

            lax.fori_loop(0, DEC_SEQ // rows, prep, 0, unroll=2)
        vb_ref[0:PAST_LEN, :] = vc_ref[0].astype(BF16)
        vb_ref[PAST_LEN:, :] = v_ref[...].astype(BF16)

    lam = _diff_lambda(lam_ref)

    def chains(bound):
        for r0 in range(0, DIFF_TQ, DIFF_SUB):
            sl = slice(r0, r0 + DIFF_SUB)
            ps, invs = [], []
            for comp, (q_ref, kb_ref) in enumerate(((q0_ref, kb0_ref), (q1_ref, kb1_ref))):
                shift = None if bound is None else _stacked_rows(
                    bound[:, comp * DIFF_HB:(comp + 1) * DIFF_HB], DIFF_SUB)
                q = _group_rms(q_ref[sl, :], gq_ref[...], DIFF_HD)
                q = _rope(q, cosq_ref[sl, :], sinq_ref[sl, :], half) * scale
                q2 = _stack_heads(q, DIFF_HB, DIFF_HD).astype(BF16)
                s = lax.dot_general(q2, kb_ref[...], _NT, preferred_element_type=F32)
                (p,), inv = _softmax_parts([s], shift)
                ps.append(p)
                invs.append(inv)
            o_ref[sl, :] = _diff_finish(ps, invs, lam, vb_ref[...], sn_ref).astype(BF16)

    qmax = scale * _rms_norm_bound(gq_ref, DIFF_HD, DIFF_HD)
    _with_score_bound(jnp.concatenate([qmax * kmax_ref[0:1, :], qmax * kmax_ref[1:2, :]], axis=-1),
                      chains)


def _diff_latent(qkv, cache_k, cache_v, cos, sin, gq, gk, lam, sub_norm):
    nb = DIFF_HEADS // DIFF_HB
    nq = DEC_SEQ // DIFF_TQ
    q0 = N_PROMPT // DIFF_TQ
    lat0 = N_PROMPT // DEC_SEQ
    n_keys = PAST_LEN + DEC_SEQ
    q_spec = lambda off: pl.BlockSpec((DIFF_TQ, DIFF_QL), lambda b, j, t: (q0 + b * nq + t, off + j))
    k_spec = lambda off: pl.BlockSpec((DEC_SEQ, DIFF_QL), lambda b, j, t: (lat0 + b, off + j))
    v_spec = pl.BlockSpec((DEC_SEQ, DIFF_VL), lambda b, j, t: (lat0 + b, 2 * D_MODEL // DIFF_VL + j))
    kc_spec = lambda off: pl.BlockSpec((1, PAST_LEN, DIFF_QL), lambda b, j, t: (b, 0, off + j))
    vc_spec = pl.BlockSpec((1, PAST_LEN, DIFF_VL), lambda b, j, t: (b, 0, j))
    tab = pl.BlockSpec((DEC_SEQ, DIFF_QL), lambda b, j, t: (0, 0))
    tabq = pl.BlockSpec((DIFF_TQ, DIFF_QL), lambda b, j, t: (t, 0))
    vec = lambda n: pl.BlockSpec((1, n), lambda b, j, t: (0, 0))
    return pl.pallas_call(
        _diff_latent_kernel,
        grid=(DEC_BATCH, nb, nq),
        in_specs=[q_spec(0), q_spec(nb), k_spec(2 * nb), k_spec(3 * nb), v_spec,
                  kc_spec(0), kc_spec(nb), vc_spec, tab, tab, tabq, tabq,
                  vec(DIFF_QL), vec(DIFF_QL),
                  pl.BlockSpec((4, DIFF_HD), lambda b, j, t: (0, 0)), vec(DIFF_VL)],
        out_specs=pl.BlockSpec((DIFF_TQ, DIFF_VL), lambda b, j, t: (b * nq + t, j)),
        out_shape=jax.ShapeDtypeStruct((N_LATENT, D_MODEL), BF16),
        scratch_shapes=[pltpu.VMEM((n_keys, DIFF_QL), BF16),
                        pltpu.VMEM((n_keys, DIFF_QL), BF16),
                        pltpu.VMEM((n_keys, DIFF_VL), BF16),
                        pltpu.VMEM((2, DIFF_HB), F32)],
        compiler_params=_params(("arbitrary", "arbitrary", "arbitrary")),
        name="diff_latent",
    )(qkv, qkv, qkv, qkv, qkv, _tokens_first(cache_k), _tokens_first(cache_k),
      _tokens_first(cache_v), cos, sin, cos, sin,
      jnp.tile(gq, DIFF_HB).reshape(1, -1), jnp.tile(gk, DIFF_HB).reshape(1, -1), lam,
      jnp.tile(sub_norm, DIFF_HB).reshape(1, -1))


MLA_HB = 2
MLA_HL = 128
MLA_LANES = MLA_HB * MLA_HL
MLA_TQ = 1024
MLA_SUB = 128


def _mla_keys(kv, kr, gk, sums_on_mxu=False):
    lane = lax.broadcasted_iota(jnp.int32, kv.shape, 1)
    kr2 = jnp.concatenate([kr] * MLA_HB, axis=1)
    k = jnp.where((lane & (MLA_HL - 1)) < MLA_NOPE, kv, kr2)
    return _group_rms(k, gk, MLA_HL, n_real=MLA_QK, sums_on_mxu=sums_on_mxu)


def _mla_out(o2):
    tq = o2.shape[0] // MLA_HB
    oa = pltpu.roll(o2[0:tq, 0:MLA_HL], MLA_HL - MLA_V, axis=1)
    ob = o2[tq:, MLA_HL:]
    lane = lax.broadcasted_iota(jnp.int32, oa.shape, 1)
    return jnp.where(lane < MLA_V, oa, ob)


def _mla_prompt_kernel(q_ref, kv_ref, kr_ref, gq_ref, gk_ref, o_ref):
    scale = MLA_QK ** -0.5 * LOG2E

    def chains(bound):
        shift = None if bound is None else _stacked_rows(bound, SEQ)
        for seq in range(PROMPT_SEQS):
            sl = slice(seq * SEQ, (seq + 1) * SEQ)
            q = _group_rms(q_ref[sl, :], gq_ref[...], MLA_HL, n_real=MLA_QK) * scale
            kv = kv_ref[sl, :]
            k = _mla_keys(kv, kr_ref[sl, :], gk_ref[...])
            q2 = _stack_heads(q, MLA_HB, MLA_HL).astype(BF16)
            s = lax.dot_general(q2, k.astype(BF16), _NT, preferred_element_type=F32)
            (p,), inv = _softmax_parts([s], shift)
            o_ref[sl, :] = _mla_out(_bdot(p.astype(BF16), kv.astype(BF16)) * inv).astype(BF16)

    _with_score_bound(scale * _rms_norm_bound(gq_ref, MLA_QK, MLA_HL)
                      * _rms_norm_bound(gk_ref, MLA_QK, MLA_HL), chains)


def _mla_prompt(qp, kvp, low, gq, gk):
    nb = MLA_HEADS // MLA_HB
    rows = PROMPT_SEQS * SEQ
    blk = pl.BlockSpec((rows, MLA_LANES), lambda b, j: (b, j))
    vec = pl.BlockSpec((1, MLA_LANES), lambda b, j: (0, 0))
    return pl.pallas_call(
        _mla_prompt_kernel,
        grid=(BATCH // PROMPT_SEQS, nb),
        in_specs=[blk, blk, pl.BlockSpec((rows, MLA_HL), lambda b, j: (b, 0)),
                  vec, vec],
        out_specs=pl.BlockSpec((rows, MLA_HB * MLA_V), lambda b, j: (b, j)),
        out_shape=jax.ShapeDtypeStruct((N_PROMPT, MLA_HEADS * MLA_V), BF16),
        compiler_params=_params(("arbitrary", "arbitrary")),
        name="mla_prompt",
    )(qp, kvp, low, gq, gk)


def _mla_latent_kernel(q_ref, kv_ref, kr_ref, kvc_ref, krc_ref, cos_ref, sin_ref, cosq_ref,
                       sinq_ref, gq_ref, gk_ref, o_ref, kb_ref, vb_ref, kmax_ref):
    scale = MLA_QK ** -0.5 * LOG2E
    half = MLA_ROPE // 4
    rows = 256

    @pl.when(pl.program_id(2) == 0)
    def _():
        kvc = kvc_ref[...]
        kc = _mla_keys(kvc, krc_ref[...], gk_ref[...])
        kb_ref[0:PAST_LEN, :] = kc.astype(BF16)
        vb_ref[0:PAST_LEN, :] = kvc.astype(BF16)

        def prep(r, carry):
            sl = pl.ds(pl.multiple_of(r * rows, rows), rows)
            dst = pl.ds(pl.multiple_of(PAST_LEN + r * rows, rows), rows)
            kv = kv_ref[sl, :]
            k = _mla_keys(kv, kr_ref[sl, :], gk_ref[...], sums_on_mxu=True)
            k = _rope(k, cos_ref[sl, :], sin_ref[sl, :], half)
            kb_ref[dst, :] = k.astype(BF16)
            vb_ref[dst, :] = kv.astype(BF16)
            return carry

        lax.fori_loop(0, DEC_SEQ // rows, prep, 0, unroll=2)
        kmax_ref[...] = jnp.maximum(_max_head_norms(kc, MLA_HL),
                                    _rms_norm_bound(gk_ref, MLA_QK, MLA_HL))

    def chains(bound):
        shift = None if bound is None else _stacked_rows(bound, MLA_SUB)
        for r0 in range(0, MLA_TQ, MLA_SUB):
            sl = slice(r0, r0 + MLA_SUB)
            q = _group_rms(q_ref[sl, :], gq_ref[...], MLA_HL, n_real=MLA_QK)
            q = _rope(q, cosq_ref[sl, :], sinq_ref[sl, :], half) * scale
            q2 = _stack_heads(q, MLA_HB, MLA_HL).astype(BF16)
            s = lax.dot_general(q2, kb_ref[...], _NT, preferred_element_type=F32)
            (p,), inv = _softmax_parts([s], shift)
            o_ref[sl, :] = _mla_out(_bdot(p.astype(BF16), vb_ref[...]) * inv).astype(BF16)

    _with_score_bound(scale * _rms_norm_bound(gq_ref, MLA_QK, MLA_HL) * kmax_ref[...], chains)


def _mla_latent(qp, kvp, low, kvc, krc, cos, sin, gq, gk):
    nb = MLA_HEADS // MLA_HB
    nq = DEC_SEQ // MLA_TQ
    q0 = N_PROMPT // MLA_TQ
    lat0 = N_PROMPT // DEC_SEQ
    n_keys = PAST_LEN + DEC_SEQ
    tab = pl.BlockSpec((DEC_SEQ, MLA_LANES), lambda b, j, t: (0, 0))
    tabq = pl.BlockSpec((MLA_TQ, MLA_LANES), lambda b, j, t: (t, 0))
    vec = pl.BlockSpec((1, MLA_LANES), lambda b, j, t: (0, 0))
    return pl.pallas_call(
        _mla_latent_kernel,
        grid=(DEC_BATCH, nb, nq),
        in_specs=[pl.BlockSpec((MLA_TQ, MLA_LANES), lambda b, j, t: (q0 + b * nq + t, j)),
                  pl.BlockSpec((DEC_SEQ, MLA_LANES), lambda b, j, t: (lat0 + b, j)),
                  pl.BlockSpec((DEC_SEQ, MLA_HL), lambda b, j, t: (lat0 + b, 0)),
                  pl.BlockSpec((PAST_LEN, MLA_LANES), lambda b, j, t: (b, j)),
                  pl.BlockSpec((PAST_LEN, MLA_HL), lambda b, j, t: (b, 0)),
                  tab, tab, tabq, tabq, vec, vec],
        out_specs=pl.BlockSpec((MLA_TQ, MLA_HB * MLA_V), lambda b, j, t: (b * nq + t, j)),
        out_shape=jax.ShapeDtypeStruct((N_LATENT, MLA_HEADS * MLA_V), BF16),
        scratch_shapes=[pltpu.VMEM((n_keys, MLA_LANES), BF16),
                        pltpu.VMEM((n_keys, MLA_LANES), BF16),
                        pltpu.VMEM((1, MLA_HB), F32)],
        compiler_params=_params(("arbitrary", "arbitrary", "arbitrary")),
        name="mla_latent",
    )(qp, kvp, low, kvc, krc, cos, sin, cos, sin, gq, gk)


MLA_LOW_Q = 0
MLA_LOW_KV = 512
MLA_LOW_KR = 768
MLA_LOW_N = 896


def _axial_tables(n_tok, rdim):
    nf = rdim // 4
    freqs = ROPE_BASE ** (-jnp.arange(nf, dtype=F32) / nf)
    t = jnp.arange(n_tok)
    rowp = (t // GRID_W).astype(F32)
    colp = (t % GRID_W).astype(F32)
    ang = jnp.stack([rowp[:, None] * freqs, colp[:, None] * freqs], axis=1)
    cos, sin = jnp.cos(ang), jnp.sin(ang)
    cos_l = jnp.stack([cos, cos], axis=2).reshape(n_tok, rdim)
    sin_l = jnp.stack([-sin, sin], axis=2).reshape(n_tok, rdim)
    return cos_l, sin_l


def _diff_rope_tables():
    cos, sin = _axial_tables(DEC_SEQ, DIFF_HD)
    return jnp.tile(cos, (1, DIFF_HB)), jnp.tile(sin, (1, DIFF_HB))


def _mla_rope_tables():
    cos, sin = _axial_tables(DEC_SEQ, MLA_ROPE)
    ones = jnp.ones((DEC_SEQ, MLA_NOPE), F32)
    pad1 = jnp.ones((DEC_SEQ, MLA_HL - MLA_QK), F32)
    cos_h = jnp.concatenate([ones, cos, pad1], axis=1)
    sin_h = jnp.concatenate([0 * ones, sin, 0 * pad1], axis=1)
    return jnp.tile(cos_h, (1, MLA_HB)), jnp.tile(sin_h, (1, MLA_HB))


def _tokens_first(cache):
    b, h, l, d = cache.shape
    return jnp.transpose(cache, (0, 2, 1, 3)).reshape(b, l, h * d)


def _pad_heads(w, heads, hd, hl):
    k = w.shape[0]
    return jnp.pad(w.reshape(k, heads, hd), ((0, 0), (0, 0), (0, hl - hd))).reshape(k, heads * hl)


def kernel(x_prompt, x_sample, cache_l0_k, cache_l0_v, state_l1_fwd, state_l1_bwd, cache_l2_k,
           cache_l2_v, cache_l3_ckv, cache_l3_krope, c, c_ctx, ada_w, ada_b, norm_mix, norm_ffn,
           ffn_w_up, ffn_conv_w, ffn_conv_b, ffn_w_down, na_w_qkv, na_q_norm, na_k_norm, na_bias,
           na_w_o, gla_w_qkvg, gla_w_gate1, gla_w_gate2, gla_b_gate, gla_o_norm, gla_w_o,
           diff_w_qkv, diff_q_norm, diff_k_norm, diff_lambda, diff_sub_norm, diff_w_o, mla_w_dq,
           mla_q_a_norm, mla_w_uq, mla_w_dkv, mla_kv_a_norm, mla_w_ukv, mla_q_norm, mla_k_norm,
           mla_w_o):
    xr = _Rows(x_prompt.reshape(N_PROMPT, D_MODEL), x_sample.reshape(N_LATENT, D_MODEL), 0)
    cvecs = jnp.concatenate([c_ctx[None], c, jnp.zeros((5, D_MODEL), F32)], axis=0)
    mods_all = _ada_mods(cvecs, ada_w, ada_b)
    halves = lambda o_p, o_s: _Rows(o_p, o_s, 0)

    mods = mods_all[0]
    qkv = _norm_mod_proj(xr, norm_mix[0], mods, [na_w_qkv], "na_qkv")
    o_p, new_l0_k, new_l0_v = _na_prompt(qkv, na_q_norm, na_k_norm)
    o_s = _na_latent(qkv, cache_l0_k, cache_l0_v, _na_bias_blocks(na_bias), na_q_norm, na_k_norm)
    x = _out_proj_residual(xr, halves(o_p, o_s), mods, na_w_o, "na_out")
    ffn_weights = (ffn_w_up, ffn_conv_w, ffn_conv_b, ffn_w_down)
    x = _ffn(x, norm_ffn[0], mods, 0, *ffn_weights)
    xr = _one_array(x)

    mods = mods_all[1]
    w_decay = jnp.concatenate(
        [gla_w_gate1[0], gla_w_gate1[1],
         jnp.zeros((D_MODEL, 128 - 2 * GLA_GATE_RANK), F32)], axis=1)
    proj = _norm_mod_proj(xr, norm_mix[1], mods, [gla_w_qkvg, w_decay], "gla_proj")
    w2 = jnp.zeros((2, 128, GLA_HK), F32)
    w2 = w2.at[0, :GLA_GATE_RANK].set(gla_w_gate2[0])
    w2 = w2.at[1, GLA_GATE_RANK:2 * GLA_GATE_RANK].set(gla_w_gate2[1])
    bg = gla_b_gate.reshape(2, 1, GLA_HK)
    o_p, new_l1_fwd, new_l1_bwd = _gla(proj, w2, bg, gla_o_norm, BATCH, SEQ, 0, hps=4)
    o_s = _gla(proj, w2, bg, gla_o_norm, DEC_BATCH, DEC_SEQ, N_PROMPT // DEC_SEQ, hps=2,
               states=(state_l1_fwd, state_l1_bwd))
    x = _out_proj_residual(xr, halves(o_p, o_s), mods, gla_w_o, "gla_out",
                           gate=proj, gate_col_block=(2 * GLA_HK + GLA_HV) // GLA_HV)
    x = _ffn(x, norm_ffn[1], mods, 1, *ffn_weights)
    xr = _one_array(x)

    mods = mods_all[2]
    qkv = _norm_mod_proj(xr, norm_mix[2], mods, [diff_w_qkv], "diff_qkv")
    o_p, kn_p, new_l2_v = _diff_prompt(qkv, diff_q_norm, diff_k_norm, diff_lambda, diff_sub_norm)
    new_l2_k = kn_p.reshape(BATCH, 2 * DIFF_HEADS, SEQ, DIFF_HD)
    cos_d, sin_d = _diff_rope_tables()
    o_s = _diff_latent(qkv, cache_l2_k, cache_l2_v, cos_d, sin_d, diff_q_norm, diff_k_norm,
                       diff_lambda, diff_sub_norm)
    x = _out_proj_residual(xr, halves(o_p, o_s), mods, diff_w_o, "diff_out")
    x = _ffn(x, norm_ffn[2], mods, 2, *ffn_weights)
    xr = _one_array(x)

    mods = mods_all[3]
    zc = lambda n: jnp.zeros((D_MODEL, n), F32)
    w_low = jnp.concatenate(
        [mla_w_dq, zc(MLA_LOW_KV - MLA_Q_RANK), mla_w_dkv[:, :MLA_KV_RANK],
         zc(MLA_NOPE), mla_w_dkv[:, MLA_KV_RANK:], zc(MLA_HL - MLA_QK)], axis=1)
    w_uq = _pad_heads(mla_w_uq, MLA_HEADS, MLA_QK, MLA_HL)
    qp, kvp, ckv, kr = _mla_front(xr, norm_mix[3], mods, w_low, mla_q_a_norm, w_uq,
                                  mla_kv_a_norm, mla_w_ukv)
    kvc = _matmul(cache_l3_ckv.reshape(DEC_BATCH * PAST_LEN, MLA_KV_RANK), mla_w_ukv,
                  "mla_ukv_cache")
    krc = jnp.pad(cache_l3_krope.reshape(DEC_BATCH * PAST_LEN, MLA_ROPE),
                  ((0, 0), (MLA_NOPE, MLA_HL - MLA_QK)))
    pad_gain = lambda g: jnp.tile(jnp.pad(g, (0, MLA_HL - MLA_QK)), MLA_HB).reshape(1, -1)
    gq, gk = pad_gain(mla_q_norm), pad_gain(mla_k_norm)
    o_p = _mla_prompt(qp, kvp, kr, gq, gk)
    cos_m, sin_m = _mla_rope_tables()
    o_s = _mla_latent(qp, kvp, kr, kvc, krc, cos_m, sin_m, gq, gk)
    new_l3_ckv = ckv[:N_PROMPT].reshape(BATCH, SEQ, MLA_KV_RANK)
    new_l3_krope = kr[:N_PROMPT, MLA_NOPE:MLA_QK].reshape(BATCH, SEQ, MLA_ROPE)
    x = _out_proj_residual(xr, halves(o_p, o_s), mods, mla_w_o, "mla_out")
    n_pt = N_PROMPT // TOK_TILE
    ffn3 = functools.partial(_ffn, x, norm_ffn[3], mods, 3, *ffn_weights)
    y_prompt = ffn3(tile0=0, n_tiles=n_pt).reshape(BATCH, SEQ, D_MODEL)
    y_sample = ffn3(tile0=n_pt, n_tiles=N_TOK_TILES - n_pt).reshape(DEC_BATCH, DEC_SEQ, D_MODEL)
    return (y_prompt, y_sample, new_l0_k, new_l0_v, new_l1_fwd, new_l1_bwd, new_l2_k, new_l2_v,
            new_l3_ckv, new_l3_krope)
```

```python
import functools
import math
from typing import NamedTuple

import jax
import jax.numpy as jnp
from jax import lax
from jax.experimental import pallas as pl
from jax.experimental.pallas import tpu as pltpu

F32 = jnp.float32
BF16 = jnp.bfloat16

D_MODEL = 1024
BATCH = 16
SEQ = 256
DEPTH = 4
DEC_BATCH = 2
DEC_SEQ = 2048
PAST_LEN = 256
GRID_W = 64
D_FF = 2816
EPS = 1e-6
ROPE_BASE = 10000.0

NA_HEADS = 16
NA_HD = 64
NA_WIN_R = 8
NA_WIN_C = 16

GLA_HEADS = 4
GLA_DK = 128
GLA_DV = 256
GLA_HK = GLA_HEADS * GLA_DK
GLA_HV = GLA_HEADS * GLA_DV
GLA_GATE_RANK = 16
GLA_GATE_NORM = 16.0

DIFF_HEADS = 8
DIFF_HD = 64
DIFF_LAMBDA_INIT = 0.8 - 0.6 * math.exp(-0.3 * 2)

MLA_HEADS = 16
MLA_Q_RANK = 384
MLA_KV_RANK = 256
MLA_NOPE = 64
MLA_ROPE = 32
MLA_V = 64
MLA_QK = MLA_NOPE + MLA_ROPE

N_PROMPT = BATCH * SEQ
N_LATENT = DEC_BATCH * DEC_SEQ
N_TOK = N_PROMPT + N_LATENT
TOK_TILE = 2048
N_TOK_TILES = N_TOK // TOK_TILE
FF_CHUNK = 256
N_FF_CHUNKS = D_FF // FF_CHUNK
NEG = -1e30
LOG2E = math.log2(math.e)
SAFE_SCORE_BOUND = 48.0

VMEM_LIMIT = 56 * 1024 * 1024

_NT = (((1,), (1,)), ((), ()))
_TN = (((0,), (0,)), ((), ()))


def _params(sem, vmem=VMEM_LIMIT):
    return pltpu.CompilerParams(dimension_semantics=sem, vmem_limit_bytes=vmem)


def _log2(n):
    assert n & (n - 1) == 0
    return n.bit_length() - 1


def _silu(x):
    return x / (1.0 + jnp.exp(-x))


def _bdot(a, b):
    return jnp.dot(a, b, preferred_element_type=F32)


def _softmax_parts(parts, shift=None):
    m = shift
    if m is None:
        m = parts[0].max(axis=-1, keepdims=True)
        for s in parts[1:]:
            m = jnp.maximum(m, s.max(axis=-1, keepdims=True))
    ps = [jnp.exp2(s - m) for s in parts]
    l = ps[0].sum(axis=-1, keepdims=True)
    for p in ps[1:]:
        l = l + p.sum(axis=-1, keepdims=True)
    return ps, 1.0 / l


def _group_rms(x, gain, group, n_real=None, sums_on_mxu=False):
    lanes = x.shape[-1]
    n_real = n_real or group
    x2 = x * x
    if group == lanes:
        ms = jnp.sum(x2, axis=-1, keepdims=True)
    elif not sums_on_mxu:
        gid = lax.broadcasted_iota(jnp.int32, x.shape, 1) >> _log2(group)
        ms = jnp.zeros_like(x)
        for i in range(lanes // group):
            sel = gid == i
            si = jnp.sum(jnp.where(sel, x2, 0.0), axis=-1, keepdims=True)
            ms = jnp.where(sel, si, ms)
    else:
        r = lax.broadcasted_iota(jnp.int32, (lanes, lanes), 0) >> _log2(group)
        c = lax.broadcasted_iota(jnp.int32, (lanes, lanes), 1) >> _log2(group)
        ones = jnp.where(r == c, 1.0, 0.0).astype(BF16)
        hi = x2.astype(BF16)
        lo = (x2 - hi.astype(F32)).astype(BF16)
        ms = _bdot(hi, ones) + _bdot(lo, ones)
    return x * lax.rsqrt(ms * (1.0 / n_real) + EPS) * gain


def _with_score_bound(bound, body):
    ok = bound.max() < SAFE_SCORE_BOUND
    pl.when(ok)(lambda: body(bound))
    pl.when(jnp.logical_not(ok))(lambda: body(None))


def _rms_norm_bound(g_ref, n, head_lanes):
    g = jnp.abs(g_ref[...])
    heads = g.shape[-1] // head_lanes
    return n ** 0.5 * jnp.concatenate(
        [g[:, h * head_lanes:(h + 1) * head_lanes].max(axis=-1, keepdims=True)
         for h in range(heads)], axis=-1)


def _stacked_rows(per_head, rows):
    r = lax.broadcasted_iota(jnp.int32, (per_head.shape[-1] * rows, 1), 0)
    out = per_head[:, 0:1]
    for h in range(1, per_head.shape[-1]):
        out = jnp.where(r >= h * rows, per_head[:, h:h + 1], out)
    return out


def _max_head_norms(x, head_lanes):
    hid = lax.broadcasted_iota(jnp.int32, x.shape, 1) >> _log2(head_lanes)
    x2 = x * x
    sq = [jnp.sum(jnp.where(hid == h, x2, 0.0), axis=-1, keepdims=True).max(axis=0, keepdims=True)
          for h in range(x.shape[-1] // head_lanes)]
    return jnp.sqrt(jnp.concatenate(sq, axis=-1))


def _rope(x, cos, sin, half):
    lanes = x.shape[-1]
    lane = lax.broadcasted_iota(jnp.int32, x.shape, 1)
    up = pltpu.roll(x, lanes - half, axis=1)
    dn = pltpu.roll(x, half, axis=1)
    swapped = jnp.where((lane & (2 * half - 1)) < half, up, dn)
    return x * cos + swapped * sin


def _stack_heads(q, n_heads, head_lanes):
    hid = lax.broadcasted_iota(jnp.int32, q.shape, 1) >> _log2(head_lanes)
    zero = jnp.zeros_like(q)
    return jnp.concatenate([jnp.where(hid == i, q, zero) for i in range(n_heads)], axis=0)


def _unstack_heads(o, n_heads, head_lanes):
    rows = o.shape[0] // n_heads
    hid = lax.broadcasted_iota(jnp.int32, (rows, o.shape[1]), 1) >> _log2(head_lanes)
    out = o[0:rows]
    for i in range(1, n_heads):
        out = jnp.where(hid == i, o[i * rows:(i + 1) * rows], out)
    return out


ADA_TN = 1536


def _ada_kernel(c_ref, w_ref, b_ref, o_ref):
    s = _silu(c_ref[...])
    w = w_ref[0]
    s_hi, w_hi = s.astype(BF16), w.astype(BF16)
    s_lo = (s - s_hi.astype(F32)).astype(BF16)
    w_lo = (w - w_hi.astype(F32)).astype(BF16)
    both = _bdot(jnp.concatenate([s_hi, s_lo], axis=0), w_hi)
    o_ref[0] = both[0:8] + both[8:16] + _bdot(s_hi, w_lo) + b_ref[0]


def _ada_mods(cvecs, ada_w, ada_b):
    out = pl.pallas_call(
        _ada_kernel,
        grid=(DEPTH, 6 * D_MODEL // ADA_TN),
        in_specs=[pl.BlockSpec((8, D_MODEL), lambda l, j: (0, 0)),
                  pl.BlockSpec((1, D_MODEL, ADA_TN), lambda l, j: (l, 0, j)),
                  pl.BlockSpec((1, 1, ADA_TN), lambda l, j: (l, 0, j))],
        out_specs=pl.BlockSpec((1, 8, ADA_TN), lambda l, j: (l, 0, j)),
        out_shape=jax.ShapeDtypeStruct((DEPTH, 8, 6 * D_MODEL), F32),
        compiler_params=_params(("arbitrary", "arbitrary")),
        name="ada_mod",
    )(cvecs, ada_w, ada_b.reshape(DEPTH, 1, 6 * D_MODEL))
    return out.reshape(DEPTH, 8, 6, D_MODEL)[:, :3]


def _mod_group_of_tile(i):
    return jnp.maximum(i - (N_PROMPT // TOK_TILE - 1), 0)


def _norm_mod_rows(x_ref, g_ref, mod_ref, h_ref, shift_idx, scale_idx, rows=64):
    g = g_ref[...]
    sc = 1.0 + mod_ref[0, scale_idx:scale_idx + 1, :]
    sh = mod_ref[0, shift_idx:shift_idx + 1, :]

    def body(r, carry):
        sl = pl.ds(pl.multiple_of(r * rows, rows), rows)
        xf = x_ref[sl, :]
        ms = jnp.mean(xf * xf, axis=-1, keepdims=True)
        y = xf * lax.rsqrt(ms + EPS) * g
        h_ref[sl, :] = (y * sc + sh).astype(BF16)
        return carry

    lax.fori_loop(0, x_ref.shape[0] // rows, body, 0, unroll=4)


ROW_TM = 512


class _Rows(NamedTuple):
    prompt: jax.Array
    latent: jax.Array
    latent_row0: int


def _one_array(x):
    return _Rows(x, x, N_PROMPT)


def _row_specs(rows, width, col_block=0):
    n_p = N_PROMPT // ROW_TM
    l0 = rows.latent_row0 // ROW_TM
    return [pl.BlockSpec((ROW_TM, width), lambda t: (jnp.minimum(t, n_p - 1), col_block)),
            pl.BlockSpec((ROW_TM, width), lambda t: (l0 + jnp.maximum(t - n_p, 0), col_block))]


def _row_group(t):
    first_latent = N_PROMPT // ROW_TM
    return jnp.where(t < first_latent, 0, 1 + (t - first_latent) // (DEC_SEQ // ROW_TM))


def _is_prompt_tile():
    return pl.program_id(0) < N_PROMPT // ROW_TM


def _proj_kernel(xp_ref, xl_ref, g_ref, mod_ref, *refs):
    *w_refs, o_ref, h_ref, wb_ref = refs

    @pl.when(pl.program_id(0) == 0)
    def _():
        off = 0
        for w_ref in w_refs:
            wb_ref[:, off:off + w_ref.shape[1]] = w_ref[...].astype(BF16)
            off += w_ref.shape[1]

    is_prompt = _is_prompt_tile()
    g = g_ref[...]
    sc = 1.0 + mod_ref[0, 1:2, :]
    sh = mod_ref[0, 0:1, :]
    rows = 64
    part = ROW_TM // 2
    for p0 in range(0, ROW_TM, part):
        for r0 in range(p0, p0 + part, rows):
            sl = slice(r0, r0 + rows)
            xf = jnp.where(is_prompt, xp_ref[sl, :], xl_ref[sl, :])
            ms = jnp.mean(xf * xf, axis=-1, keepdims=True)
            y = xf * lax.rsqrt(ms + EPS) * g
            h_ref[sl, :] = (y * sc + sh).astype(BF16)
        o_ref[p0:p0 + part, :] = _bdot(h_ref[p0:p0 + part, :], wb_ref[...])


def _norm_mod_proj(x, g, mods, ws, name):
    n = sum(w.shape[1] for w in ws)
    return pl.pallas_call(
        _proj_kernel,
        grid=(N_TOK // ROW_TM,),
        in_specs=_row_specs(x, D_MODEL) + [
            pl.BlockSpec((1, D_MODEL), lambda t: (0, 0)),
            pl.BlockSpec((1, 6, D_MODEL), lambda t: (_row_group(t), 0, 0))] + [
            pl.BlockSpec(w.shape, lambda t: (0, 0), pipeline_mode=pl.Buffered(1)) for w in ws],
        out_specs=pl.BlockSpec((ROW_TM, n), lambda t: (t, 0)),
        out_shape=jax.ShapeDtypeStruct((N_TOK, n), F32),
        scratch_shapes=[pltpu.VMEM((ROW_TM, D_MODEL), BF16),
                        pltpu.VMEM((D_MODEL, n), BF16)],
        compiler_params=_params(("arbitrary",)),
        name=name,
    )(x.prompt, x.latent, g.reshape(1, D_MODEL), mods, *ws)


def _rms(a, g):
    return a * lax.rsqrt(jnp.mean(a * a, axis=-1, keepdims=True) + EPS) * g


def _mla_front_kernel(xp_ref, xl_ref, g_ref, mod_ref, wl_ref, gqa_ref, wuq_ref, gkv_ref, wukv_ref,
                      qp_ref, kvp_ref, ckv_ref, kr_ref, h_ref, wlb_ref, wuqb_ref, wukvb_ref):
    @pl.when(pl.program_id(0) == 0)
    def _():
        wlb_ref[...] = wl_ref[...].astype(BF16)
        wuqb_ref[...] = wuq_ref[...].astype(BF16)
        wukvb_ref[...] = wukv_ref[...].astype(BF16)

    is_prompt = _is_prompt_tile()
    g = g_ref[...]
    sc = 1.0 + mod_ref[0, 1:2, :]
    sh = mod_ref[0, 0:1, :]
    rows = 64
    part = ROW_TM // 2
    for p0 in range(0, ROW_TM, part):
        for r0 in range(p0, p0 + part, rows):
            sl = slice(r0, r0 + rows)
            xf = jnp.where(is_prompt, xp_ref[sl, :], xl_ref[sl, :])
            h_ref[sl, :] = (_rms(xf, g) * sc + sh).astype(BF16)
        ps = slice(p0, p0 + part)
        low = _bdot(h_ref[ps, :], wlb_ref[...])
        qa = _rms(low[:, MLA_LOW_Q:MLA_LOW_Q + MLA_Q_RANK], gqa_ref[...])
        qp_ref[ps, :] = _bdot(qa.astype(BF16), wuqb_ref[...])
        ckv = _rms(low[:, MLA_LOW_KV:MLA_LOW_KV + MLA_KV_RANK], gkv_ref[...])
        ckv_ref[ps, :] = ckv
        kvp_ref[ps, :] = _bdot(ckv.astype(BF16), wukvb_ref[...])
        kr_ref[ps, :] = low[:, MLA_LOW_KR:MLA_LOW_KR + MLA_HL]


def _mla_front(x, g, mods, w_low, g_qa, w_uq, g_kva, w_ukv):
    n_q, n_kv = w_uq.shape[1], w_ukv.shape[1]
    const = lambda shape: pl.BlockSpec(shape, lambda t: (0, 0), pipeline_mode=pl.Buffered(1))
    out = lambda n: pl.BlockSpec((ROW_TM, n), lambda t: (t, 0))
    shape = lambda n: jax.ShapeDtypeStruct((N_TOK, n), F32)
    return pl.pallas_call(
        _mla_front_kernel,
        grid=(N_TOK // ROW_TM,),
        in_specs=_row_specs(x, D_MODEL) + [
            pl.BlockSpec((1, D_MODEL), lambda t: (0, 0)),
            pl.BlockSpec((1, 6, D_MODEL), lambda t: (_row_group(t), 0, 0)),
            const(w_low.shape), const((1, MLA_Q_RANK)), const(w_uq.shape),
            const((1, MLA_KV_RANK)), const(w_ukv.shape)],
        out_specs=[out(n_q), out(n_kv), out(MLA_KV_RANK), out(MLA_HL)],
        out_shape=[shape(n_q), shape(n_kv), shape(MLA_KV_RANK), shape(MLA_HL)],
        scratch_shapes=[pltpu.VMEM((ROW_TM, D_MODEL), BF16),
                        pltpu.VMEM(w_low.shape, BF16),
                        pltpu.VMEM(w_uq.shape, BF16),
                        pltpu.VMEM(w_ukv.shape, BF16)],
        compiler_params=_params(("arbitrary",)),
        name="mla_front",
    )(x.prompt, x.latent, g.reshape(1, D_MODEL), mods, w_low, g_qa.reshape(1, -1), w_uq,
      g_kva.reshape(1, -1), w_ukv)


def _matmul_kernel(a_ref, w_ref, o_ref):
    o_ref[...] = _bdot(a_ref[...].astype(BF16), w_ref[...].astype(BF16))


def _matmul(a, w, name):
    rows, n = a.shape[0], w.shape[1]
    return pl.pallas_call(
        _matmul_kernel,
        out_shape=jax.ShapeDtypeStruct((rows, n), F32),
        compiler_params=_params(()),
        name=name,
    )(a, w)


def _oproj_kernel(*refs, gated):
    if gated:
        xp_ref, xl_ref, ap_ref, al_ref, g_ref, mod_ref, w_ref, o_ref, wb_ref = refs
    else:
        xp_ref, xl_ref, ap_ref, al_ref, mod_ref, w_ref, o_ref, wb_ref = refs

    @pl.when(pl.program_id(0) == 0)
    def _():
        wb_ref[...] = w_ref[...].astype(BF16)

    is_prompt = _is_prompt_tile()
    a = jnp.where(is_prompt, ap_ref[...], al_ref[...])
    if gated:
        a = a * _silu(g_ref[...])
    y = _bdot(a.astype(BF16), wb_ref[...])
    x = jnp.where(is_prompt, xp_ref[...], xl_ref[...])
    o_ref[...] = x + mod_ref[0, 2:3, :] * y


def _out_proj_residual(x, a, mods, w, name, gate=None, gate_col_block=0):
    k = w.shape[0]
    in_specs = _row_specs(x, D_MODEL) + _row_specs(a, k)
    args = [x.prompt, x.latent, a.prompt, a.latent]
    if gate is not None:
        in_specs.append(pl.BlockSpec((ROW_TM, k), lambda t: (t, gate_col_block)))
        args.append(gate)
    in_specs += [pl.BlockSpec((1, 6, D_MODEL), lambda t: (_row_group(t), 0, 0)),
                 pl.BlockSpec((k, D_MODEL), lambda t: (0, 0))]
    args += [mods, w]
    return pl.pallas_call(
        functools.partial(_oproj_kernel, gated=gate is not None),
        grid=(N_TOK // ROW_TM,),
        in_specs=in_specs,
        out_specs=pl.BlockSpec((ROW_TM, D_MODEL), lambda t: (t, 0)),
        out_shape=jax.ShapeDtypeStruct((N_TOK, D_MODEL), F32),
        scratch_shapes=[pltpu.VMEM((k, D_MODEL), BF16)],
        compiler_params=_params(("arbitrary",)),
        name=name,
    )(*args)


FFN_MM_ROWS = 512
FFN_ROWS = 64
FFN_PAD = 8


def _ffn_kernel(x_ref, g_ref, mod_ref, wg_ref, wv_ref, cwg_ref, cwv_ref, cbg_ref, cbv_ref,
                wd_ref, o_ref, h_ref, u_ref, act_ref, wup_ref, wdn_ref, *, tile0):
    i = tile0 + pl.program_id(0)
    c = pl.program_id(1)
    fc = FF_CHUNK
    n = TOK_TILE // FFN_MM_ROWS
    seq_len = jnp.where(i < N_PROMPT // TOK_TILE, SEQ, DEC_SEQ)
    row = lax.broadcasted_iota(jnp.int32, (FFN_ROWS, 1), 0)
    taps = lambda cw_ref, cb_ref: [jnp.broadcast_to(cw_ref[j:j + 1, :], (FFN_ROWS, fc))
                                   for j in range(3)] + [
                                       jnp.broadcast_to(cb_ref[...], (FFN_ROWS, fc))]
    taps_g, taps_v = taps(cwg_ref, cbg_ref), taps(cwv_ref, cbv_ref)

    def up(u_ref, t):
        r0 = t * FFN_MM_ROWS
        u_ref[FFN_PAD + r0:FFN_PAD + r0 + FFN_MM_ROWS, :] = _bdot(
            h_ref[r0:r0 + FFN_MM_ROWS, :], wup_ref[...])

    def conv_act(u_ref, t):
        for r0 in range(t * FFN_MM_ROWS, (t + 1) * FFN_MM_ROWS, FFN_ROWS):
            halves = []
            for lo, (w0, w1, w2, bias) in ((0, taps_g), (fc, taps_v)):
                p0 = FFN_PAD + r0
                prev = u_ref[p0 - 1:p0 - 1 + FFN_ROWS, lo:lo + fc]
                mid = u_ref[p0:p0 + FFN_ROWS, lo:lo + fc]
                nxt = u_ref[p0 + 1:p0 + 1 + FFN_ROWS, lo:lo + fc]
                if r0 % SEQ == 0:
                    prev = jnp.where(((r0 + row) & (seq_len - 1)) == 0, 0.0, prev)
                if (r0 + FFN_ROWS) % SEQ == 0:
                    nxt = jnp.where(((r0 + row) & (seq_len - 1)) == seq_len - 1, 0.0, nxt)
                halves.append(prev * w0 + mid * w1 + nxt * w2 + bias)
            act_ref[r0:r0 + FFN_ROWS, :] = (_silu(halves[0]) * halves[1]).astype(BF16)

    def down(t):
        r0 = t * FFN_MM_ROWS
        o_ref[r0:r0 + FFN_MM_ROWS, :] += _bdot(act_ref[r0:r0 + FFN_MM_ROWS, :], wdn_ref[...])

    @pl.when(c == 0)
    def _():
        _norm_mod_rows(x_ref, g_ref, mod_ref, h_ref, 3, 4)
        zeros = jnp.zeros((FFN_PAD, 2 * fc), F32)
        u_ref[0:FFN_PAD, :] = zeros
        u_ref[FFN_PAD + TOK_TILE:, :] = zeros
        o_ref[...] = jnp.zeros_like(o_ref)

    wup_ref[:, :fc] = wg_ref[...].astype(BF16)
    wup_ref[:, fc:] = wv_ref[...].astype(BF16)
    wdn_ref[...] = wd_ref[...].astype(BF16)
    for s in range(n + 2):
        if s < n:
            up(u_ref, s)
        if 1 <= s <= n:
            conv_act(u_ref, s - 1)
        if s >= 2:
            down(s - 2)

    @pl.when(c == N_FF_CHUNKS - 1)
    def _():
        o_ref[...] = x_ref[...] + mod_ref[0, 5:6, :] * o_ref[...]


def _ffn(x, g, mods, layer, w_up, conv_w, conv_b, w_down, tile0=0, n_tiles=N_TOK_TILES):
    fc = FF_CHUNK
    ncb = N_FF_CHUNKS
    return pl.pallas_call(
        functools.partial(_ffn_kernel, tile0=tile0),
        grid=(n_tiles, ncb),
        in_specs=[pl.BlockSpec((TOK_TILE, D_MODEL), lambda i, c: (tile0 + i, 0)),
                  pl.BlockSpec((1, D_MODEL), lambda i, c: (0, 0)),
                  pl.BlockSpec((1, 6, D_MODEL), lambda i, c: (_mod_group_of_tile(tile0 + i), 0, 0)),
                  pl.BlockSpec((None, D_MODEL, fc), lambda i, c: (layer, 0, c)),
                  pl.BlockSpec((None, D_MODEL, fc), lambda i, c: (layer, 0, ncb + c)),
                  pl.BlockSpec((None, 3, fc), lambda i, c: (layer, 0, c)),
                  pl.BlockSpec((None, 3, fc), lambda i, c: (layer, 0, ncb + c)),
                  pl.BlockSpec((None, 1, fc), lambda i, c: (layer, 0, c)),
                  pl.BlockSpec((None, 1, fc), lambda i, c: (layer, 0, ncb + c)),
                  pl.BlockSpec((None, fc, D_MODEL), lambda i, c: (layer, c, 0))],
        out_specs=pl.BlockSpec((TOK_TILE, D_MODEL), lambda i, c: (i, 0)),
        out_shape=jax.ShapeDtypeStruct((n_tiles * TOK_TILE, D_MODEL), F32),
        scratch_shapes=[pltpu.VMEM((TOK_TILE, D_MODEL), BF16),
                        pltpu.VMEM((TOK_TILE + 2 * FFN_PAD, 2 * fc), F32),
                        pltpu.VMEM((TOK_TILE, fc), BF16),
                        pltpu.VMEM((D_MODEL, 2 * fc), BF16),
                        pltpu.VMEM((fc, D_MODEL), BF16)],
        compiler_params=_params(("arbitrary", "arbitrary")),
        name="conv_ffn",
    )(x, g.reshape(1, D_MODEL), mods, w_up, w_up, conv_w, conv_w,
      conv_b.reshape(DEPTH, 1, -1), conv_b.reshape(DEPTH, 1, -1), w_down)


NA_HB = 4
NA_LANES = NA_HB * NA_HD
NA_ROWS = DEC_SEQ // GRID_W
NA_KEYS = NA_WIN_R * GRID_W
PROMPT_SEQS = 4
NA_ROW_UNROLL = 16


def _store_heads(dst_ref, seq, x, n_heads, hd):
    for h in range(n_heads):
        dst_ref[seq, h] = x[:, h * hd:(h + 1) * hd]


def _na_prompt_kernel(q_ref, k_ref, v_ref, gq_ref, gk_ref, o_ref, kn_ref, vn_ref):
    scale = NA_HD ** -0.5 * LOG2E

    def chains(bound):
        shift = None if bound is None else _stacked_rows(bound, SEQ)
        for seq in range(PROMPT_SEQS):
            sl = slice(seq * SEQ, (seq + 1) * SEQ)
            q = _group_rms(q_ref[sl, :], gq_ref[...], NA_HD, sums_on_mxu=True) * scale
            k = _group_rms(k_ref[sl, :], gk_ref[...], NA_HD, sums_on_mxu=True)
            v = v_ref[sl, :]
            _store_heads(kn_ref, seq, k, NA_HB, NA_HD)
            _store_heads(vn_ref, seq, v, NA_HB, NA_HD)
            q4 = _stack_heads(q, NA_HB, NA_HD).astype(BF16)
            s = lax.dot_general(q4, k.astype(BF16), _NT, preferred_element_type=F32)
            (p,), inv = _softmax_parts([s], shift)
            o4 = _bdot(p.astype(BF16), v.astype(BF16)) * inv
            o_ref[sl, :] = _unstack_heads(o4, NA_HB, NA_HD).astype(BF16)

    _with_score_bound(scale * _rms_norm_bound(gq_ref, NA_HD, NA_HD)
                      * _rms_norm_bound(gk_ref, NA_HD, NA_HD), chains)


def _na_prompt(qkv, gq, gk):
    nb = NA_HEADS // NA_HB
    rows = PROMPT_SEQS * SEQ
    blk = lambda off: pl.BlockSpec((rows, NA_LANES), lambda b, j: (b, off + j))
    vec = pl.BlockSpec((1, NA_LANES), lambda b, j: (0, 0))
    cache = pl.BlockSpec((PROMPT_SEQS, NA_HB, SEQ, NA_HD), lambda b, j: (b, j, 0, 0))
    cache_shape = jax.ShapeDtypeStruct((BATCH, NA_HEADS, SEQ, NA_HD), F32)
    return pl.pallas_call(
        _na_prompt_kernel,
        grid=(BATCH // PROMPT_SEQS, nb),
        in_specs=[blk(0), blk(nb), blk(2 * nb), vec, vec],
        out_specs=[blk(0), cache, cache],
        out_shape=[jax.ShapeDtypeStruct((N_PROMPT, D_MODEL), BF16), cache_shape, cache_shape],
        compiler_params=_params(("arbitrary", "arbitrary")),
        name="na_prompt",
    )(qkv, qkv, qkv, jnp.tile(gq, NA_HB).reshape(1, -1), jnp.tile(gk, NA_HB).reshape(1, -1))


def _na_latent_kernel(q_ref, k_ref, v_ref, kc_ref, vc_ref, t_ref, gq_ref, gk_ref, o_ref,
                      qn_ref, kn_ref, vb_ref, kc4_ref, vc4_ref, bias_ref):
    scale = NA_HD ** -0.5 * LOG2E
    rows = 256

    def prep(r, carry):
        sl = pl.ds(pl.multiple_of(r * rows, rows), rows)
        qn_ref[sl, :] = (_group_rms(q_ref[sl, :], gq_ref[...], NA_HD, sums_on_mxu=True)
                         * scale).astype(BF16)
        kn_ref[sl, :] = _group_rms(k_ref[sl, :], gk_ref[...], NA_HD,
                                   sums_on_mxu=True).astype(BF16)
        vb_ref[sl, :] = v_ref[sl, :].astype(BF16)
        return carry

    lax.fori_loop(0, DEC_SEQ // rows, prep, 0, unroll=2)
    kc = kc_ref[0]
    kc4_ref[...] = kc.astype(BF16)
    vc4_ref[...] = vc_ref[0].astype(BF16)

    def attend(bound):
        for h in range(NA_HB):
            off = 0.0 if bound is None else bound[:, h:h + 1]
            for p in range(NA_WIN_R):
                for i in range(NA_WIN_R):
                    bias_ref[h, p, :, i * GRID_W:(i + 1) * GRID_W] = t_ref[h, p + i] - off
        shift = None if bound is None else _stacked_rows(bound, GRID_W)

        def row(r):
            kr0 = jnp.clip(r - NA_WIN_R // 2, 0, NA_ROWS - NA_WIN_R)
            pat = kr0 - r + NA_WIN_R - 1
            qs = pl.ds(pl.multiple_of(r * GRID_W, GRID_W), GRID_W)
            ks = pl.ds(pl.multiple_of(kr0 * GRID_W, GRID_W), NA_KEYS)
            q4 = _stack_heads(qn_ref[qs, :], NA_HB, NA_HD)
            s_loc = lax.dot_general(q4, kn_ref[ks, :], _NT, preferred_element_type=F32)
            s_loc = s_loc + jnp.concatenate([bias_ref[h, pat] for h in range(NA_HB)], axis=0)
            s_ctx = lax.dot_general(q4, kc4_ref[...], _NT, preferred_element_type=F32)
            if shift is None:
                (p_loc, p_ctx), inv = _softmax_parts([s_loc, s_ctx])
            else:
                p_loc, p_ctx = jnp.exp2(s_loc), jnp.exp2(s_ctx - shift)
                inv = 1.0 / (p_loc.sum(axis=-1, keepdims=True) + p_ctx.sum(axis=-1, keepdims=True))
            o4 = _bdot(p_loc.astype(BF16), vb_ref[ks, :]) + _bdot(p_ctx.astype(BF16), vc4_ref[...])
            o_ref[qs, :] = _unstack_heads(o4 * inv, NA_HB, NA_HD).astype(BF16)

        def rows_step(i, carry):
            for u in range(NA_ROW_UNROLL):
                row(i * NA_ROW_UNROLL + u)
            return carry

        lax.fori_loop(0, NA_ROWS // NA_ROW_UNROLL, rows_step, 0)

    qmax = scale * _rms_norm_bound(gq_ref, NA_HD, NA_HD)
    bias_max = jnp.concatenate(
        [t_ref[h].max(axis=0).max(axis=0, keepdims=True).max(axis=1, keepdims=True)
         for h in range(NA_HB)], axis=-1)
    _with_score_bound(
        jnp.maximum(qmax * _rms_norm_bound(gk_ref, NA_HD, NA_HD) + bias_max,
                    qmax * _max_head_norms(kc, NA_HD)), attend)


def _na_bias_blocks(bias_table):
    qc = jnp.arange(GRID_W)[:, None]
    kc = jnp.arange(GRID_W)[None, :]
    win0 = jnp.clip(qc - NA_WIN_C // 2, 0, GRID_W - NA_WIN_C)
    valid = (kc >= win0) & (kc < win0 + NA_WIN_C)
    n_co = bias_table.shape[-1]
    onehot = (kc - qc + NA_WIN_C - 1)[None] == jnp.arange(n_co)[:, None, None]
    t = jnp.einsum('hrd,dqk->hrqk', bias_table.astype(F32), onehot.astype(F32),
                   precision=lax.Precision.HIGHEST)
    return jnp.where(valid, t * LOG2E, NEG)


def _na_latent(qkv, cache_k, cache_v, bias_blocks, gq, gk):
    nb = NA_HEADS // NA_HB
    lat0 = N_PROMPT // DEC_SEQ
    blk = lambda off: pl.BlockSpec((DEC_SEQ, NA_LANES), lambda b, j: (lat0 + b, off + j))
    vec = pl.BlockSpec((1, NA_LANES), lambda b, j: (0, 0))
    cache = pl.BlockSpec((1, PAST_LEN, NA_LANES), lambda b, j: (b, 0, j))
    return pl.pallas_call(
        _na_latent_kernel,
        grid=(DEC_BATCH, nb),
        in_specs=[blk(0), blk(nb), blk(2 * nb), cache, cache,
                  pl.BlockSpec((NA_HB, 2 * NA_WIN_R - 1, GRID_W, GRID_W), lambda b, j: (j, 0, 0, 0)),
                  vec, vec],
        out_specs=pl.BlockSpec((DEC_SEQ, NA_LANES), lambda b, j: (b, j)),
        out_shape=jax.ShapeDtypeStruct((N_LATENT, D_MODEL), BF16),
        scratch_shapes=[pltpu.VMEM((DEC_SEQ, NA_LANES), BF16),
                        pltpu.VMEM((DEC_SEQ, NA_LANES), BF16),
                        pltpu.VMEM((DEC_SEQ, NA_LANES), BF16),
                        pltpu.VMEM((PAST_LEN, NA_LANES), BF16),
                        pltpu.VMEM((PAST_LEN, NA_LANES), BF16),
                        pltpu.VMEM((NA_HB, NA_WIN_R, GRID_W, NA_KEYS), F32)],
        compiler_params=_params(("arbitrary", "arbitrary")),
        name="na_latent",
    )(qkv, qkv, qkv, _tokens_first(cache_k), _tokens_first(cache_v), bias_blocks,
      jnp.tile(gq, NA_HB).reshape(1, -1), jnp.tile(gk, NA_HB).reshape(1, -1))


GLA_C = 128
GLA_SUB = 8
GLA_LEVELS = (64, 32, 16, 8)


def _split_hi_lo(x):
    hi = x.astype(BF16)
    lo = (x - hi.astype(F32)).astype(BF16)
    return jnp.concatenate([hi, lo], axis=1)


class _GlaMasks(NamedTuple):
    tri: jax.Array
    later: tuple
    sign: tuple
    pair: tuple
    diag: jax.Array


def _gla_masks(rev):
    c = GLA_C
    row = lax.broadcasted_iota(jnp.int32, (c, c), 0)
    col = lax.broadcasted_iota(jnp.int32, (c, c), 1)
    rid = lax.broadcasted_iota(jnp.int32, (c, GLA_DK), 0)
    causal = (col >= row) if rev else (col <= row)
    later, pair = [], []
    for m in GLA_LEVELS:
        later.append(((rid & m) == 0) if rev else ((rid & m) != 0))
        same = (row >> _log2(2 * m)) == (col >> _log2(2 * m))
        crossing = ((row & m) != (col & m))
        pair.append(same & crossing & causal)
    diag = ((row >> _log2(GLA_SUB)) == (col >> _log2(GLA_SUB))) & causal
    sign = tuple(jnp.where(l, 1.0, -1.0) for l in later)
    return _GlaMasks(jnp.where(causal, 1.0, 0.0).astype(BF16), tuple(later), sign, tuple(pair), diag)


def _gla_chunk(q, k, v, g, st_ref, rev, masks):
    c = GLA_C
    cs = _bdot(masks.tri, _split_hi_lo(g))
    b = cs[:, :GLA_DK] + cs[:, GLA_DK:]

    a = None
    for m, later, sign, pair in zip(GLA_LEVELS, masks.later, masks.sign, masks.pair):
        nblk = c // (2 * m)
        if rev:
            bnd = [b[j * 2 * m + m:j * 2 * m + m + 1] for j in range(nblk)]
        else:
            bnd = [b[j * 2 * m + m - 1:j * 2 * m + m] for j in range(nblk)]
        ref = jnp.concatenate([jnp.broadcast_to(x, (2 * m, GLA_DK)) for x in bnd], axis=0)
        x = (jnp.where(later, q, k) * jnp.exp((b - ref) * sign)).astype(BF16)
        blk = lax.dot_general(x, x, _NT, preferred_element_type=F32)
        a = jnp.where(pair, blk, 0.0 if a is None else a)

    nsub = c // GLA_SUB
    lane_c = lax.broadcasted_iota(jnp.int32, (GLA_SUB, c), 1)
    diag_rows = []
    for blk_i in range(nsub):
        r0 = blk_i * GLA_SUB
        qb = q[r0:r0 + GLA_SUB]
        bb = b[r0:r0 + GLA_SUB]
        acc = jnp.zeros((GLA_SUB, c), F32)
        for s in range(GLA_SUB):
            ks = k[r0 + s:r0 + s + 1]
            bs = b[r0 + s:r0 + s + 1]
            w = jnp.sum(qb * ks * jnp.exp(bb - bs), axis=-1, keepdims=True)
            acc = jnp.where(lane_c == r0 + s, w, acc)
        diag_rows.append(acc)
    a = jnp.where(masks.diag, jnp.concatenate(diag_rows, axis=0), a)

    st = st_ref[...]
    inter = lax.dot_general((q * jnp.exp(b)).astype(BF16), st.astype(BF16), _NT,
                            preferred_element_type=F32)
    o = inter + _bdot(a.astype(BF16), v.astype(BF16))

    btot = b[0:1] if rev else b[c - 1:c]
    kd = (k * jnp.exp(btot - b)).astype(BF16)
    st_ref[...] = st * jnp.exp(btot) + lax.dot_general(v.astype(BF16), kd, _TN,
                                                       preferred_element_type=F32)
    return o


def _gla_kernel(*refs, n_tok, has_state, hps):
    if has_state:
        (q_ref, k_ref, v_ref, r_ref, w2_ref, bg_ref, gn_ref, s0f_ref, s0b_ref,
         o_ref, lg_ref, of_ref, ob_ref, stf_ref, stb_ref) = refs
    else:
        (q_ref, k_ref, v_ref, r_ref, w2_ref, bg_ref, gn_ref,
         o_ref, sf_ref, sb_ref, lg_ref, of_ref, ob_ref, stf_ref, stb_ref) = refs
    nc = n_tok // GLA_C
    scale = GLA_DK ** -0.5

    rb = r_ref[...].astype(BF16)
    for z in range(2):
        x = _bdot(rb, w2_ref[z].astype(BF16)) + bg_ref[z]
        lg_ref[z] = (jnp.minimum(x, 0.0) - jnp.log1p(jnp.exp(-jnp.abs(x)))) * (1.0 / GLA_GATE_NORM)

    for hh in range(hps):
        if has_state:
            stf_ref[hh] = s0f_ref[0, hh].T
            stb_ref[hh] = s0b_ref[0, hh].T
        else:
            stf_ref[hh] = jnp.zeros((GLA_DV, GLA_DK), F32)
            stb_ref[hh] = jnp.zeros((GLA_DV, GLA_DK), F32)

    masks = {rev: _gla_masks(rev) for rev in (False, True)}

    def step(ci, carry):
        for hh in range(hps):
            kq = slice(hh * GLA_DK, (hh + 1) * GLA_DK)
            vv = slice(hh * GLA_DV, (hh + 1) * GLA_DV)
            for rev in (False, True):
                cc = (nc - 1 - ci) if rev else ci
                sl = pl.ds(pl.multiple_of(cc * GLA_C, GLA_C), GLA_C)
                o = _gla_chunk(q_ref[sl, kq] * scale, k_ref[sl, kq], v_ref[sl, vv],
                               lg_ref[1 if rev else 0, sl, kq],
                               (stb_ref if rev else stf_ref).at[hh], rev, masks[rev])
                (ob_ref if rev else of_ref)[sl, vv] = o
        return carry

    lax.fori_loop(0, nc, step, 0)

    for hh in range(hps):
        vv = slice(hh * GLA_DV, (hh + 1) * GLA_DV)
        o = of_ref[:, vv] + ob_ref[:, vv]
        o_ref[:, vv] = o * lax.rsqrt(jnp.mean(o * o, axis=-1, keepdims=True) + EPS) * gn_ref[...]
        if not has_state:
            sf_ref[0, hh] = stf_ref[hh].T
            sb_ref[0, hh] = stb_ref[hh].T


def _gla(proj, w2, bg, gnorm, n_seq, n_tok, row_block0, hps, states=None):
    kw, vw = hps * GLA_DK, hps * GLA_DV
    spec = lambda width, off: pl.BlockSpec((n_tok, width), lambda b, h: (row_block0 + b, off + h))
    in_specs = [spec(kw, 0), spec(kw, GLA_HK // kw), spec(vw, 2 * GLA_HK // vw),
                pl.BlockSpec((n_tok, 128), lambda b, h: (row_block0 + b, (2 * GLA_HK + 2 * GLA_HV) // 128)),
                pl.BlockSpec((2, 128, kw), lambda b, h: (0, 0, h)),
                pl.BlockSpec((2, 1, kw), lambda b, h: (0, 0, h)),
                pl.BlockSpec((1, GLA_DV), lambda b, h: (0, 0))]
    args = [proj, proj, proj, proj, w2, bg, gnorm.reshape(1, GLA_DV)]
    st_spec = pl.BlockSpec((1, hps, GLA_DK, GLA_DV), lambda b, h: (b, h, 0, 0))
    o_spec = pl.BlockSpec((n_tok, vw), lambda b, h: (b, h))
    o_shape = jax.ShapeDtypeStruct((n_seq * n_tok, GLA_HV), F32)
    if states is not None:
        in_specs += [st_spec, st_spec]
        args += list(states)
        out_specs, out_shape = o_spec, o_shape
    else:
        st_shape = jax.ShapeDtypeStruct((n_seq, GLA_HEADS, GLA_DK, GLA_DV), F32)
        out_specs, out_shape = [o_spec, st_spec, st_spec], [o_shape, st_shape, st_shape]
    return pl.pallas_call(
        functools.partial(_gla_kernel, n_tok=n_tok, has_state=states is not None, hps=hps),
        grid=(n_seq, GLA_HEADS // hps),
        in_specs=in_specs,
        out_specs=out_specs,
        out_shape=out_shape,
        scratch_shapes=[pltpu.VMEM((2, n_tok, kw), F32),
                        pltpu.VMEM((n_tok, vw), F32),
                        pltpu.VMEM((n_tok, vw), F32),
                        pltpu.VMEM((hps, GLA_DV, GLA_DK), F32),
                        pltpu.VMEM((hps, GLA_DV, GLA_DK), F32)],
        compiler_params=_params(("arbitrary", "arbitrary")),
        name="gla_latent" if states is not None else "gla_prompt",
    )(*args)


DIFF_HB = 2
DIFF_QL = DIFF_HB * DIFF_HD
DIFF_VL = DIFF_HB * 2 * DIFF_HD
DIFF_TQ = 512
DIFF_SUB = 128


def _diff_lambda(lam_ref):
    l = lam_ref[...]
    a = jnp.sum(l[0:1] * l[1:2], axis=-1, keepdims=True)
    b = jnp.sum(l[2:3] * l[3:4], axis=-1, keepdims=True)
    return jnp.exp(a) - jnp.exp(b) + DIFF_LAMBDA_INIT


def _diff_finish(ps, invs, lam, v, sn_ref):
    a = ps[0] * invs[0] - (lam * invs[1]) * ps[1]
    o2 = _bdot(a.astype(BF16), v)
    o = _unstack_heads(o2, DIFF_HB, 2 * DIFF_HD)
    return _group_rms(o, sn_ref[...], 2 * DIFF_HD) * (1.0 - DIFF_LAMBDA_INIT)


def _diff_prompt_kernel(q0_ref, q1_ref, k0_ref, k1_ref, v_ref, gq_ref, gk_ref, lam_ref, sn_ref,
                        o_ref, kn_ref, vn_ref):
    scale = DIFF_HD ** -0.5 * LOG2E
    lam = _diff_lambda(lam_ref)

    def chains(bound):
        shift = None if bound is None else _stacked_rows(bound, SEQ)
        for seq in range(PROMPT_SEQS):
            sl = slice(seq * SEQ, (seq + 1) * SEQ)
            ps, invs = [], []
            for comp, (q_ref, k_ref) in enumerate(((q0_ref, k0_ref), (q1_ref, k1_ref))):
                q = _group_rms(q_ref[sl, :], gq_ref[...], DIFF_HD, sums_on_mxu=True) * scale
                k = _group_rms(k_ref[sl, :], gk_ref[...], DIFF_HD, sums_on_mxu=True)
                for h in range(DIFF_HB):
                    kn_ref[seq, comp, h] = k[:, h * DIFF_HD:(h + 1) * DIFF_HD]
                q2 = _stack_heads(q, DIFF_HB, DIFF_HD).astype(BF16)
                s = lax.dot_general(q2, k.astype(BF16), _NT, preferred_element_type=F32)
                (p,), inv = _softmax_parts([s], shift)
                ps.append(p)
                invs.append(inv)
            v = v_ref[sl, :]
            _store_heads(vn_ref, seq, v, DIFF_HB, 2 * DIFF_HD)
            o_ref[sl, :] = _diff_finish(ps, invs, lam, v.astype(BF16), sn_ref).astype(BF16)

    _with_score_bound(scale * _rms_norm_bound(gq_ref, DIFF_HD, DIFF_HD)
                      * _rms_norm_bound(gk_ref, DIFF_HD, DIFF_HD), chains)


def _diff_prompt(qkv, gq, gk, lam, sub_norm):
    nb = DIFF_HEADS // DIFF_HB
    rows = PROMPT_SEQS * SEQ
    qk = lambda off: pl.BlockSpec((rows, DIFF_QL), lambda b, j: (b, off + j))
    vec = lambda n: pl.BlockSpec((1, n), lambda b, j: (0, 0))
    v_spec = pl.BlockSpec((rows, DIFF_VL), lambda b, j: (b, 2 * D_MODEL // DIFF_VL + j))
    kn_spec = pl.BlockSpec((PROMPT_SEQS, 2, DIFF_HB, SEQ, DIFF_HD), lambda b, j: (b, 0, j, 0, 0))
    vn_spec = pl.BlockSpec((PROMPT_SEQS, DIFF_HB, SEQ, 2 * DIFF_HD), lambda b, j: (b, j, 0, 0))
    return pl.pallas_call(
        _diff_prompt_kernel,
        grid=(BATCH // PROMPT_SEQS, nb),
        in_specs=[qk(0), qk(nb), qk(2 * nb), qk(3 * nb), v_spec, vec(DIFF_QL), vec(DIFF_QL),
                  pl.BlockSpec((4, DIFF_HD), lambda b, j: (0, 0)), vec(DIFF_VL)],
        out_specs=[pl.BlockSpec((rows, DIFF_VL), lambda b, j: (b, j)), kn_spec, vn_spec],
        out_shape=[jax.ShapeDtypeStruct((N_PROMPT, D_MODEL), BF16),
                   jax.ShapeDtypeStruct((BATCH, 2, DIFF_HEADS, SEQ, DIFF_HD), F32),
                   jax.ShapeDtypeStruct((BATCH, DIFF_HEADS, SEQ, 2 * DIFF_HD), F32)],
        compiler_params=_params(("arbitrary", "arbitrary")),
        name="diff_prompt",
    )(qkv, qkv, qkv, qkv, qkv, jnp.tile(gq, DIFF_HB).reshape(1, -1),
      jnp.tile(gk, DIFF_HB).reshape(1, -1), lam, jnp.tile(sub_norm, DIFF_HB).reshape(1, -1))


def _diff_latent_kernel(q0_ref, q1_ref, k0_ref, k1_ref, v_ref, kc0_ref, kc1_ref, vc_ref,
                        cos_ref, sin_ref, cosq_ref, sinq_ref, gq_ref, gk_ref, lam_ref, sn_ref,
                        o_ref, kb0_ref, kb1_ref, vb_ref, kmax_ref):
    scale = DIFF_HD ** -0.5 * LOG2E
    half = DIFF_HD // 4
    rows = 256

    @pl.when(pl.program_id(2) == 0)
    def _():
        for comp, (k_ref, kc_ref, kb_ref) in enumerate(
                ((k0_ref, kc0_ref, kb0_ref), (k1_ref, kc1_ref, kb1_ref))):
            kc = kc_ref[0]
            kb_ref[0:PAST_LEN, :] = kc.astype(BF16)
            kmax_ref[comp:comp + 1, :] = jnp.maximum(
                _max_head_norms(kc, DIFF_HD), _rms_norm_bound(gk_ref, DIFF_HD, DIFF_HD))

            def prep(r, carry):
                sl = pl.ds(pl.multiple_of(r * rows, rows), rows)
                k = _group_rms(k_ref[sl, :], gk_ref[...], DIFF_HD)
                k = _rope(k, cos_ref[sl, :], sin_ref[sl, :], half)
                kb_ref[pl.ds(pl.multiple_of(PAST_LEN + r * rows, rows), rows), :] = k.astype(BF16)
                return carry

            lax.fori_loop(0, DEC_SEQ // rows, prep, 0, unroll=2)
        vb_ref[0:PAST_LEN, :] = vc_ref[0].astype(BF16)
        vb_ref[PAST_LEN:, :] = v_ref[...].astype(BF16)

    lam = _diff_lambda(lam_ref)

    def chains(bound):
        for r0 in range(0, DIFF_TQ, DIFF_SUB):
            sl = slice(r0, r0 + DIFF_SUB)
            ps, invs = [], []
            for comp, (q_ref, kb_ref) in enumerate(((q0_ref, kb0_ref), (q1_ref, kb1_ref))):
                shift = None if bound is None else _stacked_rows(
                    bound[:, comp * DIFF_HB:(comp + 1) * DIFF_HB], DIFF_SUB)
                q = _group_rms(q_ref[sl, :], gq_ref[...], DIFF_HD)
                q = _rope(q, cosq_ref[sl, :], sinq_ref[sl, :], half) * scale
                q2 = _stack_heads(q, DIFF_HB, DIFF_HD).astype(BF16)
                s = lax.dot_general(q2, kb_ref[...], _NT, preferred_element_type=F32)
                (p,), inv = _softmax_parts([s], shift)
                ps.append(p)
                invs.append(inv)
            o_ref[sl, :] = _diff_finish(ps, invs, lam, vb_ref[...], sn_ref).astype(BF16)

    qmax = scale * _rms_norm_bound(gq_ref, DIFF_HD, DIFF_HD)
    _with_score_bound(jnp.concatenate([qmax * kmax_ref[0:1, :], qmax * kmax_ref[1:2, :]], axis=-1),
                      chains)


def _diff_latent(qkv, cache_k, cache_v, cos, sin, gq, gk, lam, sub_norm):
    nb = DIFF_HEADS // DIFF_HB
    nq = DEC_SEQ // DIFF_TQ
    q0 = N_PROMPT // DIFF_TQ
    lat0 = N_PROMPT // DEC_SEQ
    n_keys = PAST_LEN + DEC_SEQ
    q_spec = lambda off: pl.BlockSpec((DIFF_TQ, DIFF_QL), lambda b, j, t: (q0 + b * nq + t, off + j))
    k_spec = lambda off: pl.BlockSpec((DEC_SEQ, DIFF_QL), lambda b, j, t: (lat0 + b, off + j))
    v_spec = pl.BlockSpec((DEC_SEQ, DIFF_VL), lambda b, j, t: (lat0 + b, 2 * D_MODEL // DIFF_VL + j))
    kc_spec = lambda off: pl.BlockSpec((1, PAST_LEN, DIFF_QL), lambda b, j, t: (b, 0, off + j))
    vc_spec = pl.BlockSpec((1, PAST_LEN, DIFF_VL), lambda b, j, t: (b, 0, j))
    tab = pl.BlockSpec((DEC_SEQ, DIFF_QL), lambda b, j, t: (0, 0))
    tabq = pl.BlockSpec((DIFF_TQ, DIFF_QL), lambda b, j, t: (t, 0))
    vec = lambda n: pl.BlockSpec((1, n), lambda b, j, t: (0, 0))
    return pl.pallas_call(
        _diff_latent_kernel,
        grid=(DEC_BATCH, nb, nq),
        in_specs=[q_spec(0), q_spec(nb), k_spec(2 * nb), k_spec(3 * nb), v_spec,
                  kc_spec(0), kc_spec(nb), vc_spec, tab, tab, tabq, tabq,
                  vec(DIFF_QL), vec(DIFF_QL),
                  pl.BlockSpec((4, DIFF_HD), lambda b, j, t: (0, 0)), vec(DIFF_VL)],
        out_specs=pl.BlockSpec((DIFF_TQ, DIFF_VL), lambda b, j, t: (b * nq + t, j)),
        out_shape=jax.ShapeDtypeStruct((N_LATENT, D_MODEL), BF16),
        scratch_shapes=[pltpu.VMEM((n_keys, DIFF_QL), BF16),
                        pltpu.VMEM((n_keys, DIFF_QL), BF16),
                        pltpu.VMEM((n_keys, DIFF_VL), BF16),
                        pltpu.VMEM((2, DIFF_HB), F32)],
        compiler_params=_params(("arbitrary", "arbitrary", "arbitrary")),
        name="diff_latent",
    )(qkv, qkv, qkv, qkv, qkv, _tokens_first(cache_k), _tokens_first(cache_k),
      _tokens_first(cache_v), cos, sin, cos, sin,
      jnp.tile(gq, DIFF_HB).reshape(1, -1), jnp.tile(gk, DIFF_HB).reshape(1, -1), lam,
      jnp.tile(sub_norm, DIFF_HB).reshape(1, -1))


MLA_HB = 2
MLA_HL = 128
MLA_LANES = MLA_HB * MLA_HL
MLA_TQ = 1024
MLA_SUB = 128


def _mla_keys(kv, kr, gk, sums_on_mxu=False):
    lane = lax.broadcasted_iota(jnp.int32, kv.shape, 1)
    kr2 = jnp.concatenate([kr] * MLA_HB, axis=1)
    k = jnp.where((lane & (MLA_HL - 1)) < MLA_NOPE, kv, kr2)
    return _group_rms(k, gk, MLA_HL, n_real=MLA_QK, sums_on_mxu=sums_on_mxu)


def _mla_out(o2):
    tq = o2.shape[0] // MLA_HB
    oa = pltpu.roll(o2[0:tq, 0:MLA_HL], MLA_HL - MLA_V, axis=1)
    ob = o2[tq:, MLA_HL:]
    lane = lax.broadcasted_iota(jnp.int32, oa.shape, 1)
    return jnp.where(lane < MLA_V, oa, ob)


def _mla_prompt_kernel(q_ref, kv_ref, kr_ref, gq_ref, gk_ref, o_ref):
    scale = MLA_QK ** -0.5 * LOG2E

    def chains(bound):
        shift = None if bound is None else _stacked_rows(bound, SEQ)
        for seq in range(PROMPT_SEQS):
            sl = slice(seq * SEQ, (seq + 1) * SEQ)
            q = _group_rms(q_ref[sl, :], gq_ref[...], MLA_HL, n_real=MLA_QK) * scale
            kv = kv_ref[sl, :]
            k = _mla_keys(kv, kr_ref[sl, :], gk_ref[...])
            q2 = _stack_heads(q, MLA_HB, MLA_HL).astype(BF16)
            s = lax.dot_general(q2, k.astype(BF16), _NT, preferred_element_type=F32)
            (p,), inv = _softmax_parts([s], shift)
            o_ref[sl, :] = _mla_out(_bdot(p.astype(BF16), kv.astype(BF16)) * inv).astype(BF16)

    _with_score_bound(scale * _rms_norm_bound(gq_ref, MLA_QK, MLA_HL)
                      * _rms_norm_bound(gk_ref, MLA_QK, MLA_HL), chains)


def _mla_prompt(qp, kvp, low, gq, gk):
    nb = MLA_HEADS // MLA_HB
    rows = PROMPT_SEQS * SEQ
    blk = pl.BlockSpec((rows, MLA_LANES), lambda b, j: (b, j))
    vec = pl.BlockSpec((1, MLA_LANES), lambda b, j: (0, 0))
    return pl.pallas_call(
        _mla_prompt_kernel,
        grid=(BATCH // PROMPT_SEQS, nb),
        in_specs=[blk, blk, pl.BlockSpec((rows, MLA_HL), lambda b, j: (b, 0)),
                  vec, vec],
        out_specs=pl.BlockSpec((rows, MLA_HB * MLA_V), lambda b, j: (b, j)),
        out_shape=jax.ShapeDtypeStruct((N_PROMPT, MLA_HEADS * MLA_V), BF16),
        compiler_params=_params(("arbitrary", "arbitrary")),
        name="mla_prompt",
    )(qp, kvp, low, gq, gk)


def _mla_latent_kernel(q_ref, kv_ref, kr_ref, kvc_ref, krc_ref, cos_ref, sin_ref, cosq_ref,
                       sinq_ref, gq_ref, gk_ref, o_ref, kb_ref, vb_ref, kmax_ref):
    scale = MLA_QK ** -0.5 * LOG2E
    half = MLA_ROPE // 4
    rows = 256

    @pl.when(pl.program_id(2) == 0)
    def _():
        kvc = kvc_ref[...]
        kc = _mla_keys(kvc, krc_ref[...], gk_ref[...])
        kb_ref[0:PAST_LEN, :] = kc.astype(BF16)
        vb_ref[0:PAST_LEN, :] = kvc.astype(BF16)

        def prep(r, carry):
            sl = pl.ds(pl.multiple_of(r * rows, rows), rows)
            dst = pl.ds(pl.multiple_of(PAST_LEN + r * rows, rows), rows)
            kv = kv_ref[sl, :]
            k = _mla_keys(kv, kr_ref[sl, :], gk_ref[...], sums_on_mxu=True)
            k = _rope(k, cos_ref[sl, :], sin_ref[sl, :], half)
            kb_ref[dst, :] = k.astype(BF16)
            vb_ref[dst, :] = kv.astype(BF16)
            return carry

        lax.fori_loop(0, DEC_SEQ // rows, prep, 0, unroll=4)
        kmax_ref[...] = jnp.maximum(_max_head_norms(kc, MLA_HL),
                                    _rms_norm_bound(gk_ref, MLA_QK, MLA_HL))

    def chains(bound):
        shift = None if bound is None else _stacked_rows(bound, MLA_SUB)
        for r0 in range(0, MLA_TQ, MLA_SUB):
            sl = slice(r0, r0 + MLA_SUB)
            q = _group_rms(q_ref[sl, :], gq_ref[...], MLA_HL, n_real=MLA_QK)
            q = _rope(q, cosq_ref[sl, :], sinq_ref[sl, :], half) * scale
            q2 = _stack_heads(q, MLA_HB, MLA_HL).astype(BF16)
            s = lax.dot_general(q2, kb_ref[...], _NT, preferred_element_type=F32)
            (p,), inv = _softmax_parts([s], shift)
            o_ref[sl, :] = _mla_out(_bdot(p.astype(BF16), vb_ref[...]) * inv).astype(BF16)

    _with_score_bound(scale * _rms_norm_bound(gq_ref, MLA_QK, MLA_HL) * kmax_ref[...], chains)


def _mla_latent(qp, kvp, low, kvc, krc, cos, sin, gq, gk):
    nb = MLA_HEADS // MLA_HB
    nq = DEC_SEQ // MLA_TQ
    q0 = N_PROMPT // MLA_TQ
    lat0 = N_PROMPT // DEC_SEQ
    n_keys = PAST_LEN + DEC_SEQ
    tab = pl.BlockSpec((DEC_SEQ, MLA_LANES), lambda b, j, t: (0, 0))
    tabq = pl.BlockSpec((MLA_TQ, MLA_LANES), lambda b, j, t: (t, 0))
    vec = pl.BlockSpec((1, MLA_LANES), lambda b, j, t: (0, 0))
    return pl.pallas_call(
        _mla_latent_kernel,
        grid=(DEC_BATCH, nb, nq),
        in_specs=[pl.BlockSpec((MLA_TQ, MLA_LANES), lambda b, j, t: (q0 + b * nq + t, j)),
                  pl.BlockSpec((DEC_SEQ, MLA_LANES), lambda b, j, t: (lat0 + b, j)),
                  pl.BlockSpec((DEC_SEQ, MLA_HL), lambda b, j, t: (lat0 + b, 0)),
                  pl.BlockSpec((PAST_LEN, MLA_LANES), lambda b, j, t: (b, j)),
                  pl.BlockSpec((PAST_LEN, MLA_HL), lambda b, j, t: (b, 0)),
                  tab, tab, tabq, tabq, vec, vec],
        out_specs=pl.BlockSpec((MLA_TQ, MLA_HB * MLA_V), lambda b, j, t: (b * nq + t, j)),
        out_shape=jax.ShapeDtypeStruct((N_LATENT, MLA_HEADS * MLA_V), BF16),
        scratch_shapes=[pltpu.VMEM((n_keys, MLA_LANES), BF16),
                        pltpu.VMEM((n_keys, MLA_LANES), BF16),
                        pltpu.VMEM((1, MLA_HB), F32)],
        compiler_params=_params(("arbitrary", "arbitrary", "arbitrary")),
        name="mla_latent",
    )(qp, kvp, low, kvc, krc, cos, sin, cos, sin, gq, gk)


MLA_LOW_Q = 0
MLA_LOW_KV = 512
MLA_LOW_KR = 768
MLA_LOW_N = 896


def _axial_tables(n_tok, rdim):
    nf = rdim // 4
    freqs = ROPE_BASE ** (-jnp.arange(nf, dtype=F32) / nf)
    t = jnp.arange(n_tok)
    rowp = (t // GRID_W).astype(F32)
    colp = (t % GRID_W).astype(F32)
    ang = jnp.stack([rowp[:, None] * freqs, colp[:, None] * freqs], axis=1)
    cos, sin = jnp.cos(ang), jnp.sin(ang)
    cos_l = jnp.stack([cos, cos], axis=2).reshape(n_tok, rdim)
    sin_l = jnp.stack([-sin, sin], axis=2).reshape(n_tok, rdim)
    return cos_l, sin_l


def _diff_rope_tables():
    cos, sin = _axial_tables(DEC_SEQ, DIFF_HD)
    return jnp.tile(cos, (1, DIFF_HB)), jnp.tile(sin, (1, DIFF_HB))


def _mla_rope_tables():
    cos, sin = _axial_tables(DEC_SEQ, MLA_ROPE)
    ones = jnp.ones((DEC_SEQ, MLA_NOPE), F32)
    pad1 = jnp.ones((DEC_SEQ, MLA_HL - MLA_QK), F32)
    cos_h = jnp.concatenate([ones, cos, pad1], axis=1)
    sin_h = jnp.concatenate([0 * ones, sin, 0 * pad1], axis=1)
    return jnp.tile(cos_h, (1, MLA_HB)), jnp.tile(sin_h, (1, MLA_HB))


def _tokens_first(cache):
    b, h, l, d = cache.shape
    return jnp.transpose(cache, (0, 2, 1, 3)).reshape(b, l, h * d)


def _pad_heads(w, heads, hd, hl):
    k = w.shape[0]
    return jnp.pad(w.reshape(k, heads, hd), ((0, 0), (0, 0), (0, hl - hd))).reshape(k, heads * hl)


def kernel(x_prompt, x_sample, cache_l0_k, cache_l0_v, state_l1_fwd, state_l1_bwd, cache_l2_k,
           cache_l2_v, cache_l3_ckv, cache_l3_krope, c, c_ctx, ada_w, ada_b, norm_mix, norm_ffn,
           ffn_w_up, ffn_conv_w, ffn_conv_b, ffn_w_down, na_w_qkv, na_q_norm, na_k_norm, na_bias,
           na_w_o, gla_w_qkvg, gla_w_gate1, gla_w_gate2, gla_b_gate, gla_o_norm, gla_w_o,
           diff_w_qkv, diff_q_norm, diff_k_norm, diff_lambda, diff_sub_norm, diff_w_o, mla_w_dq,
           mla_q_a_norm, mla_w_uq, mla_w_dkv, mla_kv_a_norm, mla_w_ukv, mla_q_norm, mla_k_norm,
           mla_w_o):
    xr = _Rows(x_prompt.reshape(N_PROMPT, D_MODEL), x_sample.reshape(N_LATENT, D_MODEL), 0)
    cvecs = jnp.concatenate([c_ctx[None], c, jnp.zeros((5, D_MODEL), F32)], axis=0)
    mods_all = _ada_mods(cvecs, ada_w, ada_b)
    halves = lambda o_p, o_s: _Rows(o_p, o_s, 0)

    mods = mods_all[0]
    qkv = _norm_mod_proj(xr, norm_mix[0], mods, [na_w_qkv], "na_qkv")
    o_p, new_l0_k, new_l0_v = _na_prompt(qkv, na_q_norm, na_k_norm)
    o_s = _na_latent(qkv, cache_l0_k, cache_l0_v, _na_bias_blocks(na_bias), na_q_norm, na_k_norm)
    x = _out_proj_residual(xr, halves(o_p, o_s), mods, na_w_o, "na_out")
    ffn_weights = (ffn_w_up, ffn_conv_w, ffn_conv_b, ffn_w_down)
    x = _ffn(x, norm_ffn[0], mods, 0, *ffn_weights)
    xr = _one_array(x)

    mods = mods_all[1]
    w_decay = jnp.concatenate(
        [gla_w_gate1[0], gla_w_gate1[1],
         jnp.zeros((D_MODEL, 128 - 2 * GLA_GATE_RANK), F32)], axis=1)
    proj = _norm_mod_proj(xr, norm_mix[1], mods, [gla_w_qkvg, w_decay], "gla_proj")
    w2 = jnp.zeros((2, 128, GLA_HK), F32)
    w2 = w2.at[0, :GLA_GATE_RANK].set(gla_w_gate2[0])
    w2 = w2.at[1, GLA_GATE_RANK:2 * GLA_GATE_RANK].set(gla_w_gate2[1])
    bg = gla_b_gate.reshape(2, 1, GLA_HK)
    o_p, new_l1_fwd, new_l1_bwd = _gla(proj, w2, bg, gla_o_norm, BATCH, SEQ, 0, hps=4)
    o_s = _gla(proj, w2, bg, gla_o_norm, DEC_BATCH, DEC_SEQ, N_PROMPT // DEC_SEQ, hps=2,
               states=(state_l1_fwd, state_l1_bwd))
    x = _out_proj_residual(xr, halves(o_p, o_s), mods, gla_w_o, "gla_out",
                           gate=proj, gate_col_block=(2 * GLA_HK + GLA_HV) // GLA_HV)
    x = _ffn(x, norm_ffn[1], mods, 1, *ffn_weights)
    xr = _one_array(x)

    mods = mods_all[2]
    qkv = _norm_mod_proj(xr, norm_mix[2], mods, [diff_w_qkv], "diff_qkv")
    o_p, kn_p, new_l2_v = _diff_prompt(qkv, diff_q_norm, diff_k_norm, diff_lambda, diff_sub_norm)
    new_l2_k = kn_p.reshape(BATCH, 2 * DIFF_HEADS, SEQ, DIFF_HD)
    cos_d, sin_d = _diff_rope_tables()
    o_s = _diff_latent(qkv, cache_l2_k, cache_l2_v, cos_d, sin_d, diff_q_norm, diff_k_norm,
                       diff_lambda, diff_sub_norm)
    x = _out_proj_residual(xr, halves(o_p, o_s), mods, diff_w_o, "diff_out")
    x = _ffn(x, norm_ffn[2], mods, 2, *ffn_weights)
    xr = _one_array(x)

    mods = mods_all[3]
    zc = lambda n: jnp.zeros((D_MODEL, n), F32)
    w_low = jnp.concatenate(
        [mla_w_dq, zc(MLA_LOW_KV - MLA_Q_RANK), mla_w_dkv[:, :MLA_KV_RANK],
         zc(MLA_NOPE), mla_w_dkv[:, MLA_KV_RANK:], zc(MLA_HL - MLA_QK)], axis=1)
    w_uq = _pad_heads(mla_w_uq, MLA_HEADS, MLA_QK, MLA_HL)
    qp, kvp, ckv, kr = _mla_front(xr, norm_mix[3], mods, w_low, mla_q_a_norm, w_uq,
                                  mla_kv_a_norm, mla_w_ukv)
    kvc = _matmul(cache_l3_ckv.reshape(DEC_BATCH * PAST_LEN, MLA_KV_RANK), mla_w_ukv,
                  "mla_ukv_cache")
    krc = jnp.pad(cache_l3_krope.reshape(DEC_BATCH * PAST_LEN, MLA_ROPE),
                  ((0, 0), (MLA_NOPE, MLA_HL - MLA_QK)))
    pad_gain = lambda g: jnp.tile(jnp.pad(g, (0, MLA_HL - MLA_QK)), MLA_HB).reshape(1, -1)
    gq, gk = pad_gain(mla_q_norm), pad_gain(mla_k_norm)
    o_p = _mla_prompt(qp, kvp, kr, gq, gk)
    cos_m, sin_m = _mla_rope_tables()
    o_s = _mla_latent(qp, kvp, kr, kvc, krc, cos_m, sin_m, gq, gk)
    new_l3_ckv = ckv[:N_PROMPT].reshape(BATCH, SEQ, MLA_KV_RANK)
    new_l3_krope = kr[:N_PROMPT, MLA_NOPE:MLA_QK].reshape(BATCH, SEQ, MLA_ROPE)
    x = _out_proj_residual(xr, halves(o_p, o_s), mods, mla_w_o, "mla_out")
    n_pt = N_PROMPT // TOK_TILE
    ffn3 = functools.partial(_ffn, x, norm_ffn[3], mods, 3, *ffn_weights)
    y_prompt = ffn3(tile0=0, n_tiles=n_pt).reshape(BATCH, SEQ, D_MODEL)
    y_sample = ffn3(tile0=n_pt, n_tiles=N_TOK_TILES - n_pt).reshape(DEC_BATCH, DEC_SEQ, D_MODEL)
    return (y_prompt, y_sample, new_l0_k, new_l0_v, new_l1_fwd, new_l1_bwd, new_l2_k, new_l2_v,
            new_l3_ckv, new_l3_krope)
```

```python
import functools
import math
from typing import NamedTuple

import jax
import jax.numpy as jnp
from jax import lax
from jax.experimental import pallas as pl
from jax.experimental.pallas import tpu as pltpu

F32 = jnp.float32
BF16 = jnp.bfloat16

D_MODEL = 1024
BATCH = 16
SEQ = 256
DEPTH = 4
DEC_BATCH = 2
DEC_SEQ = 2048
PAST_LEN = 256
GRID_W = 64
D_FF = 2816
EPS = 1e-6
ROPE_BASE = 10000.0

NA_HEADS = 16
NA_HD = 64
NA_WIN_R = 8
NA_WIN_C = 16

GLA_HEADS = 4
GLA_DK = 128
GLA_DV = 256
GLA_HK = GLA_HEADS * GLA_DK
GLA_HV = GLA_HEADS * GLA_DV
GLA_GATE_RANK = 16
GLA_GATE_NORM = 16.0

DIFF_HEADS = 8
DIFF_HD = 64
DIFF_LAMBDA_INIT = 0.8 - 0.6 * math.exp(-0.3 * 2)

MLA_HEADS = 16
MLA_Q_RANK = 384
MLA_KV_RANK = 256
MLA_NOPE = 64
MLA_ROPE = 32
MLA_V = 64
MLA_QK = MLA_NOPE + MLA_ROPE

N_PROMPT = BATCH * SEQ
N_LATENT = DEC_BATCH * DEC_SEQ
N_TOK = N_PROMPT + N_LATENT
TOK_TILE = 2048
N_TOK_TILES = N_TOK // TOK_TILE
FF_CHUNK = 256
N_FF_CHUNKS = D_FF // FF_CHUNK
NEG = -1e30
LOG2E = math.log2(math.e)
SAFE_SCORE_BOUND = 48.0

VMEM_LIMIT = 56 * 1024 * 1024

_NT = (((1,), (1,)), ((), ()))
_TN = (((0,), (0,)), ((), ()))


def _params(sem, vmem=VMEM_LIMIT):
    return pltpu.CompilerParams(dimension_semantics=sem, vmem_limit_bytes=vmem)


def _log2(n):
    assert n & (n - 1) == 0
    return n.bit_length() - 1


def _silu(x):
    return x / (1.0 + jnp.exp(-x))


def _bdot(a, b):
    return jnp.dot(a, b, preferred_element_type=F32)


def _softmax_parts(parts, shift=None):
    m = shift
    if m is None:
        m = parts[0].max(axis=-1, keepdims=True)
        for s in parts[1:]:
            m = jnp.maximum(m, s.max(axis=-1, keepdims=True))
    ps = [jnp.exp2(s - m) for s in parts]
    l = ps[0].sum(axis=-1, keepdims=True)
    for p in ps[1:]:
        l = l + p.sum(axis=-1, keepdims=True)
    return ps, 1.0 / l


def _group_rms(x, gain, group, n_real=None, sums_on_mxu=False):
    lanes = x.shape[-1]
    n_real = n_real or group
    x2 = x * x
    if group == lanes:
        ms = jnp.sum(x2, axis=-1, keepdims=True)
    elif not sums_on_mxu:
        gid = lax.broadcasted_iota(jnp.int32, x.shape, 1) >> _log2(group)
        ms = jnp.zeros_like(x)
        for i in range(lanes // group):
            sel = gid == i
            si = jnp.sum(jnp.where(sel, x2, 0.0), axis=-1, keepdims=True)
            ms = jnp.where(sel, si, ms)
    else:
        r = lax.broadcasted_iota(jnp.int32, (lanes, lanes), 0) >> _log2(group)
        c = lax.broadcasted_iota(jnp.int32, (lanes, lanes), 1) >> _log2(group)
        ones = jnp.where(r == c, 1.0, 0.0).astype(BF16)
        hi = x2.astype(BF16)
        lo = (x2 - hi.astype(F32)).astype(BF16)
        ms = _bdot(hi, ones) + _bdot(lo, ones)
    return x * lax.rsqrt(ms * (1.0 / n_real) + EPS) * gain


def _with_score_bound(bound, body):
    ok = bound.max() < SAFE_SCORE_BOUND
    pl.when(ok)(lambda: body(bound))
    pl.when(jnp.logical_not(ok))(lambda: body(None))


def _rms_norm_bound(g_ref, n, head_lanes):
    g = jnp.abs(g_ref[...])
    heads = g.shape[-1] // head_lanes
    return n ** 0.5 * jnp.concatenate(
        [g[:, h * head_lanes:(h + 1) * head_lanes].max(axis=-1, keepdims=True)
         for h in range(heads)], axis=-1)


def _stacked_rows(per_head, rows):
    r = lax.broadcasted_iota(jnp.int32, (per_head.shape[-1] * rows, 1), 0)
    out = per_head[:, 0:1]
    for h in range(1, per_head.shape[-1]):
        out = jnp.where(r >= h * rows, per_head[:, h:h + 1], out)
    return out


def _max_head_norms(x, head_lanes):
    hid = lax.broadcasted_iota(jnp.int32, x.shape, 1) >> _log2(head_lanes)
    x2 = x * x
    sq = [jnp.sum(jnp.where(hid == h, x2, 0.0), axis=-1, keepdims=True).max(axis=0, keepdims=True)
          for h in range(x.shape[-1] // head_lanes)]
    return jnp.sqrt(jnp.concatenate(sq, axis=-1))


def _rope(x, cos, sin, half):
    lanes = x.shape[-1]
    lane = lax.broadcasted_iota(jnp.int32, x.shape, 1)
    up = pltpu.roll(x, lanes - half, axis=1)
    dn = pltpu.roll(x, half, axis=1)
    swapped = jnp.where((lane & (2 * half - 1)) < half, up, dn)
    return x * cos + swapped * sin


def _stack_heads(q, n_heads, head_lanes):
    hid = lax.broadcasted_iota(jnp.int32, q.shape, 1) >> _log2(head_lanes)
    zero = jnp.zeros_like(q)
    return jnp.concatenate([jnp.where(hid == i, q, zero) for i in range(n_heads)], axis=0)


def _unstack_heads(o, n_heads, head_lanes):
    rows = o.shape[0] // n_heads
    hid = lax.broadcasted_iota(jnp.int32, (rows, o.shape[1]), 1) >> _log2(head_lanes)
    out = o[0:rows]
    for i in range(1, n_heads):
        out = jnp.where(hid == i, o[i * rows:(i + 1) * rows], out)
    return out


ADA_TN = 1536


def _ada_kernel(c_ref, w_ref, b_ref, o_ref):
    s = _silu(c_ref[...])
    w = w_ref[0]
    s_hi, w_hi = s.astype(BF16), w.astype(BF16)
    s_lo = (s - s_hi.astype(F32)).astype(BF16)
    w_lo = (w - w_hi.astype(F32)).astype(BF16)
    both = _bdot(jnp.concatenate([s_hi, s_lo], axis=0), w_hi)
    o_ref[0] = both[0:8] + both[8:16] + _bdot(s_hi, w_lo) + b_ref[0]


def _ada_mods(cvecs, ada_w, ada_b):
    out = pl.pallas_call(
        _ada_kernel,
        grid=(DEPTH, 6 * D_MODEL // ADA_TN),
        in_specs=[pl.BlockSpec((8, D_MODEL), lambda l, j: (0, 0)),
                  pl.BlockSpec((1, D_MODEL, ADA_TN), lambda l, j: (l, 0, j)),
                  pl.BlockSpec((1, 1, ADA_TN), lambda l, j: (l, 0, j))],
        out_specs=pl.BlockSpec((1, 8, ADA_TN), lambda l, j: (l, 0, j)),
        out_shape=jax.ShapeDtypeStruct((DEPTH, 8, 6 * D_MODEL), F32),
        compiler_params=_params(("arbitrary", "arbitrary")),
        name="ada_mod",
    )(cvecs, ada_w, ada_b.reshape(DEPTH, 1, 6 * D_MODEL))
    return out.reshape(DEPTH, 8, 6, D_MODEL)[:, :3]


def _mod_group_of_tile(i):
    return jnp.maximum(i - (N_PROMPT // TOK_TILE - 1), 0)


def _norm_mod_rows(x_ref, g_ref, mod_ref, h_ref, shift_idx, scale_idx, rows=64):
    g = g_ref[...]
    sc = 1.0 + mod_ref[0, scale_idx:scale_idx + 1, :]
    sh = mod_ref[0, shift_idx:shift_idx + 1, :]

    def body(r, carry):
        sl = pl.ds(pl.multiple_of(r * rows, rows), rows)
        xf = x_ref[sl, :]
        ms = jnp.mean(xf * xf, axis=-1, keepdims=True)
        y = xf * lax.rsqrt(ms + EPS) * g
        h_ref[sl, :] = (y * sc + sh).astype(BF16)
        return carry

    lax.fori_loop(0, x_ref.shape[0] // rows, body, 0, unroll=4)


ROW_TM = 512


class _Rows(NamedTuple):
    prompt: jax.Array
    latent: jax.Array
    latent_row0: int


def _one_array(x):
    return _Rows(x, x, N_PROMPT)


def _row_specs(rows, width, col_block=0):
    n_p = N_PROMPT // ROW_TM
    l0 = rows.latent_row0 // ROW_TM
    return [pl.BlockSpec((ROW_TM, width), lambda t: (jnp.minimum(t, n_p - 1), col_block)),
            pl.BlockSpec((ROW_TM, width), lambda t: (l0 + jnp.maximum(t - n_p, 0), col_block))]


def _row_group(t):
    first_latent = N_PROMPT // ROW_TM
    return jnp.where(t < first_latent, 0, 1 + (t - first_latent) // (DEC_SEQ // ROW_TM))


def _is_prompt_tile():
    return pl.program_id(0) < N_PROMPT // ROW_TM


def _proj_kernel(xp_ref, xl_ref, g_ref, mod_ref, *refs):
    *w_refs, o_ref, h_ref, wb_ref = refs

    @pl.when(pl.program_id(0) == 0)
    def _():
        off = 0
        for w_ref in w_refs:
            wb_ref[:, off:off + w_ref.shape[1]] = w_ref[...].astype(BF16)
            off += w_ref.shape[1]

    is_prompt = _is_prompt_tile()
    g = g_ref[...]
    sc = 1.0 + mod_ref[0, 1:2, :]
    sh = mod_ref[0, 0:1, :]
    rows = 64
    part = ROW_TM // 2
    for p0 in range(0, ROW_TM, part):
        for r0 in range(p0, p0 + part, rows):
            sl = slice(r0, r0 + rows)
            xf = jnp.where(is_prompt, xp_ref[sl, :], xl_ref[sl, :])
            ms = jnp.mean(xf * xf, axis=-1, keepdims=True)
            y = xf * lax.rsqrt(ms + EPS) * g
            h_ref[sl, :] = (y * sc + sh).astype(BF16)
        o_ref[p0:p0 + part, :] = _bdot(h_ref[p0:p0 + part, :], wb_ref[...])


def _norm_mod_proj(x, g, mods, ws, name):
    n = sum(w.shape[1] for w in ws)
    return pl.pallas_call(
        _proj_kernel,
        grid=(N_TOK // ROW_TM,),
        in_specs=_row_specs(x, D_MODEL) + [
            pl.BlockSpec((1, D_MODEL), lambda t: (0, 0)),
            pl.BlockSpec((1, 6, D_MODEL), lambda t: (_row_group(t), 0, 0))] + [
            pl.BlockSpec(w.shape, lambda t: (0, 0), pipeline_mode=pl.Buffered(1)) for w in ws],
        out_specs=pl.BlockSpec((ROW_TM, n), lambda t: (t, 0)),
        out_shape=jax.ShapeDtypeStruct((N_TOK, n), F32),
        scratch_shapes=[pltpu.VMEM((ROW_TM, D_MODEL), BF16),
                        pltpu.VMEM((D_MODEL, n), BF16)],
        compiler_params=_params(("arbitrary",)),
        name=name,
    )(x.prompt, x.latent, g.reshape(1, D_MODEL), mods, *ws)


def _rms(a, g):
    return a * lax.rsqrt(jnp.mean(a * a, axis=-1, keepdims=True) + EPS) * g


def _mla_front_kernel(xp_ref, xl_ref, g_ref, mod_ref, wl_ref, gqa_ref, wuq_ref, gkv_ref, wukv_ref,
                      qp_ref, kvp_ref, ckv_ref, kr_ref, h_ref, wlb_ref, wuqb_ref, wukvb_ref):
    @pl.when(pl.program_id(0) == 0)
    def _():
        wlb_ref[...] = wl_ref[...].astype(BF16)
        wuqb_ref[...] = wuq_ref[...].astype(BF16)
        wukvb_ref[...] = wukv_ref[...].astype(BF16)

    is_prompt = _is_prompt_tile()
    g = g_ref[...]
    sc = 1.0 + mod_ref[0, 1:2, :]
    sh = mod_ref[0, 0:1, :]
    rows = 64
    part = ROW_TM // 2
    for p0 in range(0, ROW_TM, part):
        for r0 in range(p0, p0 + part, rows):
            sl = slice(r0, r0 + rows)
            xf = jnp.where(is_prompt, xp_ref[sl, :], xl_ref[sl, :])
            h_ref[sl, :] = (_rms(xf, g) * sc + sh).astype(BF16)
        ps = slice(p0, p0 + part)
        low = _bdot(h_ref[ps, :], wlb_ref[...])
        qa = _rms(low[:, MLA_LOW_Q:MLA_LOW_Q + MLA_Q_RANK], gqa_ref[...])
        qp_ref[ps, :] = _bdot(qa.astype(BF16), wuqb_ref[...])
        ckv = _rms(low[:, MLA_LOW_KV:MLA_LOW_KV + MLA_KV_RANK], gkv_ref[...])
        ckv_ref[ps, :] = ckv
        kvp_ref[ps, :] = _bdot(ckv.astype(BF16), wukvb_ref[...])
        kr_ref[ps, :] = low[:, MLA_LOW_KR:MLA_LOW_KR + MLA_HL]


def _mla_front(x, g, mods, w_low, g_qa, w_uq, g_kva, w_ukv):
    n_q, n_kv = w_uq.shape[1], w_ukv.shape[1]
    const = lambda shape: pl.BlockSpec(shape, lambda t: (0, 0), pipeline_mode=pl.Buffered(1))
    out = lambda n: pl.BlockSpec((ROW_TM, n), lambda t: (t, 0))
    shape = lambda n: jax.ShapeDtypeStruct((N_TOK, n), F32)
    return pl.pallas_call(
        _mla_front_kernel,
        grid=(N_TOK // ROW_TM,),
        in_specs=_row_specs(x, D_MODEL) + [
            pl.BlockSpec((1, D_MODEL), lambda t: (0, 0)),
            pl.BlockSpec((1, 6, D_MODEL), lambda t: (_row_group(t), 0, 0)),
            const(w_low.shape), const((1, MLA_Q_RANK)), const(w_uq.shape),
            const((1, MLA_KV_RANK)), const(w_ukv.shape)],
        out_specs=[out(n_q), out(n_kv), out(MLA_KV_RANK), out(MLA_HL)],
        out_shape=[shape(n_q), shape(n_kv), shape(MLA_KV_RANK), shape(MLA_HL)],
        scratch_shapes=[pltpu.VMEM((ROW_TM, D_MODEL), BF16),
                        pltpu.VMEM(w_low.shape, BF16),
                        pltpu.VMEM(w_uq.shape, BF16),
                        pltpu.VMEM(w_ukv.shape, BF16)],
        compiler_params=_params(("arbitrary",)),
        name="mla_front",
    )(x.prompt, x.latent, g.reshape(1, D_MODEL), mods, w_low, g_qa.reshape(1, -1), w_uq,
      g_kva.reshape(1, -1), w_ukv)


def _matmul_kernel(a_ref, w_ref, o_ref):
    o_ref[...] = _bdot(a_ref[...].astype(BF16), w_ref[...].astype(BF16))


def _matmul(a, w, name):
    rows, n = a.shape[0], w.shape[1]
    return pl.pallas_call(
        _matmul_kernel,
        out_shape=jax.ShapeDtypeStruct((rows, n), F32),
        compiler_params=_params(()),
        name=name,
    )(a, w)


def _oproj_kernel(*refs, gated):
    if gated:
        xp_ref, xl_ref, ap_ref, al_ref, g_ref, mod_ref, w_ref, o_ref, wb_ref = refs
    else:
        xp_ref, xl_ref, ap_ref, al_ref, mod_ref, w_ref, o_ref, wb_ref = refs

    @pl.when(pl.program_id(0) == 0)
    def _():
        wb_ref[...] = w_ref[...].astype(BF16)

    is_prompt = _is_prompt_tile()
    a = jnp.where(is_prompt, ap_ref[...], al_ref[...])
    if gated:
        a = a * _silu(g_ref[...])
    y = _bdot(a.astype(BF16), wb_ref[...])
    x = jnp.where(is_prompt, xp_ref[...], xl_ref[...])
    o_ref[...] = x + mod_ref[0, 2:3, :] * y


def _out_proj_residual(x, a, mods, w, name, gate=None, gate_col_block=0):
    k = w.shape[0]
    in_specs = _row_specs(x, D_MODEL) + _row_specs(a, k)
    args = [x.prompt, x.latent, a.prompt, a.latent]
    if gate is not None:
        in_specs.append(pl.BlockSpec((ROW_TM, k), lambda t: (t, gate_col_block)))
        args.append(gate)
    in_specs += [pl.BlockSpec((1, 6, D_MODEL), lambda t: (_row_group(t), 0, 0)),
                 pl.BlockSpec((k, D_MODEL), lambda t: (0, 0))]
    args += [mods, w]
    return pl.pallas_call(
        functools.partial(_oproj_kernel, gated=gate is not None),
        grid=(N_TOK // ROW_TM,),
        in_specs=in_specs,
        out_specs=pl.BlockSpec((ROW_TM, D_MODEL), lambda t: (t, 0)),
        out_shape=jax.ShapeDtypeStruct((N_TOK, D_MODEL), F32),
        scratch_shapes=[pltpu.VMEM((k, D_MODEL), BF16)],
        compiler_params=_params(("arbitrary",)),
        name=name,
    )(*args)


FFN_MM_ROWS = 512
FFN_ROWS = 64
FFN_PAD = 8


def _ffn_kernel(x_ref, g_ref, mod_ref, wg_ref, wv_ref, cwg_ref, cwv_ref, cbg_ref, cbv_ref,
                wd_ref, o_ref, h_ref, u_ref, act_ref, wup_ref, wdn_ref, *, tile0):
    i = tile0 + pl.program_id(0)
    c = pl.program_id(1)
    fc = FF_CHUNK
    n = TOK_TILE // FFN_MM_ROWS
    seq_len = jnp.where(i < N_PROMPT // TOK_TILE, SEQ, DEC_SEQ)
    row = lax.broadcasted_iota(jnp.int32, (FFN_ROWS, 1), 0)
    taps = lambda cw_ref, cb_ref: [jnp.broadcast_to(cw_ref[j:j + 1, :], (FFN_ROWS, fc))
                                   for j in range(3)] + [
                                       jnp.broadcast_to(cb_ref[...], (FFN_ROWS, fc))]
    taps_g, taps_v = taps(cwg_ref, cbg_ref), taps(cwv_ref, cbv_ref)

    def up(u_ref, t):
        r0 = t * FFN_MM_ROWS
        u_ref[FFN_PAD + r0:FFN_PAD + r0 + FFN_MM_ROWS, :] = _bdot(
            h_ref[r0:r0 + FFN_MM_ROWS, :], wup_ref[...])

    def conv_act(u_ref, t):
        for r0 in range(t * FFN_MM_ROWS, (t + 1) * FFN_MM_ROWS, FFN_ROWS):
            halves = []
            for lo, (w0, w1, w2, bias) in ((0, taps_g), (fc, taps_v)):
                p0 = FFN_PAD + r0
                prev = u_ref[p0 - 1:p0 - 1 + FFN_ROWS, lo:lo + fc]
                mid = u_ref[p0:p0 + FFN_ROWS, lo:lo + fc]
                nxt = u_ref[p0 + 1:p0 + 1 + FFN_ROWS, lo:lo + fc]
                if r0 % SEQ == 0:
                    prev = jnp.where(((r0 + row) & (seq_len - 1)) == 0, 0.0, prev)
                if (r0 + FFN_ROWS) % SEQ == 0:
                    nxt = jnp.where(((r0 + row) & (seq_len - 1)) == seq_len - 1, 0.0, nxt)
                halves.append(prev * w0 + mid * w1 + nxt * w2 + bias)
            act_ref[r0:r0 + FFN_ROWS, :] = (_silu(halves[0]) * halves[1]).astype(BF16)

    def down(t):
        r0 = t * FFN_MM_ROWS
        o_ref[r0:r0 + FFN_MM_ROWS, :] += _bdot(act_ref[r0:r0 + FFN_MM_ROWS, :], wdn_ref[...])

    @pl.when(c == 0)
    def _():
        _norm_mod_rows(x_ref, g_ref, mod_ref, h_ref, 3, 4)
        zeros = jnp.zeros((FFN_PAD, 2 * fc), F32)
        u_ref[0:FFN_PAD, :] = zeros
        u_ref[FFN_PAD + TOK_TILE:, :] = zeros
        o_ref[...] = jnp.zeros_like(o_ref)

    wup_ref[:, :fc] = wg_ref[...].astype(BF16)
    wup_ref[:, fc:] = wv_ref[...].astype(BF16)
    wdn_ref[...] = wd_ref[...].astype(BF16)
    for s in range(n + 2):
        if s < n:
            up(u_ref, s)
        if 1 <= s <= n:
            conv_act(u_ref, s - 1)
        if s >= 2:
            down(s - 2)

    @pl.when(c == N_FF_CHUNKS - 1)
    def _():
        o_ref[...] = x_ref[...] + mod_ref[0, 5:6, :] * o_ref[...]


def _ffn(x, g, mods, layer, w_up, conv_w, conv_b, w_down, tile0=0, n_tiles=N_TOK_TILES):
    fc = FF_CHUNK
    ncb = N_FF_CHUNKS
    return pl.pallas_call(
        functools.partial(_ffn_kernel, tile0=tile0),
        grid=(n_tiles, ncb),
        in_specs=[pl.BlockSpec((TOK_TILE, D_MODEL), lambda i, c: (tile0 + i, 0)),
                  pl.BlockSpec((1, D_MODEL), lambda i, c: (0, 0)),
                  pl.BlockSpec((1, 6, D_MODEL), lambda i, c: (_mod_group_of_tile(tile0 + i), 0, 0)),
                  pl.BlockSpec((None, D_MODEL, fc), lambda i, c: (layer, 0, c)),
                  pl.BlockSpec((None, D_MODEL, fc), lambda i, c: (layer, 0, ncb + c)),
                  pl.BlockSpec((None, 3, fc), lambda i, c: (layer, 0, c)),
                  pl.BlockSpec((None, 3, fc), lambda i, c: (layer, 0, ncb + c)),
                  pl.BlockSpec((None, 1, fc), lambda i, c: (layer, 0, c)),
                  pl.BlockSpec((None, 1, fc), lambda i, c: (layer, 0, ncb + c)),
                  pl.BlockSpec((None, fc, D_MODEL), lambda i, c: (layer, c, 0))],
        out_specs=pl.BlockSpec((TOK_TILE, D_MODEL), lambda i, c: (i, 0)),
        out_shape=jax.ShapeDtypeStruct((n_tiles * TOK_TILE, D_MODEL), F32),
        scratch_shapes=[pltpu.VMEM((TOK_TILE, D_MODEL), BF16),
                        pltpu.VMEM((TOK_TILE + 2 * FFN_PAD, 2 * fc), F32),
                        pltpu.VMEM((TOK_TILE, fc), BF16),
                        pltpu.VMEM((D_MODEL, 2 * fc), BF16),
                        pltpu.VMEM((fc, D_MODEL), BF16)],
        compiler_params=_params(("arbitrary", "arbitrary")),
        name="conv_ffn",
    )(x, g.reshape(1, D_MODEL), mods, w_up, w_up, conv_w, conv_w,
      conv_b.reshape(DEPTH, 1, -1), conv_b.reshape(DEPTH, 1, -1), w_down)


NA_HB = 4
NA_LANES = NA_HB * NA_HD
NA_ROWS = DEC_SEQ // GRID_W
NA_KEYS = NA_WIN_R * GRID_W
PROMPT_SEQS = 4
NA_ROW_UNROLL = 16


def _store_heads(dst_ref, seq, x, n_heads, hd):
    for h in range(n_heads):
        dst_ref[seq, h] = x[:, h * hd:(h + 1) * hd]


def _na_prompt_kernel(q_ref, k_ref, v_ref, gq_ref, gk_ref, o_ref, kn_ref, vn_ref):
    scale = NA_HD ** -0.5 * LOG2E

    def chains(bound):
        shift = None if bound is None else _stacked_rows(bound, SEQ)
        for seq in range(PROMPT_SEQS):
            sl = slice(seq * SEQ, (seq + 1) * SEQ)
            q = _group_rms(q_ref[sl, :], gq_ref[...], NA_HD, sums_on_mxu=True) * scale
            k = _group_rms(k_ref[sl, :], gk_ref[...], NA_HD, sums_on_mxu=True)
            v = v_ref[sl, :]
            _store_heads(kn_ref, seq, k, NA_HB, NA_HD)
            _store_heads(vn_ref, seq, v, NA_HB, NA_HD)
            q4 = _stack_heads(q, NA_HB, NA_HD).astype(BF16)
            s = lax.dot_general(q4, k.astype(BF16), _NT, preferred_element_type=F32)
            (p,), inv = _softmax_parts([s], shift)
            o4 = _bdot(p.astype(BF16), v.astype(BF16)) * inv
            o_ref[sl, :] = _unstack_heads(o4, NA_HB, NA_HD).astype(BF16)

    _with_score_bound(scale * _rms_norm_bound(gq_ref, NA_HD, NA_HD)
                      * _rms_norm_bound(gk_ref, NA_HD, NA_HD), chains)


def _na_prompt(qkv, gq, gk):
    nb = NA_HEADS // NA_HB
    rows = PROMPT_SEQS * SEQ
    blk = lambda off: pl.BlockSpec((rows, NA_LANES), lambda b, j: (b, off + j))
    vec = pl.BlockSpec((1, NA_LANES), lambda b, j: (0, 0))
    cache = pl.BlockSpec((PROMPT_SEQS, NA_HB, SEQ, NA_HD), lambda b, j: (b, j, 0, 0))
    cache_shape = jax.ShapeDtypeStruct((BATCH, NA_HEADS, SEQ, NA_HD), F32)
    return pl.pallas_call(
        _na_prompt_kernel,
        grid=(BATCH // PROMPT_SEQS, nb),
        in_specs=[blk(0), blk(nb), blk(2 * nb), vec, vec],
        out_specs=[blk(0), cache, cache],
        out_shape=[jax.ShapeDtypeStruct((N_PROMPT, D_MODEL), BF16), cache_shape, cache_shape],
        compiler_params=_params(("arbitrary", "arbitrary")),
        name="na_prompt",
    )(qkv, qkv, qkv, jnp.tile(gq, NA_HB).reshape(1, -1), jnp.tile(gk, NA_HB).reshape(1, -1))


def _na_latent_kernel(q_ref, k_ref, v_ref, kc_ref, vc_ref, t_ref, gq_ref, gk_ref, o_ref,
                      qn_ref, kn_ref, vb_ref, kc4_ref, vc4_ref, bias_ref):
    scale = NA_HD ** -0.5 * LOG2E
    rows = 256

    def prep(r, carry):
        sl = pl.ds(pl.multiple_of(r * rows, rows), rows)
        qn_ref[sl, :] = (_group_rms(q_ref[sl, :], gq_ref[...], NA_HD, sums_on_mxu=True)
                         * scale).astype(BF16)
        kn_ref[sl, :] = _group_rms(k_ref[sl, :], gk_ref[...], NA_HD,
                                   sums_on_mxu=True).astype(BF16)
        vb_ref[sl, :] = v_ref[sl, :].astype(BF16)
        return carry

    lax.fori_loop(0, DEC_SEQ // rows, prep, 0, unroll=2)
    kc = kc_ref[0]
    kc4_ref[...] = kc.astype(BF16)
    vc4_ref[...] = vc_ref[0].astype(BF16)

    def attend(bound):
        for h in range(NA_HB):
            off = 0.0 if bound is None else bound[:, h:h + 1]
            for p in range(NA_WIN_R):
                for i in range(NA_WIN_R):
                    bias_ref[h, p, :, i * GRID_W:(i + 1) * GRID_W] = t_ref[h, p + i] - off
        shift = None if bound is None else _stacked_rows(bound, GRID_W)

        def row(r):
            kr0 = jnp.clip(r - NA_WIN_R // 2, 0, NA_ROWS - NA_WIN_R)
            pat = kr0 - r + NA_WIN_R - 1
            qs = pl.ds(pl.multiple_of(r * GRID_W, GRID_W), GRID_W)
            ks = pl.ds(pl.multiple_of(kr0 * GRID_W, GRID_W), NA_KEYS)
            q4 = _stack_heads(qn_ref[qs, :], NA_HB, NA_HD)
            s_loc = lax.dot_general(q4, kn_ref[ks, :], _NT, preferred_element_type=F32)
            s_loc = s_loc + jnp.concatenate([bias_ref[h, pat] for h in range(NA_HB)], axis=0)
            s_ctx = lax.dot_general(q4, kc4_ref[...], _NT, preferred_element_type=F32)
            if shift is None:
                (p_loc, p_ctx), inv = _softmax_parts([s_loc, s_ctx])
            else:
                p_loc, p_ctx = jnp.exp2(s_loc), jnp.exp2(s_ctx - shift)
                inv = 1.0 / (p_loc.sum(axis=-1, keepdims=True) + p_ctx.sum(axis=-1, keepdims=True))
            o4 = _bdot(p_loc.astype(BF16), vb_ref[ks, :]) + _bdot(p_ctx.astype(BF16), vc4_ref[...])
            o_ref[qs, :] = _unstack_heads(o4 * inv, NA_HB, NA_HD).astype(BF16)

        def rows_step(i, carry):
            for u in range(NA_ROW_UNROLL):
                row(i * NA_ROW_UNROLL + u)
            return carry

        lax.fori_loop(0, NA_ROWS // NA_ROW_UNROLL, rows_step, 0)

    qmax = scale * _rms_norm_bound(gq_ref, NA_HD, NA_HD)
    bias_max = jnp.concatenate(
        [t_ref[h].max(axis=0).max(axis=0, keepdims=True).max(axis=1, keepdims=True)
         for h in range(NA_HB)], axis=-1)
    _with_score_bound(
        jnp.maximum(qmax * _rms_norm_bound(gk_ref, NA_HD, NA_HD) + bias_max,
                    qmax * _max_head_norms(kc, NA_HD)), attend)


def _na_bias_blocks(bias_table):
    qc = jnp.arange(GRID_W)[:, None]
    kc = jnp.arange(GRID_W)[None, :]
    win0 = jnp.clip(qc - NA_WIN_C // 2, 0, GRID_W - NA_WIN_C)
    valid = (kc >= win0) & (kc < win0 + NA_WIN_C)
    n_co = bias_table.shape[-1]
    onehot = (kc - qc + NA_WIN_C - 1)[None] == jnp.arange(n_co)[:, None, None]
    t = jnp.einsum('hrd,dqk->hrqk', bias_table.astype(F32), onehot.astype(F32),
                   precision=lax.Precision.HIGHEST)
    return jnp.where(valid, t * LOG2E, NEG)


def _na_latent(qkv, cache_k, cache_v, bias_blocks, gq, gk):
    nb = NA_HEADS // NA_HB
    lat0 = N_PROMPT // DEC_SEQ
    blk = lambda off: pl.BlockSpec((DEC_SEQ, NA_LANES), lambda b, j: (lat0 + b, off + j))
    vec = pl.BlockSpec((1, NA_LANES), lambda b, j: (0, 0))
    cache = pl.BlockSpec((1, PAST_LEN, NA_LANES), lambda b, j: (b, 0, j))
    return pl.pallas_call(
        _na_latent_kernel,
        grid=(DEC_BATCH, nb),
        in_specs=[blk(0), blk(nb), blk(2 * nb), cache, cache,
                  pl.BlockSpec((NA_HB, 2 * NA_WIN_R - 1, GRID_W, GRID_W), lambda b, j: (j, 0, 0, 0)),
                  vec, vec],
        out_specs=pl.BlockSpec((DEC_SEQ, NA_LANES), lambda b, j: (b, j)),
        out_shape=jax.ShapeDtypeStruct((N_LATENT, D_MODEL), BF16),
        scratch_shapes=[pltpu.VMEM((DEC_SEQ, NA_LANES), BF16),
                        pltpu.VMEM((DEC_SEQ, NA_LANES), BF16),
                        pltpu.VMEM((DEC_SEQ, NA_LANES), BF16),
                        pltpu.VMEM((PAST_LEN, NA_LANES), BF16),
                        pltpu.VMEM((PAST_LEN, NA_LANES), BF16),
                        pltpu.VMEM((NA_HB, NA_WIN_R, GRID_W, NA_KEYS), F32)],
        compiler_params=_params(("arbitrary", "arbitrary")),
        name="na_latent",
    )(qkv, qkv, qkv, _tokens_first(cache_k), _tokens_first(cache_v), bias_blocks,
      jnp.tile(gq, NA_HB).reshape(1, -1), jnp.tile(gk, NA_HB).reshape(1, -1))


GLA_C = 128
GLA_SUB = 8
GLA_LEVELS = (64, 32, 16, 8)


def _split_hi_lo(x):
    hi = x.astype(BF16)
    lo = (x - hi.astype(F32)).astype(BF16)
    return jnp.concatenate([hi, lo], axis=1)


class _GlaMasks(NamedTuple):
    tri: jax.Array
    later: tuple
    sign: tuple
    pair: tuple
    diag: jax.Array


def _gla_masks(rev):
    c = GLA_C
    row = lax.broadcasted_iota(jnp.int32, (c, c), 0)
    col = lax.broadcasted_iota(jnp.int32, (c, c), 1)
    rid = lax.broadcasted_iota(jnp.int32, (c, GLA_DK), 0)
    causal = (col >= row) if rev else (col <= row)
    later, pair = [], []
    for m in GLA_LEVELS:
        later.append(((rid & m) == 0) if rev else ((rid & m) != 0))
        same = (row >> _log2(2 * m)) == (col >> _log2(2 * m))
        crossing = ((row & m) != (col & m))
        pair.append(same & crossing & causal)
    diag = ((row >> _log2(GLA_SUB)) == (col >> _log2(GLA_SUB))) & causal
    sign = tuple(jnp.where(l, 1.0, -1.0) for l in later)
    return _GlaMasks(jnp.where(causal, 1.0, 0.0).astype(BF16), tuple(later), sign, tuple(pair), diag)


def _gla_chunk(q, k, v, g, st_ref, rev, masks):
    c = GLA_C
    cs = _bdot(masks.tri, _split_hi_lo(g))
    b = cs[:, :GLA_DK] + cs[:, GLA_DK:]

    a = None
    for m, later, sign, pair in zip(GLA_LEVELS, masks.later, masks.sign, masks.pair):
        nblk = c // (2 * m)
        if rev:
            bnd = [b[j * 2 * m + m:j * 2 * m + m + 1] for j in range(nblk)]
        else:
            bnd = [b[j * 2 * m + m - 1:j * 2 * m + m] for j in range(nblk)]
        ref = jnp.concatenate([jnp.broadcast_to(x, (2 * m, GLA_DK)) for x in bnd], axis=0)
        x = (jnp.where(later, q, k) * jnp.exp((b - ref) * sign)).astype(BF16)
        blk = lax.dot_general(x, x, _NT, preferred_element_type=F32)
        a = jnp.where(pair, blk, 0.0 if a is None else a)

    nsub = c // GLA_SUB
    lane_c = lax.broadcasted_iota(jnp.int32, (GLA_SUB, c), 1)
    diag_rows = []
    for blk_i in range(nsub):
        r0 = blk_i * GLA_SUB
        qb = q[r0:r0 + GLA_SUB]
        bb = b[r0:r0 + GLA_SUB]
        acc = jnp.zeros((GLA_SUB, c), F32)
        for s in range(GLA_SUB):
            ks = k[r0 + s:r0 + s + 1]
            bs = b[r0 + s:r0 + s + 1]
            w = jnp.sum(qb * ks * jnp.exp(bb - bs), axis=-1, keepdims=True)
            acc = jnp.where(lane_c == r0 + s, w, acc)
        diag_rows.append(acc)
    a = jnp.where(masks.diag, jnp.concatenate(diag_rows, axis=0), a)

    st = st_ref[...]
    inter = lax.dot_general((q * jnp.exp(b)).astype(BF16), st.astype(BF16), _NT,
                            preferred_element_type=F32)
    o = inter + _bdot(a.astype(BF16), v.astype(BF16))

    btot = b[0:1] if rev else b[c - 1:c]
    kd = (k * jnp.exp(btot - b)).astype(BF16)
    st_ref[...] = st * jnp.exp(btot) + lax.dot_general(v.astype(BF16), kd, _TN,
                                                       preferred_element_type=F32)
    return o


def _gla_kernel(*refs, n_tok, has_state, hps):
    if has_state:
        (q_ref, k_ref, v_ref, r_ref, w2_ref, bg_ref, gn_ref, s0f_ref, s0b_ref,
         o_ref, lg_ref, of_ref, ob_ref, stf_ref, stb_ref) = refs
    else:
        (q_ref, k_ref, v_ref, r_ref, w2_ref, bg_ref, gn_ref,
         o_ref, sf_ref, sb_ref, lg_ref, of_ref, ob_ref, stf_ref, stb_ref) = refs
    nc = n_tok // GLA_C
    scale = GLA_DK ** -0.5

    rb = r_ref[...].astype(BF16)
    for z in range(2):
        x = _bdot(rb, w2_ref[z].astype(BF16)) + bg_ref[z]
        lg_ref[z] = (jnp.minimum(x, 0.0) - jnp.log1p(jnp.exp(-jnp.abs(x)))) * (1.0 / GLA_GATE_NORM)

    for hh in range(hps):
        if has_state:
            stf_ref[hh] = s0f_ref[0, hh].T
            stb_ref[hh] = s0b_ref[0, hh].T
        else:
            stf_ref[hh] = jnp.zeros((GLA_DV, GLA_DK), F32)
            stb_ref[hh] = jnp.zeros((GLA_DV, GLA_DK), F32)

    masks = {rev: _gla_masks(rev) for rev in (False, True)}

    def step(ci, carry):
        for hh in range(hps):
            kq = slice(hh * GLA_DK, (hh + 1) * GLA_DK)
            vv = slice(hh * GLA_DV, (hh + 1) * GLA_DV)
            for rev in (False, True):
                cc = (nc - 1 - ci) if rev else ci
                sl = pl.ds(pl.multiple_of(cc * GLA_C, GLA_C), GLA_C)
                o = _gla_chunk(q_ref[sl, kq] * scale, k_ref[sl, kq], v_ref[sl, vv],
                               lg_ref[1 if rev else 0, sl, kq],
                               (stb_ref if rev else stf_ref).at[hh], rev, masks[rev])
                (ob_ref if rev else of_ref)[sl, vv] = o
        return carry

    lax.fori_loop(0, nc, step, 0, unroll=2)

    for hh in range(hps):
        vv = slice(hh * GLA_DV, (hh + 1) * GLA_DV)
        o = of_ref[:, vv] + ob_ref[:, vv]
        o_ref[:, vv] = o * lax.rsqrt(jnp.mean(o * o, axis=-1, keepdims=True) + EPS) * gn_ref[...]
        if not has_state:
            sf_ref[0, hh] = stf_ref[hh].T
            sb_ref[0, hh] = stb_ref[hh].T


def _gla(proj, w2, bg, gnorm, n_seq, n_tok, row_block0, hps, states=None):
    kw, vw = hps * GLA_DK, hps * GLA_DV
    spec = lambda width, off: pl.BlockSpec((n_tok, width), lambda b, h: (row_block0 + b, off + h))
    in_specs = [spec(kw, 0), spec(kw, GLA_HK // kw), spec(vw, 2 * GLA_HK // vw),
                pl.BlockSpec((n_tok, 128), lambda b, h: (row_block0 + b, (2 * GLA_HK + 2 * GLA_HV) // 128)),
                pl.BlockSpec((2, 128, kw), lambda b, h: (0, 0, h)),
                pl.BlockSpec((2, 1, kw), lambda b, h: (0, 0, h)),
                pl.BlockSpec((1, GLA_DV), lambda b, h: (0, 0))]
    args = [proj, proj, proj, proj, w2, bg, gnorm.reshape(1, GLA_DV)]
    st_spec = pl.BlockSpec((1, hps, GLA_DK, GLA_DV), lambda b, h: (b, h, 0, 0))
    o_spec = pl.BlockSpec((n_tok, vw), lambda b, h: (b, h))
    o_shape = jax.ShapeDtypeStruct((n_seq * n_tok, GLA_HV), F32)
    if states is not None:
        in_specs += [st_spec, st_spec]
        args += list(states)
        out_specs, out_shape = o_spec, o_shape
    else:
        st_shape = jax.ShapeDtypeStruct((n_seq, GLA_HEADS, GLA_DK, GLA_DV), F32)
        out_specs, out_shape = [o_spec, st_spec, st_spec], [o_shape, st_shape, st_shape]
    return pl.pallas_call(
        functools.partial(_gla_kernel, n_tok=n_tok, has_state=states is not None, hps=hps),
        grid=(n_seq, GLA_HEADS // hps),
        in_specs=in_specs,
        out_specs=out_specs,
        out_shape=out_shape,
        scratch_shapes=[pltpu.VMEM((2, n_tok, kw), F32),
                        pltpu.VMEM((n_tok, vw), F32),
                        pltpu.VMEM((n_tok, vw), F32),
                        pltpu.VMEM((hps, GLA_DV, GLA_DK), F32),
                        pltpu.VMEM((hps, GLA_DV, GLA_DK), F32)],
        compiler_params=_params(("arbitrary", "arbitrary")),
        name="gla_latent" if states is not None else "gla_prompt",
    )(*args)


DIFF_HB = 2
DIFF_QL = DIFF_HB * DIFF_HD
DIFF_VL = DIFF_HB * 2 * DIFF_HD
DIFF_TQ = 512
DIFF_SUB = 128


def _diff_lambda(lam_ref):
    l = lam_ref[...]
    a = jnp.sum(l[0:1] * l[1:2], axis=-1, keepdims=True)
    b = jnp.sum(l[2:3] * l[3:4], axis=-1, keepdims=True)
    return jnp.exp(a) - jnp.exp(b) + DIFF_LAMBDA_INIT


def _diff_finish(ps, invs, lam, v, sn_ref):
    a = ps[0] * invs[0] - (lam * invs[1]) * ps[1]
    o2 = _bdot(a.astype(BF16), v)
    o = _unstack_heads(o2, DIFF_HB, 2 * DIFF_HD)
    return _group_rms(o, sn_ref[...], 2 * DIFF_HD) * (1.0 - DIFF_LAMBDA_INIT)


def _diff_prompt_kernel(q0_ref, q1_ref, k0_ref, k1_ref, v_ref, gq_ref, gk_ref, lam_ref, sn_ref,
                        o_ref, kn_ref, vn_ref):
    scale = DIFF_HD ** -0.5 * LOG2E
    lam = _diff_lambda(lam_ref)

    def chains(bound):
        shift = None if bound is None else _stacked_rows(bound, SEQ)
        for seq in range(PROMPT_SEQS):
            sl = slice(seq * SEQ, (seq + 1) * SEQ)
            ps, invs = [], []
            for comp, (q_ref, k_ref) in enumerate(((q0_ref, k0_ref), (q1_ref, k1_ref))):
                q = _group_rms(q_ref[sl, :], gq_ref[...], DIFF_HD, sums_on_mxu=True) * scale
                k = _group_rms(k_ref[sl, :], gk_ref[...], DIFF_HD, sums_on_mxu=True)
                for h in range(DIFF_HB):
                    kn_ref[seq, comp, h] = k[:, h * DIFF_HD:(h + 1) * DIFF_HD]
                q2 = _stack_heads(q, DIFF_HB, DIFF_HD).astype(BF16)
                s = lax.dot_general(q2, k.astype(BF16), _NT, preferred_element_type=F32)
                (p,), inv = _softmax_parts([s], shift)
                ps.append(p)
                invs.append(inv)
            v = v_ref[sl, :]
            _store_heads(vn_ref, seq, v, DIFF_HB, 2 * DIFF_HD)
            o_ref[sl, :] = _diff_finish(ps, invs, lam, v.astype(BF16), sn_ref).astype(BF16)

    _with_score_bound(scale * _rms_norm_bound(gq_ref, DIFF_HD, DIFF_HD)
                      * _rms_norm_bound(gk_ref, DIFF_HD, DIFF_HD), chains)


def _diff_prompt(qkv, gq, gk, lam, sub_norm):
    nb = DIFF_HEADS // DIFF_HB
    rows = PROMPT_SEQS * SEQ
    qk = lambda off: pl.BlockSpec((rows, DIFF_QL), lambda b, j: (b, off + j))
    vec = lambda n: pl.BlockSpec((1, n), lambda b, j: (0, 0))
    v_spec = pl.BlockSpec((rows, DIFF_VL), lambda b, j: (b, 2 * D_MODEL // DIFF_VL + j))
    kn_spec = pl.BlockSpec((PROMPT_SEQS, 2, DIFF_HB, SEQ, DIFF_HD), lambda b, j: (b, 0, j, 0, 0))
    vn_spec = pl.BlockSpec((PROMPT_SEQS, DIFF_HB, SEQ, 2 * DIFF_HD), lambda b, j: (b, j, 0, 0))
    return pl.pallas_call(
        _diff_prompt_kernel,
        grid=(BATCH // PROMPT_SEQS, nb),
        in_specs=[qk(0), qk(nb), qk(2 * nb), qk(3 * nb), v_spec, vec(DIFF_QL), vec(DIFF_QL),
                  pl.BlockSpec((4, DIFF_HD), lambda b, j: (0, 0)), vec(DIFF_VL)],
        out_specs=[pl.BlockSpec((rows, DIFF_VL), lambda b, j: (b, j)), kn_spec, vn_spec],
        out_shape=[jax.ShapeDtypeStruct((N_PROMPT, D_MODEL), BF16),
                   jax.ShapeDtypeStruct((BATCH, 2, DIFF_HEADS, SEQ, DIFF_HD), F32),
                   jax.ShapeDtypeStruct((BATCH, DIFF_HEADS, SEQ, 2 * DIFF_HD), F32)],
        compiler_params=_params(("arbitrary", "arbitrary")),
        name="diff_prompt",
    )(qkv, qkv, qkv, qkv, qkv, jnp.tile(gq, DIFF_HB).reshape(1, -1),
      jnp.tile(gk, DIFF_HB).reshape(1, -1), lam, jnp.tile(sub_norm, DIFF_HB).reshape(1, -1))


def _diff_latent_kernel(q0_ref, q1_ref, k0_ref, k1_ref, v_ref, kc0_ref, kc1_ref, vc_ref,
                        cos_ref, sin_ref, cosq_ref, sinq_ref, gq_ref, gk_ref, lam_ref, sn_ref,
                        o_ref, kb0_ref, kb1_ref, vb_ref, kmax_ref):
    scale = DIFF_HD ** -0.5 * LOG2E
    half = DIFF_HD // 4
    rows = 256

    @pl.when(pl.program_id(2) == 0)
    def _():
        for comp, (k_ref, kc_ref, kb_ref) in enumerate(
                ((k0_ref, kc0_ref, kb0_ref), (k1_ref, kc1_ref, kb1_ref))):
            kc = kc_ref[0]
            kb_ref[0:PAST_LEN, :] = kc.astype(BF16)
            kmax_ref[comp:comp + 1, :] = jnp.maximum(
                _max_head_norms(kc, DIFF_HD), _rms_norm_bound(gk_ref, DIFF_HD, DIFF_HD))

            def prep(r, carry):
                sl = pl.ds(pl.multiple_of(r * rows, rows), rows)
                k = _group_rms(k_ref[sl, :], gk_ref[...], DIFF_HD)
                k = _rope(k, cos_ref[sl, :], sin_ref[sl, :], half)
                kb_ref[pl.ds(pl.multiple_of(PAST_LEN + r * rows, rows), rows), :] = k.astype(BF16)
                return carry

            lax.fori_loop(0, DEC_SEQ // rows, prep, 0, unroll=2)
        vb_ref[0:PAST_LEN, :] = vc_ref[0].astype(BF16)
        vb_ref[PAST_LEN:, :] = v_ref[...].astype(BF16)

    lam = _diff_lambda(lam_ref)

    def chains(bound):
        for r0 in range(0, DIFF_TQ, DIFF_SUB):
            sl = slice(r0, r0 + DIFF_SUB)
            ps, invs = [], []
            for comp, (q_ref, kb_ref) in enumerate(((q0_ref, kb0_ref), (q1_ref, kb1_ref))):
                shift = None if bound is None else _stacked_rows(
                    bound[:, comp * DIFF_HB:(comp + 1) * DIFF_HB], DIFF_SUB)
                q = _group_rms(q_ref[sl, :], gq_ref[...], DIFF_HD)
                q = _rope(q, cosq_ref[sl, :], sinq_ref[sl, :], half) * scale
                q2 = _stack_heads(q, DIFF_HB, DIFF_HD).astype(BF16)
                s = lax.dot_general(q2, kb_ref[...], _NT, preferred_element_type=F32)
                (p,), inv = _softmax_parts([s], shift)
                ps.append(p)
                invs.append(inv)
            o_ref[sl, :] = _diff_finish(ps, invs, lam, vb_ref[...], sn_ref).astype(BF16)

    qmax = scale * _rms_norm_bound(gq_ref, DIFF_HD, DIFF_HD)
    _with_score_bound(jnp.concatenate([qmax * kmax_ref[0:1, :], qmax * kmax_ref[1:2, :]], axis=-1),
                      chains)


def _diff_latent(qkv, cache_k, cache_v, cos, sin, gq, gk, lam, sub_norm):
    nb = DIFF_HEADS // DIFF_HB
    nq = DEC_SEQ // DIFF_TQ
    q0 = N_PROMPT // DIFF_TQ
    lat0 = N_PROMPT // DEC_SEQ
    n_keys = PAST_LEN + DEC_SEQ
    q_spec = lambda off: pl.BlockSpec((DIFF_TQ, DIFF_QL), lambda b, j, t: (q0 + b * nq + t, off + j))
    k_spec = lambda off: pl.BlockSpec((DEC_SEQ, DIFF_QL), lambda b, j, t: (lat0 + b, off + j))
    v_spec = pl.BlockSpec((DEC_SEQ, DIFF_VL), lambda b, j, t: (lat0 + b, 2 * D_MODEL // DIFF_VL + j))
    kc_spec = lambda off: pl.BlockSpec((1, PAST_LEN, DIFF_QL), lambda b, j, t: (b, 0, off + j))
    vc_spec = pl.BlockSpec((1, PAST_LEN, DIFF_VL), lambda b, j, t: (b, 0, j))
    tab = pl.BlockSpec((DEC_SEQ, DIFF_QL), lambda b, j, t: (0, 0))
    tabq = pl.BlockSpec((DIFF_TQ, DIFF_QL), lambda b, j, t: (t, 0))
    vec = lambda n: pl.BlockSpec((1, n), lambda b, j, t: (0, 0))
    return pl.pallas_call(
        _diff_latent_kernel,
        grid=(DEC_BATCH, nb, nq),
        in_specs=[q_spec(0), q_spec(nb), k_spec(2 * nb), k_spec(3 * nb), v_spec,
                  kc_spec(0), kc_spec(nb), vc_spec, tab, tab, tabq, tabq,
                  vec(DIFF_QL), vec(DIFF_QL),
                  pl.BlockSpec((4, DIFF_HD), lambda b, j, t: (0, 0)), vec(DIFF_VL)],
        out_specs=pl.BlockSpec((DIFF_TQ, DIFF_VL), lambda b, j, t: (b * nq + t, j)),
        out_shape=jax.ShapeDtypeStruct((N_LATENT, D_MODEL), BF16),
        scratch_shapes=[pltpu.VMEM((n_keys, DIFF_QL), BF16),
                        pltpu.VMEM((n_keys, DIFF_QL), BF16),
                        pltpu.VMEM((n_keys, DIFF_VL), BF16),
                        pltpu.VMEM((2, DIFF_HB), F32)],
        compiler_params=_params(("arbitrary", "arbitrary", "arbitrary")),
        name="diff_latent",
    )(qkv, qkv, qkv, qkv, qkv, _tokens_first(cache_k), _tokens_first(cache_k),
      _tokens_first(cache_v), cos, sin, cos, sin,
      jnp.tile(gq, DIFF_HB).reshape(1, -1), jnp.tile(gk, DIFF_HB).reshape(1, -1), lam,
      jnp.tile(sub_norm, DIFF_HB).reshape(1, -1))


MLA_HB = 2
MLA_HL = 128
MLA_LANES = MLA_HB * MLA_HL
MLA_TQ = 1024
MLA_SUB = 128


def _mla_keys(kv, kr, gk, sums_on_mxu=False):
    lane = lax.broadcasted_iota(jnp.int32, kv.shape, 1)
    kr2 = jnp.concatenate([kr] * MLA_HB, axis=1)
    k = jnp.where((lane & (MLA_HL - 1)) < MLA_NOPE, kv, kr2)
    return _group_rms(k, gk, MLA_HL, n_real=MLA_QK, sums_on_mxu=sums_on_mxu)


def _mla_out(o2):
    tq = o2.shape[0] // MLA_HB
    oa = pltpu.roll(o2[0:tq, 0:MLA_HL], MLA_HL - MLA_V, axis=1)
    ob = o2[tq:, MLA_HL:]
    lane = lax.broadcasted_iota(jnp.int32, oa.shape, 1)
    return jnp.where(lane < MLA_V, oa, ob)


def _mla_prompt_kernel(q_ref, kv_ref, kr_ref, gq_ref, gk_ref, o_ref):
    scale = MLA_QK ** -0.5 * LOG2E

    def chains(bound):
        shift = None if bound is None else _stacked_rows(bound, SEQ)
        for seq in range(PROMPT_SEQS):
            sl = slice(seq * SEQ, (seq + 1) * SEQ)
            q = _group_rms(q_ref[sl, :], gq_ref[...], MLA_HL, n_real=MLA_QK) * scale
            kv = kv_ref[sl, :]
            k = _mla_keys(kv, kr_ref[sl, :], gk_ref[...])
            q2 = _stack_heads(q, MLA_HB, MLA_HL).astype(BF16)
            s = lax.dot_general(q2, k.astype(BF16), _NT, preferred_element_type=F32)
            (p,), inv = _softmax_parts([s], shift)
            o_ref[sl, :] = _mla_out(_bdot(p.astype(BF16), kv.astype(BF16)) * inv).astype(BF16)

    _with_score_bound(scale * _rms_norm_bound(gq_ref, MLA_QK, MLA_HL)
                      * _rms_norm_bound(gk_ref, MLA_QK, MLA_HL), chains)


def _mla_prompt(qp, kvp, low, gq, gk):
    nb = MLA_HEADS // MLA_HB
    rows = PROMPT_SEQS * SEQ
    blk = pl.BlockSpec((rows, MLA_LANES), lambda b, j: (b, j))
    vec = pl.BlockSpec((1, MLA_LANES), lambda b, j: (0, 0))
    return pl.pallas_call(
        _mla_prompt_kernel,
        grid=(BATCH // PROMPT_SEQS, nb),
        in_specs=[blk, blk, pl.BlockSpec((rows, MLA_HL), lambda b, j: (b, 0)),
                  vec, vec],
        out_specs=pl.BlockSpec((rows, MLA_HB * MLA_V), lambda b, j: (b, j)),
        out_shape=jax.ShapeDtypeStruct((N_PROMPT, MLA_HEADS * MLA_V), BF16),
        compiler_params=_params(("arbitrary", "arbitrary")),
        name="mla_prompt",
    )(qp, kvp, low, gq, gk)


def _mla_latent_kernel(q_ref, kv_ref, kr_ref, kvc_ref, krc_ref, cos_ref, sin_ref, cosq_ref,
                       sinq_ref, gq_ref, gk_ref, o_ref, kb_ref, vb_ref, kmax_ref):
    scale = MLA_QK ** -0.5 * LOG2E
    half = MLA_ROPE // 4
    rows = 256

    @pl.when(pl.program_id(2) == 0)
    def _():
        kvc = kvc_ref[...]
        kc = _mla_keys(kvc, krc_ref[...], gk_ref[...])
        kb_ref[0:PAST_LEN, :] = kc.astype(BF16)
        vb_ref[0:PAST_LEN, :] = kvc.astype(BF16)

        def prep(r, carry):
            sl = pl.ds(pl.multiple_of(r * rows, rows), rows)
            dst = pl.ds(pl.multiple_of(PAST_LEN + r * rows, rows), rows)
            kv = kv_ref[sl, :]
            k = _mla_keys(kv, kr_ref[sl, :], gk_ref[...], sums_on_mxu=True)
            k = _rope(k, cos_ref[sl, :], sin_ref[sl, :], half)
            kb_ref[dst, :] = k.astype(BF16)
            vb_ref[dst, :] = kv.astype(BF16)
            return carry

        lax.fori_loop(0, DEC_SEQ // rows, prep, 0, unroll=4)
        kmax_ref[...] = jnp.maximum(_max_head_norms(kc, MLA_HL),
                                    _rms_norm_bound(gk_ref, MLA_QK, MLA_HL))

    def chains(bound):
        shift = None if bound is None else _stacked_rows(bound, MLA_SUB)
        for r0 in range(0, MLA_TQ, MLA_SUB):
            sl = slice(r0, r0 + MLA_SUB)
            q = _group_rms(q_ref[sl, :], gq_ref[...], MLA_HL, n_real=MLA_QK)
            q = _rope(q, cosq_ref[sl, :], sinq_ref[sl, :], half) * scale
            q2 = _stack_heads(q, MLA_HB, MLA_HL).astype(BF16)
            s = lax.dot_general(q2, kb_ref[...], _NT, preferred_element_type=F32)
            (p,), inv = _softmax_parts([s], shift)
            o_ref[sl, :] = _mla_out(_bdot(p.astype(BF16), vb_ref[...]) * inv).astype(BF16)

    _with_score_bound(scale * _rms_norm_bound(gq_ref, MLA_QK, MLA_HL) * kmax_ref[...], chains)


def _mla_latent(qp, kvp, low, kvc, krc, cos, sin, gq, gk):
    nb = MLA_HEADS // MLA_HB
    nq = DEC_SEQ // MLA_TQ
    q0 = N_PROMPT // MLA_TQ
    lat0 = N_PROMPT // DEC_SEQ
    n_keys = PAST_LEN + DEC_SEQ
    tab = pl.BlockSpec((DEC_SEQ, MLA_LANES), lambda b, j, t: (0, 0))
    tabq = pl.BlockSpec((MLA_TQ, MLA_LANES), lambda b, j, t: (t, 0))
    vec = pl.BlockSpec((1, MLA_LANES), lambda b, j, t: (0, 0))
    return pl.pallas_call(
        _mla_latent_kernel,
        grid=(DEC_BATCH, nb, nq),
        in_specs=[pl.BlockSpec((MLA_TQ, MLA_LANES), lambda b, j, t: (q0 + b * nq + t, j)),
                  pl.BlockSpec((DEC_SEQ, MLA_LANES), lambda b, j, t: (lat0 + b, j)),
                  pl.BlockSpec((DEC_SEQ, MLA_HL), lambda b, j, t: (lat0 + b, 0)),
                  pl.BlockSpec((PAST_LEN, MLA_LANES), lambda b, j, t: (b, j)),
                  pl.BlockSpec((PAST_LEN, MLA_HL), lambda b, j, t: (b, 0)),
                  tab, tab, tabq, tabq, vec, vec],
        out_specs=pl.BlockSpec((MLA_TQ, MLA_HB * MLA_V), lambda b, j, t: (b * nq + t, j)),
        out_shape=jax.ShapeDtypeStruct((N_LATENT, MLA_HEADS * MLA_V), BF16),
        scratch_shapes=[pltpu.VMEM((n_keys, MLA_LANES), BF16),
                        pltpu.VMEM((n_keys, MLA_LANES), BF16),
                        pltpu.VMEM((1, MLA_HB), F32)],
        compiler_params=_params(("arbitrary", "arbitrary", "arbitrary")),
        name="mla_latent",
    )(qp, kvp, low, kvc, krc, cos, sin, cos, sin, gq, gk)


MLA_LOW_Q = 0
MLA_LOW_KV = 512
MLA_LOW_KR = 768
MLA_LOW_N = 896


def _axial_tables(n_tok, rdim):
    nf = rdim // 4
    freqs = ROPE_BASE ** (-jnp.arange(nf, dtype=F32) / nf)
    t = jnp.arange(n_tok)
    rowp = (t // GRID_W).astype(F32)
    colp = (t % GRID_W).astype(F32)
    ang = jnp.stack([rowp[:, None] * freqs, colp[:, None] * freqs], axis=1)
    cos, sin = jnp.cos(ang), jnp.sin(ang)
    cos_l = jnp.stack([cos, cos], axis=2).reshape(n_tok, rdim)
    sin_l = jnp.stack([-sin, sin], axis=2).reshape(n_tok, rdim)
    return cos_l, sin_l


def _diff_rope_tables():
    cos, sin = _axial_tables(DEC_SEQ, DIFF_HD)
    return jnp.tile(cos, (1, DIFF_HB)), jnp.tile(sin, (1, DIFF_HB))


def _mla_rope_tables():
    cos, sin = _axial_tables(DEC_SEQ, MLA_ROPE)
    ones = jnp.ones((DEC_SEQ, MLA_NOPE), F32)
    pad1 = jnp.ones((DEC_SEQ, MLA_HL - MLA_QK), F32)
    cos_h = jnp.concatenate([ones, cos, pad1], axis=1)
    sin_h = jnp.concatenate([0 * ones, sin, 0 * pad1], axis=1)
    return jnp.tile(cos_h, (1, MLA_HB)), jnp.tile(sin_h, (1, MLA_HB))


def _tokens_first(cache):
    b, h, l, d = cache.shape
    return jnp.transpose(cache, (0, 2, 1, 3)).reshape(b, l, h * d)


def _pad_heads(w, heads, hd, hl):
    k = w.shape[0]
    return jnp.pad(w.reshape(k, heads, hd), ((0, 0), (0, 0), (0, hl - hd))).reshape(k, heads * hl)


def kernel(x_prompt, x_sample, cache_l0_k, cache_l0_v, state_l1_fwd, state_l1_bwd, cache_l2_k,
           cache_l2_v, cache_l3_ckv, cache_l3_krope, c, c_ctx, ada_w, ada_b, norm_mix, norm_ffn,
           ffn_w_up, ffn_conv_w, ffn_conv_b, ffn_w_down, na_w_qkv, na_q_norm, na_k_norm, na_bias,
           na_w_o, gla_w_qkvg, gla_w_gate1, gla_w_gate2, gla_b_gate, gla_o_norm, gla_w_o,
           diff_w_qkv, diff_q_norm, diff_k_norm, diff_lambda, diff_sub_norm, diff_w_o, mla_w_dq,
           mla_q_a_norm, mla_w_uq, mla_w_dkv, mla_kv_a_norm, mla_w_ukv, mla_q_norm, mla_k_norm,
           mla_w_o):
    xr = _Rows(x_prompt.reshape(N_PROMPT, D_MODEL), x_sample.reshape(N_LATENT, D_MODEL), 0)
    cvecs = jnp.concatenate([c_ctx[None], c, jnp.zeros((5, D_MODEL), F32)], axis=0)
    mods_all = _ada_mods(cvecs, ada_w, ada_b)
    halves = lambda o_p, o_s: _Rows(o_p, o_s, 0)

    mods = mods_all[0]
    qkv = _norm_mod_proj(xr, norm_mix[0], mods, [na_w_qkv], "na_qkv")
    o_p, new_l0_k, new_l0_v = _na_prompt(qkv, na_q_norm, na_k_norm)
    o_s = _na_latent(qkv, cache_l0_k, cache_l0_v, _na_bias_blocks(na_bias), na_q_norm, na_k_norm)
    x = _out_proj_residual(xr, halves(o_p, o_s), mods, na_w_o, "na_out")
    ffn_weights = (ffn_w_up, ffn_conv_w, ffn_conv_b, ffn_w_down)
    x = _ffn(x, norm_ffn[0], mods, 0, *ffn_weights)
    xr = _one_array(x)

    mods = mods_all[1]
    w_decay = jnp.concatenate(
        [gla_w_gate1[0], gla_w_gate1[1],
         jnp.zeros((D_MODEL, 128 - 2 * GLA_GATE_RANK), F32)], axis=1)
    proj = _norm_mod_proj(xr, norm_mix[1], mods, [gla_w_qkvg, w_decay], "gla_proj")
    w2 = jnp.zeros((2, 128, GLA_HK), F32)
    w2 = w2.at[0, :GLA_GATE_RANK].set(gla_w_gate2[0])
    w2 = w2.at[1, GLA_GATE_RANK:2 * GLA_GATE_RANK].set(gla_w_gate2[1])
    bg = gla_b_gate.reshape(2, 1, GLA_HK)
    o_p, new_l1_fwd, new_l1_bwd = _gla(proj, w2, bg, gla_o_norm, BATCH, SEQ, 0, hps=4)
    o_s = _gla(proj, w2, bg, gla_o_norm, DEC_BATCH, DEC_SEQ, N_PROMPT // DEC_SEQ, hps=2,
               states=(state_l1_fwd, state_l1_bwd))
    x = _out_proj_residual(xr, halves(o_p, o_s), mods, gla_w_o, "gla_out",
                           gate=proj, gate_col_block=(2 * GLA_HK + GLA_HV) // GLA_HV)
    x = _ffn(x, norm_ffn[1], mods, 1, *ffn_weights)
    xr = _one_array(x)

    mods = mods_all[2]
    qkv = _norm_mod_proj(xr, norm_mix[2], mods, [diff_w_qkv], "diff_qkv")
    o_p, kn_p, new_l2_v = _diff_prompt(qkv, diff_q_norm, diff_k_norm, diff_lambda, diff_sub_norm)
    new_l2_k = kn_p.reshape(BATCH, 2 * DIFF_HEADS, SEQ, DIFF_HD)
    cos_d, sin_d = _diff_rope_tables()
    o_s = _diff_latent(qkv, cache_l2_k, cache_l2_v, cos_d, sin_d, diff_q_norm, diff_k_norm,
                       diff_lambda, diff_sub_norm)
    x = _out_proj_residual(xr, halves(o_p, o_s), mods, diff_w_o, "diff_out")
    x = _ffn(x, norm_ffn[2], mods, 2, *ffn_weights)
    xr = _one_array(x)

    mods = mods_all[3]
    zc = lambda n: jnp.zeros((D_MODEL, n), F32)
    w_low = jnp.concatenate(
        [mla_w_dq, zc(MLA_LOW_KV - MLA_Q_RANK), mla_w_dkv[:, :MLA_KV_RANK],
         zc(MLA_NOPE), mla_w_dkv[:, MLA_KV_RANK:], zc(MLA_HL - MLA_QK)], axis=1)
    w_uq = _pad_heads(mla_w_uq, MLA_HEADS, MLA_QK, MLA_HL)
    qp, kvp, ckv, kr = _mla_front(xr, norm_mix[3], mods, w_low, mla_q_a_norm, w_uq,
                                  mla_kv_a_norm, mla_w_ukv)
    kvc = _matmul(cache_l3_ckv.reshape(DEC_BATCH * PAST_LEN, MLA_KV_RANK), mla_w_ukv,
                  "mla_ukv_cache")
    krc = jnp.pad(cache_l3_krope.reshape(DEC_BATCH * PAST_LEN, MLA_ROPE),
                  ((0, 0), (MLA_NOPE, MLA_HL - MLA_QK)))
    pad_gain = lambda g: jnp.tile(jnp.pad(g, (0, MLA_HL - MLA_QK)), MLA_HB).reshape(1, -1)
    gq, gk = pad_gain(mla_q_norm), pad_gain(mla_k_norm)
    o_p = _mla_prompt(qp, kvp, kr, gq, gk)
    cos_m, sin_m = _mla_rope_tables()
    o_s = _mla_latent(qp, kvp, kr, kvc, krc, cos_m, sin_m, gq, gk)
    new_l3_ckv = ckv[:N_PROMPT].reshape(BATCH, SEQ, MLA_KV_RANK)
    new_l3_krope = kr[:N_PROMPT, MLA_NOPE:MLA_QK].reshape(BATCH, SEQ, MLA_ROPE)
    x = _out_proj_residual(xr, halves(o_p, o_s), mods, mla_w_o, "mla_out")
    n_pt = N_PROMPT // TOK_TILE
    ffn3 = functools.partial(_ffn, x, norm_ffn[3], mods, 3, *ffn_weights)
    y_prompt = ffn3(tile0=0, n_tiles=n_pt).reshape(BATCH, SEQ, D_MODEL)
    y_sample = ffn3(tile0=n_pt, n_tiles=N_TOK_TILES - n_pt).reshape(DEC_BATCH, DEC_SEQ, D_MODEL)
    return (y_prompt, y_sample, new_l0_k, new_l0_v, new_l1_fwd, new_l1_bwd, new_l2_k, new_l2_v,
            new_l3_ckv, new_l3_krope)
```

```python
import functools
import math
from typing import NamedTuple

import jax
import jax.numpy as jnp
from jax import lax
from jax.experimental import pallas as pl
from jax.experimental.pallas import tpu as pltpu

F32 = jnp.float32
BF16 = jnp.bfloat16

D_MODEL = 1024
BATCH = 16
SEQ = 256
DEPTH = 4
DEC_BATCH = 2
DEC_SEQ = 2048
PAST_LEN = 256
GRID_W = 64
D_FF = 2816
EPS = 1e-6
ROPE_BASE = 10000.0

NA_HEADS = 16
NA_HD = 64
NA_WIN_R = 8
NA_WIN_C = 16

GLA_HEADS = 4
GLA_DK = 128
GLA_DV = 256
GLA_HK = GLA_HEADS * GLA_DK
GLA_HV = GLA_HEADS * GLA_DV
GLA_GATE_RANK = 16
GLA_GATE_NORM = 16.0

DIFF_HEADS = 8
DIFF_HD = 64
DIFF_LAMBDA_INIT = 0.8 - 0.6 * math.exp(-0.3 * 2)

MLA_HEADS = 16
MLA_Q_RANK = 384
MLA_KV_RANK = 256
MLA_NOPE = 64
MLA_ROPE = 32
MLA_V = 64
MLA_QK = MLA_NOPE + MLA_ROPE

N_PROMPT = BATCH * SEQ
N_LATENT = DEC_BATCH * DEC_SEQ
N_TOK = N_PROMPT + N_LATENT
TOK_TILE = 2048
N_TOK_TILES = N_TOK // TOK_TILE
FF_CHUNK = 256
N_FF_CHUNKS = D_FF // FF_CHUNK
NEG = -1e30
LOG2E = math.log2(math.e)
SAFE_SCORE_BOUND = 48.0

VMEM_LIMIT = 56 * 1024 * 1024

_NT = (((1,), (1,)), ((), ()))
_TN = (((0,), (0,)), ((), ()))


def _params(sem, vmem=VMEM_LIMIT):
    return pltpu.CompilerParams(dimension_semantics=sem, vmem_limit_bytes=vmem)


def _log2(n):
    assert n & (n - 1) == 0
    return n.bit_length() - 1


def _silu(x):
    return x / (1.0 + jnp.exp(-x))


def _bdot(a, b):
    return jnp.dot(a, b, preferred_element_type=F32)


def _softmax_parts(parts, shift=None):
    m = shift
    if m is None:
        m = parts[0].max(axis=-1, keepdims=True)
        for s in parts[1:]:
            m = jnp.maximum(m, s.max(axis=-1, keepdims=True))
    ps = [jnp.exp2(s - m) for s in parts]
    l = ps[0].sum(axis=-1, keepdims=True)
    for p in ps[1:]:
        l = l + p.sum(axis=-1, keepdims=True)
    return ps, 1.0 / l


def _group_rms(x, gain, group, n_real=None, sums_on_mxu=False):
    lanes = x.shape[-1]
    n_real = n_real or group
    x2 = x * x
    if group == lanes:
        ms = jnp.sum(x2, axis=-1, keepdims=True)
    elif not sums_on_mxu:
        gid = lax.broadcasted_iota(jnp.int32, x.shape, 1) >> _log2(group)
        ms = jnp.zeros_like(x)
        for i in range(lanes // group):
            sel = gid == i
            si = jnp.sum(jnp.where(sel, x2, 0.0), axis=-1, keepdims=True)
            ms = jnp.where(sel, si, ms)
    else:
        r = lax.broadcasted_iota(jnp.int32, (lanes, lanes), 0) >> _log2(group)
        c = lax.broadcasted_iota(jnp.int32, (lanes, lanes), 1) >> _log2(group)
        ones = jnp.where(r == c, 1.0, 0.0).astype(BF16)
        hi = x2.astype(BF16)
        lo = (x2 - hi.astype(F32)).astype(BF16)
        ms = _bdot(hi, ones) + _bdot(lo, ones)
    return x * lax.rsqrt(ms * (1.0 / n_real) + EPS) * gain


def _with_score_bound(bound, body):
    ok = bound.max() < SAFE_SCORE_BOUND
    pl.when(ok)(lambda: body(bound))
    pl.when(jnp.logical_not(ok))(lambda: body(None))


def _rms_norm_bound(g_ref, n, head_lanes):
    g = jnp.abs(g_ref[...])
    heads = g.shape[-1] // head_lanes
    return n ** 0.5 * jnp.concatenate(
        [g[:, h * head_lanes:(h + 1) * head_lanes].max(axis=-1, keepdims=True)
         for h in range(heads)], axis=-1)


def _stacked_rows(per_head, rows):
    r = lax.broadcasted_iota(jnp.int32, (per_head.shape[-1] * rows, 1), 0)
    out = per_head[:, 0:1]
    for h in range(1, per_head.shape[-1]):
        out = jnp.where(r >= h * rows, per_head[:, h:h + 1], out)
    return out


def _max_head_norms(x, head_lanes):
    hid = lax.broadcasted_iota(jnp.int32, x.shape, 1) >> _log2(head_lanes)
    x2 = x * x
    sq = [jnp.sum(jnp.where(hid == h, x2, 0.0), axis=-1, keepdims=True).max(axis=0, keepdims=True)
          for h in range(x.shape[-1] // head_lanes)]
    return jnp.sqrt(jnp.concatenate(sq, axis=-1))


def _rope(x, cos, sin, half):
    lanes = x.shape[-1]
    lane = lax.broadcasted_iota(jnp.int32, x.shape, 1)
    up = pltpu.roll(x, lanes - half, axis=1)
    dn = pltpu.roll(x, half, axis=1)
    swapped = jnp.where((lane & (2 * half - 1)) < half, up, dn)
    return x * cos + swapped * sin


def _stack_heads(q, n_heads, head_lanes):
    hid = lax.broadcasted_iota(jnp.int32, q.shape, 1) >> _log2(head_lanes)
    zero = jnp.zeros_like(q)
    return jnp.concatenate([jnp.where(hid == i, q, zero) for i in range(n_heads)], axis=0)


def _unstack_heads(o, n_heads, head_lanes):
    rows = o.shape[0] // n_heads
    hid = lax.broadcasted_iota(jnp.int32, (rows, o.shape[1]), 1) >> _log2(head_lanes)
    out = o[0:rows]
    for i in range(1, n_heads):
        out = jnp.where(hid == i, o[i * rows:(i + 1) * rows], out)
    return out


ADA_TK = 256


def _ada_kernel(c_ref, w_ref, b_ref, o_ref):
    @pl.when(pl.program_id(1) == 0)
    def _():
        o_ref[0] = jnp.broadcast_to(b_ref[0], o_ref.shape[1:])

    s = _silu(c_ref[...])
    w = w_ref[0]
    s_hi, w_hi = s.astype(BF16), w.astype(BF16)
    s_lo = (s - s_hi.astype(F32)).astype(BF16)
    w_lo = (w - w_hi.astype(F32)).astype(BF16)
    both = _bdot(jnp.concatenate([s_hi, s_lo], axis=0), w_hi)
    o_ref[0] += both[0:8] + both[8:16] + _bdot(s_hi, w_lo)


def _ada_mods(cvecs, ada_w, ada_b):
    n = 6 * D_MODEL
    out = pl.pallas_call(
        _ada_kernel,
        grid=(DEPTH, D_MODEL // ADA_TK),
        in_specs=[pl.BlockSpec((8, ADA_TK), lambda l, k: (0, k)),
                  pl.BlockSpec((1, ADA_TK, n), lambda l, k: (l, k, 0)),
                  pl.BlockSpec((1, 1, n), lambda l, k: (l, 0, 0))],
        out_specs=pl.BlockSpec((1, 8, n), lambda l, k: (l, 0, 0)),
        out_shape=jax.ShapeDtypeStruct((DEPTH, 8, 6 * D_MODEL), F32),
        compiler_params=_params(("arbitrary", "arbitrary")),
        name="ada_mod",
    )(cvecs, ada_w, ada_b.reshape(DEPTH, 1, 6 * D_MODEL))
    return out.reshape(DEPTH, 8, 6, D_MODEL)[:, :3]


def _mod_group_of_tile(i):
    return jnp.maximum(i - (N_PROMPT // TOK_TILE - 1), 0)


def _norm_mod_rows(x_ref, g_ref, mod_ref, h_ref, shift_idx, scale_idx, rows=64):
    g = g_ref[...]
    sc = 1.0 + mod_ref[0, scale_idx:scale_idx + 1, :]
    sh = mod_ref[0, shift_idx:shift_idx + 1, :]

    def body(r, carry):
        sl = pl.ds(pl.multiple_of(r * rows, rows), rows)
        xf = x_ref[sl, :]
        ms = jnp.mean(xf * xf, axis=-1, keepdims=True)
        y = xf * lax.rsqrt(ms + EPS) * g
        h_ref[sl, :] = (y * sc + sh).astype(BF16)
        return carry

    lax.fori_loop(0, x_ref.shape[0] // rows, body, 0, unroll=4)


ROW_TM = 512


class _Rows(NamedTuple):
    prompt: jax.Array
    latent: jax.Array
    latent_row0: int


def _one_array(x):
    return _Rows(x, x, N_PROMPT)


def _row_specs(rows, width, col_block=0):
    n_p = N_PROMPT // ROW_TM
    l0 = rows.latent_row0 // ROW_TM
    return [pl.BlockSpec((ROW_TM, width), lambda t: (jnp.minimum(t, n_p - 1), col_block)),
            pl.BlockSpec((ROW_TM, width), lambda t: (l0 + jnp.maximum(t - n_p, 0), col_block))]


def _row_group(t):
    first_latent = N_PROMPT // ROW_TM
    return jnp.where(t < first_latent, 0, 1 + (t - first_latent) // (DEC_SEQ // ROW_TM))


def _is_prompt_tile():
    return pl.program_id(0) < N_PROMPT // ROW_TM


def _proj_kernel(xp_ref, xl_ref, g_ref, mod_ref, *refs):
    *w_refs, o_ref, h_ref, wb_ref = refs

    @pl.when(pl.program_id(0) == 0)
    def _():
        off = 0
        for w_ref in w_refs:
            wb_ref[:, off:off + w_ref.shape[1]] = w_ref[...].astype(BF16)
            off += w_ref.shape[1]

    is_prompt = _is_prompt_tile()
    g = g_ref[...]
    sc = 1.0 + mod_ref[0, 1:2, :]
    sh = mod_ref[0, 0:1, :]
    rows = 64
    part = ROW_TM // 2
    for p0 in range(0, ROW_TM, part):
        for r0 in range(p0, p0 + part, rows):
            sl = slice(r0, r0 + rows)
            xf = jnp.where(is_prompt, xp_ref[sl, :], xl_ref[sl, :])
            ms = jnp.mean(xf * xf, axis=-1, keepdims=True)
            y = xf * lax.rsqrt(ms + EPS) * g
            h_ref[sl, :] = (y * sc + sh).astype(BF16)
        o_ref[p0:p0 + part, :] = _bdot(h_ref[p0:p0 + part, :], wb_ref[...])


def _norm_mod_proj(x, g, mods, ws, name):
    n = sum(w.shape[1] for w in ws)
    return pl.pallas_call(
        _proj_kernel,
        grid=(N_TOK // ROW_TM,),
        in_specs=_row_specs(x, D_MODEL) + [
            pl.BlockSpec((1, D_MODEL), lambda t: (0, 0)),
            pl.BlockSpec((1, 6, D_MODEL), lambda t: (_row_group(t), 0, 0))] + [
            pl.BlockSpec(w.shape, lambda t: (0, 0), pipeline_mode=pl.Buffered(1)) for w in ws],
        out_specs=pl.BlockSpec((ROW_TM, n), lambda t: (t, 0)),
        out_shape=jax.ShapeDtypeStruct((N_TOK, n), F32),
        scratch_shapes=[pltpu.VMEM((ROW_TM, D_MODEL), BF16),
                        pltpu.VMEM((D_MODEL, n), BF16)],
        compiler_params=_params(("arbitrary",)),
        name=name,
    )(x.prompt, x.latent, g.reshape(1, D_MODEL), mods, *ws)


def _rms(a, g):
    return a * lax.rsqrt(jnp.mean(a * a, axis=-1, keepdims=True) + EPS) * g


def _mla_front_kernel(xp_ref, xl_ref, g_ref, mod_ref, wl_ref, gqa_ref, wuq_ref, gkv_ref, wukv_ref,
                      qp_ref, kvp_ref, ckv_ref, kr_ref, h_ref, wlb_ref, wuqb_ref, wukvb_ref):
    @pl.when(pl.program_id(0) == 0)
    def _():
        wlb_ref[...] = wl_ref[...].astype(BF16)
        wuqb_ref[...] = wuq_ref[...].astype(BF16)
        wukvb_ref[...] = wukv_ref[...].astype(BF16)

    is_prompt = _is_prompt_tile()
    g = g_ref[...]
    sc = 1.0 + mod_ref[0, 1:2, :]
    sh = mod_ref[0, 0:1, :]
    rows = 64
    part = ROW_TM // 2
    for p0 in range(0, ROW_TM, part):
        for r0 in range(p0, p0 + part, rows):
            sl = slice(r0, r0 + rows)
            xf = jnp.where(is_prompt, xp_ref[sl, :], xl_ref[sl, :])
            h_ref[sl, :] = (_rms(xf, g) * sc + sh).astype(BF16)
        ps = slice(p0, p0 + part)
        low = _bdot(h_ref[ps, :], wlb_ref[...])
        qa = _rms(low[:, MLA_LOW_Q:MLA_LOW_Q + MLA_Q_RANK], gqa_ref[...])
        qp_ref[ps, :] = _bdot(qa.astype(BF16), wuqb_ref[...])
        ckv = _rms(low[:, MLA_LOW_KV:MLA_LOW_KV + MLA_KV_RANK], gkv_ref[...])
        ckv_ref[ps, :] = ckv
        kvp_ref[ps, :] = _bdot(ckv.astype(BF16), wukvb_ref[...])
        kr_ref[ps, :] = low[:, MLA_LOW_KR:MLA_LOW_KR + MLA_HL]


def _mla_front(x, g, mods, w_low, g_qa, w_uq, g_kva, w_ukv):
    n_q, n_kv = w_uq.shape[1], w_ukv.shape[1]
    const = lambda shape: pl.BlockSpec(shape, lambda t: (0, 0), pipeline_mode=pl.Buffered(1))
    out = lambda n: pl.BlockSpec((ROW_TM, n), lambda t: (t, 0))
    shape = lambda n: jax.ShapeDtypeStruct((N_TOK, n), F32)
    return pl.pallas_call(
        _mla_front_kernel,
        grid=(N_TOK // ROW_TM,),
        in_specs=_row_specs(x, D_MODEL) + [
            pl.BlockSpec((1, D_MODEL), lambda t: (0, 0)),
            pl.BlockSpec((1, 6, D_MODEL), lambda t: (_row_group(t), 0, 0)),
            const(w_low.shape), const((1, MLA_Q_RANK)), const(w_uq.shape),
            const((1, MLA_KV_RANK)), const(w_ukv.shape)],
        out_specs=[out(n_q), out(n_kv), out(MLA_KV_RANK), out(MLA_HL)],
        out_shape=[shape(n_q), shape(n_kv), shape(MLA_KV_RANK), shape(MLA_HL)],
        scratch_shapes=[pltpu.VMEM((ROW_TM, D_MODEL), BF16),
                        pltpu.VMEM(w_low.shape, BF16),
                        pltpu.VMEM(w_uq.shape, BF16),
                        pltpu.VMEM(w_ukv.shape, BF16)],
        compiler_params=_params(("arbitrary",)),
        name="mla_front",
    )(x.prompt, x.latent, g.reshape(1, D_MODEL), mods, w_low, g_qa.reshape(1, -1), w_uq,
      g_kva.reshape(1, -1), w_ukv)


def _matmul_kernel(a_ref, w_ref, o_ref):
    o_ref[...] = _bdot(a_ref[...].astype(BF16), w_ref[...].astype(BF16))


def _matmul(a, w, name):
    rows, n = a.shape[0], w.shape[1]
    return pl.pallas_call(
        _matmul_kernel,
        out_shape=jax.ShapeDtypeStruct((rows, n), F32),
        compiler_params=_params(()),
        name=name,
    )(a, w)


def _oproj_kernel(*refs, gated):
    if gated:
        xp_ref, xl_ref, ap_ref, al_ref, g_ref, mod_ref, w_ref, o_ref, wb_ref = refs
    else:
        xp_ref, xl_ref, ap_ref, al_ref, mod_ref, w_ref, o_ref, wb_ref = refs

    @pl.when(pl.program_id(0) == 0)
    def _():
        wb_ref[...] = w_ref[...].astype(BF16)

    is_prompt = _is_prompt_tile()
    a = jnp.where(is_prompt, ap_ref[...], al_ref[...])
    if gated:
        a = a * _silu(g_ref[...])
    y = _bdot(a.astype(BF16), wb_ref[...])
    x = jnp.where(is_prompt, xp_ref[...], xl_ref[...])
    o_ref[...] = x + mod_ref[0, 2:3, :] * y


def _out_proj_residual(x, a, mods, w, name, gate=None, gate_col_block=0):
    k = w.shape[0]
    in_specs = _row_specs(x, D_MODEL) + _row_specs(a, k)
    args = [x.prompt, x.latent, a.prompt, a.latent]
    if gate is not None:
        in_specs.append(pl.BlockSpec((ROW_TM, k), lambda t: (t, gate_col_block)))
        args.append(gate)
    in_specs += [pl.BlockSpec((1, 6, D_MODEL), lambda t: (_row_group(t), 0, 0)),
                 pl.BlockSpec((k, D_MODEL), lambda t: (0, 0))]
    args += [mods, w]
    return pl.pallas_call(
        functools.partial(_oproj_kernel, gated=gate is not None),
        grid=(N_TOK // ROW_TM,),
        in_specs=in_specs,
        out_specs=pl.BlockSpec((ROW_TM, D_MODEL), lambda t: (t, 0)),
        out_shape=jax.ShapeDtypeStruct((N_TOK, D_MODEL), F32),
        scratch_shapes=[pltpu.VMEM((k, D_MODEL), BF16)],
        compiler_params=_params(("arbitrary",)),
        name=name,
    )(*args)


FFN_MM_ROWS = 512
FFN_ROWS = 64
FFN_PAD = 8


def _ffn_kernel(x_ref, g_ref, mod_ref, wg_ref, wv_ref, cwg_ref, cwv_ref, cbg_ref, cbv_ref,
                wd_ref, o_ref, h_ref, u_ref, act_ref, wup_ref, wdn_ref, *, tile0):
    i = tile0 + pl.program_id(0)
    c = pl.program_id(1)
    fc = FF_CHUNK
    n = TOK_TILE // FFN_MM_ROWS
    seq_len = jnp.where(i < N_PROMPT // TOK_TILE, SEQ, DEC_SEQ)
    row = lax.broadcasted_iota(jnp.int32, (FFN_ROWS, 1), 0)
    taps = lambda cw_ref, cb_ref: [jnp.broadcast_to(cw_ref[j:j + 1, :], (FFN_ROWS, fc))
                                   for j in range(3)] + [
                                       jnp.broadcast_to(cb_ref[...], (FFN_ROWS, fc))]
    taps_g, taps_v = taps(cwg_ref, cbg_ref), taps(cwv_ref, cbv_ref)

    def up(u_ref, t):
        r0 = t * FFN_MM_ROWS
        u_ref[FFN_PAD + r0:FFN_PAD + r0 + FFN_MM_ROWS, :] = _bdot(
            h_ref[r0:r0 + FFN_MM_ROWS, :], wup_ref[...])

    def conv_act(u_ref, t):
        for r0 in range(t * FFN_MM_ROWS, (t + 1) * FFN_MM_ROWS, FFN_ROWS):
            halves = []
            for lo, (w0, w1, w2, bias) in ((0, taps_g), (fc, taps_v)):
                p0 = FFN_PAD + r0
                prev = u_ref[p0 - 1:p0 - 1 + FFN_ROWS, lo:lo + fc]
                mid = u_ref[p0:p0 + FFN_ROWS, lo:lo + fc]
                nxt = u_ref[p0 + 1:p0 + 1 + FFN_ROWS, lo:lo + fc]
                if r0 % SEQ == 0:
                    prev = jnp.where(((r0 + row) & (seq_len - 1)) == 0, 0.0, prev)
                if (r0 + FFN_ROWS) % SEQ == 0:
                    nxt = jnp.where(((r0 + row) & (seq_len - 1)) == seq_len - 1, 0.0, nxt)
                halves.append(prev * w0 + mid * w1 + nxt * w2 + bias)
            act_ref[r0:r0 + FFN_ROWS, :] = (_silu(halves[0]) * halves[1]).astype(BF16)

    def down(t):
        r0 = t * FFN_MM_ROWS
        o_ref[r0:r0 + FFN_MM_ROWS, :] += _bdot(act_ref[r0:r0 + FFN_MM_ROWS, :], wdn_ref[...])

    @pl.when(c == 0)
    def _():
        _norm_mod_rows(x_ref, g_ref, mod_ref, h_ref, 3, 4)
        zeros = jnp.zeros((FFN_PAD, 2 * fc), F32)
        u_ref[0:FFN_PAD, :] = zeros
        u_ref[FFN_PAD + TOK_TILE:, :] = zeros
        o_ref[...] = jnp.zeros_like(o_ref)

    wup_ref[:, :fc] = wg_ref[...].astype(BF16)
    wup_ref[:, fc:] = wv_ref[...].astype(BF16)
    wdn_ref[...] = wd_ref[...].astype(BF16)
    for s in range(n + 2):
        if s < n:
            up(u_ref, s)
        if 1 <= s <= n:
            conv_act(u_ref, s - 1)
        if s >= 2:
            down(s - 2)

    @pl.when(c == N_FF_CHUNKS - 1)
    def _():
        o_ref[...] = x_ref[...] + mod_ref[0, 5:6, :] * o_ref[...]


def _ffn(x, g, mods, layer, w_up, conv_w, conv_b, w_down, tile0=0, n_tiles=N_TOK_TILES):
    fc = FF_CHUNK
    ncb = N_FF_CHUNKS
    return pl.pallas_call(
        functools.partial(_ffn_kernel, tile0=tile0),
        grid=(n_tiles, ncb),
        in_specs=[pl.BlockSpec((TOK_TILE, D_MODEL), lambda i, c: (tile0 + i, 0)),
                  pl.BlockSpec((1, D_MODEL), lambda i, c: (0, 0)),
                  pl.BlockSpec((1, 6, D_MODEL), lambda i, c: (_mod_group_of_tile(tile0 + i), 0, 0)),
                  pl.BlockSpec((None, D_MODEL, fc), lambda i, c: (layer, 0, c)),
                  pl.BlockSpec((None, D_MODEL, fc), lambda i, c: (layer, 0, ncb + c)),
                  pl.BlockSpec((None, 3, fc), lambda i, c: (layer, 0, c)),
                  pl.BlockSpec((None, 3, fc), lambda i, c: (layer, 0, ncb + c)),
                  pl.BlockSpec((None, 1, fc), lambda i, c: (layer, 0, c)),
                  pl.BlockSpec((None, 1, fc), lambda i, c: (layer, 0, ncb + c)),
                  pl.BlockSpec((None, fc, D_MODEL), lambda i, c: (layer, c, 0))],
        out_specs=pl.BlockSpec((TOK_TILE, D_MODEL), lambda i, c: (i, 0)),
        out_shape=jax.ShapeDtypeStruct((n_tiles * TOK_TILE, D_MODEL), F32),
        scratch_shapes=[pltpu.VMEM((TOK_TILE, D_MODEL), BF16),
                        pltpu.VMEM((TOK_TILE + 2 * FFN_PAD, 2 * fc), F32),
                        pltpu.VMEM((TOK_TILE, fc), BF16),
                        pltpu.VMEM((D_MODEL, 2 * fc), BF16),
                        pltpu.VMEM((fc, D_MODEL), BF16)],
        compiler_params=_params(("arbitrary", "arbitrary")),
        name="conv_ffn",
    )(x, g.reshape(1, D_MODEL), mods, w_up, w_up, conv_w, conv_w,
      conv_b.reshape(DEPTH, 1, -1), conv_b.reshape(DEPTH, 1, -1), w_down)


NA_HB = 4
NA_LANES = NA_HB * NA_HD
NA_ROWS = DEC_SEQ // GRID_W
NA_KEYS = NA_WIN_R * GRID_W
PROMPT_SEQS = 4
NA_ROW_UNROLL = 16


def _store_heads(dst_ref, seq, x, n_heads, hd):
    for h in range(n_heads):
        dst_ref[seq, h] = x[:, h * hd:(h + 1) * hd]


def _na_prompt_kernel(q_ref, k_ref, v_ref, gq_ref, gk_ref, o_ref, kn_ref, vn_ref):
    scale = NA_HD ** -0.5 * LOG2E

    def chains(bound):
        shift = None if bound is None else _stacked_rows(bound, SEQ)
        for seq in range(PROMPT_SEQS):
            sl = slice(seq * SEQ, (seq + 1) * SEQ)
            q = _group_rms(q_ref[sl, :], gq_ref[...], NA_HD, sums_on_mxu=True) * scale
            k = _group_rms(k_ref[sl, :], gk_ref[...], NA_HD, sums_on_mxu=True)
            v = v_ref[sl, :]
            _store_heads(kn_ref, seq, k, NA_HB, NA_HD)
            _store_heads(vn_ref, seq, v, NA_HB, NA_HD)
            q4 = _stack_heads(q, NA_HB, NA_HD).astype(BF16)
            s = lax.dot_general(q4, k.astype(BF16), _NT, preferred_element_type=F32)
            (p,), inv = _softmax_parts([s], shift)
            o4 = _bdot(p.astype(BF16), v.astype(BF16)) * inv
            o_ref[sl, :] = _unstack_heads(o4, NA_HB, NA_HD).astype(BF16)

    _with_score_bound(scale * _rms_norm_bound(gq_ref, NA_HD, NA_HD)
                      * _rms_norm_bound(gk_ref, NA_HD, NA_HD), chains)


def _na_prompt(qkv, gq, gk):
    nb = NA_HEADS // NA_HB
    rows = PROMPT_SEQS * SEQ
    blk = lambda off: pl.BlockSpec((rows, NA_LANES), lambda b, j: (b, off + j))
    vec = pl.BlockSpec((1, NA_LANES), lambda b, j: (0, 0))
    cache = pl.BlockSpec((PROMPT_SEQS, NA_HB, SEQ, NA_HD), lambda b, j: (b, j, 0, 0))
    cache_shape = jax.ShapeDtypeStruct((BATCH, NA_HEADS, SEQ, NA_HD), F32)
    return pl.pallas_call(
        _na_prompt_kernel,
        grid=(BATCH // PROMPT_SEQS, nb),
        in_specs=[blk(0), blk(nb), blk(2 * nb), vec, vec],
        out_specs=[blk(0), cache, cache],
        out_shape=[jax.ShapeDtypeStruct((N_PROMPT, D_MODEL), BF16), cache_shape, cache_shape],
        compiler_params=_params(("arbitrary", "arbitrary")),
        name="na_prompt",
    )(qkv, qkv, qkv, jnp.tile(gq, NA_HB).reshape(1, -1), jnp.tile(gk, NA_HB).reshape(1, -1))


def _na_latent_kernel(q_ref, k_ref, v_ref, kc_ref, vc_ref, t_ref, gq_ref, gk_ref, o_ref,
                      qn_ref, kn_ref, vb_ref, kc4_ref, vc4_ref, bias_ref):
    scale = NA_HD ** -0.5 * LOG2E
    rows = 256

    def prep(r, carry):
        sl = pl.ds(pl.multiple_of(r * rows, rows), rows)
        qn_ref[sl, :] = (_group_rms(q_ref[sl, :], gq_ref[...], NA_HD, sums_on_mxu=True)
                         * scale).astype(BF16)
        kn_ref[sl, :] = _group_rms(k_ref[sl, :], gk_ref[...], NA_HD,
                                   sums_on_mxu=True).astype(BF16)
        vb_ref[sl, :] = v_ref[sl, :].astype(BF16)
        return carry

    lax.fori_loop(0, DEC_SEQ // rows, prep, 0, unroll=2)
    kc = kc_ref[0]
    kc4_ref[...] = kc.astype(BF16)
    vc4_ref[...] = vc_ref[0].astype(BF16)

    def attend(bound):
        for h in range(NA_HB):
            off = 0.0 if bound is None else bound[:, h:h + 1]
            for p in range(NA_WIN_R):
                for i in range(NA_WIN_R):
                    bias_ref[h, p, :, i * GRID_W:(i + 1) * GRID_W] = t_ref[h, p + i] - off
        shift = None if bound is None else _stacked_rows(bound, GRID_W)

        def row(r):
            kr0 = jnp.clip(r - NA_WIN_R // 2, 0, NA_ROWS - NA_WIN_R)
            pat = kr0 - r + NA_WIN_R - 1
            qs = pl.ds(pl.multiple_of(r * GRID_W, GRID_W), GRID_W)
            ks = pl.ds(pl.multiple_of(kr0 * GRID_W, GRID_W), NA_KEYS)
            q4 = _stack_heads(qn_ref[qs, :], NA_HB, NA_HD)
            s_loc = lax.dot_general(q4, kn_ref[ks, :], _NT, preferred_element_type=F32)
            s_loc = s_loc + jnp.concatenate([bias_ref[h, pat] for h in range(NA_HB)], axis=0)
            s_ctx = lax.dot_general(q4, kc4_ref[...], _NT, preferred_element_type=F32)
            if shift is None:
                (p_loc, p_ctx), inv = _softmax_parts([s_loc, s_ctx])
            else:
                p_loc, p_ctx = jnp.exp2(s_loc), jnp.exp2(s_ctx - shift)
                inv = 1.0 / (p_loc.sum(axis=-1, keepdims=True) + p_ctx.sum(axis=-1, keepdims=True))
            o4 = _bdot(p_loc.astype(BF16), vb_ref[ks, :]) + _bdot(p_ctx.astype(BF16), vc4_ref[...])
            o_ref[qs, :] = _unstack_heads(o4 * inv, NA_HB, NA_HD).astype(BF16)

        def rows_step(i, carry):
            for u in range(NA_ROW_UNROLL):
                row(i * NA_ROW_UNROLL + u)
            return carry

        lax.fori_loop(0, NA_ROWS // NA_ROW_UNROLL, rows_step, 0)

    qmax = scale * _rms_norm_bound(gq_ref, NA_HD, NA_HD)
    bias_max = jnp.concatenate(
        [t_ref[h].max(axis=0).max(axis=0, keepdims=True).max(axis=1, keepdims=True)
         for h in range(NA_HB)], axis=-1)
    _with_score_bound(
        jnp.maximum(qmax * _rms_norm_bound(gk_ref, NA_HD, NA_HD) + bias_max,
                    qmax * _max_head_norms(kc, NA_HD)), attend)


def _na_bias_blocks(bias_table):
    qc = jnp.arange(GRID_W)[:, None]
    kc = jnp.arange(GRID_W)[None, :]
    win0 = jnp.clip(qc - NA_WIN_C // 2, 0, GRID_W - NA_WIN_C)
    valid = (kc >= win0) & (kc < win0 + NA_WIN_C)
    n_co = bias_table.shape[-1]
    onehot = (kc - qc + NA_WIN_C - 1)[None] == jnp.arange(n_co)[:, None, None]
    t = jnp.einsum('hrd,dqk->hrqk', bias_table.astype(F32), onehot.astype(F32),
                   precision=lax.Precision.HIGHEST)
    return jnp.where(valid, t * LOG2E, NEG)


def _na_latent(qkv, cache_k, cache_v, bias_blocks, gq, gk):
    nb = NA_HEADS // NA_HB
    lat0 = N_PROMPT // DEC_SEQ
    blk = lambda off: pl.BlockSpec((DEC_SEQ, NA_LANES), lambda b, j: (lat0 + b, off + j))
    vec = pl.BlockSpec((1, NA_LANES), lambda b, j: (0, 0))
    cache = pl.BlockSpec((1, PAST_LEN, NA_LANES), lambda b, j: (b, 0, j))
    return pl.pallas_call(
        _na_latent_kernel,
        grid=(DEC_BATCH, nb),
        in_specs=[blk(0), blk(nb), blk(2 * nb), cache, cache,
                  pl.BlockSpec((NA_HB, 2 * NA_WIN_R - 1, GRID_W, GRID_W), lambda b, j: (j, 0, 0, 0)),
                  vec, vec],
        out_specs=pl.BlockSpec((DEC_SEQ, NA_LANES), lambda b, j: (b, j)),
        out_shape=jax.ShapeDtypeStruct((N_LATENT, D_MODEL), BF16),
        scratch_shapes=[pltpu.VMEM((DEC_SEQ, NA_LANES), BF16),
                        pltpu.VMEM((DEC_SEQ, NA_LANES), BF16),
                        pltpu.VMEM((DEC_SEQ, NA_LANES), BF16),
                        pltpu.VMEM((PAST_LEN, NA_LANES), BF16),
                        pltpu.VMEM((PAST_LEN, NA_LANES), BF16),
                        pltpu.VMEM((NA_HB, NA_WIN_R, GRID_W, NA_KEYS), F32)],
        compiler_params=_params(("arbitrary", "arbitrary")),
        name="na_latent",
    )(qkv, qkv, qkv, _tokens_first(cache_k), _tokens_first(cache_v), bias_blocks,
      jnp.tile(gq, NA_HB).reshape(1, -1), jnp.tile(gk, NA_HB).reshape(1, -1))


GLA_C = 128
GLA_SUB = 8
GLA_LEVELS = (64, 32, 16, 8)


def _split_hi_lo(x):
    hi = x.astype(BF16)
    lo = (x - hi.astype(F32)).astype(BF16)
    return jnp.concatenate([hi, lo], axis=1)


class _GlaMasks(NamedTuple):
    tri: jax.Array
    later: tuple
    sign: tuple
    pair: tuple
    diag: jax.Array


def _gla_masks(rev):
    c = GLA_C
    row = lax.broadcasted_iota(jnp.int32, (c, c), 0)
    col = lax.broadcasted_iota(jnp.int32, (c, c), 1)
    rid = lax.broadcasted_iota(jnp.int32, (c, GLA_DK), 0)
    causal = (col >= row) if rev else (col <= row)
    later, pair = [], []
    for m in GLA_LEVELS:
        later.append(((rid & m) == 0) if rev else ((rid & m) != 0))
        same = (row >> _log2(2 * m)) == (col >> _log2(2 * m))
        crossing = ((row & m) != (col & m))
        pair.append(same & crossing & causal)
    diag = ((row >> _log2(GLA_SUB)) == (col >> _log2(GLA_SUB))) & causal
    sign = tuple(jnp.where(l, 1.0, -1.0) for l in later)
    return _GlaMasks(jnp.where(causal, 1.0, 0.0).astype(BF16), tuple(later), sign, tuple(pair), diag)


def _gla_chunk(q, k, v, g, st_ref, rev, masks):
    c = GLA_C
    cs = _bdot(masks.tri, _split_hi_lo(g))
    b = cs[:, :GLA_DK] + cs[:, GLA_DK:]

    a = None
    for m, later, sign, pair in zip(GLA_LEVELS, masks.later, masks.sign, masks.pair):
        nblk = c // (2 * m)
        if rev:
            bnd = [b[j * 2 * m + m:j * 2 * m + m + 1] for j in range(nblk)]
        else:
            bnd = [b[j * 2 * m + m - 1:j * 2 * m + m] for j in range(nblk)]
        ref = jnp.concatenate([jnp.broadcast_to(x, (2 * m, GLA_DK)) for x in bnd], axis=0)
        x = (jnp.where(later, q, k) * jnp.exp((b - ref) * sign)).astype(BF16)
        blk = lax.dot_general(x, x, _NT, preferred_element_type=F32)
        a = jnp.where(pair, blk, 0.0 if a is None else a)

    nsub = c // GLA_SUB
    lane_c = lax.broadcasted_iota(jnp.int32, (GLA_SUB, c), 1)
    diag_rows = []
    for blk_i in range(nsub):
        r0 = blk_i * GLA_SUB
        qb = q[r0:r0 + GLA_SUB]
        bb = b[r0:r0 + GLA_SUB]
        acc = jnp.zeros((GLA_SUB, c), F32)
        for s in range(GLA_SUB):
            ks = k[r0 + s:r0 + s + 1]
            bs = b[r0 + s:r0 + s + 1]
            w = jnp.sum(qb * ks * jnp.exp(bb - bs), axis=-1, keepdims=True)
            acc = jnp.where(lane_c == r0 + s, w, acc)
        diag_rows.append(acc)
    a = jnp.where(masks.diag, jnp.concatenate(diag_rows, axis=0), a)

    st = st_ref[...]
    inter = lax.dot_general((q * jnp.exp(b)).astype(BF16), st.astype(BF16), _NT,
                            preferred_element_type=F32)
    o = inter + _bdot(a.astype(BF16), v.astype(BF16))

    btot = b[0:1] if rev else b[c - 1:c]
    kd = (k * jnp.exp(btot - b)).astype(BF16)
    st_ref[...] = st * jnp.exp(btot) + lax.dot_general(v.astype(BF16), kd, _TN,
                                                       preferred_element_type=F32)
    return o


def _gla_kernel(*refs, n_tok, has_state, hps):
    if has_state:
        (q_ref, k_ref, v_ref, r_ref, w2_ref, bg_ref, gn_ref, s0f_ref, s0b_ref,
         o_ref, lg_ref, of_ref, ob_ref, stf_ref, stb_ref) = refs
    else:
        (q_ref, k_ref, v_ref, r_ref, w2_ref, bg_ref, gn_ref,
         o_ref, sf_ref, sb_ref, lg_ref, of_ref, ob_ref, stf_ref, stb_ref) = refs
    nc = n_tok // GLA_C
    scale = GLA_DK ** -0.5

    rb = r_ref[...].astype(BF16)
    for z in range(2):
        x = _bdot(rb, w2_ref[z].astype(BF16)) + bg_ref[z]
        lg_ref[z] = (jnp.minimum(x, 0.0) - jnp.log1p(jnp.exp(-jnp.abs(x)))) * (1.0 / GLA_GATE_NORM)

    for hh in range(hps):
        if has_state:
            stf_ref[hh] = s0f_ref[0, hh].T
            stb_ref[hh] = s0b_ref[0, hh].T
        else:
            stf_ref[hh] = jnp.zeros((GLA_DV, GLA_DK), F32)
            stb_ref[hh] = jnp.zeros((GLA_DV, GLA_DK), F32)

    masks = {rev: _gla_masks(rev) for rev in (False, True)}

    def step(ci, carry):
        for hh in range(hps):
            kq = slice(hh * GLA_DK, (hh + 1) * GLA_DK)
            vv = slice(hh * GLA_DV, (hh + 1) * GLA_DV)
            for rev in (False, True):
                cc = (nc - 1 - ci) if rev else ci
                sl = pl.ds(pl.multiple_of(cc * GLA_C, GLA_C), GLA_C)
                o = _gla_chunk(q_ref[sl, kq] * scale, k_ref[sl, kq], v_ref[sl, vv],
                               lg_ref[1 if rev else 0, sl, kq],
                               (stb_ref if rev else stf_ref).at[hh], rev, masks[rev])
                (ob_ref if rev else of_ref)[sl, vv] = o
        return carry

    lax.fori_loop(0, nc, step, 0, unroll=2)

    for hh in range(hps):
        vv = slice(hh * GLA_DV, (hh + 1) * GLA_DV)
        o = of_ref[:, vv] + ob_ref[:, vv]
        o_ref[:, vv] = o * lax.rsqrt(jnp.mean(o * o, axis=-1, keepdims=True) + EPS) * gn_ref[...]
        if not has_state:
            sf_ref[0, hh] = stf_ref[hh].T
            sb_ref[0, hh] = stb_ref[hh].T


def _gla(proj, w2, bg, gnorm, n_seq, n_tok, row_block0, hps, states=None):
    kw, vw = hps * GLA_DK, hps * GLA_DV
    spec = lambda width, off: pl.BlockSpec((n_tok, width), lambda b, h: (row_block0 + b, off + h))
    in_specs = [spec(kw, 0), spec(kw, GLA_HK // kw), spec(vw, 2 * GLA_HK // vw),
                pl.BlockSpec((n_tok, 128), lambda b, h: (row_block0 + b, (2 * GLA_HK + 2 * GLA_HV) // 128)),
                pl.BlockSpec((2, 128, kw), lambda b, h: (0, 0, h)),
                pl.BlockSpec((2, 1, kw), lambda b, h: (0, 0, h)),
                pl.BlockSpec((1, GLA_DV), lambda b, h: (0, 0))]
    args = [proj, proj, proj, proj, w2, bg, gnorm.reshape(1, GLA_DV)]
    st_spec = pl.BlockSpec((1, hps, GLA_DK, GLA_DV), lambda b, h: (b, h, 0, 0))
    o_spec = pl.BlockSpec((n_tok, vw), lambda b, h: (b, h))
    o_shape = jax.ShapeDtypeStruct((n_seq * n_tok, GLA_HV), F32)
    if states is not None:
        in_specs += [st_spec, st_spec]
        args += list(states)
        out_specs, out_shape = o_spec, o_shape
    else:
        st_shape = jax.ShapeDtypeStruct((n_seq, GLA_HEADS, GLA_DK, GLA_DV), F32)
        out_specs, out_shape = [o_spec, st_spec, st_spec], [o_shape, st_shape, st_shape]
    return pl.pallas_call(
        functools.partial(_gla_kernel, n_tok=n_tok, has_state=states is not None, hps=hps),
        grid=(n_seq, GLA_HEADS // hps),
        in_specs=in_specs,
        out_specs=out_specs,
        out_shape=out_shape,
        scratch_shapes=[pltpu.VMEM((2, n_tok, kw), F32),
                        pltpu.VMEM((n_tok, vw), F32),
                        pltpu.VMEM((n_tok, vw), F32),
                        pltpu.VMEM((hps, GLA_DV, GLA_DK), F32),
                        pltpu.VMEM((hps, GLA_DV, GLA_DK), F32)],
        compiler_params=_params(("arbitrary", "arbitrary")),
        name="gla_latent" if states is not None else "gla_prompt",
    )(*args)


DIFF_HB = 2
DIFF_QL = DIFF_HB * DIFF_HD
DIFF_VL = DIFF_HB * 2 * DIFF_HD
DIFF_TQ = 512
DIFF_SUB = 128


def _diff_lambda(lam_ref):
    l = lam_ref[...]
    a = jnp.sum(l[0:1] * l[1:2], axis=-1, keepdims=True)
    b = jnp.sum(l[2:3] * l[3:4], axis=-1, keepdims=True)
    return jnp.exp(a) - jnp.exp(b) + DIFF_LAMBDA_INIT


def _diff_finish(ps, invs, lam, v, sn_ref):
    a = ps[0] * invs[0] - (lam * invs[1]) * ps[1]
    o2 = _bdot(a.astype(BF16), v)
    o = _unstack_heads(o2, DIFF_HB, 2 * DIFF_HD)
    return _group_rms(o, sn_ref[...], 2 * DIFF_HD) * (1.0 - DIFF_LAMBDA_INIT)


def _diff_prompt_kernel(q0_ref, q1_ref, k0_ref, k1_ref, v_ref, gq_ref, gk_ref, lam_ref, sn_ref,
                        o_ref, kn_ref, vn_ref):
    scale = DIFF_HD ** -0.5 * LOG2E
    lam = _diff_lambda(lam_ref)

    def chains(bound):
        shift = None if bound is None else _stacked_rows(bound, SEQ)
        for seq in range(PROMPT_SEQS):
            sl = slice(seq * SEQ, (seq + 1) * SEQ)
            ps, invs = [], []
            for comp, (q_ref, k_ref) in enumerate(((q0_ref, k0_ref), (q1_ref, k1_ref))):
                q = _group_rms(q_ref[sl, :], gq_ref[...], DIFF_HD, sums_on_mxu=True) * scale
                k = _group_rms(k_ref[sl, :], gk_ref[...], DIFF_HD, sums_on_mxu=True)
                for h in range(DIFF_HB):
                    kn_ref[seq, comp, h] = k[:, h * DIFF_HD:(h + 1) * DIFF_HD]
                q2 = _stack_heads(q, DIFF_HB, DIFF_HD).astype(BF16)
                s = lax.dot_general(q2, k.astype(BF16), _NT, preferred_element_type=F32)
                (p,), inv = _softmax_parts([s], shift)
                ps.append(p)
                invs.append(inv)
            v = v_ref[sl, :]
            _store_heads(vn_ref, seq, v, DIFF_HB, 2 * DIFF_HD)
            o_ref[sl, :] = _diff_finish(ps, invs, lam, v.astype(BF16), sn_ref).astype(BF16)

    _with_score_bound(scale * _rms_norm_bound(gq_ref, DIFF_HD, DIFF_HD)
                      * _rms_norm_bound(gk_ref, DIFF_HD, DIFF_HD), chains)


def _diff_prompt(qkv, gq, gk, lam, sub_norm):
    nb = DIFF_HEADS // DIFF_HB
    rows = PROMPT_SEQS * SEQ
    qk = lambda off: pl.BlockSpec((rows, DIFF_QL), lambda b, j: (b, off + j))
    vec = lambda n: pl.BlockSpec((1, n), lambda b, j: (0, 0))
    v_spec = pl.BlockSpec((rows, DIFF_VL), lambda b, j: (b, 2 * D_MODEL // DIFF_VL + j))
    kn_spec = pl.BlockSpec((PROMPT_SEQS, 2, DIFF_HB, SEQ, DIFF_HD), lambda b, j: (b, 0, j, 0, 0))
    vn_spec = pl.BlockSpec((PROMPT_SEQS, DIFF_HB, SEQ, 2 * DIFF_HD), lambda b, j: (b, j, 0, 0))
    return pl.pallas_call(
        _diff_prompt_kernel,
        grid=(BATCH // PROMPT_SEQS, nb),
        in_specs=[qk(0), qk(nb), qk(2 * nb), qk(3 * nb), v_spec, vec(DIFF_QL), vec(DIFF_QL),
                  pl.BlockSpec((4, DIFF_HD), lambda b, j: (0, 0)), vec(DIFF_VL)],
        out_specs=[pl.BlockSpec((rows, DIFF_VL), lambda b, j: (b, j)), kn_spec, vn_spec],
        out_shape=[jax.ShapeDtypeStruct((N_PROMPT, D_MODEL), BF16),
                   jax.ShapeDtypeStruct((BATCH, 2, DIFF_HEADS, SEQ, DIFF_HD), F32),
                   jax.ShapeDtypeStruct((BATCH, DIFF_HEADS, SEQ, 2 * DIFF_HD), F32)],
        compiler_params=_params(("arbitrary", "arbitrary")),
        name="diff_prompt",
    )(qkv, qkv, qkv, qkv, qkv, jnp.tile(gq, DIFF_HB).reshape(1, -1),
      jnp.tile(gk, DIFF_HB).reshape(1, -1), lam, jnp.tile(sub_norm, DIFF_HB).reshape(1, -1))


def _diff_latent_kernel(q0_ref, q1_ref, k0_ref, k1_ref, v_ref, kc0_ref, kc1_ref, vc_ref,
                        cos_ref, sin_ref, cosq_ref, sinq_ref, gq_ref, gk_ref, lam_ref, sn_ref,
                        o_ref, kb0_ref, kb1_ref, vb_ref, kmax_ref):
    scale = DIFF_HD ** -0.5 * LOG2E
    half = DIFF_HD // 4
    rows = 256

    @pl.when(pl.program_id(2) == 0)
    def _():
        for comp, (k_ref, kc_ref, kb_ref) in enumerate(
                ((k0_ref, kc0_ref, kb0_ref), (k1_ref, kc1_ref, kb1_ref))):
            kc = kc_ref[0]
            kb_ref[0:PAST_LEN, :] = kc.astype(BF16)
            kmax_ref[comp:comp + 1, :] = jnp.maximum(
                _max_head_norms(kc, DIFF_HD), _rms_norm_bound(gk_ref, DIFF_HD, DIFF_HD))

            def prep(r, carry):
                sl = pl.ds(pl.multiple_of(r * rows, rows), rows)
                k = _group_rms(k_ref[sl, :], gk_ref[...], DIFF_HD)
                k = _rope(k, cos_ref[sl, :], sin_ref[sl, :], half)
                kb_ref[pl.ds(pl.multiple_of(PAST_LEN + r * rows, rows), rows), :] = k.astype(BF16)
                return carry

            lax.fori_loop(0, DEC_SEQ // rows, prep, 0, unroll=2)
        vb_ref[0:PAST_LEN, :] = vc_ref[0].astype(BF16)
        vb_ref[PAST_LEN:, :] = v_ref[...].astype(BF16)

    lam = _diff_lambda(lam_ref)

    def chains(bound):
        for r0 in range(0, DIFF_TQ, DIFF_SUB):
            sl = slice(r0, r0 + DIFF_SUB)
            ps, invs = [], []
            for comp, (q_ref, kb_ref) in enumerate(((q0_ref, kb0_ref), (q1_ref, kb1_ref))):
                shift = None if bound is None else _stacked_rows(
                    bound[:, comp * DIFF_HB:(comp + 1) * DIFF_HB], DIFF_SUB)
                q = _group_rms(q_ref[sl, :], gq_ref[...], DIFF_HD)
                q = _rope(q, cosq_ref[sl, :], sinq_ref[sl, :], half) * scale
                q2 = _stack_heads(q, DIFF_HB, DIFF_HD).astype(BF16)
                s = lax.dot_general(q2, kb_ref[...], _NT, preferred_element_type=F32)
                (p,), inv = _softmax_parts([s], shift)
                ps.append(p)
                invs.append(inv)
            o_ref[sl, :] = _diff_finish(ps, invs, lam, vb_ref[...], sn_ref).astype(BF16)

    qmax = scale * _rms_norm_bound(gq_ref, DIFF_HD, DIFF_HD)
    _with_score_bound(jnp.concatenate([qmax * kmax_ref[0:1, :], qmax * kmax_ref[1:2, :]], axis=-1),
                      chains)


def _diff_latent(qkv, cache_k, cache_v, cos, sin, gq, gk, lam, sub_norm):
    nb = DIFF_HEADS // DIFF_HB
    nq = DEC_SEQ // DIFF_TQ
    q0 = N_PROMPT // DIFF_TQ
    lat0 = N_PROMPT // DEC_SEQ
    n_keys = PAST_LEN + DEC_SEQ
    q_spec = lambda off: pl.BlockSpec((DIFF_TQ, DIFF_QL), lambda b, j, t: (q0 + b * nq + t, off + j))
    k_spec = lambda off: pl.BlockSpec((DEC_SEQ, DIFF_QL), lambda b, j, t: (lat0 + b, off + j))
    v_spec = pl.BlockSpec((DEC_SEQ, DIFF_VL), lambda b, j, t: (lat0 + b, 2 * D_MODEL // DIFF_VL + j))
    kc_spec = lambda off: pl.BlockSpec((1, PAST_LEN, DIFF_QL), lambda b, j, t: (b, 0, off + j))
    vc_spec = pl.BlockSpec((1, PAST_LEN, DIFF_VL), lambda b, j, t: (b, 0, j))
    tab = pl.BlockSpec((DEC_SEQ, DIFF_QL), lambda b, j, t: (0, 0))
    tabq = pl.BlockSpec((DIFF_TQ, DIFF_QL), lambda b, j, t: (t, 0))
    vec = lambda n: pl.BlockSpec((1, n), lambda b, j, t: (0, 0))
    return pl.pallas_call(
        _diff_latent_kernel,
        grid=(DEC_BATCH, nb, nq),
        in_specs=[q_spec(0), q_spec(nb), k_spec(2 * nb), k_spec(3 * nb), v_spec,
                  kc_spec(0), kc_spec(nb), vc_spec, tab, tab, tabq, tabq,
                  vec(DIFF_QL), vec(DIFF_QL),
                  pl.BlockSpec((4, DIFF_HD), lambda b, j, t: (0, 0)), vec(DIFF_VL)],
        out_specs=pl.BlockSpec((DIFF_TQ, DIFF_VL), lambda b, j, t: (b * nq + t, j)),
        out_shape=jax.ShapeDtypeStruct((N_LATENT, D_MODEL), BF16),
        scratch_shapes=[pltpu.VMEM((n_keys, DIFF_QL), BF16),
                        pltpu.VMEM((n_keys, DIFF_QL), BF16),
                        pltpu.VMEM((n_keys, DIFF_VL), BF16),
                        pltpu.VMEM((2, DIFF_HB), F32)],
        compiler_params=_params(("arbitrary", "arbitrary", "arbitrary")),
        name="diff_latent",
    )(qkv, qkv, qkv, qkv, qkv, _tokens_first(cache_k), _tokens_first(cache_k),
      _tokens_first(cache_v), cos, sin, cos, sin,
      jnp.tile(gq, DIFF_HB).reshape(1, -1), jnp.tile(gk, DIFF_HB).reshape(1, -1), lam,
      jnp.tile(sub_norm, DIFF_HB).reshape(1, -1))


MLA_HB = 2
MLA_HL = 128
MLA_LANES = MLA_HB * MLA_HL
MLA_TQ = 1024
MLA_SUB = 128


def _mla_keys(kv, kr, gk, sums_on_mxu=False):
    lane = lax.broadcasted_iota(jnp.int32, kv.shape, 1)
    kr2 = jnp.concatenate([kr] * MLA_HB, axis=1)
    k = jnp.where((lane & (MLA_HL - 1)) < MLA_NOPE, kv, kr2)
    return _group_rms(k, gk, MLA_HL, n_real=MLA_QK, sums_on_mxu=sums_on_mxu)


def _mla_out(o2):
    tq = o2.shape[0] // MLA_HB
    oa = pltpu.roll(o2[0:tq, 0:MLA_HL], MLA_HL - MLA_V, axis=1)
    ob = o2[tq:, MLA_HL:]
    lane = lax.broadcasted_iota(jnp.int32, oa.shape, 1)
    return jnp.where(lane < MLA_V, oa, ob)


def _mla_prompt_kernel(q_ref, kv_ref, kr_ref, gq_ref, gk_ref, o_ref):
    scale = MLA_QK ** -0.5 * LOG2E

    def chains(bound):
        shift = None if bound is None else _stacked_rows(bound, SEQ)
        for seq in range(PROMPT_SEQS):
            sl = slice(seq * SEQ, (seq + 1) * SEQ)
            q = _group_rms(q_ref[sl, :], gq_ref[...], MLA_HL, n_real=MLA_QK) * scale
            kv = kv_ref[sl, :]
            k = _mla_keys(kv, kr_ref[sl, :], gk_ref[...])
            q2 = _stack_heads(q, MLA_HB, MLA_HL).astype(BF16)
            s = lax.dot_general(q2, k.astype(BF16), _NT, preferred_element_type=F32)
            (p,), inv = _softmax_parts([s], shift)
            o_ref[sl, :] = _mla_out(_bdot(p.astype(BF16), kv.astype(BF16)) * inv).astype(BF16)

    _with_score_bound(scale * _rms_norm_bound(gq_ref, MLA_QK, MLA_HL)
                      * _rms_norm_bound(gk_ref, MLA_QK, MLA_HL), chains)


def _mla_prompt(qp, kvp, low, gq, gk):
    nb = MLA_HEADS // MLA_HB
    rows = PROMPT_SEQS * SEQ
    blk = pl.BlockSpec((rows, MLA_LANES), lambda b, j: (b, j))
    vec = pl.BlockSpec((1, MLA_LANES), lambda b, j: (0, 0))
    return pl.pallas_call(
        _mla_prompt_kernel,
        grid=(BATCH // PROMPT_SEQS, nb),
        in_specs=[blk, blk, pl.BlockSpec((rows, MLA_HL), lambda b, j: (b, 0)),
                  vec, vec],
        out_specs=pl.BlockSpec((rows, MLA_HB * MLA_V), lambda b, j: (b, j)),
        out_shape=jax.ShapeDtypeStruct((N_PROMPT, MLA_HEADS * MLA_V), BF16),
        compiler_params=_params(("arbitrary", "arbitrary")),
        name="mla_prompt",
    )(qp, kvp, low, gq, gk)


def _mla_latent_kernel(q_ref, kv_ref, kr_ref, kvc_ref, krc_ref, cos_ref, sin_ref, cosq_ref,
                       sinq_ref, gq_ref, gk_ref, o_ref, kb_ref, vb_ref, kmax_ref):
    scale = MLA_QK ** -0.5 * LOG2E
    half = MLA_ROPE // 4
    rows = 256

    @pl.when(pl.program_id(2) == 0)
    def _():
        kvc = kvc_ref[...]
        kc = _mla_keys(kvc, krc_ref[...], gk_ref[...])
        kb_ref[0:PAST_LEN, :] = kc.astype(BF16)
        vb_ref[0:PAST_LEN, :] = kvc.astype(BF16)

        def prep(r, carry):
            sl = pl.ds(pl.multiple_of(r * rows, rows), rows)
            dst = pl.ds(pl.multiple_of(PAST_LEN + r * rows, rows), rows)
            kv = kv_ref[sl, :]
            k = _mla_keys(kv, kr_ref[sl, :], gk_ref[...], sums_on_mxu=True)
            k = _rope(k, cos_ref[sl, :], sin_ref[sl, :], half)
            kb_ref[dst, :] = k.astype(BF16)
            vb_ref[dst, :] = kv.astype(BF16)
            return carry

        lax.fori_loop(0, DEC_SEQ // rows, prep, 0, unroll=4)
        kmax_ref[...] = jnp.maximum(_max_head_norms(kc, MLA_HL),
                                    _rms_norm_bound(gk_ref, MLA_QK, MLA_HL))

    def chains(bound):
        shift = None if bound is None else _stacked_rows(bound, MLA_SUB)
        for r0 in range(0, MLA_TQ, MLA_SUB):
            sl = slice(r0, r0 + MLA_SUB)
            q = _group_rms(q_ref[sl, :], gq_ref[...], MLA_HL, n_real=MLA_QK)
            q = _rope(q, cosq_ref[sl, :], sinq_ref[sl, :], half) * scale
            q2 = _stack_heads(q, MLA_HB, MLA_HL).astype(BF16)
            s = lax.dot_general(q2, kb_ref[...], _NT, preferred_element_type=F32)
            (p,), inv = _softmax_parts([s], shift)
            o_ref[sl, :] = _mla_out(_bdot(p.astype(BF16), vb_ref[...]) * inv).astype(BF16)

    _with_score_bound(scale * _rms_norm_bound(gq_ref, MLA_QK, MLA_HL) * kmax_ref[...], chains)


def _mla_latent(qp, kvp, low, kvc, krc, cos, sin, gq, gk):
    nb = MLA_HEADS // MLA_HB
    nq = DEC_SEQ // MLA_TQ
    q0 = N_PROMPT // MLA_TQ
    lat0 = N_PROMPT // DEC_SEQ
    n_keys = PAST_LEN + DEC_SEQ
    tab = pl.BlockSpec((DEC_SEQ, MLA_LANES), lambda b, j, t: (0, 0))
    tabq = pl.BlockSpec((MLA_TQ, MLA_LANES), lambda b, j, t: (t, 0))
    vec = pl.BlockSpec((1, MLA_LANES), lambda b, j, t: (0, 0))
    return pl.pallas_call(
        _mla_latent_kernel,
        grid=(DEC_BATCH, nb, nq),
        in_specs=[pl.BlockSpec((MLA_TQ, MLA_LANES), lambda b, j, t: (q0 + b * nq + t, j)),
                  pl.BlockSpec((DEC_SEQ, MLA_LANES), lambda b, j, t: (lat0 + b, j)),
                  pl.BlockSpec((DEC_SEQ, MLA_HL), lambda b, j, t: (lat0 + b, 0)),
                  pl.BlockSpec((PAST_LEN, MLA_LANES), lambda b, j, t: (b, j)),
                  pl.BlockSpec((PAST_LEN, MLA_HL), lambda b, j, t: (b, 0)),
                  tab, tab, tabq, tabq, vec, vec],
        out_specs=pl.BlockSpec((MLA_TQ, MLA_HB * MLA_V), lambda b, j, t: (b * nq + t, j)),
        out_shape=jax.ShapeDtypeStruct((N_LATENT, MLA_HEADS * MLA_V), BF16),
        scratch_shapes=[pltpu.VMEM((n_keys, MLA_LANES), BF16),
                        pltpu.VMEM((n_keys, MLA_LANES), BF16),
                        pltpu.VMEM((1, MLA_HB), F32)],
        compiler_params=_params(("arbitrary", "arbitrary", "arbitrary")),
        name="mla_latent",
    )(qp, kvp, low, kvc, krc, cos, sin, cos, sin, gq, gk)


MLA_LOW_Q = 0
MLA_LOW_KV = 512
MLA_LOW_KR = 768
MLA_LOW_N = 896


def _axial_tables(n_tok, rdim):
    nf = rdim // 4
    freqs = ROPE_BASE ** (-jnp.arange(nf, dtype=F32) / nf)
    t = jnp.arange(n_tok)
    rowp = (t // GRID_W).astype(F32)
    colp = (t % GRID_W).astype(F32)
    ang = jnp.stack([rowp[:, None] * freqs, colp[:, None] * freqs], axis=1)
    cos, sin = jnp.cos(ang), jnp.sin(ang)
    cos_l = jnp.stack([cos, cos], axis=2).reshape(n_tok, rdim)
    sin_l = jnp.stack([-sin, sin], axis=2).reshape(n_tok, rdim)
    return cos_l, sin_l


def _diff_rope_tables():
    cos, sin = _axial_tables(DEC_SEQ, DIFF_HD)
    return jnp.tile(cos, (1, DIFF_HB)), jnp.tile(sin, (1, DIFF_HB))


def _mla_rope_tables():
    cos, sin = _axial_tables(DEC_SEQ, MLA_ROPE)
    ones = jnp.ones((DEC_SEQ, MLA_NOPE), F32)
    pad1 = jnp.ones((DEC_SEQ, MLA_HL - MLA_QK), F32)
    cos_h = jnp.concatenate([ones, cos, pad1], axis=1)
    sin_h = jnp.concatenate([0 * ones, sin, 0 * pad1], axis=1)
    return jnp.tile(cos_h, (1, MLA_HB)), jnp.tile(sin_h, (1, MLA_HB))


def _tokens_first(cache):
    b, h, l, d = cache.shape
    return jnp.transpose(cache, (0, 2, 1, 3)).reshape(b, l, h * d)


def _pad_heads(w, heads, hd, hl):
    k = w.shape[0]
    return jnp.pad(w.reshape(k, heads, hd), ((0, 0), (0, 0), (0, hl - hd))).reshape(k, heads * hl)


def kernel(x_prompt, x_sample, cache_l0_k, cache_l0_v, state_l1_fwd, state_l1_bwd, cache_l2_k,
           cache_l2_v, cache_l3_ckv, cache_l3_krope, c, c_ctx, ada_w, ada_b, norm_mix, norm_ffn,
           ffn_w_up, ffn_conv_w, ffn_conv_b, ffn_w_down, na_w_qkv, na_q_norm, na_k_norm, na_bias,
           na_w_o, gla_w_qkvg, gla_w_gate1, gla_w_gate2, gla_b_gate, gla_o_norm, gla_w_o,
           diff_w_qkv, diff_q_norm, diff_k_norm, diff_lambda, diff_sub_norm, diff_w_o, mla_w_dq,
           mla_q_a_norm, mla_w_uq, mla_w_dkv, mla_kv_a_norm, mla_w_ukv, mla_q_norm, mla_k_norm,
           mla_w_o):
    xr = _Rows(x_prompt.reshape(N_PROMPT, D_MODEL), x_sample.reshape(N_LATENT, D_MODEL), 0)
    cvecs = jnp.concatenate([c_ctx[None], c, jnp.zeros((5, D_MODEL), F32)], axis=0)
    mods_all = _ada_mods(cvecs, ada_w, ada_b)
    halves = lambda o_p, o_s: _Rows(o_p, o_s, 0)

    mods = mods_all[0]
    qkv = _norm_mod_proj(xr, norm_mix[0], mods, [na_w_qkv], "na_qkv")
    o_p, new_l0_k, new_l0_v = _na_prompt(qkv, na_q_norm, na_k_norm)
    o_s = _na_latent(qkv, cache_l0_k, cache_l0_v, _na_bias_blocks(na_bias), na_q_norm, na_k_norm)
    x = _out_proj_residual(xr, halves(o_p, o_s), mods, na_w_o, "na_out")
    ffn_weights = (ffn_w_up, ffn_conv_w, ffn_conv_b, ffn_w_down)
    x = _ffn(x, norm_ffn[0], mods, 0, *ffn_weights)
    xr = _one_array(x)

    mods = mods_all[1]
    w_decay = jnp.concatenate(
        [gla_w_gate1[0], gla_w_gate1[1],
         jnp.zeros((D_MODEL, 128 - 2 * GLA_GATE_RANK), F32)], axis=1)
    proj = _norm_mod_proj(xr, norm_mix[1], mods, [gla_w_qkvg, w_decay], "gla_proj")
    w2 = jnp.zeros((2, 128, GLA_HK), F32)
    w2 = w2.at[0, :GLA_GATE_RANK].set(gla_w_gate2[0])
    w2 = w2.at[1, GLA_GATE_RANK:2 * GLA_GATE_RANK].set(gla_w_gate2[1])
    bg = gla_b_gate.reshape(2, 1, GLA_HK)
    o_p, new_l1_fwd, new_l1_bwd = _gla(proj, w2, bg, gla_o_norm, BATCH, SEQ, 0, hps=4)
    o_s = _gla(proj, w2, bg, gla_o_norm, DEC_BATCH, DEC_SEQ, N_PROMPT // DEC_SEQ, hps=2,
               states=(state_l1_fwd, state_l1_bwd))
    x = _out_proj_residual(xr, halves(o_p, o_s), mods, gla_w_o, "gla_out",
                           gate=proj, gate_col_block=(2 * GLA_HK + GLA_HV) // GLA_HV)
    x = _ffn(x, norm_ffn[1], mods, 1, *ffn_weights)
    xr = _one_array(x)

    mods = mods_all[2]
    qkv = _norm_mod_proj(xr, norm_mix[2], mods, [diff_w_qkv], "diff_qkv")
    o_p, kn_p, new_l2_v = _diff_prompt(qkv, diff_q_norm, diff_k_norm, diff_lambda, diff_sub_norm)
    new_l2_k = kn_p.reshape(BATCH, 2 * DIFF_HEADS, SEQ, DIFF_HD)
    cos_d, sin_d = _diff_rope_tables()
    o_s = _diff_latent(qkv, cache_l2_k, cache_l2_v, cos_d, sin_d, diff_q_norm, diff_k_norm,
                       diff_lambda, diff_sub_norm)
    x = _out_proj_residual(xr, halves(o_p, o_s), mods, diff_w_o, "diff_out")
    x = _ffn(x, norm_ffn[2], mods, 2, *ffn_weights)
    xr = _one_array(x)

    mods = mods_all[3]
    zc = lambda n: jnp.zeros((D_MODEL, n), F32)
    w_low = jnp.concatenate(
        [mla_w_dq, zc(MLA_LOW_KV - MLA_Q_RANK), mla_w_dkv[:, :MLA_KV_RANK],
         zc(MLA_NOPE), mla_w_dkv[:, MLA_KV_RANK:], zc(MLA_HL - MLA_QK)], axis=1)
    w_uq = _pad_heads(mla_w_uq, MLA_HEADS, MLA_QK, MLA_HL)
    qp, kvp, ckv, kr = _mla_front(xr, norm_mix[3], mods, w_low, mla_q_a_norm, w_uq,
                                  mla_kv_a_norm, mla_w_ukv)
    kvc = _matmul(cache_l3_ckv.reshape(DEC_BATCH * PAST_LEN, MLA_KV_RANK), mla_w_ukv,
                  "mla_ukv_cache")
    krc = jnp.pad(cache_l3_krope.reshape(DEC_BATCH * PAST_LEN, MLA_ROPE),
                  ((0, 0), (MLA_NOPE, MLA_HL - MLA_QK)))
    pad_gain = lambda g: jnp.tile(jnp.pad(g, (0, MLA_HL - MLA_QK)), MLA_HB).reshape(1, -1)
    gq, gk = pad_gain(mla_q_norm), pad_gain(mla_k_norm)
    o_p = _mla_prompt(qp, kvp, kr, gq, gk)
    cos_m, sin_m = _mla_rope_tables()
    o_s = _mla_latent(qp, kvp, kr, kvc, krc, cos_m, sin_m, gq, gk)
    new_l3_ckv = ckv[:N_PROMPT].reshape(BATCH, SEQ, MLA_KV_RANK)
    new_l3_krope = kr[:N_PROMPT, MLA_NOPE:MLA_QK].reshape(BATCH, SEQ, MLA_ROPE)
    x = _out_proj_residual(xr, halves(o_p, o_s), mods, mla_w_o, "mla_out")
    n_pt = N_PROMPT // TOK_TILE
    ffn3 = functools.partial(_ffn, x, norm_ffn[3], mods, 3, *ffn_weights)
    y_prompt = ffn3(tile0=0, n_tiles=n_pt).reshape(BATCH, SEQ, D_MODEL)
    y_sample = ffn3(tile0=n_pt, n_tiles=N_TOK_TILES - n_pt).reshape(DEC_BATCH, DEC_SEQ, D_MODEL)
    return (y_prompt, y_sample, new_l0_k, new_l0_v, new_l1_fwd, new_l1_bwd, new_l2_k, new_l2_v,
            new_l3_ckv, new_l3_krope)
```
